```python
import jax, jax.numpy as jnp
from jax import lax
import numpy as np

D_MODEL = 1024
BATCH = 32
SEQ = 2048
DEPTH = 1

ATT_GROUPS = ((128, 1), (512, 4), (2048, 16))
ATT_HEADS_PER_GROUP = 4
ATT_HEAD_DIM = 64
ATT_HEADS = ATT_HEADS_PER_GROUP * len(ATT_GROUPS)
ROPE_THETA = 500000.0
ROPE_DIM = ATT_HEAD_DIM // 4
MLSTM_HEADS = 4
MLSTM_QK_DIM = 128
MLSTM_V_DIM = 256
MLSTM_CHUNK = 64
CONV_WIDTH = 4
N_EXPERTS = 256
TOP_K = 8
N_GROUPS = 8
TOPK_GROUPS = 4
EXPERT_FF = 256
SHARED_FF = 256
ROUTED_SCALE = 2.5
MOE_BLOCK = 128
NORM_EPS = 1e-6

ATT_W = ATT_HEADS * ATT_HEAD_DIM
ATT_OUT_W = ATT_HEADS_PER_GROUP * ATT_HEAD_DIM
MQK_W = MLSTM_HEADS * MLSTM_QK_DIM
MV_W = MLSTM_HEADS * MLSTM_V_DIM
IN_SPLITS = (ATT_W, ATT_W, ATT_W, MQK_W, MQK_W, MV_W, MV_W, MLSTM_HEADS, MLSTM_HEADS, D_MODEL, D_MODEL)
IN_WIDTH = int(sum(IN_SPLITS))
IN_OFFSETS = tuple(int(o) for o in np.cumsum(IN_SPLITS)[:-1])

kernel_name = "hybrid_dilated_attn_mlstm_moe_block"


def rms_norm(x, g):
    xf = x.astype(jnp.float32)
    y = xf * lax.rsqrt(jnp.mean(xf * xf, axis=-1, keepdims=True) + NORM_EPS)
    return (y * g.astype(jnp.float32)).astype(x.dtype)


def partial_rope(x, positions):
    half = ROPE_DIM // 2
    inv = jnp.power(ROPE_THETA, -jnp.arange(half, dtype=jnp.float32) / half)
    ang = positions.astype(jnp.float32)[..., None] * inv
    cos = jnp.cos(ang)[:, :, None, :]
    sin = jnp.sin(ang)[:, :, None, :]
    xf = x.astype(jnp.float32)
    x1, x2 = xf[..., :half], xf[..., half:ROPE_DIM]
    out = jnp.concatenate([x1 * cos - x2 * sin, x2 * cos + x1 * sin, xf[..., ROPE_DIM:]], axis=-1)
    return out.astype(x.dtype)


def dilated_window_attention(q, k, v, window, dilation):
    B, S, H, E = q.shape
    w_sub = window // dilation
    blk = w_sub
    L = S // dilation
    nb = -(-L // blk)
    Lp = nb * blk

    def to_blocks(t):
        t = t.astype(jnp.float32).reshape(B, L, dilation, H, E).transpose(0, 2, 1, 3, 4)
        t = jnp.pad(t, ((0, 0), (0, 0), (0, Lp - L), (0, 0), (0, 0)))
        return t.reshape(B, dilation, nb, blk, H, E)

    def with_prev(t):
        prev = jnp.pad(t[:, :, :-1], ((0, 0), (0, 0), (1, 0), (0, 0), (0, 0), (0, 0)))
        return jnp.concatenate([prev, t], axis=3)

    qb = to_blocks(q)
    kc = with_prev(to_blocks(k))
    vc = with_prev(to_blocks(v))
    s = jnp.einsum('brnqhe,brnkhe->brnhqk', qb, kc) * (E ** -0.5)
    qi = jnp.arange(blk)[:, None]
    ki = jnp.arange(2 * blk)[None, :]
    dist = blk + qi - ki
    n = jnp.arange(nb)[:, None, None]
    valid = (dist >= 0) & (dist <= w_sub) & ((n - 1) * blk + ki >= 0)
    s = jnp.where(valid[None, None, :, None], s, -jnp.inf)
    m = jnp.max(s, axis=-1, keepdims=True)
    p = jnp.exp(s - m)
    den = jnp.sum(p, axis=-1)
    den_t = den.transpose(0, 1, 2, 4, 3)
    o = jnp.einsum('brnhqk,brnkhe->brnqhe', p, vc) / den_t[..., None]
    lse = m[..., 0].transpose(0, 1, 2, 4, 3) + jnp.log(den_t)

    def from_blocks(t):
        t = t.reshape((B, dilation, Lp) + t.shape[4:])[:, :, :L]
        t = jnp.swapaxes(t, 1, 2)
        return t.reshape((B, S) + t.shape[3:])

    return from_blocks(o), from_blocks(lse)


def causal_short_conv(x, w, b):
    K = w.shape[0]
    S = x.shape[1]
    xp = jnp.pad(x, ((0, 0), (K - 1, 0), (0, 0)))
    y = b
    for j in range(K):
        y = y + xp[:, j:j + S] * w[j]
    return y


def mlstm_chunkwise(q, k, v, i_pre, f_pre):
    B, S, H, DK = q.shape
    DV = v.shape[-1]
    L = MLSTM_CHUNK
    nc = S // L
    logf = jax.nn.log_sigmoid(f_pre)

    def chunks4(t):
        return t.reshape(B, nc, L, H, t.shape[-1]).transpose(1, 0, 3, 2, 4)

    def chunks3(t):
        return t.reshape(B, nc, L, H).transpose(1, 0, 3, 2)

    causal = jnp.tril(jnp.ones((L, L), dtype=bool))

    def step(carry, xs):
        C, nvec, m = carry
        qc, kc, vc, ic, lfc = xs
        b = jnp.cumsum(lfc, axis=-1)
        dmat = jnp.where(causal, b[..., :, None] - b[..., None, :] + ic[..., None, :], -jnp.inf)
        m_t = jnp.maximum(b + m[..., None], jnp.max(dmat, axis=-1))
        wts = jnp.exp(dmat - m_t[..., None])
        sc = jnp.einsum('bhld,bhsd->bhls', qc, kc) * wts
        inter = jnp.exp(b + m[..., None] - m_t)
        num = jnp.einsum('bhls,bhsv->bhlv', sc, vc) + inter[..., None] * jnp.einsum('bhvd,bhld->bhlv', C, qc)
        den = jnp.sum(sc, axis=-1) + inter * jnp.einsum('bhd,bhld->bhl', nvec, qc)
        h = num / jnp.maximum(jnp.abs(den), jnp.exp(-m_t))[..., None]
        b_end = b[..., -1]
        g = b_end[..., None] - b + ic
        m_new = jnp.maximum(b_end + m, jnp.max(g, axis=-1))
        decay = jnp.exp(b_end + m - m_new)
        wk = jnp.exp(g - m_new[..., None])[..., None] * kc
        C_new = decay[..., None, None] * C + jnp.einsum('bhlv,bhld->bhvd', vc, wk)
        n_new = decay[..., None] * nvec + jnp.sum(wk, axis=-2)
        return (C_new, n_new, m_new), h

    init = (jnp.zeros((B, H, DV, DK), jnp.float32), jnp.zeros((B, H, DK), jnp.float32),
            jnp.zeros((B, H), jnp.float32))
    _, h = lax.scan(step, init, (chunks4(q), chunks4(k), chunks4(v), chunks3(i_pre), chunks3(logf)))
    return h.transpose(1, 0, 3, 2, 4).reshape(B, S, H, DV)


def mixer(h, positions, w_in, conv_w, conv_b, b_gates, g_mlstm, w_branch_a, w_branch_b, w_out):
    B, S, _ = h.shape
    proj = jnp.matmul(h, w_in)
    aq, ak, av, mq, mk, mv, mo, mi, mf, ga, gb = jnp.split(proj, IN_OFFSETS, axis=-1)
    aq = partial_rope(aq.reshape(B, S, ATT_HEADS, ATT_HEAD_DIM), positions)
    ak = partial_rope(ak.reshape(B, S, ATT_HEADS, ATT_HEAD_DIM), positions)
    av = av.reshape(B, S, ATT_HEADS, ATT_HEAD_DIM)
    outs, lses = [], []
    for g, (window, dil) in enumerate(ATT_GROUPS):
        sl = slice(g * ATT_HEADS_PER_GROUP, (g + 1) * ATT_HEADS_PER_GROUP)
        o, l = dilated_window_attention(aq[:, :, sl], ak[:, :, sl], av[:, :, sl], window, dil)
        outs.append(o)
        lses.append(l)
    alpha = jax.nn.softmax(jnp.stack(lses), axis=0)
    y_a = jnp.einsum('gbsh,gbshe->bshe', alpha, jnp.stack(outs)).reshape(B, S, ATT_OUT_W).astype(h.dtype)
    qk = jax.nn.silu(causal_short_conv(jnp.concatenate([mq, mk], axis=-1), conv_w, conv_b))
    mq, mk = jnp.split(qk, 2, axis=-1)
    q_m = mq.astype(jnp.float32).reshape(B, S, MLSTM_HEADS, MLSTM_QK_DIM)
    k_m = mk.astype(jnp.float32).reshape(B, S, MLSTM_HEADS, MLSTM_QK_DIM) * (MLSTM_QK_DIM ** -0.5)
    v_m = mv.astype(jnp.float32).reshape(B, S, MLSTM_HEADS, MLSTM_V_DIM)
    bg = b_gates.astype(jnp.float32)
    i_pre = mi.astype(jnp.float32) + bg[:MLSTM_HEADS]
    f_pre = mf.astype(jnp.float32) + bg[MLSTM_HEADS:]
    hb = mlstm_chunkwise(q_m, k_m, v_m, i_pre, f_pre)
    hb = rms_norm(hb, g_mlstm.reshape(MLSTM_HEADS, MLSTM_V_DIM)).reshape(B, S, MV_W)
    y_b = (hb * jax.nn.sigmoid(mo.astype(jnp.float32))).astype(h.dtype)
    merged = (jax.nn.sigmoid(ga) * jnp.matmul(y_a, w_branch_a)
              + jax.nn.sigmoid(gb) * jnp.matmul(y_b, w_branch_b))
    return jnp.matmul(merged, w_out)


def moe_ffn(h, router_w, router_bias, w_gate, w_up, w_down, ws_gate, ws_up, ws_down):
    B, S, D = h.shape
    T = B * S
    E, M = N_EXPERTS, MOE_BLOCK
    xt = h.reshape(T, D)
    scores = jax.nn.sigmoid(jnp.matmul(xt.astype(jnp.float32), router_w.astype(jnp.float32)))
    sel = scores + router_bias.astype(jnp.float32)
    group_score = jnp.sum(lax.top_k(sel.reshape(T, N_GROUPS, E // N_GROUPS), 2)[0], axis=-1)
    _, gidx = lax.top_k(group_score, TOPK_GROUPS)
    gmask = jnp.sum(jax.nn.one_hot(gidx, N_GROUPS, dtype=jnp.int32), axis=1) > 0
    sel = jnp.where(jnp.repeat(gmask, E // N_GROUPS, axis=-1), sel, -jnp.inf)
    _, idx = lax.top_k(sel, TOP_K)
    top_w = jnp.take_along_axis(scores, idx, axis=-1)
    top_w = top_w / jnp.sum(top_w, axis=-1, keepdims=True) * ROUTED_SCALE
    A = T * TOP_K
    NB = (A + E * (M - 1)) // M
    P = NB * M
    flat_e = idx.reshape(-1).astype(jnp.int32)
    flat_w = top_w.reshape(-1)
    order = jnp.argsort(flat_e)
    sorted_e = flat_e[order]
    counts = jnp.zeros((E,), jnp.int32).at[flat_e].add(1)
    padded = (counts + M - 1) // M * M
    pend = jnp.cumsum(padded)
    pstart = pend - padded
    ustart = jnp.cumsum(counts) - counts
    dest = pstart[sorted_e] + jnp.arange(A, dtype=jnp.int32) - ustart[sorted_e]
    buf_tok = jnp.zeros((P,), jnp.int32).at[dest].set((order // TOP_K).astype(jnp.int32))
    buf_w = jnp.zeros((P,), jnp.float32).at[dest].set(flat_w[order])
    blk_e = jnp.minimum(jnp.searchsorted(pend, jnp.arange(NB, dtype=jnp.int32) * M, side='right'),
                        E - 1).astype(jnp.int32)

    def expert_block(acc, blk):
        tok, wt, e = blk
        xb = xt[tok]
        hid = jax.nn.silu(jnp.matmul(xb, w_gate[e])) * jnp.matmul(xb, w_up[e])
        out = jnp.matmul(hid, w_down[e]).astype(jnp.float32) * wt[:, None]
        return acc.at[tok].add(out), None

    routed, _ = lax.scan(expert_block, jnp.zeros((T, D), jnp.float32),
                         (buf_tok.reshape(NB, M), buf_w.reshape(NB, M), blk_e))
    shared = jnp.matmul(jax.nn.silu(jnp.matmul(xt, ws_gate)) * jnp.matmul(xt, ws_up), ws_down)
    return (routed + shared.astype(jnp.float32)).astype(h.dtype).reshape(B, S, D)


def setup_inputs(seed: int = 0) -> dict:
    key = jax.random.key(seed)
    ks = jax.random.split(key, 28)
    D, Ly, H, E = D_MODEL, DEPTH, MLSTM_HEADS, N_EXPERTS

    def nrm(k, shape, fan):
        return jax.random.normal(k, shape, jnp.float32) * (fan ** -0.5)

    def gain(k, shape):
        return 1.0 + 0.05 * jax.random.normal(k, shape, jnp.float32)

    x = jax.random.normal(ks[0], (BATCH, SEQ, D), jnp.float32)
    c = jax.random.normal(ks[1], (BATCH, D), jnp.float32)
    positions = (jnp.arange(SEQ, dtype=jnp.int32)[None, :]
                 + jax.random.randint(ks[2], (BATCH, 1), 0, 4096, dtype=jnp.int32))
    b_gates = jnp.concatenate([0.1 * jax.random.normal(ks[12], (Ly, H), jnp.float32),
                               jax.random.uniform(ks[13], (Ly, H), jnp.float32, 3.0, 6.0)], axis=-1)
    return {
        "x": x,
        "c": c,
        "positions": positions,
        "w_ada": 0.5 * nrm(ks[3], (Ly, D, 6 * D), D),
        "b_ada": 0.02 * jax.random.normal(ks[4], (Ly, 6 * D), jnp.float32),
        "g_pre_mix": gain(ks[5], (Ly, D)),
        "g_post_mix": gain(ks[6], (Ly, D)),
        "g_pre_ffn": gain(ks[7], (Ly, D)),
        "g_post_ffn": gain(ks[8], (Ly, D)),
        "w_in": nrm(ks[9], (Ly, D, IN_WIDTH), D),
        "conv_w": nrm(ks[10], (Ly, CONV_WIDTH, 2 * MQK_W), CONV_WIDTH),
        "conv_b": 0.02 * jax.random.normal(ks[11], (Ly, 2 * MQK_W), jnp.float32),
        "b_gates": b_gates,
        "g_mlstm": gain(ks[14], (Ly, MV_W)),
        "w_branch_a": nrm(ks[15], (Ly, ATT_OUT_W, D), ATT_OUT_W),
        "w_branch_b": nrm(ks[16], (Ly, MV_W, D), MV_W),
        "w_out": nrm(ks[17], (Ly, D, D), D),
        "router_w": nrm(ks[18], (Ly, D, E), D),
        "router_bias": 0.01 * jax.random.normal(ks[19], (Ly, E), jnp.float32),
        "w_exp_gate": nrm(ks[20], (Ly, E, D, EXPERT_FF), D),
        "w_exp_up": nrm(ks[21], (Ly, E, D, EXPERT_FF), D),
        "w_exp_down": nrm(ks[22], (Ly, E, EXPERT_FF, D), EXPERT_FF),
        "w_sh_gate": nrm(ks[23], (Ly, D, SHARED_FF), D),
        "w_sh_up": nrm(ks[24], (Ly, D, SHARED_FF), D),
        "w_sh_down": nrm(ks[25], (Ly, SHARED_FF, D), SHARED_FF),
    }


def reference(x, c, positions, w_ada, b_ada, g_pre_mix, g_post_mix, g_pre_ffn, g_post_ffn,
              w_in, conv_w, conv_b, b_gates, g_mlstm, w_branch_a, w_branch_b, w_out,
              router_w, router_bias, w_exp_gate, w_exp_up, w_exp_down,
              w_sh_gate, w_sh_up, w_sh_down):
    for layer in range(DEPTH):
        mod = jnp.matmul(jax.nn.silu(c), w_ada[layer]) + b_ada[layer]
        sh1, sc1, gt1, sh2, sc2, gt2 = [m[:, None, :] for m in jnp.split(mod, 6, axis=-1)]
        h = rms_norm(x, g_pre_mix[layer]) * (1.0 + sc1) + sh1
        y = mixer(h, positions, w_in[layer], conv_w[layer], conv_b[layer], b_gates[layer],
                  g_mlstm[layer], w_branch_a[layer], w_branch_b[layer], w_out[layer])
        x = x + gt1 * rms_norm(y, g_post_mix[layer])
        h = rms_norm(x, g_pre_ffn[layer]) * (1.0 + sc2) + sh2
        y = moe_ffn(h, router_w[layer], router_bias[layer], w_exp_gate[layer], w_exp_up[layer],
                    w_exp_down[layer], w_sh_gate[layer], w_sh_up[layer], w_sh_down[layer])
        x = x + gt2 * rms_norm(y, g_post_ffn[layer])
    return x
```

```python
import functools

import jax
import jax.numpy as jnp
from jax import lax
from jax.experimental import pallas as pl
from jax.experimental.pallas import tpu as pltpu

F32 = jnp.float32
BF16 = jnp.bfloat16
HIGHEST = lax.Precision.HIGHEST

D_MODEL = 1024
ATT_GROUPS = ((128, 1), (512, 4), (2048, 16))
ATT_HEAD_DIM = 64
ATT_GROUP_W = 256
ATT_BLK = 128
ROPE_THETA = 500000.0
ROPE_HALF = 8
MLSTM_HEADS = 4
MLSTM_QK_DIM = 128
MLSTM_V_DIM = 256
MLSTM_CHUNK = 64
CONV_WIDTH = 4
N_EXPERTS = 256
TOP_K = 8
N_GROUPS = 8
TOPK_GROUPS = 4
EXPERT_FF = 256
ROUTED_SCALE = 2.5
NORM_EPS = 1e-6
NEG = -1e30

OFF_MV, OFF_MO, OFF_GA, OFF_GB = 0, 1024, 2048, 3072
OFF_MQ, OFF_MK = 4096, 4608
OFF_AQ, OFF_AK, OFF_AV = 5120, 5888, 6656
PROJ_W = 7424
HALF = D_MODEL // 2

EXPERT_BLOCK = 256
VMEM_LIMIT = 56 * 1024 * 1024


def _nt(a, b, precision=None):
    return lax.dot_general(a, b, (((1,), (1,)), ((), ())), preferred_element_type=F32,
                           precision=precision)


def _tn(a, b):
    return lax.dot_general(a, b, (((0,), (0,)), ((), ())), preferred_element_type=F32)


def _silu(x):
    return x * jax.nn.sigmoid(x)


def _pack_pair(lo, hi):
    lo_b = pltpu.bitcast(lo.astype(BF16).astype(F32), jnp.uint32)
    hi_b = pltpu.bitcast(hi.astype(BF16).astype(F32), jnp.uint32)
    return (lo_b >> 16) | (hi_b & jnp.uint32(0xFFFF0000))


def _unpack_pair(w):
    lo = pltpu.bitcast(w << 16, F32)
    hi = pltpu.bitcast(w & jnp.uint32(0xFFFF0000), F32)
    return lo, hi


def _mod_kernel(c_ref, w_ref, b_ref, o_ref):
    a = _silu(c_ref[...])
    o_ref[...] = jnp.dot(a, w_ref[...], preferred_element_type=F32, precision=HIGHEST) + b_ref[...]


def _adaln(c, w_ada, b_ada):
    B = c.shape[0]
    n = w_ada.shape[1]
    tn = 512
    return pl.pallas_call(
        _mod_kernel,
        grid=(n // tn,),
        in_specs=[pl.BlockSpec((B, D_MODEL), lambda j: (0, 0)),
                  pl.BlockSpec((D_MODEL, tn), lambda j: (0, j)),
                  pl.BlockSpec((1, tn), lambda j: (0, j))],
        out_specs=pl.BlockSpec((B, tn), lambda j: (0, j)),
        out_shape=jax.ShapeDtypeStruct((B, n), F32),
        name="adaln_mod",
    )(c, w_ada, b_ada.reshape(1, n))


def _proj_kernel(x_ref, mod_ref, g_ref, w_ref, wif_ref, o_ref, gates_ref, h_ref):
    @pl.when(pl.program_id(1) == 0)
    def _():
        x = x_ref[...]
        ms = jnp.mean(x * x, axis=-1, keepdims=True)
        y = x * lax.rsqrt(ms + NORM_EPS) * g_ref[...]
        h = (y * (1.0 + mod_ref[0, 1:2, :]) + mod_ref[0, 0:1, :]).astype(BF16)
        h_ref[...] = h
        gates_ref[...] = jnp.dot(h, wif_ref[...], preferred_element_type=F32)

    o_ref[...] = jnp.dot(h_ref[...], w_ref[...], preferred_element_type=F32).astype(BF16)


def _in_proj(x2, mod3, g_pre, w_main, w_if, seq):
    T = x2.shape[0]
    tm, tn = 1024, 256
    per_b = seq // tm
    return pl.pallas_call(
        _proj_kernel,
        grid=(T // tm, PROJ_W // tn),
        in_specs=[pl.BlockSpec((tm, D_MODEL), lambda i, j: (i, 0)),
                  pl.BlockSpec((1, 6, D_MODEL), lambda i, j: (i // per_b, 0, 0)),
                  pl.BlockSpec((1, D_MODEL), lambda i, j: (0, 0)),
                  pl.BlockSpec((D_MODEL, tn), lambda i, j: (0, j)),
                  pl.BlockSpec((D_MODEL, 128), lambda i, j: (0, 0))],
        out_specs=[pl.BlockSpec((tm, tn), lambda i, j: (i, j)),
                   pl.BlockSpec((tm, 128), lambda i, j: (i, 0))],
        out_shape=[jax.ShapeDtypeStruct((T, PROJ_W), BF16),
                   jax.ShapeDtypeStruct((T, 128), F32)],
        scratch_shapes=[pltpu.VMEM((tm, D_MODEL), BF16)],
        compiler_params=pltpu.CompilerParams(
            dimension_semantics=("arbitrary", "arbitrary"), vmem_limit_bytes=VMEM_LIMIT),
        name="norm_in_proj",
    )(x2, mod3, g_pre, w_main, w_if)


def _attn_kernel(q_ref, k_ref, v_ref, cs_ref, sn_ref, o_ref, qf, kf, vf, acc, m_s, l_s, *, seq):
    g = pl.program_id(1)
    lane = lax.broadcasted_iota(jnp.int32, (ATT_BLK, 128), 1)
    first = (lane % ATT_HEAD_DIM) < ROPE_HALF
    low_head = lane < ATT_HEAD_DIM

    def rope(x, cs, sn):
        partner = jnp.where(first, pltpu.roll(x, 128 - ROPE_HALF, 1), pltpu.roll(x, ROPE_HALF, 1))
        return x * cs + partner * sn

    def zero_pad(i, _):
        rows = pl.ds(pl.multiple_of(i * ATT_BLK, ATT_BLK), ATT_BLK)
        for hp in range(2):
            kf[hp, rows, :] = jnp.zeros((ATT_BLK, 128), F32)
            vf[hp, rows, :] = jnp.zeros((ATT_BLK, 128), F32)
        return 0

    lax.fori_loop(0, seq // ATT_BLK, zero_pad, 0)

    def stage(i, _):
        r = pl.multiple_of(i * ATT_BLK, ATT_BLK)
        rows = pl.ds(r, ATT_BLK)
        prow = pl.ds(pl.multiple_of(seq + i * ATT_BLK, ATT_BLK), ATT_BLK)
        cs = cs_ref[0, rows, :]
        sn = sn_ref[0, rows, :]
        for hp in range(2):
            cols = pl.ds(hp * 128, 128)
            qf[hp, rows, :] = rope(q_ref[0, rows, cols].astype(F32), cs, sn) * (ATT_HEAD_DIM ** -0.5)
            kf[hp, prow, :] = rope(k_ref[0, rows, cols].astype(F32), cs, sn)
            vf[hp, prow, :] = v_ref[0, rows, cols].astype(F32)
        return 0

    lax.fori_loop(0, seq // ATT_BLK, stage, 0)

    qi = lax.broadcasted_iota(jnp.int32, (ATT_BLK, 2 * ATT_BLK), 0)
    ki = lax.broadcasted_iota(jnp.int32, (ATT_BLK, 2 * ATT_BLK), 1)
    band = (ki >= qi) & (ki <= qi + ATT_BLK)

    def process(d, init):
        span = ATT_BLK * d

        def body(c, _):
            rho = c % d
            n = c // d
            qstart = rho + n * span
            kstart = seq + qstart - span
            first_key = jnp.where(n > 0, 0, ATT_BLK)
            valid = band & (ki >= first_key)
            qrows = pl.ds(qstart, ATT_BLK, stride=d) if d > 1 else pl.ds(qstart, ATT_BLK)
            krows = pl.ds(kstart, 2 * ATT_BLK, stride=d) if d > 1 else pl.ds(kstart, 2 * ATT_BLK)
            for hp in range(2):
                q2 = qf[hp, qrows, :]
                k2 = kf[hp, krows, :].astype(BF16)
                v2 = vf[hp, krows, :].astype(BF16)
                res = []
                for hh in range(2):
                    hm = low_head if hh == 0 else jnp.logical_not(low_head)
                    qh = jnp.where(hm, q2, 0.0).astype(BF16)
                    s = jnp.where(valid, _nt(qh, k2), NEG)
                    m = jnp.max(s, axis=1, keepdims=True)
                    p = jnp.exp(s - m)
                    l = jnp.sum(p, axis=1, keepdims=True)
                    o = jnp.dot(p.astype(BF16), v2, preferred_element_type=F32)
                    res.append((o, m, l))
                o_b = jnp.where(low_head, res[0][0], res[1][0])
                m_b = jnp.where(low_head, res[0][1], res[1][1])
                l_b = jnp.where(low_head, res[0][2], res[1][2])
                if init:
                    acc[hp, qrows, :] = o_b
                    m_s[hp, qrows, :] = m_b
                    l_s[hp, qrows, :] = l_b
                else:
                    m_old = m_s[hp, qrows, :]
                    m_new = jnp.maximum(m_old, m_b)
                    a_old = jnp.exp(m_old - m_new)
                    a_new = jnp.exp(m_b - m_new)
                    acc[hp, qrows, :] = acc[hp, qrows, :] * a_old + o_b * a_new
                    l_s[hp, qrows, :] = l_s[hp, qrows, :] * a_old + l_b * a_new
                    m_s[hp, qrows, :] = m_new
            return 0

        lax.fori_loop(0, seq // ATT_BLK, body, 0)

    for gi, (_, d) in enumerate(ATT_GROUPS):
        @pl.when(g == gi)
        def _(d=d, gi=gi):
            process(d, gi == 0)

    @pl.when(g == len(ATT_GROUPS) - 1)
    def _():
        def fin(i, _):
            rows = pl.ds(pl.multiple_of(i * ATT_BLK, ATT_BLK), ATT_BLK)
            for hp in range(2):
                o_ref[0, rows, pl.ds(hp * 128, 128)] = (acc[hp, rows, :] / l_s[hp, rows, :]).astype(BF16)
            return 0

        lax.fori_loop(0, seq // ATT_BLK, fin, 0)


def _attention(proj3, cs, sn):
    B, S, _ = proj3.shape
    ng = len(ATT_GROUPS)
    qb, kb, vb = OFF_AQ // ATT_GROUP_W, OFF_AK // ATT_GROUP_W, OFF_AV // ATT_GROUP_W
    return pl.pallas_call(
        functools.partial(_attn_kernel, seq=S),
        grid=(B, ng),
        in_specs=[pl.BlockSpec((1, S, ATT_GROUP_W), lambda b, g: (b, 0, qb + g)),
                  pl.BlockSpec((1, S, ATT_GROUP_W), lambda b, g: (b, 0, kb + g)),
                  pl.BlockSpec((1, S, ATT_GROUP_W), lambda b, g: (b, 0, vb + g)),
                  pl.BlockSpec((1, S, 128), lambda b, g: (b, 0, 0)),
                  pl.BlockSpec((1, S, 128), lambda b, g: (b, 0, 0))],
        out_specs=pl.BlockSpec((1, S, ATT_GROUP_W), lambda b, g: (b, 0, 0)),
        out_shape=jax.ShapeDtypeStruct((B, S, ATT_GROUP_W), BF16),
        scratch_shapes=[pltpu.VMEM((2, S, 128), F32),
                        pltpu.VMEM((2, 2 * S, 128), F32),
                        pltpu.VMEM((2, 2 * S, 128), F32),
                        pltpu.VMEM((2, S, 128), F32),
                        pltpu.VMEM((2, S, 128), F32),
                        pltpu.VMEM((2, S, 128), F32)],
        compiler_params=pltpu.CompilerParams(
            dimension_semantics=("arbitrary", "arbitrary"), vmem_limit_bytes=VMEM_LIMIT),
        name="dilated_attention",
    )(proj3, proj3, proj3, cs, sn)


def _log_sigmoid(x):
    return jnp.minimum(x, 0.0) - jnp.log(1.0 + jnp.exp(-jnp.abs(x)))


def _mlstm_kernel(mq_ref, mk_ref, mv_ref, mo_ref, gates_ref, cwq_ref, cwk_ref, cbq_ref, cbk_ref,
                  bg_ref, gm_ref, o_ref, pad_ref, q_s, k_s, c_ref, *, seq):
    h = pl.program_id(1)
    L = MLSTM_CHUNK
    DK = MLSTM_QK_DIM
    halo = 8

    pad_ref[0:halo, :] = jnp.zeros((halo, 2 * DK), F32)
    blk = 128
    for i in range(seq // blk):
        pad_ref[halo + i * blk:halo + (i + 1) * blk, 0:DK] = mq_ref[0, i * blk:(i + 1) * blk, :].astype(F32)
        pad_ref[halo + i * blk:halo + (i + 1) * blk, DK:2 * DK] = mk_ref[0, i * blk:(i + 1) * blk, :].astype(F32)
    for i in range(seq // blk):
        yq = jnp.broadcast_to(cbq_ref[...], (blk, DK))
        yk = jnp.broadcast_to(cbk_ref[...], (blk, DK))
        for j in range(CONV_WIDTH):
            r0 = halo + i * blk - (CONV_WIDTH - 1) + j
            yq = yq + pad_ref[r0:r0 + blk, 0:DK] * cwq_ref[j:j + 1, :]
            yk = yk + pad_ref[r0:r0 + blk, DK:2 * DK] * cwk_ref[j:j + 1, :]
        q_s[i * blk:(i + 1) * blk, :] = _silu(yq).astype(BF16)
        k_s[i * blk:(i + 1) * blk, :] = (_silu(yk) * (DK ** -0.5)).astype(BF16)

    lane = lax.broadcasted_iota(jnp.int32, (1, 128), 1)
    sel_i = (lane == h).astype(F32)
    sel_f = (lane == h + MLSTM_HEADS).astype(F32)
    r8 = lax.broadcasted_iota(jnp.int32, (8, 128), 0)
    l8 = lax.broadcasted_iota(jnp.int32, (8, 128), 1)
    selmat = jnp.where(((r8 == 0) & (l8 == h)) | ((r8 == 1) & (l8 == h + MLSTM_HEADS)), 1.0, 0.0)
    ri = lax.broadcasted_iota(jnp.int32, (L, L), 0)
    ci = lax.broadcasted_iota(jnp.int32, (L, L), 1)
    causal = ci <= ri
    tri = causal.astype(F32)
    tri_t = (ri <= ci).astype(F32)
    bias_row = bg_ref[...]
    g_row = gm_ref[...]

    c_ref[...] = jnp.zeros((DK, MLSTM_V_DIM), F32)

    def chunk(c, carry):
        n_row, m_prev = carry
        rows = pl.ds(pl.multiple_of(c * L, L), L)
        q = q_s[rows, :]
        k = k_s[rows, :]
        v = mv_ref[0, rows, :]
        gch = gates_ref[0, rows, :] + bias_row
        i_col = jnp.sum(gch * sel_i, axis=1, keepdims=True)
        f_col = jnp.sum(gch * sel_f, axis=1, keepdims=True)
        rows_b = _nt(selmat, gch, precision=HIGHEST)
        i_row = rows_b[0:1, :]
        lf_row = _log_sigmoid(rows_b[1:2, :])
        lf_col = _log_sigmoid(f_col)
        b_col_b = jnp.dot(tri, jnp.broadcast_to(lf_col, (L, L)), preferred_element_type=F32,
                          precision=HIGHEST)
        b_row_b = jnp.dot(jnp.broadcast_to(lf_row, (L, L)), tri_t, preferred_element_type=F32,
                          precision=HIGHEST)
        b_col = b_col_b[:, 0:1]
        dmat = jnp.where(causal, b_col_b - b_row_b + i_row, NEG)
        m_t = jnp.maximum(b_col + m_prev, jnp.max(dmat, axis=1, keepdims=True))
        wts = jnp.exp(dmat - m_t)
        sc = _nt(q, k) * wts
        inter = jnp.exp(b_col + m_prev - m_t)
        c_old = c_ref[...]
        num = (jnp.dot(sc.astype(BF16), v, preferred_element_type=F32)
               + inter * jnp.dot(q, c_old.astype(BF16), preferred_element_type=F32))
        den = (jnp.sum(sc, axis=1, keepdims=True)
               + inter * jnp.sum(q.astype(F32) * n_row, axis=1, keepdims=True))
        hh = num / jnp.maximum(jnp.abs(den), jnp.exp(-m_t))
        ms = jnp.mean(hh * hh, axis=1, keepdims=True)
        hn = hh * lax.rsqrt(ms + NORM_EPS) * g_row
        o_ref[0, rows, :] = (hn * jax.nn.sigmoid(mo_ref[0, rows, :].astype(F32))).astype(BF16)

        b_end = b_col_b[L - 1:L, 0:1]
        g_col = b_end - b_col + i_col
        m_new = jnp.maximum(b_end + m_prev, jnp.max(g_col, axis=0, keepdims=True))
        decay = jnp.exp(b_end + m_prev - m_new)
        wk = jnp.exp(g_col - m_new) * k.astype(F32)
        c_ref[...] = decay * c_old + _tn(wk.astype(BF16), v)
        n_new = decay * n_row + jnp.sum(wk, axis=0, keepdims=True)
        return n_new, m_new

    lax.fori_loop(0, seq // L, chunk, (jnp.zeros((1, DK), F32), jnp.zeros((1, 1), F32)))


def _mlstm(proj3, gates3, conv_w, conv_b, bg_row, g_mlstm):
    B, S, _ = proj3.shape
    H, DK, DV = MLSTM_HEADS, MLSTM_QK_DIM, MLSTM_V_DIM
    qb, kb = OFF_MQ // DK, OFF_MK // DK
    vb, ob = OFF_MV // DV, OFF_MO // DV
    nq = (H * DK) // DK
    return pl.pallas_call(
        functools.partial(_mlstm_kernel, seq=S),
        grid=(B, H),
        in_specs=[pl.BlockSpec((1, S, DK), lambda b, h: (b, 0, qb + h)),
                  pl.BlockSpec((1, S, DK), lambda b, h: (b, 0, kb + h)),
                  pl.BlockSpec((1, S, DV), lambda b, h: (b, 0, vb + h)),
                  pl.BlockSpec((1, S, DV), lambda b, h: (b, 0, ob + h)),
                  pl.BlockSpec((1, S, 128), lambda b, h: (b, 0, 0)),
                  pl.BlockSpec((CONV_WIDTH, DK), lambda b, h: (0, h)),
                  pl.BlockSpec((CONV_WIDTH, DK), lambda b, h: (0, nq + h)),
                  pl.BlockSpec((1, DK), lambda b, h: (0, h)),
                  pl.BlockSpec((1, DK), lambda b, h: (0, nq + h)),
                  pl.BlockSpec((1, 128), lambda b, h: (0, 0)),
                  pl.BlockSpec((1, DV), lambda b, h: (0, h))],
        out_specs=pl.BlockSpec((1, S, DV), lambda b, h: (b, 0, h)),
        out_shape=jax.ShapeDtypeStruct((B, S, H * DV), BF16),
        scratch_shapes=[pltpu.VMEM((S + 8, 2 * DK), F32),
                        pltpu.VMEM((S, DK), BF16),
                        pltpu.VMEM((S, DK), BF16),
                        pltpu.VMEM((DK, DV), F32)],
        compiler_params=pltpu.CompilerParams(
            dimension_semantics=("arbitrary", "arbitrary"), vmem_limit_bytes=VMEM_LIMIT),
        name="mlstm_chunkwise",
    )(proj3, proj3, proj3, proj3, gates3, conv_w, conv_w, conv_b, conv_b, bg_row, g_mlstm)


def _rms(y, g):
    ms = jnp.mean(y * y, axis=-1, keepdims=True)
    return y * lax.rsqrt(ms + NORM_EPS) * g


def _merge_kernel(ya_ref, yb_ref, ga_ref, gb_ref, x_ref, mod_ref, wa_ref, wb_ref, wo_ref,
                  gpost_ref, gpre_ref, x1_ref, h2_ref):
    pa = jnp.dot(ya_ref[...], wa_ref[...], preferred_element_type=F32)
    pb = jnp.dot(yb_ref[...], wb_ref[...], preferred_element_type=F32)
    merged = (jax.nn.sigmoid(ga_ref[...].astype(F32)) * pa
              + jax.nn.sigmoid(gb_ref[...].astype(F32)) * pb)
    y = jnp.dot(merged.astype(BF16), wo_ref[...], preferred_element_type=F32)
    x1 = x_ref[...] + mod_ref[0, 2:3, :] * _rms(y, gpost_ref[...])
    x1_ref[...] = x1
    h2 = _rms(x1, gpre_ref[...]) * (1.0 + mod_ref[0, 4:5, :]) + mod_ref[0, 3:4, :]
    h2_ref[...] = _pack_pair(h2[:, :HALF], h2[:, HALF:])


def _merge(ya2, yb2, proj2, x2, mod3, wa, wb, wo, g_post, g_pre, seq):
    T = x2.shape[0]
    tm = 512
    per_b = seq // tm
    full = lambda shape: pl.BlockSpec(shape, lambda i: (0,) * len(shape))
    return pl.pallas_call(
        _merge_kernel,
        grid=(T // tm,),
        in_specs=[pl.BlockSpec((tm, ATT_GROUP_W), lambda i: (i, 0)),
                  pl.BlockSpec((tm, D_MODEL), lambda i: (i, 0)),
                  pl.BlockSpec((tm, D_MODEL), lambda i: (i, OFF_GA // D_MODEL)),
                  pl.BlockSpec((tm, D_MODEL), lambda i: (i, OFF_GB // D_MODEL)),
                  pl.BlockSpec((tm, D_MODEL), lambda i: (i, 0)),
                  pl.BlockSpec((1, 6, D_MODEL), lambda i: (i // per_b, 0, 0)),
                  full((ATT_GROUP_W, D_MODEL)), full((D_MODEL, D_MODEL)), full((D_MODEL, D_MODEL)),
                  full((1, D_MODEL)), full((1, D_MODEL))],
        out_specs=[pl.BlockSpec((tm, D_MODEL), lambda i: (i, 0)),
                   pl.BlockSpec((tm, HALF), lambda i: (i, 0))],
        out_shape=[jax.ShapeDtypeStruct((T, D_MODEL), F32),
                   jax.ShapeDtypeStruct((T, HALF), jnp.uint32)],
        compiler_params=pltpu.CompilerParams(
            dimension_semantics=("arbitrary",), vmem_limit_bytes=VMEM_LIMIT),
        name="merge_out_proj",
    )(ya2, yb2, proj2, proj2, x2, mod3, wa, wb, wo, g_post, g_pre)


def _router_kernel(h2_ref, rlo_ref, rhi_ref, bias_ref, idx_ref, w_ref, rank_ref, cnt_ref):
    E = N_EXPERTS
    tr = h2_ref.shape[0]
    gsz = E // N_GROUPS

    @pl.when(pl.program_id(0) == 0)
    def _():
        cnt_ref[...] = jnp.zeros(cnt_ref.shape, F32)

    lo, hi = _unpack_pair(h2_ref[...])
    logits = _nt(rlo_ref[...], lo.astype(BF16)) + _nt(rhi_ref[...], hi.astype(BF16))
    scores = jax.nn.sigmoid(logits)
    sel = scores + bias_ref[:, 0:1]

    gi = lax.broadcasted_iota(jnp.int32, (gsz, tr), 0).astype(F32)
    gs_rows = []
    for g in range(N_GROUPS):
        blk = sel[g * gsz:(g + 1) * gsz, :]
        m1 = jnp.max(blk, axis=0, keepdims=True)
        a1 = jnp.min(jnp.where(blk == m1, gi, float(E)), axis=0, keepdims=True)
        m2 = jnp.max(jnp.where(gi == a1, -jnp.inf, blk), axis=0, keepdims=True)
        gs_rows.append(m1 + m2)
    gs = jnp.concatenate(gs_rows, axis=0)
    g8 = lax.broadcasted_iota(jnp.int32, (N_GROUPS, tr), 0).astype(F32)
    gmask = jnp.zeros((N_GROUPS, tr), F32)
    for _ in range(TOPK_GROUPS):
        m = jnp.max(gs, axis=0, keepdims=True)
        a = jnp.min(jnp.where(gs == m, g8, float(E)), axis=0, keepdims=True)
        hit = g8 == a
        gmask = jnp.where(hit, 1.0, gmask)
        gs = jnp.where(hit, -jnp.inf, gs)
    selm = jnp.concatenate(
        [jnp.where(gmask[g:g + 1, :] > 0.0, sel[g * gsz:(g + 1) * gsz, :], -jnp.inf)
         for g in range(N_GROUPS)], axis=0)

    ei = lax.broadcasted_iota(jnp.int32, (E, tr), 0).astype(F32)
    picks, weights = [], []
    chosen = jnp.zeros((E, tr), F32)
    for _ in range(TOP_K):
        m = jnp.max(selm, axis=0, keepdims=True)
        a = jnp.min(jnp.where(selm == m, ei, float(E)), axis=0, keepdims=True)
        hit = ei == a
        picks.append(a)
        weights.append(jnp.sum(jnp.where(hit, scores, 0.0), axis=0, keepdims=True))
        chosen = jnp.where(hit, 1.0, chosen)
        selm = jnp.where(hit, -jnp.inf, selm)
    wsum = weights[0]
    for w in weights[1:]:
        wsum = wsum + w

    ti = lax.broadcasted_iota(jnp.int32, (tr, tr), 0)
    tj = lax.broadcasted_iota(jnp.int32, (tr, tr), 1)
    before = (ti < tj).astype(BF16)
    pos = jnp.dot(chosen.astype(BF16), before, preferred_element_type=F32) + cnt_ref[:, 0:1]
    ranks = [jnp.sum(jnp.where(ei == a, pos, 0.0), axis=0, keepdims=True) for a in picks]
    cnt_ref[...] = cnt_ref[...] + jnp.sum(chosen, axis=1, keepdims=True)

    idx_ref[...] = jnp.concatenate(picks, axis=0).astype(jnp.int32)
    w_ref[...] = jnp.concatenate([w / wsum * ROUTED_SCALE for w in weights], axis=0)
    rank_ref[...] = jnp.concatenate(ranks, axis=0).astype(jnp.int32)


def _router(h2p, r_lo, r_hi, bias_col):
    T = h2p.shape[0]
    tr = 512
    full = lambda shape: pl.BlockSpec(shape, lambda i: (0,) * len(shape))
    return pl.pallas_call(
        _router_kernel,
        grid=(T // tr,),
        in_specs=[pl.BlockSpec((tr, HALF), lambda i: (i, 0)),
                  full((N_EXPERTS, HALF)), full((N_EXPERTS, HALF)), full((N_EXPERTS, 128))],
        out_specs=[pl.BlockSpec((TOP_K, tr), lambda i: (0, i)),
                   pl.BlockSpec((TOP_K, tr), lambda i: (0, i)),
                   pl.BlockSpec((TOP_K, tr), lambda i: (0, i)),
                   full((N_EXPERTS, 128))],
        out_shape=[jax.ShapeDtypeStruct((TOP_K, T), jnp.int32),
                   jax.ShapeDtypeStruct((TOP_K, T), F32),
                   jax.ShapeDtypeStruct((TOP_K, T), jnp.int32),
                   jax.ShapeDtypeStruct((N_EXPERTS, 128), F32)],
        compiler_params=pltpu.CompilerParams(
            dimension_semantics=("arbitrary",), vmem_limit_bytes=VMEM_LIMIT),
        name="router_topk",
    )(h2p, r_lo, r_hi, bias_col)


def _ffn_kernel(blk_e_ref, nused_ref, x_ref, wg_ref, wu_ref, wd_ref, y_ref, wg_s, wu_s, wd_s):
    i = pl.program_id(0)

    @pl.when(i < nused_ref[0])
    def _():
        e = blk_e_ref[i]
        prev = blk_e_ref[jnp.maximum(i - 1, 0)]

        @pl.when((i == 0) | (e != prev))
        def _():
            wg_s[...] = wg_ref[0].astype(BF16)
            wu_s[...] = wu_ref[0].astype(BF16)
            wd_s[...] = wd_ref[0].astype(BF16)

        lo, hi = _unpack_pair(x_ref[...])
        lo = lo.astype(BF16)
        hi = hi.astype(BF16)
        gate = (jnp.dot(lo, wg_s[0:HALF, :], preferred_element_type=F32)
                + jnp.dot(hi, wg_s[HALF:, :], preferred_element_type=F32))
        up = (jnp.dot(lo, wu_s[0:HALF, :], preferred_element_type=F32)
              + jnp.dot(hi, wu_s[HALF:, :], preferred_element_type=F32))
        hid = (_silu(gate) * up).astype(BF16)
        out = jnp.dot(hid, wd_s[...], preferred_element_type=F32)
        y_ref[...] = _pack_pair(out[:, :HALF], out[:, HALF:])


def _expert_ffn(blk_e, nused, xs, w_gate, w_up, w_down):
    P = xs.shape[0]
    bm = EXPERT_BLOCK
    nb = P // bm

    def row_map(i, blk_e_ref, nused_ref):
        return (jnp.minimum(i, nused_ref[0] - 1), 0)

    def w_map(i, blk_e_ref, nused_ref):
        return (blk_e_ref[jnp.minimum(i, nused_ref[0] - 1)], 0, 0)

    grid_spec = pltpu.PrefetchScalarGridSpec(
        num_scalar_prefetch=2,
        grid=(nb,),
        in_specs=[pl.BlockSpec((bm, HALF), row_map),
                  pl.BlockSpec((1, D_MODEL, EXPERT_FF), w_map),
                  pl.BlockSpec((1, D_MODEL, EXPERT_FF), w_map),
                  pl.BlockSpec((1, EXPERT_FF, D_MODEL), w_map)],
        out_specs=pl.BlockSpec((bm, HALF), row_map),
        scratch_shapes=[pltpu.VMEM((D_MODEL, EXPERT_FF), BF16),
                        pltpu.VMEM((D_MODEL, EXPERT_FF), BF16),
                        pltpu.VMEM((EXPERT_FF, D_MODEL), BF16)],
    )
    return pl.pallas_call(
        _ffn_kernel,
        grid_spec=grid_spec,
        out_shape=jax.ShapeDtypeStruct((P, HALF), jnp.uint32),
        compiler_params=pltpu.CompilerParams(
            dimension_semantics=("arbitrary",), vmem_limit_bytes=VMEM_LIMIT),
        name="routed_experts",
    )(blk_e, nused, xs, w_gate, w_up, w_down)


def _final_kernel(yg_ref, w_ref, h2_ref, x1_ref, mod_ref, wsg_ref, wsu_ref, wsd_ref, gpost_ref, o_ref):
    lo, hi = _unpack_pair(h2_ref[...])
    lo = lo.astype(BF16)
    hi = hi.astype(BF16)
    gate = (jnp.dot(lo, wsg_ref[0:HALF, :], preferred_element_type=F32)
            + jnp.dot(hi, wsg_ref[HALF:, :], preferred_element_type=F32))
    up = (jnp.dot(lo, wsu_ref[0:HALF, :], preferred_element_type=F32)
          + jnp.dot(hi, wsu_ref[HALF:, :], preferred_element_type=F32))
    shared = jnp.dot((_silu(gate) * up).astype(BF16), wsd_ref[...], preferred_element_type=F32)
    y_lo = shared[:, :HALF]
    y_hi = shared[:, HALF:]
    for k in range(TOP_K):
        r_lo, r_hi = _unpack_pair(yg_ref[k])
        wk = w_ref[:, k:k + 1]
        y_lo = y_lo + wk * r_lo
        y_hi = y_hi + wk * r_hi
    ms = (jnp.sum(y_lo * y_lo, axis=-1, keepdims=True)
          + jnp.sum(y_hi * y_hi, axis=-1, keepdims=True)) * (1.0 / D_MODEL)
    inv = lax.rsqrt(ms + NORM_EPS)
    o_ref[:, 0:HALF] = x1_ref[:, 0:HALF] + mod_ref[0, 5:6, 0:HALF] * (y_lo * inv * gpost_ref[:, 0:HALF])
    o_ref[:, HALF:] = x1_ref[:, HALF:] + mod_ref[0, 5:6, HALF:] * (y_hi * inv * gpost_ref[:, HALF:])


def _final(yg, w_tk, h2p, x1, mod3, wsg, wsu, wsd, g_post, seq):
    T = x1.shape[0]
    tm = 256
    per_b = seq // tm
    full = lambda shape: pl.BlockSpec(shape, lambda i: (0,) * len(shape))
    return pl.pallas_call(
        _final_kernel,
        grid=(T // tm,),
        in_specs=[pl.BlockSpec((TOP_K, tm, HALF), lambda i: (0, i, 0)),
                  pl.BlockSpec((tm, TOP_K), lambda i: (i, 0)),
                  pl.BlockSpec((tm, HALF), lambda i: (i, 0)),
                  pl.BlockSpec((tm, D_MODEL), lambda i: (i, 0)),
                  pl.BlockSpec((1, 6, D_MODEL), lambda i: (i // per_b, 0, 0)),
                  full((D_MODEL, EXPERT_FF)), full((D_MODEL, EXPERT_FF)), full((EXPERT_FF, D_MODEL)),
                  full((1, D_MODEL))],
        out_specs=pl.BlockSpec((tm, D_MODEL), lambda i: (i, 0)),
        out_shape=jax.ShapeDtypeStruct((T, D_MODEL), F32),
        compiler_params=pltpu.CompilerParams(
            dimension_semantics=("arbitrary",), vmem_limit_bytes=VMEM_LIMIT),
        name="shared_expert_combine",
    )(yg, w_tk, h2p, x1, mod3, wsg, wsu, wsd, g_post)


def _rope_tables(positions):
    inv = jnp.power(ROPE_THETA, -jnp.arange(ROPE_HALF, dtype=F32) / ROPE_HALF)
    ang = positions.astype(F32)[..., None] * inv
    cos, sin = jnp.cos(ang), jnp.sin(ang)
    rest = ATT_HEAD_DIM - 2 * ROPE_HALF
    cs = jnp.concatenate([cos, cos, jnp.ones(ang.shape[:-1] + (rest,), F32)], axis=-1)
    sn = jnp.concatenate([-sin, sin, jnp.zeros(ang.shape[:-1] + (rest,), F32)], axis=-1)
    return jnp.tile(cs, (1, 1, 2)), jnp.tile(sn, (1, 1, 2))


def _layer(x, c, positions, w_ada, b_ada, g_pre_mix, g_post_mix, g_pre_ffn, g_post_ffn,
           w_in, conv_w, conv_b, b_gates, g_mlstm, w_branch_a, w_branch_b, w_out,
           router_w, router_bias, w_exp_gate, w_exp_up, w_exp_down, w_sh_gate, w_sh_up, w_sh_down):
    B, S, D = x.shape
    T = B * S
    H = MLSTM_HEADS
    x2 = x.reshape(T, D)

    mod3 = _adaln(c, w_ada, b_ada).reshape(B, 6, D)

    a_w = 3 * ATT_GROUP_W
    o_mq = 3 * a_w
    o_mk = o_mq + H * MLSTM_QK_DIM
    o_mv = o_mk + H * MLSTM_QK_DIM
    o_mo = o_mv + H * MLSTM_V_DIM
    o_mi = o_mo + H * MLSTM_V_DIM
    o_ga = o_mi + 2 * H
    o_gb = o_ga + D
    seg = lambda o, w: w_in[:, o:o + w]
    w_main = jnp.concatenate(
        [seg(o_mv, H * MLSTM_V_DIM), seg(o_mo, H * MLSTM_V_DIM), seg(o_ga, D), seg(o_gb, D),
         seg(o_mq, H * MLSTM_QK_DIM), seg(o_mk, H * MLSTM_QK_DIM),
         seg(0, a_w), seg(a_w, a_w), seg(2 * a_w, a_w)], axis=1).astype(BF16)
    w_if = jnp.pad(seg(o_mi, 2 * H), ((0, 0), (0, 128 - 2 * H))).astype(BF16)

    proj, gates = _in_proj(x2, mod3, g_pre_mix.reshape(1, D), w_main, w_if, S)
    proj3 = proj.reshape(B, S, PROJ_W)

    cs, sn = _rope_tables(positions)
    y_a = _attention(proj3, cs, sn)

    bg_row = jnp.pad(b_gates.reshape(1, 2 * H), ((0, 0), (0, 128 - 2 * H)))
    y_b = _mlstm(proj3, gates.reshape(B, S, 128), conv_w, conv_b.reshape(1, -1), bg_row,
                 g_mlstm.reshape(1, -1))

    x1, h2p = _merge(y_a.reshape(T, ATT_GROUP_W), y_b.reshape(T, D), proj, x2, mod3,
                     w_branch_a.astype(BF16), w_branch_b.astype(BF16), w_out.astype(BF16),
                     g_post_mix.reshape(1, D), g_pre_ffn.reshape(1, D), S)

    rw_t = router_w.T.astype(BF16)
    bias_col = jnp.broadcast_to(router_bias.reshape(N_EXPERTS, 1), (N_EXPERTS, 128))
    idx, wts, rank, cnt = _router(h2p, rw_t[:, :HALF], rw_t[:, HALF:], bias_col)

    bm = EXPERT_BLOCK
    nb = (T * TOP_K) // bm + N_EXPERTS
    counts = cnt[:, 0].astype(jnp.int32)
    padded = (counts + bm - 1) // bm * bm
    pend = jnp.cumsum(padded)
    pstart = pend - padded
    dest = pstart[idx] + rank
    nused = (pend[-1] // bm).astype(jnp.int32).reshape(1)
    blk_e = jnp.minimum(
        jnp.searchsorted(pend, jnp.arange(nb, dtype=jnp.int32) * bm, side="right"),
        N_EXPERTS - 1).astype(jnp.int32)

    xs = _dispatch(h2p, dest, nb * bm)
    ys = _expert_ffn(blk_e, nused, xs, w_exp_gate, w_exp_up, w_exp_down)
    yg = _collect(ys, dest)

    out = _final(yg, wts.T, h2p, x1, mod3, w_sh_gate.astype(BF16), w_sh_up.astype(BF16),
                 w_sh_down.astype(BF16), g_post_ffn.reshape(1, D), S)
    return out.reshape(B, S, D)


def _dispatch(h2p, dest, n_slots):
    T = h2p.shape[0]
    tok = jnp.broadcast_to(jnp.arange(T, dtype=jnp.int32)[None, :], dest.shape)
    slot_tok = jnp.zeros((n_slots,), jnp.int32).at[dest.reshape(-1)].set(tok.reshape(-1))
    return h2p[slot_tok]


def _collect(ys, dest):
    return ys[dest]


def kernel(x, c, positions, w_ada, b_ada, g_pre_mix, g_post_mix, g_pre_ffn, g_post_ffn, w_in, conv_w, conv_b, b_gates, g_mlstm, w_branch_a, w_branch_b, w_out, router_w, router_bias, w_exp_gate, w_exp_up, w_exp_down, w_sh_gate, w_sh_up, w_sh_down):
    depth = w_ada.shape[0]
    for l in range(depth):
        x = _layer(x, c, positions, w_ada[l], b_ada[l], g_pre_mix[l], g_post_mix[l], g_pre_ffn[l],
                   g_post_ffn[l], w_in[l], conv_w[l], conv_b[l], b_gates[l], g_mlstm[l],
                   w_branch_a[l], w_branch_b[l], w_out[l], router_w[l], router_bias[l],
                   w_exp_gate[l], w_exp_up[l], w_exp_down[l], w_sh_gate[l], w_sh_up[l], w_sh_down[l])
    return x
```

```python
import functools

import jax
import jax.numpy as jnp
from jax import lax
from jax.experimental import pallas as pl
from jax.experimental.pallas import tpu as pltpu
from jax.experimental.pallas import tpu_sc as plsc

F32 = jnp.float32
BF16 = jnp.bfloat16
HIGHEST = lax.Precision.HIGHEST

D_MODEL = 1024
ATT_GROUPS = ((128, 1), (512, 4), (2048, 16))
ATT_HEAD_DIM = 64
ATT_GROUP_W = 256
ATT_BLK = 128
ROPE_THETA = 500000.0
ROPE_HALF = 8
MLSTM_HEADS = 4
MLSTM_QK_DIM = 128
MLSTM_V_DIM = 256
MLSTM_CHUNK = 64
CONV_WIDTH = 4
N_EXPERTS = 256
TOP_K = 8
N_GROUPS = 8
TOPK_GROUPS = 4
EXPERT_FF = 256
ROUTED_SCALE = 2.5
NORM_EPS = 1e-6
NEG = -1e30

OFF_MV, OFF_MO, OFF_GA, OFF_GB = 0, 1024, 2048, 3072
OFF_MQ, OFF_MK = 4096, 4608
OFF_AQ, OFF_AK, OFF_AV = 5120, 5888, 6656
PROJ_W = 7424
HALF = D_MODEL // 2

EXPERT_BLOCK = 256
VMEM_LIMIT = 56 * 1024 * 1024


def _nt(a, b, precision=None):
    return lax.dot_general(a, b, (((1,), (1,)), ((), ())), preferred_element_type=F32,
                           precision=precision)


def _tn(a, b):
    return lax.dot_general(a, b, (((0,), (0,)), ((), ())), preferred_element_type=F32)


def _silu(x):
    return x * jax.nn.sigmoid(x)


def _pack_pair(lo, hi):
    lo_b = pltpu.bitcast(lo.astype(BF16).astype(F32), jnp.uint32)
    hi_b = pltpu.bitcast(hi.astype(BF16).astype(F32), jnp.uint32)
    return (lo_b >> 16) | (hi_b & jnp.uint32(0xFFFF0000))


def _unpack_pair(w):
    lo = pltpu.bitcast(w << 16, F32)
    hi = pltpu.bitcast(w & jnp.uint32(0xFFFF0000), F32)
    return lo, hi


def _mod_kernel(c_ref, w_ref, b_ref, o_ref):
    a = _silu(c_ref[...])
    o_ref[...] = jnp.dot(a, w_ref[...], preferred_element_type=F32, precision=HIGHEST) + b_ref[...]


def _adaln(c, w_ada, b_ada):
    B = c.shape[0]
    n = w_ada.shape[1]
    tn = 512
    return pl.pallas_call(
        _mod_kernel,
        grid=(n // tn,),
        in_specs=[pl.BlockSpec((B, D_MODEL), lambda j: (0, 0)),
                  pl.BlockSpec((D_MODEL, tn), lambda j: (0, j)),
                  pl.BlockSpec((1, tn), lambda j: (0, j))],
        out_specs=pl.BlockSpec((B, tn), lambda j: (0, j)),
        out_shape=jax.ShapeDtypeStruct((B, n), F32),
        name="adaln_mod",
    )(c, w_ada, b_ada.reshape(1, n))


def _proj_kernel(x_ref, mod_ref, g_ref, w_ref, wif_ref, o_ref, gates_ref, h_ref):
    @pl.when(pl.program_id(1) == 0)
    def _():
        x = x_ref[...]
        ms = jnp.mean(x * x, axis=-1, keepdims=True)
        y = x * lax.rsqrt(ms + NORM_EPS) * g_ref[...]
        h = (y * (1.0 + mod_ref[0, 1:2, :]) + mod_ref[0, 0:1, :]).astype(BF16)
        h_ref[...] = h
        gates_ref[...] = jnp.dot(h, wif_ref[...], preferred_element_type=F32)

    o_ref[...] = jnp.dot(h_ref[...], w_ref[...], preferred_element_type=F32).astype(BF16)


def _in_proj(x2, mod3, g_pre, w_main, w_if, seq):
    T = x2.shape[0]
    tm, tn = 1024, 256
    per_b = seq // tm
    return pl.pallas_call(
        _proj_kernel,
        grid=(T // tm, PROJ_W // tn),
        in_specs=[pl.BlockSpec((tm, D_MODEL), lambda i, j: (i, 0)),
                  pl.BlockSpec((1, 6, D_MODEL), lambda i, j: (i // per_b, 0, 0)),
                  pl.BlockSpec((1, D_MODEL), lambda i, j: (0, 0)),
                  pl.BlockSpec((D_MODEL, tn), lambda i, j: (0, j)),
                  pl.BlockSpec((D_MODEL, 128), lambda i, j: (0, 0))],
        out_specs=[pl.BlockSpec((tm, tn), lambda i, j: (i, j)),
                   pl.BlockSpec((tm, 128), lambda i, j: (i, 0))],
        out_shape=[jax.ShapeDtypeStruct((T, PROJ_W), BF16),
                   jax.ShapeDtypeStruct((T, 128), F32)],
        scratch_shapes=[pltpu.VMEM((tm, D_MODEL), BF16)],
        compiler_params=pltpu.CompilerParams(
            dimension_semantics=("arbitrary", "arbitrary"), vmem_limit_bytes=VMEM_LIMIT),
        name="norm_in_proj",
    )(x2, mod3, g_pre, w_main, w_if)


def _attn_kernel(q_ref, k_ref, v_ref, cs_ref, sn_ref, o_ref, qf, kf, vf, acc, m_s, l_s, *, seq):
    g = pl.program_id(1)
    lane = lax.broadcasted_iota(jnp.int32, (ATT_BLK, 128), 1)
    first = (lane % ATT_HEAD_DIM) < ROPE_HALF
    low_head = lane < ATT_HEAD_DIM

    def rope(x, cs, sn):
        partner = jnp.where(first, pltpu.roll(x, 128 - ROPE_HALF, 1), pltpu.roll(x, ROPE_HALF, 1))
        return x * cs + partner * sn

    def zero_pad(i, _):
        rows = pl.ds(pl.multiple_of(i * ATT_BLK, ATT_BLK), ATT_BLK)
        for hp in range(2):
            kf[hp, rows, :] = jnp.zeros((ATT_BLK, 128), F32)
            vf[hp, rows, :] = jnp.zeros((ATT_BLK, 128), F32)
        return 0

    lax.fori_loop(0, seq // ATT_BLK, zero_pad, 0)

    def stage(i, _):
        r = pl.multiple_of(i * ATT_BLK, ATT_BLK)
        rows = pl.ds(r, ATT_BLK)
        prow = pl.ds(pl.multiple_of(seq + i * ATT_BLK, ATT_BLK), ATT_BLK)
        cs = cs_ref[0, rows, :]
        sn = sn_ref[0, rows, :]
        for hp in range(2):
            cols = pl.ds(hp * 128, 128)
            qf[hp, rows, :] = rope(q_ref[0, rows, cols].astype(F32), cs, sn) * (ATT_HEAD_DIM ** -0.5)
            kf[hp, prow, :] = rope(k_ref[0, rows, cols].astype(F32), cs, sn)
            vf[hp, prow, :] = v_ref[0, rows, cols].astype(F32)
        return 0

    lax.fori_loop(0, seq // ATT_BLK, stage, 0)

    qi = lax.broadcasted_iota(jnp.int32, (ATT_BLK, 2 * ATT_BLK), 0)
    ki = lax.broadcasted_iota(jnp.int32, (ATT_BLK, 2 * ATT_BLK), 1)
    band = (ki >= qi) & (ki <= qi + ATT_BLK)

    def process(d, init):
        span = ATT_BLK * d

        def body(c, _):
            rho = c % d
            n = c // d
            qstart = rho + n * span
            kstart = seq + qstart - span
            first_key = jnp.where(n > 0, 0, ATT_BLK)
            valid = band & (ki >= first_key)
            qrows = pl.ds(qstart, ATT_BLK, stride=d) if d > 1 else pl.ds(qstart, ATT_BLK)
            krows = pl.ds(kstart, 2 * ATT_BLK, stride=d) if d > 1 else pl.ds(kstart, 2 * ATT_BLK)
            for hp in range(2):
                q2 = qf[hp, qrows, :]
                k2 = kf[hp, krows, :].astype(BF16)
                v2 = vf[hp, krows, :].astype(BF16)
                res = []
                for hh in range(2):
                    hm = low_head if hh == 0 else jnp.logical_not(low_head)
                    qh = jnp.where(hm, q2, 0.0).astype(BF16)
                    s = jnp.where(valid, _nt(qh, k2), NEG)
                    m = jnp.max(s, axis=1, keepdims=True)
                    p = jnp.exp(s - m)
                    l = jnp.sum(p, axis=1, keepdims=True)
                    o = jnp.dot(p.astype(BF16), v2, preferred_element_type=F32)
                    res.append((o, m, l))
                o_b = jnp.where(low_head, res[0][0], res[1][0])
                m_b = jnp.where(low_head, res[0][1], res[1][1])
                l_b = jnp.where(low_head, res[0][2], res[1][2])
                if init:
                    acc[hp, qrows, :] = o_b
                    m_s[hp, qrows, :] = m_b
                    l_s[hp, qrows, :] = l_b
                else:
                    m_old = m_s[hp, qrows, :]
                    m_new = jnp.maximum(m_old, m_b)
                    a_old = jnp.exp(m_old - m_new)
                    a_new = jnp.exp(m_b - m_new)
                    acc[hp, qrows, :] = acc[hp, qrows, :] * a_old + o_b * a_new
                    l_s[hp, qrows, :] = l_s[hp, qrows, :] * a_old + l_b * a_new
                    m_s[hp, qrows, :] = m_new
            return 0

        lax.fori_loop(0, seq // ATT_BLK, body, 0)

    for gi, (_, d) in enumerate(ATT_GROUPS):
        @pl.when(g == gi)
        def _(d=d, gi=gi):
            process(d, gi == 0)

    @pl.when(g == len(ATT_GROUPS) - 1)
    def _():
        def fin(i, _):
            rows = pl.ds(pl.multiple_of(i * ATT_BLK, ATT_BLK), ATT_BLK)
            for hp in range(2):
                o_ref[0, rows, pl.ds(hp * 128, 128)] = (acc[hp, rows, :] / l_s[hp, rows, :]).astype(BF16)
            return 0

        lax.fori_loop(0, seq // ATT_BLK, fin, 0)


def _attention(proj3, cs, sn):
    B, S, _ = proj3.shape
    ng = len(ATT_GROUPS)
    qb, kb, vb = OFF_AQ // ATT_GROUP_W, OFF_AK // ATT_GROUP_W, OFF_AV // ATT_GROUP_W
    return pl.pallas_call(
        functools.partial(_attn_kernel, seq=S),
        grid=(B, ng),
        in_specs=[pl.BlockSpec((1, S, ATT_GROUP_W), lambda b, g: (b, 0, qb + g)),
                  pl.BlockSpec((1, S, ATT_GROUP_W), lambda b, g: (b, 0, kb + g)),
                  pl.BlockSpec((1, S, ATT_GROUP_W), lambda b, g: (b, 0, vb + g)),
                  pl.BlockSpec((1, S, 128), lambda b, g: (b, 0, 0)),
                  pl.BlockSpec((1, S, 128), lambda b, g: (b, 0, 0))],
        out_specs=pl.BlockSpec((1, S, ATT_GROUP_W), lambda b, g: (b, 0, 0)),
        out_shape=jax.ShapeDtypeStruct((B, S, ATT_GROUP_W), BF16),
        scratch_shapes=[pltpu.VMEM((2, S, 128), F32),
                        pltpu.VMEM((2, 2 * S, 128), F32),
                        pltpu.VMEM((2, 2 * S, 128), F32),
                        pltpu.VMEM((2, S, 128), F32),
                        pltpu.VMEM((2, S, 128), F32),
                        pltpu.VMEM((2, S, 128), F32)],
        compiler_params=pltpu.CompilerParams(
            dimension_semantics=("arbitrary", "arbitrary"), vmem_limit_bytes=VMEM_LIMIT),
        name="dilated_attention",
    )(proj3, proj3, proj3, cs, sn)


def _log_sigmoid(x):
    return jnp.minimum(x, 0.0) - jnp.log(1.0 + jnp.exp(-jnp.abs(x)))


def _mlstm_kernel(mq_ref, mk_ref, mv_ref, mo_ref, gates_ref, cwq_ref, cwk_ref, cbq_ref, cbk_ref,
                  bg_ref, gm_ref, o_ref, pad_ref, q_s, k_s, c_ref, *, seq):
    h = pl.program_id(1)
    L = MLSTM_CHUNK
    DK = MLSTM_QK_DIM
    halo = 8

    pad_ref[0:halo, :] = jnp.zeros((halo, 2 * DK), F32)
    blk = 128
    for i in range(seq // blk):
        pad_ref[halo + i * blk:halo + (i + 1) * blk, 0:DK] = mq_ref[0, i * blk:(i + 1) * blk, :].astype(F32)
        pad_ref[halo + i * blk:halo + (i + 1) * blk, DK:2 * DK] = mk_ref[0, i * blk:(i + 1) * blk, :].astype(F32)
    for i in range(seq // blk):
        yq = jnp.broadcast_to(cbq_ref[...], (blk, DK))
        yk = jnp.broadcast_to(cbk_ref[...], (blk, DK))
        for j in range(CONV_WIDTH):
            r0 = halo + i * blk - (CONV_WIDTH - 1) + j
            yq = yq + pad_ref[r0:r0 + blk, 0:DK] * cwq_ref[j:j + 1, :]
            yk = yk + pad_ref[r0:r0 + blk, DK:2 * DK] * cwk_ref[j:j + 1, :]
        q_s[i * blk:(i + 1) * blk, :] = _silu(yq).astype(BF16)
        k_s[i * blk:(i + 1) * blk, :] = (_silu(yk) * (DK ** -0.5)).astype(BF16)

    lane = lax.broadcasted_iota(jnp.int32, (1, 128), 1)
    sel_i = (lane == h).astype(F32)
    sel_f = (lane == h + MLSTM_HEADS).astype(F32)
    r8 = lax.broadcasted_iota(jnp.int32, (8, 128), 0)
    l8 = lax.broadcasted_iota(jnp.int32, (8, 128), 1)
    selmat = jnp.where(((r8 == 0) & (l8 == h)) | ((r8 == 1) & (l8 == h + MLSTM_HEADS)), 1.0, 0.0)
    ri = lax.broadcasted_iota(jnp.int32, (L, L), 0)
    ci = lax.broadcasted_iota(jnp.int32, (L, L), 1)
    causal = ci <= ri
    tri = causal.astype(F32)
    tri_t = (ri <= ci).astype(F32)
    bias_row = bg_ref[...]
    g_row = gm_ref[...]

    c_ref[...] = jnp.zeros((DK, MLSTM_V_DIM), F32)

    def chunk(c, carry):
        n_row, m_prev = carry
        rows = pl.ds(pl.multiple_of(c * L, L), L)
        q = q_s[rows, :]
        k = k_s[rows, :]
        v = mv_ref[0, rows, :]
        gch = gates_ref[0, rows, :] + bias_row
        i_col = jnp.sum(gch * sel_i, axis=1, keepdims=True)
        f_col = jnp.sum(gch * sel_f, axis=1, keepdims=True)
        rows_b = _nt(selmat, gch, precision=HIGHEST)
        i_row = rows_b[0:1, :]
        lf_row = _log_sigmoid(rows_b[1:2, :])
        lf_col = _log_sigmoid(f_col)
        b_col_b = jnp.dot(tri, jnp.broadcast_to(lf_col, (L, L)), preferred_element_type=F32,
                          precision=HIGHEST)
        b_row_b = jnp.dot(jnp.broadcast_to(lf_row, (L, L)), tri_t, preferred_element_type=F32,
                          precision=HIGHEST)
        b_col = b_col_b[:, 0:1]
        dmat = jnp.where(causal, b_col_b - b_row_b + i_row, NEG)
        m_t = jnp.maximum(b_col + m_prev, jnp.max(dmat, axis=1, keepdims=True))
        wts = jnp.exp(dmat - m_t)
        sc = _nt(q, k) * wts
        inter = jnp.exp(b_col + m_prev - m_t)
        c_old = c_ref[...]
        num = (jnp.dot(sc.astype(BF16), v, preferred_element_type=F32)
               + inter * jnp.dot(q, c_old.astype(BF16), preferred_element_type=F32))
        den = (jnp.sum(sc, axis=1, keepdims=True)
               + inter * jnp.sum(q.astype(F32) * n_row, axis=1, keepdims=True))
        hh = num / jnp.maximum(jnp.abs(den), jnp.exp(-m_t))
        ms = jnp.mean(hh * hh, axis=1, keepdims=True)
        hn = hh * lax.rsqrt(ms + NORM_EPS) * g_row
        o_ref[0, rows, :] = (hn * jax.nn.sigmoid(mo_ref[0, rows, :].astype(F32))).astype(BF16)

        b_end = b_col_b[L - 1:L, 0:1]
        g_col = b_end - b_col + i_col
        m_new = jnp.maximum(b_end + m_prev, jnp.max(g_col, axis=0, keepdims=True))
        decay = jnp.exp(b_end + m_prev - m_new)
        wk = jnp.exp(g_col - m_new) * k.astype(F32)
        c_ref[...] = decay * c_old + _tn(wk.astype(BF16), v)
        n_new = decay * n_row + jnp.sum(wk, axis=0, keepdims=True)
        return n_new, m_new

    lax.fori_loop(0, seq // L, chunk, (jnp.zeros((1, DK), F32), jnp.zeros((1, 1), F32)))


def _mlstm(proj3, gates3, conv_w, conv_b, bg_row, g_mlstm):
    B, S, _ = proj3.shape
    H, DK, DV = MLSTM_HEADS, MLSTM_QK_DIM, MLSTM_V_DIM
    qb, kb = OFF_MQ // DK, OFF_MK // DK
    vb, ob = OFF_MV // DV, OFF_MO // DV
    nq = (H * DK) // DK
    return pl.pallas_call(
        functools.partial(_mlstm_kernel, seq=S),
        grid=(B, H),
        in_specs=[pl.BlockSpec((1, S, DK), lambda b, h: (b, 0, qb + h)),
                  pl.BlockSpec((1, S, DK), lambda b, h: (b, 0, kb + h)),
                  pl.BlockSpec((1, S, DV), lambda b, h: (b, 0, vb + h)),
                  pl.BlockSpec((1, S, DV), lambda b, h: (b, 0, ob + h)),
                  pl.BlockSpec((1, S, 128), lambda b, h: (b, 0, 0)),
                  pl.BlockSpec((CONV_WIDTH, DK), lambda b, h: (0, h)),
                  pl.BlockSpec((CONV_WIDTH, DK), lambda b, h: (0, nq + h)),
                  pl.BlockSpec((1, DK), lambda b, h: (0, h)),
                  pl.BlockSpec((1, DK), lambda b, h: (0, nq + h)),
                  pl.BlockSpec((1, 128), lambda b, h: (0, 0)),
                  pl.BlockSpec((1, DV), lambda b, h: (0, h))],
        out_specs=pl.BlockSpec((1, S, DV), lambda b, h: (b, 0, h)),
        out_shape=jax.ShapeDtypeStruct((B, S, H * DV), BF16),
        scratch_shapes=[pltpu.VMEM((S + 8, 2 * DK), F32),
                        pltpu.VMEM((S, DK), BF16),
                        pltpu.VMEM((S, DK), BF16),
                        pltpu.VMEM((DK, DV), F32)],
        compiler_params=pltpu.CompilerParams(
            dimension_semantics=("arbitrary", "arbitrary"), vmem_limit_bytes=VMEM_LIMIT),
        name="mlstm_chunkwise",
    )(proj3, proj3, proj3, proj3, gates3, conv_w, conv_w, conv_b, conv_b, bg_row, g_mlstm)


def _rms(y, g):
    ms = jnp.mean(y * y, axis=-1, keepdims=True)
    return y * lax.rsqrt(ms + NORM_EPS) * g


def _merge_kernel(ya_ref, yb_ref, ga_ref, gb_ref, x_ref, mod_ref, wa_ref, wb_ref, wo_ref,
                  gpost_ref, gpre_ref, x1_ref, h2_ref):
    pa = jnp.dot(ya_ref[...], wa_ref[...], preferred_element_type=F32)
    pb = jnp.dot(yb_ref[...], wb_ref[...], preferred_element_type=F32)
    merged = (jax.nn.sigmoid(ga_ref[...].astype(F32)) * pa
              + jax.nn.sigmoid(gb_ref[...].astype(F32)) * pb)
    y = jnp.dot(merged.astype(BF16), wo_ref[...], preferred_element_type=F32)
    x1 = x_ref[...] + mod_ref[0, 2:3, :] * _rms(y, gpost_ref[...])
    x1_ref[...] = x1
    h2 = _rms(x1, gpre_ref[...]) * (1.0 + mod_ref[0, 4:5, :]) + mod_ref[0, 3:4, :]
    h2_ref[...] = _pack_pair(h2[:, :HALF], h2[:, HALF:])


def _merge(ya2, yb2, proj2, x2, mod3, wa, wb, wo, g_post, g_pre, seq):
    T = x2.shape[0]
    tm = 512
    per_b = seq // tm
    full = lambda shape: pl.BlockSpec(shape, lambda i: (0,) * len(shape))
    return pl.pallas_call(
        _merge_kernel,
        grid=(T // tm,),
        in_specs=[pl.BlockSpec((tm, ATT_GROUP_W), lambda i: (i, 0)),
                  pl.BlockSpec((tm, D_MODEL), lambda i: (i, 0)),
                  pl.BlockSpec((tm, D_MODEL), lambda i: (i, OFF_GA // D_MODEL)),
                  pl.BlockSpec((tm, D_MODEL), lambda i: (i, OFF_GB // D_MODEL)),
                  pl.BlockSpec((tm, D_MODEL), lambda i: (i, 0)),
                  pl.BlockSpec((1, 6, D_MODEL), lambda i: (i // per_b, 0, 0)),
                  full((ATT_GROUP_W, D_MODEL)), full((D_MODEL, D_MODEL)), full((D_MODEL, D_MODEL)),
                  full((1, D_MODEL)), full((1, D_MODEL))],
        out_specs=[pl.BlockSpec((tm, D_MODEL), lambda i: (i, 0)),
                   pl.BlockSpec((tm, HALF), lambda i: (i, 0))],
        out_shape=[jax.ShapeDtypeStruct((T, D_MODEL), F32),
                   jax.ShapeDtypeStruct((T, HALF), jnp.uint32)],
        compiler_params=pltpu.CompilerParams(
            dimension_semantics=("arbitrary",), vmem_limit_bytes=VMEM_LIMIT),
        name="merge_out_proj",
    )(ya2, yb2, proj2, proj2, x2, mod3, wa, wb, wo, g_post, g_pre)


def _router_kernel(h2_ref, rlo_ref, rhi_ref, bias_ref, idx_ref, w_ref, rank_ref, cnt_ref):
    E = N_EXPERTS
    tr = h2_ref.shape[0]
    gsz = E // N_GROUPS

    @pl.when(pl.program_id(0) == 0)
    def _():
        cnt_ref[...] = jnp.zeros(cnt_ref.shape, F32)

    lo, hi = _unpack_pair(h2_ref[...])
    logits = _nt(rlo_ref[...], lo.astype(BF16)) + _nt(rhi_ref[...], hi.astype(BF16))
    scores = jax.nn.sigmoid(logits)
    sel = scores + bias_ref[:, 0:1]

    gi = lax.broadcasted_iota(jnp.int32, (gsz, tr), 0).astype(F32)
    gs_rows = []
    for g in range(N_GROUPS):
        blk = sel[g * gsz:(g + 1) * gsz, :]
        m1 = jnp.max(blk, axis=0, keepdims=True)
        a1 = jnp.min(jnp.where(blk == m1, gi, float(E)), axis=0, keepdims=True)
        m2 = jnp.max(jnp.where(gi == a1, -jnp.inf, blk), axis=0, keepdims=True)
        gs_rows.append(m1 + m2)
    gs = jnp.concatenate(gs_rows, axis=0)
    g8 = lax.broadcasted_iota(jnp.int32, (N_GROUPS, tr), 0).astype(F32)
    gmask = jnp.zeros((N_GROUPS, tr), F32)
    for _ in range(TOPK_GROUPS):
        m = jnp.max(gs, axis=0, keepdims=True)
        a = jnp.min(jnp.where(gs == m, g8, float(E)), axis=0, keepdims=True)
        hit = g8 == a
        gmask = jnp.where(hit, 1.0, gmask)
        gs = jnp.where(hit, -jnp.inf, gs)
    selm = jnp.concatenate(
        [jnp.where(gmask[g:g + 1, :] > 0.0, sel[g * gsz:(g + 1) * gsz, :], -jnp.inf)
         for g in range(N_GROUPS)], axis=0)

    ei = lax.broadcasted_iota(jnp.int32, (E, tr), 0).astype(F32)
    picks, weights = [], []
    chosen = jnp.zeros((E, tr), F32)
    for _ in range(TOP_K):
        m = jnp.max(selm, axis=0, keepdims=True)
        a = jnp.min(jnp.where(selm == m, ei, float(E)), axis=0, keepdims=True)
        hit = ei == a
        picks.append(a)
        weights.append(jnp.sum(jnp.where(hit, scores, 0.0), axis=0, keepdims=True))
        chosen = jnp.where(hit, 1.0, chosen)
        selm = jnp.where(hit, -jnp.inf, selm)
    wsum = weights[0]
    for w in weights[1:]:
        wsum = wsum + w

    ti = lax.broadcasted_iota(jnp.int32, (tr, tr), 0)
    tj = lax.broadcasted_iota(jnp.int32, (tr, tr), 1)
    before = (ti < tj).astype(BF16)
    pos = jnp.dot(chosen.astype(BF16), before, preferred_element_type=F32) + cnt_ref[:, 0:1]
    ranks = [jnp.sum(jnp.where(ei == a, pos, 0.0), axis=0, keepdims=True) for a in picks]
    cnt_ref[...] = cnt_ref[...] + jnp.sum(chosen, axis=1, keepdims=True)

    idx_ref[...] = jnp.concatenate(picks, axis=0).astype(jnp.int32)
    w_ref[...] = jnp.concatenate([w / wsum * ROUTED_SCALE for w in weights], axis=0)
    rank_ref[...] = jnp.concatenate(ranks, axis=0).astype(jnp.int32)


def _router(h2p, r_lo, r_hi, bias_col):
    T = h2p.shape[0]
    tr = 512
    full = lambda shape: pl.BlockSpec(shape, lambda i: (0,) * len(shape))
    return pl.pallas_call(
        _router_kernel,
        grid=(T // tr,),
        in_specs=[pl.BlockSpec((tr, HALF), lambda i: (i, 0)),
                  full((N_EXPERTS, HALF)), full((N_EXPERTS, HALF)), full((N_EXPERTS, 128))],
        out_specs=[pl.BlockSpec((TOP_K, tr), lambda i: (0, i)),
                   pl.BlockSpec((TOP_K, tr), lambda i: (0, i)),
                   pl.BlockSpec((TOP_K, tr), lambda i: (0, i)),
                   full((N_EXPERTS, 128))],
        out_shape=[jax.ShapeDtypeStruct((TOP_K, T), jnp.int32),
                   jax.ShapeDtypeStruct((TOP_K, T), F32),
                   jax.ShapeDtypeStruct((TOP_K, T), jnp.int32),
                   jax.ShapeDtypeStruct((N_EXPERTS, 128), F32)],
        compiler_params=pltpu.CompilerParams(
            dimension_semantics=("arbitrary",), vmem_limit_bytes=VMEM_LIMIT),
        name="router_topk",
    )(h2p, r_lo, r_hi, bias_col)


def _ffn_kernel(blk_e_ref, nused_ref, x_ref, wg_ref, wu_ref, wd_ref, y_ref, wg_s, wu_s, wd_s):
    i = pl.program_id(0)

    @pl.when(i < nused_ref[0])
    def _():
        e = blk_e_ref[i]
        prev = blk_e_ref[jnp.maximum(i - 1, 0)]

        @pl.when((i == 0) | (e != prev))
        def _():
            wg_s[...] = wg_ref[0].astype(BF16)
            wu_s[...] = wu_ref[0].astype(BF16)
            wd_s[...] = wd_ref[0].astype(BF16)

        lo, hi = _unpack_pair(x_ref[...])
        lo = lo.astype(BF16)
        hi = hi.astype(BF16)
        gate = (jnp.dot(lo, wg_s[0:HALF, :], preferred_element_type=F32)
                + jnp.dot(hi, wg_s[HALF:, :], preferred_element_type=F32))
        up = (jnp.dot(lo, wu_s[0:HALF, :], preferred_element_type=F32)
              + jnp.dot(hi, wu_s[HALF:, :], preferred_element_type=F32))
        hid = (_silu(gate) * up).astype(BF16)
        out = jnp.dot(hid, wd_s[...], preferred_element_type=F32)
        y_ref[...] = _pack_pair(out[:, :HALF], out[:, HALF:])


def _expert_ffn(blk_e, nused, xs, w_gate, w_up, w_down):
    P = xs.shape[0]
    bm = EXPERT_BLOCK
    nb = P // bm

    def row_map(i, blk_e_ref, nused_ref):
        return (jnp.minimum(i, nused_ref[0] - 1), 0)

    def w_map(i, blk_e_ref, nused_ref):
        return (blk_e_ref[jnp.minimum(i, nused_ref[0] - 1)], 0, 0)

    grid_spec = pltpu.PrefetchScalarGridSpec(
        num_scalar_prefetch=2,
        grid=(nb,),
        in_specs=[pl.BlockSpec((bm, HALF), row_map),
                  pl.BlockSpec((1, D_MODEL, EXPERT_FF), w_map),
                  pl.BlockSpec((1, D_MODEL, EXPERT_FF), w_map),
                  pl.BlockSpec((1, EXPERT_FF, D_MODEL), w_map)],
        out_specs=pl.BlockSpec((bm, HALF), row_map),
        scratch_shapes=[pltpu.VMEM((D_MODEL, EXPERT_FF), BF16),
                        pltpu.VMEM((D_MODEL, EXPERT_FF), BF16),
                        pltpu.VMEM((EXPERT_FF, D_MODEL), BF16)],
    )
    return pl.pallas_call(
        _ffn_kernel,
        grid_spec=grid_spec,
        out_shape=jax.ShapeDtypeStruct((P, HALF), jnp.uint32),
        compiler_params=pltpu.CompilerParams(
            dimension_semantics=("arbitrary",), vmem_limit_bytes=VMEM_LIMIT),
        name="routed_experts",
    )(blk_e, nused, xs, w_gate, w_up, w_down)


def _final_kernel(yg_ref, w_ref, h2_ref, x1_ref, mod_ref, wsg_ref, wsu_ref, wsd_ref, gpost_ref, o_ref):
    lo, hi = _unpack_pair(h2_ref[...])
    lo = lo.astype(BF16)
    hi = hi.astype(BF16)
    gate = (jnp.dot(lo, wsg_ref[0:HALF, :], preferred_element_type=F32)
            + jnp.dot(hi, wsg_ref[HALF:, :], preferred_element_type=F32))
    up = (jnp.dot(lo, wsu_ref[0:HALF, :], preferred_element_type=F32)
          + jnp.dot(hi, wsu_ref[HALF:, :], preferred_element_type=F32))
    shared = jnp.dot((_silu(gate) * up).astype(BF16), wsd_ref[...], preferred_element_type=F32)
    y_lo = shared[:, :HALF]
    y_hi = shared[:, HALF:]
    for k in range(TOP_K):
        r_lo, r_hi = _unpack_pair(yg_ref[k])
        wk = w_ref[:, k:k + 1]
        y_lo = y_lo + wk * r_lo
        y_hi = y_hi + wk * r_hi
    ms = (jnp.sum(y_lo * y_lo, axis=-1, keepdims=True)
          + jnp.sum(y_hi * y_hi, axis=-1, keepdims=True)) * (1.0 / D_MODEL)
    inv = lax.rsqrt(ms + NORM_EPS)
    o_ref[:, 0:HALF] = x1_ref[:, 0:HALF] + mod_ref[0, 5:6, 0:HALF] * (y_lo * inv * gpost_ref[:, 0:HALF])
    o_ref[:, HALF:] = x1_ref[:, HALF:] + mod_ref[0, 5:6, HALF:] * (y_hi * inv * gpost_ref[:, HALF:])


def _final(yg, w_tk, h2p, x1, mod3, wsg, wsu, wsd, g_post, seq):
    T = x1.shape[0]
    tm = 256
    per_b = seq // tm
    full = lambda shape: pl.BlockSpec(shape, lambda i: (0,) * len(shape))
    return pl.pallas_call(
        _final_kernel,
        grid=(T // tm,),
        in_specs=[pl.BlockSpec((TOP_K, tm, HALF), lambda i: (0, i, 0)),
                  pl.BlockSpec((tm, TOP_K), lambda i: (i, 0)),
                  pl.BlockSpec((tm, HALF), lambda i: (i, 0)),
                  pl.BlockSpec((tm, D_MODEL), lambda i: (i, 0)),
                  pl.BlockSpec((1, 6, D_MODEL), lambda i: (i // per_b, 0, 0)),
                  full((D_MODEL, EXPERT_FF)), full((D_MODEL, EXPERT_FF)), full((EXPERT_FF, D_MODEL)),
                  full((1, D_MODEL))],
        out_specs=pl.BlockSpec((tm, D_MODEL), lambda i: (i, 0)),
        out_shape=jax.ShapeDtypeStruct((T, D_MODEL), F32),
        compiler_params=pltpu.CompilerParams(
            dimension_semantics=("arbitrary",), vmem_limit_bytes=VMEM_LIMIT),
        name="shared_expert_combine",
    )(yg, w_tk, h2p, x1, mod3, wsg, wsu, wsd, g_post)


def _rope_tables(positions):
    inv = jnp.power(ROPE_THETA, -jnp.arange(ROPE_HALF, dtype=F32) / ROPE_HALF)
    ang = positions.astype(F32)[..., None] * inv
    cos, sin = jnp.cos(ang), jnp.sin(ang)
    rest = ATT_HEAD_DIM - 2 * ROPE_HALF
    cs = jnp.concatenate([cos, cos, jnp.ones(ang.shape[:-1] + (rest,), F32)], axis=-1)
    sn = jnp.concatenate([-sin, sin, jnp.zeros(ang.shape[:-1] + (rest,), F32)], axis=-1)
    return jnp.tile(cs, (1, 1, 2)), jnp.tile(sn, (1, 1, 2))


def _layer(x, c, positions, w_ada, b_ada, g_pre_mix, g_post_mix, g_pre_ffn, g_post_ffn,
           w_in, conv_w, conv_b, b_gates, g_mlstm, w_branch_a, w_branch_b, w_out,
           router_w, router_bias, w_exp_gate, w_exp_up, w_exp_down, w_sh_gate, w_sh_up, w_sh_down):
    B, S, D = x.shape
    T = B * S
    H = MLSTM_HEADS
    x2 = x.reshape(T, D)

    mod3 = _adaln(c, w_ada, b_ada).reshape(B, 6, D)

    a_w = 3 * ATT_GROUP_W
    o_mq = 3 * a_w
    o_mk = o_mq + H * MLSTM_QK_DIM
    o_mv = o_mk + H * MLSTM_QK_DIM
    o_mo = o_mv + H * MLSTM_V_DIM
    o_mi = o_mo + H * MLSTM_V_DIM
    o_ga = o_mi + 2 * H
    o_gb = o_ga + D
    seg = lambda o, w: w_in[:, o:o + w]
    w_main = jnp.concatenate(
        [seg(o_mv, H * MLSTM_V_DIM), seg(o_mo, H * MLSTM_V_DIM), seg(o_ga, D), seg(o_gb, D),
         seg(o_mq, H * MLSTM_QK_DIM), seg(o_mk, H * MLSTM_QK_DIM),
         seg(0, a_w), seg(a_w, a_w), seg(2 * a_w, a_w)], axis=1).astype(BF16)
    w_if = jnp.pad(seg(o_mi, 2 * H), ((0, 0), (0, 128 - 2 * H))).astype(BF16)

    proj, gates = _in_proj(x2, mod3, g_pre_mix.reshape(1, D), w_main, w_if, S)
    proj3 = proj.reshape(B, S, PROJ_W)

    cs, sn = _rope_tables(positions)
    y_a = _attention(proj3, cs, sn)

    bg_row = jnp.pad(b_gates.reshape(1, 2 * H), ((0, 0), (0, 128 - 2 * H)))
    y_b = _mlstm(proj3, gates.reshape(B, S, 128), conv_w, conv_b.reshape(1, -1), bg_row,
                 g_mlstm.reshape(1, -1))

    x1, h2p = _merge(y_a.reshape(T, ATT_GROUP_W), y_b.reshape(T, D), proj, x2, mod3,
                     w_branch_a.astype(BF16), w_branch_b.astype(BF16), w_out.astype(BF16),
                     g_post_mix.reshape(1, D), g_pre_ffn.reshape(1, D), S)

    rw_t = router_w.T.astype(BF16)
    bias_col = jnp.broadcast_to(router_bias.reshape(N_EXPERTS, 1), (N_EXPERTS, 128))
    idx, wts, rank, cnt = _router(h2p, rw_t[:, :HALF], rw_t[:, HALF:], bias_col)

    bm = EXPERT_BLOCK
    nb = (T * TOP_K) // bm + N_EXPERTS
    counts = cnt[:, 0].astype(jnp.int32)
    padded = (counts + bm - 1) // bm * bm
    pend = jnp.cumsum(padded)
    pstart = pend - padded
    dest = pstart[idx] + rank
    nused = (pend[-1] // bm).astype(jnp.int32).reshape(1)
    blk_e = jnp.minimum(
        jnp.searchsorted(pend, jnp.arange(nb, dtype=jnp.int32) * bm, side="right"),
        N_EXPERTS - 1).astype(jnp.int32)

    xs = _dispatch(h2p, dest, nb * bm)
    ys = _expert_ffn(blk_e, nused, xs, w_exp_gate, w_exp_up, w_exp_down)
    yg = _collect(ys, dest)

    out = _final(yg, wts.T, h2p, x1, mod3, w_sh_gate.astype(BF16), w_sh_up.astype(BF16),
                 w_sh_down.astype(BF16), g_post_ffn.reshape(1, D), S)
    return out.reshape(B, S, D)


SC_CORES = 2
SC_SUBCORES = 16
SC_WORKERS = SC_CORES * SC_SUBCORES
SC_ROWS = 64


def _sc_mesh():
    return plsc.VectorSubcoreMesh(core_axis_name="c", subcore_axis_name="s",
                                  num_cores=SC_CORES, num_subcores=SC_SUBCORES)


def _worker_id():
    return lax.axis_index("s") * SC_CORES + lax.axis_index("c")


def _dispatch(h2p, dest, n_slots):
    T = h2p.shape[0]
    per_w = T // SC_WORKERS
    nch = per_w // SC_ROWS
    idx = dest.reshape(TOP_K, SC_WORKERS, nch, SC_ROWS).transpose(1, 2, 0, 3)
    idx = idx.reshape(SC_WORKERS, nch * TOP_K, SC_ROWS)

    def body(x_hbm, idx_hbm, xs_hbm, idx_v, buf0, buf1, rsem0, rsem1, ssem0, ssem1):
        wid = _worker_id()
        base = wid * per_w
        pltpu.sync_copy(idx_hbm.at[wid], idx_v)
        bufs = ((buf0, rsem0, ssem0), (buf1, rsem1, ssem1))

        def read(c, buf, rsem):
            return pltpu.make_async_copy(x_hbm.at[pl.ds(base + c * SC_ROWS, SC_ROWS)], buf, rsem)

        def scatter(c, k, buf, ssem):
            return pltpu.make_async_copy(buf, xs_hbm.at[idx_v.at[c * TOP_K + k]], ssem)

        read(0, buf0, rsem0).start()

        @pl.loop(0, nch, step=2)
        def _(c0):
            for b in range(2):
                c = c0 + b
                buf, rsem, ssem = bufs[b]
                obuf, orsem, ossem = bufs[1 - b]
                read(c, buf, rsem).wait()

                @pl.when(c > 0)
                def _():
                    for k in range(TOP_K):
                        scatter(c - 1, k, obuf, ossem).wait()

                @pl.when(c + 1 < nch)
                def _():
                    read(c + 1, obuf, orsem).start()

                for k in range(TOP_K):
                    scatter(c, k, buf, ssem).start()

        for k in range(TOP_K):
            scatter(nch - 1, k, buf1, ssem1).wait()

    run = pl.kernel(
        body,
        out_type=jax.ShapeDtypeStruct((n_slots, HALF), jnp.uint32),
        mesh=_sc_mesh(),
        scratch_types=[pltpu.VMEM((nch * TOP_K, SC_ROWS), jnp.int32),
                       pltpu.VMEM((SC_ROWS, HALF), jnp.uint32),
                       pltpu.VMEM((SC_ROWS, HALF), jnp.uint32),
                       pltpu.SemaphoreType.DMA, pltpu.SemaphoreType.DMA,
                       pltpu.SemaphoreType.DMA, pltpu.SemaphoreType.DMA],
        name="sc_dispatch",
    )
    return run(h2p, idx)


def _collect(ys, dest):
    n = dest.size
    per_w = n // SC_WORKERS
    nch = per_w // SC_ROWS
    idx = dest.reshape(SC_WORKERS, nch, SC_ROWS)

    def body(ys_hbm, idx_hbm, out_hbm, idx_v, buf0, buf1, gsem0, gsem1, wsem0, wsem1):
        wid = _worker_id()
        base = wid * per_w
        pltpu.sync_copy(idx_hbm.at[wid], idx_v)
        bufs = ((buf0, gsem0, wsem0), (buf1, gsem1, wsem1))

        def gather(c, buf, gsem):
            return pltpu.make_async_copy(ys_hbm.at[idx_v.at[c]], buf, gsem)

        def write(c, buf, wsem):
            return pltpu.make_async_copy(buf, out_hbm.at[pl.ds(base + c * SC_ROWS, SC_ROWS)], wsem)

        gather(0, buf0, gsem0).start()

        @pl.loop(0, nch, step=2)
        def _(c0):
            for b in range(2):
                c = c0 + b
                buf, gsem, wsem = bufs[b]
                obuf, ogsem, owsem = bufs[1 - b]
                gather(c, buf, gsem).wait()

                @pl.when(c > 0)
                def _():
                    write(c - 1, obuf, owsem).wait()

                @pl.when(c + 1 < nch)
                def _():
                    gather(c + 1, obuf, ogsem).start()

                write(c, buf, wsem).start()

        write(nch - 1, buf1, wsem1).wait()

    run = pl.kernel(
        body,
        out_type=jax.ShapeDtypeStruct((n, HALF), jnp.uint32),
        mesh=_sc_mesh(),
        scratch_types=[pltpu.VMEM((nch, SC_ROWS), jnp.int32),
                       pltpu.VMEM((SC_ROWS, HALF), jnp.uint32),
                       pltpu.VMEM((SC_ROWS, HALF), jnp.uint32),
                       pltpu.SemaphoreType.DMA, pltpu.SemaphoreType.DMA,
                       pltpu.SemaphoreType.DMA, pltpu.SemaphoreType.DMA],
        name="sc_collect",
    )
    return run(ys, idx).reshape(dest.shape + (HALF,))


def kernel(x, c, positions, w_ada, b_ada, g_pre_mix, g_post_mix, g_pre_ffn, g_post_ffn, w_in, conv_w, conv_b, b_gates, g_mlstm, w_branch_a, w_branch_b, w_out, router_w, router_bias, w_exp_gate, w_exp_up, w_exp_down, w_sh_gate, w_sh_up, w_sh_down):
    depth = w_ada.shape[0]
    for l in range(depth):
        x = _layer(x, c, positions, w_ada[l], b_ada[l], g_pre_mix[l], g_post_mix[l], g_pre_ffn[l],
                   g_post_ffn[l], w_in[l], conv_w[l], conv_b[l], b_gates[l], g_mlstm[l],
                   w_branch_a[l], w_branch_b[l], w_out[l], router_w[l], router_bias[l],
                   w_exp_gate[l], w_exp_up[l], w_exp_down[l], w_sh_gate[l], w_sh_up[l], w_sh_down[l])
    return x
```

```python
import functools

import jax
import jax.numpy as jnp
from jax import lax
from jax.experimental import pallas as pl
from jax.experimental.pallas import tpu as pltpu
from jax.experimental.pallas import tpu_sc as plsc

F32 = jnp.float32
BF16 = jnp.bfloat16
HIGHEST = lax.Precision.HIGHEST

D_MODEL = 1024
ATT_GROUPS = ((128, 1), (512, 4), (2048, 16))
ATT_HEAD_DIM = 64
ATT_GROUP_W = 256
ATT_BLK = 128
ROPE_THETA = 500000.0
ROPE_HALF = 8
MLSTM_HEADS = 4
MLSTM_QK_DIM = 128
MLSTM_V_DIM = 256
MLSTM_CHUNK = 64
CONV_WIDTH = 4
N_EXPERTS = 256
TOP_K = 8
N_GROUPS = 8
TOPK_GROUPS = 4
EXPERT_FF = 256
ROUTED_SCALE = 2.5
NORM_EPS = 1e-6
NEG = -1e30

OFF_MV, OFF_MO, OFF_GA, OFF_GB = 0, 1024, 2048, 3072
OFF_MQ, OFF_MK = 4096, 4608
OFF_AQ, OFF_AK, OFF_AV = 5120, 5888, 6656
PROJ_W = 7424
HALF = D_MODEL // 2

EXPERT_BLOCK = 256
VMEM_LIMIT = 56 * 1024 * 1024


def _nt(a, b, precision=None):
    return lax.dot_general(a, b, (((1,), (1,)), ((), ())), preferred_element_type=F32,
                           precision=precision)


def _tn(a, b):
    return lax.dot_general(a, b, (((0,), (0,)), ((), ())), preferred_element_type=F32)


def _silu(x):
    return x * jax.nn.sigmoid(x)


def _pack_pair(lo, hi):
    lo_b = pltpu.bitcast(lo.astype(BF16).astype(F32), jnp.uint32)
    hi_b = pltpu.bitcast(hi.astype(BF16).astype(F32), jnp.uint32)
    return (lo_b >> 16) | (hi_b & jnp.uint32(0xFFFF0000))


def _unpack_pair(w):
    lo = pltpu.bitcast(w << 16, F32)
    hi = pltpu.bitcast(w & jnp.uint32(0xFFFF0000), F32)
    return lo, hi


def _mod_kernel(c_ref, w_ref, b_ref, o_ref):
    a = _silu(c_ref[...])
    o_ref[...] = jnp.dot(a, w_ref[...], preferred_element_type=F32, precision=HIGHEST) + b_ref[...]


def _adaln(c, w_ada, b_ada):
    B = c.shape[0]
    n = w_ada.shape[1]
    tn = 512
    return pl.pallas_call(
        _mod_kernel,
        grid=(n // tn,),
        in_specs=[pl.BlockSpec((B, D_MODEL), lambda j: (0, 0)),
                  pl.BlockSpec((D_MODEL, tn), lambda j: (0, j)),
                  pl.BlockSpec((1, tn), lambda j: (0, j))],
        out_specs=pl.BlockSpec((B, tn), lambda j: (0, j)),
        out_shape=jax.ShapeDtypeStruct((B, n), F32),
        name="adaln_mod",
    )(c, w_ada, b_ada.reshape(1, n))


def _proj_kernel(x_ref, mod_ref, g_ref, w_ref, wif_ref, o_ref, gates_ref, h_ref):
    @pl.when(pl.program_id(1) == 0)
    def _():
        x = x_ref[...]
        ms = jnp.mean(x * x, axis=-1, keepdims=True)
        y = x * lax.rsqrt(ms + NORM_EPS) * g_ref[...]
        h = (y * (1.0 + mod_ref[0, 1:2, :]) + mod_ref[0, 0:1, :]).astype(BF16)
        h_ref[...] = h
        gates_ref[...] = jnp.dot(h, wif_ref[...], preferred_element_type=F32)

    o_ref[...] = jnp.dot(h_ref[...], w_ref[...], preferred_element_type=F32).astype(BF16)


def _in_proj(x2, mod3, g_pre, w_main, w_if, seq):
    T = x2.shape[0]
    tm, tn = 1024, 256
    per_b = seq // tm
    return pl.pallas_call(
        _proj_kernel,
        grid=(T // tm, PROJ_W // tn),
        in_specs=[pl.BlockSpec((tm, D_MODEL), lambda i, j: (i, 0)),
                  pl.BlockSpec((1, 6, D_MODEL), lambda i, j: (i // per_b, 0, 0)),
                  pl.BlockSpec((1, D_MODEL), lambda i, j: (0, 0)),
                  pl.BlockSpec((D_MODEL, tn), lambda i, j: (0, j)),
                  pl.BlockSpec((D_MODEL, 128), lambda i, j: (0, 0))],
        out_specs=[pl.BlockSpec((tm, tn), lambda i, j: (i, j)),
                   pl.BlockSpec((tm, 128), lambda i, j: (i, 0))],
        out_shape=[jax.ShapeDtypeStruct((T, PROJ_W), BF16),
                   jax.ShapeDtypeStruct((T, 128), F32)],
        scratch_shapes=[pltpu.VMEM((tm, D_MODEL), BF16)],
        compiler_params=pltpu.CompilerParams(
            dimension_semantics=("arbitrary", "arbitrary"), vmem_limit_bytes=VMEM_LIMIT),
        name="norm_in_proj",
    )(x2, mod3, g_pre, w_main, w_if)


def _attn_kernel(q_ref, k_ref, v_ref, cs_ref, sn_ref, o_ref, qf, kf, vf, acc, m_s, l_s, *, seq):
    g = pl.program_id(1)
    lane = lax.broadcasted_iota(jnp.int32, (ATT_BLK, 128), 1)
    first = (lane % ATT_HEAD_DIM) < ROPE_HALF
    low_head = lane < ATT_HEAD_DIM

    def rope(x, cs, sn):
        partner = jnp.where(first, pltpu.roll(x, 128 - ROPE_HALF, 1), pltpu.roll(x, ROPE_HALF, 1))
        return x * cs + partner * sn

    def zero_pad(i, _):
        rows = pl.ds(pl.multiple_of(i * ATT_BLK, ATT_BLK), ATT_BLK)
        for hp in range(2):
            kf[hp, rows, :] = jnp.zeros((ATT_BLK, 128), F32)
            vf[hp, rows, :] = jnp.zeros((ATT_BLK, 128), F32)
        return 0

    lax.fori_loop(0, seq // ATT_BLK, zero_pad, 0)

    def stage(i, _):
        r = pl.multiple_of(i * ATT_BLK, ATT_BLK)
        rows = pl.ds(r, ATT_BLK)
        prow = pl.ds(pl.multiple_of(seq + i * ATT_BLK, ATT_BLK), ATT_BLK)
        cs = cs_ref[0, rows, :]
        sn = sn_ref[0, rows, :]
        for hp in range(2):
            cols = pl.ds(hp * 128, 128)
            qf[hp, rows, :] = rope(q_ref[0, rows, cols].astype(F32), cs, sn) * (ATT_HEAD_DIM ** -0.5)
            kf[hp, prow, :] = rope(k_ref[0, rows, cols].astype(F32), cs, sn)
            vf[hp, prow, :] = v_ref[0, rows, cols].astype(F32)
        return 0

    lax.fori_loop(0, seq // ATT_BLK, stage, 0)

    qi = lax.broadcasted_iota(jnp.int32, (ATT_BLK, 2 * ATT_BLK), 0)
    ki = lax.broadcasted_iota(jnp.int32, (ATT_BLK, 2 * ATT_BLK), 1)
    band = (ki >= qi) & (ki <= qi + ATT_BLK)

    def process(d, init):
        span = ATT_BLK * d

        def body(c, _):
            rho = c % d
            n = c // d
            qstart = rho + n * span
            kstart = seq + qstart - span
            first_key = jnp.where(n > 0, 0, ATT_BLK)
            valid = band & (ki >= first_key)
            qrows = pl.ds(qstart, ATT_BLK, stride=d) if d > 1 else pl.ds(qstart, ATT_BLK)
            krows = pl.ds(kstart, 2 * ATT_BLK, stride=d) if d > 1 else pl.ds(kstart, 2 * ATT_BLK)
            for hp in range(2):
                q2 = qf[hp, qrows, :]
                k2 = kf[hp, krows, :].astype(BF16)
                v2 = vf[hp, krows, :].astype(BF16)
                res = []
                for hh in range(2):
                    hm = low_head if hh == 0 else jnp.logical_not(low_head)
                    qh = jnp.where(hm, q2, 0.0).astype(BF16)
                    s = jnp.where(valid, _nt(qh, k2), NEG)
                    m = jnp.max(s, axis=1, keepdims=True)
                    p = jnp.exp(s - m)
                    l = jnp.sum(p, axis=1, keepdims=True)
                    o = jnp.dot(p.astype(BF16), v2, preferred_element_type=F32)
                    res.append((o, m, l))
                o_b = jnp.where(low_head, res[0][0], res[1][0])
                m_b = jnp.where(low_head, res[0][1], res[1][1])
                l_b = jnp.where(low_head, res[0][2], res[1][2])
                if init:
                    acc[hp, qrows, :] = o_b
                    m_s[hp, qrows, :] = m_b
                    l_s[hp, qrows, :] = l_b
                else:
                    m_old = m_s[hp, qrows, :]
                    m_new = jnp.maximum(m_old, m_b)
                    a_old = jnp.exp(m_old - m_new)
                    a_new = jnp.exp(m_b - m_new)
                    acc[hp, qrows, :] = acc[hp, qrows, :] * a_old + o_b * a_new
                    l_s[hp, qrows, :] = l_s[hp, qrows, :] * a_old + l_b * a_new
                    m_s[hp, qrows, :] = m_new
            return 0

        lax.fori_loop(0, seq // ATT_BLK, body, 0)

    for gi, (_, d) in enumerate(ATT_GROUPS):
        @pl.when(g == gi)
        def _(d=d, gi=gi):
            process(d, gi == 0)

    @pl.when(g == len(ATT_GROUPS) - 1)
    def _():
        def fin(i, _):
            rows = pl.ds(pl.multiple_of(i * ATT_BLK, ATT_BLK), ATT_BLK)
            for hp in range(2):
                o_ref[0, rows, pl.ds(hp * 128, 128)] = (acc[hp, rows, :] / l_s[hp, rows, :]).astype(BF16)
            return 0

        lax.fori_loop(0, seq // ATT_BLK, fin, 0)


def _attention(proj3, cs, sn):
    B, S, _ = proj3.shape
    ng = len(ATT_GROUPS)
    qb, kb, vb = OFF_AQ // ATT_GROUP_W, OFF_AK // ATT_GROUP_W, OFF_AV // ATT_GROUP_W
    return pl.pallas_call(
        functools.partial(_attn_kernel, seq=S),
        grid=(B, ng),
        in_specs=[pl.BlockSpec((1, S, ATT_GROUP_W), lambda b, g: (b, 0, qb + g)),
                  pl.BlockSpec((1, S, ATT_GROUP_W), lambda b, g: (b, 0, kb + g)),
                  pl.BlockSpec((1, S, ATT_GROUP_W), lambda b, g: (b, 0, vb + g)),
                  pl.BlockSpec((1, S, 128), lambda b, g: (b, 0, 0)),
                  pl.BlockSpec((1, S, 128), lambda b, g: (b, 0, 0))],
        out_specs=pl.BlockSpec((1, S, ATT_GROUP_W), lambda b, g: (b, 0, 0)),
        out_shape=jax.ShapeDtypeStruct((B, S, ATT_GROUP_W), BF16),
        scratch_shapes=[pltpu.VMEM((2, S, 128), F32),
                        pltpu.VMEM((2, 2 * S, 128), F32),
                        pltpu.VMEM((2, 2 * S, 128), F32),
                        pltpu.VMEM((2, S, 128), F32),
                        pltpu.VMEM((2, S, 128), F32),
                        pltpu.VMEM((2, S, 128), F32)],
        compiler_params=pltpu.CompilerParams(
            dimension_semantics=("arbitrary", "arbitrary"), vmem_limit_bytes=VMEM_LIMIT),
        name="dilated_attention",
    )(proj3, proj3, proj3, cs, sn)


def _log_sigmoid(x):
    return jnp.minimum(x, 0.0) - jnp.log(1.0 + jnp.exp(-jnp.abs(x)))


def _mlstm_kernel(mq_ref, mk_ref, mv_ref, mo_ref, gates_ref, cwq_ref, cwk_ref, cbq_ref, cbk_ref,
                  bg_ref, gm_ref, o_ref, pad_ref, q_s, k_s, c_ref, *, seq):
    h = pl.program_id(1)
    L = MLSTM_CHUNK
    DK = MLSTM_QK_DIM
    halo = 8

    pad_ref[0:halo, :] = jnp.zeros((halo, 2 * DK), F32)
    blk = 128
    for i in range(seq // blk):
        pad_ref[halo + i * blk:halo + (i + 1) * blk, 0:DK] = mq_ref[0, i * blk:(i + 1) * blk, :].astype(F32)
        pad_ref[halo + i * blk:halo + (i + 1) * blk, DK:2 * DK] = mk_ref[0, i * blk:(i + 1) * blk, :].astype(F32)
    for i in range(seq // blk):
        yq = jnp.broadcast_to(cbq_ref[...], (blk, DK))
        yk = jnp.broadcast_to(cbk_ref[...], (blk, DK))
        for j in range(CONV_WIDTH):
            r0 = halo + i * blk - (CONV_WIDTH - 1) + j
            yq = yq + pad_ref[r0:r0 + blk, 0:DK] * cwq_ref[j:j + 1, :]
            yk = yk + pad_ref[r0:r0 + blk, DK:2 * DK] * cwk_ref[j:j + 1, :]
        q_s[i * blk:(i + 1) * blk, :] = _silu(yq).astype(BF16)
        k_s[i * blk:(i + 1) * blk, :] = (_silu(yk) * (DK ** -0.5)).astype(BF16)

    lane = lax.broadcasted_iota(jnp.int32, (1, 128), 1)
    sel_i = (lane == h).astype(F32)
    sel_f = (lane == h + MLSTM_HEADS).astype(F32)
    r8 = lax.broadcasted_iota(jnp.int32, (8, 128), 0)
    l8 = lax.broadcasted_iota(jnp.int32, (8, 128), 1)
    selmat = jnp.where(((r8 == 0) & (l8 == h)) | ((r8 == 1) & (l8 == h + MLSTM_HEADS)), 1.0, 0.0)
    ri = lax.broadcasted_iota(jnp.int32, (L, L), 0)
    ci = lax.broadcasted_iota(jnp.int32, (L, L), 1)
    causal = ci <= ri
    tri = causal.astype(F32)
    tri_t = (ri <= ci).astype(F32)
    bias_row = bg_ref[...]
    g_row = gm_ref[...]

    c_ref[...] = jnp.zeros((DK, MLSTM_V_DIM), F32)

    def chunk(c, carry):
        n_row, m_prev = carry
        rows = pl.ds(pl.multiple_of(c * L, L), L)
        q = q_s[rows, :]
        k = k_s[rows, :]
        v = mv_ref[0, rows, :]
        gch = gates_ref[0, rows, :] + bias_row
        i_col = jnp.sum(gch * sel_i, axis=1, keepdims=True)
        f_col = jnp.sum(gch * sel_f, axis=1, keepdims=True)
        rows_b = _nt(selmat, gch, precision=HIGHEST)
        i_row = rows_b[0:1, :]
        lf_row = _log_sigmoid(rows_b[1:2, :])
        lf_col = _log_sigmoid(f_col)
        b_col_b = jnp.dot(tri, jnp.broadcast_to(lf_col, (L, L)), preferred_element_type=F32,
                          precision=HIGHEST)
        b_row_b = jnp.dot(jnp.broadcast_to(lf_row, (L, L)), tri_t, preferred_element_type=F32,
                          precision=HIGHEST)
        b_col = b_col_b[:, 0:1]
        dmat = jnp.where(causal, b_col_b - b_row_b + i_row, NEG)
        m_t = jnp.maximum(b_col + m_prev, jnp.max(dmat, axis=1, keepdims=True))
        wts = jnp.exp(dmat - m_t)
        sc = _nt(q, k) * wts
        inter = jnp.exp(b_col + m_prev - m_t)
        c_old = c_ref[...]
        num = (jnp.dot(sc.astype(BF16), v, preferred_element_type=F32)
               + inter * jnp.dot(q, c_old.astype(BF16), preferred_element_type=F32))
        den = (jnp.sum(sc, axis=1, keepdims=True)
               + inter * jnp.sum(q.astype(F32) * n_row, axis=1, keepdims=True))
        hh = num / jnp.maximum(jnp.abs(den), jnp.exp(-m_t))
        ms = jnp.mean(hh * hh, axis=1, keepdims=True)
        hn = hh * lax.rsqrt(ms + NORM_EPS) * g_row
        o_ref[0, rows, :] = (hn * jax.nn.sigmoid(mo_ref[0, rows, :].astype(F32))).astype(BF16)

        b_end = b_col_b[L - 1:L, 0:1]
        g_col = b_end - b_col + i_col
        m_new = jnp.maximum(b_end + m_prev, jnp.max(g_col, axis=0, keepdims=True))
        decay = jnp.exp(b_end + m_prev - m_new)
        wk = jnp.exp(g_col - m_new) * k.astype(F32)
        c_ref[...] = decay * c_old + _tn(wk.astype(BF16), v)
        n_new = decay * n_row + jnp.sum(wk, axis=0, keepdims=True)
        return n_new, m_new

    lax.fori_loop(0, seq // L, chunk, (jnp.zeros((1, DK), F32), jnp.zeros((1, 1), F32)), unroll=4)


def _mlstm(proj3, gates3, conv_w, conv_b, bg_row, g_mlstm):
    B, S, _ = proj3.shape
    H, DK, DV = MLSTM_HEADS, MLSTM_QK_DIM, MLSTM_V_DIM
    qb, kb = OFF_MQ // DK, OFF_MK // DK
    vb, ob = OFF_MV // DV, OFF_MO // DV
    nq = (H * DK) // DK
    return pl.pallas_call(
        functools.partial(_mlstm_kernel, seq=S),
        grid=(B, H),
        in_specs=[pl.BlockSpec((1, S, DK), lambda b, h: (b, 0, qb + h)),
                  pl.BlockSpec((1, S, DK), lambda b, h: (b, 0, kb + h)),
                  pl.BlockSpec((1, S, DV), lambda b, h: (b, 0, vb + h)),
                  pl.BlockSpec((1, S, DV), lambda b, h: (b, 0, ob + h)),
                  pl.BlockSpec((1, S, 128), lambda b, h: (b, 0, 0)),
                  pl.BlockSpec((CONV_WIDTH, DK), lambda b, h: (0, h)),
                  pl.BlockSpec((CONV_WIDTH, DK), lambda b, h: (0, nq + h)),
                  pl.BlockSpec((1, DK), lambda b, h: (0, h)),
                  pl.BlockSpec((1, DK), lambda b, h: (0, nq + h)),
                  pl.BlockSpec((1, 128), lambda b, h: (0, 0)),
                  pl.BlockSpec((1, DV), lambda b, h: (0, h))],
        out_specs=pl.BlockSpec((1, S, DV), lambda b, h: (b, 0, h)),
        out_shape=jax.ShapeDtypeStruct((B, S, H * DV), BF16),
        scratch_shapes=[pltpu.VMEM((S + 8, 2 * DK), F32),
                        pltpu.VMEM((S, DK), BF16),
                        pltpu.VMEM((S, DK), BF16),
                        pltpu.VMEM((DK, DV), F32)],
        compiler_params=pltpu.CompilerParams(
            dimension_semantics=("arbitrary", "arbitrary"), vmem_limit_bytes=VMEM_LIMIT),
        name="mlstm_chunkwise",
    )(proj3, proj3, proj3, proj3, gates3, conv_w, conv_w, conv_b, conv_b, bg_row, g_mlstm)


def _rms(y, g):
    ms = jnp.mean(y * y, axis=-1, keepdims=True)
    return y * lax.rsqrt(ms + NORM_EPS) * g


def _merge_kernel(ya_ref, yb_ref, ga_ref, gb_ref, x_ref, mod_ref, wa_ref, wb_ref, wo_ref,
                  gpost_ref, gpre_ref, x1_ref, h2_ref):
    pa = jnp.dot(ya_ref[...], wa_ref[...], preferred_element_type=F32)
    pb = jnp.dot(yb_ref[...], wb_ref[...], preferred_element_type=F32)
    merged = (jax.nn.sigmoid(ga_ref[...].astype(F32)) * pa
              + jax.nn.sigmoid(gb_ref[...].astype(F32)) * pb)
    y = jnp.dot(merged.astype(BF16), wo_ref[...], preferred_element_type=F32)
    x1 = x_ref[...] + mod_ref[0, 2:3, :] * _rms(y, gpost_ref[...])
    x1_ref[...] = x1
    h2 = _rms(x1, gpre_ref[...]) * (1.0 + mod_ref[0, 4:5, :]) + mod_ref[0, 3:4, :]
    h2_ref[...] = _pack_pair(h2[:, :HALF], h2[:, HALF:])


def _merge(ya2, yb2, proj2, x2, mod3, wa, wb, wo, g_post, g_pre, seq):
    T = x2.shape[0]
    tm = 512
    per_b = seq // tm
    full = lambda shape: pl.BlockSpec(shape, lambda i: (0,) * len(shape))
    return pl.pallas_call(
        _merge_kernel,
        grid=(T // tm,),
        in_specs=[pl.BlockSpec((tm, ATT_GROUP_W), lambda i: (i, 0)),
                  pl.BlockSpec((tm, D_MODEL), lambda i: (i, 0)),
                  pl.BlockSpec((tm, D_MODEL), lambda i: (i, OFF_GA // D_MODEL)),
                  pl.BlockSpec((tm, D_MODEL), lambda i: (i, OFF_GB // D_MODEL)),
                  pl.BlockSpec((tm, D_MODEL), lambda i: (i, 0)),
                  pl.BlockSpec((1, 6, D_MODEL), lambda i: (i // per_b, 0, 0)),
                  full((ATT_GROUP_W, D_MODEL)), full((D_MODEL, D_MODEL)), full((D_MODEL, D_MODEL)),
                  full((1, D_MODEL)), full((1, D_MODEL))],
        out_specs=[pl.BlockSpec((tm, D_MODEL), lambda i: (i, 0)),
                   pl.BlockSpec((tm, HALF), lambda i: (i, 0))],
        out_shape=[jax.ShapeDtypeStruct((T, D_MODEL), F32),
                   jax.ShapeDtypeStruct((T, HALF), jnp.uint32)],
        compiler_params=pltpu.CompilerParams(
            dimension_semantics=("arbitrary",), vmem_limit_bytes=VMEM_LIMIT),
        name="merge_out_proj",
    )(ya2, yb2, proj2, proj2, x2, mod3, wa, wb, wo, g_post, g_pre)


def _router_kernel(h2_ref, rlo_ref, rhi_ref, bias_ref, idx_ref, w_ref, rank_ref, cnt_ref):
    E = N_EXPERTS
    tr = h2_ref.shape[0]
    gsz = E // N_GROUPS

    @pl.when(pl.program_id(0) == 0)
    def _():
        cnt_ref[...] = jnp.zeros(cnt_ref.shape, F32)

    lo, hi = _unpack_pair(h2_ref[...])
    logits = _nt(rlo_ref[...], lo.astype(BF16)) + _nt(rhi_ref[...], hi.astype(BF16))
    scores = jax.nn.sigmoid(logits)
    sel = scores + bias_ref[:, 0:1]

    gi = lax.broadcasted_iota(jnp.int32, (gsz, tr), 0).astype(F32)
    gs_rows = []
    for g in range(N_GROUPS):
        blk = sel[g * gsz:(g + 1) * gsz, :]
        m1 = jnp.max(blk, axis=0, keepdims=True)
        a1 = jnp.min(jnp.where(blk == m1, gi, float(E)), axis=0, keepdims=True)
        m2 = jnp.max(jnp.where(gi == a1, -jnp.inf, blk), axis=0, keepdims=True)
        gs_rows.append(m1 + m2)
    gs = jnp.concatenate(gs_rows, axis=0)
    g8 = lax.broadcasted_iota(jnp.int32, (N_GROUPS, tr), 0).astype(F32)
    gmask = jnp.zeros((N_GROUPS, tr), F32)
    for _ in range(TOPK_GROUPS):
        m = jnp.max(gs, axis=0, keepdims=True)
        a = jnp.min(jnp.where(gs == m, g8, float(E)), axis=0, keepdims=True)
        hit = g8 == a
        gmask = jnp.where(hit, 1.0, gmask)
        gs = jnp.where(hit, -jnp.inf, gs)
    selm = jnp.concatenate(
        [jnp.where(gmask[g:g + 1, :] > 0.0, sel[g * gsz:(g + 1) * gsz, :], -jnp.inf)
         for g in range(N_GROUPS)], axis=0)

    ei = lax.broadcasted_iota(jnp.int32, (E, tr), 0).astype(F32)
    picks, weights = [], []
    chosen = jnp.zeros((E, tr), F32)
    for _ in range(TOP_K):
        m = jnp.max(selm, axis=0, keepdims=True)
        a = jnp.min(jnp.where(selm == m, ei, float(E)), axis=0, keepdims=True)
        hit = ei == a
        picks.append(a)
        weights.append(jnp.sum(jnp.where(hit, scores, 0.0), axis=0, keepdims=True))
        chosen = jnp.where(hit, 1.0, chosen)
        selm = jnp.where(hit, -jnp.inf, selm)
    wsum = weights[0]
    for w in weights[1:]:
        wsum = wsum + w

    ti = lax.broadcasted_iota(jnp.int32, (tr, tr), 0)
    tj = lax.broadcasted_iota(jnp.int32, (tr, tr), 1)
    before = (ti < tj).astype(BF16)
    pos = jnp.dot(chosen.astype(BF16), before, preferred_element_type=F32) + cnt_ref[:, 0:1]
    ranks = [jnp.sum(jnp.where(ei == a, pos, 0.0), axis=0, keepdims=True) for a in picks]
    cnt_ref[...] = cnt_ref[...] + jnp.sum(chosen, axis=1, keepdims=True)

    idx_ref[...] = jnp.concatenate(picks, axis=0).astype(jnp.int32)
    w_ref[...] = jnp.concatenate([w / wsum * ROUTED_SCALE for w in weights], axis=0)
    rank_ref[...] = jnp.concatenate(ranks, axis=0).astype(jnp.int32)


def _router(h2p, r_lo, r_hi, bias_col):
    T = h2p.shape[0]
    tr = 512
    full = lambda shape: pl.BlockSpec(shape, lambda i: (0,) * len(shape))
    return pl.pallas_call(
        _router_kernel,
        grid=(T // tr,),
        in_specs=[pl.BlockSpec((tr, HALF), lambda i: (i, 0)),
                  full((N_EXPERTS, HALF)), full((N_EXPERTS, HALF)), full((N_EXPERTS, 128))],
        out_specs=[pl.BlockSpec((TOP_K, tr), lambda i: (0, i)),
                   pl.BlockSpec((TOP_K, tr), lambda i: (0, i)),
                   pl.BlockSpec((TOP_K, tr), lambda i: (0, i)),
                   full((N_EXPERTS, 128))],
        out_shape=[jax.ShapeDtypeStruct((TOP_K, T), jnp.int32),
                   jax.ShapeDtypeStruct((TOP_K, T), F32),
                   jax.ShapeDtypeStruct((TOP_K, T), jnp.int32),
                   jax.ShapeDtypeStruct((N_EXPERTS, 128), F32)],
        compiler_params=pltpu.CompilerParams(
            dimension_semantics=("arbitrary",), vmem_limit_bytes=VMEM_LIMIT),
        name="router_topk",
    )(h2p, r_lo, r_hi, bias_col)


def _dest_kernel(idx_ref, rank_ref, pstart_ref, dest_ref):
    tr = idx_ref.shape[1]
    ei = lax.broadcasted_iota(jnp.int32, (N_EXPERTS, tr), 0)
    start = pstart_ref[:, 0:1]
    rows = []
    for k in range(TOP_K):
        hit = ei == idx_ref[k:k + 1, :]
        rows.append(jnp.sum(jnp.where(hit, start, 0.0), axis=0, keepdims=True))
    dest_ref[...] = jnp.concatenate(rows, axis=0).astype(jnp.int32) + rank_ref[...]


def _slot_index(idx, rank, pstart_col):
    T = idx.shape[1]
    tr = 1024
    return pl.pallas_call(
        _dest_kernel,
        grid=(T // tr,),
        in_specs=[pl.BlockSpec((TOP_K, tr), lambda i: (0, i)),
                  pl.BlockSpec((TOP_K, tr), lambda i: (0, i)),
                  pl.BlockSpec((N_EXPERTS, 128), lambda i: (0, 0))],
        out_specs=pl.BlockSpec((TOP_K, tr), lambda i: (0, i)),
        out_shape=jax.ShapeDtypeStruct((TOP_K, T), jnp.int32),
        name="slot_index",
    )(idx, rank, pstart_col)


def _ffn_kernel(blk_e_ref, nused_ref, x_ref, wg_ref, wu_ref, wd_ref, y_ref, wg_s, wu_s, wd_s):
    i = pl.program_id(0)

    @pl.when(i < nused_ref[0])
    def _():
        e = blk_e_ref[i]
        prev = blk_e_ref[jnp.maximum(i - 1, 0)]

        @pl.when((i == 0) | (e != prev))
        def _():
            wg_s[...] = wg_ref[0].astype(BF16)
            wu_s[...] = wu_ref[0].astype(BF16)
            wd_s[...] = wd_ref[0].astype(BF16)

        lo, hi = _unpack_pair(x_ref[...])
        lo = lo.astype(BF16)
        hi = hi.astype(BF16)
        gate = (jnp.dot(lo, wg_s[0:HALF, :], preferred_element_type=F32)
                + jnp.dot(hi, wg_s[HALF:, :], preferred_element_type=F32))
        up = (jnp.dot(lo, wu_s[0:HALF, :], preferred_element_type=F32)
              + jnp.dot(hi, wu_s[HALF:, :], preferred_element_type=F32))
        hid = (_silu(gate) * up).astype(BF16)
        out = jnp.dot(hid, wd_s[...], preferred_element_type=F32)
        y_ref[...] = _pack_pair(out[:, :HALF], out[:, HALF:])


def _expert_ffn(blk_e, nused, xs, w_gate, w_up, w_down):
    P = xs.shape[0]
    bm = EXPERT_BLOCK
    nb = P // bm

    def row_map(i, blk_e_ref, nused_ref):
        return (jnp.minimum(i, nused_ref[0] - 1), 0)

    def w_map(i, blk_e_ref, nused_ref):
        return (blk_e_ref[jnp.minimum(i, nused_ref[0] - 1)], 0, 0)

    grid_spec = pltpu.PrefetchScalarGridSpec(
        num_scalar_prefetch=2,
        grid=(nb,),
        in_specs=[pl.BlockSpec((bm, HALF), row_map),
                  pl.BlockSpec((1, D_MODEL, EXPERT_FF), w_map),
                  pl.BlockSpec((1, D_MODEL, EXPERT_FF), w_map),
                  pl.BlockSpec((1, EXPERT_FF, D_MODEL), w_map)],
        out_specs=pl.BlockSpec((bm, HALF), row_map),
        scratch_shapes=[pltpu.VMEM((D_MODEL, EXPERT_FF), BF16),
                        pltpu.VMEM((D_MODEL, EXPERT_FF), BF16),
                        pltpu.VMEM((EXPERT_FF, D_MODEL), BF16)],
    )
    return pl.pallas_call(
        _ffn_kernel,
        grid_spec=grid_spec,
        out_shape=jax.ShapeDtypeStruct((P, HALF), jnp.uint32),
        compiler_params=pltpu.CompilerParams(
            dimension_semantics=("arbitrary",), vmem_limit_bytes=VMEM_LIMIT),
        name="routed_experts",
    )(blk_e, nused, xs, w_gate, w_up, w_down)


def _final_kernel(yg_ref, w_ref, h2_ref, x1_ref, mod_ref, wsg_ref, wsu_ref, wsd_ref, gpost_ref, o_ref):
    lo, hi = _unpack_pair(h2_ref[...])
    lo = lo.astype(BF16)
    hi = hi.astype(BF16)
    gate = (jnp.dot(lo, wsg_ref[0:HALF, :], preferred_element_type=F32)
            + jnp.dot(hi, wsg_ref[HALF:, :], preferred_element_type=F32))
    up = (jnp.dot(lo, wsu_ref[0:HALF, :], preferred_element_type=F32)
          + jnp.dot(hi, wsu_ref[HALF:, :], preferred_element_type=F32))
    shared = jnp.dot((_silu(gate) * up).astype(BF16), wsd_ref[...], preferred_element_type=F32)
    y_lo = shared[:, :HALF]
    y_hi = shared[:, HALF:]
    for k in range(TOP_K):
        r_lo, r_hi = _unpack_pair(yg_ref[k])
        wk = w_ref[:, k:k + 1]
        y_lo = y_lo + wk * r_lo
        y_hi = y_hi + wk * r_hi
    ms = (jnp.sum(y_lo * y_lo, axis=-1, keepdims=True)
          + jnp.sum(y_hi * y_hi, axis=-1, keepdims=True)) * (1.0 / D_MODEL)
    inv = lax.rsqrt(ms + NORM_EPS)
    o_ref[:, 0:HALF] = x1_ref[:, 0:HALF] + mod_ref[0, 5:6, 0:HALF] * (y_lo * inv * gpost_ref[:, 0:HALF])
    o_ref[:, HALF:] = x1_ref[:, HALF:] + mod_ref[0, 5:6, HALF:] * (y_hi * inv * gpost_ref[:, HALF:])


def _final(yg, w_tk, h2p, x1, mod3, wsg, wsu, wsd, g_post, seq):
    T = x1.shape[0]
    tm = 256
    per_b = seq // tm
    full = lambda shape: pl.BlockSpec(shape, lambda i: (0,) * len(shape))
    return pl.pallas_call(
        _final_kernel,
        grid=(T // tm,),
        in_specs=[pl.BlockSpec((TOP_K, tm, HALF), lambda i: (0, i, 0)),
                  pl.BlockSpec((tm, TOP_K), lambda i: (i, 0)),
                  pl.BlockSpec((tm, HALF), lambda i: (i, 0)),
                  pl.BlockSpec((tm, D_MODEL), lambda i: (i, 0)),
                  pl.BlockSpec((1, 6, D_MODEL), lambda i: (i // per_b, 0, 0)),
                  full((D_MODEL, EXPERT_FF)), full((D_MODEL, EXPERT_FF)), full((EXPERT_FF, D_MODEL)),
                  full((1, D_MODEL))],
        out_specs=pl.BlockSpec((tm, D_MODEL), lambda i: (i, 0)),
        out_shape=jax.ShapeDtypeStruct((T, D_MODEL), F32),
        compiler_params=pltpu.CompilerParams(
            dimension_semantics=("arbitrary",), vmem_limit_bytes=VMEM_LIMIT),
        name="shared_expert_combine",
    )(yg, w_tk, h2p, x1, mod3, wsg, wsu, wsd, g_post)


def _rope_tables(positions):
    inv = jnp.power(ROPE_THETA, -jnp.arange(ROPE_HALF, dtype=F32) / ROPE_HALF)
    ang = positions.astype(F32)[..., None] * inv
    cos, sin = jnp.cos(ang), jnp.sin(ang)
    rest = ATT_HEAD_DIM - 2 * ROPE_HALF
    cs = jnp.concatenate([cos, cos, jnp.ones(ang.shape[:-1] + (rest,), F32)], axis=-1)
    sn = jnp.concatenate([-sin, sin, jnp.zeros(ang.shape[:-1] + (rest,), F32)], axis=-1)
    return jnp.tile(cs, (1, 1, 2)), jnp.tile(sn, (1, 1, 2))


def _layer(x, c, positions, w_ada, b_ada, g_pre_mix, g_post_mix, g_pre_ffn, g_post_ffn,
           w_in, conv_w, conv_b, b_gates, g_mlstm, w_branch_a, w_branch_b, w_out,
           router_w, router_bias, w_exp_gate, w_exp_up, w_exp_down, w_sh_gate, w_sh_up, w_sh_down):
    B, S, D = x.shape
    T = B * S
    H = MLSTM_HEADS
    x2 = x.reshape(T, D)

    mod3 = _adaln(c, w_ada, b_ada).reshape(B, 6, D)

    a_w = 3 * ATT_GROUP_W
    o_mq = 3 * a_w
    o_mk = o_mq + H * MLSTM_QK_DIM
    o_mv = o_mk + H * MLSTM_QK_DIM
    o_mo = o_mv + H * MLSTM_V_DIM
    o_mi = o_mo + H * MLSTM_V_DIM
    o_ga = o_mi + 2 * H
    o_gb = o_ga + D
    seg = lambda o, w: w_in[:, o:o + w]
    w_main = jnp.concatenate(
        [seg(o_mv, H * MLSTM_V_DIM), seg(o_mo, H * MLSTM_V_DIM), seg(o_ga, D), seg(o_gb, D),
         seg(o_mq, H * MLSTM_QK_DIM), seg(o_mk, H * MLSTM_QK_DIM),
         seg(0, a_w), seg(a_w, a_w), seg(2 * a_w, a_w)], axis=1).astype(BF16)
    w_if = jnp.pad(seg(o_mi, 2 * H), ((0, 0), (0, 128 - 2 * H))).astype(BF16)

    proj, gates = _in_proj(x2, mod3, g_pre_mix.reshape(1, D), w_main, w_if, S)
    proj3 = proj.reshape(B, S, PROJ_W)

    cs, sn = _rope_tables(positions)
    y_a = _attention(proj3, cs, sn)

    bg_row = jnp.pad(b_gates.reshape(1, 2 * H), ((0, 0), (0, 128 - 2 * H)))
    y_b = _mlstm(proj3, gates.reshape(B, S, 128), conv_w, conv_b.reshape(1, -1), bg_row,
                 g_mlstm.reshape(1, -1))

    x1, h2p = _merge(y_a.reshape(T, ATT_GROUP_W), y_b.reshape(T, D), proj, x2, mod3,
                     w_branch_a.astype(BF16), w_branch_b.astype(BF16), w_out.astype(BF16),
                     g_post_mix.reshape(1, D), g_pre_ffn.reshape(1, D), S)

    rw_t = router_w.T.astype(BF16)
    bias_col = jnp.broadcast_to(router_bias.reshape(N_EXPERTS, 1), (N_EXPERTS, 128))
    idx, wts, rank, cnt = _router(h2p, rw_t[:, :HALF], rw_t[:, HALF:], bias_col)

    bm = EXPERT_BLOCK
    nb = (T * TOP_K) // bm + N_EXPERTS
    counts = cnt[:, 0].astype(jnp.int32)
    padded = (counts + bm - 1) // bm * bm
    pend = jnp.cumsum(padded)
    pstart = pend - padded
    pstart_col = jnp.broadcast_to(pstart.astype(F32).reshape(N_EXPERTS, 1), (N_EXPERTS, 128))
    dest = _slot_index(idx, rank, pstart_col)
    nused = (pend[-1] // bm).astype(jnp.int32).reshape(1)
    blk_e = jnp.minimum(
        jnp.searchsorted(pend, jnp.arange(nb, dtype=jnp.int32) * bm, side="right"),
        N_EXPERTS - 1).astype(jnp.int32)

    xs = _dispatch(h2p, dest, nb * bm)
    ys = _expert_ffn(blk_e, nused, xs, w_exp_gate, w_exp_up, w_exp_down)
    yg = _collect(ys, dest)

    out = _final(yg, wts.T, h2p, x1, mod3, w_sh_gate.astype(BF16), w_sh_up.astype(BF16),
                 w_sh_down.astype(BF16), g_post_ffn.reshape(1, D), S)
    return out.reshape(B, S, D)


SC_CORES = 2
SC_SUBCORES = 16
SC_WORKERS = SC_CORES * SC_SUBCORES
SC_ROWS = 64


def _sc_mesh():
    return plsc.VectorSubcoreMesh(core_axis_name="c", subcore_axis_name="s",
                                  num_cores=SC_CORES, num_subcores=SC_SUBCORES)


def _worker_id():
    return lax.axis_index("s") * SC_CORES + lax.axis_index("c")


def _dispatch(h2p, dest, n_slots):
    T = h2p.shape[0]
    per_w = T // SC_WORKERS
    nch = per_w // SC_ROWS
    idx = dest.reshape(TOP_K, SC_WORKERS, nch, SC_ROWS).transpose(1, 2, 0, 3)
    idx = idx.reshape(SC_WORKERS, nch * TOP_K, SC_ROWS)

    def body(x_hbm, idx_hbm, xs_hbm, idx_v, buf0, buf1, rsem0, rsem1, ssem0, ssem1):
        wid = _worker_id()
        base = wid * per_w
        pltpu.sync_copy(idx_hbm.at[wid], idx_v)
        bufs = ((buf0, rsem0, ssem0), (buf1, rsem1, ssem1))

        def read(c, buf, rsem):
            return pltpu.make_async_copy(x_hbm.at[pl.ds(base + c * SC_ROWS, SC_ROWS)], buf, rsem)

        def scatter(c, k, buf, ssem):
            return pltpu.make_async_copy(buf, xs_hbm.at[idx_v.at[c * TOP_K + k]], ssem)

        read(0, buf0, rsem0).start()

        @pl.loop(0, nch, step=2)
        def _(c0):
            for b in range(2):
                c = c0 + b
                buf, rsem, ssem = bufs[b]
                obuf, orsem, ossem = bufs[1 - b]
                read(c, buf, rsem).wait()

                @pl.when(c > 0)
                def _():
                    for k in range(TOP_K):
                        scatter(c - 1, k, obuf, ossem).wait()

                @pl.when(c + 1 < nch)
                def _():
                    read(c + 1, obuf, orsem).start()

                for k in range(TOP_K):
                    scatter(c, k, buf, ssem).start()

        for k in range(TOP_K):
            scatter(nch - 1, k, buf1, ssem1).wait()

    run = pl.kernel(
        body,
        out_type=jax.ShapeDtypeStruct((n_slots, HALF), jnp.uint32),
        mesh=_sc_mesh(),
        scratch_types=[pltpu.VMEM((nch * TOP_K, SC_ROWS), jnp.int32),
                       pltpu.VMEM((SC_ROWS, HALF), jnp.uint32),
                       pltpu.VMEM((SC_ROWS, HALF), jnp.uint32),
                       pltpu.SemaphoreType.DMA, pltpu.SemaphoreType.DMA,
                       pltpu.SemaphoreType.DMA, pltpu.SemaphoreType.DMA],
        name="sc_dispatch",
    )
    return run(h2p, idx)


def _collect(ys, dest):
    n = dest.size
    per_w = n // SC_WORKERS
    nch = per_w // SC_ROWS
    idx = dest.reshape(SC_WORKERS, nch, SC_ROWS)

    def body(ys_hbm, idx_hbm, out_hbm, idx_v, buf0, buf1, gsem0, gsem1, wsem0, wsem1):
        wid = _worker_id()
        base = wid * per_w
        pltpu.sync_copy(idx_hbm.at[wid], idx_v)
        bufs = ((buf0, gsem0, wsem0), (buf1, gsem1, wsem1))

        def gather(c, buf, gsem):
            return pltpu.make_async_copy(ys_hbm.at[idx_v.at[c]], buf, gsem)

        def write(c, buf, wsem):
            return pltpu.make_async_copy(buf, out_hbm.at[pl.ds(base + c * SC_ROWS, SC_ROWS)], wsem)

        gather(0, buf0, gsem0).start()

        @pl.loop(0, nch, step=2)
        def _(c0):
            for b in range(2):
                c = c0 + b
                buf, gsem, wsem = bufs[b]
                obuf, ogsem, owsem = bufs[1 - b]
                gather(c, buf, gsem).wait()

                @pl.when(c > 0)
                def _():
                    write(c - 1, obuf, owsem).wait()

                @pl.when(c + 1 < nch)
                def _():
                    gather(c + 1, obuf, ogsem).start()

                write(c, buf, wsem).start()

        write(nch - 1, buf1, wsem1).wait()

    run = pl.kernel(
        body,
        out_type=jax.ShapeDtypeStruct((n, HALF), jnp.uint32),
        mesh=_sc_mesh(),
        scratch_types=[pltpu.VMEM((nch, SC_ROWS), jnp.int32),
                       pltpu.VMEM((SC_ROWS, HALF), jnp.uint32),
                       pltpu.VMEM((SC_ROWS, HALF), jnp.uint32),
                       pltpu.SemaphoreType.DMA, pltpu.SemaphoreType.DMA,
                       pltpu.SemaphoreType.DMA, pltpu.SemaphoreType.DMA],
        name="sc_collect",
    )
    return run(ys, idx).reshape(dest.shape + (HALF,))


def kernel(x, c, positions, w_ada, b_ada, g_pre_mix, g_post_mix, g_pre_ffn, g_post_ffn, w_in, conv_w, conv_b, b_gates, g_mlstm, w_branch_a, w_branch_b, w_out, router_w, router_bias, w_exp_gate, w_exp_up, w_exp_down, w_sh_gate, w_sh_up, w_sh_down):
    depth = w_ada.shape[0]
    for l in range(depth):
        x = _layer(x, c, positions, w_ada[l], b_ada[l], g_pre_mix[l], g_post_mix[l], g_pre_ffn[l],
                   g_post_ffn[l], w_in[l], conv_w[l], conv_b[l], b_gates[l], g_mlstm[l],
                   w_branch_a[l], w_branch_b[l], w_out[l], router_w[l], router_bias[l],
                   w_exp_gate[l], w_exp_up[l], w_exp_down[l], w_sh_gate[l], w_sh_up[l], w_sh_down[l])
    return x
```

```python
import functools

import jax
import jax.numpy as jnp
from jax import lax
from jax.experimental import pallas as pl
from jax.experimental.pallas import tpu as pltpu
from jax.experimental.pallas import tpu_sc as plsc

F32 = jnp.float32
BF16 = jnp.bfloat16
HIGHEST = lax.Precision.HIGHEST

D_MODEL = 1024
ATT_GROUPS = ((128, 1), (512, 4), (2048, 16))
ATT_HEAD_DIM = 64
ATT_GROUP_W = 256
ATT_BLK = 128
ROPE_THETA = 500000.0
ROPE_HALF = 8
MLSTM_HEADS = 4
MLSTM_QK_DIM = 128
MLSTM_V_DIM = 256
MLSTM_BLOCK = 128
CONV_WIDTH = 4
N_EXPERTS = 256
TOP_K = 8
N_GROUPS = 8
TOPK_GROUPS = 4
EXPERT_FF = 256
ROUTED_SCALE = 2.5
NORM_EPS = 1e-6
NEG = -1e30

OFF_MV, OFF_MO, OFF_GA, OFF_GB = 0, 1024, 2048, 3072
OFF_MQ, OFF_MK = 4096, 4608
OFF_AQ, OFF_AK, OFF_AV = 5120, 5888, 6656
PROJ_W = 7424
HALF = D_MODEL // 2

EXPERT_BLOCK = 256
VMEM_LIMIT = 56 * 1024 * 1024


def _nt(a, b, precision=None):
    return lax.dot_general(a, b, (((1,), (1,)), ((), ())), preferred_element_type=F32,
                           precision=precision)


def _tn(a, b):
    return lax.dot_general(a, b, (((0,), (0,)), ((), ())), preferred_element_type=F32)


def _silu(x):
    return x * jax.nn.sigmoid(x)


def _pack_pair(lo, hi):
    lo_b = pltpu.bitcast(lo.astype(BF16).astype(F32), jnp.uint32)
    hi_b = pltpu.bitcast(hi.astype(BF16).astype(F32), jnp.uint32)
    return (lo_b >> 16) | (hi_b & jnp.uint32(0xFFFF0000))


def _unpack_pair(w):
    lo = pltpu.bitcast(w << 16, F32)
    hi = pltpu.bitcast(w & jnp.uint32(0xFFFF0000), F32)
    return lo, hi


def _mod_kernel(c_ref, w_ref, b_ref, o_ref):
    a = _silu(c_ref[...])
    o_ref[...] = jnp.dot(a, w_ref[...], preferred_element_type=F32, precision=HIGHEST) + b_ref[...]


def _adaln(c, w_ada, b_ada):
    B = c.shape[0]
    n = w_ada.shape[1]
    tn = 512
    return pl.pallas_call(
        _mod_kernel,
        grid=(n // tn,),
        in_specs=[pl.BlockSpec((B, D_MODEL), lambda j: (0, 0)),
                  pl.BlockSpec((D_MODEL, tn), lambda j: (0, j)),
                  pl.BlockSpec((1, tn), lambda j: (0, j))],
        out_specs=pl.BlockSpec((B, tn), lambda j: (0, j)),
        out_shape=jax.ShapeDtypeStruct((B, n), F32),
        name="adaln_mod",
    )(c, w_ada, b_ada.reshape(1, n))


def _proj_kernel(x_ref, mod_ref, g_ref, w_ref, wif_ref, o_ref, gates_ref, h_ref):
    @pl.when(pl.program_id(1) == 0)
    def _():
        x = x_ref[...]
        ms = jnp.mean(x * x, axis=-1, keepdims=True)
        y = x * lax.rsqrt(ms + NORM_EPS) * g_ref[...]
        h = (y * (1.0 + mod_ref[0, 1:2, :]) + mod_ref[0, 0:1, :]).astype(BF16)
        h_ref[...] = h
        gates_ref[...] = jnp.dot(h, wif_ref[...], preferred_element_type=F32)

    o_ref[...] = jnp.dot(h_ref[...], w_ref[...], preferred_element_type=F32).astype(BF16)


def _in_proj(x2, mod3, g_pre, w_main, w_if, seq):
    T = x2.shape[0]
    tm, tn = 1024, 256
    per_b = seq // tm
    return pl.pallas_call(
        _proj_kernel,
        grid=(T // tm, PROJ_W // tn),
        in_specs=[pl.BlockSpec((tm, D_MODEL), lambda i, j: (i, 0)),
                  pl.BlockSpec((1, 6, D_MODEL), lambda i, j: (i // per_b, 0, 0)),
                  pl.BlockSpec((1, D_MODEL), lambda i, j: (0, 0)),
                  pl.BlockSpec((D_MODEL, tn), lambda i, j: (0, j)),
                  pl.BlockSpec((D_MODEL, 128), lambda i, j: (0, 0))],
        out_specs=[pl.BlockSpec((tm, tn), lambda i, j: (i, j)),
                   pl.BlockSpec((tm, 128), lambda i, j: (i, 0))],
        out_shape=[jax.ShapeDtypeStruct((T, PROJ_W), BF16),
                   jax.ShapeDtypeStruct((T, 128), F32)],
        scratch_shapes=[pltpu.VMEM((tm, D_MODEL), BF16)],
        compiler_params=pltpu.CompilerParams(
            dimension_semantics=("arbitrary", "arbitrary"), vmem_limit_bytes=VMEM_LIMIT),
        name="norm_in_proj",
    )(x2, mod3, g_pre, w_main, w_if)


def _attn_kernel(q_ref, k_ref, v_ref, cs_ref, sn_ref, o_ref, qf, kf, vf, acc, m_s, l_s, *, seq):
    g = pl.program_id(1)
    lane = lax.broadcasted_iota(jnp.int32, (ATT_BLK, 128), 1)
    first = (lane % ATT_HEAD_DIM) < ROPE_HALF
    low_head = lane < ATT_HEAD_DIM

    def rope(x, cs, sn):
        partner = jnp.where(first, pltpu.roll(x, 128 - ROPE_HALF, 1), pltpu.roll(x, ROPE_HALF, 1))
        return x * cs + partner * sn

    def zero_pad(i, _):
        rows = pl.ds(pl.multiple_of(i * ATT_BLK, ATT_BLK), ATT_BLK)
        for hp in range(2):
            kf[hp, rows, :] = jnp.zeros((ATT_BLK, 128), F32)
            vf[hp, rows, :] = jnp.zeros((ATT_BLK, 128), F32)
        return 0

    lax.fori_loop(0, seq // ATT_BLK, zero_pad, 0)

    def stage(i, _):
        r = pl.multiple_of(i * ATT_BLK, ATT_BLK)
        rows = pl.ds(r, ATT_BLK)
        prow = pl.ds(pl.multiple_of(seq + i * ATT_BLK, ATT_BLK), ATT_BLK)
        cs = cs_ref[0, rows, :]
        sn = sn_ref[0, rows, :]
        for hp in range(2):
            cols = pl.ds(hp * 128, 128)
            qf[hp, rows, :] = rope(q_ref[0, rows, cols].astype(F32), cs, sn) * (ATT_HEAD_DIM ** -0.5)
            kf[hp, prow, :] = rope(k_ref[0, rows, cols].astype(F32), cs, sn)
            vf[hp, prow, :] = v_ref[0, rows, cols].astype(F32)
        return 0

    lax.fori_loop(0, seq // ATT_BLK, stage, 0)

    qi = lax.broadcasted_iota(jnp.int32, (ATT_BLK, 2 * ATT_BLK), 0)
    ki = lax.broadcasted_iota(jnp.int32, (ATT_BLK, 2 * ATT_BLK), 1)
    band = (ki >= qi) & (ki <= qi + ATT_BLK)

    def process(d, init):
        span = ATT_BLK * d

        def body(c, _):
            rho = c % d
            n = c // d
            qstart = rho + n * span
            kstart = seq + qstart - span
            first_key = jnp.where(n > 0, 0, ATT_BLK)
            valid = band & (ki >= first_key)
            qrows = pl.ds(qstart, ATT_BLK, stride=d) if d > 1 else pl.ds(qstart, ATT_BLK)
            krows = pl.ds(kstart, 2 * ATT_BLK, stride=d) if d > 1 else pl.ds(kstart, 2 * ATT_BLK)
            for hp in range(2):
                q2 = qf[hp, qrows, :]
                k2 = kf[hp, krows, :].astype(BF16)
                v2 = vf[hp, krows, :].astype(BF16)
                res = []
                for hh in range(2):
                    hm = low_head if hh == 0 else jnp.logical_not(low_head)
                    qh = jnp.where(hm, q2, 0.0).astype(BF16)
                    s = jnp.where(valid, _nt(qh, k2), NEG)
                    m = jnp.max(s, axis=1, keepdims=True)
                    p = jnp.exp(s - m)
                    l = jnp.sum(p, axis=1, keepdims=True)
                    o = jnp.dot(p.astype(BF16), v2, preferred_element_type=F32)
                    res.append((o, m, l))
                o_b = jnp.where(low_head, res[0][0], res[1][0])
                m_b = jnp.where(low_head, res[0][1], res[1][1])
                l_b = jnp.where(low_head, res[0][2], res[1][2])
                if init:
                    acc[hp, qrows, :] = o_b
                    m_s[hp, qrows, :] = m_b
                    l_s[hp, qrows, :] = l_b
                else:
                    m_old = m_s[hp, qrows, :]
                    m_new = jnp.maximum(m_old, m_b)
                    a_old = jnp.exp(m_old - m_new)
                    a_new = jnp.exp(m_b - m_new)
                    acc[hp, qrows, :] = acc[hp, qrows, :] * a_old + o_b * a_new
                    l_s[hp, qrows, :] = l_s[hp, qrows, :] * a_old + l_b * a_new
                    m_s[hp, qrows, :] = m_new
            return 0

        lax.fori_loop(0, seq // ATT_BLK, body, 0)

    for gi, (_, d) in enumerate(ATT_GROUPS):
        @pl.when(g == gi)
        def _(d=d, gi=gi):
            process(d, gi == 0)

    @pl.when(g == len(ATT_GROUPS) - 1)
    def _():
        def fin(i, _):
            rows = pl.ds(pl.multiple_of(i * ATT_BLK, ATT_BLK), ATT_BLK)
            for hp in range(2):
                o_ref[0, rows, pl.ds(hp * 128, 128)] = (acc[hp, rows, :] / l_s[hp, rows, :]).astype(BF16)
            return 0

        lax.fori_loop(0, seq // ATT_BLK, fin, 0)


def _attention(proj3, cs, sn):
    B, S, _ = proj3.shape
    ng = len(ATT_GROUPS)
    qb, kb, vb = OFF_AQ // ATT_GROUP_W, OFF_AK // ATT_GROUP_W, OFF_AV // ATT_GROUP_W
    return pl.pallas_call(
        functools.partial(_attn_kernel, seq=S),
        grid=(B, ng),
        in_specs=[pl.BlockSpec((1, S, ATT_GROUP_W), lambda b, g: (b, 0, qb + g)),
                  pl.BlockSpec((1, S, ATT_GROUP_W), lambda b, g: (b, 0, kb + g)),
                  pl.BlockSpec((1, S, ATT_GROUP_W), lambda b, g: (b, 0, vb + g)),
                  pl.BlockSpec((1, S, 128), lambda b, g: (b, 0, 0)),
                  pl.BlockSpec((1, S, 128), lambda b, g: (b, 0, 0))],
        out_specs=pl.BlockSpec((1, S, ATT_GROUP_W), lambda b, g: (b, 0, 0)),
        out_shape=jax.ShapeDtypeStruct((B, S, ATT_GROUP_W), BF16),
        scratch_shapes=[pltpu.VMEM((2, S, 128), F32),
                        pltpu.VMEM((2, 2 * S, 128), F32),
                        pltpu.VMEM((2, 2 * S, 128), F32),
                        pltpu.VMEM((2, S, 128), F32),
                        pltpu.VMEM((2, S, 128), F32),
                        pltpu.VMEM((2, S, 128), F32)],
        compiler_params=pltpu.CompilerParams(
            dimension_semantics=("arbitrary", "arbitrary"), vmem_limit_bytes=VMEM_LIMIT),
        name="dilated_attention",
    )(proj3, proj3, proj3, cs, sn)


def _log_sigmoid(x):
    return jnp.minimum(x, 0.0) - jnp.log(1.0 + jnp.exp(-jnp.abs(x)))


def _mlstm_kernel(mq_ref, mk_ref, mv_ref, mo_ref, gt_ref, cwq_ref, cwk_ref, cbq_ref, cbk_ref,
                  bg_ref, gm_ref, o_ref, pad_ref, q_s, k_s, va_s, rows_s, acc_s, kv_s, inter_s, emt_s,
                  c_s, *, seq):
    h = pl.program_id(1)
    L = MLSTM_BLOCK
    NC = seq // L
    DK, DV = MLSTM_QK_DIM, MLSTM_V_DIM
    DA = DV + 128
    halo = 8

    pad_ref[0:halo, :] = jnp.zeros((halo, 2 * DK), F32)
    for i in range(NC):
        pad_ref[halo + i * L:halo + (i + 1) * L, 0:DK] = mq_ref[0, i * L:(i + 1) * L, :].astype(F32)
        pad_ref[halo + i * L:halo + (i + 1) * L, DK:2 * DK] = mk_ref[0, i * L:(i + 1) * L, :].astype(F32)
        va_s[i * L:(i + 1) * L, 0:DV] = mv_ref[0, i * L:(i + 1) * L, :]
        va_s[i * L:(i + 1) * L, DV:DA] = jnp.ones((L, DA - DV), BF16)
    for i in range(NC):
        yq = jnp.broadcast_to(cbq_ref[...], (L, DK))
        yk = jnp.broadcast_to(cbk_ref[...], (L, DK))
        for j in range(CONV_WIDTH):
            r0 = halo + i * L - (CONV_WIDTH - 1) + j
            yq = yq + pad_ref[r0:r0 + L, 0:DK] * cwq_ref[j:j + 1, :]
            yk = yk + pad_ref[r0:r0 + L, DK:2 * DK] * cwk_ref[j:j + 1, :]
        q_s[i * L:(i + 1) * L, :] = _silu(yq).astype(BF16)
        k_s[i * L:(i + 1) * L, :] = (_silu(yk) * (DK ** -0.5)).astype(BF16)

    lane = lax.broadcasted_iota(jnp.int32, (1, 128), 1)
    bias = bg_ref[...]
    b_i = jnp.sum(jnp.where(lane == h, bias, 0.0), axis=1, keepdims=True)
    b_f = jnp.sum(jnp.where(lane == h + MLSTM_HEADS, bias, 0.0), axis=1, keepdims=True)
    ri = lax.broadcasted_iota(jnp.int32, (L, L), 0)
    ci = lax.broadcasted_iota(jnp.int32, (L, L), 1)
    causal = ci <= ri
    eye = (ri == ci).astype(F32)
    i_rows = gt_ref[0, h] + b_i
    lf_rows = _log_sigmoid(gt_ref[0, h + MLSTM_HEADS] + b_f)
    b_rows = jnp.dot(lf_rows, (ri <= ci).astype(F32), preferred_element_type=F32,
                     precision=HIGHEST)
    b_end = b_rows[:, L - 1:L]
    g_rows = b_end - b_rows + i_rows
    g_max = jnp.max(g_rows, axis=1, keepdims=True)
    m = jnp.zeros((1, 1), F32)
    m_prev, m_new = [], []
    for c in range(NC):
        m_prev.append(m)
        m = jnp.maximum(b_end[c:c + 1, :] + m, g_max[c:c + 1, :])
        m_new.append(m)
    m_prev = jnp.concatenate(m_prev, axis=0)
    m_new = jnp.concatenate(m_new, axis=0)
    rows_s[0] = b_rows
    rows_s[1] = jnp.exp(g_rows - m_new)
    rows_s[2] = b_rows - i_rows
    rows_s[3] = jnp.broadcast_to(m_prev, (NC, L))
    rows_s[4] = jnp.broadcast_to(jnp.exp(b_end + m_prev - m_new), (NC, L))

    r2 = lax.broadcasted_iota(jnp.int32, (2 * L, 2 * L), 0)
    c2 = lax.broadcasted_iota(jnp.int32, (2 * L, 2 * L), 1)
    ones_blk = ((r2 < L) == (c2 < L)).astype(BF16)

    def local(c, _):
        rows = pl.ds(pl.multiple_of(c * L, L), L)
        crow = pl.ds(c, 1)
        b_r = rows_s[0, crow, :]
        w_r = rows_s[1, crow, :]
        u_r = rows_s[2, crow, :]
        mp = rows_s[3, crow, :]
        x2 = jnp.concatenate([eye * b_r, eye * w_r], axis=1)
        hi = x2.astype(BF16)
        lo = (x2 - hi.astype(F32)).astype(BF16)
        yb = (jnp.dot(hi, ones_blk, preferred_element_type=F32)
              + jnp.dot(lo, ones_blk, preferred_element_type=F32))
        b_b = yb[:, 0:L]
        w_b = yb[:, L:2 * L]
        dmat = jnp.where(causal, b_b - u_r, NEG)
        m_t = jnp.maximum(b_b + mp, jnp.max(dmat, axis=1, keepdims=True))
        q = q_s[rows, :]
        k = k_s[rows, :]
        va = va_s[rows, :]
        sc = _nt(q, k) * jnp.exp(dmat - m_t)
        acc_s[rows, :] = jnp.dot(sc.astype(BF16), va, preferred_element_type=F32)
        kv_s[c] = _tn((w_b * k.astype(F32)).astype(BF16), va)
        inter_s[rows, :] = jnp.exp(b_b + mp - m_t)
        emt_s[rows, :] = jnp.exp(-m_t)
        return 0

    lax.fori_loop(0, NC, local, 0, unroll=2)

    g_row = gm_ref[...]
    c_s[...] = jnp.zeros((DK, DA), F32)

    def recur(c, _):
        rows = pl.ds(pl.multiple_of(c * L, L), L)
        state = c_s[...]
        read = jnp.dot(q_s[rows, :], state.astype(BF16), preferred_element_type=F32)
        inter = inter_s[rows, :]
        out = acc_s[rows, :] + jnp.concatenate([inter, inter, inter], axis=1) * read
        den = out[:, DV:DA]
        emt = emt_s[rows, :]
        nrm = jnp.maximum(jnp.abs(jnp.concatenate([den, den], axis=1)),
                          jnp.concatenate([emt, emt], axis=1))
        hh = out[:, 0:DV] / nrm
        ms = jnp.mean(hh * hh, axis=1, keepdims=True)
        hn = hh * lax.rsqrt(ms + NORM_EPS) * g_row
        o_ref[0, rows, :] = (hn * jax.nn.sigmoid(mo_ref[0, rows, :].astype(F32))).astype(BF16)
        dec = rows_s[4, pl.ds(c, 1), :]
        c_s[...] = jnp.concatenate([dec, dec, dec], axis=1) * state + kv_s[c]
        return 0

    lax.fori_loop(0, NC, recur, 0, unroll=2)


def _mlstm(proj3, gates_t, conv_w, conv_b, bg_row, g_mlstm):
    B, S, _ = proj3.shape
    H, DK, DV = MLSTM_HEADS, MLSTM_QK_DIM, MLSTM_V_DIM
    L = MLSTM_BLOCK
    NC = S // L
    DA = DV + 128
    qb, kb = OFF_MQ // DK, OFF_MK // DK
    vb, ob = OFF_MV // DV, OFF_MO // DV
    nq = (H * DK) // DK
    return pl.pallas_call(
        functools.partial(_mlstm_kernel, seq=S),
        grid=(B, H),
        in_specs=[pl.BlockSpec((1, S, DK), lambda b, h: (b, 0, qb + h)),
                  pl.BlockSpec((1, S, DK), lambda b, h: (b, 0, kb + h)),
                  pl.BlockSpec((1, S, DV), lambda b, h: (b, 0, vb + h)),
                  pl.BlockSpec((1, S, DV), lambda b, h: (b, 0, ob + h)),
                  pl.BlockSpec((1, 2 * H, NC, L), lambda b, h: (b, 0, 0, 0)),
                  pl.BlockSpec((CONV_WIDTH, DK), lambda b, h: (0, h)),
                  pl.BlockSpec((CONV_WIDTH, DK), lambda b, h: (0, nq + h)),
                  pl.BlockSpec((1, DK), lambda b, h: (0, h)),
                  pl.BlockSpec((1, DK), lambda b, h: (0, nq + h)),
                  pl.BlockSpec((1, 128), lambda b, h: (0, 0)),
                  pl.BlockSpec((1, DV), lambda b, h: (0, h))],
        out_specs=pl.BlockSpec((1, S, DV), lambda b, h: (b, 0, h)),
        out_shape=jax.ShapeDtypeStruct((B, S, H * DV), BF16),
        scratch_shapes=[pltpu.VMEM((S + 8, 2 * DK), F32),
                        pltpu.VMEM((S, DK), BF16),
                        pltpu.VMEM((S, DK), BF16),
                        pltpu.VMEM((S, DA), BF16),
                        pltpu.VMEM((5, NC, L), F32),
                        pltpu.VMEM((S, DA), F32),
                        pltpu.VMEM((NC, DK, DA), F32),
                        pltpu.VMEM((S, L), F32),
                        pltpu.VMEM((S, L), F32),
                        pltpu.VMEM((DK, DA), F32)],
        compiler_params=pltpu.CompilerParams(
            dimension_semantics=("arbitrary", "arbitrary"), vmem_limit_bytes=VMEM_LIMIT),
        name="mlstm_chunkwise",
    )(proj3, proj3, proj3, proj3, gates_t, conv_w, conv_w, conv_b, conv_b, bg_row, g_mlstm)


def _rms(y, g):
    ms = jnp.mean(y * y, axis=-1, keepdims=True)
    return y * lax.rsqrt(ms + NORM_EPS) * g


def _merge_kernel(ya_ref, yb_ref, ga_ref, gb_ref, x_ref, mod_ref, wa_ref, wb_ref, wo_ref,
                  gpost_ref, gpre_ref, x1_ref, h2_ref):
    pa = jnp.dot(ya_ref[...], wa_ref[...], preferred_element_type=F32)
    pb = jnp.dot(yb_ref[...], wb_ref[...], preferred_element_type=F32)
    merged = (jax.nn.sigmoid(ga_ref[...].astype(F32)) * pa
              + jax.nn.sigmoid(gb_ref[...].astype(F32)) * pb)
    y = jnp.dot(merged.astype(BF16), wo_ref[...], preferred_element_type=F32)
    x1 = x_ref[...] + mod_ref[0, 2:3, :] * _rms(y, gpost_ref[...])
    x1_ref[...] = x1
    h2 = _rms(x1, gpre_ref[...]) * (1.0 + mod_ref[0, 4:5, :]) + mod_ref[0, 3:4, :]
    h2_ref[...] = _pack_pair(h2[:, :HALF], h2[:, HALF:])


def _merge(ya2, yb2, proj2, x2, mod3, wa, wb, wo, g_post, g_pre, seq):
    T = x2.shape[0]
    tm = 512
    per_b = seq // tm
    full = lambda shape: pl.BlockSpec(shape, lambda i: (0,) * len(shape))
    return pl.pallas_call(
        _merge_kernel,
        grid=(T // tm,),
        in_specs=[pl.BlockSpec((tm, ATT_GROUP_W), lambda i: (i, 0)),
                  pl.BlockSpec((tm, D_MODEL), lambda i: (i, 0)),
                  pl.BlockSpec((tm, D_MODEL), lambda i: (i, OFF_GA // D_MODEL)),
                  pl.BlockSpec((tm, D_MODEL), lambda i: (i, OFF_GB // D_MODEL)),
                  pl.BlockSpec((tm, D_MODEL), lambda i: (i, 0)),
                  pl.BlockSpec((1, 6, D_MODEL), lambda i: (i // per_b, 0, 0)),
                  full((ATT_GROUP_W, D_MODEL)), full((D_MODEL, D_MODEL)), full((D_MODEL, D_MODEL)),
                  full((1, D_MODEL)), full((1, D_MODEL))],
        out_specs=[pl.BlockSpec((tm, D_MODEL), lambda i: (i, 0)),
                   pl.BlockSpec((tm, HALF), lambda i: (i, 0))],
        out_shape=[jax.ShapeDtypeStruct((T, D_MODEL), F32),
                   jax.ShapeDtypeStruct((T, HALF), jnp.uint32)],
        compiler_params=pltpu.CompilerParams(
            dimension_semantics=("arbitrary",), vmem_limit_bytes=VMEM_LIMIT),
        name="merge_out_proj",
    )(ya2, yb2, proj2, proj2, x2, mod3, wa, wb, wo, g_post, g_pre)


def _router_kernel(h2_ref, rlo_ref, rhi_ref, bias_ref, idx_ref, w_ref, rank_ref, cnt_ref):
    E = N_EXPERTS
    tr = h2_ref.shape[0]
    gsz = E // N_GROUPS

    @pl.when(pl.program_id(0) == 0)
    def _():
        cnt_ref[...] = jnp.zeros(cnt_ref.shape, F32)

    lo, hi = _unpack_pair(h2_ref[...])
    logits = _nt(rlo_ref[...], lo.astype(BF16)) + _nt(rhi_ref[...], hi.astype(BF16))
    scores = jax.nn.sigmoid(logits)
    sel = scores + bias_ref[:, 0:1]

    gi = lax.broadcasted_iota(jnp.int32, (gsz, tr), 0).astype(F32)
    gs_rows = []
    for g in range(N_GROUPS):
        blk = sel[g * gsz:(g + 1) * gsz, :]
        m1 = jnp.max(blk, axis=0, keepdims=True)
        a1 = jnp.min(jnp.where(blk == m1, gi, float(E)), axis=0, keepdims=True)
        m2 = jnp.max(jnp.where(gi == a1, -jnp.inf, blk), axis=0, keepdims=True)
        gs_rows.append(m1 + m2)
    gs = jnp.concatenate(gs_rows, axis=0)
    g8 = lax.broadcasted_iota(jnp.int32, (N_GROUPS, tr), 0).astype(F32)
    gmask = jnp.zeros((N_GROUPS, tr), F32)
    for _ in range(TOPK_GROUPS):
        m = jnp.max(gs, axis=0, keepdims=True)
        a = jnp.min(jnp.where(gs == m, g8, float(E)), axis=0, keepdims=True)
        hit = g8 == a
        gmask = jnp.where(hit, 1.0, gmask)
        gs = jnp.where(hit, -jnp.inf, gs)
    selm = jnp.concatenate(
        [jnp.where(gmask[g:g + 1, :] > 0.0, sel[g * gsz:(g + 1) * gsz, :], -jnp.inf)
         for g in range(N_GROUPS)], axis=0)

    ei = lax.broadcasted_iota(jnp.int32, (E, tr), 0).astype(F32)
    picks, weights = [], []
    chosen = jnp.zeros((E, tr), F32)
    for _ in range(TOP_K):
        m = jnp.max(selm, axis=0, keepdims=True)
        a = jnp.min(jnp.where(selm == m, ei, float(E)), axis=0, keepdims=True)
        hit = ei == a
        picks.append(a)
        weights.append(jnp.sum(jnp.where(hit, scores, 0.0), axis=0, keepdims=True))
        chosen = jnp.where(hit, 1.0, chosen)
        selm = jnp.where(hit, -jnp.inf, selm)
    wsum = weights[0]
    for w in weights[1:]:
        wsum = wsum + w

    ti = lax.broadcasted_iota(jnp.int32, (tr, tr), 0)
    tj = lax.broadcasted_iota(jnp.int32, (tr, tr), 1)
    before = (ti < tj).astype(BF16)
    pos = jnp.dot(chosen.astype(BF16), before, preferred_element_type=F32) + cnt_ref[:, 0:1]
    ranks = [jnp.sum(jnp.where(ei == a, pos, 0.0), axis=0, keepdims=True) for a in picks]
    cnt_ref[...] = cnt_ref[...] + jnp.sum(chosen, axis=1, keepdims=True)

    idx_ref[...] = jnp.concatenate(picks, axis=0).astype(jnp.int32)
    w_ref[...] = jnp.concatenate([w / wsum * ROUTED_SCALE for w in weights], axis=0)
    rank_ref[...] = jnp.concatenate(ranks, axis=0).astype(jnp.int32)


def _router(h2p, r_lo, r_hi, bias_col):
    T = h2p.shape[0]
    tr = 512
    full = lambda shape: pl.BlockSpec(shape, lambda i: (0,) * len(shape))
    return pl.pallas_call(
        _router_kernel,
        grid=(T // tr,),
        in_specs=[pl.BlockSpec((tr, HALF), lambda i: (i, 0)),
                  full((N_EXPERTS, HALF)), full((N_EXPERTS, HALF)), full((N_EXPERTS, 128))],
        out_specs=[pl.BlockSpec((TOP_K, tr), lambda i: (0, i)),
                   pl.BlockSpec((TOP_K, tr), lambda i: (0, i)),
                   pl.BlockSpec((TOP_K, tr), lambda i: (0, i)),
                   full((N_EXPERTS, 128))],
        out_shape=[jax.ShapeDtypeStruct((TOP_K, T), jnp.int32),
                   jax.ShapeDtypeStruct((TOP_K, T), F32),
                   jax.ShapeDtypeStruct((TOP_K, T), jnp.int32),
                   jax.ShapeDtypeStruct((N_EXPERTS, 128), F32)],
        compiler_params=pltpu.CompilerParams(
            dimension_semantics=("arbitrary",), vmem_limit_bytes=VMEM_LIMIT),
        name="router_topk",
    )(h2p, r_lo, r_hi, bias_col)


def _dest_kernel(idx_ref, rank_ref, pstart_ref, dest_ref):
    tr = idx_ref.shape[1]
    ei = lax.broadcasted_iota(jnp.int32, (N_EXPERTS, tr), 0)
    start = pstart_ref[:, 0:1]
    rows = []
    for k in range(TOP_K):
        hit = ei == idx_ref[k:k + 1, :]
        rows.append(jnp.sum(jnp.where(hit, start, 0.0), axis=0, keepdims=True))
    dest_ref[...] = jnp.concatenate(rows, axis=0).astype(jnp.int32) + rank_ref[...]


def _slot_index(idx, rank, pstart_col):
    T = idx.shape[1]
    tr = 1024
    return pl.pallas_call(
        _dest_kernel,
        grid=(T // tr,),
        in_specs=[pl.BlockSpec((TOP_K, tr), lambda i: (0, i)),
                  pl.BlockSpec((TOP_K, tr), lambda i: (0, i)),
                  pl.BlockSpec((N_EXPERTS, 128), lambda i: (0, 0))],
        out_specs=pl.BlockSpec((TOP_K, tr), lambda i: (0, i)),
        out_shape=jax.ShapeDtypeStruct((TOP_K, T), jnp.int32),
        name="slot_index",
    )(idx, rank, pstart_col)


def _ffn_kernel(blk_e_ref, nused_ref, x_ref, wg_ref, wu_ref, wd_ref, y_ref, wg_s, wu_s, wd_s):
    i = pl.program_id(0)

    @pl.when(i < nused_ref[0])
    def _():
        e = blk_e_ref[i]
        prev = blk_e_ref[jnp.maximum(i - 1, 0)]

        @pl.when((i == 0) | (e != prev))
        def _():
            wg_s[...] = wg_ref[0].astype(BF16)
            wu_s[...] = wu_ref[0].astype(BF16)
            wd_s[...] = wd_ref[0].astype(BF16)

        lo, hi = _unpack_pair(x_ref[...])
        lo = lo.astype(BF16)
        hi = hi.astype(BF16)
        gate = (jnp.dot(lo, wg_s[0:HALF, :], preferred_element_type=F32)
                + jnp.dot(hi, wg_s[HALF:, :], preferred_element_type=F32))
        up = (jnp.dot(lo, wu_s[0:HALF, :], preferred_element_type=F32)
              + jnp.dot(hi, wu_s[HALF:, :], preferred_element_type=F32))
        hid = (_silu(gate) * up).astype(BF16)
        out = jnp.dot(hid, wd_s[...], preferred_element_type=F32)
        y_ref[...] = _pack_pair(out[:, :HALF], out[:, HALF:])


def _expert_ffn(blk_e, nused, xs, w_gate, w_up, w_down):
    P = xs.shape[0]
    bm = EXPERT_BLOCK
    nb = P // bm

    def row_map(i, blk_e_ref, nused_ref):
        return (jnp.minimum(i, nused_ref[0] - 1), 0)

    def w_map(i, blk_e_ref, nused_ref):
        return (blk_e_ref[jnp.minimum(i, nused_ref[0] - 1)], 0, 0)

    grid_spec = pltpu.PrefetchScalarGridSpec(
        num_scalar_prefetch=2,
        grid=(nb,),
        in_specs=[pl.BlockSpec((bm, HALF), row_map),
                  pl.BlockSpec((1, D_MODEL, EXPERT_FF), w_map),
                  pl.BlockSpec((1, D_MODEL, EXPERT_FF), w_map),
                  pl.BlockSpec((1, EXPERT_FF, D_MODEL), w_map)],
        out_specs=pl.BlockSpec((bm, HALF), row_map),
        scratch_shapes=[pltpu.VMEM((D_MODEL, EXPERT_FF), BF16),
                        pltpu.VMEM((D_MODEL, EXPERT_FF), BF16),
                        pltpu.VMEM((EXPERT_FF, D_MODEL), BF16)],
    )
    return pl.pallas_call(
        _ffn_kernel,
        grid_spec=grid_spec,
        out_shape=jax.ShapeDtypeStruct((P, HALF), jnp.uint32),
        compiler_params=pltpu.CompilerParams(
            dimension_semantics=("arbitrary",), vmem_limit_bytes=VMEM_LIMIT),
        name="routed_experts",
    )(blk_e, nused, xs, w_gate, w_up, w_down)


def _final_kernel(yg_ref, w_ref, h2_ref, x1_ref, mod_ref, wsg_ref, wsu_ref, wsd_ref, gpost_ref, o_ref):
    lo, hi = _unpack_pair(h2_ref[...])
    lo = lo.astype(BF16)
    hi = hi.astype(BF16)
    gate = (jnp.dot(lo, wsg_ref[0:HALF, :], preferred_element_type=F32)
            + jnp.dot(hi, wsg_ref[HALF:, :], preferred_element_type=F32))
    up = (jnp.dot(lo, wsu_ref[0:HALF, :], preferred_element_type=F32)
          + jnp.dot(hi, wsu_ref[HALF:, :], preferred_element_type=F32))
    shared = jnp.dot((_silu(gate) * up).astype(BF16), wsd_ref[...], preferred_element_type=F32)
    y_lo = shared[:, :HALF]
    y_hi = shared[:, HALF:]
    for k in range(TOP_K):
        r_lo, r_hi = _unpack_pair(yg_ref[k])
        wk = w_ref[:, k:k + 1]
        y_lo = y_lo + wk * r_lo
        y_hi = y_hi + wk * r_hi
    ms = (jnp.sum(y_lo * y_lo, axis=-1, keepdims=True)
          + jnp.sum(y_hi * y_hi, axis=-1, keepdims=True)) * (1.0 / D_MODEL)
    inv = lax.rsqrt(ms + NORM_EPS)
    o_ref[:, 0:HALF] = x1_ref[:, 0:HALF] + mod_ref[0, 5:6, 0:HALF] * (y_lo * inv * gpost_ref[:, 0:HALF])
    o_ref[:, HALF:] = x1_ref[:, HALF:] + mod_ref[0, 5:6, HALF:] * (y_hi * inv * gpost_ref[:, HALF:])


def _final(yg, w_tk, h2p, x1, mod3, wsg, wsu, wsd, g_post, seq):
    T = x1.shape[0]
    tm = 256
    per_b = seq // tm
    full = lambda shape: pl.BlockSpec(shape, lambda i: (0,) * len(shape))
    return pl.pallas_call(
        _final_kernel,
        grid=(T // tm,),
        in_specs=[pl.BlockSpec((TOP_K, tm, HALF), lambda i: (0, i, 0)),
                  pl.BlockSpec((tm, TOP_K), lambda i: (i, 0)),
                  pl.BlockSpec((tm, HALF), lambda i: (i, 0)),
                  pl.BlockSpec((tm, D_MODEL), lambda i: (i, 0)),
                  pl.BlockSpec((1, 6, D_MODEL), lambda i: (i // per_b, 0, 0)),
                  full((D_MODEL, EXPERT_FF)), full((D_MODEL, EXPERT_FF)), full((EXPERT_FF, D_MODEL)),
                  full((1, D_MODEL))],
        out_specs=pl.BlockSpec((tm, D_MODEL), lambda i: (i, 0)),
        out_shape=jax.ShapeDtypeStruct((T, D_MODEL), F32),
        compiler_params=pltpu.CompilerParams(
            dimension_semantics=("arbitrary",), vmem_limit_bytes=VMEM_LIMIT),
        name="shared_expert_combine",
    )(yg, w_tk, h2p, x1, mod3, wsg, wsu, wsd, g_post)


def _rope_tables(positions):
    inv = jnp.power(ROPE_THETA, -jnp.arange(ROPE_HALF, dtype=F32) / ROPE_HALF)
    ang = positions.astype(F32)[..., None] * inv
    cos, sin = jnp.cos(ang), jnp.sin(ang)
    rest = ATT_HEAD_DIM - 2 * ROPE_HALF
    cs = jnp.concatenate([cos, cos, jnp.ones(ang.shape[:-1] + (rest,), F32)], axis=-1)
    sn = jnp.concatenate([-sin, sin, jnp.zeros(ang.shape[:-1] + (rest,), F32)], axis=-1)
    return jnp.tile(cs, (1, 1, 2)), jnp.tile(sn, (1, 1, 2))


def _layer(x, c, positions, w_ada, b_ada, g_pre_mix, g_post_mix, g_pre_ffn, g_post_ffn,
           w_in, conv_w, conv_b, b_gates, g_mlstm, w_branch_a, w_branch_b, w_out,
           router_w, router_bias, w_exp_gate, w_exp_up, w_exp_down, w_sh_gate, w_sh_up, w_sh_down):
    B, S, D = x.shape
    T = B * S
    H = MLSTM_HEADS
    x2 = x.reshape(T, D)

    mod3 = _adaln(c, w_ada, b_ada).reshape(B, 6, D)

    a_w = 3 * ATT_GROUP_W
    o_mq = 3 * a_w
    o_mk = o_mq + H * MLSTM_QK_DIM
    o_mv = o_mk + H * MLSTM_QK_DIM
    o_mo = o_mv + H * MLSTM_V_DIM
    o_mi = o_mo + H * MLSTM_V_DIM
    o_ga = o_mi + 2 * H
    o_gb = o_ga + D
    seg = lambda o, w: w_in[:, o:o + w]
    w_main = jnp.concatenate(
        [seg(o_mv, H * MLSTM_V_DIM), seg(o_mo, H * MLSTM_V_DIM), seg(o_ga, D), seg(o_gb, D),
         seg(o_mq, H * MLSTM_QK_DIM), seg(o_mk, H * MLSTM_QK_DIM),
         seg(0, a_w), seg(a_w, a_w), seg(2 * a_w, a_w)], axis=1).astype(BF16)
    w_if = jnp.pad(seg(o_mi, 2 * H), ((0, 0), (0, 128 - 2 * H))).astype(BF16)

    proj, gates = _in_proj(x2, mod3, g_pre_mix.reshape(1, D), w_main, w_if, S)
    proj3 = proj.reshape(B, S, PROJ_W)

    cs, sn = _rope_tables(positions)
    y_a = _attention(proj3, cs, sn)

    bg_row = jnp.pad(b_gates.reshape(1, 2 * H), ((0, 0), (0, 128 - 2 * H)))
    gates_t = gates[:, :2 * H].reshape(B, S, 2 * H).transpose(0, 2, 1)
    gates_t = gates_t.reshape(B, 2 * H, S // MLSTM_BLOCK, MLSTM_BLOCK)
    y_b = _mlstm(proj3, gates_t, conv_w, conv_b.reshape(1, -1), bg_row, g_mlstm.reshape(1, -1))

    x1, h2p = _merge(y_a.reshape(T, ATT_GROUP_W), y_b.reshape(T, D), proj, x2, mod3,
                     w_branch_a.astype(BF16), w_branch_b.astype(BF16), w_out.astype(BF16),
                     g_post_mix.reshape(1, D), g_pre_ffn.reshape(1, D), S)

    rw_t = router_w.T.astype(BF16)
    bias_col = jnp.broadcast_to(router_bias.reshape(N_EXPERTS, 1), (N_EXPERTS, 128))
    idx, wts, rank, cnt = _router(h2p, rw_t[:, :HALF], rw_t[:, HALF:], bias_col)

    bm = EXPERT_BLOCK
    nb = (T * TOP_K) // bm + N_EXPERTS
    counts = cnt[:, 0].astype(jnp.int32)
    padded = (counts + bm - 1) // bm * bm
    pend = jnp.cumsum(padded)
    pstart = pend - padded
    pstart_col = jnp.broadcast_to(pstart.astype(F32).reshape(N_EXPERTS, 1), (N_EXPERTS, 128))
    dest = _slot_index(idx, rank, pstart_col)
    nused = (pend[-1] // bm).astype(jnp.int32).reshape(1)
    blk_e = jnp.minimum(
        jnp.searchsorted(pend, jnp.arange(nb, dtype=jnp.int32) * bm, side="right"),
        N_EXPERTS - 1).astype(jnp.int32)

    xs = _dispatch(h2p, dest, nb * bm)
    ys = _expert_ffn(blk_e, nused, xs, w_exp_gate, w_exp_up, w_exp_down)
    yg = _collect(ys, dest)

    out = _final(yg, wts.T, h2p, x1, mod3, w_sh_gate.astype(BF16), w_sh_up.astype(BF16),
                 w_sh_down.astype(BF16), g_post_ffn.reshape(1, D), S)
    return out.reshape(B, S, D)


SC_CORES = 2
SC_SUBCORES = 16
SC_WORKERS = SC_CORES * SC_SUBCORES
SC_ROWS = 64


def _sc_mesh():
    return plsc.VectorSubcoreMesh(core_axis_name="c", subcore_axis_name="s",
                                  num_cores=SC_CORES, num_subcores=SC_SUBCORES)


def _worker_id():
    return lax.axis_index("s") * SC_CORES + lax.axis_index("c")


def _dispatch(h2p, dest, n_slots):
    T = h2p.shape[0]
    per_w = T // SC_WORKERS
    nch = per_w // SC_ROWS
    idx = dest.reshape(TOP_K, SC_WORKERS, nch, SC_ROWS).transpose(1, 2, 0, 3)
    idx = idx.reshape(SC_WORKERS, nch * TOP_K, SC_ROWS)

    def body(x_hbm, idx_hbm, xs_hbm, idx_v, buf0, buf1, rsem0, rsem1, ssem0, ssem1):
        wid = _worker_id()
        base = wid * per_w
        pltpu.sync_copy(idx_hbm.at[wid], idx_v)
        bufs = ((buf0, rsem0, ssem0), (buf1, rsem1, ssem1))

        def read(c, buf, rsem):
            return pltpu.make_async_copy(x_hbm.at[pl.ds(base + c * SC_ROWS, SC_ROWS)], buf, rsem)

        def scatter(c, k, buf, ssem):
            return pltpu.make_async_copy(buf, xs_hbm.at[idx_v.at[c * TOP_K + k]], ssem)

        read(0, buf0, rsem0).start()

        @pl.loop(0, nch, step=2)
        def _(c0):
            for b in range(2):
                c = c0 + b
                buf, rsem, ssem = bufs[b]
                obuf, orsem, ossem = bufs[1 - b]
                read(c, buf, rsem).wait()

                @pl.when(c > 0)
                def _():
                    for k in range(TOP_K):
                        scatter(c - 1, k, obuf, ossem).wait()

                @pl.when(c + 1 < nch)
                def _():
                    read(c + 1, obuf, orsem).start()

                for k in range(TOP_K):
                    scatter(c, k, buf, ssem).start()

        for k in range(TOP_K):
            scatter(nch - 1, k, buf1, ssem1).wait()

    run = pl.kernel(
        body,
        out_type=jax.ShapeDtypeStruct((n_slots, HALF), jnp.uint32),
        mesh=_sc_mesh(),
        scratch_types=[pltpu.VMEM((nch * TOP_K, SC_ROWS), jnp.int32),
                       pltpu.VMEM((SC_ROWS, HALF), jnp.uint32),
                       pltpu.VMEM((SC_ROWS, HALF), jnp.uint32),
                       pltpu.SemaphoreType.DMA, pltpu.SemaphoreType.DMA,
                       pltpu.SemaphoreType.DMA, pltpu.SemaphoreType.DMA],
        name="sc_dispatch",
    )
    return run(h2p, idx)


def _collect(ys, dest):
    n = dest.size
    per_w = n // SC_WORKERS
    nch = per_w // SC_ROWS
    idx = dest.reshape(SC_WORKERS, nch, SC_ROWS)

    def body(ys_hbm, idx_hbm, out_hbm, idx_v, buf0, buf1, gsem0, gsem1, wsem0, wsem1):
        wid = _worker_id()
        base = wid * per_w
        pltpu.sync_copy(idx_hbm.at[wid], idx_v)
        bufs = ((buf0, gsem0, wsem0), (buf1, gsem1, wsem1))

        def gather(c, buf, gsem):
            return pltpu.make_async_copy(ys_hbm.at[idx_v.at[c]], buf, gsem)

        def write(c, buf, wsem):
            return pltpu.make_async_copy(buf, out_hbm.at[pl.ds(base + c * SC_ROWS, SC_ROWS)], wsem)

        gather(0, buf0, gsem0).start()

        @pl.loop(0, nch, step=2)
        def _(c0):
            for b in range(2):
                c = c0 + b
                buf, gsem, wsem = bufs[b]
                obuf, ogsem, owsem = bufs[1 - b]
                gather(c, buf, gsem).wait()

                @pl.when(c > 0)
                def _():
                    write(c - 1, obuf, owsem).wait()

                @pl.when(c + 1 < nch)
                def _():
                    gather(c + 1, obuf, ogsem).start()

                write(c, buf, wsem).start()

        write(nch - 1, buf1, wsem1).wait()

    run = pl.kernel(
        body,
        out_type=jax.ShapeDtypeStruct((n, HALF), jnp.uint32),
        mesh=_sc_mesh(),
        scratch_types=[pltpu.VMEM((nch, SC_ROWS), jnp.int32),
                       pltpu.VMEM((SC_ROWS, HALF), jnp.uint32),
                       pltpu.VMEM((SC_ROWS, HALF), jnp.uint32),
                       pltpu.SemaphoreType.DMA, pltpu.SemaphoreType.DMA,
                       pltpu.SemaphoreType.DMA, pltpu.SemaphoreType.DMA],
        name="sc_collect",
    )
    return run(ys, idx).reshape(dest.shape + (HALF,))


def kernel(x, c, positions, w_ada, b_ada, g_pre_mix, g_post_mix, g_pre_ffn, g_post_ffn, w_in, conv_w, conv_b, b_gates, g_mlstm, w_branch_a, w_branch_b, w_out, router_w, router_bias, w_exp_gate, w_exp_up, w_exp_down, w_sh_gate, w_sh_up, w_sh_down):
    depth = w_ada.shape[0]
    for l in range(depth):
        x = _layer(x, c, positions, w_ada[l], b_ada[l], g_pre_mix[l], g_post_mix[l], g_pre_ffn[l],
                   g_post_ffn[l], w_in[l], conv_w[l], conv_b[l], b_gates[l], g_mlstm[l],
                   w_branch_a[l], w_branch_b[l], w_out[l], router_w[l], router_bias[l],
                   w_exp_gate[l], w_exp_up[l], w_exp_down[l], w_sh_gate[l], w_sh_up[l], w_sh_down[l])
    return x
```

```python
import functools

import jax
import jax.numpy as jnp
from jax import lax
from jax.experimental import pallas as pl
from jax.experimental.pallas import tpu as pltpu
from jax.experimental.pallas import tpu_sc as plsc

F32 = jnp.float32
BF16 = jnp.bfloat16
HIGHEST = lax.Precision.HIGHEST

D_MODEL = 1024
ATT_GROUPS = ((128, 1), (512, 4), (2048, 16))
ATT_HEAD_DIM = 64
ATT_GROUP_W = 256
ATT_BLK = 128
ROPE_THETA = 500000.0
ROPE_HALF = 8
MLSTM_HEADS = 4
MLSTM_QK_DIM = 128
MLSTM_V_DIM = 256
MLSTM_BLOCK = 128
CONV_WIDTH = 4
N_EXPERTS = 256
TOP_K = 8
N_GROUPS = 8
TOPK_GROUPS = 4
EXPERT_FF = 256
ROUTED_SCALE = 2.5
NORM_EPS = 1e-6
NEG = -1e30

OFF_MV, OFF_MO, OFF_GA, OFF_GB = 0, 1024, 2048, 3072
OFF_MQ, OFF_MK = 4096, 4608
OFF_AQ, OFF_AK, OFF_AV = 5120, 5888, 6656
PROJ_W = 7424
HALF = D_MODEL // 2

EXPERT_BLOCK = 512
VMEM_LIMIT = 56 * 1024 * 1024


def _nt(a, b, precision=None):
    return lax.dot_general(a, b, (((1,), (1,)), ((), ())), preferred_element_type=F32,
                           precision=precision)


def _tn(a, b):
    return lax.dot_general(a, b, (((0,), (0,)), ((), ())), preferred_element_type=F32)


def _silu(x):
    return x * jax.nn.sigmoid(x)


def _pack_pair(lo, hi):
    lo_b = pltpu.bitcast(lo.astype(BF16).astype(F32), jnp.uint32)
    hi_b = pltpu.bitcast(hi.astype(BF16).astype(F32), jnp.uint32)
    return (lo_b >> 16) | (hi_b & jnp.uint32(0xFFFF0000))


def _unpack_pair(w):
    lo = pltpu.bitcast(w << 16, F32)
    hi = pltpu.bitcast(w & jnp.uint32(0xFFFF0000), F32)
    return lo, hi


def _mod_kernel(c_ref, w_ref, b_ref, o_ref):
    a = _silu(c_ref[...])
    o_ref[...] = jnp.dot(a, w_ref[...], preferred_element_type=F32, precision=HIGHEST) + b_ref[...]


def _adaln(c, w_ada, b_ada):
    B = c.shape[0]
    n = w_ada.shape[1]
    tn = 512
    return pl.pallas_call(
        _mod_kernel,
        grid=(n // tn,),
        in_specs=[pl.BlockSpec((B, D_MODEL), lambda j: (0, 0)),
                  pl.BlockSpec((D_MODEL, tn), lambda j: (0, j)),
                  pl.BlockSpec((1, tn), lambda j: (0, j))],
        out_specs=pl.BlockSpec((B, tn), lambda j: (0, j)),
        out_shape=jax.ShapeDtypeStruct((B, n), F32),
        name="adaln_mod",
    )(c, w_ada, b_ada.reshape(1, n))


def _proj_kernel(x_ref, mod_ref, g_ref, w_ref, wif_ref, o_ref, gates_ref, h_ref):
    @pl.when(pl.program_id(1) == 0)
    def _():
        x = x_ref[...]
        ms = jnp.mean(x * x, axis=-1, keepdims=True)
        y = x * lax.rsqrt(ms + NORM_EPS) * g_ref[...]
        h = (y * (1.0 + mod_ref[0, 1:2, :]) + mod_ref[0, 0:1, :]).astype(BF16)
        h_ref[...] = h
        gates_ref[...] = jnp.dot(h, wif_ref[...], preferred_element_type=F32)

    o_ref[...] = jnp.dot(h_ref[...], w_ref[...], preferred_element_type=F32).astype(BF16)


def _in_proj(x2, mod3, g_pre, w_main, w_if, seq):
    T = x2.shape[0]
    tm, tn = 1024, PROJ_W // 2
    per_b = seq // tm
    return pl.pallas_call(
        _proj_kernel,
        grid=(T // tm, PROJ_W // tn),
        in_specs=[pl.BlockSpec((tm, D_MODEL), lambda i, j: (i, 0)),
                  pl.BlockSpec((1, 6, D_MODEL), lambda i, j: (i // per_b, 0, 0)),
                  pl.BlockSpec((1, D_MODEL), lambda i, j: (0, 0)),
                  pl.BlockSpec((D_MODEL, tn), lambda i, j: (0, j)),
                  pl.BlockSpec((D_MODEL, 128), lambda i, j: (0, 0))],
        out_specs=[pl.BlockSpec((tm, tn), lambda i, j: (i, j)),
                   pl.BlockSpec((tm, 128), lambda i, j: (i, 0))],
        out_shape=[jax.ShapeDtypeStruct((T, PROJ_W), BF16),
                   jax.ShapeDtypeStruct((T, 128), F32)],
        scratch_shapes=[pltpu.VMEM((tm, D_MODEL), BF16)],
        compiler_params=pltpu.CompilerParams(
            dimension_semantics=("arbitrary", "arbitrary"), vmem_limit_bytes=VMEM_LIMIT),
        name="norm_in_proj",
    )(x2, mod3, g_pre, w_main, w_if)


def _attn_kernel(q_ref, k_ref, v_ref, cs_ref, sn_ref, o_ref, qf, kf, vf, acc, m_s, l_s, *, seq):
    g = pl.program_id(1)
    lane = lax.broadcasted_iota(jnp.int32, (ATT_BLK, 128), 1)
    first = (lane % ATT_HEAD_DIM) < ROPE_HALF
    low_head = lane < ATT_HEAD_DIM

    def rope(x, cs, sn):
        partner = jnp.where(first, pltpu.roll(x, 128 - ROPE_HALF, 1), pltpu.roll(x, ROPE_HALF, 1))
        return x * cs + partner * sn

    def zero_pad(i, _):
        rows = pl.ds(pl.multiple_of(i * ATT_BLK, ATT_BLK), ATT_BLK)
        for hp in range(2):
            kf[hp, rows, :] = jnp.zeros((ATT_BLK, 128), F32)
            vf[hp, rows, :] = jnp.zeros((ATT_BLK, 128), F32)
        return 0

    lax.fori_loop(0, seq // ATT_BLK, zero_pad, 0)

    def stage(i, _):
        r = pl.multiple_of(i * ATT_BLK, ATT_BLK)
        rows = pl.ds(r, ATT_BLK)
        prow = pl.ds(pl.multiple_of(seq + i * ATT_BLK, ATT_BLK), ATT_BLK)
        cs = cs_ref[0, rows, :]
        sn = sn_ref[0, rows, :]
        for hp in range(2):
            cols = pl.ds(hp * 128, 128)
            qf[hp, rows, :] = rope(q_ref[0, rows, cols].astype(F32), cs, sn) * (ATT_HEAD_DIM ** -0.5)
            kf[hp, prow, :] = rope(k_ref[0, rows, cols].astype(F32), cs, sn)
            vf[hp, prow, :] = v_ref[0, rows, cols].astype(F32)
        return 0

    lax.fori_loop(0, seq // ATT_BLK, stage, 0)

    qi = lax.broadcasted_iota(jnp.int32, (ATT_BLK, 2 * ATT_BLK), 0)
    ki = lax.broadcasted_iota(jnp.int32, (ATT_BLK, 2 * ATT_BLK), 1)
    band = (ki >= qi) & (ki <= qi + ATT_BLK)

    def process(d, init):
        span = ATT_BLK * d

        def body(c, _):
            rho = c % d
            n = c // d
            qstart = rho + n * span
            kstart = seq + qstart - span
            first_key = jnp.where(n > 0, 0, ATT_BLK)
            valid = band & (ki >= first_key)
            qrows = pl.ds(qstart, ATT_BLK, stride=d) if d > 1 else pl.ds(qstart, ATT_BLK)
            krows = pl.ds(kstart, 2 * ATT_BLK, stride=d) if d > 1 else pl.ds(kstart, 2 * ATT_BLK)
            for hp in range(2):
                q2 = qf[hp, qrows, :]
                k2 = kf[hp, krows, :].astype(BF16)
                v2 = vf[hp, krows, :].astype(BF16)
                res = []
                for hh in range(2):
                    hm = low_head if hh == 0 else jnp.logical_not(low_head)
                    qh = jnp.where(hm, q2, 0.0).astype(BF16)
                    s = jnp.where(valid, _nt(qh, k2), NEG)
                    m = jnp.max(s, axis=1, keepdims=True)
                    p = jnp.exp(s - m)
                    l = jnp.sum(p, axis=1, keepdims=True)
                    o = jnp.dot(p.astype(BF16), v2, preferred_element_type=F32)
                    res.append((o, m, l))
                o_b = jnp.where(low_head, res[0][0], res[1][0])
                m_b = jnp.where(low_head, res[0][1], res[1][1])
                l_b = jnp.where(low_head, res[0][2], res[1][2])
                if init:
                    acc[hp, qrows, :] = o_b
                    m_s[hp, qrows, :] = m_b
                    l_s[hp, qrows, :] = l_b
                else:
                    m_old = m_s[hp, qrows, :]
                    m_new = jnp.maximum(m_old, m_b)
                    a_old = jnp.exp(m_old - m_new)
                    a_new = jnp.exp(m_b - m_new)
                    acc[hp, qrows, :] = acc[hp, qrows, :] * a_old + o_b * a_new
                    l_s[hp, qrows, :] = l_s[hp, qrows, :] * a_old + l_b * a_new
                    m_s[hp, qrows, :] = m_new
            return 0

        lax.fori_loop(0, seq // ATT_BLK, body, 0)

    for gi, (_, d) in enumerate(ATT_GROUPS):
        @pl.when(g == gi)
        def _(d=d, gi=gi):
            process(d, gi == 0)

    @pl.when(g == len(ATT_GROUPS) - 1)
    def _():
        def fin(i, _):
            rows = pl.ds(pl.multiple_of(i * ATT_BLK, ATT_BLK), ATT_BLK)
            for hp in range(2):
                o_ref[0, rows, pl.ds(hp * 128, 128)] = (acc[hp, rows, :] / l_s[hp, rows, :]).astype(BF16)
            return 0

        lax.fori_loop(0, seq // ATT_BLK, fin, 0)


def _attention(proj3, cs, sn):
    B, S, _ = proj3.shape
    ng = len(ATT_GROUPS)
    qb, kb, vb = OFF_AQ // ATT_GROUP_W, OFF_AK // ATT_GROUP_W, OFF_AV // ATT_GROUP_W
    return pl.pallas_call(
        functools.partial(_attn_kernel, seq=S),
        grid=(B, ng),
        in_specs=[pl.BlockSpec((1, S, ATT_GROUP_W), lambda b, g: (b, 0, qb + g)),
                  pl.BlockSpec((1, S, ATT_GROUP_W), lambda b, g: (b, 0, kb + g)),
                  pl.BlockSpec((1, S, ATT_GROUP_W), lambda b, g: (b, 0, vb + g)),
                  pl.BlockSpec((1, S, 128), lambda b, g: (b, 0, 0)),
                  pl.BlockSpec((1, S, 128), lambda b, g: (b, 0, 0))],
        out_specs=pl.BlockSpec((1, S, ATT_GROUP_W), lambda b, g: (b, 0, 0)),
        out_shape=jax.ShapeDtypeStruct((B, S, ATT_GROUP_W), BF16),
        scratch_shapes=[pltpu.VMEM((2, S, 128), F32),
                        pltpu.VMEM((2, 2 * S, 128), F32),
                        pltpu.VMEM((2, 2 * S, 128), F32),
                        pltpu.VMEM((2, S, 128), F32),
                        pltpu.VMEM((2, S, 128), F32),
                        pltpu.VMEM((2, S, 128), F32)],
        compiler_params=pltpu.CompilerParams(
            dimension_semantics=("arbitrary", "arbitrary"), vmem_limit_bytes=VMEM_LIMIT),
        name="dilated_attention",
    )(proj3, proj3, proj3, cs, sn)


def _log_sigmoid(x):
    return jnp.minimum(x, 0.0) - jnp.log(1.0 + jnp.exp(-jnp.abs(x)))


def _mlstm_kernel(mq_ref, mk_ref, mv_ref, mo_ref, gt_ref, cwq_ref, cwk_ref, cbq_ref, cbk_ref,
                  bg_ref, gm_ref, o_ref, pad_ref, q_s, k_s, va_s, rows_s, acc_s, kv_s, inter_s, emt_s,
                  c_s, *, seq):
    h = pl.program_id(1)
    L = MLSTM_BLOCK
    NC = seq // L
    DK, DV = MLSTM_QK_DIM, MLSTM_V_DIM
    DA = DV + 128
    halo = 8

    pad_ref[0:halo, :] = jnp.zeros((halo, 2 * DK), F32)
    for i in range(NC):
        pad_ref[halo + i * L:halo + (i + 1) * L, 0:DK] = mq_ref[0, i * L:(i + 1) * L, :].astype(F32)
        pad_ref[halo + i * L:halo + (i + 1) * L, DK:2 * DK] = mk_ref[0, i * L:(i + 1) * L, :].astype(F32)
        va_s[i * L:(i + 1) * L, 0:DV] = mv_ref[0, i * L:(i + 1) * L, :]
        va_s[i * L:(i + 1) * L, DV:DA] = jnp.ones((L, DA - DV), BF16)
    for i in range(NC):
        yq = jnp.broadcast_to(cbq_ref[...], (L, DK))
        yk = jnp.broadcast_to(cbk_ref[...], (L, DK))
        for j in range(CONV_WIDTH):
            r0 = halo + i * L - (CONV_WIDTH - 1) + j
            yq = yq + pad_ref[r0:r0 + L, 0:DK] * cwq_ref[j:j + 1, :]
            yk = yk + pad_ref[r0:r0 + L, DK:2 * DK] * cwk_ref[j:j + 1, :]
        q_s[i * L:(i + 1) * L, :] = _silu(yq).astype(BF16)
        k_s[i * L:(i + 1) * L, :] = (_silu(yk) * (DK ** -0.5)).astype(BF16)

    lane = lax.broadcasted_iota(jnp.int32, (1, 128), 1)
    bias = bg_ref[...]
    b_i = jnp.sum(jnp.where(lane == h, bias, 0.0), axis=1, keepdims=True)
    b_f = jnp.sum(jnp.where(lane == h + MLSTM_HEADS, bias, 0.0), axis=1, keepdims=True)
    ri = lax.broadcasted_iota(jnp.int32, (L, L), 0)
    ci = lax.broadcasted_iota(jnp.int32, (L, L), 1)
    causal = ci <= ri
    eye = (ri == ci).astype(F32)
    i_rows = gt_ref[0, h] + b_i
    lf_rows = _log_sigmoid(gt_ref[0, h + MLSTM_HEADS] + b_f)
    b_rows = jnp.dot(lf_rows, (ri <= ci).astype(F32), preferred_element_type=F32,
                     precision=HIGHEST)
    b_end = b_rows[:, L - 1:L]
    g_rows = b_end - b_rows + i_rows
    g_max = jnp.max(g_rows, axis=1, keepdims=True)
    m = jnp.zeros((1, 1), F32)
    m_prev, m_new = [], []
    for c in range(NC):
        m_prev.append(m)
        m = jnp.maximum(b_end[c:c + 1, :] + m, g_max[c:c + 1, :])
        m_new.append(m)
    m_prev = jnp.concatenate(m_prev, axis=0)
    m_new = jnp.concatenate(m_new, axis=0)
    rows_s[0] = b_rows
    rows_s[1] = jnp.exp(g_rows - m_new)
    rows_s[2] = b_rows - i_rows
    rows_s[3] = jnp.broadcast_to(m_prev, (NC, L))
    rows_s[4] = jnp.broadcast_to(jnp.exp(b_end + m_prev - m_new), (NC, L))

    r2 = lax.broadcasted_iota(jnp.int32, (2 * L, 2 * L), 0)
    c2 = lax.broadcasted_iota(jnp.int32, (2 * L, 2 * L), 1)
    ones_blk = ((r2 < L) == (c2 < L)).astype(BF16)

    def local(c, _):
        rows = pl.ds(pl.multiple_of(c * L, L), L)
        crow = pl.ds(c, 1)
        b_r = rows_s[0, crow, :]
        w_r = rows_s[1, crow, :]
        u_r = rows_s[2, crow, :]
        mp = rows_s[3, crow, :]
        x2 = jnp.concatenate([eye * b_r, eye * w_r], axis=1)
        hi = x2.astype(BF16)
        lo = (x2 - hi.astype(F32)).astype(BF16)
        yb = (jnp.dot(hi, ones_blk, preferred_element_type=F32)
              + jnp.dot(lo, ones_blk, preferred_element_type=F32))
        b_b = yb[:, 0:L]
        w_b = yb[:, L:2 * L]
        dmat = jnp.where(causal, b_b - u_r, NEG)
        m_t = jnp.maximum(b_b + mp, jnp.max(dmat, axis=1, keepdims=True))
        q = q_s[rows, :]
        k = k_s[rows, :]
        va = va_s[rows, :]
        sc = _nt(q, k) * jnp.exp(dmat - m_t)
        acc_s[rows, :] = jnp.dot(sc.astype(BF16), va, preferred_element_type=F32)
        kv_s[c] = _tn((w_b * k.astype(F32)).astype(BF16), va)
        inter_s[rows, :] = jnp.exp(b_b + mp - m_t)
        emt_s[rows, :] = jnp.exp(-m_t)
        return 0

    lax.fori_loop(0, NC, local, 0, unroll=2)

    g_row = gm_ref[...]
    c_s[...] = jnp.zeros((DK, DA), F32)

    def recur(c, _):
        rows = pl.ds(pl.multiple_of(c * L, L), L)
        state = c_s[...]
        read = jnp.dot(q_s[rows, :], state.astype(BF16), preferred_element_type=F32)
        inter = inter_s[rows, :]
        out = acc_s[rows, :] + jnp.concatenate([inter, inter, inter], axis=1) * read
        den = out[:, DV:DA]
        emt = emt_s[rows, :]
        nrm = jnp.maximum(jnp.abs(jnp.concatenate([den, den], axis=1)),
                          jnp.concatenate([emt, emt], axis=1))
        hh = out[:, 0:DV] / nrm
        ms = jnp.mean(hh * hh, axis=1, keepdims=True)
        hn = hh * lax.rsqrt(ms + NORM_EPS) * g_row
        o_ref[0, rows, :] = (hn * jax.nn.sigmoid(mo_ref[0, rows, :].astype(F32))).astype(BF16)
        dec = rows_s[4, pl.ds(c, 1), :]
        c_s[...] = jnp.concatenate([dec, dec, dec], axis=1) * state + kv_s[c]
        return 0

    lax.fori_loop(0, NC, recur, 0, unroll=2)


def _mlstm(proj3, gates_t, conv_w, conv_b, bg_row, g_mlstm):
    B, S, _ = proj3.shape
    H, DK, DV = MLSTM_HEADS, MLSTM_QK_DIM, MLSTM_V_DIM
    L = MLSTM_BLOCK
    NC = S // L
    DA = DV + 128
    qb, kb = OFF_MQ // DK, OFF_MK // DK
    vb, ob = OFF_MV // DV, OFF_MO // DV
    nq = (H * DK) // DK
    return pl.pallas_call(
        functools.partial(_mlstm_kernel, seq=S),
        grid=(B, H),
        in_specs=[pl.BlockSpec((1, S, DK), lambda b, h: (b, 0, qb + h)),
                  pl.BlockSpec((1, S, DK), lambda b, h: (b, 0, kb + h)),
                  pl.BlockSpec((1, S, DV), lambda b, h: (b, 0, vb + h)),
                  pl.BlockSpec((1, S, DV), lambda b, h: (b, 0, ob + h)),
                  pl.BlockSpec((1, 2 * H, NC, L), lambda b, h: (b, 0, 0, 0)),
                  pl.BlockSpec((CONV_WIDTH, DK), lambda b, h: (0, h)),
                  pl.BlockSpec((CONV_WIDTH, DK), lambda b, h: (0, nq + h)),
                  pl.BlockSpec((1, DK), lambda b, h: (0, h)),
                  pl.BlockSpec((1, DK), lambda b, h: (0, nq + h)),
                  pl.BlockSpec((1, 128), lambda b, h: (0, 0)),
                  pl.BlockSpec((1, DV), lambda b, h: (0, h))],
        out_specs=pl.BlockSpec((1, S, DV), lambda b, h: (b, 0, h)),
        out_shape=jax.ShapeDtypeStruct((B, S, H * DV), BF16),
        scratch_shapes=[pltpu.VMEM((S + 8, 2 * DK), F32),
                        pltpu.VMEM((S, DK), BF16),
                        pltpu.VMEM((S, DK), BF16),
                        pltpu.VMEM((S, DA), BF16),
                        pltpu.VMEM((5, NC, L), F32),
                        pltpu.VMEM((S, DA), F32),
                        pltpu.VMEM((NC, DK, DA), F32),
                        pltpu.VMEM((S, L), F32),
                        pltpu.VMEM((S, L), F32),
                        pltpu.VMEM((DK, DA), F32)],
        compiler_params=pltpu.CompilerParams(
            dimension_semantics=("arbitrary", "arbitrary"), vmem_limit_bytes=VMEM_LIMIT),
        name="mlstm_chunkwise",
    )(proj3, proj3, proj3, proj3, gates_t, conv_w, conv_w, conv_b, conv_b, bg_row, g_mlstm)


def _rms(y, g):
    ms = jnp.mean(y * y, axis=-1, keepdims=True)
    return y * lax.rsqrt(ms + NORM_EPS) * g


def _merge_kernel(ya_ref, yb_ref, ga_ref, gb_ref, x_ref, mod_ref, wa_ref, wb_ref, wo_ref,
                  gpost_ref, gpre_ref, x1_ref, h2_ref):
    pa = jnp.dot(ya_ref[...], wa_ref[...], preferred_element_type=F32)
    pb = jnp.dot(yb_ref[...], wb_ref[...], preferred_element_type=F32)
    merged = (jax.nn.sigmoid(ga_ref[...].astype(F32)) * pa
              + jax.nn.sigmoid(gb_ref[...].astype(F32)) * pb)
    y = jnp.dot(merged.astype(BF16), wo_ref[...], preferred_element_type=F32)
    x1 = x_ref[...] + mod_ref[0, 2:3, :] * _rms(y, gpost_ref[...])
    x1_ref[...] = x1
    h2 = _rms(x1, gpre_ref[...]) * (1.0 + mod_ref[0, 4:5, :]) + mod_ref[0, 3:4, :]
    h2_ref[...] = _pack_pair(h2[:, :HALF], h2[:, HALF:])


def _merge(ya2, yb2, proj2, x2, mod3, wa, wb, wo, g_post, g_pre, seq):
    T = x2.shape[0]
    tm = 512
    per_b = seq // tm
    full = lambda shape: pl.BlockSpec(shape, lambda i: (0,) * len(shape))
    return pl.pallas_call(
        _merge_kernel,
        grid=(T // tm,),
        in_specs=[pl.BlockSpec((tm, ATT_GROUP_W), lambda i: (i, 0)),
                  pl.BlockSpec((tm, D_MODEL), lambda i: (i, 0)),
                  pl.BlockSpec((tm, D_MODEL), lambda i: (i, OFF_GA // D_MODEL)),
                  pl.BlockSpec((tm, D_MODEL), lambda i: (i, OFF_GB // D_MODEL)),
                  pl.BlockSpec((tm, D_MODEL), lambda i: (i, 0)),
                  pl.BlockSpec((1, 6, D_MODEL), lambda i: (i // per_b, 0, 0)),
                  full((ATT_GROUP_W, D_MODEL)), full((D_MODEL, D_MODEL)), full((D_MODEL, D_MODEL)),
                  full((1, D_MODEL)), full((1, D_MODEL))],
        out_specs=[pl.BlockSpec((tm, D_MODEL), lambda i: (i, 0)),
                   pl.BlockSpec((tm, HALF), lambda i: (i, 0))],
        out_shape=[jax.ShapeDtypeStruct((T, D_MODEL), F32),
                   jax.ShapeDtypeStruct((T, HALF), jnp.uint32)],
        compiler_params=pltpu.CompilerParams(
            dimension_semantics=("arbitrary",), vmem_limit_bytes=VMEM_LIMIT),
        name="merge_out_proj",
    )(ya2, yb2, proj2, proj2, x2, mod3, wa, wb, wo, g_post, g_pre)


def _router_kernel(h2_ref, rlo_ref, rhi_ref, bias_ref, idx_ref, w_ref, rank_ref, cnt_ref):
    E = N_EXPERTS
    tr = h2_ref.shape[0]
    gsz = E // N_GROUPS

    @pl.when(pl.program_id(0) == 0)
    def _():
        cnt_ref[...] = jnp.zeros(cnt_ref.shape, F32)

    lo, hi = _unpack_pair(h2_ref[...])
    logits = _nt(rlo_ref[...], lo.astype(BF16)) + _nt(rhi_ref[...], hi.astype(BF16))
    scores = jax.nn.sigmoid(logits)
    sel = scores + bias_ref[:, 0:1]

    gi = lax.broadcasted_iota(jnp.int32, (gsz, tr), 0).astype(F32)
    gs_rows = []
    for g in range(N_GROUPS):
        blk = sel[g * gsz:(g + 1) * gsz, :]
        m1 = jnp.max(blk, axis=0, keepdims=True)
        a1 = jnp.min(jnp.where(blk == m1, gi, float(E)), axis=0, keepdims=True)
        m2 = jnp.max(jnp.where(gi == a1, -jnp.inf, blk), axis=0, keepdims=True)
        gs_rows.append(m1 + m2)
    gs = jnp.concatenate(gs_rows, axis=0)
    g8 = lax.broadcasted_iota(jnp.int32, (N_GROUPS, tr), 0).astype(F32)
    gmask = jnp.zeros((N_GROUPS, tr), F32)
    for _ in range(TOPK_GROUPS):
        m = jnp.max(gs, axis=0, keepdims=True)
        a = jnp.min(jnp.where(gs == m, g8, float(E)), axis=0, keepdims=True)
        hit = g8 == a
        gmask = jnp.where(hit, 1.0, gmask)
        gs = jnp.where(hit, -jnp.inf, gs)
    selm = jnp.concatenate(
        [jnp.where(gmask[g:g + 1, :] > 0.0, sel[g * gsz:(g + 1) * gsz, :], -jnp.inf)
         for g in range(N_GROUPS)], axis=0)

    ei = lax.broadcasted_iota(jnp.int32, (E, tr), 0).astype(F32)
    picks, weights = [], []
    chosen = jnp.zeros((E, tr), F32)
    for _ in range(TOP_K):
        m = jnp.max(selm, axis=0, keepdims=True)
        a = jnp.min(jnp.where(selm == m, ei, float(E)), axis=0, keepdims=True)
        hit = ei == a
        picks.append(a)
        weights.append(jnp.sum(jnp.where(hit, scores, 0.0), axis=0, keepdims=True))
        chosen = jnp.where(hit, 1.0, chosen)
        selm = jnp.where(hit, -jnp.inf, selm)
    wsum = weights[0]
    for w in weights[1:]:
        wsum = wsum + w

    ti = lax.broadcasted_iota(jnp.int32, (tr, tr), 0)
    tj = lax.broadcasted_iota(jnp.int32, (tr, tr), 1)
    before = (ti < tj).astype(BF16)
    pos = jnp.dot(chosen.astype(BF16), before, preferred_element_type=F32) + cnt_ref[:, 0:1]
    ranks = [jnp.sum(jnp.where(ei == a, pos, 0.0), axis=0, keepdims=True) for a in picks]
    cnt_ref[...] = cnt_ref[...] + jnp.sum(chosen, axis=1, keepdims=True)

    idx_ref[...] = jnp.concatenate(picks, axis=0).astype(jnp.int32)
    w_ref[...] = jnp.concatenate([w / wsum * ROUTED_SCALE for w in weights], axis=0)
    rank_ref[...] = jnp.concatenate(ranks, axis=0).astype(jnp.int32)


def _router(h2p, r_lo, r_hi, bias_col):
    T = h2p.shape[0]
    tr = 512
    full = lambda shape: pl.BlockSpec(shape, lambda i: (0,) * len(shape))
    return pl.pallas_call(
        _router_kernel,
        grid=(T // tr,),
        in_specs=[pl.BlockSpec((tr, HALF), lambda i: (i, 0)),
                  full((N_EXPERTS, HALF)), full((N_EXPERTS, HALF)), full((N_EXPERTS, 128))],
        out_specs=[pl.BlockSpec((TOP_K, tr), lambda i: (0, i)),
                   pl.BlockSpec((TOP_K, tr), lambda i: (0, i)),
                   pl.BlockSpec((TOP_K, tr), lambda i: (0, i)),
                   full((N_EXPERTS, 128))],
        out_shape=[jax.ShapeDtypeStruct((TOP_K, T), jnp.int32),
                   jax.ShapeDtypeStruct((TOP_K, T), F32),
                   jax.ShapeDtypeStruct((TOP_K, T), jnp.int32),
                   jax.ShapeDtypeStruct((N_EXPERTS, 128), F32)],
        compiler_params=pltpu.CompilerParams(
            dimension_semantics=("arbitrary",), vmem_limit_bytes=VMEM_LIMIT),
        name="router_topk",
    )(h2p, r_lo, r_hi, bias_col)


def _dest_kernel(idx_ref, rank_ref, pstart_ref, dest_ref):
    tr = idx_ref.shape[1]
    ei = lax.broadcasted_iota(jnp.int32, (N_EXPERTS, tr), 0)
    start = pstart_ref[:, 0:1]
    rows = []
    for k in range(TOP_K):
        hit = ei == idx_ref[k:k + 1, :]
        rows.append(jnp.sum(jnp.where(hit, start, 0.0), axis=0, keepdims=True))
    dest_ref[...] = jnp.concatenate(rows, axis=0).astype(jnp.int32) + rank_ref[...]


def _slot_index(idx, rank, pstart_col):
    T = idx.shape[1]
    tr = 1024
    return pl.pallas_call(
        _dest_kernel,
        grid=(T // tr,),
        in_specs=[pl.BlockSpec((TOP_K, tr), lambda i: (0, i)),
                  pl.BlockSpec((TOP_K, tr), lambda i: (0, i)),
                  pl.BlockSpec((N_EXPERTS, 128), lambda i: (0, 0))],
        out_specs=pl.BlockSpec((TOP_K, tr), lambda i: (0, i)),
        out_shape=jax.ShapeDtypeStruct((TOP_K, T), jnp.int32),
        name="slot_index",
    )(idx, rank, pstart_col)


def _ffn_kernel(blk_e_ref, nused_ref, x_ref, wg_ref, wu_ref, wd_ref, y_ref, wg_s, wu_s, wd_s):
    i = pl.program_id(0)

    @pl.when(i < nused_ref[0])
    def _():
        e = blk_e_ref[i]
        prev = blk_e_ref[jnp.maximum(i - 1, 0)]

        @pl.when((i == 0) | (e != prev))
        def _():
            wg_s[...] = wg_ref[0].astype(BF16)
            wu_s[...] = wu_ref[0].astype(BF16)
            wd_s[...] = wd_ref[0].astype(BF16)

        lo, hi = _unpack_pair(x_ref[...])
        lo = lo.astype(BF16)
        hi = hi.astype(BF16)
        gate = (jnp.dot(lo, wg_s[0:HALF, :], preferred_element_type=F32)
                + jnp.dot(hi, wg_s[HALF:, :], preferred_element_type=F32))
        up = (jnp.dot(lo, wu_s[0:HALF, :], preferred_element_type=F32)
              + jnp.dot(hi, wu_s[HALF:, :], preferred_element_type=F32))
        hid = (_silu(gate) * up).astype(BF16)
        out = jnp.dot(hid, wd_s[...], preferred_element_type=F32)
        y_ref[...] = _pack_pair(out[:, :HALF], out[:, HALF:])


def _expert_ffn(blk_e, nused, xs, w_gate, w_up, w_down):
    P = xs.shape[0]
    bm = EXPERT_BLOCK
    nb = P // bm

    def row_map(i, blk_e_ref, nused_ref):
        return (jnp.minimum(i, nused_ref[0] - 1), 0)

    def w_map(i, blk_e_ref, nused_ref):
        return (blk_e_ref[jnp.minimum(i, nused_ref[0] - 1)], 0, 0)

    grid_spec = pltpu.PrefetchScalarGridSpec(
        num_scalar_prefetch=2,
        grid=(nb,),
        in_specs=[pl.BlockSpec((bm, HALF), row_map),
                  pl.BlockSpec((1, D_MODEL, EXPERT_FF), w_map),
                  pl.BlockSpec((1, D_MODEL, EXPERT_FF), w_map),
                  pl.BlockSpec((1, EXPERT_FF, D_MODEL), w_map)],
        out_specs=pl.BlockSpec((bm, HALF), row_map),
        scratch_shapes=[pltpu.VMEM((D_MODEL, EXPERT_FF), BF16),
                        pltpu.VMEM((D_MODEL, EXPERT_FF), BF16),
                        pltpu.VMEM((EXPERT_FF, D_MODEL), BF16)],
    )
    return pl.pallas_call(
        _ffn_kernel,
        grid_spec=grid_spec,
        out_shape=jax.ShapeDtypeStruct((P, HALF), jnp.uint32),
        compiler_params=pltpu.CompilerParams(
            dimension_semantics=("arbitrary",), vmem_limit_bytes=VMEM_LIMIT),
        name="routed_experts",
    )(blk_e, nused, xs, w_gate, w_up, w_down)


def _final_kernel(yg_ref, w_ref, h2_ref, x1_ref, mod_ref, wsg_ref, wsu_ref, wsd_ref, gpost_ref, o_ref):
    lo, hi = _unpack_pair(h2_ref[...])
    lo = lo.astype(BF16)
    hi = hi.astype(BF16)
    gate = (jnp.dot(lo, wsg_ref[0:HALF, :], preferred_element_type=F32)
            + jnp.dot(hi, wsg_ref[HALF:, :], preferred_element_type=F32))
    up = (jnp.dot(lo, wsu_ref[0:HALF, :], preferred_element_type=F32)
          + jnp.dot(hi, wsu_ref[HALF:, :], preferred_element_type=F32))
    shared = jnp.dot((_silu(gate) * up).astype(BF16), wsd_ref[...], preferred_element_type=F32)
    y_lo = shared[:, :HALF]
    y_hi = shared[:, HALF:]
    for k in range(TOP_K):
        r_lo, r_hi = _unpack_pair(yg_ref[k])
        wk = w_ref[:, k:k + 1]
        y_lo = y_lo + wk * r_lo
        y_hi = y_hi + wk * r_hi
    ms = (jnp.sum(y_lo * y_lo, axis=-1, keepdims=True)
          + jnp.sum(y_hi * y_hi, axis=-1, keepdims=True)) * (1.0 / D_MODEL)
    inv = lax.rsqrt(ms + NORM_EPS)
    o_ref[:, 0:HALF] = x1_ref[:, 0:HALF] + mod_ref[0, 5:6, 0:HALF] * (y_lo * inv * gpost_ref[:, 0:HALF])
    o_ref[:, HALF:] = x1_ref[:, HALF:] + mod_ref[0, 5:6, HALF:] * (y_hi * inv * gpost_ref[:, HALF:])


def _final(yg, w_tk, h2p, x1, mod3, wsg, wsu, wsd, g_post, seq):
    T = x1.shape[0]
    tm = 256
    per_b = seq // tm
    full = lambda shape: pl.BlockSpec(shape, lambda i: (0,) * len(shape))
    return pl.pallas_call(
        _final_kernel,
        grid=(T // tm,),
        in_specs=[pl.BlockSpec((TOP_K, tm, HALF), lambda i: (0, i, 0)),
                  pl.BlockSpec((tm, TOP_K), lambda i: (i, 0)),
                  pl.BlockSpec((tm, HALF), lambda i: (i, 0)),
                  pl.BlockSpec((tm, D_MODEL), lambda i: (i, 0)),
                  pl.BlockSpec((1, 6, D_MODEL), lambda i: (i // per_b, 0, 0)),
                  full((D_MODEL, EXPERT_FF)), full((D_MODEL, EXPERT_FF)), full((EXPERT_FF, D_MODEL)),
                  full((1, D_MODEL))],
        out_specs=pl.BlockSpec((tm, D_MODEL), lambda i: (i, 0)),
        out_shape=jax.ShapeDtypeStruct((T, D_MODEL), F32),
        compiler_params=pltpu.CompilerParams(
            dimension_semantics=("arbitrary",), vmem_limit_bytes=VMEM_LIMIT),
        name="shared_expert_combine",
    )(yg, w_tk, h2p, x1, mod3, wsg, wsu, wsd, g_post)


def _rope_tables(positions):
    inv = jnp.power(ROPE_THETA, -jnp.arange(ROPE_HALF, dtype=F32) / ROPE_HALF)
    ang = positions.astype(F32)[..., None] * inv
    cos, sin = jnp.cos(ang), jnp.sin(ang)
    rest = ATT_HEAD_DIM - 2 * ROPE_HALF
    cs = jnp.concatenate([cos, cos, jnp.ones(ang.shape[:-1] + (rest,), F32)], axis=-1)
    sn = jnp.concatenate([-sin, sin, jnp.zeros(ang.shape[:-1] + (rest,), F32)], axis=-1)
    return jnp.tile(cs, (1, 1, 2)), jnp.tile(sn, (1, 1, 2))


def _layer(x, c, positions, w_ada, b_ada, g_pre_mix, g_post_mix, g_pre_ffn, g_post_ffn,
           w_in, conv_w, conv_b, b_gates, g_mlstm, w_branch_a, w_branch_b, w_out,
           router_w, router_bias, w_exp_gate, w_exp_up, w_exp_down, w_sh_gate, w_sh_up, w_sh_down):
    B, S, D = x.shape
    T = B * S
    H = MLSTM_HEADS
    x2 = x.reshape(T, D)

    mod3 = _adaln(c, w_ada, b_ada).reshape(B, 6, D)

    a_w = 3 * ATT_GROUP_W
    o_mq = 3 * a_w
    o_mk = o_mq + H * MLSTM_QK_DIM
    o_mv = o_mk + H * MLSTM_QK_DIM
    o_mo = o_mv + H * MLSTM_V_DIM
    o_mi = o_mo + H * MLSTM_V_DIM
    o_ga = o_mi + 2 * H
    o_gb = o_ga + D
    seg = lambda o, w: w_in[:, o:o + w]
    w_main = jnp.concatenate(
        [seg(o_mv, H * MLSTM_V_DIM), seg(o_mo, H * MLSTM_V_DIM), seg(o_ga, D), seg(o_gb, D),
         seg(o_mq, H * MLSTM_QK_DIM), seg(o_mk, H * MLSTM_QK_DIM),
         seg(0, a_w), seg(a_w, a_w), seg(2 * a_w, a_w)], axis=1).astype(BF16)
    w_if = jnp.pad(seg(o_mi, 2 * H), ((0, 0), (0, 128 - 2 * H))).astype(BF16)

    proj, gates = _in_proj(x2, mod3, g_pre_mix.reshape(1, D), w_main, w_if, S)
    proj3 = proj.reshape(B, S, PROJ_W)

    cs, sn = _rope_tables(positions)
    y_a = _attention(proj3, cs, sn)

    bg_row = jnp.pad(b_gates.reshape(1, 2 * H), ((0, 0), (0, 128 - 2 * H)))
    gates_t = gates[:, :2 * H].reshape(B, S, 2 * H).transpose(0, 2, 1)
    gates_t = gates_t.reshape(B, 2 * H, S // MLSTM_BLOCK, MLSTM_BLOCK)
    y_b = _mlstm(proj3, gates_t, conv_w, conv_b.reshape(1, -1), bg_row, g_mlstm.reshape(1, -1))

    x1, h2p = _merge(y_a.reshape(T, ATT_GROUP_W), y_b.reshape(T, D), proj, x2, mod3,
                     w_branch_a.astype(BF16), w_branch_b.astype(BF16), w_out.astype(BF16),
                     g_post_mix.reshape(1, D), g_pre_ffn.reshape(1, D), S)

    rw_t = router_w.T.astype(BF16)
    bias_col = jnp.broadcast_to(router_bias.reshape(N_EXPERTS, 1), (N_EXPERTS, 128))
    idx, wts, rank, cnt = _router(h2p, rw_t[:, :HALF], rw_t[:, HALF:], bias_col)

    bm = EXPERT_BLOCK
    nb = (T * TOP_K) // bm + N_EXPERTS
    counts = cnt[:, 0].astype(jnp.int32)
    padded = (counts + bm - 1) // bm * bm
    pend = jnp.cumsum(padded)
    pstart = pend - padded
    pstart_col = jnp.broadcast_to(pstart.astype(F32).reshape(N_EXPERTS, 1), (N_EXPERTS, 128))
    dest = _slot_index(idx, rank, pstart_col)
    nused = (pend[-1] // bm).astype(jnp.int32).reshape(1)
    blk_e = jnp.minimum(
        jnp.searchsorted(pend, jnp.arange(nb, dtype=jnp.int32) * bm, side="right"),
        N_EXPERTS - 1).astype(jnp.int32)

    xs = _dispatch(h2p, dest, nb * bm)
    ys = _expert_ffn(blk_e, nused, xs, w_exp_gate, w_exp_up, w_exp_down)
    yg = _collect(ys, dest)

    out = _final(yg, wts.T, h2p, x1, mod3, w_sh_gate.astype(BF16), w_sh_up.astype(BF16),
                 w_sh_down.astype(BF16), g_post_ffn.reshape(1, D), S)
    return out.reshape(B, S, D)


SC_CORES = 2
SC_SUBCORES = 16
SC_WORKERS = SC_CORES * SC_SUBCORES
SC_ROWS = 64


def _sc_mesh():
    return plsc.VectorSubcoreMesh(core_axis_name="c", subcore_axis_name="s",
                                  num_cores=SC_CORES, num_subcores=SC_SUBCORES)


def _worker_id():
    return lax.axis_index("s") * SC_CORES + lax.axis_index("c")


def _dispatch(h2p, dest, n_slots):
    T = h2p.shape[0]
    per_w = T // SC_WORKERS
    nch = per_w // SC_ROWS
    idx = dest.reshape(TOP_K, SC_WORKERS, nch, SC_ROWS).transpose(1, 2, 0, 3)
    idx = idx.reshape(SC_WORKERS, nch * TOP_K, SC_ROWS)

    def body(x_hbm, idx_hbm, xs_hbm, idx_v, buf0, buf1, rsem0, rsem1, ssem0, ssem1):
        wid = _worker_id()
        base = wid * per_w
        pltpu.sync_copy(idx_hbm.at[wid], idx_v)
        bufs = ((buf0, rsem0, ssem0), (buf1, rsem1, ssem1))

        def read(c, buf, rsem):
            return pltpu.make_async_copy(x_hbm.at[pl.ds(base + c * SC_ROWS, SC_ROWS)], buf, rsem)

        def scatter(c, k, buf, ssem):
            return pltpu.make_async_copy(buf, xs_hbm.at[idx_v.at[c * TOP_K + k]], ssem)

        read(0, buf0, rsem0).start()

        @pl.loop(0, nch, step=2)
        def _(c0):
            for b in range(2):
                c = c0 + b
                buf, rsem, ssem = bufs[b]
                obuf, orsem, ossem = bufs[1 - b]
                read(c, buf, rsem).wait()

                @pl.when(c > 0)
                def _():
                    for k in range(TOP_K):
                        scatter(c - 1, k, obuf, ossem).wait()

                @pl.when(c + 1 < nch)
                def _():
                    read(c + 1, obuf, orsem).start()

                for k in range(TOP_K):
                    scatter(c, k, buf, ssem).start()

        for k in range(TOP_K):
            scatter(nch - 1, k, buf1, ssem1).wait()

    run = pl.kernel(
        body,
        out_type=jax.ShapeDtypeStruct((n_slots, HALF), jnp.uint32),
        mesh=_sc_mesh(),
        scratch_types=[pltpu.VMEM((nch * TOP_K, SC_ROWS), jnp.int32),
                       pltpu.VMEM((SC_ROWS, HALF), jnp.uint32),
                       pltpu.VMEM((SC_ROWS, HALF), jnp.uint32),
                       pltpu.SemaphoreType.DMA, pltpu.SemaphoreType.DMA,
                       pltpu.SemaphoreType.DMA, pltpu.SemaphoreType.DMA],
        name="sc_dispatch",
    )
    return run(h2p, idx)


def _collect(ys, dest):
    n = dest.size
    per_w = n // SC_WORKERS
    nch = per_w // SC_ROWS
    idx = dest.reshape(SC_WORKERS, nch, SC_ROWS)

    def body(ys_hbm, idx_hbm, out_hbm, idx_v, buf0, buf1, gsem0, gsem1, wsem0, wsem1):
        wid = _worker_id()
        base = wid * per_w
        pltpu.sync_copy(idx_hbm.at[wid], idx_v)
        bufs = ((buf0, gsem0, wsem0), (buf1, gsem1, wsem1))

        def gather(c, buf, gsem):
            return pltpu.make_async_copy(ys_hbm.at[idx_v.at[c]], buf, gsem)

        def write(c, buf, wsem):
            return pltpu.make_async_copy(buf, out_hbm.at[pl.ds(base + c * SC_ROWS, SC_ROWS)], wsem)

        gather(0, buf0, gsem0).start()

        @pl.loop(0, nch, step=2)
        def _(c0):
            for b in range(2):
                c = c0 + b
                buf, gsem, wsem = bufs[b]
                obuf, ogsem, owsem = bufs[1 - b]
                gather(c, buf, gsem).wait()

                @pl.when(c > 0)
                def _():
                    write(c - 1, obuf, owsem).wait()

                @pl.when(c + 1 < nch)
                def _():
                    gather(c + 1, obuf, ogsem).start()

                write(c, buf, wsem).start()

        write(nch - 1, buf1, wsem1).wait()

    run = pl.kernel(
        body,
        out_type=jax.ShapeDtypeStruct((n, HALF), jnp.uint32),
        mesh=_sc_mesh(),
        scratch_types=[pltpu.VMEM((nch, SC_ROWS), jnp.int32),
                       pltpu.VMEM((SC_ROWS, HALF), jnp.uint32),
                       pltpu.VMEM((SC_ROWS, HALF), jnp.uint32),
                       pltpu.SemaphoreType.DMA, pltpu.SemaphoreType.DMA,
                       pltpu.SemaphoreType.DMA, pltpu.SemaphoreType.DMA],
        name="sc_collect",
    )
    return run(ys, idx).reshape(dest.shape + (HALF,))


def kernel(x, c, positions, w_ada, b_ada, g_pre_mix, g_post_mix, g_pre_ffn, g_post_ffn, w_in, conv_w, conv_b, b_gates, g_mlstm, w_branch_a, w_branch_b, w_out, router_w, router_bias, w_exp_gate, w_exp_up, w_exp_down, w_sh_gate, w_sh_up, w_sh_down):
    depth = w_ada.shape[0]
    for l in range(depth):
        x = _layer(x, c, positions, w_ada[l], b_ada[l], g_pre_mix[l], g_post_mix[l], g_pre_ffn[l],
                   g_post_ffn[l], w_in[l], conv_w[l], conv_b[l], b_gates[l], g_mlstm[l],
                   w_branch_a[l], w_branch_b[l], w_out[l], router_w[l], router_bias[l],
                   w_exp_gate[l], w_exp_up[l], w_exp_down[l], w_sh_gate[l], w_sh_up[l], w_sh_down[l])
    return x
```

```python
import functools

import jax
import jax.numpy as jnp
from jax import lax
from jax.experimental import pallas as pl
from jax.experimental.pallas import tpu as pltpu
from jax.experimental.pallas import tpu_sc as plsc

F32 = jnp.float32
BF16 = jnp.bfloat16
HIGHEST = lax.Precision.HIGHEST

D_MODEL = 1024
ATT_GROUPS = ((128, 1), (512, 4), (2048, 16))
ATT_HEAD_DIM = 64
ATT_GROUP_W = 256
ATT_BLK = 128
ROPE_THETA = 500000.0
ROPE_HALF = 8
MLSTM_HEADS = 4
MLSTM_QK_DIM = 128
MLSTM_V_DIM = 256
MLSTM_BLOCK = 128
CONV_WIDTH = 4
N_EXPERTS = 256
TOP_K = 8
N_GROUPS = 8
TOPK_GROUPS = 4
EXPERT_FF = 256
ROUTED_SCALE = 2.5
NORM_EPS = 1e-6
NEG = -1e30

OFF_MV, OFF_MO, OFF_GA, OFF_GB = 0, 1024, 2048, 3072
OFF_MQ, OFF_MK = 4096, 4608
OFF_AQ, OFF_AK, OFF_AV = 5120, 5888, 6656
PROJ_W = 7424
HALF = D_MODEL // 2

EXPERT_BLOCK = 512
VMEM_LIMIT = 56 * 1024 * 1024


def _nt(a, b, precision=None):
    return lax.dot_general(a, b, (((1,), (1,)), ((), ())), preferred_element_type=F32,
                           precision=precision)


def _tn(a, b):
    return lax.dot_general(a, b, (((0,), (0,)), ((), ())), preferred_element_type=F32)


def _silu(x):
    return x * jax.nn.sigmoid(x)


def _pack_pair(lo, hi):
    lo_b = pltpu.bitcast(lo.astype(BF16).astype(F32), jnp.uint32)
    hi_b = pltpu.bitcast(hi.astype(BF16).astype(F32), jnp.uint32)
    return (lo_b >> 16) | (hi_b & jnp.uint32(0xFFFF0000))


def _unpack_pair(w):
    lo = pltpu.bitcast(w << 16, F32)
    hi = pltpu.bitcast(w & jnp.uint32(0xFFFF0000), F32)
    return lo, hi


def _mod_kernel(c_ref, w_ref, b_ref, o_ref):
    a = _silu(c_ref[...])
    o_ref[...] = jnp.dot(a, w_ref[...], preferred_element_type=F32, precision=HIGHEST) + b_ref[...]


def _adaln(c, w_ada, b_ada):
    B = c.shape[0]
    n = w_ada.shape[1]
    tn = 512
    return pl.pallas_call(
        _mod_kernel,
        grid=(n // tn,),
        in_specs=[pl.BlockSpec((B, D_MODEL), lambda j: (0, 0)),
                  pl.BlockSpec((D_MODEL, tn), lambda j: (0, j)),
                  pl.BlockSpec((1, tn), lambda j: (0, j))],
        out_specs=pl.BlockSpec((B, tn), lambda j: (0, j)),
        out_shape=jax.ShapeDtypeStruct((B, n), F32),
        name="adaln_mod",
    )(c, w_ada, b_ada.reshape(1, n))


def _proj_kernel(x_ref, mod_ref, g_ref, w_ref, wif_ref, o_ref, gates_ref, h_ref):
    @pl.when(pl.program_id(1) == 0)
    def _():
        x = x_ref[...]
        ms = jnp.mean(x * x, axis=-1, keepdims=True)
        y = x * lax.rsqrt(ms + NORM_EPS) * g_ref[...]
        h = (y * (1.0 + mod_ref[0, 1:2, :]) + mod_ref[0, 0:1, :]).astype(BF16)
        h_ref[...] = h
        gates_ref[...] = jnp.dot(h, wif_ref[...], preferred_element_type=F32)

    o_ref[...] = jnp.dot(h_ref[...], w_ref[...], preferred_element_type=F32).astype(BF16)


def _in_proj(x2, mod3, g_pre, w_main, w_if, seq):
    T = x2.shape[0]
    tm, tn = 1024, PROJ_W // 2
    per_b = seq // tm
    return pl.pallas_call(
        _proj_kernel,
        grid=(T // tm, PROJ_W // tn),
        in_specs=[pl.BlockSpec((tm, D_MODEL), lambda i, j: (i, 0)),
                  pl.BlockSpec((1, 6, D_MODEL), lambda i, j: (i // per_b, 0, 0)),
                  pl.BlockSpec((1, D_MODEL), lambda i, j: (0, 0)),
                  pl.BlockSpec((D_MODEL, tn), lambda i, j: (0, j)),
                  pl.BlockSpec((D_MODEL, 128), lambda i, j: (0, 0))],
        out_specs=[pl.BlockSpec((tm, tn), lambda i, j: (i, j)),
                   pl.BlockSpec((tm, 128), lambda i, j: (i, 0))],
        out_shape=[jax.ShapeDtypeStruct((T, PROJ_W), BF16),
                   jax.ShapeDtypeStruct((T, 128), F32)],
        scratch_shapes=[pltpu.VMEM((tm, D_MODEL), BF16)],
        compiler_params=pltpu.CompilerParams(
            dimension_semantics=("arbitrary", "arbitrary"), vmem_limit_bytes=VMEM_LIMIT),
        name="norm_in_proj",
    )(x2, mod3, g_pre, w_main, w_if)


def _attn_kernel(q_ref, k_ref, v_ref, cs_ref, sn_ref, o_ref, qf, kf, vf, acc, m_s, l_s, *, seq):
    g = pl.program_id(1)
    lane = lax.broadcasted_iota(jnp.int32, (ATT_BLK, 128), 1)
    first = (lane % ATT_HEAD_DIM) < ROPE_HALF
    low_head = lane < ATT_HEAD_DIM

    def rope(x, cs, sn):
        partner = jnp.where(first, pltpu.roll(x, 128 - ROPE_HALF, 1), pltpu.roll(x, ROPE_HALF, 1))
        return x * cs + partner * sn

    def zero_pad(i, _):
        rows = pl.ds(pl.multiple_of(i * ATT_BLK, ATT_BLK), ATT_BLK)
        for hp in range(2):
            kf[hp, rows, :] = jnp.zeros((ATT_BLK, 128), F32)
            vf[hp, rows, :] = jnp.zeros((ATT_BLK, 128), F32)
        return 0

    lax.fori_loop(0, seq // ATT_BLK, zero_pad, 0)

    def stage(i, _):
        r = pl.multiple_of(i * ATT_BLK, ATT_BLK)
        rows = pl.ds(r, ATT_BLK)
        prow = pl.ds(pl.multiple_of(seq + i * ATT_BLK, ATT_BLK), ATT_BLK)
        cs = cs_ref[0, rows, :]
        sn = sn_ref[0, rows, :]
        for hp in range(2):
            cols = pl.ds(hp * 128, 128)
            qf[hp, rows, :] = rope(q_ref[0, rows, cols].astype(F32), cs, sn) * (ATT_HEAD_DIM ** -0.5)
            kf[hp, prow, :] = rope(k_ref[0, rows, cols].astype(F32), cs, sn)
            vf[hp, prow, :] = v_ref[0, rows, cols].astype(F32)
        return 0

    lax.fori_loop(0, seq // ATT_BLK, stage, 0)

    qi = lax.broadcasted_iota(jnp.int32, (ATT_BLK, 2 * ATT_BLK), 0)
    ki = lax.broadcasted_iota(jnp.int32, (ATT_BLK, 2 * ATT_BLK), 1)
    band = (ki >= qi) & (ki <= qi + ATT_BLK)

    def process(d, init):
        span = ATT_BLK * d

        def body(c, _):
            rho = c % d
            n = c // d
            qstart = rho + n * span
            kstart = seq + qstart - span
            first_key = jnp.where(n > 0, 0, ATT_BLK)
            valid = band & (ki >= first_key)
            qrows = pl.ds(qstart, ATT_BLK, stride=d) if d > 1 else pl.ds(qstart, ATT_BLK)
            krows = pl.ds(kstart, 2 * ATT_BLK, stride=d) if d > 1 else pl.ds(kstart, 2 * ATT_BLK)
            for hp in range(2):
                q2 = qf[hp, qrows, :]
                k2 = kf[hp, krows, :].astype(BF16)
                v2 = vf[hp, krows, :].astype(BF16)
                res = []
                for hh in range(2):
                    hm = low_head if hh == 0 else jnp.logical_not(low_head)
                    qh = jnp.where(hm, q2, 0.0).astype(BF16)
                    s = jnp.where(valid, _nt(qh, k2), NEG)
                    m = jnp.max(s, axis=1, keepdims=True)
                    p = jnp.exp(s - m)
                    l = jnp.sum(p, axis=1, keepdims=True)
                    o = jnp.dot(p.astype(BF16), v2, preferred_element_type=F32)
                    res.append((o, m, l))
                o_b = jnp.where(low_head, res[0][0], res[1][0])
                m_b = jnp.where(low_head, res[0][1], res[1][1])
                l_b = jnp.where(low_head, res[0][2], res[1][2])
                if init:
                    acc[hp, qrows, :] = o_b
                    m_s[hp, qrows, :] = m_b
                    l_s[hp, qrows, :] = l_b
                else:
                    m_old = m_s[hp, qrows, :]
                    m_new = jnp.maximum(m_old, m_b)
                    a_old = jnp.exp(m_old - m_new)
                    a_new = jnp.exp(m_b - m_new)
                    acc[hp, qrows, :] = acc[hp, qrows, :] * a_old + o_b * a_new
                    l_s[hp, qrows, :] = l_s[hp, qrows, :] * a_old + l_b * a_new
                    m_s[hp, qrows, :] = m_new
            return 0

        lax.fori_loop(0, seq // ATT_BLK, body, 0)

    for gi, (_, d) in enumerate(ATT_GROUPS):
        @pl.when(g == gi)
        def _(d=d, gi=gi):
            process(d, gi == 0)

    @pl.when(g == len(ATT_GROUPS) - 1)
    def _():
        def fin(i, _):
            rows = pl.ds(pl.multiple_of(i * ATT_BLK, ATT_BLK), ATT_BLK)
            for hp in range(2):
                o_ref[0, rows, pl.ds(hp * 128, 128)] = (acc[hp, rows, :] / l_s[hp, rows, :]).astype(BF16)
            return 0

        lax.fori_loop(0, seq // ATT_BLK, fin, 0)


def _attention(proj3, cs, sn):
    B, S, _ = proj3.shape
    ng = len(ATT_GROUPS)
    qb, kb, vb = OFF_AQ // ATT_GROUP_W, OFF_AK // ATT_GROUP_W, OFF_AV // ATT_GROUP_W
    return pl.pallas_call(
        functools.partial(_attn_kernel, seq=S),
        grid=(B, ng),
        in_specs=[pl.BlockSpec((1, S, ATT_GROUP_W), lambda b, g: (b, 0, qb + g)),
                  pl.BlockSpec((1, S, ATT_GROUP_W), lambda b, g: (b, 0, kb + g)),
                  pl.BlockSpec((1, S, ATT_GROUP_W), lambda b, g: (b, 0, vb + g)),
                  pl.BlockSpec((1, S, 128), lambda b, g: (b, 0, 0)),
                  pl.BlockSpec((1, S, 128), lambda b, g: (b, 0, 0))],
        out_specs=pl.BlockSpec((1, S, ATT_GROUP_W), lambda b, g: (b, 0, 0)),
        out_shape=jax.ShapeDtypeStruct((B, S, ATT_GROUP_W), BF16),
        scratch_shapes=[pltpu.VMEM((2, S, 128), F32),
                        pltpu.VMEM((2, 2 * S, 128), F32),
                        pltpu.VMEM((2, 2 * S, 128), F32),
                        pltpu.VMEM((2, S, 128), F32),
                        pltpu.VMEM((2, S, 128), F32),
                        pltpu.VMEM((2, S, 128), F32)],
        compiler_params=pltpu.CompilerParams(
            dimension_semantics=("arbitrary", "arbitrary"), vmem_limit_bytes=VMEM_LIMIT),
        name="dilated_attention",
    )(proj3, proj3, proj3, cs, sn)


def _log_sigmoid(x):
    return jnp.minimum(x, 0.0) - jnp.log(1.0 + jnp.exp(-jnp.abs(x)))


def _mlstm_kernel(mq_ref, mk_ref, mv_ref, mo_ref, gt_ref, cwq_ref, cwk_ref, cbq_ref, cbk_ref,
                  bg_ref, gm_ref, o_ref, pad_ref, q_s, k_s, va_s, rows_s, acc_s, kv_s, inter_s, emt_s,
                  c_s, *, seq):
    h = pl.program_id(1)
    L = MLSTM_BLOCK
    NC = seq // L
    DK, DV = MLSTM_QK_DIM, MLSTM_V_DIM
    DA = DV + 128
    halo = 8

    pad_ref[0:halo, :] = jnp.zeros((halo, 2 * DK), F32)
    for i in range(NC):
        pad_ref[halo + i * L:halo + (i + 1) * L, 0:DK] = mq_ref[0, i * L:(i + 1) * L, :].astype(F32)
        pad_ref[halo + i * L:halo + (i + 1) * L, DK:2 * DK] = mk_ref[0, i * L:(i + 1) * L, :].astype(F32)
        va_s[i * L:(i + 1) * L, 0:DV] = mv_ref[0, i * L:(i + 1) * L, :]
        va_s[i * L:(i + 1) * L, DV:DA] = jnp.ones((L, DA - DV), BF16)
    for i in range(NC):
        yq = jnp.broadcast_to(cbq_ref[...], (L, DK))
        yk = jnp.broadcast_to(cbk_ref[...], (L, DK))
        for j in range(CONV_WIDTH):
            r0 = halo + i * L - (CONV_WIDTH - 1) + j
            yq = yq + pad_ref[r0:r0 + L, 0:DK] * cwq_ref[j:j + 1, :]
            yk = yk + pad_ref[r0:r0 + L, DK:2 * DK] * cwk_ref[j:j + 1, :]
        q_s[i * L:(i + 1) * L, :] = _silu(yq).astype(BF16)
        k_s[i * L:(i + 1) * L, :] = (_silu(yk) * (DK ** -0.5)).astype(BF16)

    lane = lax.broadcasted_iota(jnp.int32, (1, 128), 1)
    bias = bg_ref[...]
    b_i = jnp.sum(jnp.where(lane == h, bias, 0.0), axis=1, keepdims=True)
    b_f = jnp.sum(jnp.where(lane == h + MLSTM_HEADS, bias, 0.0), axis=1, keepdims=True)
    ri = lax.broadcasted_iota(jnp.int32, (L, L), 0)
    ci = lax.broadcasted_iota(jnp.int32, (L, L), 1)
    causal = ci <= ri
    eye = (ri == ci).astype(F32)
    i_rows = gt_ref[0, h] + b_i
    lf_rows = _log_sigmoid(gt_ref[0, h + MLSTM_HEADS] + b_f)
    b_rows = jnp.dot(lf_rows, (ri <= ci).astype(F32), preferred_element_type=F32,
                     precision=HIGHEST)
    b_end = b_rows[:, L - 1:L]
    g_rows = b_end - b_rows + i_rows
    g_max = jnp.max(g_rows, axis=1, keepdims=True)
    m = jnp.zeros((1, 1), F32)
    m_prev, m_new = [], []
    for c in range(NC):
        m_prev.append(m)
        m = jnp.maximum(b_end[c:c + 1, :] + m, g_max[c:c + 1, :])
        m_new.append(m)
    m_prev = jnp.concatenate(m_prev, axis=0)
    m_new = jnp.concatenate(m_new, axis=0)
    rows_s[0] = b_rows
    rows_s[1] = jnp.exp(g_rows - m_new)
    rows_s[2] = b_rows - i_rows
    rows_s[3] = jnp.broadcast_to(m_prev, (NC, L))
    rows_s[4] = jnp.broadcast_to(jnp.exp(b_end + m_prev - m_new), (NC, L))

    r2 = lax.broadcasted_iota(jnp.int32, (2 * L, 2 * L), 0)
    c2 = lax.broadcasted_iota(jnp.int32, (2 * L, 2 * L), 1)
    ones_blk = ((r2 < L) == (c2 < L)).astype(BF16)

    def local(c, _):
        rows = pl.ds(pl.multiple_of(c * L, L), L)
        crow = pl.ds(c, 1)
        b_r = rows_s[0, crow, :]
        w_r = rows_s[1, crow, :]
        u_r = rows_s[2, crow, :]
        mp = rows_s[3, crow, :]
        x2 = jnp.concatenate([eye * b_r, eye * w_r], axis=1)
        hi = x2.astype(BF16)
        lo = (x2 - hi.astype(F32)).astype(BF16)
        yb = (jnp.dot(hi, ones_blk, preferred_element_type=F32)
              + jnp.dot(lo, ones_blk, preferred_element_type=F32))
        b_b = yb[:, 0:L]
        w_b = yb[:, L:2 * L]
        dmat = jnp.where(causal, b_b - u_r, NEG)
        m_t = jnp.maximum(b_b + mp, jnp.max(dmat, axis=1, keepdims=True))
        q = q_s[rows, :]
        k = k_s[rows, :]
        va = va_s[rows, :]
        sc = _nt(q, k) * jnp.exp(dmat - m_t)
        acc_s[rows, :] = jnp.dot(sc.astype(BF16), va, preferred_element_type=F32)
        kv_s[c] = _tn((w_b * k.astype(F32)).astype(BF16), va)
        inter_s[rows, :] = jnp.exp(b_b + mp - m_t)
        emt_s[rows, :] = jnp.exp(-m_t)
        return 0

    lax.fori_loop(0, NC, local, 0, unroll=2)

    g_row = gm_ref[...]
    c_s[...] = jnp.zeros((DK, DA), F32)

    def recur(c, _):
        rows = pl.ds(pl.multiple_of(c * L, L), L)
        state = c_s[...]
        read = jnp.dot(q_s[rows, :], state.astype(BF16), preferred_element_type=F32)
        inter = inter_s[rows, :]
        out = acc_s[rows, :] + jnp.concatenate([inter, inter, inter], axis=1) * read
        den = out[:, DV:DA]
        emt = emt_s[rows, :]
        nrm = jnp.maximum(jnp.abs(jnp.concatenate([den, den], axis=1)),
                          jnp.concatenate([emt, emt], axis=1))
        hh = out[:, 0:DV] / nrm
        ms = jnp.mean(hh * hh, axis=1, keepdims=True)
        hn = hh * lax.rsqrt(ms + NORM_EPS) * g_row
        o_ref[0, rows, :] = (hn * jax.nn.sigmoid(mo_ref[0, rows, :].astype(F32))).astype(BF16)
        dec = rows_s[4, pl.ds(c, 1), :]
        c_s[...] = jnp.concatenate([dec, dec, dec], axis=1) * state + kv_s[c]
        return 0

    lax.fori_loop(0, NC, recur, 0, unroll=2)


def _mlstm(proj3, gates_t, conv_w, conv_b, bg_row, g_mlstm):
    B, S, _ = proj3.shape
    H, DK, DV = MLSTM_HEADS, MLSTM_QK_DIM, MLSTM_V_DIM
    L = MLSTM_BLOCK
    NC = S // L
    DA = DV + 128
    qb, kb = OFF_MQ // DK, OFF_MK // DK
    vb, ob = OFF_MV // DV, OFF_MO // DV
    nq = (H * DK) // DK
    return pl.pallas_call(
        functools.partial(_mlstm_kernel, seq=S),
        grid=(B, H),
        in_specs=[pl.BlockSpec((1, S, DK), lambda b, h: (b, 0, qb + h)),
                  pl.BlockSpec((1, S, DK), lambda b, h: (b, 0, kb + h)),
                  pl.BlockSpec((1, S, DV), lambda b, h: (b, 0, vb + h)),
                  pl.BlockSpec((1, S, DV), lambda b, h: (b, 0, ob + h)),
                  pl.BlockSpec((1, 2 * H, NC, L), lambda b, h: (b, 0, 0, 0)),
                  pl.BlockSpec((CONV_WIDTH, DK), lambda b, h: (0, h)),
                  pl.BlockSpec((CONV_WIDTH, DK), lambda b, h: (0, nq + h)),
                  pl.BlockSpec((1, DK), lambda b, h: (0, h)),
                  pl.BlockSpec((1, DK), lambda b, h: (0, nq + h)),
                  pl.BlockSpec((1, 128), lambda b, h: (0, 0)),
                  pl.BlockSpec((1, DV), lambda b, h: (0, h))],
        out_specs=pl.BlockSpec((1, S, DV), lambda b, h: (b, 0, h)),
        out_shape=jax.ShapeDtypeStruct((B, S, H * DV), BF16),
        scratch_shapes=[pltpu.VMEM((S + 8, 2 * DK), F32),
                        pltpu.VMEM((S, DK), BF16),
                        pltpu.VMEM((S, DK), BF16),
                        pltpu.VMEM((S, DA), BF16),
                        pltpu.VMEM((5, NC, L), F32),
                        pltpu.VMEM((S, DA), F32),
                        pltpu.VMEM((NC, DK, DA), F32),
                        pltpu.VMEM((S, L), F32),
                        pltpu.VMEM((S, L), F32),
                        pltpu.VMEM((DK, DA), F32)],
        compiler_params=pltpu.CompilerParams(
            dimension_semantics=("arbitrary", "arbitrary"), vmem_limit_bytes=VMEM_LIMIT),
        name="mlstm_chunkwise",
    )(proj3, proj3, proj3, proj3, gates_t, conv_w, conv_w, conv_b, conv_b, bg_row, g_mlstm)


def _rms(y, g):
    ms = jnp.mean(y * y, axis=-1, keepdims=True)
    return y * lax.rsqrt(ms + NORM_EPS) * g


def _merge_kernel(ya_ref, yb_ref, ga_ref, gb_ref, x_ref, mod_ref, wa_ref, wb_ref, wo_ref,
                  gpost_ref, gpre_ref, x1_ref, h2_ref):
    pa = jnp.dot(ya_ref[...], wa_ref[...], preferred_element_type=F32)
    pb = jnp.dot(yb_ref[...], wb_ref[...], preferred_element_type=F32)
    merged = (jax.nn.sigmoid(ga_ref[...].astype(F32)) * pa
              + jax.nn.sigmoid(gb_ref[...].astype(F32)) * pb)
    y = jnp.dot(merged.astype(BF16), wo_ref[...], preferred_element_type=F32)
    x1 = x_ref[...] + mod_ref[0, 2:3, :] * _rms(y, gpost_ref[...])
    x1_ref[...] = x1
    h2 = _rms(x1, gpre_ref[...]) * (1.0 + mod_ref[0, 4:5, :]) + mod_ref[0, 3:4, :]
    h2_ref[...] = _pack_pair(h2[:, :HALF], h2[:, HALF:])


def _merge(ya2, yb2, proj2, x2, mod3, wa, wb, wo, g_post, g_pre, seq):
    T = x2.shape[0]
    tm = 512
    per_b = seq // tm
    full = lambda shape: pl.BlockSpec(shape, lambda i: (0,) * len(shape))
    return pl.pallas_call(
        _merge_kernel,
        grid=(T // tm,),
        in_specs=[pl.BlockSpec((tm, ATT_GROUP_W), lambda i: (i, 0)),
                  pl.BlockSpec((tm, D_MODEL), lambda i: (i, 0)),
                  pl.BlockSpec((tm, D_MODEL), lambda i: (i, OFF_GA // D_MODEL)),
                  pl.BlockSpec((tm, D_MODEL), lambda i: (i, OFF_GB // D_MODEL)),
                  pl.BlockSpec((tm, D_MODEL), lambda i: (i, 0)),
                  pl.BlockSpec((1, 6, D_MODEL), lambda i: (i // per_b, 0, 0)),
                  full((ATT_GROUP_W, D_MODEL)), full((D_MODEL, D_MODEL)), full((D_MODEL, D_MODEL)),
                  full((1, D_MODEL)), full((1, D_MODEL))],
        out_specs=[pl.BlockSpec((tm, D_MODEL), lambda i: (i, 0)),
                   pl.BlockSpec((tm, HALF), lambda i: (i, 0))],
        out_shape=[jax.ShapeDtypeStruct((T, D_MODEL), F32),
                   jax.ShapeDtypeStruct((T, HALF), jnp.uint32)],
        compiler_params=pltpu.CompilerParams(
            dimension_semantics=("arbitrary",), vmem_limit_bytes=VMEM_LIMIT),
        name="merge_out_proj",
    )(ya2, yb2, proj2, proj2, x2, mod3, wa, wb, wo, g_post, g_pre)


def _router_kernel(h2_ref, rlo_ref, rhi_ref, bias_ref, idx_ref, w_ref, rank_ref, cnt_ref):
    E = N_EXPERTS
    tr = h2_ref.shape[0]
    gsz = E // N_GROUPS

    @pl.when(pl.program_id(0) == 0)
    def _():
        cnt_ref[...] = jnp.zeros(cnt_ref.shape, F32)

    lo, hi = _unpack_pair(h2_ref[...])
    logits = _nt(rlo_ref[...], lo.astype(BF16)) + _nt(rhi_ref[...], hi.astype(BF16))
    scores = jax.nn.sigmoid(logits)
    sel = scores + bias_ref[:, 0:1]

    gi = lax.broadcasted_iota(jnp.int32, (gsz, tr), 0).astype(F32)
    gs_rows = []
    for g in range(N_GROUPS):
        blk = sel[g * gsz:(g + 1) * gsz, :]
        m1 = jnp.max(blk, axis=0, keepdims=True)
        a1 = jnp.min(jnp.where(blk == m1, gi, float(E)), axis=0, keepdims=True)
        m2 = jnp.max(jnp.where(gi == a1, -jnp.inf, blk), axis=0, keepdims=True)
        gs_rows.append(m1 + m2)
    gs = jnp.concatenate(gs_rows, axis=0)
    g8 = lax.broadcasted_iota(jnp.int32, (N_GROUPS, tr), 0).astype(F32)
    gmask = jnp.zeros((N_GROUPS, tr), F32)
    for _ in range(TOPK_GROUPS):
        m = jnp.max(gs, axis=0, keepdims=True)
        a = jnp.min(jnp.where(gs == m, g8, float(E)), axis=0, keepdims=True)
        hit = g8 == a
        gmask = jnp.where(hit, 1.0, gmask)
        gs = jnp.where(hit, -jnp.inf, gs)
    selm = jnp.concatenate(
        [jnp.where(gmask[g:g + 1, :] > 0.0, sel[g * gsz:(g + 1) * gsz, :], -jnp.inf)
         for g in range(N_GROUPS)], axis=0)

    ei = lax.broadcasted_iota(jnp.int32, (E, tr), 0).astype(F32)
    picks, weights = [], []
    chosen = jnp.zeros((E, tr), F32)
    for _ in range(TOP_K):
        m = jnp.max(selm, axis=0, keepdims=True)
        a = jnp.min(jnp.where(selm == m, ei, float(E)), axis=0, keepdims=True)
        hit = ei == a
        picks.append(a)
        weights.append(jnp.sum(jnp.where(hit, scores, 0.0), axis=0, keepdims=True))
        chosen = jnp.where(hit, 1.0, chosen)
        selm = jnp.where(hit, -jnp.inf, selm)
    wsum = weights[0]
    for w in weights[1:]:
        wsum = wsum + w

    ti = lax.broadcasted_iota(jnp.int32, (tr, tr), 0)
    tj = lax.broadcasted_iota(jnp.int32, (tr, tr), 1)
    before = (ti < tj).astype(BF16)
    pos = jnp.dot(chosen.astype(BF16), before, preferred_element_type=F32) + cnt_ref[:, 0:1]
    ranks = [jnp.sum(jnp.where(ei == a, pos, 0.0), axis=0, keepdims=True) for a in picks]
    cnt_ref[...] = cnt_ref[...] + jnp.sum(chosen, axis=1, keepdims=True)

    idx_ref[...] = jnp.concatenate(picks, axis=0).astype(jnp.int32)
    w_ref[...] = jnp.concatenate([w / wsum * ROUTED_SCALE for w in weights], axis=0)
    rank_ref[...] = jnp.concatenate(ranks, axis=0).astype(jnp.int32)


def _router(h2p, r_lo, r_hi, bias_col):
    T = h2p.shape[0]
    tr = 512
    full = lambda shape: pl.BlockSpec(shape, lambda i: (0,) * len(shape))
    return pl.pallas_call(
        _router_kernel,
        grid=(T // tr,),
        in_specs=[pl.BlockSpec((tr, HALF), lambda i: (i, 0)),
                  full((N_EXPERTS, HALF)), full((N_EXPERTS, HALF)), full((N_EXPERTS, 128))],
        out_specs=[pl.BlockSpec((TOP_K, tr), lambda i: (0, i)),
                   pl.BlockSpec((TOP_K, tr), lambda i: (0, i)),
                   pl.BlockSpec((TOP_K, tr), lambda i: (0, i)),
                   full((N_EXPERTS, 128))],
        out_shape=[jax.ShapeDtypeStruct((TOP_K, T), jnp.int32),
                   jax.ShapeDtypeStruct((TOP_K, T), F32),
                   jax.ShapeDtypeStruct((TOP_K, T), jnp.int32),
                   jax.ShapeDtypeStruct((N_EXPERTS, 128), F32)],
        compiler_params=pltpu.CompilerParams(
            dimension_semantics=("arbitrary",), vmem_limit_bytes=VMEM_LIMIT),
        name="router_topk",
    )(h2p, r_lo, r_hi, bias_col)


def _dest_kernel(idx_ref, rank_ref, pstart_ref, dest_ref):
    tr = idx_ref.shape[1]
    ei = lax.broadcasted_iota(jnp.int32, (N_EXPERTS, tr), 0)
    start = pstart_ref[:, 0:1]
    rows = []
    for k in range(TOP_K):
        hit = ei == idx_ref[k:k + 1, :]
        rows.append(jnp.sum(jnp.where(hit, start, 0.0), axis=0, keepdims=True))
    dest_ref[...] = jnp.concatenate(rows, axis=0).astype(jnp.int32) + rank_ref[...]


def _slot_index(idx, rank, pstart_col):
    T = idx.shape[1]
    tr = 1024
    return pl.pallas_call(
        _dest_kernel,
        grid=(T // tr,),
        in_specs=[pl.BlockSpec((TOP_K, tr), lambda i: (0, i)),
                  pl.BlockSpec((TOP_K, tr), lambda i: (0, i)),
                  pl.BlockSpec((N_EXPERTS, 128), lambda i: (0, 0))],
        out_specs=pl.BlockSpec((TOP_K, tr), lambda i: (0, i)),
        out_shape=jax.ShapeDtypeStruct((TOP_K, T), jnp.int32),
        name="slot_index",
    )(idx, rank, pstart_col)


def _ffn_kernel(first_ref, nblk_ref, nused_ref, xs_hbm, wg_ref, wu_ref, wd_ref, ys_hbm,
                xbuf, ybuf, in_sem, out_sem, wg_s, wu_s, wd_s):
    e = pl.program_id(0)
    bm = EXPERT_BLOCK
    nused = nused_ref[0]
    first = first_ref[e]
    n = nblk_ref[e]

    def in_copy(g, slot):
        return pltpu.make_async_copy(xs_hbm.at[pl.ds(g * bm, bm)], xbuf.at[slot], in_sem.at[slot])

    def out_copy(g, slot):
        return pltpu.make_async_copy(ybuf.at[slot], ys_hbm.at[pl.ds(g * bm, bm)], out_sem.at[slot])

    @pl.when((e == 0) & (nused > 0))
    def _():
        in_copy(0, 0).start()

    @pl.when(n > 0)
    def _():
        wg_s[...] = wg_ref[0].astype(BF16)
        wu_s[...] = wu_ref[0].astype(BF16)
        wd_s[...] = wd_ref[0].astype(BF16)

        def block(j, _):
            g = first + j
            slot = g % 2
            in_copy(g, slot).wait()

            @pl.when(g + 1 < nused)
            def _():
                in_copy(g + 1, 1 - slot).start()

            lo, hi = _unpack_pair(xbuf[slot])
            lo = lo.astype(BF16)
            hi = hi.astype(BF16)
            gate = (jnp.dot(lo, wg_s[0:HALF, :], preferred_element_type=F32)
                    + jnp.dot(hi, wg_s[HALF:, :], preferred_element_type=F32))
            up = (jnp.dot(lo, wu_s[0:HALF, :], preferred_element_type=F32)
                  + jnp.dot(hi, wu_s[HALF:, :], preferred_element_type=F32))
            hid = (_silu(gate) * up).astype(BF16)
            out = jnp.dot(hid, wd_s[...], preferred_element_type=F32)

            @pl.when(g >= 2)
            def _():
                out_copy(g - 2, slot).wait()

            ybuf[slot] = _pack_pair(out[:, :HALF], out[:, HALF:])
            out_copy(g, slot).start()
            return 0

        lax.fori_loop(0, n, block, 0)

    @pl.when(e == pl.num_programs(0) - 1)
    def _():
        @pl.when(nused >= 2)
        def _():
            out_copy(nused - 2, (nused - 2) % 2).wait()

        @pl.when(nused >= 1)
        def _():
            out_copy(nused - 1, (nused - 1) % 2).wait()


def _expert_ffn(first_blk, nblk, nused, xs, w_gate, w_up, w_down):
    P = xs.shape[0]
    bm = EXPERT_BLOCK
    w_map = lambda e, *_: (e, 0, 0)
    grid_spec = pltpu.PrefetchScalarGridSpec(
        num_scalar_prefetch=3,
        grid=(w_gate.shape[0],),
        in_specs=[pl.BlockSpec(memory_space=pl.ANY),
                  pl.BlockSpec((1, D_MODEL, EXPERT_FF), w_map),
                  pl.BlockSpec((1, D_MODEL, EXPERT_FF), w_map),
                  pl.BlockSpec((1, EXPERT_FF, D_MODEL), w_map)],
        out_specs=pl.BlockSpec(memory_space=pl.ANY),
        scratch_shapes=[pltpu.VMEM((2, bm, HALF), jnp.uint32),
                        pltpu.VMEM((2, bm, HALF), jnp.uint32),
                        pltpu.SemaphoreType.DMA((2,)),
                        pltpu.SemaphoreType.DMA((2,)),
                        pltpu.VMEM((D_MODEL, EXPERT_FF), BF16),
                        pltpu.VMEM((D_MODEL, EXPERT_FF), BF16),
                        pltpu.VMEM((EXPERT_FF, D_MODEL), BF16)],
    )
    return pl.pallas_call(
        _ffn_kernel,
        grid_spec=grid_spec,
        out_shape=jax.ShapeDtypeStruct((P, HALF), jnp.uint32),
        compiler_params=pltpu.CompilerParams(
            dimension_semantics=("arbitrary",), vmem_limit_bytes=VMEM_LIMIT),
        name="routed_experts",
    )(first_blk, nblk, nused, xs, w_gate, w_up, w_down)


def _final_kernel(yg_ref, w_ref, h2_ref, x1_ref, mod_ref, wsg_ref, wsu_ref, wsd_ref, gpost_ref, o_ref):
    lo, hi = _unpack_pair(h2_ref[...])
    lo = lo.astype(BF16)
    hi = hi.astype(BF16)
    gate = (jnp.dot(lo, wsg_ref[0:HALF, :], preferred_element_type=F32)
            + jnp.dot(hi, wsg_ref[HALF:, :], preferred_element_type=F32))
    up = (jnp.dot(lo, wsu_ref[0:HALF, :], preferred_element_type=F32)
          + jnp.dot(hi, wsu_ref[HALF:, :], preferred_element_type=F32))
    shared = jnp.dot((_silu(gate) * up).astype(BF16), wsd_ref[...], preferred_element_type=F32)
    y_lo = shared[:, :HALF]
    y_hi = shared[:, HALF:]
    for k in range(TOP_K):
        r_lo, r_hi = _unpack_pair(yg_ref[k])
        wk = w_ref[:, k:k + 1]
        y_lo = y_lo + wk * r_lo
        y_hi = y_hi + wk * r_hi
    ms = (jnp.sum(y_lo * y_lo, axis=-1, keepdims=True)
          + jnp.sum(y_hi * y_hi, axis=-1, keepdims=True)) * (1.0 / D_MODEL)
    inv = lax.rsqrt(ms + NORM_EPS)
    o_ref[:, 0:HALF] = x1_ref[:, 0:HALF] + mod_ref[0, 5:6, 0:HALF] * (y_lo * inv * gpost_ref[:, 0:HALF])
    o_ref[:, HALF:] = x1_ref[:, HALF:] + mod_ref[0, 5:6, HALF:] * (y_hi * inv * gpost_ref[:, HALF:])


def _final(yg, w_tk, h2p, x1, mod3, wsg, wsu, wsd, g_post, seq):
    T = x1.shape[0]
    tm = 256
    per_b = seq // tm
    full = lambda shape: pl.BlockSpec(shape, lambda i: (0,) * len(shape))
    return pl.pallas_call(
        _final_kernel,
        grid=(T // tm,),
        in_specs=[pl.BlockSpec((TOP_K, tm, HALF), lambda i: (0, i, 0)),
                  pl.BlockSpec((tm, TOP_K), lambda i: (i, 0)),
                  pl.BlockSpec((tm, HALF), lambda i: (i, 0)),
                  pl.BlockSpec((tm, D_MODEL), lambda i: (i, 0)),
                  pl.BlockSpec((1, 6, D_MODEL), lambda i: (i // per_b, 0, 0)),
                  full((D_MODEL, EXPERT_FF)), full((D_MODEL, EXPERT_FF)), full((EXPERT_FF, D_MODEL)),
                  full((1, D_MODEL))],
        out_specs=pl.BlockSpec((tm, D_MODEL), lambda i: (i, 0)),
        out_shape=jax.ShapeDtypeStruct((T, D_MODEL), F32),
        compiler_params=pltpu.CompilerParams(
            dimension_semantics=("arbitrary",), vmem_limit_bytes=VMEM_LIMIT),
        name="shared_expert_combine",
    )(yg, w_tk, h2p, x1, mod3, wsg, wsu, wsd, g_post)


def _rope_tables(positions):
    inv = jnp.power(ROPE_THETA, -jnp.arange(ROPE_HALF, dtype=F32) / ROPE_HALF)
    ang = positions.astype(F32)[..., None] * inv
    cos, sin = jnp.cos(ang), jnp.sin(ang)
    rest = ATT_HEAD_DIM - 2 * ROPE_HALF
    cs = jnp.concatenate([cos, cos, jnp.ones(ang.shape[:-1] + (rest,), F32)], axis=-1)
    sn = jnp.concatenate([-sin, sin, jnp.zeros(ang.shape[:-1] + (rest,), F32)], axis=-1)
    return jnp.tile(cs, (1, 1, 2)), jnp.tile(sn, (1, 1, 2))


def _layer(x, c, positions, w_ada, b_ada, g_pre_mix, g_post_mix, g_pre_ffn, g_post_ffn,
           w_in, conv_w, conv_b, b_gates, g_mlstm, w_branch_a, w_branch_b, w_out,
           router_w, router_bias, w_exp_gate, w_exp_up, w_exp_down, w_sh_gate, w_sh_up, w_sh_down):
    B, S, D = x.shape
    T = B * S
    H = MLSTM_HEADS
    x2 = x.reshape(T, D)

    mod3 = _adaln(c, w_ada, b_ada).reshape(B, 6, D)

    a_w = 3 * ATT_GROUP_W
    o_mq = 3 * a_w
    o_mk = o_mq + H * MLSTM_QK_DIM
    o_mv = o_mk + H * MLSTM_QK_DIM
    o_mo = o_mv + H * MLSTM_V_DIM
    o_mi = o_mo + H * MLSTM_V_DIM
    o_ga = o_mi + 2 * H
    o_gb = o_ga + D
    seg = lambda o, w: w_in[:, o:o + w]
    w_main = jnp.concatenate(
        [seg(o_mv, H * MLSTM_V_DIM), seg(o_mo, H * MLSTM_V_DIM), seg(o_ga, D), seg(o_gb, D),
         seg(o_mq, H * MLSTM_QK_DIM), seg(o_mk, H * MLSTM_QK_DIM),
         seg(0, a_w), seg(a_w, a_w), seg(2 * a_w, a_w)], axis=1).astype(BF16)
    w_if = jnp.pad(seg(o_mi, 2 * H), ((0, 0), (0, 128 - 2 * H))).astype(BF16)

    proj, gates = _in_proj(x2, mod3, g_pre_mix.reshape(1, D), w_main, w_if, S)
    proj3 = proj.reshape(B, S, PROJ_W)

    cs, sn = _rope_tables(positions)
    y_a = _attention(proj3, cs, sn)

    bg_row = jnp.pad(b_gates.reshape(1, 2 * H), ((0, 0), (0, 128 - 2 * H)))
    gates_t = gates[:, :2 * H].reshape(B, S, 2 * H).transpose(0, 2, 1)
    gates_t = gates_t.reshape(B, 2 * H, S // MLSTM_BLOCK, MLSTM_BLOCK)
    y_b = _mlstm(proj3, gates_t, conv_w, conv_b.reshape(1, -1), bg_row, g_mlstm.reshape(1, -1))

    x1, h2p = _merge(y_a.reshape(T, ATT_GROUP_W), y_b.reshape(T, D), proj, x2, mod3,
                     w_branch_a.astype(BF16), w_branch_b.astype(BF16), w_out.astype(BF16),
                     g_post_mix.reshape(1, D), g_pre_ffn.reshape(1, D), S)

    rw_t = router_w.T.astype(BF16)
    bias_col = jnp.broadcast_to(router_bias.reshape(N_EXPERTS, 1), (N_EXPERTS, 128))
    idx, wts, rank, cnt = _router(h2p, rw_t[:, :HALF], rw_t[:, HALF:], bias_col)

    bm = EXPERT_BLOCK
    nb = (T * TOP_K) // bm + N_EXPERTS
    counts = cnt[:, 0].astype(jnp.int32)
    padded = (counts + bm - 1) // bm * bm
    pend = jnp.cumsum(padded)
    pstart = pend - padded
    pstart_col = jnp.broadcast_to(pstart.astype(F32).reshape(N_EXPERTS, 1), (N_EXPERTS, 128))
    dest = _slot_index(idx, rank, pstart_col)
    nused = (pend[-1] // bm).astype(jnp.int32).reshape(1)

    xs = _dispatch(h2p, dest, nb * bm)
    ys = _expert_ffn((pstart // bm).astype(jnp.int32), (padded // bm).astype(jnp.int32), nused,
                     xs, w_exp_gate, w_exp_up, w_exp_down)
    yg = _collect(ys, dest)

    out = _final(yg, wts.T, h2p, x1, mod3, w_sh_gate.astype(BF16), w_sh_up.astype(BF16),
                 w_sh_down.astype(BF16), g_post_ffn.reshape(1, D), S)
    return out.reshape(B, S, D)


SC_CORES = 2
SC_SUBCORES = 16
SC_WORKERS = SC_CORES * SC_SUBCORES
SC_ROWS = 64


def _sc_mesh():
    return plsc.VectorSubcoreMesh(core_axis_name="c", subcore_axis_name="s",
                                  num_cores=SC_CORES, num_subcores=SC_SUBCORES)


def _worker_id():
    return lax.axis_index("s") * SC_CORES + lax.axis_index("c")


def _dispatch(h2p, dest, n_slots):
    T = h2p.shape[0]
    per_w = T // SC_WORKERS
    nch = per_w // SC_ROWS
    idx = dest.reshape(TOP_K, SC_WORKERS, nch, SC_ROWS).transpose(1, 2, 0, 3)
    idx = idx.reshape(SC_WORKERS, nch * TOP_K, SC_ROWS)

    def body(x_hbm, idx_hbm, xs_hbm, idx_v, buf0, buf1, rsem0, rsem1, ssem0, ssem1):
        wid = _worker_id()
        base = wid * per_w
        pltpu.sync_copy(idx_hbm.at[wid], idx_v)
        bufs = ((buf0, rsem0, ssem0), (buf1, rsem1, ssem1))

        def read(c, buf, rsem):
            return pltpu.make_async_copy(x_hbm.at[pl.ds(base + c * SC_ROWS, SC_ROWS)], buf, rsem)

        def scatter(c, k, buf, ssem):
            return pltpu.make_async_copy(buf, xs_hbm.at[idx_v.at[c * TOP_K + k]], ssem)

        read(0, buf0, rsem0).start()

        @pl.loop(0, nch, step=2)
        def _(c0):
            for b in range(2):
                c = c0 + b
                buf, rsem, ssem = bufs[b]
                obuf, orsem, ossem = bufs[1 - b]
                read(c, buf, rsem).wait()

                @pl.when(c > 0)
                def _():
                    for k in range(TOP_K):
                        scatter(c - 1, k, obuf, ossem).wait()

                @pl.when(c + 1 < nch)
                def _():
                    read(c + 1, obuf, orsem).start()

                for k in range(TOP_K):
                    scatter(c, k, buf, ssem).start()

        for k in range(TOP_K):
            scatter(nch - 1, k, buf1, ssem1).wait()

    run = pl.kernel(
        body,
        out_type=jax.ShapeDtypeStruct((n_slots, HALF), jnp.uint32),
        mesh=_sc_mesh(),
        scratch_types=[pltpu.VMEM((nch * TOP_K, SC_ROWS), jnp.int32),
                       pltpu.VMEM((SC_ROWS, HALF), jnp.uint32),
                       pltpu.VMEM((SC_ROWS, HALF), jnp.uint32),
                       pltpu.SemaphoreType.DMA, pltpu.SemaphoreType.DMA,
                       pltpu.SemaphoreType.DMA, pltpu.SemaphoreType.DMA],
        name="sc_dispatch",
    )
    return run(h2p, idx)


def _collect(ys, dest):
    n = dest.size
    per_w = n // SC_WORKERS
    nch = per_w // SC_ROWS
    idx = dest.reshape(SC_WORKERS, nch, SC_ROWS)

    def body(ys_hbm, idx_hbm, out_hbm, idx_v, buf0, buf1, gsem0, gsem1, wsem0, wsem1):
        wid = _worker_id()
        base = wid * per_w
        pltpu.sync_copy(idx_hbm.at[wid], idx_v)
        bufs = ((buf0, gsem0, wsem0), (buf1, gsem1, wsem1))

        def gather(c, buf, gsem):
            return pltpu.make_async_copy(ys_hbm.at[idx_v.at[c]], buf, gsem)

        def write(c, buf, wsem):
            return pltpu.make_async_copy(buf, out_hbm.at[pl.ds(base + c * SC_ROWS, SC_ROWS)], wsem)

        gather(0, buf0, gsem0).start()

        @pl.loop(0, nch, step=2)
        def _(c0):
            for b in range(2):
                c = c0 + b
                buf, gsem, wsem = bufs[b]
                obuf, ogsem, owsem = bufs[1 - b]
                gather(c, buf, gsem).wait()

                @pl.when(c > 0)
                def _():
                    write(c - 1, obuf, owsem).wait()

                @pl.when(c + 1 < nch)
                def _():
                    gather(c + 1, obuf, ogsem).start()

                write(c, buf, wsem).start()

        write(nch - 1, buf1, wsem1).wait()

    run = pl.kernel(
        body,
        out_type=jax.ShapeDtypeStruct((n, HALF), jnp.uint32),
        mesh=_sc_mesh(),
        scratch_types=[pltpu.VMEM((nch, SC_ROWS), jnp.int32),
                       pltpu.VMEM((SC_ROWS, HALF), jnp.uint32),
                       pltpu.VMEM((SC_ROWS, HALF), jnp.uint32),
                       pltpu.SemaphoreType.DMA, pltpu.SemaphoreType.DMA,
                       pltpu.SemaphoreType.DMA, pltpu.SemaphoreType.DMA],
        name="sc_collect",
    )
    return run(ys, idx).reshape(dest.shape + (HALF,))


def kernel(x, c, positions, w_ada, b_ada, g_pre_mix, g_post_mix, g_pre_ffn, g_post_ffn, w_in, conv_w, conv_b, b_gates, g_mlstm, w_branch_a, w_branch_b, w_out, router_w, router_bias, w_exp_gate, w_exp_up, w_exp_down, w_sh_gate, w_sh_up, w_sh_down):
    depth = w_ada.shape[0]
    for l in range(depth):
        x = _layer(x, c, positions, w_ada[l], b_ada[l], g_pre_mix[l], g_post_mix[l], g_pre_ffn[l],
                   g_post_ffn[l], w_in[l], conv_w[l], conv_b[l], b_gates[l], g_mlstm[l],
                   w_branch_a[l], w_branch_b[l], w_out[l], router_w[l], router_bias[l],
                   w_exp_gate[l], w_exp_up[l], w_exp_down[l], w_sh_gate[l], w_sh_up[l], w_sh_down[l])
    return x
```

```python
import functools

import jax
import jax.numpy as jnp
from jax import lax
from jax.experimental import pallas as pl
from jax.experimental.pallas import tpu as pltpu
from jax.experimental.pallas import tpu_sc as plsc

F32 = jnp.float32
BF16 = jnp.bfloat16
HIGHEST = lax.Precision.HIGHEST

D_MODEL = 1024
ATT_GROUPS = ((128, 1), (512, 4), (2048, 16))
ATT_HEAD_DIM = 64
ATT_GROUP_W = 256
ATT_BLK = 128
ROPE_THETA = 500000.0
ROPE_HALF = 8
MLSTM_HEADS = 4
MLSTM_QK_DIM = 128
MLSTM_V_DIM = 256
MLSTM_BLOCK = 128
CONV_WIDTH = 4
N_EXPERTS = 256
TOP_K = 8
N_GROUPS = 8
TOPK_GROUPS = 4
EXPERT_FF = 256
ROUTED_SCALE = 2.5
NORM_EPS = 1e-6
NEG = -1e30

OFF_MV, OFF_MO, OFF_GA, OFF_GB = 0, 1024, 2048, 3072
OFF_MQ, OFF_MK = 4096, 4608
OFF_AQ, OFF_AK, OFF_AV = 5120, 5888, 6656
PROJ_W = 7424
HALF = D_MODEL // 2

EXPERT_BLOCK = 512
EXPERT_SLOTS = 4
VMEM_LIMIT = 56 * 1024 * 1024


def _nt(a, b, precision=None):
    return lax.dot_general(a, b, (((1,), (1,)), ((), ())), preferred_element_type=F32,
                           precision=precision)


def _tn(a, b):
    return lax.dot_general(a, b, (((0,), (0,)), ((), ())), preferred_element_type=F32)


def _silu(x):
    return x * jax.nn.sigmoid(x)


def _pack_pair(lo, hi):
    lo_b = pltpu.bitcast(lo.astype(BF16).astype(F32), jnp.uint32)
    hi_b = pltpu.bitcast(hi.astype(BF16).astype(F32), jnp.uint32)
    return (lo_b >> 16) | (hi_b & jnp.uint32(0xFFFF0000))


def _unpack_pair(w):
    lo = pltpu.bitcast(w << 16, F32)
    hi = pltpu.bitcast(w & jnp.uint32(0xFFFF0000), F32)
    return lo, hi


def _mod_kernel(c_ref, w_ref, b_ref, o_ref):
    a = _silu(c_ref[...])
    o_ref[...] = jnp.dot(a, w_ref[...], preferred_element_type=F32, precision=HIGHEST) + b_ref[...]


def _adaln(c, w_ada, b_ada):
    B = c.shape[0]
    n = w_ada.shape[1]
    tn = 512
    return pl.pallas_call(
        _mod_kernel,
        grid=(n // tn,),
        in_specs=[pl.BlockSpec((B, D_MODEL), lambda j: (0, 0)),
                  pl.BlockSpec((D_MODEL, tn), lambda j: (0, j)),
                  pl.BlockSpec((1, tn), lambda j: (0, j))],
        out_specs=pl.BlockSpec((B, tn), lambda j: (0, j)),
        out_shape=jax.ShapeDtypeStruct((B, n), F32),
        name="adaln_mod",
    )(c, w_ada, b_ada.reshape(1, n))


def _proj_kernel(x_ref, mod_ref, g_ref, w_ref, wif_ref, o_ref, gates_ref, h_ref):
    @pl.when(pl.program_id(1) == 0)
    def _():
        x = x_ref[...]
        ms = jnp.mean(x * x, axis=-1, keepdims=True)
        y = x * lax.rsqrt(ms + NORM_EPS) * g_ref[...]
        h = (y * (1.0 + mod_ref[0, 1:2, :]) + mod_ref[0, 0:1, :]).astype(BF16)
        h_ref[...] = h
        gates_ref[...] = jnp.dot(h, wif_ref[...], preferred_element_type=F32)

    o_ref[...] = jnp.dot(h_ref[...], w_ref[...], preferred_element_type=F32).astype(BF16)


def _in_proj(x2, mod3, g_pre, w_main, w_if, seq):
    T = x2.shape[0]
    tm, tn = 1024, PROJ_W // 2
    per_b = seq // tm
    return pl.pallas_call(
        _proj_kernel,
        grid=(T // tm, PROJ_W // tn),
        in_specs=[pl.BlockSpec((tm, D_MODEL), lambda i, j: (i, 0)),
                  pl.BlockSpec((1, 6, D_MODEL), lambda i, j: (i // per_b, 0, 0)),
                  pl.BlockSpec((1, D_MODEL), lambda i, j: (0, 0)),
                  pl.BlockSpec((D_MODEL, tn), lambda i, j: (0, j)),
                  pl.BlockSpec((D_MODEL, 128), lambda i, j: (0, 0))],
        out_specs=[pl.BlockSpec((tm, tn), lambda i, j: (i, j)),
                   pl.BlockSpec((tm, 128), lambda i, j: (i, 0))],
        out_shape=[jax.ShapeDtypeStruct((T, PROJ_W), BF16),
                   jax.ShapeDtypeStruct((T, 128), F32)],
        scratch_shapes=[pltpu.VMEM((tm, D_MODEL), BF16)],
        compiler_params=pltpu.CompilerParams(
            dimension_semantics=("arbitrary", "arbitrary"), vmem_limit_bytes=VMEM_LIMIT),
        name="norm_in_proj",
    )(x2, mod3, g_pre, w_main, w_if)


def _attn_kernel(q_ref, k_ref, v_ref, cs_ref, sn_ref, o_ref, qf, kf, vf, acc, m_s, l_s, *, seq):
    g = pl.program_id(1)
    lane = lax.broadcasted_iota(jnp.int32, (ATT_BLK, 128), 1)
    first = (lane % ATT_HEAD_DIM) < ROPE_HALF
    low_head = lane < ATT_HEAD_DIM

    def rope(x, cs, sn):
        partner = jnp.where(first, pltpu.roll(x, 128 - ROPE_HALF, 1), pltpu.roll(x, ROPE_HALF, 1))
        return x * cs + partner * sn

    def zero_pad(i, _):
        rows = pl.ds(pl.multiple_of(i * ATT_BLK, ATT_BLK), ATT_BLK)
        for hp in range(2):
            kf[hp, rows, :] = jnp.zeros((ATT_BLK, 128), F32)
            vf[hp, rows, :] = jnp.zeros((ATT_BLK, 128), F32)
        return 0

    lax.fori_loop(0, seq // ATT_BLK, zero_pad, 0)

    def stage(i, _):
        r = pl.multiple_of(i * ATT_BLK, ATT_BLK)
        rows = pl.ds(r, ATT_BLK)
        prow = pl.ds(pl.multiple_of(seq + i * ATT_BLK, ATT_BLK), ATT_BLK)
        cs = cs_ref[0, rows, :]
        sn = sn_ref[0, rows, :]
        for hp in range(2):
            cols = pl.ds(hp * 128, 128)
            qf[hp, rows, :] = rope(q_ref[0, rows, cols].astype(F32), cs, sn) * (ATT_HEAD_DIM ** -0.5)
            kf[hp, prow, :] = rope(k_ref[0, rows, cols].astype(F32), cs, sn)
            vf[hp, prow, :] = v_ref[0, rows, cols].astype(F32)
        return 0

    lax.fori_loop(0, seq // ATT_BLK, stage, 0)

    qi = lax.broadcasted_iota(jnp.int32, (ATT_BLK, 2 * ATT_BLK), 0)
    ki = lax.broadcasted_iota(jnp.int32, (ATT_BLK, 2 * ATT_BLK), 1)
    band = (ki >= qi) & (ki <= qi + ATT_BLK)

    def process(d, init):
        span = ATT_BLK * d

        def body(c, _):
            rho = c % d
            n = c // d
            qstart = rho + n * span
            kstart = seq + qstart - span
            first_key = jnp.where(n > 0, 0, ATT_BLK)
            valid = band & (ki >= first_key)
            qrows = pl.ds(qstart, ATT_BLK, stride=d) if d > 1 else pl.ds(qstart, ATT_BLK)
            krows = pl.ds(kstart, 2 * ATT_BLK, stride=d) if d > 1 else pl.ds(kstart, 2 * ATT_BLK)
            for hp in range(2):
                q2 = qf[hp, qrows, :]
                k2 = kf[hp, krows, :].astype(BF16)
                v2 = vf[hp, krows, :].astype(BF16)
                res = []
                for hh in range(2):
                    hm = low_head if hh == 0 else jnp.logical_not(low_head)
                    qh = jnp.where(hm, q2, 0.0).astype(BF16)
                    s = jnp.where(valid, _nt(qh, k2), NEG)
                    m = jnp.max(s, axis=1, keepdims=True)
                    p = jnp.exp(s - m)
                    l = jnp.sum(p, axis=1, keepdims=True)
                    o = jnp.dot(p.astype(BF16), v2, preferred_element_type=F32)
                    res.append((o, m, l))
                o_b = jnp.where(low_head, res[0][0], res[1][0])
                m_b = jnp.where(low_head, res[0][1], res[1][1])
                l_b = jnp.where(low_head, res[0][2], res[1][2])
                if init:
                    acc[hp, qrows, :] = o_b
                    m_s[hp, qrows, :] = m_b
                    l_s[hp, qrows, :] = l_b
                else:
                    m_old = m_s[hp, qrows, :]
                    m_new = jnp.maximum(m_old, m_b)
                    a_old = jnp.exp(m_old - m_new)
                    a_new = jnp.exp(m_b - m_new)
                    acc[hp, qrows, :] = acc[hp, qrows, :] * a_old + o_b * a_new
                    l_s[hp, qrows, :] = l_s[hp, qrows, :] * a_old + l_b * a_new
                    m_s[hp, qrows, :] = m_new
            return 0

        lax.fori_loop(0, seq // ATT_BLK, body, 0)

    for gi, (_, d) in enumerate(ATT_GROUPS):
        @pl.when(g == gi)
        def _(d=d, gi=gi):
            process(d, gi == 0)

    @pl.when(g == len(ATT_GROUPS) - 1)
    def _():
        def fin(i, _):
            rows = pl.ds(pl.multiple_of(i * ATT_BLK, ATT_BLK), ATT_BLK)
            for hp in range(2):
                o_ref[0, rows, pl.ds(hp * 128, 128)] = (acc[hp, rows, :] / l_s[hp, rows, :]).astype(BF16)
            return 0

        lax.fori_loop(0, seq // ATT_BLK, fin, 0)


def _attention(proj3, cs, sn):
    B, S, _ = proj3.shape
    ng = len(ATT_GROUPS)
    qb, kb, vb = OFF_AQ // ATT_GROUP_W, OFF_AK // ATT_GROUP_W, OFF_AV // ATT_GROUP_W
    return pl.pallas_call(
        functools.partial(_attn_kernel, seq=S),
        grid=(B, ng),
        in_specs=[pl.BlockSpec((1, S, ATT_GROUP_W), lambda b, g: (b, 0, qb + g)),
                  pl.BlockSpec((1, S, ATT_GROUP_W), lambda b, g: (b, 0, kb + g)),
                  pl.BlockSpec((1, S, ATT_GROUP_W), lambda b, g: (b, 0, vb + g)),
                  pl.BlockSpec((1, S, 128), lambda b, g: (b, 0, 0)),
                  pl.BlockSpec((1, S, 128), lambda b, g: (b, 0, 0))],
        out_specs=pl.BlockSpec((1, S, ATT_GROUP_W), lambda b, g: (b, 0, 0)),
        out_shape=jax.ShapeDtypeStruct((B, S, ATT_GROUP_W), BF16),
        scratch_shapes=[pltpu.VMEM((2, S, 128), F32),
                        pltpu.VMEM((2, 2 * S, 128), F32),
                        pltpu.VMEM((2, 2 * S, 128), F32),
                        pltpu.VMEM((2, S, 128), F32),
                        pltpu.VMEM((2, S, 128), F32),
                        pltpu.VMEM((2, S, 128), F32)],
        compiler_params=pltpu.CompilerParams(
            dimension_semantics=("arbitrary", "arbitrary"), vmem_limit_bytes=VMEM_LIMIT),
        name="dilated_attention",
    )(proj3, proj3, proj3, cs, sn)


def _log_sigmoid(x):
    return jnp.minimum(x, 0.0) - jnp.log(1.0 + jnp.exp(-jnp.abs(x)))


def _mlstm_kernel(mq_ref, mk_ref, mv_ref, mo_ref, gt_ref, cwq_ref, cwk_ref, cbq_ref, cbk_ref,
                  bg_ref, gm_ref, o_ref, pad_ref, q_s, k_s, va_s, rows_s, acc_s, kv_s, inter_s, emt_s,
                  c_s, *, seq):
    h = pl.program_id(1)
    L = MLSTM_BLOCK
    NC = seq // L
    DK, DV = MLSTM_QK_DIM, MLSTM_V_DIM
    DA = DV + 128
    halo = 8

    pad_ref[0:halo, :] = jnp.zeros((halo, 2 * DK), F32)
    for i in range(NC):
        pad_ref[halo + i * L:halo + (i + 1) * L, 0:DK] = mq_ref[0, i * L:(i + 1) * L, :].astype(F32)
        pad_ref[halo + i * L:halo + (i + 1) * L, DK:2 * DK] = mk_ref[0, i * L:(i + 1) * L, :].astype(F32)
        va_s[i * L:(i + 1) * L, 0:DV] = mv_ref[0, i * L:(i + 1) * L, :]
        va_s[i * L:(i + 1) * L, DV:DA] = jnp.ones((L, DA - DV), BF16)
    for i in range(NC):
        yq = jnp.broadcast_to(cbq_ref[...], (L, DK))
        yk = jnp.broadcast_to(cbk_ref[...], (L, DK))
        for j in range(CONV_WIDTH):
            r0 = halo + i * L - (CONV_WIDTH - 1) + j
            yq = yq + pad_ref[r0:r0 + L, 0:DK] * cwq_ref[j:j + 1, :]
            yk = yk + pad_ref[r0:r0 + L, DK:2 * DK] * cwk_ref[j:j + 1, :]
        q_s[i * L:(i + 1) * L, :] = _silu(yq).astype(BF16)
        k_s[i * L:(i + 1) * L, :] = (_silu(yk) * (DK ** -0.5)).astype(BF16)

    lane = lax.broadcasted_iota(jnp.int32, (1, 128), 1)
    bias = bg_ref[...]
    b_i = jnp.sum(jnp.where(lane == h, bias, 0.0), axis=1, keepdims=True)
    b_f = jnp.sum(jnp.where(lane == h + MLSTM_HEADS, bias, 0.0), axis=1, keepdims=True)
    ri = lax.broadcasted_iota(jnp.int32, (L, L), 0)
    ci = lax.broadcasted_iota(jnp.int32, (L, L), 1)
    causal = ci <= ri
    eye = (ri == ci).astype(F32)
    i_rows = gt_ref[0, h] + b_i
    lf_rows = _log_sigmoid(gt_ref[0, h + MLSTM_HEADS] + b_f)
    b_rows = jnp.dot(lf_rows, (ri <= ci).astype(F32), preferred_element_type=F32,
                     precision=HIGHEST)
    b_end = b_rows[:, L - 1:L]
    g_rows = b_end - b_rows + i_rows
    g_max = jnp.max(g_rows, axis=1, keepdims=True)
    m = jnp.zeros((1, 1), F32)
    m_prev, m_new = [], []
    for c in range(NC):
        m_prev.append(m)
        m = jnp.maximum(b_end[c:c + 1, :] + m, g_max[c:c + 1, :])
        m_new.append(m)
    m_prev = jnp.concatenate(m_prev, axis=0)
    m_new = jnp.concatenate(m_new, axis=0)
    rows_s[0] = b_rows
    rows_s[1] = jnp.exp(g_rows - m_new)
    rows_s[2] = b_rows - i_rows
    rows_s[3] = jnp.broadcast_to(m_prev, (NC, L))
    rows_s[4] = jnp.broadcast_to(jnp.exp(b_end + m_prev - m_new), (NC, L))

    r2 = lax.broadcasted_iota(jnp.int32, (2 * L, 2 * L), 0)
    c2 = lax.broadcasted_iota(jnp.int32, (2 * L, 2 * L), 1)
    ones_blk = ((r2 < L) == (c2 < L)).astype(BF16)

    def local(c, _):
        rows = pl.ds(pl.multiple_of(c * L, L), L)
        crow = pl.ds(c, 1)
        b_r = rows_s[0, crow, :]
        w_r = rows_s[1, crow, :]
        u_r = rows_s[2, crow, :]
        mp = rows_s[3, crow, :]
        x2 = jnp.concatenate([eye * b_r, eye * w_r], axis=1)
        hi = x2.astype(BF16)
        lo = (x2 - hi.astype(F32)).astype(BF16)
        yb = (jnp.dot(hi, ones_blk, preferred_element_type=F32)
              + jnp.dot(lo, ones_blk, preferred_element_type=F32))
        b_b = yb[:, 0:L]
        w_b = yb[:, L:2 * L]
        dmat = jnp.where(causal, b_b - u_r, NEG)
        m_t = jnp.maximum(b_b + mp, jnp.max(dmat, axis=1, keepdims=True))
        q = q_s[rows, :]
        k = k_s[rows, :]
        va = va_s[rows, :]
        sc = _nt(q, k) * jnp.exp(dmat - m_t)
        acc_s[rows, :] = jnp.dot(sc.astype(BF16), va, preferred_element_type=F32)
        kv_s[c] = _tn((w_b * k.astype(F32)).astype(BF16), va)
        inter_s[rows, :] = jnp.exp(b_b + mp - m_t)
        emt_s[rows, :] = jnp.exp(-m_t)
        return 0

    lax.fori_loop(0, NC, local, 0, unroll=2)

    g_row = gm_ref[...]
    c_s[...] = jnp.zeros((DK, DA), F32)

    def recur(c, _):
        rows = pl.ds(pl.multiple_of(c * L, L), L)
        state = c_s[...]
        read = jnp.dot(q_s[rows, :], state.astype(BF16), preferred_element_type=F32)
        inter = inter_s[rows, :]
        out = acc_s[rows, :] + jnp.concatenate([inter, inter, inter], axis=1) * read
        den = out[:, DV:DA]
        emt = emt_s[rows, :]
        nrm = jnp.maximum(jnp.abs(jnp.concatenate([den, den], axis=1)),
                          jnp.concatenate([emt, emt], axis=1))
        hh = out[:, 0:DV] / nrm
        ms = jnp.mean(hh * hh, axis=1, keepdims=True)
        hn = hh * lax.rsqrt(ms + NORM_EPS) * g_row
        o_ref[0, rows, :] = (hn * jax.nn.sigmoid(mo_ref[0, rows, :].astype(F32))).astype(BF16)
        dec = rows_s[4, pl.ds(c, 1), :]
        c_s[...] = jnp.concatenate([dec, dec, dec], axis=1) * state + kv_s[c]
        return 0

    lax.fori_loop(0, NC, recur, 0, unroll=2)


def _mlstm(proj3, gates_t, conv_w, conv_b, bg_row, g_mlstm):
    B, S, _ = proj3.shape
    H, DK, DV = MLSTM_HEADS, MLSTM_QK_DIM, MLSTM_V_DIM
    L = MLSTM_BLOCK
    NC = S // L
    DA = DV + 128
    qb, kb = OFF_MQ // DK, OFF_MK // DK
    vb, ob = OFF_MV // DV, OFF_MO // DV
    nq = (H * DK) // DK
    return pl.pallas_call(
        functools.partial(_mlstm_kernel, seq=S),
        grid=(B, H),
        in_specs=[pl.BlockSpec((1, S, DK), lambda b, h: (b, 0, qb + h)),
                  pl.BlockSpec((1, S, DK), lambda b, h: (b, 0, kb + h)),
                  pl.BlockSpec((1, S, DV), lambda b, h: (b, 0, vb + h)),
                  pl.BlockSpec((1, S, DV), lambda b, h: (b, 0, ob + h)),
                  pl.BlockSpec((1, 2 * H, NC, L), lambda b, h: (b, 0, 0, 0)),
                  pl.BlockSpec((CONV_WIDTH, DK), lambda b, h: (0, h)),
                  pl.BlockSpec((CONV_WIDTH, DK), lambda b, h: (0, nq + h)),
                  pl.BlockSpec((1, DK), lambda b, h: (0, h)),
                  pl.BlockSpec((1, DK), lambda b, h: (0, nq + h)),
                  pl.BlockSpec((1, 128), lambda b, h: (0, 0)),
                  pl.BlockSpec((1, DV), lambda b, h: (0, h))],
        out_specs=pl.BlockSpec((1, S, DV), lambda b, h: (b, 0, h)),
        out_shape=jax.ShapeDtypeStruct((B, S, H * DV), BF16),
        scratch_shapes=[pltpu.VMEM((S + 8, 2 * DK), F32),
                        pltpu.VMEM((S, DK), BF16),
                        pltpu.VMEM((S, DK), BF16),
                        pltpu.VMEM((S, DA), BF16),
                        pltpu.VMEM((5, NC, L), F32),
                        pltpu.VMEM((S, DA), F32),
                        pltpu.VMEM((NC, DK, DA), F32),
                        pltpu.VMEM((S, L), F32),
                        pltpu.VMEM((S, L), F32),
                        pltpu.VMEM((DK, DA), F32)],
        compiler_params=pltpu.CompilerParams(
            dimension_semantics=("arbitrary", "arbitrary"), vmem_limit_bytes=VMEM_LIMIT),
        name="mlstm_chunkwise",
    )(proj3, proj3, proj3, proj3, gates_t, conv_w, conv_w, conv_b, conv_b, bg_row, g_mlstm)


def _rms(y, g):
    ms = jnp.mean(y * y, axis=-1, keepdims=True)
    return y * lax.rsqrt(ms + NORM_EPS) * g


def _merge_kernel(ya_ref, yb_ref, ga_ref, gb_ref, x_ref, mod_ref, wa_ref, wb_ref, wo_ref,
                  gpost_ref, gpre_ref, x1_ref, h2_ref):
    pa = jnp.dot(ya_ref[...], wa_ref[...], preferred_element_type=F32)
    pb = jnp.dot(yb_ref[...], wb_ref[...], preferred_element_type=F32)
    merged = (jax.nn.sigmoid(ga_ref[...].astype(F32)) * pa
              + jax.nn.sigmoid(gb_ref[...].astype(F32)) * pb)
    y = jnp.dot(merged.astype(BF16), wo_ref[...], preferred_element_type=F32)
    x1 = x_ref[...] + mod_ref[0, 2:3, :] * _rms(y, gpost_ref[...])
    x1_ref[...] = x1
    h2 = _rms(x1, gpre_ref[...]) * (1.0 + mod_ref[0, 4:5, :]) + mod_ref[0, 3:4, :]
    h2_ref[...] = _pack_pair(h2[:, :HALF], h2[:, HALF:])


def _merge(ya2, yb2, proj2, x2, mod3, wa, wb, wo, g_post, g_pre, seq):
    T = x2.shape[0]
    tm = 512
    per_b = seq // tm
    full = lambda shape: pl.BlockSpec(shape, lambda i: (0,) * len(shape))
    return pl.pallas_call(
        _merge_kernel,
        grid=(T // tm,),
        in_specs=[pl.BlockSpec((tm, ATT_GROUP_W), lambda i: (i, 0)),
                  pl.BlockSpec((tm, D_MODEL), lambda i: (i, 0)),
                  pl.BlockSpec((tm, D_MODEL), lambda i: (i, OFF_GA // D_MODEL)),
                  pl.BlockSpec((tm, D_MODEL), lambda i: (i, OFF_GB // D_MODEL)),
                  pl.BlockSpec((tm, D_MODEL), lambda i: (i, 0)),
                  pl.BlockSpec((1, 6, D_MODEL), lambda i: (i // per_b, 0, 0)),
                  full((ATT_GROUP_W, D_MODEL)), full((D_MODEL, D_MODEL)), full((D_MODEL, D_MODEL)),
                  full((1, D_MODEL)), full((1, D_MODEL))],
        out_specs=[pl.BlockSpec((tm, D_MODEL), lambda i: (i, 0)),
                   pl.BlockSpec((tm, HALF), lambda i: (i, 0))],
        out_shape=[jax.ShapeDtypeStruct((T, D_MODEL), F32),
                   jax.ShapeDtypeStruct((T, HALF), jnp.uint32)],
        compiler_params=pltpu.CompilerParams(
            dimension_semantics=("arbitrary",), vmem_limit_bytes=VMEM_LIMIT),
        name="merge_out_proj",
    )(ya2, yb2, proj2, proj2, x2, mod3, wa, wb, wo, g_post, g_pre)


def _router_kernel(h2_ref, rlo_ref, rhi_ref, bias_ref, idx_ref, w_ref, rank_ref, cnt_ref):
    E = N_EXPERTS
    tr = h2_ref.shape[0]
    gsz = E // N_GROUPS

    @pl.when(pl.program_id(0) == 0)
    def _():
        cnt_ref[...] = jnp.zeros(cnt_ref.shape, F32)

    lo, hi = _unpack_pair(h2_ref[...])
    logits = _nt(rlo_ref[...], lo.astype(BF16)) + _nt(rhi_ref[...], hi.astype(BF16))
    scores = jax.nn.sigmoid(logits)
    sel = scores + bias_ref[:, 0:1]

    gi = lax.broadcasted_iota(jnp.int32, (gsz, tr), 0).astype(F32)
    gs_rows = []
    for g in range(N_GROUPS):
        blk = sel[g * gsz:(g + 1) * gsz, :]
        m1 = jnp.max(blk, axis=0, keepdims=True)
        a1 = jnp.min(jnp.where(blk == m1, gi, float(E)), axis=0, keepdims=True)
        m2 = jnp.max(jnp.where(gi == a1, -jnp.inf, blk), axis=0, keepdims=True)
        gs_rows.append(m1 + m2)
    gs = jnp.concatenate(gs_rows, axis=0)
    g8 = lax.broadcasted_iota(jnp.int32, (N_GROUPS, tr), 0).astype(F32)
    gmask = jnp.zeros((N_GROUPS, tr), F32)
    for _ in range(TOPK_GROUPS):
        m = jnp.max(gs, axis=0, keepdims=True)
        a = jnp.min(jnp.where(gs == m, g8, float(E)), axis=0, keepdims=True)
        hit = g8 == a
        gmask = jnp.where(hit, 1.0, gmask)
        gs = jnp.where(hit, -jnp.inf, gs)
    selm = jnp.concatenate(
        [jnp.where(gmask[g:g + 1, :] > 0.0, sel[g * gsz:(g + 1) * gsz, :], -jnp.inf)
         for g in range(N_GROUPS)], axis=0)

    ei = lax.broadcasted_iota(jnp.int32, (E, tr), 0).astype(F32)
    picks, weights = [], []
    chosen = jnp.zeros((E, tr), F32)
    for _ in range(TOP_K):
        m = jnp.max(selm, axis=0, keepdims=True)
        a = jnp.min(jnp.where(selm == m, ei, float(E)), axis=0, keepdims=True)
        hit = ei == a
        picks.append(a)
        weights.append(jnp.sum(jnp.where(hit, scores, 0.0), axis=0, keepdims=True))
        chosen = jnp.where(hit, 1.0, chosen)
        selm = jnp.where(hit, -jnp.inf, selm)
    wsum = weights[0]
    for w in weights[1:]:
        wsum = wsum + w

    ti = lax.broadcasted_iota(jnp.int32, (tr, tr), 0)
    tj = lax.broadcasted_iota(jnp.int32, (tr, tr), 1)
    before = (ti < tj).astype(BF16)
    pos = jnp.dot(chosen.astype(BF16), before, preferred_element_type=F32) + cnt_ref[:, 0:1]
    ranks = [jnp.sum(jnp.where(ei == a, pos, 0.0), axis=0, keepdims=True) for a in picks]
    cnt_ref[...] = cnt_ref[...] + jnp.sum(chosen, axis=1, keepdims=True)

    idx_ref[...] = jnp.concatenate(picks, axis=0).astype(jnp.int32)
    w_ref[...] = jnp.concatenate([w / wsum * ROUTED_SCALE for w in weights], axis=0)
    rank_ref[...] = jnp.concatenate(ranks, axis=0).astype(jnp.int32)


def _router(h2p, r_lo, r_hi, bias_col):
    T = h2p.shape[0]
    tr = 512
    full = lambda shape: pl.BlockSpec(shape, lambda i: (0,) * len(shape))
    return pl.pallas_call(
        _router_kernel,
        grid=(T // tr,),
        in_specs=[pl.BlockSpec((tr, HALF), lambda i: (i, 0)),
                  full((N_EXPERTS, HALF)), full((N_EXPERTS, HALF)), full((N_EXPERTS, 128))],
        out_specs=[pl.BlockSpec((TOP_K, tr), lambda i: (0, i)),
                   pl.BlockSpec((TOP_K, tr), lambda i: (0, i)),
                   pl.BlockSpec((TOP_K, tr), lambda i: (0, i)),
                   full((N_EXPERTS, 128))],
        out_shape=[jax.ShapeDtypeStruct((TOP_K, T), jnp.int32),
                   jax.ShapeDtypeStruct((TOP_K, T), F32),
                   jax.ShapeDtypeStruct((TOP_K, T), jnp.int32),
                   jax.ShapeDtypeStruct((N_EXPERTS, 128), F32)],
        compiler_params=pltpu.CompilerParams(
            dimension_semantics=("arbitrary",), vmem_limit_bytes=VMEM_LIMIT),
        name="router_topk",
    )(h2p, r_lo, r_hi, bias_col)


def _dest_kernel(idx_ref, rank_ref, pstart_ref, dest_ref):
    tr = idx_ref.shape[1]
    ei = lax.broadcasted_iota(jnp.int32, (N_EXPERTS, tr), 0)
    start = pstart_ref[:, 0:1]
    rows = []
    for k in range(TOP_K):
        hit = ei == idx_ref[k:k + 1, :]
        rows.append(jnp.sum(jnp.where(hit, start, 0.0), axis=0, keepdims=True))
    dest_ref[...] = jnp.concatenate(rows, axis=0).astype(jnp.int32) + rank_ref[...]


def _slot_index(idx, rank, pstart_col):
    T = idx.shape[1]
    tr = 1024
    return pl.pallas_call(
        _dest_kernel,
        grid=(T // tr,),
        in_specs=[pl.BlockSpec((TOP_K, tr), lambda i: (0, i)),
                  pl.BlockSpec((TOP_K, tr), lambda i: (0, i)),
                  pl.BlockSpec((N_EXPERTS, 128), lambda i: (0, 0))],
        out_specs=pl.BlockSpec((TOP_K, tr), lambda i: (0, i)),
        out_shape=jax.ShapeDtypeStruct((TOP_K, T), jnp.int32),
        name="slot_index",
    )(idx, rank, pstart_col)


def _ffn_kernel(first_ref, nblk_ref, nused_ref, xs_hbm, wg_ref, wu_ref, wd_ref, ys_hbm,
                xbuf, ybuf, in_sem, out_sem, wg_s, wu_s, wd_s):
    e = pl.program_id(0)
    bm = EXPERT_BLOCK
    ns = EXPERT_SLOTS
    nused = nused_ref[0]
    first = first_ref[e]
    n = nblk_ref[e]

    def in_copy(g):
        slot = g % ns
        return pltpu.make_async_copy(xs_hbm.at[pl.ds(g * bm, bm)], xbuf.at[slot], in_sem.at[slot])

    def out_copy(g):
        slot = g % ns
        return pltpu.make_async_copy(ybuf.at[slot], ys_hbm.at[pl.ds(g * bm, bm)], out_sem.at[slot])

    for q in range(ns - 1):
        @pl.when((e == 0) & (nused > q))
        def _(q=q):
            in_copy(q).start()

    @pl.when(n > 0)
    def _():
        wg_s[...] = wg_ref[0].astype(BF16)
        wu_s[...] = wu_ref[0].astype(BF16)
        wd_s[...] = wd_ref[0].astype(BF16)

        def block(j, _):
            g = first + j
            slot = g % ns
            in_copy(g).wait()

            @pl.when(g + ns - 1 < nused)
            def _():
                in_copy(g + ns - 1).start()

            lo, hi = _unpack_pair(xbuf[slot])
            lo = lo.astype(BF16)
            hi = hi.astype(BF16)
            gate = (jnp.dot(lo, wg_s[0:HALF, :], preferred_element_type=F32)
                    + jnp.dot(hi, wg_s[HALF:, :], preferred_element_type=F32))
            up = (jnp.dot(lo, wu_s[0:HALF, :], preferred_element_type=F32)
                  + jnp.dot(hi, wu_s[HALF:, :], preferred_element_type=F32))
            hid = (_silu(gate) * up).astype(BF16)
            out = jnp.dot(hid, wd_s[...], preferred_element_type=F32)

            @pl.when(g >= ns)
            def _():
                out_copy(g - ns).wait()

            ybuf[slot] = _pack_pair(out[:, :HALF], out[:, HALF:])
            out_copy(g).start()
            return 0

        lax.fori_loop(0, n, block, 0)

    @pl.when(e == pl.num_programs(0) - 1)
    def _():
        for q in range(ns, 0, -1):
            @pl.when(nused >= q)
            def _(q=q):
                out_copy(nused - q).wait()


def _expert_ffn(first_blk, nblk, nused, xs, w_gate, w_up, w_down):
    P = xs.shape[0]
    bm = EXPERT_BLOCK
    w_map = lambda e, *_: (e, 0, 0)
    grid_spec = pltpu.PrefetchScalarGridSpec(
        num_scalar_prefetch=3,
        grid=(w_gate.shape[0],),
        in_specs=[pl.BlockSpec(memory_space=pl.ANY),
                  pl.BlockSpec((1, D_MODEL, EXPERT_FF), w_map),
                  pl.BlockSpec((1, D_MODEL, EXPERT_FF), w_map),
                  pl.BlockSpec((1, EXPERT_FF, D_MODEL), w_map)],
        out_specs=pl.BlockSpec(memory_space=pl.ANY),
        scratch_shapes=[pltpu.VMEM((EXPERT_SLOTS, bm, HALF), jnp.uint32),
                        pltpu.VMEM((EXPERT_SLOTS, bm, HALF), jnp.uint32),
                        pltpu.SemaphoreType.DMA((EXPERT_SLOTS,)),
                        pltpu.SemaphoreType.DMA((EXPERT_SLOTS,)),
                        pltpu.VMEM((D_MODEL, EXPERT_FF), BF16),
                        pltpu.VMEM((D_MODEL, EXPERT_FF), BF16),
                        pltpu.VMEM((EXPERT_FF, D_MODEL), BF16)],
    )
    return pl.pallas_call(
        _ffn_kernel,
        grid_spec=grid_spec,
        out_shape=jax.ShapeDtypeStruct((P, HALF), jnp.uint32),
        compiler_params=pltpu.CompilerParams(
            dimension_semantics=("arbitrary",), vmem_limit_bytes=VMEM_LIMIT),
        name="routed_experts",
    )(first_blk, nblk, nused, xs, w_gate, w_up, w_down)


def _final_kernel(yg_ref, w_ref, h2_ref, x1_ref, mod_ref, wsg_ref, wsu_ref, wsd_ref, gpost_ref, o_ref):
    lo, hi = _unpack_pair(h2_ref[...])
    lo = lo.astype(BF16)
    hi = hi.astype(BF16)
    gate = (jnp.dot(lo, wsg_ref[0:HALF, :], preferred_element_type=F32)
            + jnp.dot(hi, wsg_ref[HALF:, :], preferred_element_type=F32))
    up = (jnp.dot(lo, wsu_ref[0:HALF, :], preferred_element_type=F32)
          + jnp.dot(hi, wsu_ref[HALF:, :], preferred_element_type=F32))
    shared = jnp.dot((_silu(gate) * up).astype(BF16), wsd_ref[...], preferred_element_type=F32)
    y_lo = shared[:, :HALF]
    y_hi = shared[:, HALF:]
    for k in range(TOP_K):
        r_lo, r_hi = _unpack_pair(yg_ref[k])
        wk = w_ref[:, k:k + 1]
        y_lo = y_lo + wk * r_lo
        y_hi = y_hi + wk * r_hi
    ms = (jnp.sum(y_lo * y_lo, axis=-1, keepdims=True)
          + jnp.sum(y_hi * y_hi, axis=-1, keepdims=True)) * (1.0 / D_MODEL)
    inv = lax.rsqrt(ms + NORM_EPS)
    o_ref[:, 0:HALF] = x1_ref[:, 0:HALF] + mod_ref[0, 5:6, 0:HALF] * (y_lo * inv * gpost_ref[:, 0:HALF])
    o_ref[:, HALF:] = x1_ref[:, HALF:] + mod_ref[0, 5:6, HALF:] * (y_hi * inv * gpost_ref[:, HALF:])


def _final(yg, w_tk, h2p, x1, mod3, wsg, wsu, wsd, g_post, seq):
    T = x1.shape[0]
    tm = 256
    per_b = seq // tm
    full = lambda shape: pl.BlockSpec(shape, lambda i: (0,) * len(shape))
    return pl.pallas_call(
        _final_kernel,
        grid=(T // tm,),
        in_specs=[pl.BlockSpec((TOP_K, tm, HALF), lambda i: (0, i, 0)),
                  pl.BlockSpec((tm, TOP_K), lambda i: (i, 0)),
                  pl.BlockSpec((tm, HALF), lambda i: (i, 0)),
                  pl.BlockSpec((tm, D_MODEL), lambda i: (i, 0)),
                  pl.BlockSpec((1, 6, D_MODEL), lambda i: (i // per_b, 0, 0)),
                  full((D_MODEL, EXPERT_FF)), full((D_MODEL, EXPERT_FF)), full((EXPERT_FF, D_MODEL)),
                  full((1, D_MODEL))],
        out_specs=pl.BlockSpec((tm, D_MODEL), lambda i: (i, 0)),
        out_shape=jax.ShapeDtypeStruct((T, D_MODEL), F32),
        compiler_params=pltpu.CompilerParams(
            dimension_semantics=("arbitrary",), vmem_limit_bytes=VMEM_LIMIT),
        name="shared_expert_combine",
    )(yg, w_tk, h2p, x1, mod3, wsg, wsu, wsd, g_post)


def _rope_tables(positions):
    inv = jnp.power(ROPE_THETA, -jnp.arange(ROPE_HALF, dtype=F32) / ROPE_HALF)
    ang = positions.astype(F32)[..., None] * inv
    cos, sin = jnp.cos(ang), jnp.sin(ang)
    rest = ATT_HEAD_DIM - 2 * ROPE_HALF
    cs = jnp.concatenate([cos, cos, jnp.ones(ang.shape[:-1] + (rest,), F32)], axis=-1)
    sn = jnp.concatenate([-sin, sin, jnp.zeros(ang.shape[:-1] + (rest,), F32)], axis=-1)
    return jnp.tile(cs, (1, 1, 2)), jnp.tile(sn, (1, 1, 2))


def _layer(x, c, positions, w_ada, b_ada, g_pre_mix, g_post_mix, g_pre_ffn, g_post_ffn,
           w_in, conv_w, conv_b, b_gates, g_mlstm, w_branch_a, w_branch_b, w_out,
           router_w, router_bias, w_exp_gate, w_exp_up, w_exp_down, w_sh_gate, w_sh_up, w_sh_down):
    B, S, D = x.shape
    T = B * S
    H = MLSTM_HEADS
    x2 = x.reshape(T, D)

    mod3 = _adaln(c, w_ada, b_ada).reshape(B, 6, D)

    a_w = 3 * ATT_GROUP_W
    o_mq = 3 * a_w
    o_mk = o_mq + H * MLSTM_QK_DIM
    o_mv = o_mk + H * MLSTM_QK_DIM
    o_mo = o_mv + H * MLSTM_V_DIM
    o_mi = o_mo + H * MLSTM_V_DIM
    o_ga = o_mi + 2 * H
    o_gb = o_ga + D
    seg = lambda o, w: w_in[:, o:o + w]
    w_main = jnp.concatenate(
        [seg(o_mv, H * MLSTM_V_DIM), seg(o_mo, H * MLSTM_V_DIM), seg(o_ga, D), seg(o_gb, D),
         seg(o_mq, H * MLSTM_QK_DIM), seg(o_mk, H * MLSTM_QK_DIM),
         seg(0, a_w), seg(a_w, a_w), seg(2 * a_w, a_w)], axis=1).astype(BF16)
    w_if = jnp.pad(seg(o_mi, 2 * H), ((0, 0), (0, 128 - 2 * H))).astype(BF16)

    proj, gates = _in_proj(x2, mod3, g_pre_mix.reshape(1, D), w_main, w_if, S)
    proj3 = proj.reshape(B, S, PROJ_W)

    cs, sn = _rope_tables(positions)
    y_a = _attention(proj3, cs, sn)

    bg_row = jnp.pad(b_gates.reshape(1, 2 * H), ((0, 0), (0, 128 - 2 * H)))
    gates_t = gates[:, :2 * H].reshape(B, S, 2 * H).transpose(0, 2, 1)
    gates_t = gates_t.reshape(B, 2 * H, S // MLSTM_BLOCK, MLSTM_BLOCK)
    y_b = _mlstm(proj3, gates_t, conv_w, conv_b.reshape(1, -1), bg_row, g_mlstm.reshape(1, -1))

    x1, h2p = _merge(y_a.reshape(T, ATT_GROUP_W), y_b.reshape(T, D), proj, x2, mod3,
                     w_branch_a.astype(BF16), w_branch_b.astype(BF16), w_out.astype(BF16),
                     g_post_mix.reshape(1, D), g_pre_ffn.reshape(1, D), S)

    rw_t = router_w.T.astype(BF16)
    bias_col = jnp.broadcast_to(router_bias.reshape(N_EXPERTS, 1), (N_EXPERTS, 128))
    idx, wts, rank, cnt = _router(h2p, rw_t[:, :HALF], rw_t[:, HALF:], bias_col)

    bm = EXPERT_BLOCK
    nb = (T * TOP_K) // bm + N_EXPERTS
    counts = cnt[:, 0].astype(jnp.int32)
    padded = (counts + bm - 1) // bm * bm
    pend = jnp.cumsum(padded)
    pstart = pend - padded
    pstart_col = jnp.broadcast_to(pstart.astype(F32).reshape(N_EXPERTS, 1), (N_EXPERTS, 128))
    dest = _slot_index(idx, rank, pstart_col)
    nused = (pend[-1] // bm).astype(jnp.int32).reshape(1)

    xs = _dispatch(h2p, dest, nb * bm)
    ys = _expert_ffn((pstart // bm).astype(jnp.int32), (padded // bm).astype(jnp.int32), nused,
                     xs, w_exp_gate, w_exp_up, w_exp_down)
    yg = _collect(ys, dest)

    out = _final(yg, wts.T, h2p, x1, mod3, w_sh_gate.astype(BF16), w_sh_up.astype(BF16),
                 w_sh_down.astype(BF16), g_post_ffn.reshape(1, D), S)
    return out.reshape(B, S, D)


SC_CORES = 2
SC_SUBCORES = 16
SC_WORKERS = SC_CORES * SC_SUBCORES
SC_ROWS = 64


def _sc_mesh():
    return plsc.VectorSubcoreMesh(core_axis_name="c", subcore_axis_name="s",
                                  num_cores=SC_CORES, num_subcores=SC_SUBCORES)


def _worker_id():
    return lax.axis_index("s") * SC_CORES + lax.axis_index("c")


def _dispatch(h2p, dest, n_slots):
    T = h2p.shape[0]
    per_w = T // SC_WORKERS
    nch = per_w // SC_ROWS
    idx = dest.reshape(TOP_K, SC_WORKERS, nch, SC_ROWS).transpose(1, 2, 0, 3)
    idx = idx.reshape(SC_WORKERS, nch * TOP_K, SC_ROWS)

    def body(x_hbm, idx_hbm, xs_hbm, idx_v, buf0, buf1, rsem0, rsem1, ssem0, ssem1):
        wid = _worker_id()
        base = wid * per_w
        pltpu.sync_copy(idx_hbm.at[wid], idx_v)
        bufs = ((buf0, rsem0, ssem0), (buf1, rsem1, ssem1))

        def read(c, buf, rsem):
            return pltpu.make_async_copy(x_hbm.at[pl.ds(base + c * SC_ROWS, SC_ROWS)], buf, rsem)

        def scatter(c, k, buf, ssem):
            return pltpu.make_async_copy(buf, xs_hbm.at[idx_v.at[c * TOP_K + k]], ssem)

        read(0, buf0, rsem0).start()

        @pl.loop(0, nch, step=2)
        def _(c0):
            for b in range(2):
                c = c0 + b
                buf, rsem, ssem = bufs[b]
                obuf, orsem, ossem = bufs[1 - b]
                read(c, buf, rsem).wait()

                @pl.when(c > 0)
                def _():
                    for k in range(TOP_K):
                        scatter(c - 1, k, obuf, ossem).wait()

                @pl.when(c + 1 < nch)
                def _():
                    read(c + 1, obuf, orsem).start()

                for k in range(TOP_K):
                    scatter(c, k, buf, ssem).start()

        for k in range(TOP_K):
            scatter(nch - 1, k, buf1, ssem1).wait()

    run = pl.kernel(
        body,
        out_type=jax.ShapeDtypeStruct((n_slots, HALF), jnp.uint32),
        mesh=_sc_mesh(),
        scratch_types=[pltpu.VMEM((nch * TOP_K, SC_ROWS), jnp.int32),
                       pltpu.VMEM((SC_ROWS, HALF), jnp.uint32),
                       pltpu.VMEM((SC_ROWS, HALF), jnp.uint32),
                       pltpu.SemaphoreType.DMA, pltpu.SemaphoreType.DMA,
                       pltpu.SemaphoreType.DMA, pltpu.SemaphoreType.DMA],
        name="sc_dispatch",
    )
    return run(h2p, idx)


def _collect(ys, dest):
    n = dest.size
    per_w = n // SC_WORKERS
    nch = per_w // SC_ROWS
    idx = dest.reshape(SC_WORKERS, nch, SC_ROWS)

    def body(ys_hbm, idx_hbm, out_hbm, idx_v, buf0, buf1, gsem0, gsem1, wsem0, wsem1):
        wid = _worker_id()
        base = wid * per_w
        pltpu.sync_copy(idx_hbm.at[wid], idx_v)
        bufs = ((buf0, gsem0, wsem0), (buf1, gsem1, wsem1))

        def gather(c, buf, gsem):
            return pltpu.make_async_copy(ys_hbm.at[idx_v.at[c]], buf, gsem)

        def write(c, buf, wsem):
            return pltpu.make_async_copy(buf, out_hbm.at[pl.ds(base + c * SC_ROWS, SC_ROWS)], wsem)

        gather(0, buf0, gsem0).start()

        @pl.loop(0, nch, step=2)
        def _(c0):
            for b in range(2):
                c = c0 + b
                buf, gsem, wsem = bufs[b]
                obuf, ogsem, owsem = bufs[1 - b]
                gather(c, buf, gsem).wait()

                @pl.when(c > 0)
                def _():
                    write(c - 1, obuf, owsem).wait()

                @pl.when(c + 1 < nch)
                def _():
                    gather(c + 1, obuf, ogsem).start()

                write(c, buf, wsem).start()

        write(nch - 1, buf1, wsem1).wait()

    run = pl.kernel(
        body,
        out_type=jax.ShapeDtypeStruct((n, HALF), jnp.uint32),
        mesh=_sc_mesh(),
        scratch_types=[pltpu.VMEM((nch, SC_ROWS), jnp.int32),
                       pltpu.VMEM((SC_ROWS, HALF), jnp.uint32),
                       pltpu.VMEM((SC_ROWS, HALF), jnp.uint32),
                       pltpu.SemaphoreType.DMA, pltpu.SemaphoreType.DMA,
                       pltpu.SemaphoreType.DMA, pltpu.SemaphoreType.DMA],
        name="sc_collect",
    )
    return run(ys, idx).reshape(dest.shape + (HALF,))


def kernel(x, c, positions, w_ada, b_ada, g_pre_mix, g_post_mix, g_pre_ffn, g_post_ffn, w_in, conv_w, conv_b, b_gates, g_mlstm, w_branch_a, w_branch_b, w_out, router_w, router_bias, w_exp_gate, w_exp_up, w_exp_down, w_sh_gate, w_sh_up, w_sh_down):
    depth = w_ada.shape[0]
    for l in range(depth):
        x = _layer(x, c, positions, w_ada[l], b_ada[l], g_pre_mix[l], g_post_mix[l], g_pre_ffn[l],
                   g_post_ffn[l], w_in[l], conv_w[l], conv_b[l], b_gates[l], g_mlstm[l],
                   w_branch_a[l], w_branch_b[l], w_out[l], router_w[l], router_bias[l],
                   w_exp_gate[l], w_exp_up[l], w_exp_down[l], w_sh_gate[l], w_sh_up[l], w_sh_down[l])
    return x
```

```python
import functools

import jax
import jax.numpy as jnp
from jax import lax
from jax.experimental import pallas as pl
from jax.experimental.pallas import tpu as pltpu
from jax.experimental.pallas import tpu_sc as plsc

F32 = jnp.float32
BF16 = jnp.bfloat16
HIGHEST = lax.Precision.HIGHEST

D_MODEL = 1024
ATT_GROUPS = ((128, 1), (512, 4), (2048, 16))
ATT_HEAD_DIM = 64
ATT_GROUP_W = 256
ATT_BLK = 128
ROPE_THETA = 500000.0
ROPE_HALF = 8
MLSTM_HEADS = 4
MLSTM_QK_DIM = 128
MLSTM_V_DIM = 256
MLSTM_BLOCK = 128
MLSTM_GROUP = 4
CONV_WIDTH = 4
N_EXPERTS = 256
TOP_K = 8
N_GROUPS = 8
TOPK_GROUPS = 4
EXPERT_FF = 256
ROUTED_SCALE = 2.5
NORM_EPS = 1e-6
NEG = -1e30

OFF_MV, OFF_MO, OFF_GA, OFF_GB = 0, 1024, 2048, 3072
OFF_MQ, OFF_MK = 4096, 4608
OFF_AQ, OFF_AK, OFF_AV = 5120, 5888, 6656
PROJ_W = 7424
HALF = D_MODEL // 2

EXPERT_BLOCK = 512
EXPERT_SLOTS = 4
VMEM_LIMIT = 56 * 1024 * 1024


def _nt(a, b, precision=None):
    return lax.dot_general(a, b, (((1,), (1,)), ((), ())), preferred_element_type=F32,
                           precision=precision)


def _tn(a, b):
    return lax.dot_general(a, b, (((0,), (0,)), ((), ())), preferred_element_type=F32)


def _silu(x):
    return x * jax.nn.sigmoid(x)


def _pack_pair(lo, hi):
    lo_b = pltpu.bitcast(lo.astype(BF16).astype(F32), jnp.uint32)
    hi_b = pltpu.bitcast(hi.astype(BF16).astype(F32), jnp.uint32)
    return (lo_b >> 16) | (hi_b & jnp.uint32(0xFFFF0000))


def _unpack_pair(w):
    lo = pltpu.bitcast(w << 16, F32)
    hi = pltpu.bitcast(w & jnp.uint32(0xFFFF0000), F32)
    return lo, hi


def _mod_kernel(c_ref, w_ref, b_ref, o_ref):
    a = _silu(c_ref[...])
    o_ref[...] = jnp.dot(a, w_ref[...], preferred_element_type=F32, precision=HIGHEST) + b_ref[...]


def _adaln(c, w_ada, b_ada):
    B = c.shape[0]
    n = w_ada.shape[1]
    tn = 512
    return pl.pallas_call(
        _mod_kernel,
        grid=(n // tn,),
        in_specs=[pl.BlockSpec((B, D_MODEL), lambda j: (0, 0)),
                  pl.BlockSpec((D_MODEL, tn), lambda j: (0, j)),
                  pl.BlockSpec((1, tn), lambda j: (0, j))],
        out_specs=pl.BlockSpec((B, tn), lambda j: (0, j)),
        out_shape=jax.ShapeDtypeStruct((B, n), F32),
        name="adaln_mod",
    )(c, w_ada, b_ada.reshape(1, n))


def _proj_kernel(x_ref, mod_ref, g_ref, w_ref, wif_ref, o_ref, gates_ref, h_ref):
    @pl.when(pl.program_id(1) == 0)
    def _():
        x = x_ref[...]
        ms = jnp.mean(x * x, axis=-1, keepdims=True)
        y = x * lax.rsqrt(ms + NORM_EPS) * g_ref[...]
        h = (y * (1.0 + mod_ref[0, 1:2, :]) + mod_ref[0, 0:1, :]).astype(BF16)
        h_ref[...] = h
        gates_ref[...] = jnp.dot(h, wif_ref[...], preferred_element_type=F32)

    o_ref[...] = jnp.dot(h_ref[...], w_ref[...], preferred_element_type=F32).astype(BF16)


def _in_proj(x2, mod3, g_pre, w_main, w_if, seq):
    T = x2.shape[0]
    tm, tn = 1024, PROJ_W // 2
    per_b = seq // tm
    return pl.pallas_call(
        _proj_kernel,
        grid=(T // tm, PROJ_W // tn),
        in_specs=[pl.BlockSpec((tm, D_MODEL), lambda i, j: (i, 0)),
                  pl.BlockSpec((1, 6, D_MODEL), lambda i, j: (i // per_b, 0, 0)),
                  pl.BlockSpec((1, D_MODEL), lambda i, j: (0, 0)),
                  pl.BlockSpec((D_MODEL, tn), lambda i, j: (0, j)),
                  pl.BlockSpec((D_MODEL, 128), lambda i, j: (0, 0))],
        out_specs=[pl.BlockSpec((tm, tn), lambda i, j: (i, j)),
                   pl.BlockSpec((tm, 128), lambda i, j: (i, 0))],
        out_shape=[jax.ShapeDtypeStruct((T, PROJ_W), BF16),
                   jax.ShapeDtypeStruct((T, 128), F32)],
        scratch_shapes=[pltpu.VMEM((tm, D_MODEL), BF16)],
        compiler_params=pltpu.CompilerParams(
            dimension_semantics=("arbitrary", "arbitrary"), vmem_limit_bytes=VMEM_LIMIT),
        name="norm_in_proj",
    )(x2, mod3, g_pre, w_main, w_if)


def _attn_kernel(q_ref, k_ref, v_ref, cs_ref, sn_ref, o_ref, qf, kf, vf, acc, m_s, l_s, *, seq):
    g = pl.program_id(1)
    lane = lax.broadcasted_iota(jnp.int32, (ATT_BLK, 128), 1)
    first = (lane % ATT_HEAD_DIM) < ROPE_HALF
    low_head = lane < ATT_HEAD_DIM

    def rope(x, cs, sn):
        partner = jnp.where(first, pltpu.roll(x, 128 - ROPE_HALF, 1), pltpu.roll(x, ROPE_HALF, 1))
        return x * cs + partner * sn

    def zero_pad(i, _):
        rows = pl.ds(pl.multiple_of(i * ATT_BLK, ATT_BLK), ATT_BLK)
        for hp in range(2):
            kf[hp, rows, :] = jnp.zeros((ATT_BLK, 128), F32)
            vf[hp, rows, :] = jnp.zeros((ATT_BLK, 128), F32)
        return 0

    lax.fori_loop(0, seq // ATT_BLK, zero_pad, 0)

    def stage(i, _):
        r = pl.multiple_of(i * ATT_BLK, ATT_BLK)
        rows = pl.ds(r, ATT_BLK)
        prow = pl.ds(pl.multiple_of(seq + i * ATT_BLK, ATT_BLK), ATT_BLK)
        cs = cs_ref[0, rows, :]
        sn = sn_ref[0, rows, :]
        for hp in range(2):
            cols = pl.ds(hp * 128, 128)
            qf[hp, rows, :] = rope(q_ref[0, rows, cols].astype(F32), cs, sn) * (ATT_HEAD_DIM ** -0.5)
            kf[hp, prow, :] = rope(k_ref[0, rows, cols].astype(F32), cs, sn)
            vf[hp, prow, :] = v_ref[0, rows, cols].astype(F32)
        return 0

    lax.fori_loop(0, seq // ATT_BLK, stage, 0)

    qi = lax.broadcasted_iota(jnp.int32, (ATT_BLK, 2 * ATT_BLK), 0)
    ki = lax.broadcasted_iota(jnp.int32, (ATT_BLK, 2 * ATT_BLK), 1)
    band = (ki >= qi) & (ki <= qi + ATT_BLK)

    def process(d, init):
        span = ATT_BLK * d

        def body(c, _):
            rho = c % d
            n = c // d
            qstart = rho + n * span
            kstart = seq + qstart - span
            first_key = jnp.where(n > 0, 0, ATT_BLK)
            valid = band & (ki >= first_key)
            qrows = pl.ds(qstart, ATT_BLK, stride=d) if d > 1 else pl.ds(qstart, ATT_BLK)
            krows = pl.ds(kstart, 2 * ATT_BLK, stride=d) if d > 1 else pl.ds(kstart, 2 * ATT_BLK)
            heads = [(hp, hh) for hp in range(2) for hh in range(2)]
            q2 = [qf[hp, qrows, :] for hp in range(2)]
            k2 = [kf[hp, krows, :].astype(BF16) for hp in range(2)]
            v2 = [vf[hp, krows, :].astype(BF16) for hp in range(2)]
            qh = [jnp.where(low_head if hh == 0 else jnp.logical_not(low_head), q2[hp], 0.0).astype(BF16)
                  for hp, hh in heads]
            s = [jnp.where(valid, _nt(qh[i], k2[hp]), NEG) for i, (hp, hh) in enumerate(heads)]
            m = [jnp.max(x, axis=1, keepdims=True) for x in s]
            p = [jnp.exp(x - mx) for x, mx in zip(s, m)]
            l = [jnp.sum(x, axis=1, keepdims=True) for x in p]
            o = [jnp.dot(p[i].astype(BF16), v2[hp], preferred_element_type=F32)
                 for i, (hp, hh) in enumerate(heads)]
            for hp in range(2):
                o_b = jnp.where(low_head, o[2 * hp], o[2 * hp + 1])
                m_b = jnp.where(low_head, m[2 * hp], m[2 * hp + 1])
                l_b = jnp.where(low_head, l[2 * hp], l[2 * hp + 1])
                if init:
                    acc[hp, qrows, :] = o_b
                    m_s[hp, qrows, :] = m_b
                    l_s[hp, qrows, :] = l_b
                else:
                    m_old = m_s[hp, qrows, :]
                    m_new = jnp.maximum(m_old, m_b)
                    a_old = jnp.exp(m_old - m_new)
                    a_new = jnp.exp(m_b - m_new)
                    acc[hp, qrows, :] = acc[hp, qrows, :] * a_old + o_b * a_new
                    l_s[hp, qrows, :] = l_s[hp, qrows, :] * a_old + l_b * a_new
                    m_s[hp, qrows, :] = m_new
            return 0

        lax.fori_loop(0, seq // ATT_BLK, body, 0, unroll=2)

    for gi, (_, d) in enumerate(ATT_GROUPS):
        @pl.when(g == gi)
        def _(d=d, gi=gi):
            process(d, gi == 0)

    @pl.when(g == len(ATT_GROUPS) - 1)
    def _():
        def fin(i, _):
            rows = pl.ds(pl.multiple_of(i * ATT_BLK, ATT_BLK), ATT_BLK)
            for hp in range(2):
                o_ref[0, rows, pl.ds(hp * 128, 128)] = (acc[hp, rows, :] / l_s[hp, rows, :]).astype(BF16)
            return 0

        lax.fori_loop(0, seq // ATT_BLK, fin, 0)


def _attention(proj3, cs, sn):
    B, S, _ = proj3.shape
    ng = len(ATT_GROUPS)
    qb, kb, vb = OFF_AQ // ATT_GROUP_W, OFF_AK // ATT_GROUP_W, OFF_AV // ATT_GROUP_W
    return pl.pallas_call(
        functools.partial(_attn_kernel, seq=S),
        grid=(B, ng),
        in_specs=[pl.BlockSpec((1, S, ATT_GROUP_W), lambda b, g: (b, 0, qb + g)),
                  pl.BlockSpec((1, S, ATT_GROUP_W), lambda b, g: (b, 0, kb + g)),
                  pl.BlockSpec((1, S, ATT_GROUP_W), lambda b, g: (b, 0, vb + g)),
                  pl.BlockSpec((1, S, 128), lambda b, g: (b, 0, 0)),
                  pl.BlockSpec((1, S, 128), lambda b, g: (b, 0, 0))],
        out_specs=pl.BlockSpec((1, S, ATT_GROUP_W), lambda b, g: (b, 0, 0)),
        out_shape=jax.ShapeDtypeStruct((B, S, ATT_GROUP_W), BF16),
        scratch_shapes=[pltpu.VMEM((2, S, 128), F32),
                        pltpu.VMEM((2, 2 * S, 128), F32),
                        pltpu.VMEM((2, 2 * S, 128), F32),
                        pltpu.VMEM((2, S, 128), F32),
                        pltpu.VMEM((2, S, 128), F32),
                        pltpu.VMEM((2, S, 128), F32)],
        compiler_params=pltpu.CompilerParams(
            dimension_semantics=("arbitrary", "arbitrary"), vmem_limit_bytes=VMEM_LIMIT),
        name="dilated_attention",
    )(proj3, proj3, proj3, cs, sn)


def _log_sigmoid(x):
    return jnp.minimum(x, 0.0) - jnp.log(1.0 + jnp.exp(-jnp.abs(x)))


def _mlstm_kernel(mq_ref, mk_ref, mv_ref, mo_ref, gt_ref, cwq_ref, cwk_ref, cbq_ref, cbk_ref,
                  bg_ref, gm_ref, o_ref, q_s, k_s, va_s, rows_s, acc_s, kv_s, inter_s, emt_s,
                  c_s, *, seq):
    h = pl.program_id(1)
    L = MLSTM_BLOCK
    NC = seq // L
    DK, DV = MLSTM_QK_DIM, MLSTM_V_DIM
    DA = DV + 128
    nshift = CONV_WIDTH - 1

    tt = lax.broadcasted_iota(jnp.int32, (nshift * L, 2 * L), 0)
    uu = lax.broadcasted_iota(jnp.int32, (nshift * L, 2 * L), 1)
    shift_mat = (uu == L + tt % L - (tt // L + 1)).astype(BF16)
    conv_w = jnp.concatenate([cwq_ref[...], cwk_ref[...]], axis=1)
    conv_b = jnp.concatenate([cbq_ref[...], cbk_ref[...]], axis=1)
    prev = jnp.zeros((L, 2 * DK), BF16)
    for i in range(NC):
        blk = slice(i * L, (i + 1) * L)
        va_s[blk, 0:DV] = mv_ref[0, blk, :]
        va_s[blk, DV:DA] = jnp.ones((L, DA - DV), BF16)
        cur = jnp.concatenate([mq_ref[0, blk, :], mk_ref[0, blk, :]], axis=1)
        shifted = jnp.dot(shift_mat, jnp.concatenate([prev, cur], axis=0),
                          preferred_element_type=F32)
        y = conv_b + cur.astype(F32) * conv_w[nshift:nshift + 1, :]
        for s in range(nshift):
            y = y + shifted[s * L:(s + 1) * L, :] * conv_w[nshift - 1 - s:nshift - s, :]
        y = _silu(y)
        q_s[blk, :] = y[:, 0:DK].astype(BF16)
        k_s[blk, :] = (y[:, DK:2 * DK] * (DK ** -0.5)).astype(BF16)
        prev = cur

    lane = lax.broadcasted_iota(jnp.int32, (1, 128), 1)
    bias = bg_ref[...]
    b_i = jnp.sum(jnp.where(lane == h, bias, 0.0), axis=1, keepdims=True)
    b_f = jnp.sum(jnp.where(lane == h + MLSTM_HEADS, bias, 0.0), axis=1, keepdims=True)
    ri = lax.broadcasted_iota(jnp.int32, (L, L), 0)
    ci = lax.broadcasted_iota(jnp.int32, (L, L), 1)
    causal = ci <= ri
    eye = (ri == ci).astype(F32)
    i_rows = gt_ref[0, h] + b_i
    lf_rows = _log_sigmoid(gt_ref[0, h + MLSTM_HEADS] + b_f)
    b_rows = jnp.dot(lf_rows, (ri <= ci).astype(F32), preferred_element_type=F32,
                     precision=HIGHEST)
    b_end = b_rows[:, L - 1:L]
    g_rows = b_end - b_rows + i_rows
    g_max = jnp.max(g_rows, axis=1, keepdims=True)
    m = jnp.zeros((1, 1), F32)
    m_prev, m_new = [], []
    for c in range(NC):
        m_prev.append(m)
        m = jnp.maximum(b_end[c:c + 1, :] + m, g_max[c:c + 1, :])
        m_new.append(m)
    m_prev = jnp.concatenate(m_prev, axis=0)
    m_new = jnp.concatenate(m_new, axis=0)
    rows_s[0] = b_rows
    rows_s[1] = jnp.exp(g_rows - m_new)
    rows_s[2] = b_rows - i_rows
    rows_s[3] = jnp.broadcast_to(m_prev, (NC, L))
    rows_s[4] = jnp.broadcast_to(jnp.exp(b_end + m_prev - m_new), (NC, L))

    r2 = lax.broadcasted_iota(jnp.int32, (2 * L, 2 * L), 0)
    c2 = lax.broadcasted_iota(jnp.int32, (2 * L, 2 * L), 1)
    ones_blk = ((r2 < L) == (c2 < L)).astype(BF16)

    G = MLSTM_GROUP

    def local(cg, _):
        cs = [cg * G + i for i in range(G)]
        rows = [pl.ds(pl.multiple_of(c * L, L), L) for c in cs]
        b_r = [rows_s[0, pl.ds(c, 1), :] for c in cs]
        w_r = [rows_s[1, pl.ds(c, 1), :] for c in cs]
        u_r = [rows_s[2, pl.ds(c, 1), :] for c in cs]
        mp = [rows_s[3, pl.ds(c, 1), :] for c in cs]
        q = [q_s[r, :] for r in rows]
        k = [k_s[r, :] for r in rows]
        va = [va_s[r, :] for r in rows]
        qk = [_nt(a, b) for a, b in zip(q, k)]
        x2 = [jnp.concatenate([eye * a, eye * b], axis=1) for a, b in zip(b_r, w_r)]
        hi = [x.astype(BF16) for x in x2]
        lo = [(x - h_.astype(F32)).astype(BF16) for x, h_ in zip(x2, hi)]
        yb = [jnp.dot(h_, ones_blk, preferred_element_type=F32)
              + jnp.dot(l_, ones_blk, preferred_element_type=F32) for h_, l_ in zip(hi, lo)]
        b_b = [y[:, 0:L] for y in yb]
        w_b = [y[:, L:2 * L] for y in yb]
        for i in range(G):
            kv_s[cs[i]] = _tn((w_b[i] * k[i].astype(F32)).astype(BF16), va[i])
        dmat = [jnp.where(causal, b - u, NEG) for b, u in zip(b_b, u_r)]
        m_t = [jnp.maximum(b + m_, jnp.max(d, axis=1, keepdims=True))
               for b, m_, d in zip(b_b, mp, dmat)]
        sc = [a * jnp.exp(d - m_) for a, d, m_ in zip(qk, dmat, m_t)]
        for i in range(G):
            acc_s[rows[i], :] = jnp.dot(sc[i].astype(BF16), va[i], preferred_element_type=F32)
            inter_s[rows[i], :] = jnp.exp(b_b[i] + mp[i] - m_t[i])
            emt_s[rows[i], :] = jnp.exp(-m_t[i])
        return 0

    lax.fori_loop(0, NC // G, local, 0)

    g_row = gm_ref[...]
    c_s[...] = jnp.zeros((DK, DA), F32)

    def recur(cg, _):
        cs = [cg * G + i for i in range(G)]
        rows = [pl.ds(pl.multiple_of(c * L, L), L) for c in cs]
        states = [c_s[...]]
        for c in cs:
            dec = rows_s[4, pl.ds(c, 1), :]
            states.append(jnp.concatenate([dec, dec, dec], axis=1) * states[-1] + kv_s[c])
        c_s[...] = states[G]
        read = [jnp.dot(q_s[r, :], st.astype(BF16), preferred_element_type=F32)
                for r, st in zip(rows, states)]
        inter = [inter_s[r, :] for r in rows]
        out = [acc_s[r, :] + jnp.concatenate([it, it, it], axis=1) * rd
               for r, it, rd in zip(rows, inter, read)]
        emt = [emt_s[r, :] for r in rows]
        nrm = [jnp.maximum(jnp.abs(jnp.concatenate([o[:, DV:DA], o[:, DV:DA]], axis=1)),
                           jnp.concatenate([e_, e_], axis=1)) for o, e_ in zip(out, emt)]
        hh = [o[:, 0:DV] / n_ for o, n_ in zip(out, nrm)]
        ms = [jnp.mean(x * x, axis=1, keepdims=True) for x in hh]
        hn = [x * lax.rsqrt(m_ + NORM_EPS) * g_row for x, m_ in zip(hh, ms)]
        for i in range(G):
            o_ref[0, rows[i], :] = (hn[i] * jax.nn.sigmoid(mo_ref[0, rows[i], :].astype(F32))).astype(BF16)
        return 0

    lax.fori_loop(0, NC // G, recur, 0)


def _mlstm(proj3, gates_t, conv_w, conv_b, bg_row, g_mlstm):
    B, S, _ = proj3.shape
    H, DK, DV = MLSTM_HEADS, MLSTM_QK_DIM, MLSTM_V_DIM
    L = MLSTM_BLOCK
    NC = S // L
    DA = DV + 128
    qb, kb = OFF_MQ // DK, OFF_MK // DK
    vb, ob = OFF_MV // DV, OFF_MO // DV
    nq = (H * DK) // DK
    return pl.pallas_call(
        functools.partial(_mlstm_kernel, seq=S),
        grid=(B, H),
        in_specs=[pl.BlockSpec((1, S, DK), lambda b, h: (b, 0, qb + h)),
                  pl.BlockSpec((1, S, DK), lambda b, h: (b, 0, kb + h)),
                  pl.BlockSpec((1, S, DV), lambda b, h: (b, 0, vb + h)),
                  pl.BlockSpec((1, S, DV), lambda b, h: (b, 0, ob + h)),
                  pl.BlockSpec((1, 2 * H, NC, L), lambda b, h: (b, 0, 0, 0)),
                  pl.BlockSpec((CONV_WIDTH, DK), lambda b, h: (0, h)),
                  pl.BlockSpec((CONV_WIDTH, DK), lambda b, h: (0, nq + h)),
                  pl.BlockSpec((1, DK), lambda b, h: (0, h)),
                  pl.BlockSpec((1, DK), lambda b, h: (0, nq + h)),
                  pl.BlockSpec((1, 128), lambda b, h: (0, 0)),
                  pl.BlockSpec((1, DV), lambda b, h: (0, h))],
        out_specs=pl.BlockSpec((1, S, DV), lambda b, h: (b, 0, h)),
        out_shape=jax.ShapeDtypeStruct((B, S, H * DV), BF16),
        scratch_shapes=[pltpu.VMEM((S, DK), BF16),
                        pltpu.VMEM((S, DK), BF16),
                        pltpu.VMEM((S, DA), BF16),
                        pltpu.VMEM((5, NC, L), F32),
                        pltpu.VMEM((S, DA), F32),
                        pltpu.VMEM((NC, DK, DA), F32),
                        pltpu.VMEM((S, L), F32),
                        pltpu.VMEM((S, L), F32),
                        pltpu.VMEM((DK, DA), F32)],
        compiler_params=pltpu.CompilerParams(
            dimension_semantics=("arbitrary", "arbitrary"), vmem_limit_bytes=VMEM_LIMIT),
        name="mlstm_chunkwise",
    )(proj3, proj3, proj3, proj3, gates_t, conv_w, conv_w, conv_b, conv_b, bg_row, g_mlstm)


def _rms(y, g):
    ms = jnp.mean(y * y, axis=-1, keepdims=True)
    return y * lax.rsqrt(ms + NORM_EPS) * g


def _merge_kernel(ya_ref, yb_ref, ga_ref, gb_ref, x_ref, mod_ref, wa_ref, wb_ref, wo_ref,
                  gpost_ref, gpre_ref, x1_ref, h2_ref):
    pa = jnp.dot(ya_ref[...], wa_ref[...], preferred_element_type=F32)
    pb = jnp.dot(yb_ref[...], wb_ref[...], preferred_element_type=F32)
    merged = (jax.nn.sigmoid(ga_ref[...].astype(F32)) * pa
              + jax.nn.sigmoid(gb_ref[...].astype(F32)) * pb)
    y = jnp.dot(merged.astype(BF16), wo_ref[...], preferred_element_type=F32)
    x1 = x_ref[...] + mod_ref[0, 2:3, :] * _rms(y, gpost_ref[...])
    x1_ref[...] = x1
    h2 = _rms(x1, gpre_ref[...]) * (1.0 + mod_ref[0, 4:5, :]) + mod_ref[0, 3:4, :]
    h2_ref[...] = _pack_pair(h2[:, :HALF], h2[:, HALF:])


def _merge(ya2, yb2, proj2, x2, mod3, wa, wb, wo, g_post, g_pre, seq):
    T = x2.shape[0]
    tm = 512
    per_b = seq // tm
    full = lambda shape: pl.BlockSpec(shape, lambda i: (0,) * len(shape))
    return pl.pallas_call(
        _merge_kernel,
        grid=(T // tm,),
        in_specs=[pl.BlockSpec((tm, ATT_GROUP_W), lambda i: (i, 0)),
                  pl.BlockSpec((tm, D_MODEL), lambda i: (i, 0)),
                  pl.BlockSpec((tm, D_MODEL), lambda i: (i, OFF_GA // D_MODEL)),
                  pl.BlockSpec((tm, D_MODEL), lambda i: (i, OFF_GB // D_MODEL)),
                  pl.BlockSpec((tm, D_MODEL), lambda i: (i, 0)),
                  pl.BlockSpec((1, 6, D_MODEL), lambda i: (i // per_b, 0, 0)),
                  full((ATT_GROUP_W, D_MODEL)), full((D_MODEL, D_MODEL)), full((D_MODEL, D_MODEL)),
                  full((1, D_MODEL)), full((1, D_MODEL))],
        out_specs=[pl.BlockSpec((tm, D_MODEL), lambda i: (i, 0)),
                   pl.BlockSpec((tm, HALF), lambda i: (i, 0))],
        out_shape=[jax.ShapeDtypeStruct((T, D_MODEL), F32),
                   jax.ShapeDtypeStruct((T, HALF), jnp.uint32)],
        compiler_params=pltpu.CompilerParams(
            dimension_semantics=("arbitrary",), vmem_limit_bytes=VMEM_LIMIT),
        name="merge_out_proj",
    )(ya2, yb2, proj2, proj2, x2, mod3, wa, wb, wo, g_post, g_pre)


def _router_kernel(h2_ref, rlo_ref, rhi_ref, bias_ref, idx_ref, w_ref, rank_ref, cnt_ref):
    E = N_EXPERTS
    tr = h2_ref.shape[0]
    gsz = E // N_GROUPS

    @pl.when(pl.program_id(0) == 0)
    def _():
        cnt_ref[...] = jnp.zeros(cnt_ref.shape, F32)

    lo, hi = _unpack_pair(h2_ref[...])
    logits = _nt(rlo_ref[...], lo.astype(BF16)) + _nt(rhi_ref[...], hi.astype(BF16))
    scores = jax.nn.sigmoid(logits)
    sel = scores + bias_ref[:, 0:1]

    gi = lax.broadcasted_iota(jnp.int32, (gsz, tr), 0).astype(F32)
    gs_rows = []
    for g in range(N_GROUPS):
        blk = sel[g * gsz:(g + 1) * gsz, :]
        m1 = jnp.max(blk, axis=0, keepdims=True)
        a1 = jnp.min(jnp.where(blk == m1, gi, float(E)), axis=0, keepdims=True)
        m2 = jnp.max(jnp.where(gi == a1, -jnp.inf, blk), axis=0, keepdims=True)
        gs_rows.append(m1 + m2)
    gs = jnp.concatenate(gs_rows, axis=0)
    g8 = lax.broadcasted_iota(jnp.int32, (N_GROUPS, tr), 0).astype(F32)
    gmask = jnp.zeros((N_GROUPS, tr), F32)
    for _ in range(TOPK_GROUPS):
        m = jnp.max(gs, axis=0, keepdims=True)
        a = jnp.min(jnp.where(gs == m, g8, float(E)), axis=0, keepdims=True)
        hit = g8 == a
        gmask = jnp.where(hit, 1.0, gmask)
        gs = jnp.where(hit, -jnp.inf, gs)
    selm = jnp.concatenate(
        [jnp.where(gmask[g:g + 1, :] > 0.0, sel[g * gsz:(g + 1) * gsz, :], -jnp.inf)
         for g in range(N_GROUPS)], axis=0)

    ei = lax.broadcasted_iota(jnp.int32, (E, tr), 0).astype(F32)
    picks, weights = [], []
    chosen = jnp.zeros((E, tr), F32)
    for _ in range(TOP_K):
        m = jnp.max(selm, axis=0, keepdims=True)
        a = jnp.min(jnp.where(selm == m, ei, float(E)), axis=0, keepdims=True)
        hit = ei == a
        picks.append(a)
        weights.append(jnp.sum(jnp.where(hit, scores, 0.0), axis=0, keepdims=True))
        chosen = jnp.where(hit, 1.0, chosen)
        selm = jnp.where(hit, -jnp.inf, selm)
    wsum = weights[0]
    for w in weights[1:]:
        wsum = wsum + w

    ti = lax.broadcasted_iota(jnp.int32, (tr, tr), 0)
    tj = lax.broadcasted_iota(jnp.int32, (tr, tr), 1)
    before = (ti < tj).astype(BF16)
    pos = jnp.dot(chosen.astype(BF16), before, preferred_element_type=F32) + cnt_ref[:, 0:1]
    ranks = [jnp.sum(jnp.where(ei == a, pos, 0.0), axis=0, keepdims=True) for a in picks]
    cnt_ref[...] = cnt_ref[...] + jnp.sum(chosen, axis=1, keepdims=True)

    idx_ref[...] = jnp.concatenate(picks, axis=0).astype(jnp.int32)
    w_ref[...] = jnp.concatenate([w / wsum * ROUTED_SCALE for w in weights], axis=0)
    rank_ref[...] = jnp.concatenate(ranks, axis=0).astype(jnp.int32)


def _router(h2p, r_lo, r_hi, bias_col):
    T = h2p.shape[0]
    tr = 512
    full = lambda shape: pl.BlockSpec(shape, lambda i: (0,) * len(shape))
    return pl.pallas_call(
        _router_kernel,
        grid=(T // tr,),
        in_specs=[pl.BlockSpec((tr, HALF), lambda i: (i, 0)),
                  full((N_EXPERTS, HALF)), full((N_EXPERTS, HALF)), full((N_EXPERTS, 128))],
        out_specs=[pl.BlockSpec((TOP_K, tr), lambda i: (0, i)),
                   pl.BlockSpec((TOP_K, tr), lambda i: (0, i)),
                   pl.BlockSpec((TOP_K, tr), lambda i: (0, i)),
                   full((N_EXPERTS, 128))],
        out_shape=[jax.ShapeDtypeStruct((TOP_K, T), jnp.int32),
                   jax.ShapeDtypeStruct((TOP_K, T), F32),
                   jax.ShapeDtypeStruct((TOP_K, T), jnp.int32),
                   jax.ShapeDtypeStruct((N_EXPERTS, 128), F32)],
        compiler_params=pltpu.CompilerParams(
            dimension_semantics=("arbitrary",), vmem_limit_bytes=VMEM_LIMIT),
        name="router_topk",
    )(h2p, r_lo, r_hi, bias_col)


def _dest_kernel(idx_ref, rank_ref, pstart_ref, dest_ref):
    tr = idx_ref.shape[1]
    ei = lax.broadcasted_iota(jnp.int32, (N_EXPERTS, tr), 0)
    start = pstart_ref[:, 0:1]
    rows = []
    for k in range(TOP_K):
        hit = ei == idx_ref[k:k + 1, :]
        rows.append(jnp.sum(jnp.where(hit, start, 0.0), axis=0, keepdims=True))
    dest_ref[...] = jnp.concatenate(rows, axis=0).astype(jnp.int32) + rank_ref[...]


def _slot_index(idx, rank, pstart_col):
    T = idx.shape[1]
    tr = 1024
    return pl.pallas_call(
        _dest_kernel,
        grid=(T // tr,),
        in_specs=[pl.BlockSpec((TOP_K, tr), lambda i: (0, i)),
                  pl.BlockSpec((TOP_K, tr), lambda i: (0, i)),
                  pl.BlockSpec((N_EXPERTS, 128), lambda i: (0, 0))],
        out_specs=pl.BlockSpec((TOP_K, tr), lambda i: (0, i)),
        out_shape=jax.ShapeDtypeStruct((TOP_K, T), jnp.int32),
        name="slot_index",
    )(idx, rank, pstart_col)


def _ffn_kernel(first_ref, nblk_ref, nused_ref, xs_hbm, wg_ref, wu_ref, wd_ref, ys_hbm,
                xbuf, ybuf, in_sem, out_sem, wg_s, wu_s, wd_s):
    e = pl.program_id(0)
    bm = EXPERT_BLOCK
    ns = EXPERT_SLOTS
    nused = nused_ref[0]
    first = first_ref[e]
    n = nblk_ref[e]

    def in_copy(g):
        slot = g % ns
        return pltpu.make_async_copy(xs_hbm.at[pl.ds(g * bm, bm)], xbuf.at[slot], in_sem.at[slot])

    def out_copy(g):
        slot = g % ns
        return pltpu.make_async_copy(ybuf.at[slot], ys_hbm.at[pl.ds(g * bm, bm)], out_sem.at[slot])

    for q in range(ns - 1):
        @pl.when((e == 0) & (nused > q))
        def _(q=q):
            in_copy(q).start()

    @pl.when(n > 0)
    def _():
        wg_s[...] = wg_ref[0].astype(BF16)
        wu_s[...] = wu_ref[0].astype(BF16)
        wd_s[...] = wd_ref[0].astype(BF16)

        def block(j, _):
            g = first + j
            slot = g % ns
            in_copy(g).wait()

            @pl.when(g + ns - 1 < nused)
            def _():
                in_copy(g + ns - 1).start()

            lo, hi = _unpack_pair(xbuf[slot])
            lo = lo.astype(BF16)
            hi = hi.astype(BF16)
            gate = (jnp.dot(lo, wg_s[0:HALF, :], preferred_element_type=F32)
                    + jnp.dot(hi, wg_s[HALF:, :], preferred_element_type=F32))
            up = (jnp.dot(lo, wu_s[0:HALF, :], preferred_element_type=F32)
                  + jnp.dot(hi, wu_s[HALF:, :], preferred_element_type=F32))
            hid = (_silu(gate) * up).astype(BF16)
            out = jnp.dot(hid, wd_s[...], preferred_element_type=F32)

            @pl.when(g >= ns)
            def _():
                out_copy(g - ns).wait()

            ybuf[slot] = _pack_pair(out[:, :HALF], out[:, HALF:])
            out_copy(g).start()
            return 0

        lax.fori_loop(0, n, block, 0)

    @pl.when(e == pl.num_programs(0) - 1)
    def _():
        for q in range(ns, 0, -1):
            @pl.when(nused >= q)
            def _(q=q):
                out_copy(nused - q).wait()


def _expert_ffn(first_blk, nblk, nused, xs, w_gate, w_up, w_down):
    P = xs.shape[0]
    bm = EXPERT_BLOCK
    w_map = lambda e, *_: (e, 0, 0)
    grid_spec = pltpu.PrefetchScalarGridSpec(
        num_scalar_prefetch=3,
        grid=(w_gate.shape[0],),
        in_specs=[pl.BlockSpec(memory_space=pl.ANY),
                  pl.BlockSpec((1, D_MODEL, EXPERT_FF), w_map),
                  pl.BlockSpec((1, D_MODEL, EXPERT_FF), w_map),
                  pl.BlockSpec((1, EXPERT_FF, D_MODEL), w_map)],
        out_specs=pl.BlockSpec(memory_space=pl.ANY),
        scratch_shapes=[pltpu.VMEM((EXPERT_SLOTS, bm, HALF), jnp.uint32),
                        pltpu.VMEM((EXPERT_SLOTS, bm, HALF), jnp.uint32),
                        pltpu.SemaphoreType.DMA((EXPERT_SLOTS,)),
                        pltpu.SemaphoreType.DMA((EXPERT_SLOTS,)),
                        pltpu.VMEM((D_MODEL, EXPERT_FF), BF16),
                        pltpu.VMEM((D_MODEL, EXPERT_FF), BF16),
                        pltpu.VMEM((EXPERT_FF, D_MODEL), BF16)],
    )
    return pl.pallas_call(
        _ffn_kernel,
        grid_spec=grid_spec,
        out_shape=jax.ShapeDtypeStruct((P, HALF), jnp.uint32),
        compiler_params=pltpu.CompilerParams(
            dimension_semantics=("arbitrary",), vmem_limit_bytes=VMEM_LIMIT),
        name="routed_experts",
    )(first_blk, nblk, nused, xs, w_gate, w_up, w_down)


def _final_kernel(yg_ref, w_ref, h2_ref, x1_ref, mod_ref, wsg_ref, wsu_ref, wsd_ref, gpost_ref, o_ref):
    lo, hi = _unpack_pair(h2_ref[...])
    lo = lo.astype(BF16)
    hi = hi.astype(BF16)
    gate = (jnp.dot(lo, wsg_ref[0:HALF, :], preferred_element_type=F32)
            + jnp.dot(hi, wsg_ref[HALF:, :], preferred_element_type=F32))
    up = (jnp.dot(lo, wsu_ref[0:HALF, :], preferred_element_type=F32)
          + jnp.dot(hi, wsu_ref[HALF:, :], preferred_element_type=F32))
    shared = jnp.dot((_silu(gate) * up).astype(BF16), wsd_ref[...], preferred_element_type=F32)
    y_lo = shared[:, :HALF]
    y_hi = shared[:, HALF:]
    for k in range(TOP_K):
        r_lo, r_hi = _unpack_pair(yg_ref[k])
        wk = w_ref[:, k:k + 1]
        y_lo = y_lo + wk * r_lo
        y_hi = y_hi + wk * r_hi
    ms = (jnp.sum(y_lo * y_lo, axis=-1, keepdims=True)
          + jnp.sum(y_hi * y_hi, axis=-1, keepdims=True)) * (1.0 / D_MODEL)
    inv = lax.rsqrt(ms + NORM_EPS)
    o_ref[:, 0:HALF] = x1_ref[:, 0:HALF] + mod_ref[0, 5:6, 0:HALF] * (y_lo * inv * gpost_ref[:, 0:HALF])
    o_ref[:, HALF:] = x1_ref[:, HALF:] + mod_ref[0, 5:6, HALF:] * (y_hi * inv * gpost_ref[:, HALF:])


def _final(yg, w_tk, h2p, x1, mod3, wsg, wsu, wsd, g_post, seq):
    T = x1.shape[0]
    tm = 256
    per_b = seq // tm
    full = lambda shape: pl.BlockSpec(shape, lambda i: (0,) * len(shape))
    return pl.pallas_call(
        _final_kernel,
        grid=(T // tm,),
        in_specs=[pl.BlockSpec((TOP_K, tm, HALF), lambda i: (0, i, 0)),
                  pl.BlockSpec((tm, TOP_K), lambda i: (i, 0)),
                  pl.BlockSpec((tm, HALF), lambda i: (i, 0)),
                  pl.BlockSpec((tm, D_MODEL), lambda i: (i, 0)),
                  pl.BlockSpec((1, 6, D_MODEL), lambda i: (i // per_b, 0, 0)),
                  full((D_MODEL, EXPERT_FF)), full((D_MODEL, EXPERT_FF)), full((EXPERT_FF, D_MODEL)),
                  full((1, D_MODEL))],
        out_specs=pl.BlockSpec((tm, D_MODEL), lambda i: (i, 0)),
        out_shape=jax.ShapeDtypeStruct((T, D_MODEL), F32),
        compiler_params=pltpu.CompilerParams(
            dimension_semantics=("arbitrary",), vmem_limit_bytes=VMEM_LIMIT),
        name="shared_expert_combine",
    )(yg, w_tk, h2p, x1, mod3, wsg, wsu, wsd, g_post)


def _rope_tables(positions):
    inv = jnp.power(ROPE_THETA, -jnp.arange(ROPE_HALF, dtype=F32) / ROPE_HALF)
    ang = positions.astype(F32)[..., None] * inv
    cos, sin = jnp.cos(ang), jnp.sin(ang)
    rest = ATT_HEAD_DIM - 2 * ROPE_HALF
    cs = jnp.concatenate([cos, cos, jnp.ones(ang.shape[:-1] + (rest,), F32)], axis=-1)
    sn = jnp.concatenate([-sin, sin, jnp.zeros(ang.shape[:-1] + (rest,), F32)], axis=-1)
    return jnp.tile(cs, (1, 1, 2)), jnp.tile(sn, (1, 1, 2))


def _layer(x, c, positions, w_ada, b_ada, g_pre_mix, g_post_mix, g_pre_ffn, g_post_ffn,
           w_in, conv_w, conv_b, b_gates, g_mlstm, w_branch_a, w_branch_b, w_out,
           router_w, router_bias, w_exp_gate, w_exp_up, w_exp_down, w_sh_gate, w_sh_up, w_sh_down):
    B, S, D = x.shape
    T = B * S
    H = MLSTM_HEADS
    x2 = x.reshape(T, D)

    mod3 = _adaln(c, w_ada, b_ada).reshape(B, 6, D)

    a_w = 3 * ATT_GROUP_W
    o_mq = 3 * a_w
    o_mk = o_mq + H * MLSTM_QK_DIM
    o_mv = o_mk + H * MLSTM_QK_DIM
    o_mo = o_mv + H * MLSTM_V_DIM
    o_mi = o_mo + H * MLSTM_V_DIM
    o_ga = o_mi + 2 * H
    o_gb = o_ga + D
    seg = lambda o, w: w_in[:, o:o + w]
    w_main = jnp.concatenate(
        [seg(o_mv, H * MLSTM_V_DIM), seg(o_mo, H * MLSTM_V_DIM), seg(o_ga, D), seg(o_gb, D),
         seg(o_mq, H * MLSTM_QK_DIM), seg(o_mk, H * MLSTM_QK_DIM),
         seg(0, a_w), seg(a_w, a_w), seg(2 * a_w, a_w)], axis=1).astype(BF16)
    w_if = jnp.pad(seg(o_mi, 2 * H), ((0, 0), (0, 128 - 2 * H))).astype(BF16)

    proj, gates = _in_proj(x2, mod3, g_pre_mix.reshape(1, D), w_main, w_if, S)
    proj3 = proj.reshape(B, S, PROJ_W)

    cs, sn = _rope_tables(positions)
    y_a = _attention(proj3, cs, sn)

    bg_row = jnp.pad(b_gates.reshape(1, 2 * H), ((0, 0), (0, 128 - 2 * H)))
    gates_t = gates[:, :2 * H].reshape(B, S, 2 * H).transpose(0, 2, 1)
    gates_t = gates_t.reshape(B, 2 * H, S // MLSTM_BLOCK, MLSTM_BLOCK)
    y_b = _mlstm(proj3, gates_t, conv_w, conv_b.reshape(1, -1), bg_row, g_mlstm.reshape(1, -1))

    x1, h2p = _merge(y_a.reshape(T, ATT_GROUP_W), y_b.reshape(T, D), proj, x2, mod3,
                     w_branch_a.astype(BF16), w_branch_b.astype(BF16), w_out.astype(BF16),
                     g_post_mix.reshape(1, D), g_pre_ffn.reshape(1, D), S)

    rw_t = router_w.T.astype(BF16)
    bias_col = jnp.broadcast_to(router_bias.reshape(N_EXPERTS, 1), (N_EXPERTS, 128))
    idx, wts, rank, cnt = _router(h2p, rw_t[:, :HALF], rw_t[:, HALF:], bias_col)

    bm = EXPERT_BLOCK
    nb = (T * TOP_K) // bm + N_EXPERTS
    counts = cnt[:, 0].astype(jnp.int32)
    padded = (counts + bm - 1) // bm * bm
    pend = jnp.cumsum(padded)
    pstart = pend - padded
    pstart_col = jnp.broadcast_to(pstart.astype(F32).reshape(N_EXPERTS, 1), (N_EXPERTS, 128))
    dest = _slot_index(idx, rank, pstart_col)
    nused = (pend[-1] // bm).astype(jnp.int32).reshape(1)

    xs = _dispatch(h2p, dest, nb * bm)
    ys = _expert_ffn((pstart // bm).astype(jnp.int32), (padded // bm).astype(jnp.int32), nused,
                     xs, w_exp_gate, w_exp_up, w_exp_down)
    yg = _collect(ys, dest)

    out = _final(yg, wts.T, h2p, x1, mod3, w_sh_gate.astype(BF16), w_sh_up.astype(BF16),
                 w_sh_down.astype(BF16), g_post_ffn.reshape(1, D), S)
    return out.reshape(B, S, D)


SC_CORES = 2
SC_SUBCORES = 16
SC_WORKERS = SC_CORES * SC_SUBCORES
SC_ROWS = 64


def _sc_mesh():
    return plsc.VectorSubcoreMesh(core_axis_name="c", subcore_axis_name="s",
                                  num_cores=SC_CORES, num_subcores=SC_SUBCORES)


def _worker_id():
    return lax.axis_index("s") * SC_CORES + lax.axis_index("c")


def _dispatch(h2p, dest, n_slots):
    T = h2p.shape[0]
    per_w = T // SC_WORKERS
    nch = per_w // SC_ROWS
    idx = dest.reshape(TOP_K, SC_WORKERS, nch, SC_ROWS).transpose(1, 2, 0, 3)
    idx = idx.reshape(SC_WORKERS, nch * TOP_K, SC_ROWS)

    def body(x_hbm, idx_hbm, xs_hbm, idx_v, buf0, buf1, rsem0, rsem1, ssem0, ssem1):
        wid = _worker_id()
        base = wid * per_w
        pltpu.sync_copy(idx_hbm.at[wid], idx_v)
        bufs = ((buf0, rsem0, ssem0), (buf1, rsem1, ssem1))

        def read(c, buf, rsem):
            return pltpu.make_async_copy(x_hbm.at[pl.ds(base + c * SC_ROWS, SC_ROWS)], buf, rsem)

        def scatter(c, k, buf, ssem):
            return pltpu.make_async_copy(buf, xs_hbm.at[idx_v.at[c * TOP_K + k]], ssem)

        read(0, buf0, rsem0).start()

        @pl.loop(0, nch, step=2)
        def _(c0):
            for b in range(2):
                c = c0 + b
                buf, rsem, ssem = bufs[b]
                obuf, orsem, ossem = bufs[1 - b]
                read(c, buf, rsem).wait()

                @pl.when(c > 0)
                def _():
                    for k in range(TOP_K):
                        scatter(c - 1, k, obuf, ossem).wait()

                @pl.when(c + 1 < nch)
                def _():
                    read(c + 1, obuf, orsem).start()

                for k in range(TOP_K):
                    scatter(c, k, buf, ssem).start()

        for k in range(TOP_K):
            scatter(nch - 1, k, buf1, ssem1).wait()

    run = pl.kernel(
        body,
        out_type=jax.ShapeDtypeStruct((n_slots, HALF), jnp.uint32),
        mesh=_sc_mesh(),
        scratch_types=[pltpu.VMEM((nch * TOP_K, SC_ROWS), jnp.int32),
                       pltpu.VMEM((SC_ROWS, HALF), jnp.uint32),
                       pltpu.VMEM((SC_ROWS, HALF), jnp.uint32),
                       pltpu.SemaphoreType.DMA, pltpu.SemaphoreType.DMA,
                       pltpu.SemaphoreType.DMA, pltpu.SemaphoreType.DMA],
        name="sc_dispatch",
    )
    return run(h2p, idx)


def _collect(ys, dest):
    n = dest.size
    per_w = n // SC_WORKERS
    nch = per_w // SC_ROWS
    idx = dest.reshape(SC_WORKERS, nch, SC_ROWS)

    def body(ys_hbm, idx_hbm, out_hbm, idx_v, buf0, buf1, gsem0, gsem1, wsem0, wsem1):
        wid = _worker_id()
        base = wid * per_w
        pltpu.sync_copy(idx_hbm.at[wid], idx_v)
        bufs = ((buf0, gsem0, wsem0), (buf1, gsem1, wsem1))

        def gather(c, buf, gsem):
            return pltpu.make_async_copy(ys_hbm.at[idx_v.at[c]], buf, gsem)

        def write(c, buf, wsem):
            return pltpu.make_async_copy(buf, out_hbm.at[pl.ds(base + c * SC_ROWS, SC_ROWS)], wsem)

        gather(0, buf0, gsem0).start()

        @pl.loop(0, nch, step=2)
        def _(c0):
            for b in range(2):
                c = c0 + b
                buf, gsem, wsem = bufs[b]
                obuf, ogsem, owsem = bufs[1 - b]
                gather(c, buf, gsem).wait()

                @pl.when(c > 0)
                def _():
                    write(c - 1, obuf, owsem).wait()

                @pl.when(c + 1 < nch)
                def _():
                    gather(c + 1, obuf, ogsem).start()

                write(c, buf, wsem).start()

        write(nch - 1, buf1, wsem1).wait()

    run = pl.kernel(
        body,
        out_type=jax.ShapeDtypeStruct((n, HALF), jnp.uint32),
        mesh=_sc_mesh(),
        scratch_types=[pltpu.VMEM((nch, SC_ROWS), jnp.int32),
                       pltpu.VMEM((SC_ROWS, HALF), jnp.uint32),
                       pltpu.VMEM((SC_ROWS, HALF), jnp.uint32),
                       pltpu.SemaphoreType.DMA, pltpu.SemaphoreType.DMA,
                       pltpu.SemaphoreType.DMA, pltpu.SemaphoreType.DMA],
        name="sc_collect",
    )
    return run(ys, idx).reshape(dest.shape + (HALF,))


def kernel(x, c, positions, w_ada, b_ada, g_pre_mix, g_post_mix, g_pre_ffn, g_post_ffn, w_in, conv_w, conv_b, b_gates, g_mlstm, w_branch_a, w_branch_b, w_out, router_w, router_bias, w_exp_gate, w_exp_up, w_exp_down, w_sh_gate, w_sh_up, w_sh_down):
    depth = w_ada.shape[0]
    for l in range(depth):
        x = _layer(x, c, positions, w_ada[l], b_ada[l], g_pre_mix[l], g_post_mix[l], g_pre_ffn[l],
                   g_post_ffn[l], w_in[l], conv_w[l], conv_b[l], b_gates[l], g_mlstm[l],
                   w_branch_a[l], w_branch_b[l], w_out[l], router_w[l], router_bias[l],
                   w_exp_gate[l], w_exp_up[l], w_exp_down[l], w_sh_gate[l], w_sh_up[l], w_sh_down[l])
    return x
```

```python
import functools

import jax
import jax.numpy as jnp
from jax import lax
from jax.experimental import pallas as pl
from jax.experimental.pallas import tpu as pltpu
from jax.experimental.pallas import tpu_sc as plsc

F32 = jnp.float32
BF16 = jnp.bfloat16
HIGHEST = lax.Precision.HIGHEST

D_MODEL = 1024
ATT_GROUPS = ((128, 1), (512, 4), (2048, 16))
ATT_HEAD_DIM = 64
ATT_GROUP_W = 256
ATT_BLK = 128
ROPE_THETA = 500000.0
ROPE_HALF = 8
MLSTM_HEADS = 4
MLSTM_QK_DIM = 128
MLSTM_V_DIM = 256
MLSTM_BLOCK = 128
MLSTM_GROUP = 4
CONV_WIDTH = 4
N_EXPERTS = 256
TOP_K = 8
N_GROUPS = 8
TOPK_GROUPS = 4
EXPERT_FF = 256
ROUTED_SCALE = 2.5
NORM_EPS = 1e-6
NEG = -1e30

OFF_MV, OFF_MO, OFF_GA, OFF_GB = 0, 1024, 2048, 3072
OFF_MQ, OFF_MK = 4096, 4608
OFF_AQ, OFF_AK, OFF_AV = 5120, 5888, 6656
PROJ_W = 7424
HALF = D_MODEL // 2

EXPERT_BLOCK = 512
EXPERT_SLOTS = 4
MOE_PARTS = 2
VMEM_LIMIT = 56 * 1024 * 1024


def _nt(a, b, precision=None):
    return lax.dot_general(a, b, (((1,), (1,)), ((), ())), preferred_element_type=F32,
                           precision=precision)


def _tn(a, b):
    return lax.dot_general(a, b, (((0,), (0,)), ((), ())), preferred_element_type=F32)


def _silu(x):
    return x * jax.nn.sigmoid(x)


def _pack_pair(lo, hi):
    lo_b = pltpu.bitcast(lo.astype(BF16).astype(F32), jnp.uint32)
    hi_b = pltpu.bitcast(hi.astype(BF16).astype(F32), jnp.uint32)
    return (lo_b >> 16) | (hi_b & jnp.uint32(0xFFFF0000))


def _unpack_pair(w):
    lo = pltpu.bitcast(w << 16, F32)
    hi = pltpu.bitcast(w & jnp.uint32(0xFFFF0000), F32)
    return lo, hi


def _mod_kernel(c_ref, w_ref, b_ref, o_ref):
    a = _silu(c_ref[...])
    o_ref[...] = jnp.dot(a, w_ref[...], preferred_element_type=F32, precision=HIGHEST) + b_ref[...]


def _adaln(c, w_ada, b_ada):
    B = c.shape[0]
    n = w_ada.shape[1]
    tn = 512
    return pl.pallas_call(
        _mod_kernel,
        grid=(n // tn,),
        in_specs=[pl.BlockSpec((B, D_MODEL), lambda j: (0, 0)),
                  pl.BlockSpec((D_MODEL, tn), lambda j: (0, j)),
                  pl.BlockSpec((1, tn), lambda j: (0, j))],
        out_specs=pl.BlockSpec((B, tn), lambda j: (0, j)),
        out_shape=jax.ShapeDtypeStruct((B, n), F32),
        name="adaln_mod",
    )(c, w_ada, b_ada.reshape(1, n))


def _proj_kernel(x_ref, mod_ref, g_ref, w_ref, wif_ref, o_ref, gates_ref, h_ref):
    @pl.when(pl.program_id(1) == 0)
    def _():
        x = x_ref[...]
        ms = jnp.mean(x * x, axis=-1, keepdims=True)
        y = x * lax.rsqrt(ms + NORM_EPS) * g_ref[...]
        h = (y * (1.0 + mod_ref[0, 1:2, :]) + mod_ref[0, 0:1, :]).astype(BF16)
        h_ref[...] = h
        gates_ref[...] = jnp.dot(h, wif_ref[...], preferred_element_type=F32)

    o_ref[...] = jnp.dot(h_ref[...], w_ref[...], preferred_element_type=F32).astype(BF16)


def _in_proj(x2, mod3, g_pre, w_main, w_if, seq):
    T = x2.shape[0]
    tm, tn = 1024, PROJ_W // 2
    per_b = seq // tm
    return pl.pallas_call(
        _proj_kernel,
        grid=(T // tm, PROJ_W // tn),
        in_specs=[pl.BlockSpec((tm, D_MODEL), lambda i, j: (i, 0)),
                  pl.BlockSpec((1, 6, D_MODEL), lambda i, j: (i // per_b, 0, 0)),
                  pl.BlockSpec((1, D_MODEL), lambda i, j: (0, 0)),
                  pl.BlockSpec((D_MODEL, tn), lambda i, j: (0, j)),
                  pl.BlockSpec((D_MODEL, 128), lambda i, j: (0, 0))],
        out_specs=[pl.BlockSpec((tm, tn), lambda i, j: (i, j)),
                   pl.BlockSpec((tm, 128), lambda i, j: (i, 0))],
        out_shape=[jax.ShapeDtypeStruct((T, PROJ_W), BF16),
                   jax.ShapeDtypeStruct((T, 128), F32)],
        scratch_shapes=[pltpu.VMEM((tm, D_MODEL), BF16)],
        compiler_params=pltpu.CompilerParams(
            dimension_semantics=("arbitrary", "arbitrary"), vmem_limit_bytes=VMEM_LIMIT),
        name="norm_in_proj",
    )(x2, mod3, g_pre, w_main, w_if)


def _attn_kernel(q_ref, k_ref, v_ref, cs_ref, sn_ref, o_ref, qf, kf, vf, acc, m_s, l_s, *, seq):
    g = pl.program_id(1)
    lane = lax.broadcasted_iota(jnp.int32, (ATT_BLK, 128), 1)
    first = (lane % ATT_HEAD_DIM) < ROPE_HALF
    low_head = lane < ATT_HEAD_DIM

    def rope(x, cs, sn):
        partner = jnp.where(first, pltpu.roll(x, 128 - ROPE_HALF, 1), pltpu.roll(x, ROPE_HALF, 1))
        return x * cs + partner * sn

    def zero_pad(i, _):
        rows = pl.ds(pl.multiple_of(i * ATT_BLK, ATT_BLK), ATT_BLK)
        for hp in range(2):
            kf[hp, rows, :] = jnp.zeros((ATT_BLK, 128), F32)
            vf[hp, rows, :] = jnp.zeros((ATT_BLK, 128), F32)
        return 0

    lax.fori_loop(0, seq // ATT_BLK, zero_pad, 0)

    def stage(i, _):
        r = pl.multiple_of(i * ATT_BLK, ATT_BLK)
        rows = pl.ds(r, ATT_BLK)
        prow = pl.ds(pl.multiple_of(seq + i * ATT_BLK, ATT_BLK), ATT_BLK)
        cs = cs_ref[0, rows, :]
        sn = sn_ref[0, rows, :]
        for hp in range(2):
            cols = pl.ds(hp * 128, 128)
            qf[hp, rows, :] = rope(q_ref[0, rows, cols].astype(F32), cs, sn) * (ATT_HEAD_DIM ** -0.5)
            kf[hp, prow, :] = rope(k_ref[0, rows, cols].astype(F32), cs, sn)
            vf[hp, prow, :] = v_ref[0, rows, cols].astype(F32)
        return 0

    lax.fori_loop(0, seq // ATT_BLK, stage, 0)

    qi = lax.broadcasted_iota(jnp.int32, (ATT_BLK, 2 * ATT_BLK), 0)
    ki = lax.broadcasted_iota(jnp.int32, (ATT_BLK, 2 * ATT_BLK), 1)
    band = (ki >= qi) & (ki <= qi + ATT_BLK)

    def process(d, init):
        span = ATT_BLK * d

        def body(c, _):
            rho = c % d
            n = c // d
            qstart = rho + n * span
            kstart = seq + qstart - span
            first_key = jnp.where(n > 0, 0, ATT_BLK)
            valid = band & (ki >= first_key)
            qrows = pl.ds(qstart, ATT_BLK, stride=d) if d > 1 else pl.ds(qstart, ATT_BLK)
            krows = pl.ds(kstart, 2 * ATT_BLK, stride=d) if d > 1 else pl.ds(kstart, 2 * ATT_BLK)
            heads = [(hp, hh) for hp in range(2) for hh in range(2)]
            q2 = [qf[hp, qrows, :] for hp in range(2)]
            k2 = [kf[hp, krows, :].astype(BF16) for hp in range(2)]
            v2 = [vf[hp, krows, :].astype(BF16) for hp in range(2)]
            qh = [jnp.where(low_head if hh == 0 else jnp.logical_not(low_head), q2[hp], 0.0).astype(BF16)
                  for hp, hh in heads]
            s = [jnp.where(valid, _nt(qh[i], k2[hp]), NEG) for i, (hp, hh) in enumerate(heads)]
            m = [jnp.max(x, axis=1, keepdims=True) for x in s]
            p = [jnp.exp(x - mx) for x, mx in zip(s, m)]
            l = [jnp.sum(x, axis=1, keepdims=True) for x in p]
            o = [jnp.dot(p[i].astype(BF16), v2[hp], preferred_element_type=F32)
                 for i, (hp, hh) in enumerate(heads)]
            for hp in range(2):
                o_b = jnp.where(low_head, o[2 * hp], o[2 * hp + 1])
                m_b = jnp.where(low_head, m[2 * hp], m[2 * hp + 1])
                l_b = jnp.where(low_head, l[2 * hp], l[2 * hp + 1])
                if init:
                    acc[hp, qrows, :] = o_b
                    m_s[hp, qrows, :] = m_b
                    l_s[hp, qrows, :] = l_b
                else:
                    m_old = m_s[hp, qrows, :]
                    m_new = jnp.maximum(m_old, m_b)
                    a_old = jnp.exp(m_old - m_new)
                    a_new = jnp.exp(m_b - m_new)
                    acc[hp, qrows, :] = acc[hp, qrows, :] * a_old + o_b * a_new
                    l_s[hp, qrows, :] = l_s[hp, qrows, :] * a_old + l_b * a_new
                    m_s[hp, qrows, :] = m_new
            return 0

        lax.fori_loop(0, seq // ATT_BLK, body, 0, unroll=2)

    for gi, (_, d) in enumerate(ATT_GROUPS):
        @pl.when(g == gi)
        def _(d=d, gi=gi):
            process(d, gi == 0)

    @pl.when(g == len(ATT_GROUPS) - 1)
    def _():
        def fin(i, _):
            rows = pl.ds(pl.multiple_of(i * ATT_BLK, ATT_BLK), ATT_BLK)
            for hp in range(2):
                o_ref[0, rows, pl.ds(hp * 128, 128)] = (acc[hp, rows, :] / l_s[hp, rows, :]).astype(BF16)
            return 0

        lax.fori_loop(0, seq // ATT_BLK, fin, 0)


def _attention(proj3, cs, sn):
    B, S, _ = proj3.shape
    ng = len(ATT_GROUPS)
    qb, kb, vb = OFF_AQ // ATT_GROUP_W, OFF_AK // ATT_GROUP_W, OFF_AV // ATT_GROUP_W
    return pl.pallas_call(
        functools.partial(_attn_kernel, seq=S),
        grid=(B, ng),
        in_specs=[pl.BlockSpec((1, S, ATT_GROUP_W), lambda b, g: (b, 0, qb + g)),
                  pl.BlockSpec((1, S, ATT_GROUP_W), lambda b, g: (b, 0, kb + g)),
                  pl.BlockSpec((1, S, ATT_GROUP_W), lambda b, g: (b, 0, vb + g)),
                  pl.BlockSpec((1, S, 128), lambda b, g: (b, 0, 0)),
                  pl.BlockSpec((1, S, 128), lambda b, g: (b, 0, 0))],
        out_specs=pl.BlockSpec((1, S, ATT_GROUP_W), lambda b, g: (b, 0, 0)),
        out_shape=jax.ShapeDtypeStruct((B, S, ATT_GROUP_W), BF16),
        scratch_shapes=[pltpu.VMEM((2, S, 128), F32),
                        pltpu.VMEM((2, 2 * S, 128), F32),
                        pltpu.VMEM((2, 2 * S, 128), F32),
                        pltpu.VMEM((2, S, 128), F32),
                        pltpu.VMEM((2, S, 128), F32),
                        pltpu.VMEM((2, S, 128), F32)],
        compiler_params=pltpu.CompilerParams(
            dimension_semantics=("arbitrary", "arbitrary"), vmem_limit_bytes=VMEM_LIMIT),
        name="dilated_attention",
    )(proj3, proj3, proj3, cs, sn)


def _log_sigmoid(x):
    return jnp.minimum(x, 0.0) - jnp.log(1.0 + jnp.exp(-jnp.abs(x)))


def _mlstm_kernel(mq_ref, mk_ref, mv_ref, mo_ref, gt_ref, cwq_ref, cwk_ref, cbq_ref, cbk_ref,
                  bg_ref, gm_ref, o_ref, q_s, k_s, va_s, rows_s, acc_s, kv_s, inter_s, emt_s,
                  c_s, *, seq):
    h = pl.program_id(1)
    L = MLSTM_BLOCK
    NC = seq // L
    DK, DV = MLSTM_QK_DIM, MLSTM_V_DIM
    DA = DV + 128
    nshift = CONV_WIDTH - 1

    tt = lax.broadcasted_iota(jnp.int32, (nshift * L, 2 * L), 0)
    uu = lax.broadcasted_iota(jnp.int32, (nshift * L, 2 * L), 1)
    shift_mat = (uu == L + tt % L - (tt // L + 1)).astype(BF16)
    conv_w = jnp.concatenate([cwq_ref[...], cwk_ref[...]], axis=1)
    conv_b = jnp.concatenate([cbq_ref[...], cbk_ref[...]], axis=1)
    prev = jnp.zeros((L, 2 * DK), BF16)
    for i in range(NC):
        blk = slice(i * L, (i + 1) * L)
        va_s[blk, 0:DV] = mv_ref[0, blk, :]
        va_s[blk, DV:DA] = jnp.ones((L, DA - DV), BF16)
        cur = jnp.concatenate([mq_ref[0, blk, :], mk_ref[0, blk, :]], axis=1)
        shifted = jnp.dot(shift_mat, jnp.concatenate([prev, cur], axis=0),
                          preferred_element_type=F32)
        y = conv_b + cur.astype(F32) * conv_w[nshift:nshift + 1, :]
        for s in range(nshift):
            y = y + shifted[s * L:(s + 1) * L, :] * conv_w[nshift - 1 - s:nshift - s, :]
        y = _silu(y)
        q_s[blk, :] = y[:, 0:DK].astype(BF16)
        k_s[blk, :] = (y[:, DK:2 * DK] * (DK ** -0.5)).astype(BF16)
        prev = cur

    lane = lax.broadcasted_iota(jnp.int32, (1, 128), 1)
    bias = bg_ref[...]
    b_i = jnp.sum(jnp.where(lane == h, bias, 0.0), axis=1, keepdims=True)
    b_f = jnp.sum(jnp.where(lane == h + MLSTM_HEADS, bias, 0.0), axis=1, keepdims=True)
    ri = lax.broadcasted_iota(jnp.int32, (L, L), 0)
    ci = lax.broadcasted_iota(jnp.int32, (L, L), 1)
    causal = ci <= ri
    eye = (ri == ci).astype(F32)
    i_rows = gt_ref[0, h] + b_i
    lf_rows = _log_sigmoid(gt_ref[0, h + MLSTM_HEADS] + b_f)
    b_rows = jnp.dot(lf_rows, (ri <= ci).astype(F32), preferred_element_type=F32,
                     precision=HIGHEST)
    b_end = b_rows[:, L - 1:L]
    g_rows = b_end - b_rows + i_rows
    g_max = jnp.max(g_rows, axis=1, keepdims=True)
    m = jnp.zeros((1, 1), F32)
    m_prev, m_new = [], []
    for c in range(NC):
        m_prev.append(m)
        m = jnp.maximum(b_end[c:c + 1, :] + m, g_max[c:c + 1, :])
        m_new.append(m)
    m_prev = jnp.concatenate(m_prev, axis=0)
    m_new = jnp.concatenate(m_new, axis=0)
    rows_s[0] = b_rows
    rows_s[1] = jnp.exp(g_rows - m_new)
    rows_s[2] = b_rows - i_rows
    rows_s[3] = jnp.broadcast_to(m_prev, (NC, L))
    rows_s[4] = jnp.broadcast_to(jnp.exp(b_end + m_prev - m_new), (NC, L))

    r2 = lax.broadcasted_iota(jnp.int32, (2 * L, 2 * L), 0)
    c2 = lax.broadcasted_iota(jnp.int32, (2 * L, 2 * L), 1)
    ones_blk = ((r2 < L) == (c2 < L)).astype(BF16)

    G = MLSTM_GROUP

    def local(cg, _):
        cs = [cg * G + i for i in range(G)]
        rows = [pl.ds(pl.multiple_of(c * L, L), L) for c in cs]
        b_r = [rows_s[0, pl.ds(c, 1), :] for c in cs]
        w_r = [rows_s[1, pl.ds(c, 1), :] for c in cs]
        u_r = [rows_s[2, pl.ds(c, 1), :] for c in cs]
        mp = [rows_s[3, pl.ds(c, 1), :] for c in cs]
        q = [q_s[r, :] for r in rows]
        k = [k_s[r, :] for r in rows]
        va = [va_s[r, :] for r in rows]
        qk = [_nt(a, b) for a, b in zip(q, k)]
        x2 = [jnp.concatenate([eye * a, eye * b], axis=1) for a, b in zip(b_r, w_r)]
        hi = [x.astype(BF16) for x in x2]
        lo = [(x - h_.astype(F32)).astype(BF16) for x, h_ in zip(x2, hi)]
        yb = [jnp.dot(h_, ones_blk, preferred_element_type=F32)
              + jnp.dot(l_, ones_blk, preferred_element_type=F32) for h_, l_ in zip(hi, lo)]
        b_b = [y[:, 0:L] for y in yb]
        w_b = [y[:, L:2 * L] for y in yb]
        for i in range(G):
            kv_s[cs[i]] = _tn((w_b[i] * k[i].astype(F32)).astype(BF16), va[i])
        dmat = [jnp.where(causal, b - u, NEG) for b, u in zip(b_b, u_r)]
        m_t = [jnp.maximum(b + m_, jnp.max(d, axis=1, keepdims=True))
               for b, m_, d in zip(b_b, mp, dmat)]
        sc = [a * jnp.exp(d - m_) for a, d, m_ in zip(qk, dmat, m_t)]
        for i in range(G):
            acc_s[rows[i], :] = jnp.dot(sc[i].astype(BF16), va[i], preferred_element_type=F32)
            inter_s[rows[i], :] = jnp.exp(b_b[i] + mp[i] - m_t[i])
            emt_s[rows[i], :] = jnp.exp(-m_t[i])
        return 0

    lax.fori_loop(0, NC // G, local, 0)

    g_row = gm_ref[...]
    c_s[...] = jnp.zeros((DK, DA), F32)

    def recur(cg, _):
        cs = [cg * G + i for i in range(G)]
        rows = [pl.ds(pl.multiple_of(c * L, L), L) for c in cs]
        states = [c_s[...]]
        for c in cs:
            dec = rows_s[4, pl.ds(c, 1), :]
            states.append(jnp.concatenate([dec, dec, dec], axis=1) * states[-1] + kv_s[c])
        c_s[...] = states[G]
        read = [jnp.dot(q_s[r, :], st.astype(BF16), preferred_element_type=F32)
                for r, st in zip(rows, states)]
        inter = [inter_s[r, :] for r in rows]
        out = [acc_s[r, :] + jnp.concatenate([it, it, it], axis=1) * rd
               for r, it, rd in zip(rows, inter, read)]
        emt = [emt_s[r, :] for r in rows]
        nrm = [jnp.maximum(jnp.abs(jnp.concatenate([o[:, DV:DA], o[:, DV:DA]], axis=1)),
                           jnp.concatenate([e_, e_], axis=1)) for o, e_ in zip(out, emt)]
        hh = [o[:, 0:DV] / n_ for o, n_ in zip(out, nrm)]
        ms = [jnp.mean(x * x, axis=1, keepdims=True) for x in hh]
        hn = [x * lax.rsqrt(m_ + NORM_EPS) * g_row for x, m_ in zip(hh, ms)]
        for i in range(G):
            o_ref[0, rows[i], :] = (hn[i] * jax.nn.sigmoid(mo_ref[0, rows[i], :].astype(F32))).astype(BF16)
        return 0

    lax.fori_loop(0, NC // G, recur, 0)


def _mlstm(proj3, gates_t, conv_w, conv_b, bg_row, g_mlstm):
    B, S, _ = proj3.shape
    H, DK, DV = MLSTM_HEADS, MLSTM_QK_DIM, MLSTM_V_DIM
    L = MLSTM_BLOCK
    NC = S // L
    DA = DV + 128
    qb, kb = OFF_MQ // DK, OFF_MK // DK
    vb, ob = OFF_MV // DV, OFF_MO // DV
    nq = (H * DK) // DK
    return pl.pallas_call(
        functools.partial(_mlstm_kernel, seq=S),
        grid=(B, H),
        in_specs=[pl.BlockSpec((1, S, DK), lambda b, h: (b, 0, qb + h)),
                  pl.BlockSpec((1, S, DK), lambda b, h: (b, 0, kb + h)),
                  pl.BlockSpec((1, S, DV), lambda b, h: (b, 0, vb + h)),
                  pl.BlockSpec((1, S, DV), lambda b, h: (b, 0, ob + h)),
                  pl.BlockSpec((1, 2 * H, NC, L), lambda b, h: (b, 0, 0, 0)),
                  pl.BlockSpec((CONV_WIDTH, DK), lambda b, h: (0, h)),
                  pl.BlockSpec((CONV_WIDTH, DK), lambda b, h: (0, nq + h)),
                  pl.BlockSpec((1, DK), lambda b, h: (0, h)),
                  pl.BlockSpec((1, DK), lambda b, h: (0, nq + h)),
                  pl.BlockSpec((1, 128), lambda b, h: (0, 0)),
                  pl.BlockSpec((1, DV), lambda b, h: (0, h))],
        out_specs=pl.BlockSpec((1, S, DV), lambda b, h: (b, 0, h)),
        out_shape=jax.ShapeDtypeStruct((B, S, H * DV), BF16),
        scratch_shapes=[pltpu.VMEM((S, DK), BF16),
                        pltpu.VMEM((S, DK), BF16),
                        pltpu.VMEM((S, DA), BF16),
                        pltpu.VMEM((5, NC, L), F32),
                        pltpu.VMEM((S, DA), F32),
                        pltpu.VMEM((NC, DK, DA), F32),
                        pltpu.VMEM((S, L), F32),
                        pltpu.VMEM((S, L), F32),
                        pltpu.VMEM((DK, DA), F32)],
        compiler_params=pltpu.CompilerParams(
            dimension_semantics=("arbitrary", "arbitrary"), vmem_limit_bytes=VMEM_LIMIT),
        name="mlstm_chunkwise",
    )(proj3, proj3, proj3, proj3, gates_t, conv_w, conv_w, conv_b, conv_b, bg_row, g_mlstm)


def _rms(y, g):
    ms = jnp.mean(y * y, axis=-1, keepdims=True)
    return y * lax.rsqrt(ms + NORM_EPS) * g


def _merge_kernel(ya_ref, yb_ref, ga_ref, gb_ref, x_ref, mod_ref, wa_ref, wb_ref, wo_ref,
                  gpost_ref, gpre_ref, x1_ref, h2_ref):
    pa = jnp.dot(ya_ref[...], wa_ref[...], preferred_element_type=F32)
    pb = jnp.dot(yb_ref[...], wb_ref[...], preferred_element_type=F32)
    merged = (jax.nn.sigmoid(ga_ref[...].astype(F32)) * pa
              + jax.nn.sigmoid(gb_ref[...].astype(F32)) * pb)
    y = jnp.dot(merged.astype(BF16), wo_ref[...], preferred_element_type=F32)
    x1 = x_ref[...] + mod_ref[0, 2:3, :] * _rms(y, gpost_ref[...])
    x1_ref[...] = x1
    h2 = _rms(x1, gpre_ref[...]) * (1.0 + mod_ref[0, 4:5, :]) + mod_ref[0, 3:4, :]
    h2_ref[...] = _pack_pair(h2[:, :HALF], h2[:, HALF:])


def _merge(ya2, yb2, proj2, x2, mod3, wa, wb, wo, g_post, g_pre, seq):
    T = x2.shape[0]
    tm = 512
    per_b = seq // tm
    full = lambda shape: pl.BlockSpec(shape, lambda i: (0,) * len(shape))
    return pl.pallas_call(
        _merge_kernel,
        grid=(T // tm,),
        in_specs=[pl.BlockSpec((tm, ATT_GROUP_W), lambda i: (i, 0)),
                  pl.BlockSpec((tm, D_MODEL), lambda i: (i, 0)),
                  pl.BlockSpec((tm, D_MODEL), lambda i: (i, OFF_GA // D_MODEL)),
                  pl.BlockSpec((tm, D_MODEL), lambda i: (i, OFF_GB // D_MODEL)),
                  pl.BlockSpec((tm, D_MODEL), lambda i: (i, 0)),
                  pl.BlockSpec((1, 6, D_MODEL), lambda i: (i // per_b, 0, 0)),
                  full((ATT_GROUP_W, D_MODEL)), full((D_MODEL, D_MODEL)), full((D_MODEL, D_MODEL)),
                  full((1, D_MODEL)), full((1, D_MODEL))],
        out_specs=[pl.BlockSpec((tm, D_MODEL), lambda i: (i, 0)),
                   pl.BlockSpec((tm, HALF), lambda i: (i, 0))],
        out_shape=[jax.ShapeDtypeStruct((T, D_MODEL), F32),
                   jax.ShapeDtypeStruct((T, HALF), jnp.uint32)],
        compiler_params=pltpu.CompilerParams(
            dimension_semantics=("arbitrary",), vmem_limit_bytes=VMEM_LIMIT),
        name="merge_out_proj",
    )(ya2, yb2, proj2, proj2, x2, mod3, wa, wb, wo, g_post, g_pre)


def _router_kernel(h2_ref, rlo_ref, rhi_ref, bias_ref, idx_ref, w_ref, rank_ref, cnt_ref):
    E = N_EXPERTS
    tr = h2_ref.shape[0]
    gsz = E // N_GROUPS

    @pl.when(pl.program_id(0) == 0)
    def _():
        cnt_ref[...] = jnp.zeros(cnt_ref.shape, F32)

    lo, hi = _unpack_pair(h2_ref[...])
    logits = _nt(rlo_ref[...], lo.astype(BF16)) + _nt(rhi_ref[...], hi.astype(BF16))
    scores = jax.nn.sigmoid(logits)
    sel = scores + bias_ref[:, 0:1]

    gi = lax.broadcasted_iota(jnp.int32, (gsz, tr), 0).astype(F32)
    gs_rows = []
    for g in range(N_GROUPS):
        blk = sel[g * gsz:(g + 1) * gsz, :]
        m1 = jnp.max(blk, axis=0, keepdims=True)
        a1 = jnp.min(jnp.where(blk == m1, gi, float(E)), axis=0, keepdims=True)
        m2 = jnp.max(jnp.where(gi == a1, -jnp.inf, blk), axis=0, keepdims=True)
        gs_rows.append(m1 + m2)
    gs = jnp.concatenate(gs_rows, axis=0)
    g8 = lax.broadcasted_iota(jnp.int32, (N_GROUPS, tr), 0).astype(F32)
    gmask = jnp.zeros((N_GROUPS, tr), F32)
    for _ in range(TOPK_GROUPS):
        m = jnp.max(gs, axis=0, keepdims=True)
        a = jnp.min(jnp.where(gs == m, g8, float(E)), axis=0, keepdims=True)
        hit = g8 == a
        gmask = jnp.where(hit, 1.0, gmask)
        gs = jnp.where(hit, -jnp.inf, gs)
    selm = jnp.concatenate(
        [jnp.where(gmask[g:g + 1, :] > 0.0, sel[g * gsz:(g + 1) * gsz, :], -jnp.inf)
         for g in range(N_GROUPS)], axis=0)

    ei = lax.broadcasted_iota(jnp.int32, (E, tr), 0).astype(F32)
    picks, weights = [], []
    chosen = jnp.zeros((E, tr), F32)
    for _ in range(TOP_K):
        m = jnp.max(selm, axis=0, keepdims=True)
        a = jnp.min(jnp.where(selm == m, ei, float(E)), axis=0, keepdims=True)
        hit = ei == a
        picks.append(a)
        weights.append(jnp.sum(jnp.where(hit, scores, 0.0), axis=0, keepdims=True))
        chosen = jnp.where(hit, 1.0, chosen)
        selm = jnp.where(hit, -jnp.inf, selm)
    wsum = weights[0]
    for w in weights[1:]:
        wsum = wsum + w

    ti = lax.broadcasted_iota(jnp.int32, (tr, tr), 0)
    tj = lax.broadcasted_iota(jnp.int32, (tr, tr), 1)
    before = (ti < tj).astype(BF16)
    pos = jnp.dot(chosen.astype(BF16), before, preferred_element_type=F32) + cnt_ref[:, 0:1]
    ranks = [jnp.sum(jnp.where(ei == a, pos, 0.0), axis=0, keepdims=True) for a in picks]
    cnt_ref[...] = cnt_ref[...] + jnp.sum(chosen, axis=1, keepdims=True)

    idx_ref[...] = jnp.concatenate(picks, axis=0).astype(jnp.int32)
    w_ref[...] = jnp.concatenate([w / wsum * ROUTED_SCALE for w in weights], axis=0)
    rank_ref[...] = jnp.concatenate(ranks, axis=0).astype(jnp.int32)


def _router(h2p, r_lo, r_hi, bias_col, row0, T):
    tr = 512
    off = row0 // tr
    full = lambda shape: pl.BlockSpec(shape, lambda i: (0,) * len(shape))
    return pl.pallas_call(
        _router_kernel,
        grid=(T // tr,),
        in_specs=[pl.BlockSpec((tr, HALF), lambda i: (i + off, 0)),
                  full((N_EXPERTS, HALF)), full((N_EXPERTS, HALF)), full((N_EXPERTS, 128))],
        out_specs=[pl.BlockSpec((TOP_K, tr), lambda i: (0, i)),
                   pl.BlockSpec((TOP_K, tr), lambda i: (0, i)),
                   pl.BlockSpec((TOP_K, tr), lambda i: (0, i)),
                   full((N_EXPERTS, 128))],
        out_shape=[jax.ShapeDtypeStruct((TOP_K, T), jnp.int32),
                   jax.ShapeDtypeStruct((TOP_K, T), F32),
                   jax.ShapeDtypeStruct((TOP_K, T), jnp.int32),
                   jax.ShapeDtypeStruct((N_EXPERTS, 128), F32)],
        compiler_params=pltpu.CompilerParams(
            dimension_semantics=("arbitrary",), vmem_limit_bytes=VMEM_LIMIT),
        name="router_topk",
    )(h2p, r_lo, r_hi, bias_col)


def _dest_kernel(idx_ref, rank_ref, pstart_ref, dest_ref):
    tr = idx_ref.shape[1]
    ei = lax.broadcasted_iota(jnp.int32, (N_EXPERTS, tr), 0)
    start = pstart_ref[:, 0:1]
    rows = []
    for k in range(TOP_K):
        hit = ei == idx_ref[k:k + 1, :]
        rows.append(jnp.sum(jnp.where(hit, start, 0.0), axis=0, keepdims=True))
    dest_ref[...] = jnp.concatenate(rows, axis=0).astype(jnp.int32) + rank_ref[...]


def _slot_index(idx, rank, pstart_col):
    T = idx.shape[1]
    tr = 1024
    return pl.pallas_call(
        _dest_kernel,
        grid=(T // tr,),
        in_specs=[pl.BlockSpec((TOP_K, tr), lambda i: (0, i)),
                  pl.BlockSpec((TOP_K, tr), lambda i: (0, i)),
                  pl.BlockSpec((N_EXPERTS, 128), lambda i: (0, 0))],
        out_specs=pl.BlockSpec((TOP_K, tr), lambda i: (0, i)),
        out_shape=jax.ShapeDtypeStruct((TOP_K, T), jnp.int32),
        name="slot_index",
    )(idx, rank, pstart_col)


def _ffn_kernel(first_ref, nblk_ref, nused_ref, xs_hbm, wg_ref, wu_ref, wd_ref, ys_hbm,
                xbuf, ybuf, in_sem, out_sem, wg_s, wu_s, wd_s):
    e = pl.program_id(0)
    bm = EXPERT_BLOCK
    ns = EXPERT_SLOTS
    nused = nused_ref[0]
    first = first_ref[e]
    n = nblk_ref[e]

    def in_copy(g):
        slot = g % ns
        return pltpu.make_async_copy(xs_hbm.at[pl.ds(g * bm, bm)], xbuf.at[slot], in_sem.at[slot])

    def out_copy(g):
        slot = g % ns
        return pltpu.make_async_copy(ybuf.at[slot], ys_hbm.at[pl.ds(g * bm, bm)], out_sem.at[slot])

    for q in range(ns - 1):
        @pl.when((e == 0) & (nused > q))
        def _(q=q):
            in_copy(q).start()

    @pl.when(n > 0)
    def _():
        wg_s[...] = wg_ref[0].astype(BF16)
        wu_s[...] = wu_ref[0].astype(BF16)
        wd_s[...] = wd_ref[0].astype(BF16)

        def block(j, _):
            g = first + j
            slot = g % ns
            in_copy(g).wait()

            @pl.when(g + ns - 1 < nused)
            def _():
                in_copy(g + ns - 1).start()

            lo, hi = _unpack_pair(xbuf[slot])
            x = jnp.concatenate([lo.astype(BF16), hi.astype(BF16)], axis=1)
            gate = jnp.dot(x, wg_s[...], preferred_element_type=F32)
            up = jnp.dot(x, wu_s[...], preferred_element_type=F32)
            hid = (_silu(gate) * up).astype(BF16)
            out = jnp.dot(hid, wd_s[...], preferred_element_type=F32)

            @pl.when(g >= ns)
            def _():
                out_copy(g - ns).wait()

            ybuf[slot] = _pack_pair(out[:, :HALF], out[:, HALF:])
            out_copy(g).start()
            return 0

        lax.fori_loop(0, n, block, 0)

    @pl.when(e == pl.num_programs(0) - 1)
    def _():
        for q in range(ns, 0, -1):
            @pl.when(nused >= q)
            def _(q=q):
                out_copy(nused - q).wait()


def _expert_ffn(first_blk, nblk, nused, xs, w_gate, w_up, w_down):
    P = xs.shape[0]
    bm = EXPERT_BLOCK
    w_map = lambda e, *_: (e, 0, 0)
    grid_spec = pltpu.PrefetchScalarGridSpec(
        num_scalar_prefetch=3,
        grid=(w_gate.shape[0],),
        in_specs=[pl.BlockSpec(memory_space=pl.ANY),
                  pl.BlockSpec((1, D_MODEL, EXPERT_FF), w_map),
                  pl.BlockSpec((1, D_MODEL, EXPERT_FF), w_map),
                  pl.BlockSpec((1, EXPERT_FF, D_MODEL), w_map)],
        out_specs=pl.BlockSpec(memory_space=pl.ANY),
        scratch_shapes=[pltpu.VMEM((EXPERT_SLOTS, bm, HALF), jnp.uint32),
                        pltpu.VMEM((EXPERT_SLOTS, bm, HALF), jnp.uint32),
                        pltpu.SemaphoreType.DMA((EXPERT_SLOTS,)),
                        pltpu.SemaphoreType.DMA((EXPERT_SLOTS,)),
                        pltpu.VMEM((D_MODEL, EXPERT_FF), BF16),
                        pltpu.VMEM((D_MODEL, EXPERT_FF), BF16),
                        pltpu.VMEM((EXPERT_FF, D_MODEL), BF16)],
    )
    return pl.pallas_call(
        _ffn_kernel,
        grid_spec=grid_spec,
        out_shape=jax.ShapeDtypeStruct((P, HALF), jnp.uint32),
        compiler_params=pltpu.CompilerParams(
            dimension_semantics=("arbitrary",), vmem_limit_bytes=VMEM_LIMIT),
        name="routed_experts",
    )(first_blk, nblk, nused, xs, w_gate, w_up, w_down)


def _final_kernel(yg_ref, w_ref, h2_ref, x1_ref, mod_ref, wsg_ref, wsu_ref, wsd_ref, gpost_ref, *rest):
    o_ref = rest[-1]
    lo, hi = _unpack_pair(h2_ref[...])
    h2 = jnp.concatenate([lo.astype(BF16), hi.astype(BF16)], axis=1)
    gate = jnp.dot(h2, wsg_ref[...], preferred_element_type=F32)
    up = jnp.dot(h2, wsu_ref[...], preferred_element_type=F32)
    shared = jnp.dot((_silu(gate) * up).astype(BF16), wsd_ref[...], preferred_element_type=F32)
    y_lo = shared[:, :HALF]
    y_hi = shared[:, HALF:]
    for k in range(TOP_K):
        r_lo, r_hi = _unpack_pair(yg_ref[k])
        wk = w_ref[:, k:k + 1]
        y_lo = y_lo + wk * r_lo
        y_hi = y_hi + wk * r_hi
    ms = (jnp.sum(y_lo * y_lo, axis=-1, keepdims=True)
          + jnp.sum(y_hi * y_hi, axis=-1, keepdims=True)) * (1.0 / D_MODEL)
    inv = lax.rsqrt(ms + NORM_EPS)
    o_ref[:, 0:HALF] = x1_ref[:, 0:HALF] + mod_ref[0, 5:6, 0:HALF] * (y_lo * inv * gpost_ref[:, 0:HALF])
    o_ref[:, HALF:] = x1_ref[:, HALF:] + mod_ref[0, 5:6, HALF:] * (y_hi * inv * gpost_ref[:, HALF:])


def _final(yg, w_tk, h2p, x1, mod3, wsg, wsu, wsd, g_post, seq, row0, out_prev):
    T = x1.shape[0]
    tp = yg.shape[1]
    tm = 256
    per_b = seq // tm
    off = row0 // tm
    full = lambda shape: pl.BlockSpec(shape, lambda i: (0,) * len(shape))
    in_specs = [pl.BlockSpec((TOP_K, tm, HALF), lambda i: (0, i, 0)),
                pl.BlockSpec((tm, TOP_K), lambda i: (i, 0)),
                pl.BlockSpec((tm, HALF), lambda i: (i + off, 0)),
                pl.BlockSpec((tm, D_MODEL), lambda i: (i + off, 0)),
                pl.BlockSpec((1, 6, D_MODEL), lambda i: ((i + off) // per_b, 0, 0)),
                full((D_MODEL, EXPERT_FF)), full((D_MODEL, EXPERT_FF)), full((EXPERT_FF, D_MODEL)),
                full((1, D_MODEL))]
    args = [yg, w_tk, h2p, x1, mod3, wsg, wsu, wsd, g_post]
    aliases = {}
    if out_prev is not None:
        in_specs.append(pl.BlockSpec(memory_space=pl.ANY))
        args.append(out_prev)
        aliases = {len(args) - 1: 0}
    return pl.pallas_call(
        _final_kernel,
        grid=(tp // tm,),
        in_specs=in_specs,
        out_specs=pl.BlockSpec((tm, D_MODEL), lambda i: (i + off, 0)),
        out_shape=jax.ShapeDtypeStruct((T, D_MODEL), F32),
        input_output_aliases=aliases,
        compiler_params=pltpu.CompilerParams(
            dimension_semantics=("arbitrary",), vmem_limit_bytes=VMEM_LIMIT),
        name="shared_expert_combine",
    )(*args)


def _rope_tables(positions):
    inv = jnp.power(ROPE_THETA, -jnp.arange(ROPE_HALF, dtype=F32) / ROPE_HALF)
    ang = positions.astype(F32)[..., None] * inv
    cos, sin = jnp.cos(ang), jnp.sin(ang)
    rest = ATT_HEAD_DIM - 2 * ROPE_HALF
    cs = jnp.concatenate([cos, cos, jnp.ones(ang.shape[:-1] + (rest,), F32)], axis=-1)
    sn = jnp.concatenate([-sin, sin, jnp.zeros(ang.shape[:-1] + (rest,), F32)], axis=-1)
    return jnp.tile(cs, (1, 1, 2)), jnp.tile(sn, (1, 1, 2))


def _layer(x, c, positions, w_ada, b_ada, g_pre_mix, g_post_mix, g_pre_ffn, g_post_ffn,
           w_in, conv_w, conv_b, b_gates, g_mlstm, w_branch_a, w_branch_b, w_out,
           router_w, router_bias, w_exp_gate, w_exp_up, w_exp_down, w_sh_gate, w_sh_up, w_sh_down):
    B, S, D = x.shape
    T = B * S
    H = MLSTM_HEADS
    x2 = x.reshape(T, D)

    mod3 = _adaln(c, w_ada, b_ada).reshape(B, 6, D)

    a_w = 3 * ATT_GROUP_W
    o_mq = 3 * a_w
    o_mk = o_mq + H * MLSTM_QK_DIM
    o_mv = o_mk + H * MLSTM_QK_DIM
    o_mo = o_mv + H * MLSTM_V_DIM
    o_mi = o_mo + H * MLSTM_V_DIM
    o_ga = o_mi + 2 * H
    o_gb = o_ga + D
    seg = lambda o, w: w_in[:, o:o + w]
    w_main = jnp.concatenate(
        [seg(o_mv, H * MLSTM_V_DIM), seg(o_mo, H * MLSTM_V_DIM), seg(o_ga, D), seg(o_gb, D),
         seg(o_mq, H * MLSTM_QK_DIM), seg(o_mk, H * MLSTM_QK_DIM),
         seg(0, a_w), seg(a_w, a_w), seg(2 * a_w, a_w)], axis=1).astype(BF16)
    w_if = jnp.pad(seg(o_mi, 2 * H), ((0, 0), (0, 128 - 2 * H))).astype(BF16)

    proj, gates = _in_proj(x2, mod3, g_pre_mix.reshape(1, D), w_main, w_if, S)
    proj3 = proj.reshape(B, S, PROJ_W)

    cs, sn = _rope_tables(positions)
    y_a = _attention(proj3, cs, sn)

    bg_row = jnp.pad(b_gates.reshape(1, 2 * H), ((0, 0), (0, 128 - 2 * H)))
    gates_t = gates[:, :2 * H].reshape(B, S, 2 * H).transpose(0, 2, 1)
    gates_t = gates_t.reshape(B, 2 * H, S // MLSTM_BLOCK, MLSTM_BLOCK)
    y_b = _mlstm(proj3, gates_t, conv_w, conv_b.reshape(1, -1), bg_row, g_mlstm.reshape(1, -1))

    x1, h2p = _merge(y_a.reshape(T, ATT_GROUP_W), y_b.reshape(T, D), proj, x2, mod3,
                     w_branch_a.astype(BF16), w_branch_b.astype(BF16), w_out.astype(BF16),
                     g_post_mix.reshape(1, D), g_pre_ffn.reshape(1, D), S)

    rw_t = router_w.T.astype(BF16)
    bias_col = jnp.broadcast_to(router_bias.reshape(N_EXPERTS, 1), (N_EXPERTS, 128))
    wsg, wsu, wsd = w_sh_gate.astype(BF16), w_sh_up.astype(BF16), w_sh_down.astype(BF16)

    tp = T // MOE_PARTS
    bm = EXPERT_BLOCK
    nb = (tp * TOP_K) // bm + N_EXPERTS
    out = None
    for part in range(MOE_PARTS):
        row0 = part * tp
        idx, wts, rank, cnt = _router(h2p, rw_t[:, :HALF], rw_t[:, HALF:], bias_col, row0, tp)

        counts = cnt[:, 0].astype(jnp.int32)
        padded = (counts + bm - 1) // bm * bm
        pend = jnp.cumsum(padded)
        pstart = pend - padded
        pstart_col = jnp.broadcast_to(pstart.astype(F32).reshape(N_EXPERTS, 1), (N_EXPERTS, 128))
        dest = _slot_index(idx, rank, pstart_col)
        nused = (pend[-1] // bm).astype(jnp.int32).reshape(1)

        xs = _dispatch(h2p, dest, nb * bm, row0)
        ys = _expert_ffn((pstart // bm).astype(jnp.int32), (padded // bm).astype(jnp.int32), nused,
                         xs, w_exp_gate, w_exp_up, w_exp_down)
        yg = _collect(ys, dest)
        out = _final(yg, wts.T, h2p, x1, mod3, wsg, wsu, wsd, g_post_ffn.reshape(1, D), S, row0, out)
    return out.reshape(B, S, D)


SC_CORES = 2
SC_SUBCORES = 16
SC_WORKERS = SC_CORES * SC_SUBCORES
SC_ROWS = 64


def _sc_mesh():
    return plsc.VectorSubcoreMesh(core_axis_name="c", subcore_axis_name="s",
                                  num_cores=SC_CORES, num_subcores=SC_SUBCORES)


def _worker_id():
    return lax.axis_index("s") * SC_CORES + lax.axis_index("c")


def _dispatch(h2p, dest, n_slots, row0):
    T = dest.shape[1]
    per_w = T // SC_WORKERS
    nch = per_w // SC_ROWS
    idx = dest.reshape(TOP_K, SC_WORKERS, nch, SC_ROWS).transpose(1, 2, 0, 3)
    idx = idx.reshape(SC_WORKERS, nch * TOP_K, SC_ROWS)

    def body(x_hbm, idx_hbm, xs_hbm, idx_v, buf0, buf1, rsem0, rsem1, ssem0, ssem1):
        wid = _worker_id()
        base = row0 + wid * per_w
        pltpu.sync_copy(idx_hbm.at[wid], idx_v)
        bufs = ((buf0, rsem0, ssem0), (buf1, rsem1, ssem1))

        def read(c, buf, rsem):
            return pltpu.make_async_copy(x_hbm.at[pl.ds(base + c * SC_ROWS, SC_ROWS)], buf, rsem)

        def scatter(c, k, buf, ssem):
            return pltpu.make_async_copy(buf, xs_hbm.at[idx_v.at[c * TOP_K + k]], ssem)

        read(0, buf0, rsem0).start()

        @pl.loop(0, nch, step=2)
        def _(c0):
            for b in range(2):
                c = c0 + b
                buf, rsem, ssem = bufs[b]
                obuf, orsem, ossem = bufs[1 - b]
                read(c, buf, rsem).wait()

                @pl.when(c > 0)
                def _():
                    for k in range(TOP_K):
                        scatter(c - 1, k, obuf, ossem).wait()

                @pl.when(c + 1 < nch)
                def _():
                    read(c + 1, obuf, orsem).start()

                for k in range(TOP_K):
                    scatter(c, k, buf, ssem).start()

        for k in range(TOP_K):
            scatter(nch - 1, k, buf1, ssem1).wait()

    run = pl.kernel(
        body,
        out_type=jax.ShapeDtypeStruct((n_slots, HALF), jnp.uint32),
        mesh=_sc_mesh(),
        scratch_types=[pltpu.VMEM((nch * TOP_K, SC_ROWS), jnp.int32),
                       pltpu.VMEM((SC_ROWS, HALF), jnp.uint32),
                       pltpu.VMEM((SC_ROWS, HALF), jnp.uint32),
                       pltpu.SemaphoreType.DMA, pltpu.SemaphoreType.DMA,
                       pltpu.SemaphoreType.DMA, pltpu.SemaphoreType.DMA],
        name="sc_dispatch",
    )
    return run(h2p, idx)


def _collect(ys, dest):
    n = dest.size
    per_w = n // SC_WORKERS
    nch = per_w // SC_ROWS
    idx = dest.reshape(SC_WORKERS, nch, SC_ROWS)

    def body(ys_hbm, idx_hbm, out_hbm, idx_v, buf0, buf1, gsem0, gsem1, wsem0, wsem1):
        wid = _worker_id()
        base = wid * per_w
        pltpu.sync_copy(idx_hbm.at[wid], idx_v)
        bufs = ((buf0, gsem0, wsem0), (buf1, gsem1, wsem1))

        def gather(c, buf, gsem):
            return pltpu.make_async_copy(ys_hbm.at[idx_v.at[c]], buf, gsem)

        def write(c, buf, wsem):
            return pltpu.make_async_copy(buf, out_hbm.at[pl.ds(base + c * SC_ROWS, SC_ROWS)], wsem)

        gather(0, buf0, gsem0).start()

        @pl.loop(0, nch, step=2)
        def _(c0):
            for b in range(2):
                c = c0 + b
                buf, gsem, wsem = bufs[b]
                obuf, ogsem, owsem = bufs[1 - b]
                gather(c, buf, gsem).wait()

                @pl.when(c > 0)
                def _():
                    write(c - 1, obuf, owsem).wait()

                @pl.when(c + 1 < nch)
                def _():
                    gather(c + 1, obuf, ogsem).start()

                write(c, buf, wsem).start()

        write(nch - 1, buf1, wsem1).wait()

    run = pl.kernel(
        body,
        out_type=jax.ShapeDtypeStruct((n, HALF), jnp.uint32),
        mesh=_sc_mesh(),
        scratch_types=[pltpu.VMEM((nch, SC_ROWS), jnp.int32),
                       pltpu.VMEM((SC_ROWS, HALF), jnp.uint32),
                       pltpu.VMEM((SC_ROWS, HALF), jnp.uint32),
                       pltpu.SemaphoreType.DMA, pltpu.SemaphoreType.DMA,
                       pltpu.SemaphoreType.DMA, pltpu.SemaphoreType.DMA],
        name="sc_collect",
    )
    return run(ys, idx).reshape(dest.shape + (HALF,))


def kernel(x, c, positions, w_ada, b_ada, g_pre_mix, g_post_mix, g_pre_ffn, g_post_ffn, w_in, conv_w, conv_b, b_gates, g_mlstm, w_branch_a, w_branch_b, w_out, router_w, router_bias, w_exp_gate, w_exp_up, w_exp_down, w_sh_gate, w_sh_up, w_sh_down):
    depth = w_ada.shape[0]
    for l in range(depth):
        x = _layer(x, c, positions, w_ada[l], b_ada[l], g_pre_mix[l], g_post_mix[l], g_pre_ffn[l],
                   g_post_ffn[l], w_in[l], conv_w[l], conv_b[l], b_gates[l], g_mlstm[l],
                   w_branch_a[l], w_branch_b[l], w_out[l], router_w[l], router_bias[l],
                   w_exp_gate[l], w_exp_up[l], w_exp_down[l], w_sh_gate[l], w_sh_up[l], w_sh_down[l])
    return x
```

```python
import functools

import jax
import jax.numpy as jnp
from jax import lax
from jax.experimental import pallas as pl
from jax.experimental.pallas import tpu as pltpu
from jax.experimental.pallas import tpu_sc as plsc

F32 = jnp.float32
BF16 = jnp.bfloat16
HIGHEST = lax.Precision.HIGHEST

D_MODEL = 1024
ATT_GROUPS = ((128, 1), (512, 4), (2048, 16))
ATT_HEAD_DIM = 64
ATT_GROUP_W = 256
ATT_BLK = 128
ROPE_THETA = 500000.0
ROPE_HALF = 8
MLSTM_HEADS = 4
MLSTM_QK_DIM = 128
MLSTM_V_DIM = 256
MLSTM_BLOCK = 128
MLSTM_GROUP = 4
CONV_WIDTH = 4
N_EXPERTS = 256
TOP_K = 8
N_GROUPS = 8
TOPK_GROUPS = 4
EXPERT_FF = 256
ROUTED_SCALE = 2.5
NORM_EPS = 1e-6
NEG = -1e30

OFF_MV, OFF_MO, OFF_GA, OFF_GB = 0, 1024, 2048, 3072
OFF_MQ, OFF_MK = 4096, 4608
OFF_AQ, OFF_AK, OFF_AV = 5120, 5888, 6656
PROJ_W = 7424
HALF = D_MODEL // 2

EXPERT_BLOCK = 512
EXPERT_SLOTS = 6
MOE_PARTS = 2
VMEM_LIMIT = 56 * 1024 * 1024


def _nt(a, b, precision=None):
    return lax.dot_general(a, b, (((1,), (1,)), ((), ())), preferred_element_type=F32,
                           precision=precision)


def _tn(a, b):
    return lax.dot_general(a, b, (((0,), (0,)), ((), ())), preferred_element_type=F32)


_sigmoid = jax.nn.sigmoid


def _silu(x):
    return x * _sigmoid(x)


def _pack_pair(lo, hi):
    lo_b = pltpu.bitcast(lo.astype(BF16).astype(F32), jnp.uint32)
    hi_b = pltpu.bitcast(hi.astype(BF16).astype(F32), jnp.uint32)
    return (lo_b >> 16) | (hi_b & jnp.uint32(0xFFFF0000))


def _unpack_pair(w):
    lo = pltpu.bitcast(w << 16, F32)
    hi = pltpu.bitcast(w & jnp.uint32(0xFFFF0000), F32)
    return lo, hi


def _mod_kernel(c_ref, w_ref, b_ref, o_ref):
    a = _silu(c_ref[...])
    o_ref[...] = jnp.dot(a, w_ref[...], preferred_element_type=F32, precision=HIGHEST) + b_ref[...]


def _adaln(c, w_ada, b_ada):
    B = c.shape[0]
    n = w_ada.shape[1]
    tn = 512
    return pl.pallas_call(
        _mod_kernel,
        grid=(n // tn,),
        in_specs=[pl.BlockSpec((B, D_MODEL), lambda j: (0, 0)),
                  pl.BlockSpec((D_MODEL, tn), lambda j: (0, j)),
                  pl.BlockSpec((1, tn), lambda j: (0, j))],
        out_specs=pl.BlockSpec((B, tn), lambda j: (0, j)),
        out_shape=jax.ShapeDtypeStruct((B, n), F32),
        name="adaln_mod",
    )(c, w_ada, b_ada.reshape(1, n))


def _proj_kernel(x_ref, mod_ref, g_ref, w_ref, wif_ref, o_ref, gates_ref, h_ref):
    @pl.when(pl.program_id(1) == 0)
    def _():
        x = x_ref[...]
        ms = jnp.mean(x * x, axis=-1, keepdims=True)
        y = x * lax.rsqrt(ms + NORM_EPS) * g_ref[...]
        h = (y * (1.0 + mod_ref[0, 1:2, :]) + mod_ref[0, 0:1, :]).astype(BF16)
        h_ref[...] = h
        gates_ref[...] = jnp.dot(h, wif_ref[...], preferred_element_type=F32)

    o_ref[...] = jnp.dot(h_ref[...], w_ref[...], preferred_element_type=F32).astype(BF16)


def _in_proj(x2, mod3, g_pre, w_main, w_if, seq):
    T = x2.shape[0]
    tm, tn = 1024, PROJ_W // 2
    per_b = seq // tm
    return pl.pallas_call(
        _proj_kernel,
        grid=(T // tm, PROJ_W // tn),
        in_specs=[pl.BlockSpec((tm, D_MODEL), lambda i, j: (i, 0)),
                  pl.BlockSpec((1, 6, D_MODEL), lambda i, j: (i // per_b, 0, 0)),
                  pl.BlockSpec((1, D_MODEL), lambda i, j: (0, 0)),
                  pl.BlockSpec((D_MODEL, tn), lambda i, j: (0, j)),
                  pl.BlockSpec((D_MODEL, 128), lambda i, j: (0, 0))],
        out_specs=[pl.BlockSpec((tm, tn), lambda i, j: (i, j)),
                   pl.BlockSpec((tm, 128), lambda i, j: (i, 0))],
        out_shape=[jax.ShapeDtypeStruct((T, PROJ_W), BF16),
                   jax.ShapeDtypeStruct((T, 128), F32)],
        scratch_shapes=[pltpu.VMEM((tm, D_MODEL), BF16)],
        compiler_params=pltpu.CompilerParams(
            dimension_semantics=("arbitrary", "arbitrary"), vmem_limit_bytes=VMEM_LIMIT),
        name="norm_in_proj",
    )(x2, mod3, g_pre, w_main, w_if)


def _attn_kernel(q_ref, k_ref, v_ref, cs_ref, sn_ref, o_ref, qf, kf, vf, acc, m_s, l_s, *, seq):
    g = pl.program_id(1)
    lane = lax.broadcasted_iota(jnp.int32, (ATT_BLK, 128), 1)
    first = (lane % ATT_HEAD_DIM) < ROPE_HALF
    low_head = lane < ATT_HEAD_DIM

    def rope(x, cs, sn):
        partner = jnp.where(first, pltpu.roll(x, 128 - ROPE_HALF, 1), pltpu.roll(x, ROPE_HALF, 1))
        return x * cs + partner * sn

    def zero_pad(i, _):
        rows = pl.ds(pl.multiple_of(i * ATT_BLK, ATT_BLK), ATT_BLK)
        for hp in range(2):
            kf[hp, rows, :] = jnp.zeros((ATT_BLK, 128), F32)
            vf[hp, rows, :] = jnp.zeros((ATT_BLK, 128), F32)
        return 0

    lax.fori_loop(0, seq // ATT_BLK, zero_pad, 0)

    def stage(i, _):
        r = pl.multiple_of(i * ATT_BLK, ATT_BLK)
        rows = pl.ds(r, ATT_BLK)
        prow = pl.ds(pl.multiple_of(seq + i * ATT_BLK, ATT_BLK), ATT_BLK)
        cs = cs_ref[0, rows, :]
        sn = sn_ref[0, rows, :]
        for hp in range(2):
            cols = pl.ds(hp * 128, 128)
            qf[hp, rows, :] = rope(q_ref[0, rows, cols].astype(F32), cs, sn) * (ATT_HEAD_DIM ** -0.5)
            kf[hp, prow, :] = rope(k_ref[0, rows, cols].astype(F32), cs, sn)
            vf[hp, prow, :] = v_ref[0, rows, cols].astype(F32)
        return 0

    lax.fori_loop(0, seq // ATT_BLK, stage, 0)

    qi = lax.broadcasted_iota(jnp.int32, (ATT_BLK, 2 * ATT_BLK), 0)
    ki = lax.broadcasted_iota(jnp.int32, (ATT_BLK, 2 * ATT_BLK), 1)
    band = (ki >= qi) & (ki <= qi + ATT_BLK)

    def process(d, init):
        span = ATT_BLK * d

        def body(c, _):
            rho = c % d
            n = c // d
            qstart = rho + n * span
            kstart = seq + qstart - span
            first_key = jnp.where(n > 0, 0, ATT_BLK)
            valid = band & (ki >= first_key)
            qrows = pl.ds(qstart, ATT_BLK, stride=d) if d > 1 else pl.ds(qstart, ATT_BLK)
            krows = pl.ds(kstart, 2 * ATT_BLK, stride=d) if d > 1 else pl.ds(kstart, 2 * ATT_BLK)
            heads = [(hp, hh) for hp in range(2) for hh in range(2)]
            q2 = [qf[hp, qrows, :] for hp in range(2)]
            k2 = [kf[hp, krows, :].astype(BF16) for hp in range(2)]
            v2 = [vf[hp, krows, :].astype(BF16) for hp in range(2)]
            qh = [jnp.where(low_head if hh == 0 else jnp.logical_not(low_head), q2[hp], 0.0).astype(BF16)
                  for hp, hh in heads]
            s = [jnp.where(valid, _nt(qh[i], k2[hp]), NEG) for i, (hp, hh) in enumerate(heads)]
            m = [jnp.max(x, axis=1, keepdims=True) for x in s]
            p = [jnp.exp(x - mx) for x, mx in zip(s, m)]
            l = [jnp.sum(x, axis=1, keepdims=True) for x in p]
            o = [jnp.dot(p[i].astype(BF16), v2[hp], preferred_element_type=F32)
                 for i, (hp, hh) in enumerate(heads)]
            for hp in range(2):
                o_b = jnp.where(low_head, o[2 * hp], o[2 * hp + 1])
                m_b = jnp.where(low_head, m[2 * hp], m[2 * hp + 1])
                l_b = jnp.where(low_head, l[2 * hp], l[2 * hp + 1])
                if init:
                    acc[hp, qrows, :] = o_b
                    m_s[hp, qrows, :] = m_b
                    l_s[hp, qrows, :] = l_b
                else:
                    m_old = m_s[hp, qrows, :]
                    m_new = jnp.maximum(m_old, m_b)
                    a_old = jnp.exp(m_old - m_new)
                    a_new = jnp.exp(m_b - m_new)
                    acc[hp, qrows, :] = acc[hp, qrows, :] * a_old + o_b * a_new
                    l_s[hp, qrows, :] = l_s[hp, qrows, :] * a_old + l_b * a_new
                    m_s[hp, qrows, :] = m_new
            return 0

        lax.fori_loop(0, seq // ATT_BLK, body, 0, unroll=2)

    for gi, (_, d) in enumerate(ATT_GROUPS):
        @pl.when(g == gi)
        def _(d=d, gi=gi):
            process(d, gi == 0)

    @pl.when(g == len(ATT_GROUPS) - 1)
    def _():
        def fin(i, _):
            rows = pl.ds(pl.multiple_of(i * ATT_BLK, ATT_BLK), ATT_BLK)
            for hp in range(2):
                o_ref[0, rows, pl.ds(hp * 128, 128)] = (acc[hp, rows, :] / l_s[hp, rows, :]).astype(BF16)
            return 0

        lax.fori_loop(0, seq // ATT_BLK, fin, 0)


def _attention(proj3, cs, sn):
    B, S, _ = proj3.shape
    ng = len(ATT_GROUPS)
    qb, kb, vb = OFF_AQ // ATT_GROUP_W, OFF_AK // ATT_GROUP_W, OFF_AV // ATT_GROUP_W
    return pl.pallas_call(
        functools.partial(_attn_kernel, seq=S),
        grid=(B, ng),
        in_specs=[pl.BlockSpec((1, S, ATT_GROUP_W), lambda b, g: (b, 0, qb + g)),
                  pl.BlockSpec((1, S, ATT_GROUP_W), lambda b, g: (b, 0, kb + g)),
                  pl.BlockSpec((1, S, ATT_GROUP_W), lambda b, g: (b, 0, vb + g)),
                  pl.BlockSpec((1, S, 128), lambda b, g: (b, 0, 0)),
                  pl.BlockSpec((1, S, 128), lambda b, g: (b, 0, 0))],
        out_specs=pl.BlockSpec((1, S, ATT_GROUP_W), lambda b, g: (b, 0, 0)),
        out_shape=jax.ShapeDtypeStruct((B, S, ATT_GROUP_W), BF16),
        scratch_shapes=[pltpu.VMEM((2, S, 128), F32),
                        pltpu.VMEM((2, 2 * S, 128), F32),
                        pltpu.VMEM((2, 2 * S, 128), F32),
                        pltpu.VMEM((2, S, 128), F32),
                        pltpu.VMEM((2, S, 128), F32),
                        pltpu.VMEM((2, S, 128), F32)],
        compiler_params=pltpu.CompilerParams(
            dimension_semantics=("arbitrary", "arbitrary"), vmem_limit_bytes=VMEM_LIMIT),
        name="dilated_attention",
    )(proj3, proj3, proj3, cs, sn)


def _log_sigmoid(x):
    return jnp.minimum(x, 0.0) - jnp.log(1.0 + jnp.exp(-jnp.abs(x)))


def _mlstm_kernel(mq_ref, mk_ref, mv_ref, mo_ref, gt_ref, cwq_ref, cwk_ref, cbq_ref, cbk_ref,
                  bg_ref, gm_ref, o_ref, q_s, k_s, va_s, rows_s, acc_s, kv_s, inter_s, emt_s,
                  c_s, *, seq):
    h = pl.program_id(1)
    L = MLSTM_BLOCK
    NC = seq // L
    DK, DV = MLSTM_QK_DIM, MLSTM_V_DIM
    DA = DV + 128
    nshift = CONV_WIDTH - 1

    tt = lax.broadcasted_iota(jnp.int32, (nshift * L, 2 * L), 0)
    uu = lax.broadcasted_iota(jnp.int32, (nshift * L, 2 * L), 1)
    shift_mat = (uu == L + tt % L - (tt // L + 1)).astype(BF16)
    conv_w = jnp.concatenate([cwq_ref[...], cwk_ref[...]], axis=1)
    conv_b = jnp.concatenate([cbq_ref[...], cbk_ref[...]], axis=1)
    prev = jnp.zeros((L, 2 * DK), BF16)
    for i in range(NC):
        blk = slice(i * L, (i + 1) * L)
        va_s[blk, 0:DV] = mv_ref[0, blk, :]
        va_s[blk, DV:DA] = jnp.ones((L, DA - DV), BF16)
        cur = jnp.concatenate([mq_ref[0, blk, :], mk_ref[0, blk, :]], axis=1)
        shifted = jnp.dot(shift_mat, jnp.concatenate([prev, cur], axis=0),
                          preferred_element_type=F32)
        y = conv_b + cur.astype(F32) * conv_w[nshift:nshift + 1, :]
        for s in range(nshift):
            y = y + shifted[s * L:(s + 1) * L, :] * conv_w[nshift - 1 - s:nshift - s, :]
        y = _silu(y)
        q_s[blk, :] = y[:, 0:DK].astype(BF16)
        k_s[blk, :] = (y[:, DK:2 * DK] * (DK ** -0.5)).astype(BF16)
        prev = cur

    lane = lax.broadcasted_iota(jnp.int32, (1, 128), 1)
    bias = bg_ref[...]
    b_i = jnp.sum(jnp.where(lane == h, bias, 0.0), axis=1, keepdims=True)
    b_f = jnp.sum(jnp.where(lane == h + MLSTM_HEADS, bias, 0.0), axis=1, keepdims=True)
    ri = lax.broadcasted_iota(jnp.int32, (L, L), 0)
    ci = lax.broadcasted_iota(jnp.int32, (L, L), 1)
    causal = ci <= ri
    eye = (ri == ci).astype(F32)
    i_rows = gt_ref[0, h] + b_i
    lf_rows = _log_sigmoid(gt_ref[0, h + MLSTM_HEADS] + b_f)
    b_rows = jnp.dot(lf_rows, (ri <= ci).astype(F32), preferred_element_type=F32,
                     precision=HIGHEST)
    b_end = b_rows[:, L - 1:L]
    g_rows = b_end - b_rows + i_rows
    g_max = jnp.max(g_rows, axis=1, keepdims=True)
    m = jnp.zeros((1, 1), F32)
    m_prev, m_new = [], []
    for c in range(NC):
        m_prev.append(m)
        m = jnp.maximum(b_end[c:c + 1, :] + m, g_max[c:c + 1, :])
        m_new.append(m)
    m_prev = jnp.concatenate(m_prev, axis=0)
    m_new = jnp.concatenate(m_new, axis=0)
    rows_s[0] = b_rows
    rows_s[1] = jnp.exp(g_rows - m_new)
    rows_s[2] = b_rows - i_rows
    rows_s[3] = jnp.broadcast_to(m_prev, (NC, L))
    rows_s[4] = jnp.broadcast_to(jnp.exp(b_end + m_prev - m_new), (NC, L))

    r2 = lax.broadcasted_iota(jnp.int32, (2 * L, 2 * L), 0)
    c2 = lax.broadcasted_iota(jnp.int32, (2 * L, 2 * L), 1)
    ones_blk = ((r2 < L) == (c2 < L)).astype(BF16)

    G = MLSTM_GROUP

    def local(cg, _):
        cs = [cg * G + i for i in range(G)]
        rows = [pl.ds(pl.multiple_of(c * L, L), L) for c in cs]
        b_r = [rows_s[0, pl.ds(c, 1), :] for c in cs]
        w_r = [rows_s[1, pl.ds(c, 1), :] for c in cs]
        u_r = [rows_s[2, pl.ds(c, 1), :] for c in cs]
        mp = [rows_s[3, pl.ds(c, 1), :] for c in cs]
        q = [q_s[r, :] for r in rows]
        k = [k_s[r, :] for r in rows]
        va = [va_s[r, :] for r in rows]
        qk = [_nt(a, b) for a, b in zip(q, k)]
        x2 = [jnp.concatenate([eye * a, eye * b], axis=1) for a, b in zip(b_r, w_r)]
        hi = [x.astype(BF16) for x in x2]
        lo = [(x - h_.astype(F32)).astype(BF16) for x, h_ in zip(x2, hi)]
        yb = [jnp.dot(h_, ones_blk, preferred_element_type=F32)
              + jnp.dot(l_, ones_blk, preferred_element_type=F32) for h_, l_ in zip(hi, lo)]
        b_b = [y[:, 0:L] for y in yb]
        w_b = [y[:, L:2 * L] for y in yb]
        for i in range(G):
            kv_s[cs[i]] = _tn((w_b[i] * k[i].astype(F32)).astype(BF16), va[i])
        dmat = [jnp.where(causal, b - u, NEG) for b, u in zip(b_b, u_r)]
        m_t = [jnp.maximum(b + m_, jnp.max(d, axis=1, keepdims=True))
               for b, m_, d in zip(b_b, mp, dmat)]
        sc = [a * jnp.exp(d - m_) for a, d, m_ in zip(qk, dmat, m_t)]
        for i in range(G):
            acc_s[rows[i], :] = jnp.dot(sc[i].astype(BF16), va[i], preferred_element_type=F32)
            inter_s[rows[i], :] = jnp.exp(b_b[i] + mp[i] - m_t[i])
            emt_s[rows[i], :] = jnp.exp(-m_t[i])
        return 0

    lax.fori_loop(0, NC // G, local, 0)

    g_row = gm_ref[...]
    c_s[...] = jnp.zeros((DK, DA), F32)

    def recur(cg, _):
        cs = [cg * G + i for i in range(G)]
        rows = [pl.ds(pl.multiple_of(c * L, L), L) for c in cs]
        states = [c_s[...]]
        for c in cs:
            dec = rows_s[4, pl.ds(c, 1), :]
            states.append(jnp.concatenate([dec, dec, dec], axis=1) * states[-1] + kv_s[c])
        c_s[...] = states[G]
        read = [jnp.dot(q_s[r, :], st.astype(BF16), preferred_element_type=F32)
                for r, st in zip(rows, states)]
        inter = [inter_s[r, :] for r in rows]
        out = [acc_s[r, :] + jnp.concatenate([it, it, it], axis=1) * rd
               for r, it, rd in zip(rows, inter, read)]
        emt = [emt_s[r, :] for r in rows]
        nrm = [jnp.maximum(jnp.abs(jnp.concatenate([o[:, DV:DA], o[:, DV:DA]], axis=1)),
                           jnp.concatenate([e_, e_], axis=1)) for o, e_ in zip(out, emt)]
        hh = [o[:, 0:DV] / n_ for o, n_ in zip(out, nrm)]
        ms = [jnp.mean(x * x, axis=1, keepdims=True) for x in hh]
        hn = [x * lax.rsqrt(m_ + NORM_EPS) * g_row for x, m_ in zip(hh, ms)]
        for i in range(G):
            o_ref[0, rows[i], :] = (hn[i] * _sigmoid(mo_ref[0, rows[i], :].astype(F32))).astype(BF16)
        return 0

    lax.fori_loop(0, NC // G, recur, 0)


def _mlstm(proj3, gates_t, conv_w, conv_b, bg_row, g_mlstm):
    B, S, _ = proj3.shape
    H, DK, DV = MLSTM_HEADS, MLSTM_QK_DIM, MLSTM_V_DIM
    L = MLSTM_BLOCK
    NC = S // L
    DA = DV + 128
    qb, kb = OFF_MQ // DK, OFF_MK // DK
    vb, ob = OFF_MV // DV, OFF_MO // DV
    nq = (H * DK) // DK
    return pl.pallas_call(
        functools.partial(_mlstm_kernel, seq=S),
        grid=(B, H),
        in_specs=[pl.BlockSpec((1, S, DK), lambda b, h: (b, 0, qb + h)),
                  pl.BlockSpec((1, S, DK), lambda b, h: (b, 0, kb + h)),
                  pl.BlockSpec((1, S, DV), lambda b, h: (b, 0, vb + h)),
                  pl.BlockSpec((1, S, DV), lambda b, h: (b, 0, ob + h)),
                  pl.BlockSpec((1, 2 * H, NC, L), lambda b, h: (b, 0, 0, 0)),
                  pl.BlockSpec((CONV_WIDTH, DK), lambda b, h: (0, h)),
                  pl.BlockSpec((CONV_WIDTH, DK), lambda b, h: (0, nq + h)),
                  pl.BlockSpec((1, DK), lambda b, h: (0, h)),
                  pl.BlockSpec((1, DK), lambda b, h: (0, nq + h)),
                  pl.BlockSpec((1, 128), lambda b, h: (0, 0)),
                  pl.BlockSpec((1, DV), lambda b, h: (0, h))],
        out_specs=pl.BlockSpec((1, S, DV), lambda b, h: (b, 0, h)),
        out_shape=jax.ShapeDtypeStruct((B, S, H * DV), BF16),
        scratch_shapes=[pltpu.VMEM((S, DK), BF16),
                        pltpu.VMEM((S, DK), BF16),
                        pltpu.VMEM((S, DA), BF16),
                        pltpu.VMEM((5, NC, L), F32),
                        pltpu.VMEM((S, DA), F32),
                        pltpu.VMEM((NC, DK, DA), F32),
                        pltpu.VMEM((S, L), F32),
                        pltpu.VMEM((S, L), F32),
                        pltpu.VMEM((DK, DA), F32)],
        compiler_params=pltpu.CompilerParams(
            dimension_semantics=("arbitrary", "arbitrary"), vmem_limit_bytes=VMEM_LIMIT),
        name="mlstm_chunkwise",
    )(proj3, proj3, proj3, proj3, gates_t, conv_w, conv_w, conv_b, conv_b, bg_row, g_mlstm)


def _rms(y, g):
    ms = jnp.mean(y * y, axis=-1, keepdims=True)
    return y * lax.rsqrt(ms + NORM_EPS) * g


def _merge_kernel(ya_ref, yb_ref, ga_ref, gb_ref, x_ref, mod_ref, wa_ref, wb_ref, wo_ref,
                  gpost_ref, gpre_ref, x1_ref, h2_ref):
    pa = jnp.dot(ya_ref[...], wa_ref[...], preferred_element_type=F32)
    pb = jnp.dot(yb_ref[...], wb_ref[...], preferred_element_type=F32)
    merged = (_sigmoid(ga_ref[...].astype(F32)) * pa
              + _sigmoid(gb_ref[...].astype(F32)) * pb)
    y = jnp.dot(merged.astype(BF16), wo_ref[...], preferred_element_type=F32)
    x1 = x_ref[...] + mod_ref[0, 2:3, :] * _rms(y, gpost_ref[...])
    x1_ref[...] = x1
    h2 = _rms(x1, gpre_ref[...]) * (1.0 + mod_ref[0, 4:5, :]) + mod_ref[0, 3:4, :]
    h2_ref[...] = _pack_pair(h2[:, :HALF], h2[:, HALF:])


def _merge(ya2, yb2, proj2, x2, mod3, wa, wb, wo, g_post, g_pre, seq):
    T = x2.shape[0]
    tm = 512
    per_b = seq // tm
    full = lambda shape: pl.BlockSpec(shape, lambda i: (0,) * len(shape))
    return pl.pallas_call(
        _merge_kernel,
        grid=(T // tm,),
        in_specs=[pl.BlockSpec((tm, ATT_GROUP_W), lambda i: (i, 0)),
                  pl.BlockSpec((tm, D_MODEL), lambda i: (i, 0)),
                  pl.BlockSpec((tm, D_MODEL), lambda i: (i, OFF_GA // D_MODEL)),
                  pl.BlockSpec((tm, D_MODEL), lambda i: (i, OFF_GB // D_MODEL)),
                  pl.BlockSpec((tm, D_MODEL), lambda i: (i, 0)),
                  pl.BlockSpec((1, 6, D_MODEL), lambda i: (i // per_b, 0, 0)),
                  full((ATT_GROUP_W, D_MODEL)), full((D_MODEL, D_MODEL)), full((D_MODEL, D_MODEL)),
                  full((1, D_MODEL)), full((1, D_MODEL))],
        out_specs=[pl.BlockSpec((tm, D_MODEL), lambda i: (i, 0)),
                   pl.BlockSpec((tm, HALF), lambda i: (i, 0))],
        out_shape=[jax.ShapeDtypeStruct((T, D_MODEL), F32),
                   jax.ShapeDtypeStruct((T, HALF), jnp.uint32)],
        compiler_params=pltpu.CompilerParams(
            dimension_semantics=("arbitrary",), vmem_limit_bytes=VMEM_LIMIT),
        name="merge_out_proj",
    )(ya2, yb2, proj2, proj2, x2, mod3, wa, wb, wo, g_post, g_pre)


def _router_kernel(h2_ref, rlo_ref, rhi_ref, bias_ref, idx_ref, w_ref, rank_ref, cnt_ref):
    E = N_EXPERTS
    tr = h2_ref.shape[0]
    gsz = E // N_GROUPS

    @pl.when(pl.program_id(0) == 0)
    def _():
        cnt_ref[...] = jnp.zeros(cnt_ref.shape, F32)

    lo, hi = _unpack_pair(h2_ref[...])
    logits = _nt(rlo_ref[...], lo.astype(BF16)) + _nt(rhi_ref[...], hi.astype(BF16))
    scores = _sigmoid(logits)
    sel = scores + bias_ref[:, 0:1]

    gi = lax.broadcasted_iota(jnp.int32, (gsz, tr), 0).astype(F32)
    gs_rows = []
    for g in range(N_GROUPS):
        blk = sel[g * gsz:(g + 1) * gsz, :]
        m1 = jnp.max(blk, axis=0, keepdims=True)
        a1 = jnp.min(jnp.where(blk == m1, gi, float(E)), axis=0, keepdims=True)
        m2 = jnp.max(jnp.where(gi == a1, -jnp.inf, blk), axis=0, keepdims=True)
        gs_rows.append(m1 + m2)
    gs = jnp.concatenate(gs_rows, axis=0)
    g8 = lax.broadcasted_iota(jnp.int32, (N_GROUPS, tr), 0).astype(F32)
    gmask = jnp.zeros((N_GROUPS, tr), F32)
    for _ in range(TOPK_GROUPS):
        m = jnp.max(gs, axis=0, keepdims=True)
        a = jnp.min(jnp.where(gs == m, g8, float(E)), axis=0, keepdims=True)
        hit = g8 == a
        gmask = jnp.where(hit, 1.0, gmask)
        gs = jnp.where(hit, -jnp.inf, gs)
    selm = jnp.concatenate(
        [jnp.where(gmask[g:g + 1, :] > 0.0, sel[g * gsz:(g + 1) * gsz, :], -jnp.inf)
         for g in range(N_GROUPS)], axis=0)

    ei = lax.broadcasted_iota(jnp.int32, (E, tr), 0).astype(F32)
    picks, weights, hits = [], [], []
    candidates = selm
    for _ in range(TOP_K):
        m = jnp.max(selm, axis=0, keepdims=True)
        a = jnp.min(jnp.where(selm == m, ei, float(E)), axis=0, keepdims=True)
        hit = ei == a
        picks.append(a)
        hits.append(hit)
        weights.append(jnp.sum(jnp.where(hit, scores, 0.0), axis=0, keepdims=True))
        selm = jnp.where(hit, -jnp.inf, selm)
    chosen = jnp.where(selm != candidates, 1.0, 0.0)
    wsum = weights[0]
    for w in weights[1:]:
        wsum = wsum + w

    ti = lax.broadcasted_iota(jnp.int32, (tr, tr), 0)
    tj = lax.broadcasted_iota(jnp.int32, (tr, tr), 1)
    before = (ti < tj).astype(BF16)
    pos = jnp.dot(chosen.astype(BF16), before, preferred_element_type=F32) + cnt_ref[:, 0:1]
    ranks = [jnp.sum(jnp.where(hit, pos, 0.0), axis=0, keepdims=True) for hit in hits]
    cnt_ref[...] = cnt_ref[...] + jnp.sum(chosen, axis=1, keepdims=True)

    idx_ref[...] = jnp.concatenate(picks, axis=0).astype(jnp.int32)
    w_ref[...] = jnp.concatenate([w / wsum * ROUTED_SCALE for w in weights], axis=0)
    rank_ref[...] = jnp.concatenate(ranks, axis=0).astype(jnp.int32)


def _router(h2p, r_lo, r_hi, bias_col, row0, T):
    tr = 512
    off = row0 // tr
    full = lambda shape: pl.BlockSpec(shape, lambda i: (0,) * len(shape))
    return pl.pallas_call(
        _router_kernel,
        grid=(T // tr,),
        in_specs=[pl.BlockSpec((tr, HALF), lambda i: (i + off, 0)),
                  full((N_EXPERTS, HALF)), full((N_EXPERTS, HALF)), full((N_EXPERTS, 128))],
        out_specs=[pl.BlockSpec((TOP_K, tr), lambda i: (0, i)),
                   pl.BlockSpec((TOP_K, tr), lambda i: (0, i)),
                   pl.BlockSpec((TOP_K, tr), lambda i: (0, i)),
                   full((N_EXPERTS, 128))],
        out_shape=[jax.ShapeDtypeStruct((TOP_K, T), jnp.int32),
                   jax.ShapeDtypeStruct((TOP_K, T), F32),
                   jax.ShapeDtypeStruct((TOP_K, T), jnp.int32),
                   jax.ShapeDtypeStruct((N_EXPERTS, 128), F32)],
        compiler_params=pltpu.CompilerParams(
            dimension_semantics=("arbitrary",), vmem_limit_bytes=VMEM_LIMIT),
        name="router_topk",
    )(h2p, r_lo, r_hi, bias_col)


def _dest_kernel(idx_ref, rank_ref, pstart_ref, dest_ref):
    tr = idx_ref.shape[1]
    ei = lax.broadcasted_iota(jnp.int32, (N_EXPERTS, tr), 0)
    start = pstart_ref[:, 0:1]
    rows = []
    for k in range(TOP_K):
        hit = ei == idx_ref[k:k + 1, :]
        rows.append(jnp.sum(jnp.where(hit, start, 0.0), axis=0, keepdims=True))
    dest_ref[...] = jnp.concatenate(rows, axis=0).astype(jnp.int32) + rank_ref[...]


def _slot_index(idx, rank, pstart_col):
    T = idx.shape[1]
    tr = 1024
    return pl.pallas_call(
        _dest_kernel,
        grid=(T // tr,),
        in_specs=[pl.BlockSpec((TOP_K, tr), lambda i: (0, i)),
                  pl.BlockSpec((TOP_K, tr), lambda i: (0, i)),
                  pl.BlockSpec((N_EXPERTS, 128), lambda i: (0, 0))],
        out_specs=pl.BlockSpec((TOP_K, tr), lambda i: (0, i)),
        out_shape=jax.ShapeDtypeStruct((TOP_K, T), jnp.int32),
        name="slot_index",
    )(idx, rank, pstart_col)


def _ffn_kernel(first_ref, nblk_ref, nused_ref, xs_hbm, wg_ref, wu_ref, wd_ref, ys_hbm,
                xbuf, ybuf, in_sem, out_sem, wg_s, wu_s, wd_s):
    e = pl.program_id(0)
    bm = EXPERT_BLOCK
    ns = EXPERT_SLOTS
    nused = nused_ref[0]
    first = first_ref[e]
    n = nblk_ref[e]

    def in_copy(g):
        slot = g % ns
        return pltpu.make_async_copy(xs_hbm.at[pl.ds(g * bm, bm)], xbuf.at[slot], in_sem.at[slot])

    def out_copy(g):
        slot = g % ns
        return pltpu.make_async_copy(ybuf.at[slot], ys_hbm.at[pl.ds(g * bm, bm)], out_sem.at[slot])

    for q in range(ns - 1):
        @pl.when((e == 0) & (nused > q))
        def _(q=q):
            in_copy(q).start()

    @pl.when(n > 0)
    def _():
        wg_s[...] = wg_ref[0].astype(BF16)
        wu_s[...] = wu_ref[0].astype(BF16)
        wd_s[...] = wd_ref[0].astype(BF16)

        def block(j, _):
            g = first + j
            slot = g % ns
            in_copy(g).wait()

            @pl.when(g + ns - 1 < nused)
            def _():
                in_copy(g + ns - 1).start()

            lo, hi = _unpack_pair(xbuf[slot])
            x = jnp.concatenate([lo.astype(BF16), hi.astype(BF16)], axis=1)
            gate = jnp.dot(x, wg_s[...], preferred_element_type=F32)
            up = jnp.dot(x, wu_s[...], preferred_element_type=F32)
            hid = (_silu(gate) * up).astype(BF16)
            out = jnp.dot(hid, wd_s[...], preferred_element_type=F32)

            @pl.when(g >= ns)
            def _():
                out_copy(g - ns).wait()

            ybuf[slot] = _pack_pair(out[:, :HALF], out[:, HALF:])
            out_copy(g).start()
            return 0

        lax.fori_loop(0, n, block, 0)

    @pl.when(e == pl.num_programs(0) - 1)
    def _():
        for q in range(ns, 0, -1):
            @pl.when(nused >= q)
            def _(q=q):
                out_copy(nused - q).wait()


def _expert_ffn(first_blk, nblk, nused, xs, w_gate, w_up, w_down):
    P = xs.shape[0]
    bm = EXPERT_BLOCK
    w_map = lambda e, *_: (e, 0, 0)
    grid_spec = pltpu.PrefetchScalarGridSpec(
        num_scalar_prefetch=3,
        grid=(w_gate.shape[0],),
        in_specs=[pl.BlockSpec(memory_space=pl.ANY),
                  pl.BlockSpec((1, D_MODEL, EXPERT_FF), w_map),
                  pl.BlockSpec((1, D_MODEL, EXPERT_FF), w_map),
                  pl.BlockSpec((1, EXPERT_FF, D_MODEL), w_map)],
        out_specs=pl.BlockSpec(memory_space=pl.ANY),
        scratch_shapes=[pltpu.VMEM((EXPERT_SLOTS, bm, HALF), jnp.uint32),
                        pltpu.VMEM((EXPERT_SLOTS, bm, HALF), jnp.uint32),
                        pltpu.SemaphoreType.DMA((EXPERT_SLOTS,)),
                        pltpu.SemaphoreType.DMA((EXPERT_SLOTS,)),
                        pltpu.VMEM((D_MODEL, EXPERT_FF), BF16),
                        pltpu.VMEM((D_MODEL, EXPERT_FF), BF16),
                        pltpu.VMEM((EXPERT_FF, D_MODEL), BF16)],
    )
    return pl.pallas_call(
        _ffn_kernel,
        grid_spec=grid_spec,
        out_shape=jax.ShapeDtypeStruct((P, HALF), jnp.uint32),
        compiler_params=pltpu.CompilerParams(
            dimension_semantics=("arbitrary",), vmem_limit_bytes=VMEM_LIMIT),
        name="routed_experts",
    )(first_blk, nblk, nused, xs, w_gate, w_up, w_down)


def _final_kernel(yg_ref, w_ref, h2_ref, x1_ref, mod_ref, wsg_ref, wsu_ref, wsd_ref, gpost_ref, *rest):
    o_ref = rest[-1]
    lo, hi = _unpack_pair(h2_ref[...])
    h2 = jnp.concatenate([lo.astype(BF16), hi.astype(BF16)], axis=1)
    gate = jnp.dot(h2, wsg_ref[...], preferred_element_type=F32)
    up = jnp.dot(h2, wsu_ref[...], preferred_element_type=F32)
    shared = jnp.dot((_silu(gate) * up).astype(BF16), wsd_ref[...], preferred_element_type=F32)
    y_lo = shared[:, :HALF]
    y_hi = shared[:, HALF:]
    for k in range(TOP_K):
        r_lo, r_hi = _unpack_pair(yg_ref[k])
        wk = w_ref[:, k:k + 1]
        y_lo = y_lo + wk * r_lo
        y_hi = y_hi + wk * r_hi
    ms = (jnp.sum(y_lo * y_lo, axis=-1, keepdims=True)
          + jnp.sum(y_hi * y_hi, axis=-1, keepdims=True)) * (1.0 / D_MODEL)
    inv = lax.rsqrt(ms + NORM_EPS)
    o_ref[:, 0:HALF] = x1_ref[:, 0:HALF] + mod_ref[0, 5:6, 0:HALF] * (y_lo * inv * gpost_ref[:, 0:HALF])
    o_ref[:, HALF:] = x1_ref[:, HALF:] + mod_ref[0, 5:6, HALF:] * (y_hi * inv * gpost_ref[:, HALF:])


def _final(yg, w_tk, h2p, x1, mod3, wsg, wsu, wsd, g_post, seq, row0, out_prev):
    T = x1.shape[0]
    tp = yg.shape[1]
    tm = 512
    per_b = seq // tm
    off = row0 // tm
    full = lambda shape: pl.BlockSpec(shape, lambda i: (0,) * len(shape))
    in_specs = [pl.BlockSpec((TOP_K, tm, HALF), lambda i: (0, i, 0)),
                pl.BlockSpec((tm, TOP_K), lambda i: (i, 0)),
                pl.BlockSpec((tm, HALF), lambda i: (i + off, 0)),
                pl.BlockSpec((tm, D_MODEL), lambda i: (i + off, 0)),
                pl.BlockSpec((1, 6, D_MODEL), lambda i: ((i + off) // per_b, 0, 0)),
                full((D_MODEL, EXPERT_FF)), full((D_MODEL, EXPERT_FF)), full((EXPERT_FF, D_MODEL)),
                full((1, D_MODEL))]
    args = [yg, w_tk, h2p, x1, mod3, wsg, wsu, wsd, g_post]
    aliases = {}
    if out_prev is not None:
        in_specs.append(pl.BlockSpec(memory_space=pl.ANY))
        args.append(out_prev)
        aliases = {len(args) - 1: 0}
    return pl.pallas_call(
        _final_kernel,
        grid=(tp // tm,),
        in_specs=in_specs,
        out_specs=pl.BlockSpec((tm, D_MODEL), lambda i: (i + off, 0)),
        out_shape=jax.ShapeDtypeStruct((T, D_MODEL), F32),
        input_output_aliases=aliases,
        compiler_params=pltpu.CompilerParams(
            dimension_semantics=("arbitrary",), vmem_limit_bytes=VMEM_LIMIT),
        name="shared_expert_combine",
    )(*args)


def _rope_tables(positions):
    inv = jnp.power(ROPE_THETA, -jnp.arange(ROPE_HALF, dtype=F32) / ROPE_HALF)
    ang = positions.astype(F32)[..., None] * inv
    cos, sin = jnp.cos(ang), jnp.sin(ang)
    rest = ATT_HEAD_DIM - 2 * ROPE_HALF
    cs = jnp.concatenate([cos, cos, jnp.ones(ang.shape[:-1] + (rest,), F32)], axis=-1)
    sn = jnp.concatenate([-sin, sin, jnp.zeros(ang.shape[:-1] + (rest,), F32)], axis=-1)
    return jnp.tile(cs, (1, 1, 2)), jnp.tile(sn, (1, 1, 2))


def _layer(x, c, positions, w_ada, b_ada, g_pre_mix, g_post_mix, g_pre_ffn, g_post_ffn,
           w_in, conv_w, conv_b, b_gates, g_mlstm, w_branch_a, w_branch_b, w_out,
           router_w, router_bias, w_exp_gate, w_exp_up, w_exp_down, w_sh_gate, w_sh_up, w_sh_down):
    B, S, D = x.shape
    T = B * S
    H = MLSTM_HEADS
    x2 = x.reshape(T, D)

    mod3 = _adaln(c, w_ada, b_ada).reshape(B, 6, D)

    a_w = 3 * ATT_GROUP_W
    o_mq = 3 * a_w
    o_mk = o_mq + H * MLSTM_QK_DIM
    o_mv = o_mk + H * MLSTM_QK_DIM
    o_mo = o_mv + H * MLSTM_V_DIM
    o_mi = o_mo + H * MLSTM_V_DIM
    o_ga = o_mi + 2 * H
    o_gb = o_ga + D
    seg = lambda o, w: w_in[:, o:o + w]
    w_main = jnp.concatenate(
        [seg(o_mv, H * MLSTM_V_DIM), seg(o_mo, H * MLSTM_V_DIM), seg(o_ga, D), seg(o_gb, D),
         seg(o_mq, H * MLSTM_QK_DIM), seg(o_mk, H * MLSTM_QK_DIM),
         seg(0, a_w), seg(a_w, a_w), seg(2 * a_w, a_w)], axis=1).astype(BF16)
    w_if = jnp.pad(seg(o_mi, 2 * H), ((0, 0), (0, 128 - 2 * H))).astype(BF16)

    proj, gates = _in_proj(x2, mod3, g_pre_mix.reshape(1, D), w_main, w_if, S)
    proj3 = proj.reshape(B, S, PROJ_W)

    cs, sn = _rope_tables(positions)
    y_a = _attention(proj3, cs, sn)

    bg_row = jnp.pad(b_gates.reshape(1, 2 * H), ((0, 0), (0, 128 - 2 * H)))
    gates_t = gates[:, :2 * H].reshape(B, S, 2 * H).transpose(0, 2, 1)
    gates_t = gates_t.reshape(B, 2 * H, S // MLSTM_BLOCK, MLSTM_BLOCK)
    y_b = _mlstm(proj3, gates_t, conv_w, conv_b.reshape(1, -1), bg_row, g_mlstm.reshape(1, -1))

    x1, h2p = _merge(y_a.reshape(T, ATT_GROUP_W), y_b.reshape(T, D), proj, x2, mod3,
                     w_branch_a.astype(BF16), w_branch_b.astype(BF16), w_out.astype(BF16),
                     g_post_mix.reshape(1, D), g_pre_ffn.reshape(1, D), S)

    rw_t = router_w.T.astype(BF16)
    bias_col = jnp.broadcast_to(router_bias.reshape(N_EXPERTS, 1), (N_EXPERTS, 128))
    wsg, wsu, wsd = w_sh_gate.astype(BF16), w_sh_up.astype(BF16), w_sh_down.astype(BF16)

    tp = T // MOE_PARTS
    bm = EXPERT_BLOCK
    nb = (tp * TOP_K) // bm + N_EXPERTS
    out = None
    for part in range(MOE_PARTS):
        row0 = part * tp
        idx, wts, rank, cnt = _router(h2p, rw_t[:, :HALF], rw_t[:, HALF:], bias_col, row0, tp)

        counts = cnt[:, 0].astype(jnp.int32)
        padded = (counts + bm - 1) // bm * bm
        pend = jnp.cumsum(padded)
        pstart = pend - padded
        pstart_col = jnp.broadcast_to(pstart.astype(F32).reshape(N_EXPERTS, 1), (N_EXPERTS, 128))
        dest = _slot_index(idx, rank, pstart_col)
        nused = (pend[-1] // bm).astype(jnp.int32).reshape(1)

        xs = _dispatch(h2p, dest, nb * bm, row0)
        ys = _expert_ffn((pstart // bm).astype(jnp.int32), (padded // bm).astype(jnp.int32), nused,
                         xs, w_exp_gate, w_exp_up, w_exp_down)
        yg = _collect(ys, dest)
        out = _final(yg, wts.T, h2p, x1, mod3, wsg, wsu, wsd, g_post_ffn.reshape(1, D), S, row0, out)
    return out.reshape(B, S, D)


SC_CORES = 2
SC_SUBCORES = 16
SC_WORKERS = SC_CORES * SC_SUBCORES
SC_ROWS = 64


def _sc_mesh():
    return plsc.VectorSubcoreMesh(core_axis_name="c", subcore_axis_name="s",
                                  num_cores=SC_CORES, num_subcores=SC_SUBCORES)


def _worker_id():
    return lax.axis_index("s") * SC_CORES + lax.axis_index("c")


def _dispatch(h2p, dest, n_slots, row0):
    T = dest.shape[1]
    per_w = T // SC_WORKERS
    nch = per_w // SC_ROWS
    idx = dest.reshape(TOP_K, SC_WORKERS, nch, SC_ROWS).transpose(1, 2, 0, 3)
    idx = idx.reshape(SC_WORKERS, nch * TOP_K, SC_ROWS)

    def body(x_hbm, idx_hbm, xs_hbm, idx_v, buf0, buf1, rsem0, rsem1, ssem0, ssem1):
        wid = _worker_id()
        base = row0 + wid * per_w
        pltpu.sync_copy(idx_hbm.at[wid], idx_v)
        bufs = ((buf0, rsem0, ssem0), (buf1, rsem1, ssem1))

        def read(c, buf, rsem):
            return pltpu.make_async_copy(x_hbm.at[pl.ds(base + c * SC_ROWS, SC_ROWS)], buf, rsem)

        def scatter(c, k, buf, ssem):
            return pltpu.make_async_copy(buf, xs_hbm.at[idx_v.at[c * TOP_K + k]], ssem)

        read(0, buf0, rsem0).start()

        @pl.loop(0, nch, step=2)
        def _(c0):
            for b in range(2):
                c = c0 + b
                buf, rsem, ssem = bufs[b]
                obuf, orsem, ossem = bufs[1 - b]
                read(c, buf, rsem).wait()

                @pl.when(c > 0)
                def _():
                    for k in range(TOP_K):
                        scatter(c - 1, k, obuf, ossem).wait()

                @pl.when(c + 1 < nch)
                def _():
                    read(c + 1, obuf, orsem).start()

                for k in range(TOP_K):
                    scatter(c, k, buf, ssem).start()

        for k in range(TOP_K):
            scatter(nch - 1, k, buf1, ssem1).wait()

    run = pl.kernel(
        body,
        out_type=jax.ShapeDtypeStruct((n_slots, HALF), jnp.uint32),
        mesh=_sc_mesh(),
        scratch_types=[pltpu.VMEM((nch * TOP_K, SC_ROWS), jnp.int32),
                       pltpu.VMEM((SC_ROWS, HALF), jnp.uint32),
                       pltpu.VMEM((SC_ROWS, HALF), jnp.uint32),
                       pltpu.SemaphoreType.DMA, pltpu.SemaphoreType.DMA,
                       pltpu.SemaphoreType.DMA, pltpu.SemaphoreType.DMA],
        name="sc_dispatch",
    )
    return run(h2p, idx)


def _collect(ys, dest):
    n = dest.size
    per_w = n // SC_WORKERS
    nch = per_w // SC_ROWS
    idx = dest.reshape(SC_WORKERS, nch, SC_ROWS)

    def body(ys_hbm, idx_hbm, out_hbm, idx_v, buf0, buf1, gsem0, gsem1, wsem0, wsem1):
        wid = _worker_id()
        base = wid * per_w
        pltpu.sync_copy(idx_hbm.at[wid], idx_v)
        bufs = ((buf0, gsem0, wsem0), (buf1, gsem1, wsem1))

        def gather(c, buf, gsem):
            return pltpu.make_async_copy(ys_hbm.at[idx_v.at[c]], buf, gsem)

        def write(c, buf, wsem):
            return pltpu.make_async_copy(buf, out_hbm.at[pl.ds(base + c * SC_ROWS, SC_ROWS)], wsem)

        gather(0, buf0, gsem0).start()

        @pl.loop(0, nch, step=2)
        def _(c0):
            for b in range(2):
                c = c0 + b
                buf, gsem, wsem = bufs[b]
                obuf, ogsem, owsem = bufs[1 - b]
                gather(c, buf, gsem).wait()

                @pl.when(c > 0)
                def _():
                    write(c - 1, obuf, owsem).wait()

                @pl.when(c + 1 < nch)
                def _():
                    gather(c + 1, obuf, ogsem).start()

                write(c, buf, wsem).start()

        write(nch - 1, buf1, wsem1).wait()

    run = pl.kernel(
        body,
        out_type=jax.ShapeDtypeStruct((n, HALF), jnp.uint32),
        mesh=_sc_mesh(),
        scratch_types=[pltpu.VMEM((nch, SC_ROWS), jnp.int32),
                       pltpu.VMEM((SC_ROWS, HALF), jnp.uint32),
                       pltpu.VMEM((SC_ROWS, HALF), jnp.uint32),
                       pltpu.SemaphoreType.DMA, pltpu.SemaphoreType.DMA,
                       pltpu.SemaphoreType.DMA, pltpu.SemaphoreType.DMA],
        name="sc_collect",
    )
    return run(ys, idx).reshape(dest.shape + (HALF,))


def kernel(x, c, positions, w_ada, b_ada, g_pre_mix, g_post_mix, g_pre_ffn, g_post_ffn, w_in, conv_w, conv_b, b_gates, g_mlstm, w_branch_a, w_branch_b, w_out, router_w, router_bias, w_exp_gate, w_exp_up, w_exp_down, w_sh_gate, w_sh_up, w_sh_down):
    depth = w_ada.shape[0]
    for l in range(depth):
        x = _layer(x, c, positions, w_ada[l], b_ada[l], g_pre_mix[l], g_post_mix[l], g_pre_ffn[l],
                   g_post_ffn[l], w_in[l], conv_w[l], conv_b[l], b_gates[l], g_mlstm[l],
                   w_branch_a[l], w_branch_b[l], w_out[l], router_w[l], router_bias[l],
                   w_exp_gate[l], w_exp_up[l], w_exp_down[l], w_sh_gate[l], w_sh_up[l], w_sh_down[l])
    return x
```

```python
import functools

import jax
import jax.numpy as jnp
from jax import lax
from jax.experimental import pallas as pl
from jax.experimental.pallas import tpu as pltpu
from jax.experimental.pallas import tpu_sc as plsc

F32 = jnp.float32
BF16 = jnp.bfloat16
HIGHEST = lax.Precision.HIGHEST

D_MODEL = 1024
ATT_GROUPS = ((128, 1), (512, 4), (2048, 16))
ATT_HEAD_DIM = 64
ATT_GROUP_W = 256
ATT_BLK = 128
ROPE_THETA = 500000.0
ROPE_HALF = 8
MLSTM_HEADS = 4
MLSTM_QK_DIM = 128
MLSTM_V_DIM = 256
MLSTM_BLOCK = 128
MLSTM_GROUP = 4
CONV_WIDTH = 4
N_EXPERTS = 256
TOP_K = 8
N_GROUPS = 8
TOPK_GROUPS = 4
EXPERT_FF = 256
ROUTED_SCALE = 2.5
NORM_EPS = 1e-6
NEG = -1e30

OFF_MV, OFF_MO, OFF_GA, OFF_GB = 0, 1024, 2048, 3072
OFF_MQ, OFF_MK = 4096, 4608
OFF_AQ, OFF_AK, OFF_AV = 5120, 5888, 6656
PROJ_W = 7424
HALF = D_MODEL // 2

EXPERT_BLOCK = 512
EXPERT_SLOTS = 6
MOE_PARTS = 2
VMEM_LIMIT = 56 * 1024 * 1024


def _nt(a, b, precision=None):
    return lax.dot_general(a, b, (((1,), (1,)), ((), ())), preferred_element_type=F32,
                           precision=precision)


def _tn(a, b):
    return lax.dot_general(a, b, (((0,), (0,)), ((), ())), preferred_element_type=F32)


_sigmoid = jax.nn.sigmoid


def _silu(x):
    return x * _sigmoid(x)


def _pack_pair(lo, hi):
    lo_b = pltpu.bitcast(lo.astype(BF16).astype(F32), jnp.uint32)
    hi_b = pltpu.bitcast(hi.astype(BF16).astype(F32), jnp.uint32)
    return (lo_b >> 16) | (hi_b & jnp.uint32(0xFFFF0000))


def _unpack_pair(w):
    lo = pltpu.bitcast(w << 16, F32)
    hi = pltpu.bitcast(w & jnp.uint32(0xFFFF0000), F32)
    return lo, hi


def _mod_kernel(c_ref, w_ref, b_ref, o_ref):
    a = _silu(c_ref[...])
    o_ref[...] = jnp.dot(a, w_ref[...], preferred_element_type=F32, precision=HIGHEST) + b_ref[...]


def _adaln(c, w_ada, b_ada):
    B = c.shape[0]
    n = w_ada.shape[1]
    tn = 512
    return pl.pallas_call(
        _mod_kernel,
        grid=(n // tn,),
        in_specs=[pl.BlockSpec((B, D_MODEL), lambda j: (0, 0)),
                  pl.BlockSpec((D_MODEL, tn), lambda j: (0, j)),
                  pl.BlockSpec((1, tn), lambda j: (0, j))],
        out_specs=pl.BlockSpec((B, tn), lambda j: (0, j)),
        out_shape=jax.ShapeDtypeStruct((B, n), F32),
        name="adaln_mod",
    )(c, w_ada, b_ada.reshape(1, n))


def _proj_kernel(x_ref, mod_ref, g_ref, w_ref, wif_ref, o_ref, gates_ref, h_ref):
    @pl.when(pl.program_id(1) == 0)
    def _():
        x = x_ref[...]
        ms = jnp.mean(x * x, axis=-1, keepdims=True)
        y = x * lax.rsqrt(ms + NORM_EPS) * g_ref[...]
        h = (y * (1.0 + mod_ref[0, 1:2, :]) + mod_ref[0, 0:1, :]).astype(BF16)
        h_ref[...] = h
        gates_ref[...] = jnp.dot(h, wif_ref[...], preferred_element_type=F32)

    o_ref[...] = jnp.dot(h_ref[...], w_ref[...], preferred_element_type=F32).astype(BF16)


def _in_proj(x2, mod3, g_pre, w_main, w_if, seq):
    T = x2.shape[0]
    tm, tn = 1024, PROJ_W // 2
    per_b = seq // tm
    return pl.pallas_call(
        _proj_kernel,
        grid=(T // tm, PROJ_W // tn),
        in_specs=[pl.BlockSpec((tm, D_MODEL), lambda i, j: (i, 0)),
                  pl.BlockSpec((1, 6, D_MODEL), lambda i, j: (i // per_b, 0, 0)),
                  pl.BlockSpec((1, D_MODEL), lambda i, j: (0, 0)),
                  pl.BlockSpec((D_MODEL, tn), lambda i, j: (0, j)),
                  pl.BlockSpec((D_MODEL, 128), lambda i, j: (0, 0))],
        out_specs=[pl.BlockSpec((tm, tn), lambda i, j: (i, j)),
                   pl.BlockSpec((tm, 128), lambda i, j: (i, 0))],
        out_shape=[jax.ShapeDtypeStruct((T, PROJ_W), BF16),
                   jax.ShapeDtypeStruct((T, 128), F32)],
        scratch_shapes=[pltpu.VMEM((tm, D_MODEL), BF16)],
        compiler_params=pltpu.CompilerParams(
            dimension_semantics=("arbitrary", "arbitrary"), vmem_limit_bytes=VMEM_LIMIT),
        name="norm_in_proj",
    )(x2, mod3, g_pre, w_main, w_if)


def _attn_kernel(q_ref, k_ref, v_ref, cs_ref, sn_ref, o_ref, qf, kf, vf, acc, m_s, l_s, *, seq):
    g = pl.program_id(1)
    lane = lax.broadcasted_iota(jnp.int32, (ATT_BLK, 128), 1)
    first = (lane % ATT_HEAD_DIM) < ROPE_HALF
    low_head = lane < ATT_HEAD_DIM

    def rope(x, cs, sn):
        partner = jnp.where(first, pltpu.roll(x, 128 - ROPE_HALF, 1), pltpu.roll(x, ROPE_HALF, 1))
        return x * cs + partner * sn

    def zero_pad(i, _):
        rows = pl.ds(pl.multiple_of(i * ATT_BLK, ATT_BLK), ATT_BLK)
        for hp in range(2):
            kf[hp, rows, :] = jnp.zeros((ATT_BLK, 128), F32)
            vf[hp, rows, :] = jnp.zeros((ATT_BLK, 128), F32)
        return 0

    lax.fori_loop(0, seq // ATT_BLK, zero_pad, 0)

    def stage(i, _):
        r = pl.multiple_of(i * ATT_BLK, ATT_BLK)
        rows = pl.ds(r, ATT_BLK)
        prow = pl.ds(pl.multiple_of(seq + i * ATT_BLK, ATT_BLK), ATT_BLK)
        cs = cs_ref[0, rows, :]
        sn = sn_ref[0, rows, :]
        for hp in range(2):
            cols = pl.ds(hp * 128, 128)
            qf[hp, rows, :] = rope(q_ref[0, rows, cols].astype(F32), cs, sn) * (ATT_HEAD_DIM ** -0.5)
            kf[hp, prow, :] = rope(k_ref[0, rows, cols].astype(F32), cs, sn)
            vf[hp, prow, :] = v_ref[0, rows, cols].astype(F32)
        return 0

    lax.fori_loop(0, seq // ATT_BLK, stage, 0)

    qi = lax.broadcasted_iota(jnp.int32, (ATT_BLK, 2 * ATT_BLK), 0)
    ki = lax.broadcasted_iota(jnp.int32, (ATT_BLK, 2 * ATT_BLK), 1)
    band = (ki >= qi) & (ki <= qi + ATT_BLK)

    def process(d, init):
        span = ATT_BLK * d

        def body(c, _):
            rho = c % d
            n = c // d
            qstart = rho + n * span
            kstart = seq + qstart - span
            first_key = jnp.where(n > 0, 0, ATT_BLK)
            valid = band & (ki >= first_key)
            qrows = pl.ds(qstart, ATT_BLK, stride=d) if d > 1 else pl.ds(qstart, ATT_BLK)
            krows = pl.ds(kstart, 2 * ATT_BLK, stride=d) if d > 1 else pl.ds(kstart, 2 * ATT_BLK)
            heads = [(hp, hh) for hp in range(2) for hh in range(2)]
            q2 = [qf[hp, qrows, :] for hp in range(2)]
            k2 = [kf[hp, krows, :].astype(BF16) for hp in range(2)]
            v2 = [vf[hp, krows, :].astype(BF16) for hp in range(2)]
            qh = [jnp.where(low_head if hh == 0 else jnp.logical_not(low_head), q2[hp], 0.0).astype(BF16)
                  for hp, hh in heads]
            s = [jnp.where(valid, _nt(qh[i], k2[hp]), NEG) for i, (hp, hh) in enumerate(heads)]
            m = [jnp.max(x, axis=1, keepdims=True) for x in s]
            p = [jnp.exp(x - mx) for x, mx in zip(s, m)]
            l = [jnp.sum(x, axis=1, keepdims=True) for x in p]
            o = [jnp.dot(p[i].astype(BF16), v2[hp], preferred_element_type=F32)
                 for i, (hp, hh) in enumerate(heads)]
            for hp in range(2):
                o_b = jnp.where(low_head, o[2 * hp], o[2 * hp + 1])
                m_b = jnp.where(low_head, m[2 * hp], m[2 * hp + 1])
                l_b = jnp.where(low_head, l[2 * hp], l[2 * hp + 1])
                if init:
                    acc[hp, qrows, :] = o_b
                    m_s[hp, qrows, :] = m_b
                    l_s[hp, qrows, :] = l_b
                else:
                    m_old = m_s[hp, qrows, :]
                    m_new = jnp.maximum(m_old, m_b)
                    a_old = jnp.exp(m_old - m_new)
                    a_new = jnp.exp(m_b - m_new)
                    acc[hp, qrows, :] = acc[hp, qrows, :] * a_old + o_b * a_new
                    l_s[hp, qrows, :] = l_s[hp, qrows, :] * a_old + l_b * a_new
                    m_s[hp, qrows, :] = m_new
            return 0

        lax.fori_loop(0, seq // ATT_BLK, body, 0, unroll=2)

    for gi, (_, d) in enumerate(ATT_GROUPS):
        @pl.when(g == gi)
        def _(d=d, gi=gi):
            process(d, gi == 0)

    @pl.when(g == len(ATT_GROUPS) - 1)
    def _():
        def fin(i, _):
            rows = pl.ds(pl.multiple_of(i * ATT_BLK, ATT_BLK), ATT_BLK)
            for hp in range(2):
                o_ref[0, rows, pl.ds(hp * 128, 128)] = (acc[hp, rows, :] / l_s[hp, rows, :]).astype(BF16)
            return 0

        lax.fori_loop(0, seq // ATT_BLK, fin, 0)


def _attention(proj3, cs, sn):
    B, S, _ = proj3.shape
    ng = len(ATT_GROUPS)
    qb, kb, vb = OFF_AQ // ATT_GROUP_W, OFF_AK // ATT_GROUP_W, OFF_AV // ATT_GROUP_W
    return pl.pallas_call(
        functools.partial(_attn_kernel, seq=S),
        grid=(B, ng),
        in_specs=[pl.BlockSpec((1, S, ATT_GROUP_W), lambda b, g: (b, 0, qb + g)),
                  pl.BlockSpec((1, S, ATT_GROUP_W), lambda b, g: (b, 0, kb + g)),
                  pl.BlockSpec((1, S, ATT_GROUP_W), lambda b, g: (b, 0, vb + g)),
                  pl.BlockSpec((1, S, 128), lambda b, g: (b, 0, 0)),
                  pl.BlockSpec((1, S, 128), lambda b, g: (b, 0, 0))],
        out_specs=pl.BlockSpec((1, S, ATT_GROUP_W), lambda b, g: (b, 0, 0)),
        out_shape=jax.ShapeDtypeStruct((B, S, ATT_GROUP_W), BF16),
        scratch_shapes=[pltpu.VMEM((2, S, 128), F32),
                        pltpu.VMEM((2, 2 * S, 128), F32),
                        pltpu.VMEM((2, 2 * S, 128), F32),
                        pltpu.VMEM((2, S, 128), F32),
                        pltpu.VMEM((2, S, 128), F32),
                        pltpu.VMEM((2, S, 128), F32)],
        compiler_params=pltpu.CompilerParams(
            dimension_semantics=("arbitrary", "arbitrary"), vmem_limit_bytes=VMEM_LIMIT),
        name="dilated_attention",
    )(proj3, proj3, proj3, cs, sn)


def _log_sigmoid(x):
    return jnp.minimum(x, 0.0) - jnp.log(1.0 + jnp.exp(-jnp.abs(x)))


def _mlstm_kernel(mq_ref, mk_ref, mv_ref, mo_ref, gt_ref, cwq_ref, cwk_ref, cbq_ref, cbk_ref,
                  bg_ref, gm_ref, o_ref, q_s, k_s, va_s, rows_s, acc_s, kv_s, inter_s, emt_s,
                  c_s, *, seq):
    h = pl.program_id(1)
    L = MLSTM_BLOCK
    NC = seq // L
    DK, DV = MLSTM_QK_DIM, MLSTM_V_DIM
    DA = DV + 128
    nshift = CONV_WIDTH - 1

    tt = lax.broadcasted_iota(jnp.int32, (nshift * L, 2 * L), 0)
    uu = lax.broadcasted_iota(jnp.int32, (nshift * L, 2 * L), 1)
    shift_mat = (uu == L + tt % L - (tt // L + 1)).astype(BF16)
    conv_w = jnp.concatenate([cwq_ref[...], cwk_ref[...]], axis=1)
    conv_b = jnp.concatenate([cbq_ref[...], cbk_ref[...]], axis=1)
    prev = jnp.zeros((L, 2 * DK), BF16)
    for i in range(NC):
        blk = slice(i * L, (i + 1) * L)
        va_s[blk, 0:DV] = mv_ref[0, blk, :]
        va_s[blk, DV:DA] = jnp.ones((L, DA - DV), BF16)
        cur = jnp.concatenate([mq_ref[0, blk, :], mk_ref[0, blk, :]], axis=1)
        shifted = jnp.dot(shift_mat, jnp.concatenate([prev, cur], axis=0),
                          preferred_element_type=F32)
        y = conv_b + cur.astype(F32) * conv_w[nshift:nshift + 1, :]
        for s in range(nshift):
            y = y + shifted[s * L:(s + 1) * L, :] * conv_w[nshift - 1 - s:nshift - s, :]
        y = _silu(y)
        q_s[blk, :] = y[:, 0:DK].astype(BF16)
        k_s[blk, :] = (y[:, DK:2 * DK] * (DK ** -0.5)).astype(BF16)
        prev = cur

    lane = lax.broadcasted_iota(jnp.int32, (1, 128), 1)
    bias = bg_ref[...]
    b_i = jnp.sum(jnp.where(lane == h, bias, 0.0), axis=1, keepdims=True)
    b_f = jnp.sum(jnp.where(lane == h + MLSTM_HEADS, bias, 0.0), axis=1, keepdims=True)
    ri = lax.broadcasted_iota(jnp.int32, (L, L), 0)
    ci = lax.broadcasted_iota(jnp.int32, (L, L), 1)
    causal = ci <= ri
    eye = (ri == ci).astype(F32)
    i_rows = gt_ref[0, h] + b_i
    lf_rows = _log_sigmoid(gt_ref[0, h + MLSTM_HEADS] + b_f)
    b_rows = jnp.dot(lf_rows, (ri <= ci).astype(F32), preferred_element_type=F32,
                     precision=HIGHEST)
    b_end = b_rows[:, L - 1:L]
    g_rows = b_end - b_rows + i_rows
    g_max = jnp.max(g_rows, axis=1, keepdims=True)
    m = jnp.zeros((1, 1), F32)
    m_prev, m_new = [], []
    for c in range(NC):
        m_prev.append(m)
        m = jnp.maximum(b_end[c:c + 1, :] + m, g_max[c:c + 1, :])
        m_new.append(m)
    m_prev = jnp.concatenate(m_prev, axis=0)
    m_new = jnp.concatenate(m_new, axis=0)
    rows_s[0] = b_rows
    rows_s[1] = jnp.exp(g_rows - m_new)
    rows_s[2] = b_rows - i_rows
    rows_s[3] = jnp.broadcast_to(m_prev, (NC, L))
    rows_s[4] = jnp.broadcast_to(jnp.exp(b_end + m_prev - m_new), (NC, L))

    r2 = lax.broadcasted_iota(jnp.int32, (2 * L, 2 * L), 0)
    c2 = lax.broadcasted_iota(jnp.int32, (2 * L, 2 * L), 1)
    ones_blk = ((r2 < L) == (c2 < L)).astype(BF16)

    G = MLSTM_GROUP

    def local(cg, _):
        cs = [cg * G + i for i in range(G)]
        rows = [pl.ds(pl.multiple_of(c * L, L), L) for c in cs]
        b_r = [rows_s[0, pl.ds(c, 1), :] for c in cs]
        w_r = [rows_s[1, pl.ds(c, 1), :] for c in cs]
        u_r = [rows_s[2, pl.ds(c, 1), :] for c in cs]
        mp = [rows_s[3, pl.ds(c, 1), :] for c in cs]
        q = [q_s[r, :] for r in rows]
        k = [k_s[r, :] for r in rows]
        va = [va_s[r, :] for r in rows]
        qk = [_nt(a, b) for a, b in zip(q, k)]
        x2 = [jnp.concatenate([eye * a, eye * b], axis=1) for a, b in zip(b_r, w_r)]
        hi = [x.astype(BF16) for x in x2]
        lo = [(x - h_.astype(F32)).astype(BF16) for x, h_ in zip(x2, hi)]
        yb = [jnp.dot(h_, ones_blk, preferred_element_type=F32)
              + jnp.dot(l_, ones_blk, preferred_element_type=F32) for h_, l_ in zip(hi, lo)]
        b_b = [y[:, 0:L] for y in yb]
        w_b = [y[:, L:2 * L] for y in yb]
        for i in range(G):
            kv_s[cs[i]] = _tn((w_b[i] * k[i].astype(F32)).astype(BF16), va[i])
        dmat = [jnp.where(causal, b - u, NEG) for b, u in zip(b_b, u_r)]
        m_t = [jnp.maximum(b + m_, jnp.max(d, axis=1, keepdims=True))
               for b, m_, d in zip(b_b, mp, dmat)]
        sc = [a * jnp.exp(d - m_) for a, d, m_ in zip(qk, dmat, m_t)]
        for i in range(G):
            acc_s[rows[i], :] = jnp.dot(sc[i].astype(BF16), va[i], preferred_element_type=F32)
            inter_s[rows[i], :] = jnp.exp(b_b[i] + mp[i] - m_t[i])
            emt_s[rows[i], :] = jnp.exp(-m_t[i])
        return 0

    lax.fori_loop(0, NC // G, local, 0)

    g_row = gm_ref[...]
    c_s[...] = jnp.zeros((DK, DA), F32)

    def recur(cg, _):
        cs = [cg * G + i for i in range(G)]
        rows = [pl.ds(pl.multiple_of(c * L, L), L) for c in cs]
        states = [c_s[...]]
        for c in cs:
            dec = rows_s[4, pl.ds(c, 1), :]
            states.append(jnp.concatenate([dec, dec, dec], axis=1) * states[-1] + kv_s[c])
        c_s[...] = states[G]
        read = [jnp.dot(q_s[r, :], st.astype(BF16), preferred_element_type=F32)
                for r, st in zip(rows, states)]
        inter = [inter_s[r, :] for r in rows]
        out = [acc_s[r, :] + jnp.concatenate([it, it, it], axis=1) * rd
               for r, it, rd in zip(rows, inter, read)]
        emt = [emt_s[r, :] for r in rows]
        nrm = [jnp.maximum(jnp.abs(jnp.concatenate([o[:, DV:DA], o[:, DV:DA]], axis=1)),
                           jnp.concatenate([e_, e_], axis=1)) for o, e_ in zip(out, emt)]
        hh = [o[:, 0:DV] / n_ for o, n_ in zip(out, nrm)]
        ms = [jnp.mean(x * x, axis=1, keepdims=True) for x in hh]
        hn = [x * lax.rsqrt(m_ + NORM_EPS) * g_row for x, m_ in zip(hh, ms)]
        for i in range(G):
            o_ref[0, rows[i], :] = (hn[i] * _sigmoid(mo_ref[0, rows[i], :].astype(F32))).astype(BF16)
        return 0

    lax.fori_loop(0, NC // G, recur, 0)


def _mlstm(proj3, gates_t, conv_w, conv_b, bg_row, g_mlstm):
    B, S, _ = proj3.shape
    H, DK, DV = MLSTM_HEADS, MLSTM_QK_DIM, MLSTM_V_DIM
    L = MLSTM_BLOCK
    NC = S // L
    DA = DV + 128
    qb, kb = OFF_MQ // DK, OFF_MK // DK
    vb, ob = OFF_MV // DV, OFF_MO // DV
    nq = (H * DK) // DK
    return pl.pallas_call(
        functools.partial(_mlstm_kernel, seq=S),
        grid=(B, H),
        in_specs=[pl.BlockSpec((1, S, DK), lambda b, h: (b, 0, qb + h)),
                  pl.BlockSpec((1, S, DK), lambda b, h: (b, 0, kb + h)),
                  pl.BlockSpec((1, S, DV), lambda b, h: (b, 0, vb + h)),
                  pl.BlockSpec((1, S, DV), lambda b, h: (b, 0, ob + h)),
                  pl.BlockSpec((1, 2 * H, NC, L), lambda b, h: (b, 0, 0, 0)),
                  pl.BlockSpec((CONV_WIDTH, DK), lambda b, h: (0, h)),
                  pl.BlockSpec((CONV_WIDTH, DK), lambda b, h: (0, nq + h)),
                  pl.BlockSpec((1, DK), lambda b, h: (0, h)),
                  pl.BlockSpec((1, DK), lambda b, h: (0, nq + h)),
                  pl.BlockSpec((1, 128), lambda b, h: (0, 0)),
                  pl.BlockSpec((1, DV), lambda b, h: (0, h))],
        out_specs=pl.BlockSpec((1, S, DV), lambda b, h: (b, 0, h)),
        out_shape=jax.ShapeDtypeStruct((B, S, H * DV), BF16),
        scratch_shapes=[pltpu.VMEM((S, DK), BF16),
                        pltpu.VMEM((S, DK), BF16),
                        pltpu.VMEM((S, DA), BF16),
                        pltpu.VMEM((5, NC, L), F32),
                        pltpu.VMEM((S, DA), F32),
                        pltpu.VMEM((NC, DK, DA), F32),
                        pltpu.VMEM((S, L), F32),
                        pltpu.VMEM((S, L), F32),
                        pltpu.VMEM((DK, DA), F32)],
        compiler_params=pltpu.CompilerParams(
            dimension_semantics=("arbitrary", "arbitrary"), vmem_limit_bytes=VMEM_LIMIT),
        name="mlstm_chunkwise",
    )(proj3, proj3, proj3, proj3, gates_t, conv_w, conv_w, conv_b, conv_b, bg_row, g_mlstm)


def _rms(y, g):
    ms = jnp.mean(y * y, axis=-1, keepdims=True)
    return y * lax.rsqrt(ms + NORM_EPS) * g


def _merge_kernel(ya_ref, yb_ref, ga_ref, gb_ref, x_ref, mod_ref, wa_ref, wb_ref, wo_ref,
                  gpost_ref, gpre_ref, x1_ref, h2_ref):
    pa = jnp.dot(ya_ref[...], wa_ref[...], preferred_element_type=F32)
    pb = jnp.dot(yb_ref[...], wb_ref[...], preferred_element_type=F32)
    merged = (_sigmoid(ga_ref[...].astype(F32)) * pa
              + _sigmoid(gb_ref[...].astype(F32)) * pb)
    y = jnp.dot(merged.astype(BF16), wo_ref[...], preferred_element_type=F32)
    x1 = x_ref[...] + mod_ref[0, 2:3, :] * _rms(y, gpost_ref[...])
    x1_ref[...] = x1
    h2 = _rms(x1, gpre_ref[...]) * (1.0 + mod_ref[0, 4:5, :]) + mod_ref[0, 3:4, :]
    h2_ref[...] = _pack_pair(h2[:, :HALF], h2[:, HALF:])


def _merge(ya2, yb2, proj2, x2, mod3, wa, wb, wo, g_post, g_pre, seq):
    T = x2.shape[0]
    tm = 512
    per_b = seq // tm
    full = lambda shape: pl.BlockSpec(shape, lambda i: (0,) * len(shape))
    return pl.pallas_call(
        _merge_kernel,
        grid=(T // tm,),
        in_specs=[pl.BlockSpec((tm, ATT_GROUP_W), lambda i: (i, 0)),
                  pl.BlockSpec((tm, D_MODEL), lambda i: (i, 0)),
                  pl.BlockSpec((tm, D_MODEL), lambda i: (i, OFF_GA // D_MODEL)),
                  pl.BlockSpec((tm, D_MODEL), lambda i: (i, OFF_GB // D_MODEL)),
                  pl.BlockSpec((tm, D_MODEL), lambda i: (i, 0)),
                  pl.BlockSpec((1, 6, D_MODEL), lambda i: (i // per_b, 0, 0)),
                  full((ATT_GROUP_W, D_MODEL)), full((D_MODEL, D_MODEL)), full((D_MODEL, D_MODEL)),
                  full((1, D_MODEL)), full((1, D_MODEL))],
        out_specs=[pl.BlockSpec((tm, D_MODEL), lambda i: (i, 0)),
                   pl.BlockSpec((tm, HALF), lambda i: (i, 0))],
        out_shape=[jax.ShapeDtypeStruct((T, D_MODEL), F32),
                   jax.ShapeDtypeStruct((T, HALF), jnp.uint32)],
        compiler_params=pltpu.CompilerParams(
            dimension_semantics=("arbitrary",), vmem_limit_bytes=VMEM_LIMIT),
        name="merge_out_proj",
    )(ya2, yb2, proj2, proj2, x2, mod3, wa, wb, wo, g_post, g_pre)


def _router_kernel(h2_ref, rlo_ref, rhi_ref, bias_ref, idx_ref, w_ref, rank_ref, cnt_ref):
    E = N_EXPERTS
    tr = h2_ref.shape[0]
    gsz = E // N_GROUPS

    @pl.when(pl.program_id(0) == 0)
    def _():
        cnt_ref[...] = jnp.zeros(cnt_ref.shape, F32)

    lo, hi = _unpack_pair(h2_ref[...])
    logits = _nt(rlo_ref[...], lo.astype(BF16)) + _nt(rhi_ref[...], hi.astype(BF16))
    scores = _sigmoid(logits)
    sel = scores + bias_ref[:, 0:1]

    gi = lax.broadcasted_iota(jnp.int32, (gsz, tr), 0).astype(F32)
    gs_rows = []
    for g in range(N_GROUPS):
        blk = sel[g * gsz:(g + 1) * gsz, :]
        m1 = jnp.max(blk, axis=0, keepdims=True)
        a1 = jnp.min(jnp.where(blk == m1, gi, float(E)), axis=0, keepdims=True)
        m2 = jnp.max(jnp.where(gi == a1, -jnp.inf, blk), axis=0, keepdims=True)
        gs_rows.append(m1 + m2)
    gs = jnp.concatenate(gs_rows, axis=0)
    g8 = lax.broadcasted_iota(jnp.int32, (N_GROUPS, tr), 0).astype(F32)
    gmask = jnp.zeros((N_GROUPS, tr), F32)
    for _ in range(TOPK_GROUPS):
        m = jnp.max(gs, axis=0, keepdims=True)
        a = jnp.min(jnp.where(gs == m, g8, float(E)), axis=0, keepdims=True)
        hit = g8 == a
        gmask = jnp.where(hit, 1.0, gmask)
        gs = jnp.where(hit, -jnp.inf, gs)
    selm = jnp.concatenate(
        [jnp.where(gmask[g:g + 1, :] > 0.0, sel[g * gsz:(g + 1) * gsz, :], -jnp.inf)
         for g in range(N_GROUPS)], axis=0)

    ei = lax.broadcasted_iota(jnp.int32, (E, tr), 0).astype(F32)
    picks, weights, hits = [], [], []
    candidates = selm
    for _ in range(TOP_K):
        m = jnp.max(selm, axis=0, keepdims=True)
        a = jnp.min(jnp.where(selm == m, ei, float(E)), axis=0, keepdims=True)
        hit = ei == a
        picks.append(a)
        hits.append(hit)
        weights.append(jnp.sum(jnp.where(hit, scores, 0.0), axis=0, keepdims=True))
        selm = jnp.where(hit, -jnp.inf, selm)
    chosen = jnp.where(selm != candidates, 1.0, 0.0)
    wsum = weights[0]
    for w in weights[1:]:
        wsum = wsum + w

    ti = lax.broadcasted_iota(jnp.int32, (tr, tr), 0)
    tj = lax.broadcasted_iota(jnp.int32, (tr, tr), 1)
    before = (ti < tj).astype(BF16)
    pos = jnp.dot(chosen.astype(BF16), before, preferred_element_type=F32) + cnt_ref[:, 0:1]
    ranks = [jnp.sum(jnp.where(hit, pos, 0.0), axis=0, keepdims=True) for hit in hits]
    cnt_ref[...] = cnt_ref[...] + jnp.sum(chosen, axis=1, keepdims=True)

    idx_ref[...] = jnp.concatenate(picks, axis=0).astype(jnp.int32)
    w_ref[...] = jnp.concatenate([w / wsum * ROUTED_SCALE for w in weights], axis=0)
    rank_ref[...] = jnp.concatenate(ranks, axis=0).astype(jnp.int32)


def _router(h2p, r_lo, r_hi, bias_col, row0, T):
    tr = 512
    off = row0 // tr
    full = lambda shape: pl.BlockSpec(shape, lambda i: (0,) * len(shape))
    return pl.pallas_call(
        _router_kernel,
        grid=(T // tr,),
        in_specs=[pl.BlockSpec((tr, HALF), lambda i: (i + off, 0)),
                  full((N_EXPERTS, HALF)), full((N_EXPERTS, HALF)), full((N_EXPERTS, 128))],
        out_specs=[pl.BlockSpec((TOP_K, tr), lambda i: (0, i)),
                   pl.BlockSpec((TOP_K, tr), lambda i: (0, i)),
                   pl.BlockSpec((TOP_K, tr), lambda i: (0, i)),
                   full((N_EXPERTS, 128))],
        out_shape=[jax.ShapeDtypeStruct((TOP_K, T), jnp.int32),
                   jax.ShapeDtypeStruct((TOP_K, T), F32),
                   jax.ShapeDtypeStruct((TOP_K, T), jnp.int32),
                   jax.ShapeDtypeStruct((N_EXPERTS, 128), F32)],
        compiler_params=pltpu.CompilerParams(
            dimension_semantics=("arbitrary",), vmem_limit_bytes=VMEM_LIMIT),
        name="router_topk",
    )(h2p, r_lo, r_hi, bias_col)


def _dest_kernel(idx_ref, rank_ref, pstart_ref, dest_ref):
    tr = idx_ref.shape[1]
    ei = lax.broadcasted_iota(jnp.int32, (N_EXPERTS, tr), 0)
    start = pstart_ref[:, 0:1]
    rows = []
    for k in range(TOP_K):
        hit = ei == idx_ref[k:k + 1, :]
        rows.append(jnp.sum(jnp.where(hit, start, 0.0), axis=0, keepdims=True))
    dest_ref[...] = jnp.concatenate(rows, axis=0).astype(jnp.int32) + rank_ref[...]


def _slot_index(idx, rank, pstart_col):
    T = idx.shape[1]
    tr = 1024
    return pl.pallas_call(
        _dest_kernel,
        grid=(T // tr,),
        in_specs=[pl.BlockSpec((TOP_K, tr), lambda i: (0, i)),
                  pl.BlockSpec((TOP_K, tr), lambda i: (0, i)),
                  pl.BlockSpec((N_EXPERTS, 128), lambda i: (0, 0))],
        out_specs=pl.BlockSpec((TOP_K, tr), lambda i: (0, i)),
        out_shape=jax.ShapeDtypeStruct((TOP_K, T), jnp.int32),
        name="slot_index",
    )(idx, rank, pstart_col)


def _ffn_kernel(first_ref, nblk_ref, nused_ref, xs_hbm, wg_ref, wu_ref, wd_ref, ys_hbm,
                xbuf, ybuf, in_sem, out_sem, wg_s, wu_s, wd_s):
    e = pl.program_id(0)
    bm = EXPERT_BLOCK
    ns = EXPERT_SLOTS
    nused = nused_ref[0]
    first = first_ref[e]
    n = nblk_ref[e]

    def in_copy(g):
        slot = g % ns
        return pltpu.make_async_copy(xs_hbm.at[pl.ds(g * bm, bm)], xbuf.at[slot], in_sem.at[slot])

    def out_copy(g):
        slot = g % ns
        return pltpu.make_async_copy(ybuf.at[slot], ys_hbm.at[pl.ds(g * bm, bm)], out_sem.at[slot])

    def fetch(g):
        @pl.when(g < nused)
        def _():
            in_copy(g).start()

    def release(g):
        @pl.when(g >= ns)
        def _():
            out_copy(g - ns).wait()

    def ffn(g):
        lo, hi = _unpack_pair(xbuf[g % ns])
        x = jnp.concatenate([lo.astype(BF16), hi.astype(BF16)], axis=1)
        gate = jnp.dot(x, wg_s[...], preferred_element_type=F32)
        up = jnp.dot(x, wu_s[...], preferred_element_type=F32)
        hid = (_silu(gate) * up).astype(BF16)
        return jnp.dot(hid, wd_s[...], preferred_element_type=F32)

    def pack(g, out):
        ybuf[g % ns] = _pack_pair(out[:, :HALF], out[:, HALF:])

    @pl.when(e == 0)
    def _():
        for q in range(ns - 1):
            fetch(q)

    @pl.when(n > 0)
    def _():
        wg_s[...] = wg_ref[0].astype(BF16)
        wu_s[...] = wu_ref[0].astype(BF16)
        wd_s[...] = wd_ref[0].astype(BF16)

        def two_blocks(j, _):
            g = first + 2 * j
            in_copy(g).wait()
            in_copy(g + 1).wait()
            fetch(g + ns - 1)
            release(g)
            release(g + 1)
            out_a = ffn(g)
            out_b = ffn(g + 1)
            pack(g, out_a)
            pack(g + 1, out_b)
            out_copy(g).start()
            out_copy(g + 1).start()
            fetch(g + ns)
            return 0

        lax.fori_loop(0, n // 2, two_blocks, 0)

        @pl.when(n % 2 == 1)
        def _():
            g = first + n - 1
            in_copy(g).wait()
            fetch(g + ns - 1)
            release(g)
            pack(g, ffn(g))
            out_copy(g).start()

    @pl.when(e == pl.num_programs(0) - 1)
    def _():
        for q in range(ns, 0, -1):
            @pl.when(nused >= q)
            def _(q=q):
                out_copy(nused - q).wait()


def _expert_ffn(first_blk, nblk, nused, xs, w_gate, w_up, w_down):
    P = xs.shape[0]
    bm = EXPERT_BLOCK
    w_map = lambda e, *_: (e, 0, 0)
    grid_spec = pltpu.PrefetchScalarGridSpec(
        num_scalar_prefetch=3,
        grid=(w_gate.shape[0],),
        in_specs=[pl.BlockSpec(memory_space=pl.ANY),
                  pl.BlockSpec((1, D_MODEL, EXPERT_FF), w_map),
                  pl.BlockSpec((1, D_MODEL, EXPERT_FF), w_map),
                  pl.BlockSpec((1, EXPERT_FF, D_MODEL), w_map)],
        out_specs=pl.BlockSpec(memory_space=pl.ANY),
        scratch_shapes=[pltpu.VMEM((EXPERT_SLOTS, bm, HALF), jnp.uint32),
                        pltpu.VMEM((EXPERT_SLOTS, bm, HALF), jnp.uint32),
                        pltpu.SemaphoreType.DMA((EXPERT_SLOTS,)),
                        pltpu.SemaphoreType.DMA((EXPERT_SLOTS,)),
                        pltpu.VMEM((D_MODEL, EXPERT_FF), BF16),
                        pltpu.VMEM((D_MODEL, EXPERT_FF), BF16),
                        pltpu.VMEM((EXPERT_FF, D_MODEL), BF16)],
    )
    return pl.pallas_call(
        _ffn_kernel,
        grid_spec=grid_spec,
        out_shape=jax.ShapeDtypeStruct((P, HALF), jnp.uint32),
        compiler_params=pltpu.CompilerParams(
            dimension_semantics=("arbitrary",), vmem_limit_bytes=VMEM_LIMIT),
        name="routed_experts",
    )(first_blk, nblk, nused, xs, w_gate, w_up, w_down)


def _final_kernel(yg_ref, w_ref, h2_ref, x1_ref, mod_ref, wsg_ref, wsu_ref, wsd_ref, gpost_ref, *rest):
    o_ref = rest[-1]
    lo, hi = _unpack_pair(h2_ref[...])
    h2 = jnp.concatenate([lo.astype(BF16), hi.astype(BF16)], axis=1)
    gate = jnp.dot(h2, wsg_ref[...], preferred_element_type=F32)
    up = jnp.dot(h2, wsu_ref[...], preferred_element_type=F32)
    shared = jnp.dot((_silu(gate) * up).astype(BF16), wsd_ref[...], preferred_element_type=F32)
    y_lo = shared[:, :HALF]
    y_hi = shared[:, HALF:]
    for k in range(TOP_K):
        r_lo, r_hi = _unpack_pair(yg_ref[k])
        wk = w_ref[:, k:k + 1]
        y_lo = y_lo + wk * r_lo
        y_hi = y_hi + wk * r_hi
    ms = (jnp.sum(y_lo * y_lo, axis=-1, keepdims=True)
          + jnp.sum(y_hi * y_hi, axis=-1, keepdims=True)) * (1.0 / D_MODEL)
    inv = lax.rsqrt(ms + NORM_EPS)
    o_ref[:, 0:HALF] = x1_ref[:, 0:HALF] + mod_ref[0, 5:6, 0:HALF] * (y_lo * inv * gpost_ref[:, 0:HALF])
    o_ref[:, HALF:] = x1_ref[:, HALF:] + mod_ref[0, 5:6, HALF:] * (y_hi * inv * gpost_ref[:, HALF:])


def _final(yg, w_tk, h2p, x1, mod3, wsg, wsu, wsd, g_post, seq, row0, out_prev):
    T = x1.shape[0]
    tp = yg.shape[1]
    tm = 512
    per_b = seq // tm
    off = row0 // tm
    full = lambda shape: pl.BlockSpec(shape, lambda i: (0,) * len(shape))
    in_specs = [pl.BlockSpec((TOP_K, tm, HALF), lambda i: (0, i, 0)),
                pl.BlockSpec((tm, TOP_K), lambda i: (i, 0)),
                pl.BlockSpec((tm, HALF), lambda i: (i + off, 0)),
                pl.BlockSpec((tm, D_MODEL), lambda i: (i + off, 0)),
                pl.BlockSpec((1, 6, D_MODEL), lambda i: ((i + off) // per_b, 0, 0)),
                full((D_MODEL, EXPERT_FF)), full((D_MODEL, EXPERT_FF)), full((EXPERT_FF, D_MODEL)),
                full((1, D_MODEL))]
    args = [yg, w_tk, h2p, x1, mod3, wsg, wsu, wsd, g_post]
    aliases = {}
    if out_prev is not None:
        in_specs.append(pl.BlockSpec(memory_space=pl.ANY))
        args.append(out_prev)
        aliases = {len(args) - 1: 0}
    return pl.pallas_call(
        _final_kernel,
        grid=(tp // tm,),
        in_specs=in_specs,
        out_specs=pl.BlockSpec((tm, D_MODEL), lambda i: (i + off, 0)),
        out_shape=jax.ShapeDtypeStruct((T, D_MODEL), F32),
        input_output_aliases=aliases,
        compiler_params=pltpu.CompilerParams(
            dimension_semantics=("arbitrary",), vmem_limit_bytes=VMEM_LIMIT),
        name="shared_expert_combine",
    )(*args)


def _rope_tables(positions):
    inv = jnp.power(ROPE_THETA, -jnp.arange(ROPE_HALF, dtype=F32) / ROPE_HALF)
    ang = positions.astype(F32)[..., None] * inv
    cos, sin = jnp.cos(ang), jnp.sin(ang)
    rest = ATT_HEAD_DIM - 2 * ROPE_HALF
    cs = jnp.concatenate([cos, cos, jnp.ones(ang.shape[:-1] + (rest,), F32)], axis=-1)
    sn = jnp.concatenate([-sin, sin, jnp.zeros(ang.shape[:-1] + (rest,), F32)], axis=-1)
    return jnp.tile(cs, (1, 1, 2)), jnp.tile(sn, (1, 1, 2))


def _layer(x, c, positions, w_ada, b_ada, g_pre_mix, g_post_mix, g_pre_ffn, g_post_ffn,
           w_in, conv_w, conv_b, b_gates, g_mlstm, w_branch_a, w_branch_b, w_out,
           router_w, router_bias, w_exp_gate, w_exp_up, w_exp_down, w_sh_gate, w_sh_up, w_sh_down):
    B, S, D = x.shape
    T = B * S
    H = MLSTM_HEADS
    x2 = x.reshape(T, D)

    mod3 = _adaln(c, w_ada, b_ada).reshape(B, 6, D)

    a_w = 3 * ATT_GROUP_W
    o_mq = 3 * a_w
    o_mk = o_mq + H * MLSTM_QK_DIM
    o_mv = o_mk + H * MLSTM_QK_DIM
    o_mo = o_mv + H * MLSTM_V_DIM
    o_mi = o_mo + H * MLSTM_V_DIM
    o_ga = o_mi + 2 * H
    o_gb = o_ga + D
    seg = lambda o, w: w_in[:, o:o + w]
    w_main = jnp.concatenate(
        [seg(o_mv, H * MLSTM_V_DIM), seg(o_mo, H * MLSTM_V_DIM), seg(o_ga, D), seg(o_gb, D),
         seg(o_mq, H * MLSTM_QK_DIM), seg(o_mk, H * MLSTM_QK_DIM),
         seg(0, a_w), seg(a_w, a_w), seg(2 * a_w, a_w)], axis=1).astype(BF16)
    w_if = jnp.pad(seg(o_mi, 2 * H), ((0, 0), (0, 128 - 2 * H))).astype(BF16)

    proj, gates = _in_proj(x2, mod3, g_pre_mix.reshape(1, D), w_main, w_if, S)
    proj3 = proj.reshape(B, S, PROJ_W)

    cs, sn = _rope_tables(positions)
    y_a = _attention(proj3, cs, sn)

    bg_row = jnp.pad(b_gates.reshape(1, 2 * H), ((0, 0), (0, 128 - 2 * H)))
    gates_t = gates[:, :2 * H].reshape(B, S, 2 * H).transpose(0, 2, 1)
    gates_t = gates_t.reshape(B, 2 * H, S // MLSTM_BLOCK, MLSTM_BLOCK)
    y_b = _mlstm(proj3, gates_t, conv_w, conv_b.reshape(1, -1), bg_row, g_mlstm.reshape(1, -1))

    x1, h2p = _merge(y_a.reshape(T, ATT_GROUP_W), y_b.reshape(T, D), proj, x2, mod3,
                     w_branch_a.astype(BF16), w_branch_b.astype(BF16), w_out.astype(BF16),
                     g_post_mix.reshape(1, D), g_pre_ffn.reshape(1, D), S)

    rw_t = router_w.T.astype(BF16)
    bias_col = jnp.broadcast_to(router_bias.reshape(N_EXPERTS, 1), (N_EXPERTS, 128))
    wsg, wsu, wsd = w_sh_gate.astype(BF16), w_sh_up.astype(BF16), w_sh_down.astype(BF16)

    tp = T // MOE_PARTS
    bm = EXPERT_BLOCK
    nb = (tp * TOP_K) // bm + N_EXPERTS
    out = None
    for part in range(MOE_PARTS):
        row0 = part * tp
        idx, wts, rank, cnt = _router(h2p, rw_t[:, :HALF], rw_t[:, HALF:], bias_col, row0, tp)

        counts = cnt[:, 0].astype(jnp.int32)
        padded = (counts + bm - 1) // bm * bm
        pend = jnp.cumsum(padded)
        pstart = pend - padded
        pstart_col = jnp.broadcast_to(pstart.astype(F32).reshape(N_EXPERTS, 1), (N_EXPERTS, 128))
        dest = _slot_index(idx, rank, pstart_col)
        nused = (pend[-1] // bm).astype(jnp.int32).reshape(1)

        xs = _dispatch(h2p, dest, nb * bm, row0)
        ys = _expert_ffn((pstart // bm).astype(jnp.int32), (padded // bm).astype(jnp.int32), nused,
                         xs, w_exp_gate, w_exp_up, w_exp_down)
        yg = _collect(ys, dest)
        out = _final(yg, wts.T, h2p, x1, mod3, wsg, wsu, wsd, g_post_ffn.reshape(1, D), S, row0, out)
    return out.reshape(B, S, D)


SC_CORES = 2
SC_SUBCORES = 16
SC_WORKERS = SC_CORES * SC_SUBCORES
SC_ROWS = 64


def _sc_mesh():
    return plsc.VectorSubcoreMesh(core_axis_name="c", subcore_axis_name="s",
                                  num_cores=SC_CORES, num_subcores=SC_SUBCORES)


def _worker_id():
    return lax.axis_index("s") * SC_CORES + lax.axis_index("c")


def _dispatch(h2p, dest, n_slots, row0):
    T = dest.shape[1]
    per_w = T // SC_WORKERS
    nch = per_w // SC_ROWS
    idx = dest.reshape(TOP_K, SC_WORKERS, nch, SC_ROWS).transpose(1, 2, 0, 3)
    idx = idx.reshape(SC_WORKERS, nch * TOP_K, SC_ROWS)

    def body(x_hbm, idx_hbm, xs_hbm, idx_v, buf0, buf1, rsem0, rsem1, ssem0, ssem1):
        wid = _worker_id()
        base = row0 + wid * per_w
        pltpu.sync_copy(idx_hbm.at[wid], idx_v)
        bufs = ((buf0, rsem0, ssem0), (buf1, rsem1, ssem1))

        def read(c, buf, rsem):
            return pltpu.make_async_copy(x_hbm.at[pl.ds(base + c * SC_ROWS, SC_ROWS)], buf, rsem)

        def scatter(c, k, buf, ssem):
            return pltpu.make_async_copy(buf, xs_hbm.at[idx_v.at[c * TOP_K + k]], ssem)

        read(0, buf0, rsem0).start()

        @pl.loop(0, nch, step=2)
        def _(c0):
            for b in range(2):
                c = c0 + b
                buf, rsem, ssem = bufs[b]
                obuf, orsem, ossem = bufs[1 - b]
                read(c, buf, rsem).wait()

                @pl.when(c > 0)
                def _():
                    for k in range(TOP_K):
                        scatter(c - 1, k, obuf, ossem).wait()

                @pl.when(c + 1 < nch)
                def _():
                    read(c + 1, obuf, orsem).start()

                for k in range(TOP_K):
                    scatter(c, k, buf, ssem).start()

        for k in range(TOP_K):
            scatter(nch - 1, k, buf1, ssem1).wait()

    run = pl.kernel(
        body,
        out_type=jax.ShapeDtypeStruct((n_slots, HALF), jnp.uint32),
        mesh=_sc_mesh(),
        scratch_types=[pltpu.VMEM((nch * TOP_K, SC_ROWS), jnp.int32),
                       pltpu.VMEM((SC_ROWS, HALF), jnp.uint32),
                       pltpu.VMEM((SC_ROWS, HALF), jnp.uint32),
                       pltpu.SemaphoreType.DMA, pltpu.SemaphoreType.DMA,
                       pltpu.SemaphoreType.DMA, pltpu.SemaphoreType.DMA],
        name="sc_dispatch",
    )
    return run(h2p, idx)


def _collect(ys, dest):
    n = dest.size
    per_w = n // SC_WORKERS
    nch = per_w // SC_ROWS
    idx = dest.reshape(SC_WORKERS, nch, SC_ROWS)

    def body(ys_hbm, idx_hbm, out_hbm, idx_v, buf0, buf1, gsem0, gsem1, wsem0, wsem1):
        wid = _worker_id()
        base = wid * per_w
        pltpu.sync_copy(idx_hbm.at[wid], idx_v)
        bufs = ((buf0, gsem0, wsem0), (buf1, gsem1, wsem1))

        def gather(c, buf, gsem):
            return pltpu.make_async_copy(ys_hbm.at[idx_v.at[c]], buf, gsem)

        def write(c, buf, wsem):
            return pltpu.make_async_copy(buf, out_hbm.at[pl.ds(base + c * SC_ROWS, SC_ROWS)], wsem)

        gather(0, buf0, gsem0).start()

        @pl.loop(0, nch, step=2)
        def _(c0):
            for b in range(2):
                c = c0 + b
                buf, gsem, wsem = bufs[b]
                obuf, ogsem, owsem = bufs[1 - b]
                gather(c, buf, gsem).wait()

                @pl.when(c > 0)
                def _():
                    write(c - 1, obuf, owsem).wait()

                @pl.when(c + 1 < nch)
                def _():
                    gather(c + 1, obuf, ogsem).start()

                write(c, buf, wsem).start()

        write(nch - 1, buf1, wsem1).wait()

    run = pl.kernel(
        body,
        out_type=jax.ShapeDtypeStruct((n, HALF), jnp.uint32),
        mesh=_sc_mesh(),
        scratch_types=[pltpu.VMEM((nch, SC_ROWS), jnp.int32),
                       pltpu.VMEM((SC_ROWS, HALF), jnp.uint32),
                       pltpu.VMEM((SC_ROWS, HALF), jnp.uint32),
                       pltpu.SemaphoreType.DMA, pltpu.SemaphoreType.DMA,
                       pltpu.SemaphoreType.DMA, pltpu.SemaphoreType.DMA],
        name="sc_collect",
    )
    return run(ys, idx).reshape(dest.shape + (HALF,))


def kernel(x, c, positions, w_ada, b_ada, g_pre_mix, g_post_mix, g_pre_ffn, g_post_ffn, w_in, conv_w, conv_b, b_gates, g_mlstm, w_branch_a, w_branch_b, w_out, router_w, router_bias, w_exp_gate, w_exp_up, w_exp_down, w_sh_gate, w_sh_up, w_sh_down):
    depth = w_ada.shape[0]
    for l in range(depth):
        x = _layer(x, c, positions, w_ada[l], b_ada[l], g_pre_mix[l], g_post_mix[l], g_pre_ffn[l],
                   g_post_ffn[l], w_in[l], conv_w[l], conv_b[l], b_gates[l], g_mlstm[l],
                   w_branch_a[l], w_branch_b[l], w_out[l], router_w[l], router_bias[l],
                   w_exp_gate[l], w_exp_up[l], w_exp_down[l], w_sh_gate[l], w_sh_up[l], w_sh_down[l])
    return x
```

```python
import functools

import jax
import jax.numpy as jnp
from jax import lax
from jax.experimental import pallas as pl
from jax.experimental.pallas import tpu as pltpu
from jax.experimental.pallas import tpu_sc as plsc

F32 = jnp.float32
BF16 = jnp.bfloat16
HIGHEST = lax.Precision.HIGHEST

D_MODEL = 1024
ATT_GROUPS = ((128, 1), (512, 4), (2048, 16))
ATT_HEAD_DIM = 64
ATT_GROUP_W = 256
ATT_BLK = 128
ATT_PAIR = 2
ROPE_THETA = 500000.0
ROPE_HALF = 8
MLSTM_HEADS = 4
MLSTM_QK_DIM = 128
MLSTM_V_DIM = 256
MLSTM_BLOCK = 128
MLSTM_GROUP = 8
CONV_WIDTH = 4
N_EXPERTS = 256
TOP_K = 8
N_GROUPS = 8
TOPK_GROUPS = 4
EXPERT_FF = 256
ROUTED_SCALE = 2.5
NORM_EPS = 1e-6
NEG = -1e30

OFF_MV, OFF_MO, OFF_GA, OFF_GB = 0, 1024, 2048, 3072
OFF_MQ, OFF_MK = 4096, 4608
OFF_AQ, OFF_AK, OFF_AV = 5120, 5888, 6656
PROJ_W = 7424
HALF = D_MODEL // 2

EXPERT_BLOCK = 512
EXPERT_SLOTS = 6
MOE_PARTS = 2
VMEM_LIMIT = 56 * 1024 * 1024


def _nt(a, b, precision=None):
    return lax.dot_general(a, b, (((1,), (1,)), ((), ())), preferred_element_type=F32,
                           precision=precision)


def _tn(a, b):
    return lax.dot_general(a, b, (((0,), (0,)), ((), ())), preferred_element_type=F32)


_sigmoid = jax.nn.sigmoid


def _silu(x):
    return x * _sigmoid(x)


def _pack_pair(lo, hi):
    lo_b = pltpu.bitcast(lo.astype(BF16).astype(F32), jnp.uint32)
    hi_b = pltpu.bitcast(hi.astype(BF16).astype(F32), jnp.uint32)
    return (lo_b >> 16) | (hi_b & jnp.uint32(0xFFFF0000))


def _unpack_pair(w):
    lo = pltpu.bitcast(w << 16, F32)
    hi = pltpu.bitcast(w & jnp.uint32(0xFFFF0000), F32)
    return lo, hi


def _mod_kernel(c_ref, w_ref, b_ref, o_ref):
    a = _silu(c_ref[...])
    o_ref[...] = jnp.dot(a, w_ref[...], preferred_element_type=F32, precision=HIGHEST) + b_ref[...]


def _adaln(c, w_ada, b_ada):
    B = c.shape[0]
    n = w_ada.shape[1]
    tn = 512
    return pl.pallas_call(
        _mod_kernel,
        grid=(n // tn,),
        in_specs=[pl.BlockSpec((B, D_MODEL), lambda j: (0, 0)),
                  pl.BlockSpec((D_MODEL, tn), lambda j: (0, j)),
                  pl.BlockSpec((1, tn), lambda j: (0, j))],
        out_specs=pl.BlockSpec((B, tn), lambda j: (0, j)),
        out_shape=jax.ShapeDtypeStruct((B, n), F32),
        name="adaln_mod",
    )(c, w_ada, b_ada.reshape(1, n))


def _proj_kernel(x_ref, mod_ref, g_ref, w_ref, wif_ref, o_ref, gates_ref, h_ref):
    @pl.when(pl.program_id(1) == 0)
    def _():
        x = x_ref[...]
        ms = jnp.mean(x * x, axis=-1, keepdims=True)
        y = x * lax.rsqrt(ms + NORM_EPS) * g_ref[...]
        h = (y * (1.0 + mod_ref[0, 1:2, :]) + mod_ref[0, 0:1, :]).astype(BF16)
        h_ref[...] = h
        gates_ref[...] = jnp.dot(h, wif_ref[...], preferred_element_type=F32)

    o_ref[...] = jnp.dot(h_ref[...], w_ref[...], preferred_element_type=F32).astype(BF16)


def _in_proj(x2, mod3, g_pre, w_main, w_if, seq):
    T = x2.shape[0]
    tm, tn = 1024, PROJ_W // 2
    per_b = seq // tm
    return pl.pallas_call(
        _proj_kernel,
        grid=(T // tm, PROJ_W // tn),
        in_specs=[pl.BlockSpec((tm, D_MODEL), lambda i, j: (i, 0)),
                  pl.BlockSpec((1, 6, D_MODEL), lambda i, j: (i // per_b, 0, 0)),
                  pl.BlockSpec((1, D_MODEL), lambda i, j: (0, 0)),
                  pl.BlockSpec((D_MODEL, tn), lambda i, j: (0, j)),
                  pl.BlockSpec((D_MODEL, 128), lambda i, j: (0, 0))],
        out_specs=[pl.BlockSpec((tm, tn), lambda i, j: (i, j)),
                   pl.BlockSpec((tm, 128), lambda i, j: (i, 0))],
        out_shape=[jax.ShapeDtypeStruct((T, PROJ_W), BF16),
                   jax.ShapeDtypeStruct((T, 128), F32)],
        scratch_shapes=[pltpu.VMEM((tm, D_MODEL), BF16)],
        compiler_params=pltpu.CompilerParams(
            dimension_semantics=("arbitrary", "arbitrary"), vmem_limit_bytes=VMEM_LIMIT),
        name="norm_in_proj",
    )(x2, mod3, g_pre, w_main, w_if)


def _attn_kernel(q_ref, k_ref, v_ref, cs_ref, sn_ref, o_ref, qf, kf, vf, acc, m_s, l_s, *, seq):
    g = pl.program_id(1)
    lane = lax.broadcasted_iota(jnp.int32, (ATT_BLK, 128), 1)
    first = (lane % ATT_HEAD_DIM) < ROPE_HALF
    low_head = lane < ATT_HEAD_DIM

    def rope(x, cs, sn):
        partner = jnp.where(first, pltpu.roll(x, 128 - ROPE_HALF, 1), pltpu.roll(x, ROPE_HALF, 1))
        return x * cs + partner * sn

    def zero_pad(i, _):
        rows = pl.ds(pl.multiple_of(i * ATT_BLK, ATT_BLK), ATT_BLK)
        for hp in range(2):
            kf[hp, rows, :] = jnp.zeros((ATT_BLK, 128), F32)
            vf[hp, rows, :] = jnp.zeros((ATT_BLK, 128), F32)
        return 0

    lax.fori_loop(0, seq // ATT_BLK, zero_pad, 0)

    def stage(i, _):
        r = pl.multiple_of(i * ATT_BLK, ATT_BLK)
        rows = pl.ds(r, ATT_BLK)
        prow = pl.ds(pl.multiple_of(seq + i * ATT_BLK, ATT_BLK), ATT_BLK)
        cs = cs_ref[0, rows, :]
        sn = sn_ref[0, rows, :]
        for hp in range(2):
            cols = pl.ds(hp * 128, 128)
            qf[hp, rows, :] = rope(q_ref[0, rows, cols].astype(F32), cs, sn) * (ATT_HEAD_DIM ** -0.5)
            kf[hp, prow, :] = rope(k_ref[0, rows, cols].astype(F32), cs, sn)
            vf[hp, prow, :] = v_ref[0, rows, cols].astype(F32)
        return 0

    lax.fori_loop(0, seq // ATT_BLK, stage, 0)

    qi = lax.broadcasted_iota(jnp.int32, (ATT_BLK, 2 * ATT_BLK), 0)
    ki = lax.broadcasted_iota(jnp.int32, (ATT_BLK, 2 * ATT_BLK), 1)
    band = (ki >= qi) & (ki <= qi + ATT_BLK)

    def process(d, init):
        span = ATT_BLK * d
        single = seq == span

        def body(cp, _):
            blocks = [cp * ATT_PAIR + i for i in range(ATT_PAIR)]
            qrows, krows, valid = [], [], []
            for c in blocks:
                rho = c % d
                n = c // d
                qstart = rho + n * span
                if single:
                    kstart, nk = seq + qstart, ATT_BLK
                    valid.append(band[:, ATT_BLK:])
                else:
                    kstart, nk = seq + qstart - span, 2 * ATT_BLK
                    valid.append(band & (ki >= jnp.where(n > 0, 0, ATT_BLK)))
                qrows.append(pl.ds(qstart, ATT_BLK, stride=d) if d > 1 else pl.ds(qstart, ATT_BLK))
                krows.append(pl.ds(kstart, nk, stride=d) if d > 1 else pl.ds(kstart, nk))
            units = [(b, hp) for b in range(ATT_PAIR) for hp in range(2)]
            heads = [(u, hh) for u in range(len(units)) for hh in range(2)]
            q2 = [qf[hp, qrows[b], :] for b, hp in units]
            k2 = [kf[hp, krows[b], :].astype(BF16) for b, hp in units]
            v2 = [vf[hp, krows[b], :].astype(BF16) for b, hp in units]
            qh = [jnp.where(low_head if hh == 0 else jnp.logical_not(low_head), q2[u], 0.0).astype(BF16)
                  for u, hh in heads]
            s = [jnp.where(valid[units[u][0]], _nt(qh[i], k2[u]), NEG) for i, (u, hh) in enumerate(heads)]
            m = [jnp.max(x, axis=1, keepdims=True) for x in s]
            p = [jnp.exp(x - mx) for x, mx in zip(s, m)]
            l = [jnp.sum(x, axis=1, keepdims=True) for x in p]
            o = [jnp.dot(p[i].astype(BF16), v2[u], preferred_element_type=F32)
                 for i, (u, hh) in enumerate(heads)]
            for u, (b, hp) in enumerate(units):
                o_b = jnp.where(low_head, o[2 * u], o[2 * u + 1])
                m_b = jnp.where(low_head, m[2 * u], m[2 * u + 1])
                l_b = jnp.where(low_head, l[2 * u], l[2 * u + 1])
                if init:
                    acc[hp, qrows[b], :] = o_b
                    m_s[hp, qrows[b], :] = m_b
                    l_s[hp, qrows[b], :] = l_b
                else:
                    m_old = m_s[hp, qrows[b], :]
                    m_new = jnp.maximum(m_old, m_b)
                    a_old = jnp.exp(m_old - m_new)
                    a_new = jnp.exp(m_b - m_new)
                    acc[hp, qrows[b], :] = acc[hp, qrows[b], :] * a_old + o_b * a_new
                    l_s[hp, qrows[b], :] = l_s[hp, qrows[b], :] * a_old + l_b * a_new
                    m_s[hp, qrows[b], :] = m_new
            return 0

        lax.fori_loop(0, seq // (ATT_BLK * ATT_PAIR), body, 0)

    for gi, (_, d) in enumerate(ATT_GROUPS):
        @pl.when(g == gi)
        def _(d=d, gi=gi):
            process(d, gi == 0)

    @pl.when(g == len(ATT_GROUPS) - 1)
    def _():
        def fin(i, _):
            rows = pl.ds(pl.multiple_of(i * ATT_BLK, ATT_BLK), ATT_BLK)
            for hp in range(2):
                o_ref[0, rows, pl.ds(hp * 128, 128)] = (acc[hp, rows, :] / l_s[hp, rows, :]).astype(BF16)
            return 0

        lax.fori_loop(0, seq // ATT_BLK, fin, 0)


def _attention(proj3, cs, sn):
    B, S, _ = proj3.shape
    ng = len(ATT_GROUPS)
    qb, kb, vb = OFF_AQ // ATT_GROUP_W, OFF_AK // ATT_GROUP_W, OFF_AV // ATT_GROUP_W
    return pl.pallas_call(
        functools.partial(_attn_kernel, seq=S),
        grid=(B, ng),
        in_specs=[pl.BlockSpec((1, S, ATT_GROUP_W), lambda b, g: (b, 0, qb + g)),
                  pl.BlockSpec((1, S, ATT_GROUP_W), lambda b, g: (b, 0, kb + g)),
                  pl.BlockSpec((1, S, ATT_GROUP_W), lambda b, g: (b, 0, vb + g)),
                  pl.BlockSpec((1, S, 128), lambda b, g: (b, 0, 0)),
                  pl.BlockSpec((1, S, 128), lambda b, g: (b, 0, 0))],
        out_specs=pl.BlockSpec((1, S, ATT_GROUP_W), lambda b, g: (b, 0, 0)),
        out_shape=jax.ShapeDtypeStruct((B, S, ATT_GROUP_W), BF16),
        scratch_shapes=[pltpu.VMEM((2, S, 128), F32),
                        pltpu.VMEM((2, 2 * S, 128), F32),
                        pltpu.VMEM((2, 2 * S, 128), F32),
                        pltpu.VMEM((2, S, 128), F32),
                        pltpu.VMEM((2, S, 128), F32),
                        pltpu.VMEM((2, S, 128), F32)],
        compiler_params=pltpu.CompilerParams(
            dimension_semantics=("arbitrary", "arbitrary"), vmem_limit_bytes=VMEM_LIMIT),
        name="dilated_attention",
    )(proj3, proj3, proj3, cs, sn)


def _log_sigmoid(x):
    return jnp.minimum(x, 0.0) - jnp.log(1.0 + jnp.exp(-jnp.abs(x)))


def _mlstm_kernel(mq_ref, mk_ref, mv_ref, mo_ref, gt_ref, cwq_ref, cwk_ref, cbq_ref, cbk_ref,
                  bg_ref, gm_ref, o_ref, q_s, k_s, va_s, rows_s, acc_s, kv_s, inter_s, emt_s,
                  c_s, *, seq):
    h = pl.program_id(1)
    L = MLSTM_BLOCK
    NC = seq // L
    DK, DV = MLSTM_QK_DIM, MLSTM_V_DIM
    DA = DV + 128
    nshift = CONV_WIDTH - 1

    tt = lax.broadcasted_iota(jnp.int32, (nshift * L, 2 * L), 0)
    uu = lax.broadcasted_iota(jnp.int32, (nshift * L, 2 * L), 1)
    shift_mat = (uu == L + tt % L - (tt // L + 1)).astype(BF16)
    conv_w = jnp.concatenate([cwq_ref[...], cwk_ref[...]], axis=1)
    conv_b = jnp.concatenate([cbq_ref[...], cbk_ref[...]], axis=1)
    prev = jnp.zeros((L, 2 * DK), BF16)
    for i in range(NC):
        blk = slice(i * L, (i + 1) * L)
        va_s[blk, 0:DV] = mv_ref[0, blk, :]
        va_s[blk, DV:DA] = jnp.ones((L, DA - DV), BF16)
        cur = jnp.concatenate([mq_ref[0, blk, :], mk_ref[0, blk, :]], axis=1)
        shifted = jnp.dot(shift_mat, jnp.concatenate([prev, cur], axis=0),
                          preferred_element_type=F32)
        y = conv_b + cur.astype(F32) * conv_w[nshift:nshift + 1, :]
        for s in range(nshift):
            y = y + shifted[s * L:(s + 1) * L, :] * conv_w[nshift - 1 - s:nshift - s, :]
        y = _silu(y)
        q_s[blk, :] = y[:, 0:DK].astype(BF16)
        k_s[blk, :] = (y[:, DK:2 * DK] * (DK ** -0.5)).astype(BF16)
        prev = cur

    lane = lax.broadcasted_iota(jnp.int32, (1, 128), 1)
    bias = bg_ref[...]
    b_i = jnp.sum(jnp.where(lane == h, bias, 0.0), axis=1, keepdims=True)
    b_f = jnp.sum(jnp.where(lane == h + MLSTM_HEADS, bias, 0.0), axis=1, keepdims=True)
    ri = lax.broadcasted_iota(jnp.int32, (L, L), 0)
    ci = lax.broadcasted_iota(jnp.int32, (L, L), 1)
    causal = ci <= ri
    eye = (ri == ci).astype(F32)
    i_rows = gt_ref[0, h] + b_i
    lf_rows = _log_sigmoid(gt_ref[0, h + MLSTM_HEADS] + b_f)
    b_rows = jnp.dot(lf_rows, (ri <= ci).astype(F32), preferred_element_type=F32,
                     precision=HIGHEST)
    b_end = b_rows[:, L - 1:L]
    g_rows = b_end - b_rows + i_rows
    g_max = jnp.max(g_rows, axis=1, keepdims=True)
    m = jnp.zeros((1, 1), F32)
    m_prev, m_new = [], []
    for c in range(NC):
        m_prev.append(m)
        m = jnp.maximum(b_end[c:c + 1, :] + m, g_max[c:c + 1, :])
        m_new.append(m)
    m_prev = jnp.concatenate(m_prev, axis=0)
    m_new = jnp.concatenate(m_new, axis=0)
    rows_s[0] = b_rows
    rows_s[1] = jnp.exp(g_rows - m_new)
    rows_s[2] = b_rows - i_rows
    rows_s[3] = jnp.broadcast_to(m_prev, (NC, L))
    rows_s[4] = jnp.broadcast_to(jnp.exp(b_end + m_prev - m_new), (NC, L))

    r2 = lax.broadcasted_iota(jnp.int32, (2 * L, 2 * L), 0)
    c2 = lax.broadcasted_iota(jnp.int32, (2 * L, 2 * L), 1)
    ones_blk = ((r2 < L) == (c2 < L)).astype(BF16)

    G = MLSTM_GROUP

    def local(cg, _):
        cs = [cg * G + i for i in range(G)]
        rows = [pl.ds(pl.multiple_of(c * L, L), L) for c in cs]
        b_r = [rows_s[0, pl.ds(c, 1), :] for c in cs]
        w_r = [rows_s[1, pl.ds(c, 1), :] for c in cs]
        u_r = [rows_s[2, pl.ds(c, 1), :] for c in cs]
        mp = [rows_s[3, pl.ds(c, 1), :] for c in cs]
        q = [q_s[r, :] for r in rows]
        k = [k_s[r, :] for r in rows]
        va = [va_s[r, :] for r in rows]
        qk = [_nt(a, b) for a, b in zip(q, k)]
        x2 = [jnp.concatenate([eye * a, eye * b], axis=1) for a, b in zip(b_r, w_r)]
        hi = [x.astype(BF16) for x in x2]
        lo = [(x - h_.astype(F32)).astype(BF16) for x, h_ in zip(x2, hi)]
        yb = [jnp.dot(h_, ones_blk, preferred_element_type=F32)
              + jnp.dot(l_, ones_blk, preferred_element_type=F32) for h_, l_ in zip(hi, lo)]
        b_b = [y[:, 0:L] for y in yb]
        w_b = [y[:, L:2 * L] for y in yb]
        for i in range(G):
            kv_s[cs[i]] = _tn((w_b[i] * k[i].astype(F32)).astype(BF16), va[i])
        dmat = [jnp.where(causal, b - u, NEG) for b, u in zip(b_b, u_r)]
        m_t = [jnp.maximum(b + m_, jnp.max(d, axis=1, keepdims=True))
               for b, m_, d in zip(b_b, mp, dmat)]
        sc = [a * jnp.exp(d - m_) for a, d, m_ in zip(qk, dmat, m_t)]
        for i in range(G):
            acc_s[rows[i], :] = jnp.dot(sc[i].astype(BF16), va[i], preferred_element_type=F32)
            inter_s[rows[i], :] = jnp.exp(b_b[i] + mp[i] - m_t[i])
            emt_s[rows[i], :] = jnp.exp(-m_t[i])
        return 0

    lax.fori_loop(0, NC // G, local, 0)

    g_row = gm_ref[...]
    c_s[...] = jnp.zeros((DK, DA), F32)

    def recur(cg, _):
        cs = [cg * G + i for i in range(G)]
        rows = [pl.ds(pl.multiple_of(c * L, L), L) for c in cs]
        states = [c_s[...]]
        for c in cs:
            dec = rows_s[4, pl.ds(c, 1), :]
            states.append(jnp.concatenate([dec, dec, dec], axis=1) * states[-1] + kv_s[c])
        c_s[...] = states[G]
        read = [jnp.dot(q_s[r, :], st.astype(BF16), preferred_element_type=F32)
                for r, st in zip(rows, states)]
        inter = [inter_s[r, :] for r in rows]
        out = [acc_s[r, :] + jnp.concatenate([it, it, it], axis=1) * rd
               for r, it, rd in zip(rows, inter, read)]
        emt = [emt_s[r, :] for r in rows]
        nrm = [jnp.maximum(jnp.abs(jnp.concatenate([o[:, DV:DA], o[:, DV:DA]], axis=1)),
                           jnp.concatenate([e_, e_], axis=1)) for o, e_ in zip(out, emt)]
        hh = [o[:, 0:DV] / n_ for o, n_ in zip(out, nrm)]
        ms = [jnp.mean(x * x, axis=1, keepdims=True) for x in hh]
        hn = [x * lax.rsqrt(m_ + NORM_EPS) * g_row for x, m_ in zip(hh, ms)]
        for i in range(G):
            o_ref[0, rows[i], :] = (hn[i] * _sigmoid(mo_ref[0, rows[i], :].astype(F32))).astype(BF16)
        return 0

    lax.fori_loop(0, NC // G, recur, 0)


def _mlstm(proj3, gates_t, conv_w, conv_b, bg_row, g_mlstm):
    B, S, _ = proj3.shape
    H, DK, DV = MLSTM_HEADS, MLSTM_QK_DIM, MLSTM_V_DIM
    L = MLSTM_BLOCK
    NC = S // L
    DA = DV + 128
    qb, kb = OFF_MQ // DK, OFF_MK // DK
    vb, ob = OFF_MV // DV, OFF_MO // DV
    nq = (H * DK) // DK
    return pl.pallas_call(
        functools.partial(_mlstm_kernel, seq=S),
        grid=(B, H),
        in_specs=[pl.BlockSpec((1, S, DK), lambda b, h: (b, 0, qb + h)),
                  pl.BlockSpec((1, S, DK), lambda b, h: (b, 0, kb + h)),
                  pl.BlockSpec((1, S, DV), lambda b, h: (b, 0, vb + h)),
                  pl.BlockSpec((1, S, DV), lambda b, h: (b, 0, ob + h)),
                  pl.BlockSpec((1, 2 * H, NC, L), lambda b, h: (b, 0, 0, 0)),
                  pl.BlockSpec((CONV_WIDTH, DK), lambda b, h: (0, h)),
                  pl.BlockSpec((CONV_WIDTH, DK), lambda b, h: (0, nq + h)),
                  pl.BlockSpec((1, DK), lambda b, h: (0, h)),
                  pl.BlockSpec((1, DK), lambda b, h: (0, nq + h)),
                  pl.BlockSpec((1, 128), lambda b, h: (0, 0)),
                  pl.BlockSpec((1, DV), lambda b, h: (0, h))],
        out_specs=pl.BlockSpec((1, S, DV), lambda b, h: (b, 0, h)),
        out_shape=jax.ShapeDtypeStruct((B, S, H * DV), BF16),
        scratch_shapes=[pltpu.VMEM((S, DK), BF16),
                        pltpu.VMEM((S, DK), BF16),
                        pltpu.VMEM((S, DA), BF16),
                        pltpu.VMEM((5, NC, L), F32),
                        pltpu.VMEM((S, DA), F32),
                        pltpu.VMEM((NC, DK, DA), F32),
                        pltpu.VMEM((S, L), F32),
                        pltpu.VMEM((S, L), F32),
                        pltpu.VMEM((DK, DA), F32)],
        compiler_params=pltpu.CompilerParams(
            dimension_semantics=("arbitrary", "arbitrary"), vmem_limit_bytes=VMEM_LIMIT),
        name="mlstm_chunkwise",
    )(proj3, proj3, proj3, proj3, gates_t, conv_w, conv_w, conv_b, conv_b, bg_row, g_mlstm)


def _rms(y, g):
    ms = jnp.mean(y * y, axis=-1, keepdims=True)
    return y * lax.rsqrt(ms + NORM_EPS) * g


def _merge_kernel(ya_ref, yb_ref, ga_ref, gb_ref, x_ref, mod_ref, wa_ref, wb_ref, wo_ref,
                  gpost_ref, gpre_ref, x1_ref, h2_ref):
    pa = jnp.dot(ya_ref[...], wa_ref[...], preferred_element_type=F32)
    pb = jnp.dot(yb_ref[...], wb_ref[...], preferred_element_type=F32)
    merged = (_sigmoid(ga_ref[...].astype(F32)) * pa
              + _sigmoid(gb_ref[...].astype(F32)) * pb)
    y = jnp.dot(merged.astype(BF16), wo_ref[...], preferred_element_type=F32)
    x1 = x_ref[...] + mod_ref[0, 2:3, :] * _rms(y, gpost_ref[...])
    x1_ref[...] = x1
    h2 = _rms(x1, gpre_ref[...]) * (1.0 + mod_ref[0, 4:5, :]) + mod_ref[0, 3:4, :]
    h2_ref[...] = _pack_pair(h2[:, :HALF], h2[:, HALF:])


def _merge(ya2, yb2, proj2, x2, mod3, wa, wb, wo, g_post, g_pre, seq):
    T = x2.shape[0]
    tm = 512
    per_b = seq // tm
    full = lambda shape: pl.BlockSpec(shape, lambda i: (0,) * len(shape))
    return pl.pallas_call(
        _merge_kernel,
        grid=(T // tm,),
        in_specs=[pl.BlockSpec((tm, ATT_GROUP_W), lambda i: (i, 0)),
                  pl.BlockSpec((tm, D_MODEL), lambda i: (i, 0)),
                  pl.BlockSpec((tm, D_MODEL), lambda i: (i, OFF_GA // D_MODEL)),
                  pl.BlockSpec((tm, D_MODEL), lambda i: (i, OFF_GB // D_MODEL)),
                  pl.BlockSpec((tm, D_MODEL), lambda i: (i, 0)),
                  pl.BlockSpec((1, 6, D_MODEL), lambda i: (i // per_b, 0, 0)),
                  full((ATT_GROUP_W, D_MODEL)), full((D_MODEL, D_MODEL)), full((D_MODEL, D_MODEL)),
                  full((1, D_MODEL)), full((1, D_MODEL))],
        out_specs=[pl.BlockSpec((tm, D_MODEL), lambda i: (i, 0)),
                   pl.BlockSpec((tm, HALF), lambda i: (i, 0))],
        out_shape=[jax.ShapeDtypeStruct((T, D_MODEL), F32),
                   jax.ShapeDtypeStruct((T, HALF), jnp.uint32)],
        compiler_params=pltpu.CompilerParams(
            dimension_semantics=("arbitrary",), vmem_limit_bytes=VMEM_LIMIT),
        name="merge_out_proj",
    )(ya2, yb2, proj2, proj2, x2, mod3, wa, wb, wo, g_post, g_pre)


def _router_kernel(h2_ref, rlo_ref, rhi_ref, bias_ref, idx_ref, w_ref, rank_ref, cnt_ref):
    E = N_EXPERTS
    tr = h2_ref.shape[0]
    gsz = E // N_GROUPS

    @pl.when(pl.program_id(0) == 0)
    def _():
        cnt_ref[...] = jnp.zeros(cnt_ref.shape, F32)

    lo, hi = _unpack_pair(h2_ref[...])
    logits = _nt(rlo_ref[...], lo.astype(BF16)) + _nt(rhi_ref[...], hi.astype(BF16))
    scores = _sigmoid(logits)
    sel = scores + bias_ref[:, 0:1]

    gi = lax.broadcasted_iota(jnp.int32, (gsz, tr), 0).astype(F32)
    gs_rows = []
    for g in range(N_GROUPS):
        blk = sel[g * gsz:(g + 1) * gsz, :]
        m1 = jnp.max(blk, axis=0, keepdims=True)
        a1 = jnp.min(jnp.where(blk == m1, gi, float(E)), axis=0, keepdims=True)
        m2 = jnp.max(jnp.where(gi == a1, -jnp.inf, blk), axis=0, keepdims=True)
        gs_rows.append(m1 + m2)
    gs = jnp.concatenate(gs_rows, axis=0)
    g8 = lax.broadcasted_iota(jnp.int32, (N_GROUPS, tr), 0).astype(F32)
    gmask = jnp.zeros((N_GROUPS, tr), F32)
    for _ in range(TOPK_GROUPS):
        m = jnp.max(gs, axis=0, keepdims=True)
        a = jnp.min(jnp.where(gs == m, g8, float(E)), axis=0, keepdims=True)
        hit = g8 == a
        gmask = jnp.where(hit, 1.0, gmask)
        gs = jnp.where(hit, -jnp.inf, gs)
    selm = jnp.concatenate(
        [jnp.where(gmask[g:g + 1, :] > 0.0, sel[g * gsz:(g + 1) * gsz, :], -jnp.inf)
         for g in range(N_GROUPS)], axis=0)

    ei = lax.broadcasted_iota(jnp.int32, (E, tr), 0).astype(F32)
    picks, weights, hits = [], [], []
    candidates = selm
    for _ in range(TOP_K):
        m = jnp.max(selm, axis=0, keepdims=True)
        a = jnp.min(jnp.where(selm == m, ei, float(E)), axis=0, keepdims=True)
        hit = ei == a
        picks.append(a)
        hits.append(hit)
        weights.append(jnp.sum(jnp.where(hit, scores, 0.0), axis=0, keepdims=True))
        selm = jnp.where(hit, -jnp.inf, selm)
    chosen = jnp.where(selm != candidates, 1.0, 0.0)
    wsum = weights[0]
    for w in weights[1:]:
        wsum = wsum + w

    ti = lax.broadcasted_iota(jnp.int32, (tr, tr), 0)
    tj = lax.broadcasted_iota(jnp.int32, (tr, tr), 1)
    before = (ti < tj).astype(BF16)
    pos = jnp.dot(chosen.astype(BF16), before, preferred_element_type=F32) + cnt_ref[:, 0:1]
    ranks = [jnp.sum(jnp.where(hit, pos, 0.0), axis=0, keepdims=True) for hit in hits]
    cnt_ref[...] = cnt_ref[...] + jnp.sum(chosen, axis=1, keepdims=True)

    idx_ref[...] = jnp.concatenate(picks, axis=0).astype(jnp.int32)
    w_ref[...] = jnp.concatenate([w / wsum * ROUTED_SCALE for w in weights], axis=0)
    rank_ref[...] = jnp.concatenate(ranks, axis=0).astype(jnp.int32)


def _router(h2p, r_lo, r_hi, bias_col, row0, T):
    tr = 512
    off = row0 // tr
    full = lambda shape: pl.BlockSpec(shape, lambda i: (0,) * len(shape))
    return pl.pallas_call(
        _router_kernel,
        grid=(T // tr,),
        in_specs=[pl.BlockSpec((tr, HALF), lambda i: (i + off, 0)),
                  full((N_EXPERTS, HALF)), full((N_EXPERTS, HALF)), full((N_EXPERTS, 128))],
        out_specs=[pl.BlockSpec((TOP_K, tr), lambda i: (0, i)),
                   pl.BlockSpec((TOP_K, tr), lambda i: (0, i)),
                   pl.BlockSpec((TOP_K, tr), lambda i: (0, i)),
                   full((N_EXPERTS, 128))],
        out_shape=[jax.ShapeDtypeStruct((TOP_K, T), jnp.int32),
                   jax.ShapeDtypeStruct((TOP_K, T), F32),
                   jax.ShapeDtypeStruct((TOP_K, T), jnp.int32),
                   jax.ShapeDtypeStruct((N_EXPERTS, 128), F32)],
        compiler_params=pltpu.CompilerParams(
            dimension_semantics=("arbitrary",), vmem_limit_bytes=VMEM_LIMIT),
        name="router_topk",
    )(h2p, r_lo, r_hi, bias_col)


def _dest_kernel(idx_ref, rank_ref, pstart_ref, dest_ref):
    tr = idx_ref.shape[1]
    ei = lax.broadcasted_iota(jnp.int32, (N_EXPERTS, tr), 0)
    start = pstart_ref[:, 0:1]
    rows = []
    for k in range(TOP_K):
        hit = ei == idx_ref[k:k + 1, :]
        rows.append(jnp.sum(jnp.where(hit, start, 0.0), axis=0, keepdims=True))
    dest_ref[...] = jnp.concatenate(rows, axis=0).astype(jnp.int32) + rank_ref[...]


def _slot_index(idx, rank, pstart_col):
    T = idx.shape[1]
    tr = 1024
    return pl.pallas_call(
        _dest_kernel,
        grid=(T // tr,),
        in_specs=[pl.BlockSpec((TOP_K, tr), lambda i: (0, i)),
                  pl.BlockSpec((TOP_K, tr), lambda i: (0, i)),
                  pl.BlockSpec((N_EXPERTS, 128), lambda i: (0, 0))],
        out_specs=pl.BlockSpec((TOP_K, tr), lambda i: (0, i)),
        out_shape=jax.ShapeDtypeStruct((TOP_K, T), jnp.int32),
        name="slot_index",
    )(idx, rank, pstart_col)


def _ffn_kernel(first_ref, nblk_ref, nused_ref, xs_hbm, wg_ref, wu_ref, wd_ref, ys_hbm,
                xbuf, ybuf, in_sem, out_sem, wg_s, wu_s, wd_s):
    e = pl.program_id(0)
    bm = EXPERT_BLOCK
    ns = EXPERT_SLOTS
    nused = nused_ref[0]
    first = first_ref[e]
    n = nblk_ref[e]

    def in_copy(g):
        slot = g % ns
        return pltpu.make_async_copy(xs_hbm.at[pl.ds(g * bm, bm)], xbuf.at[slot], in_sem.at[slot])

    def out_copy(g):
        slot = g % ns
        return pltpu.make_async_copy(ybuf.at[slot], ys_hbm.at[pl.ds(g * bm, bm)], out_sem.at[slot])

    def fetch(g):
        @pl.when(g < nused)
        def _():
            in_copy(g).start()

    def release(g):
        @pl.when(g >= ns)
        def _():
            out_copy(g - ns).wait()

    def ffn(g):
        lo, hi = _unpack_pair(xbuf[g % ns])
        x = jnp.concatenate([lo.astype(BF16), hi.astype(BF16)], axis=1)
        gate = jnp.dot(x, wg_s[...], preferred_element_type=F32)
        up = jnp.dot(x, wu_s[...], preferred_element_type=F32)
        hid = (_silu(gate) * up).astype(BF16)
        return jnp.dot(hid, wd_s[...], preferred_element_type=F32)

    def pack(g, out):
        ybuf[g % ns] = _pack_pair(out[:, :HALF], out[:, HALF:])

    @pl.when(e == 0)
    def _():
        for q in range(ns - 1):
            fetch(q)

    @pl.when(n > 0)
    def _():
        wg_s[...] = wg_ref[0].astype(BF16)
        wu_s[...] = wu_ref[0].astype(BF16)
        wd_s[...] = wd_ref[0].astype(BF16)

        def two_blocks(j, _):
            g = first + 2 * j
            in_copy(g).wait()
            in_copy(g + 1).wait()
            fetch(g + ns - 1)
            release(g)
            release(g + 1)
            out_a = ffn(g)
            out_b = ffn(g + 1)
            pack(g, out_a)
            pack(g + 1, out_b)
            out_copy(g).start()
            out_copy(g + 1).start()
            fetch(g + ns)
            return 0

        lax.fori_loop(0, n // 2, two_blocks, 0)

        @pl.when(n % 2 == 1)
        def _():
            g = first + n - 1
            in_copy(g).wait()
            fetch(g + ns - 1)
            release(g)
            pack(g, ffn(g))
            out_copy(g).start()

    @pl.when(e == pl.num_programs(0) - 1)
    def _():
        for q in range(ns, 0, -1):
            @pl.when(nused >= q)
            def _(q=q):
                out_copy(nused - q).wait()


def _expert_ffn(first_blk, nblk, nused, xs, w_gate, w_up, w_down):
    P = xs.shape[0]
    bm = EXPERT_BLOCK
    w_map = lambda e, *_: (e, 0, 0)
    grid_spec = pltpu.PrefetchScalarGridSpec(
        num_scalar_prefetch=3,
        grid=(w_gate.shape[0],),
        in_specs=[pl.BlockSpec(memory_space=pl.ANY),
                  pl.BlockSpec((1, D_MODEL, EXPERT_FF), w_map),
                  pl.BlockSpec((1, D_MODEL, EXPERT_FF), w_map),
                  pl.BlockSpec((1, EXPERT_FF, D_MODEL), w_map)],
        out_specs=pl.BlockSpec(memory_space=pl.ANY),
        scratch_shapes=[pltpu.VMEM((EXPERT_SLOTS, bm, HALF), jnp.uint32),
                        pltpu.VMEM((EXPERT_SLOTS, bm, HALF), jnp.uint32),
                        pltpu.SemaphoreType.DMA((EXPERT_SLOTS,)),
                        pltpu.SemaphoreType.DMA((EXPERT_SLOTS,)),
                        pltpu.VMEM((D_MODEL, EXPERT_FF), BF16),
                        pltpu.VMEM((D_MODEL, EXPERT_FF), BF16),
                        pltpu.VMEM((EXPERT_FF, D_MODEL), BF16)],
    )
    return pl.pallas_call(
        _ffn_kernel,
        grid_spec=grid_spec,
        out_shape=jax.ShapeDtypeStruct((P, HALF), jnp.uint32),
        compiler_params=pltpu.CompilerParams(
            dimension_semantics=("arbitrary",), vmem_limit_bytes=VMEM_LIMIT),
        name="routed_experts",
    )(first_blk, nblk, nused, xs, w_gate, w_up, w_down)


def _final_kernel(yg_ref, w_ref, h2_ref, x1_ref, mod_ref, wsg_ref, wsu_ref, wsd_ref, gpost_ref, *rest):
    o_ref = rest[-1]
    lo, hi = _unpack_pair(h2_ref[...])
    h2 = jnp.concatenate([lo.astype(BF16), hi.astype(BF16)], axis=1)
    gate = jnp.dot(h2, wsg_ref[...], preferred_element_type=F32)
    up = jnp.dot(h2, wsu_ref[...], preferred_element_type=F32)
    shared = jnp.dot((_silu(gate) * up).astype(BF16), wsd_ref[...], preferred_element_type=F32)
    y_lo = shared[:, :HALF]
    y_hi = shared[:, HALF:]
    for k in range(TOP_K):
        r_lo, r_hi = _unpack_pair(yg_ref[k])
        wk = w_ref[:, k:k + 1]
        y_lo = y_lo + wk * r_lo
        y_hi = y_hi + wk * r_hi
    ms = (jnp.sum(y_lo * y_lo, axis=-1, keepdims=True)
          + jnp.sum(y_hi * y_hi, axis=-1, keepdims=True)) * (1.0 / D_MODEL)
    inv = lax.rsqrt(ms + NORM_EPS)
    o_ref[:, 0:HALF] = x1_ref[:, 0:HALF] + mod_ref[0, 5:6, 0:HALF] * (y_lo * inv * gpost_ref[:, 0:HALF])
    o_ref[:, HALF:] = x1_ref[:, HALF:] + mod_ref[0, 5:6, HALF:] * (y_hi * inv * gpost_ref[:, HALF:])


def _final(yg, w_tk, h2p, x1, mod3, wsg, wsu, wsd, g_post, seq, row0, out_prev):
    T = x1.shape[0]
    tp = yg.shape[1]
    tm = 512
    per_b = seq // tm
    off = row0 // tm
    full = lambda shape: pl.BlockSpec(shape, lambda i: (0,) * len(shape))
    in_specs = [pl.BlockSpec((TOP_K, tm, HALF), lambda i: (0, i, 0)),
                pl.BlockSpec((tm, TOP_K), lambda i: (i, 0)),
                pl.BlockSpec((tm, HALF), lambda i: (i + off, 0)),
                pl.BlockSpec((tm, D_MODEL), lambda i: (i + off, 0)),
                pl.BlockSpec((1, 6, D_MODEL), lambda i: ((i + off) // per_b, 0, 0)),
                full((D_MODEL, EXPERT_FF)), full((D_MODEL, EXPERT_FF)), full((EXPERT_FF, D_MODEL)),
                full((1, D_MODEL))]
    args = [yg, w_tk, h2p, x1, mod3, wsg, wsu, wsd, g_post]
    aliases = {}
    if out_prev is not None:
        in_specs.append(pl.BlockSpec(memory_space=pl.ANY))
        args.append(out_prev)
        aliases = {len(args) - 1: 0}
    return pl.pallas_call(
        _final_kernel,
        grid=(tp // tm,),
        in_specs=in_specs,
        out_specs=pl.BlockSpec((tm, D_MODEL), lambda i: (i + off, 0)),
        out_shape=jax.ShapeDtypeStruct((T, D_MODEL), F32),
        input_output_aliases=aliases,
        compiler_params=pltpu.CompilerParams(
            dimension_semantics=("arbitrary",), vmem_limit_bytes=VMEM_LIMIT),
        name="shared_expert_combine",
    )(*args)


def _rope_tables(positions):
    inv = jnp.power(ROPE_THETA, -jnp.arange(ROPE_HALF, dtype=F32) / ROPE_HALF)
    ang = positions.astype(F32)[..., None] * inv
    cos, sin = jnp.cos(ang), jnp.sin(ang)
    rest = ATT_HEAD_DIM - 2 * ROPE_HALF
    cs = jnp.concatenate([cos, cos, jnp.ones(ang.shape[:-1] + (rest,), F32)], axis=-1)
    sn = jnp.concatenate([-sin, sin, jnp.zeros(ang.shape[:-1] + (rest,), F32)], axis=-1)
    return jnp.tile(cs, (1, 1, 2)), jnp.tile(sn, (1, 1, 2))


def _layer(x, c, positions, w_ada, b_ada, g_pre_mix, g_post_mix, g_pre_ffn, g_post_ffn,
           w_in, conv_w, conv_b, b_gates, g_mlstm, w_branch_a, w_branch_b, w_out,
           router_w, router_bias, w_exp_gate, w_exp_up, w_exp_down, w_sh_gate, w_sh_up, w_sh_down):
    B, S, D = x.shape
    T = B * S
    H = MLSTM_HEADS
    x2 = x.reshape(T, D)

    mod3 = _adaln(c, w_ada, b_ada).reshape(B, 6, D)

    a_w = 3 * ATT_GROUP_W
    o_mq = 3 * a_w
    o_mk = o_mq + H * MLSTM_QK_DIM
    o_mv = o_mk + H * MLSTM_QK_DIM
    o_mo = o_mv + H * MLSTM_V_DIM
    o_mi = o_mo + H * MLSTM_V_DIM
    o_ga = o_mi + 2 * H
    o_gb = o_ga + D
    seg = lambda o, w: w_in[:, o:o + w]
    w_main = jnp.concatenate(
        [seg(o_mv, H * MLSTM_V_DIM), seg(o_mo, H * MLSTM_V_DIM), seg(o_ga, D), seg(o_gb, D),
         seg(o_mq, H * MLSTM_QK_DIM), seg(o_mk, H * MLSTM_QK_DIM),
         seg(0, a_w), seg(a_w, a_w), seg(2 * a_w, a_w)], axis=1).astype(BF16)
    w_if = jnp.pad(seg(o_mi, 2 * H), ((0, 0), (0, 128 - 2 * H))).astype(BF16)

    proj, gates = _in_proj(x2, mod3, g_pre_mix.reshape(1, D), w_main, w_if, S)
    proj3 = proj.reshape(B, S, PROJ_W)

    cs, sn = _rope_tables(positions)
    y_a = _attention(proj3, cs, sn)

    bg_row = jnp.pad(b_gates.reshape(1, 2 * H), ((0, 0), (0, 128 - 2 * H)))
    gates_t = gates[:, :2 * H].reshape(B, S, 2 * H).transpose(0, 2, 1)
    gates_t = gates_t.reshape(B, 2 * H, S // MLSTM_BLOCK, MLSTM_BLOCK)
    y_b = _mlstm(proj3, gates_t, conv_w, conv_b.reshape(1, -1), bg_row, g_mlstm.reshape(1, -1))

    x1, h2p = _merge(y_a.reshape(T, ATT_GROUP_W), y_b.reshape(T, D), proj, x2, mod3,
                     w_branch_a.astype(BF16), w_branch_b.astype(BF16), w_out.astype(BF16),
                     g_post_mix.reshape(1, D), g_pre_ffn.reshape(1, D), S)

    rw_t = router_w.T.astype(BF16)
    bias_col = jnp.broadcast_to(router_bias.reshape(N_EXPERTS, 1), (N_EXPERTS, 128))
    wsg, wsu, wsd = w_sh_gate.astype(BF16), w_sh_up.astype(BF16), w_sh_down.astype(BF16)

    tp = T // MOE_PARTS
    bm = EXPERT_BLOCK
    nb = (tp * TOP_K) // bm + N_EXPERTS
    out = None
    for part in range(MOE_PARTS):
        row0 = part * tp
        idx, wts, rank, cnt = _router(h2p, rw_t[:, :HALF], rw_t[:, HALF:], bias_col, row0, tp)

        counts = cnt[:, 0].astype(jnp.int32)
        padded = (counts + bm - 1) // bm * bm
        pend = jnp.cumsum(padded)
        pstart = pend - padded
        pstart_col = jnp.broadcast_to(pstart.astype(F32).reshape(N_EXPERTS, 1), (N_EXPERTS, 128))
        dest = _slot_index(idx, rank, pstart_col)
        nused = (pend[-1] // bm).astype(jnp.int32).reshape(1)

        xs = _dispatch(h2p, dest, nb * bm, row0)
        ys = _expert_ffn((pstart // bm).astype(jnp.int32), (padded // bm).astype(jnp.int32), nused,
                         xs, w_exp_gate, w_exp_up, w_exp_down)
        yg = _collect(ys, dest)
        out = _final(yg, wts.T, h2p, x1, mod3, wsg, wsu, wsd, g_post_ffn.reshape(1, D), S, row0, out)
    return out.reshape(B, S, D)


SC_CORES = 2
SC_SUBCORES = 16
SC_WORKERS = SC_CORES * SC_SUBCORES
SC_ROWS = 64


def _sc_mesh():
    return plsc.VectorSubcoreMesh(core_axis_name="c", subcore_axis_name="s",
                                  num_cores=SC_CORES, num_subcores=SC_SUBCORES)


def _worker_id():
    return lax.axis_index("s") * SC_CORES + lax.axis_index("c")


def _dispatch(h2p, dest, n_slots, row0):
    T = dest.shape[1]
    per_w = T // SC_WORKERS
    nch = per_w // SC_ROWS
    idx = dest.reshape(TOP_K, SC_WORKERS, nch, SC_ROWS).transpose(1, 2, 0, 3)
    idx = idx.reshape(SC_WORKERS, nch * TOP_K, SC_ROWS)

    def body(x_hbm, idx_hbm, xs_hbm, idx_v, buf0, buf1, rsem0, rsem1, ssem0, ssem1):
        wid = _worker_id()
        base = row0 + wid * per_w
        pltpu.sync_copy(idx_hbm.at[wid], idx_v)
        bufs = ((buf0, rsem0, ssem0), (buf1, rsem1, ssem1))

        def read(c, buf, rsem):
            return pltpu.make_async_copy(x_hbm.at[pl.ds(base + c * SC_ROWS, SC_ROWS)], buf, rsem)

        def scatter(c, k, buf, ssem):
            return pltpu.make_async_copy(buf, xs_hbm.at[idx_v.at[c * TOP_K + k]], ssem)

        read(0, buf0, rsem0).start()

        @pl.loop(0, nch, step=2)
        def _(c0):
            for b in range(2):
                c = c0 + b
                buf, rsem, ssem = bufs[b]
                obuf, orsem, ossem = bufs[1 - b]
                read(c, buf, rsem).wait()

                @pl.when(c > 0)
                def _():
                    for k in range(TOP_K):
                        scatter(c - 1, k, obuf, ossem).wait()

                @pl.when(c + 1 < nch)
                def _():
                    read(c + 1, obuf, orsem).start()

                for k in range(TOP_K):
                    scatter(c, k, buf, ssem).start()

        for k in range(TOP_K):
            scatter(nch - 1, k, buf1, ssem1).wait()

    run = pl.kernel(
        body,
        out_type=jax.ShapeDtypeStruct((n_slots, HALF), jnp.uint32),
        mesh=_sc_mesh(),
        scratch_types=[pltpu.VMEM((nch * TOP_K, SC_ROWS), jnp.int32),
                       pltpu.VMEM((SC_ROWS, HALF), jnp.uint32),
                       pltpu.VMEM((SC_ROWS, HALF), jnp.uint32),
                       pltpu.SemaphoreType.DMA, pltpu.SemaphoreType.DMA,
                       pltpu.SemaphoreType.DMA, pltpu.SemaphoreType.DMA],
        name="sc_dispatch",
    )
    return run(h2p, idx)


def _collect(ys, dest):
    n = dest.size
    per_w = n // SC_WORKERS
    nch = per_w // SC_ROWS
    idx = dest.reshape(SC_WORKERS, nch, SC_ROWS)

    def body(ys_hbm, idx_hbm, out_hbm, idx_v, buf0, buf1, gsem0, gsem1, wsem0, wsem1):
        wid = _worker_id()
        base = wid * per_w
        pltpu.sync_copy(idx_hbm.at[wid], idx_v)
        bufs = ((buf0, gsem0, wsem0), (buf1, gsem1, wsem1))

        def gather(c, buf, gsem):
            return pltpu.make_async_copy(ys_hbm.at[idx_v.at[c]], buf, gsem)

        def write(c, buf, wsem):
            return pltpu.make_async_copy(buf, out_hbm.at[pl.ds(base + c * SC_ROWS, SC_ROWS)], wsem)

        gather(0, buf0, gsem0).start()

        @pl.loop(0, nch, step=2)
        def _(c0):
            for b in range(2):
                c = c0 + b
                buf, gsem, wsem = bufs[b]
                obuf, ogsem, owsem = bufs[1 - b]
                gather(c, buf, gsem).wait()

                @pl.when(c > 0)
                def _():
                    write(c - 1, obuf, owsem).wait()

                @pl.when(c + 1 < nch)
                def _():
                    gather(c + 1, obuf, ogsem).start()

                write(c, buf, wsem).start()

        write(nch - 1, buf1, wsem1).wait()

    run = pl.kernel(
        body,
        out_type=jax.ShapeDtypeStruct((n, HALF), jnp.uint32),
        mesh=_sc_mesh(),
        scratch_types=[pltpu.VMEM((nch, SC_ROWS), jnp.int32),
                       pltpu.VMEM((SC_ROWS, HALF), jnp.uint32),
                       pltpu.VMEM((SC_ROWS, HALF), jnp.uint32),
                       pltpu.SemaphoreType.DMA, pltpu.SemaphoreType.DMA,
                       pltpu.SemaphoreType.DMA, pltpu.SemaphoreType.DMA],
        name="sc_collect",
    )
    return run(ys, idx).reshape(dest.shape + (HALF,))


def kernel(x, c, positions, w_ada, b_ada, g_pre_mix, g_post_mix, g_pre_ffn, g_post_ffn, w_in, conv_w, conv_b, b_gates, g_mlstm, w_branch_a, w_branch_b, w_out, router_w, router_bias, w_exp_gate, w_exp_up, w_exp_down, w_sh_gate, w_sh_up, w_sh_down):
    depth = w_ada.shape[0]
    for l in range(depth):
        x = _layer(x, c, positions, w_ada[l], b_ada[l], g_pre_mix[l], g_post_mix[l], g_pre_ffn[l],
                   g_post_ffn[l], w_in[l], conv_w[l], conv_b[l], b_gates[l], g_mlstm[l],
                   w_branch_a[l], w_branch_b[l], w_out[l], router_w[l], router_bias[l],
                   w_exp_gate[l], w_exp_up[l], w_exp_down[l], w_sh_gate[l], w_sh_up[l], w_sh_down[l])
    return x
```

```python
import functools

import jax
import jax.numpy as jnp
from jax import lax
from jax.experimental import pallas as pl
from jax.experimental.pallas import tpu as pltpu
from jax.experimental.pallas import tpu_sc as plsc

F32 = jnp.float32
BF16 = jnp.bfloat16
HIGHEST = lax.Precision.HIGHEST

D_MODEL = 1024
ATT_GROUPS = ((128, 1), (512, 4), (2048, 16))
ATT_HEAD_DIM = 64
ATT_GROUP_W = 256
ATT_BLK = 128
ATT_PAIR = 2
ROPE_THETA = 500000.0
ROPE_HALF = 8
MLSTM_HEADS = 4
MLSTM_QK_DIM = 128
MLSTM_V_DIM = 256
MLSTM_BLOCK = 128
MLSTM_GROUP = 8
CONV_WIDTH = 4
N_EXPERTS = 256
TOP_K = 8
N_GROUPS = 8
TOPK_GROUPS = 4
EXPERT_FF = 256
ROUTED_SCALE = 2.5
NORM_EPS = 1e-6
NEG = -1e30

OFF_MV, OFF_MO, OFF_GA, OFF_GB = 0, 1024, 2048, 3072
OFF_MQ, OFF_MK = 4096, 4608
OFF_AQ, OFF_AK, OFF_AV = 5120, 5888, 6656
PROJ_W = 7424
HALF = D_MODEL // 2

EXPERT_BLOCK = 512
EXPERT_SLOTS = 6
MOE_PARTS = 2
MERGE_SPLIT = 2
VMEM_LIMIT = 56 * 1024 * 1024


def _nt(a, b, precision=None):
    return lax.dot_general(a, b, (((1,), (1,)), ((), ())), preferred_element_type=F32,
                           precision=precision)


def _tn(a, b):
    return lax.dot_general(a, b, (((0,), (0,)), ((), ())), preferred_element_type=F32)


_sigmoid = jax.nn.sigmoid


def _silu(x):
    return x * _sigmoid(x)


def _pack_pair(lo, hi):
    lo_b = pltpu.bitcast(lo.astype(BF16).astype(F32), jnp.uint32)
    hi_b = pltpu.bitcast(hi.astype(BF16).astype(F32), jnp.uint32)
    return (lo_b >> 16) | (hi_b & jnp.uint32(0xFFFF0000))


def _unpack_pair(w):
    lo = pltpu.bitcast(w << 16, F32)
    hi = pltpu.bitcast(w & jnp.uint32(0xFFFF0000), F32)
    return lo, hi


def _mod_kernel(c_ref, w_ref, b_ref, o_ref):
    a = _silu(c_ref[...])
    o_ref[...] = jnp.dot(a, w_ref[...], preferred_element_type=F32, precision=HIGHEST) + b_ref[...]


def _adaln(c, w_ada, b_ada):
    B = c.shape[0]
    n = w_ada.shape[1]
    tn = 512
    return pl.pallas_call(
        _mod_kernel,
        grid=(n // tn,),
        in_specs=[pl.BlockSpec((B, D_MODEL), lambda j: (0, 0)),
                  pl.BlockSpec((D_MODEL, tn), lambda j: (0, j)),
                  pl.BlockSpec((1, tn), lambda j: (0, j))],
        out_specs=pl.BlockSpec((B, tn), lambda j: (0, j)),
        out_shape=jax.ShapeDtypeStruct((B, n), F32),
        name="adaln_mod",
    )(c, w_ada, b_ada.reshape(1, n))


def _proj_kernel(x_ref, mod_ref, g_ref, w_ref, wif_ref, o_ref, gates_ref, h_ref):
    @pl.when(pl.program_id(1) == 0)
    def _():
        x = x_ref[...]
        ms = jnp.mean(x * x, axis=-1, keepdims=True)
        y = x * lax.rsqrt(ms + NORM_EPS) * g_ref[...]
        h = (y * (1.0 + mod_ref[0, 1:2, :]) + mod_ref[0, 0:1, :]).astype(BF16)
        h_ref[...] = h
        gates_ref[...] = jnp.dot(h, wif_ref[...], preferred_element_type=F32)

    o_ref[...] = jnp.dot(h_ref[...], w_ref[...], preferred_element_type=F32).astype(BF16)


def _in_proj(x2, mod3, g_pre, w_main, w_if, seq):
    T = x2.shape[0]
    tm, tn = 1024, PROJ_W // 2
    per_b = seq // tm
    return pl.pallas_call(
        _proj_kernel,
        grid=(T // tm, PROJ_W // tn),
        in_specs=[pl.BlockSpec((tm, D_MODEL), lambda i, j: (i, 0)),
                  pl.BlockSpec((1, 6, D_MODEL), lambda i, j: (i // per_b, 0, 0)),
                  pl.BlockSpec((1, D_MODEL), lambda i, j: (0, 0)),
                  pl.BlockSpec((D_MODEL, tn), lambda i, j: (0, j)),
                  pl.BlockSpec((D_MODEL, 128), lambda i, j: (0, 0))],
        out_specs=[pl.BlockSpec((tm, tn), lambda i, j: (i, j)),
                   pl.BlockSpec((tm, 128), lambda i, j: (i, 0))],
        out_shape=[jax.ShapeDtypeStruct((T, PROJ_W), BF16),
                   jax.ShapeDtypeStruct((T, 128), F32)],
        scratch_shapes=[pltpu.VMEM((tm, D_MODEL), BF16)],
        compiler_params=pltpu.CompilerParams(
            dimension_semantics=("arbitrary", "arbitrary"), vmem_limit_bytes=VMEM_LIMIT),
        name="norm_in_proj",
    )(x2, mod3, g_pre, w_main, w_if)


def _attn_kernel(q_ref, k_ref, v_ref, cs_ref, sn_ref, o_ref, qf, kf, vf, acc, m_s, l_s, *, seq):
    g = pl.program_id(1)
    lane = lax.broadcasted_iota(jnp.int32, (ATT_BLK, 128), 1)
    first = (lane % ATT_HEAD_DIM) < ROPE_HALF
    low_head = lane < ATT_HEAD_DIM

    def rope(x, cs, sn):
        partner = jnp.where(first, pltpu.roll(x, 128 - ROPE_HALF, 1), pltpu.roll(x, ROPE_HALF, 1))
        return x * cs + partner * sn

    def zero_pad(i, _):
        rows = pl.ds(pl.multiple_of(i * ATT_BLK, ATT_BLK), ATT_BLK)
        for hp in range(2):
            kf[hp, rows, :] = jnp.zeros((ATT_BLK, 128), F32)
            vf[hp, rows, :] = jnp.zeros((ATT_BLK, 128), F32)
        return 0

    lax.fori_loop(0, seq // ATT_BLK, zero_pad, 0)

    def stage(i, _):
        r = pl.multiple_of(i * ATT_BLK, ATT_BLK)
        rows = pl.ds(r, ATT_BLK)
        prow = pl.ds(pl.multiple_of(seq + i * ATT_BLK, ATT_BLK), ATT_BLK)
        cs = cs_ref[0, rows, :]
        sn = sn_ref[0, rows, :]
        for hp in range(2):
            cols = pl.ds(hp * 128, 128)
            qf[hp, rows, :] = rope(q_ref[0, rows, cols].astype(F32), cs, sn) * (ATT_HEAD_DIM ** -0.5)
            kf[hp, prow, :] = rope(k_ref[0, rows, cols].astype(F32), cs, sn)
            vf[hp, prow, :] = v_ref[0, rows, cols].astype(F32)
        return 0

    lax.fori_loop(0, seq // ATT_BLK, stage, 0)

    qi = lax.broadcasted_iota(jnp.int32, (ATT_BLK, 2 * ATT_BLK), 0)
    ki = lax.broadcasted_iota(jnp.int32, (ATT_BLK, 2 * ATT_BLK), 1)
    band = (ki >= qi) & (ki <= qi + ATT_BLK)

    def process(d, init):
        span = ATT_BLK * d
        single = seq == span

        def body(cp, _):
            blocks = [cp * ATT_PAIR + i for i in range(ATT_PAIR)]
            qrows, krows, valid = [], [], []
            for c in blocks:
                rho = c % d
                n = c // d
                qstart = rho + n * span
                if single:
                    kstart, nk = seq + qstart, ATT_BLK
                    valid.append(band[:, ATT_BLK:])
                else:
                    kstart, nk = seq + qstart - span, 2 * ATT_BLK
                    valid.append(band & (ki >= jnp.where(n > 0, 0, ATT_BLK)))
                qrows.append(pl.ds(qstart, ATT_BLK, stride=d) if d > 1 else pl.ds(qstart, ATT_BLK))
                krows.append(pl.ds(kstart, nk, stride=d) if d > 1 else pl.ds(kstart, nk))
            units = [(b, hp) for b in range(ATT_PAIR) for hp in range(2)]
            heads = [(u, hh) for u in range(len(units)) for hh in range(2)]
            q2 = [qf[hp, qrows[b], :] for b, hp in units]
            k2 = [kf[hp, krows[b], :].astype(BF16) for b, hp in units]
            v2 = [vf[hp, krows[b], :].astype(BF16) for b, hp in units]
            qh = [jnp.where(low_head if hh == 0 else jnp.logical_not(low_head), q2[u], 0.0).astype(BF16)
                  for u, hh in heads]
            s = [jnp.where(valid[units[u][0]], _nt(qh[i], k2[u]), NEG) for i, (u, hh) in enumerate(heads)]
            m = [jnp.max(x, axis=1, keepdims=True) for x in s]
            p = [jnp.exp(x - mx) for x, mx in zip(s, m)]
            l = [jnp.sum(x, axis=1, keepdims=True) for x in p]
            o = [jnp.dot(p[i].astype(BF16), v2[u], preferred_element_type=F32)
                 for i, (u, hh) in enumerate(heads)]
            for u, (b, hp) in enumerate(units):
                o_b = jnp.where(low_head, o[2 * u], o[2 * u + 1])
                m_b = jnp.where(low_head, m[2 * u], m[2 * u + 1])
                l_b = jnp.where(low_head, l[2 * u], l[2 * u + 1])
                if init:
                    acc[hp, qrows[b], :] = o_b
                    m_s[hp, qrows[b], :] = m_b
                    l_s[hp, qrows[b], :] = l_b
                else:
                    m_old = m_s[hp, qrows[b], :]
                    m_new = jnp.maximum(m_old, m_b)
                    a_old = jnp.exp(m_old - m_new)
                    a_new = jnp.exp(m_b - m_new)
                    acc[hp, qrows[b], :] = acc[hp, qrows[b], :] * a_old + o_b * a_new
                    l_s[hp, qrows[b], :] = l_s[hp, qrows[b], :] * a_old + l_b * a_new
                    m_s[hp, qrows[b], :] = m_new
            return 0

        lax.fori_loop(0, seq // (ATT_BLK * ATT_PAIR), body, 0)

    for gi, (_, d) in enumerate(ATT_GROUPS):
        @pl.when(g == gi)
        def _(d=d, gi=gi):
            process(d, gi == 0)

    @pl.when(g == len(ATT_GROUPS) - 1)
    def _():
        def fin(i, _):
            rows = pl.ds(pl.multiple_of(i * ATT_BLK, ATT_BLK), ATT_BLK)
            for hp in range(2):
                o_ref[0, rows, pl.ds(hp * 128, 128)] = (acc[hp, rows, :] / l_s[hp, rows, :]).astype(BF16)
            return 0

        lax.fori_loop(0, seq // ATT_BLK, fin, 0)


def _attention(proj3, cs, sn):
    B, S, _ = proj3.shape
    ng = len(ATT_GROUPS)
    qb, kb, vb = OFF_AQ // ATT_GROUP_W, OFF_AK // ATT_GROUP_W, OFF_AV // ATT_GROUP_W
    return pl.pallas_call(
        functools.partial(_attn_kernel, seq=S),
        grid=(B, ng),
        in_specs=[pl.BlockSpec((1, S, ATT_GROUP_W), lambda b, g: (b, 0, qb + g)),
                  pl.BlockSpec((1, S, ATT_GROUP_W), lambda b, g: (b, 0, kb + g)),
                  pl.BlockSpec((1, S, ATT_GROUP_W), lambda b, g: (b, 0, vb + g)),
                  pl.BlockSpec((1, S, 128), lambda b, g: (b, 0, 0)),
                  pl.BlockSpec((1, S, 128), lambda b, g: (b, 0, 0))],
        out_specs=pl.BlockSpec((1, S, ATT_GROUP_W), lambda b, g: (b, 0, 0)),
        out_shape=jax.ShapeDtypeStruct((B, S, ATT_GROUP_W), BF16),
        scratch_shapes=[pltpu.VMEM((2, S, 128), F32),
                        pltpu.VMEM((2, 2 * S, 128), F32),
                        pltpu.VMEM((2, 2 * S, 128), F32),
                        pltpu.VMEM((2, S, 128), F32),
                        pltpu.VMEM((2, S, 128), F32),
                        pltpu.VMEM((2, S, 128), F32)],
        compiler_params=pltpu.CompilerParams(
            dimension_semantics=("arbitrary", "arbitrary"), vmem_limit_bytes=VMEM_LIMIT),
        name="dilated_attention",
    )(proj3, proj3, proj3, cs, sn)


def _log_sigmoid(x):
    return jnp.minimum(x, 0.0) - jnp.log(1.0 + jnp.exp(-jnp.abs(x)))


def _mlstm_kernel(mq_ref, mk_ref, mv_ref, mo_ref, gt_ref, cwq_ref, cwk_ref, cbq_ref, cbk_ref,
                  bg_ref, gm_ref, o_ref, q_s, k_s, va_s, rows_s, acc_s, kv_s, inter_s, emt_s,
                  c_s, *, seq):
    h = pl.program_id(1)
    L = MLSTM_BLOCK
    NC = seq // L
    DK, DV = MLSTM_QK_DIM, MLSTM_V_DIM
    DA = DV + 128
    nshift = CONV_WIDTH - 1

    tt = lax.broadcasted_iota(jnp.int32, (nshift * L, 2 * L), 0)
    uu = lax.broadcasted_iota(jnp.int32, (nshift * L, 2 * L), 1)
    shift_mat = (uu == L + tt % L - (tt // L + 1)).astype(BF16)
    conv_w = jnp.concatenate([cwq_ref[...], cwk_ref[...]], axis=1)
    conv_b = jnp.concatenate([cbq_ref[...], cbk_ref[...]], axis=1)
    prev = jnp.zeros((L, 2 * DK), BF16)
    for i in range(NC):
        blk = slice(i * L, (i + 1) * L)
        va_s[blk, 0:DV] = mv_ref[0, blk, :]
        va_s[blk, DV:DA] = jnp.ones((L, DA - DV), BF16)
        cur = jnp.concatenate([mq_ref[0, blk, :], mk_ref[0, blk, :]], axis=1)
        shifted = jnp.dot(shift_mat, jnp.concatenate([prev, cur], axis=0),
                          preferred_element_type=F32)
        y = conv_b + cur.astype(F32) * conv_w[nshift:nshift + 1, :]
        for s in range(nshift):
            y = y + shifted[s * L:(s + 1) * L, :] * conv_w[nshift - 1 - s:nshift - s, :]
        y = _silu(y)
        q_s[blk, :] = y[:, 0:DK].astype(BF16)
        k_s[blk, :] = (y[:, DK:2 * DK] * (DK ** -0.5)).astype(BF16)
        prev = cur

    lane = lax.broadcasted_iota(jnp.int32, (1, 128), 1)
    bias = bg_ref[...]
    b_i = jnp.sum(jnp.where(lane == h, bias, 0.0), axis=1, keepdims=True)
    b_f = jnp.sum(jnp.where(lane == h + MLSTM_HEADS, bias, 0.0), axis=1, keepdims=True)
    ri = lax.broadcasted_iota(jnp.int32, (L, L), 0)
    ci = lax.broadcasted_iota(jnp.int32, (L, L), 1)
    causal = ci <= ri
    eye = (ri == ci).astype(F32)
    i_rows = gt_ref[0, h] + b_i
    lf_rows = _log_sigmoid(gt_ref[0, h + MLSTM_HEADS] + b_f)
    b_rows = jnp.dot(lf_rows, (ri <= ci).astype(F32), preferred_element_type=F32,
                     precision=HIGHEST)
    b_end = b_rows[:, L - 1:L]
    g_rows = b_end - b_rows + i_rows
    g_max = jnp.max(g_rows, axis=1, keepdims=True)
    m = jnp.zeros((1, 1), F32)
    m_prev, m_new = [], []
    for c in range(NC):
        m_prev.append(m)
        m = jnp.maximum(b_end[c:c + 1, :] + m, g_max[c:c + 1, :])
        m_new.append(m)
    m_prev = jnp.concatenate(m_prev, axis=0)
    m_new = jnp.concatenate(m_new, axis=0)
    rows_s[0] = b_rows
    rows_s[1] = jnp.exp(g_rows - m_new)
    rows_s[2] = b_rows - i_rows
    rows_s[3] = jnp.broadcast_to(m_prev, (NC, L))
    rows_s[4] = jnp.broadcast_to(jnp.exp(b_end + m_prev - m_new), (NC, L))

    r2 = lax.broadcasted_iota(jnp.int32, (2 * L, 2 * L), 0)
    c2 = lax.broadcasted_iota(jnp.int32, (2 * L, 2 * L), 1)
    ones_blk = ((r2 < L) == (c2 < L)).astype(BF16)

    G = MLSTM_GROUP

    def local(cg, _):
        cs = [cg * G + i for i in range(G)]
        rows = [pl.ds(pl.multiple_of(c * L, L), L) for c in cs]
        b_r = [rows_s[0, pl.ds(c, 1), :] for c in cs]
        w_r = [rows_s[1, pl.ds(c, 1), :] for c in cs]
        u_r = [rows_s[2, pl.ds(c, 1), :] for c in cs]
        mp = [rows_s[3, pl.ds(c, 1), :] for c in cs]
        q = [q_s[r, :] for r in rows]
        k = [k_s[r, :] for r in rows]
        va = [va_s[r, :] for r in rows]
        qk = [_nt(a, b) for a, b in zip(q, k)]
        x2 = [jnp.concatenate([eye * a, eye * b], axis=1) for a, b in zip(b_r, w_r)]
        hi = [x.astype(BF16) for x in x2]
        lo = [(x - h_.astype(F32)).astype(BF16) for x, h_ in zip(x2, hi)]
        yb = [jnp.dot(h_, ones_blk, preferred_element_type=F32)
              + jnp.dot(l_, ones_blk, preferred_element_type=F32) for h_, l_ in zip(hi, lo)]
        b_b = [y[:, 0:L] for y in yb]
        w_b = [y[:, L:2 * L] for y in yb]
        for i in range(G):
            kv_s[cs[i]] = _tn((w_b[i] * k[i].astype(F32)).astype(BF16), va[i])
        dmat = [jnp.where(causal, b - u, NEG) for b, u in zip(b_b, u_r)]
        m_t = [jnp.maximum(b + m_, jnp.max(d, axis=1, keepdims=True))
               for b, m_, d in zip(b_b, mp, dmat)]
        sc = [a * jnp.exp(d - m_) for a, d, m_ in zip(qk, dmat, m_t)]
        for i in range(G):
            acc_s[rows[i], :] = jnp.dot(sc[i].astype(BF16), va[i], preferred_element_type=F32)
            inter_s[rows[i], :] = jnp.exp(b_b[i] + mp[i] - m_t[i])
            emt_s[rows[i], :] = jnp.exp(-m_t[i])
        return 0

    lax.fori_loop(0, NC // G, local, 0)

    g_row = gm_ref[...]
    c_s[...] = jnp.zeros((DK, DA), F32)

    def recur(cg, _):
        cs = [cg * G + i for i in range(G)]
        rows = [pl.ds(pl.multiple_of(c * L, L), L) for c in cs]
        states = [c_s[...]]
        for c in cs:
            dec = rows_s[4, pl.ds(c, 1), :]
            states.append(jnp.concatenate([dec, dec, dec], axis=1) * states[-1] + kv_s[c])
        c_s[...] = states[G]
        read = [jnp.dot(q_s[r, :], st.astype(BF16), preferred_element_type=F32)
                for r, st in zip(rows, states)]
        inter = [inter_s[r, :] for r in rows]
        out = [acc_s[r, :] + jnp.concatenate([it, it, it], axis=1) * rd
               for r, it, rd in zip(rows, inter, read)]
        emt = [emt_s[r, :] for r in rows]
        nrm = [jnp.maximum(jnp.abs(jnp.concatenate([o[:, DV:DA], o[:, DV:DA]], axis=1)),
                           jnp.concatenate([e_, e_], axis=1)) for o, e_ in zip(out, emt)]
        hh = [o[:, 0:DV] / n_ for o, n_ in zip(out, nrm)]
        ms = [jnp.mean(x * x, axis=1, keepdims=True) for x in hh]
        hn = [x * lax.rsqrt(m_ + NORM_EPS) * g_row for x, m_ in zip(hh, ms)]
        for i in range(G):
            o_ref[0, rows[i], :] = (hn[i] * _sigmoid(mo_ref[0, rows[i], :].astype(F32))).astype(BF16)
        return 0

    lax.fori_loop(0, NC // G, recur, 0)


def _mlstm(proj3, gates_t, conv_w, conv_b, bg_row, g_mlstm):
    B, S, _ = proj3.shape
    H, DK, DV = MLSTM_HEADS, MLSTM_QK_DIM, MLSTM_V_DIM
    L = MLSTM_BLOCK
    NC = S // L
    DA = DV + 128
    qb, kb = OFF_MQ // DK, OFF_MK // DK
    vb, ob = OFF_MV // DV, OFF_MO // DV
    nq = (H * DK) // DK
    return pl.pallas_call(
        functools.partial(_mlstm_kernel, seq=S),
        grid=(B, H),
        in_specs=[pl.BlockSpec((1, S, DK), lambda b, h: (b, 0, qb + h)),
                  pl.BlockSpec((1, S, DK), lambda b, h: (b, 0, kb + h)),
                  pl.BlockSpec((1, S, DV), lambda b, h: (b, 0, vb + h)),
                  pl.BlockSpec((1, S, DV), lambda b, h: (b, 0, ob + h)),
                  pl.BlockSpec((1, 2 * H, NC, L), lambda b, h: (b, 0, 0, 0)),
                  pl.BlockSpec((CONV_WIDTH, DK), lambda b, h: (0, h)),
                  pl.BlockSpec((CONV_WIDTH, DK), lambda b, h: (0, nq + h)),
                  pl.BlockSpec((1, DK), lambda b, h: (0, h)),
                  pl.BlockSpec((1, DK), lambda b, h: (0, nq + h)),
                  pl.BlockSpec((1, 128), lambda b, h: (0, 0)),
                  pl.BlockSpec((1, DV), lambda b, h: (0, h))],
        out_specs=pl.BlockSpec((1, S, DV), lambda b, h: (b, 0, h)),
        out_shape=jax.ShapeDtypeStruct((B, S, H * DV), BF16),
        scratch_shapes=[pltpu.VMEM((S, DK), BF16),
                        pltpu.VMEM((S, DK), BF16),
                        pltpu.VMEM((S, DA), BF16),
                        pltpu.VMEM((5, NC, L), F32),
                        pltpu.VMEM((S, DA), F32),
                        pltpu.VMEM((NC, DK, DA), F32),
                        pltpu.VMEM((S, L), F32),
                        pltpu.VMEM((S, L), F32),
                        pltpu.VMEM((DK, DA), F32)],
        compiler_params=pltpu.CompilerParams(
            dimension_semantics=("arbitrary", "arbitrary"), vmem_limit_bytes=VMEM_LIMIT),
        name="mlstm_chunkwise",
    )(proj3, proj3, proj3, proj3, gates_t, conv_w, conv_w, conv_b, conv_b, bg_row, g_mlstm)


def _rms(y, g):
    ms = jnp.mean(y * y, axis=-1, keepdims=True)
    return y * lax.rsqrt(ms + NORM_EPS) * g


def _merge_kernel(ya_ref, yb_ref, ga_ref, gb_ref, x_ref, mod_ref, wa_ref, wb_ref, wo_ref,
                  gpost_ref, gpre_ref, x1_ref, h2_ref):
    tm = x_ref.shape[0]
    slabs = [pl.ds(s * (tm // MERGE_SPLIT), tm // MERGE_SPLIT) for s in range(MERGE_SPLIT)]
    pa = [jnp.dot(ya_ref[r, :], wa_ref[...], preferred_element_type=F32) for r in slabs]
    pb = [jnp.dot(yb_ref[r, :], wb_ref[...], preferred_element_type=F32) for r in slabs]
    merged = [_sigmoid(ga_ref[r, :].astype(F32)) * a + _sigmoid(gb_ref[r, :].astype(F32)) * b
              for r, a, b in zip(slabs, pa, pb)]
    y = [jnp.dot(m.astype(BF16), wo_ref[...], preferred_element_type=F32) for m in merged]
    x1 = [x_ref[r, :] + mod_ref[0, 2:3, :] * _rms(v, gpost_ref[...]) for r, v in zip(slabs, y)]
    for r, v in zip(slabs, x1):
        x1_ref[r, :] = v
    h2 = [_rms(v, gpre_ref[...]) * (1.0 + mod_ref[0, 4:5, :]) + mod_ref[0, 3:4, :] for v in x1]
    for r, v in zip(slabs, h2):
        h2_ref[r, :] = _pack_pair(v[:, :HALF], v[:, HALF:])


def _merge(ya2, yb2, proj2, x2, mod3, wa, wb, wo, g_post, g_pre, seq):
    T = x2.shape[0]
    tm = 512 * MERGE_SPLIT
    per_b = seq // tm
    full = lambda shape: pl.BlockSpec(shape, lambda i: (0,) * len(shape))
    return pl.pallas_call(
        _merge_kernel,
        grid=(T // tm,),
        in_specs=[pl.BlockSpec((tm, ATT_GROUP_W), lambda i: (i, 0)),
                  pl.BlockSpec((tm, D_MODEL), lambda i: (i, 0)),
                  pl.BlockSpec((tm, D_MODEL), lambda i: (i, OFF_GA // D_MODEL)),
                  pl.BlockSpec((tm, D_MODEL), lambda i: (i, OFF_GB // D_MODEL)),
                  pl.BlockSpec((tm, D_MODEL), lambda i: (i, 0)),
                  pl.BlockSpec((1, 6, D_MODEL), lambda i: (i // per_b, 0, 0)),
                  full((ATT_GROUP_W, D_MODEL)), full((D_MODEL, D_MODEL)), full((D_MODEL, D_MODEL)),
                  full((1, D_MODEL)), full((1, D_MODEL))],
        out_specs=[pl.BlockSpec((tm, D_MODEL), lambda i: (i, 0)),
                   pl.BlockSpec((tm, HALF), lambda i: (i, 0))],
        out_shape=[jax.ShapeDtypeStruct((T, D_MODEL), F32),
                   jax.ShapeDtypeStruct((T, HALF), jnp.uint32)],
        compiler_params=pltpu.CompilerParams(
            dimension_semantics=("arbitrary",), vmem_limit_bytes=VMEM_LIMIT),
        name="merge_out_proj",
    )(ya2, yb2, proj2, proj2, x2, mod3, wa, wb, wo, g_post, g_pre)


def _router_kernel(h2_ref, rlo_ref, rhi_ref, bias_ref, idx_ref, w_ref, rank_ref, cnt_ref):
    E = N_EXPERTS
    tr = h2_ref.shape[0]
    gsz = E // N_GROUPS

    @pl.when(pl.program_id(0) == 0)
    def _():
        cnt_ref[...] = jnp.zeros(cnt_ref.shape, F32)

    lo, hi = _unpack_pair(h2_ref[...])
    logits = _nt(rlo_ref[...], lo.astype(BF16)) + _nt(rhi_ref[...], hi.astype(BF16))
    scores = _sigmoid(logits)
    sel = scores + bias_ref[:, 0:1]

    gi = lax.broadcasted_iota(jnp.int32, (gsz, tr), 0).astype(F32)
    gs_rows = []
    for g in range(N_GROUPS):
        blk = sel[g * gsz:(g + 1) * gsz, :]
        m1 = jnp.max(blk, axis=0, keepdims=True)
        a1 = jnp.min(jnp.where(blk == m1, gi, float(E)), axis=0, keepdims=True)
        m2 = jnp.max(jnp.where(gi == a1, -jnp.inf, blk), axis=0, keepdims=True)
        gs_rows.append(m1 + m2)
    gs = jnp.concatenate(gs_rows, axis=0)
    g8 = lax.broadcasted_iota(jnp.int32, (N_GROUPS, tr), 0).astype(F32)
    gmask = jnp.zeros((N_GROUPS, tr), F32)
    for _ in range(TOPK_GROUPS):
        m = jnp.max(gs, axis=0, keepdims=True)
        a = jnp.min(jnp.where(gs == m, g8, float(E)), axis=0, keepdims=True)
        hit = g8 == a
        gmask = jnp.where(hit, 1.0, gmask)
        gs = jnp.where(hit, -jnp.inf, gs)
    selm = jnp.concatenate(
        [jnp.where(gmask[g:g + 1, :] > 0.0, sel[g * gsz:(g + 1) * gsz, :], -jnp.inf)
         for g in range(N_GROUPS)], axis=0)

    ei = lax.broadcasted_iota(jnp.int32, (E, tr), 0).astype(F32)
    picks, weights, hits = [], [], []
    candidates = selm
    for _ in range(TOP_K):
        m = jnp.max(selm, axis=0, keepdims=True)
        a = jnp.min(jnp.where(selm == m, ei, float(E)), axis=0, keepdims=True)
        hit = ei == a
        picks.append(a)
        hits.append(hit)
        weights.append(jnp.sum(jnp.where(hit, scores, 0.0), axis=0, keepdims=True))
        selm = jnp.where(hit, -jnp.inf, selm)
    chosen = jnp.where(selm != candidates, 1.0, 0.0)
    wsum = weights[0]
    for w in weights[1:]:
        wsum = wsum + w

    ti = lax.broadcasted_iota(jnp.int32, (tr, tr), 0)
    tj = lax.broadcasted_iota(jnp.int32, (tr, tr), 1)
    before = (ti < tj).astype(BF16)
    pos = jnp.dot(chosen.astype(BF16), before, preferred_element_type=F32) + cnt_ref[:, 0:1]
    ranks = [jnp.sum(jnp.where(hit, pos, 0.0), axis=0, keepdims=True) for hit in hits]
    cnt_ref[...] = cnt_ref[...] + jnp.sum(chosen, axis=1, keepdims=True)

    idx_ref[...] = jnp.concatenate(picks, axis=0).astype(jnp.int32)
    w_ref[...] = jnp.concatenate([w / wsum * ROUTED_SCALE for w in weights], axis=0)
    rank_ref[...] = jnp.concatenate(ranks, axis=0).astype(jnp.int32)


def _router(h2p, r_lo, r_hi, bias_col, row0, T):
    tr = 512
    off = row0 // tr
    full = lambda shape: pl.BlockSpec(shape, lambda i: (0,) * len(shape))
    return pl.pallas_call(
        _router_kernel,
        grid=(T // tr,),
        in_specs=[pl.BlockSpec((tr, HALF), lambda i: (i + off, 0)),
                  full((N_EXPERTS, HALF)), full((N_EXPERTS, HALF)), full((N_EXPERTS, 128))],
        out_specs=[pl.BlockSpec((TOP_K, tr), lambda i: (0, i)),
                   pl.BlockSpec((TOP_K, tr), lambda i: (0, i)),
                   pl.BlockSpec((TOP_K, tr), lambda i: (0, i)),
                   full((N_EXPERTS, 128))],
        out_shape=[jax.ShapeDtypeStruct((TOP_K, T), jnp.int32),
                   jax.ShapeDtypeStruct((TOP_K, T), F32),
                   jax.ShapeDtypeStruct((TOP_K, T), jnp.int32),
                   jax.ShapeDtypeStruct((N_EXPERTS, 128), F32)],
        compiler_params=pltpu.CompilerParams(
            dimension_semantics=("arbitrary",), vmem_limit_bytes=VMEM_LIMIT),
        name="router_topk",
    )(h2p, r_lo, r_hi, bias_col)


def _dest_kernel(idx_ref, rank_ref, pstart_ref, dest_ref):
    tr = idx_ref.shape[1]
    ei = lax.broadcasted_iota(jnp.int32, (N_EXPERTS, tr), 0)
    start = pstart_ref[:, 0:1]
    rows = []
    for k in range(TOP_K):
        hit = ei == idx_ref[k:k + 1, :]
        rows.append(jnp.sum(jnp.where(hit, start, 0.0), axis=0, keepdims=True))
    dest_ref[...] = jnp.concatenate(rows, axis=0).astype(jnp.int32) + rank_ref[...]


def _slot_index(idx, rank, pstart_col):
    T = idx.shape[1]
    tr = 1024
    return pl.pallas_call(
        _dest_kernel,
        grid=(T // tr,),
        in_specs=[pl.BlockSpec((TOP_K, tr), lambda i: (0, i)),
                  pl.BlockSpec((TOP_K, tr), lambda i: (0, i)),
                  pl.BlockSpec((N_EXPERTS, 128), lambda i: (0, 0))],
        out_specs=pl.BlockSpec((TOP_K, tr), lambda i: (0, i)),
        out_shape=jax.ShapeDtypeStruct((TOP_K, T), jnp.int32),
        name="slot_index",
    )(idx, rank, pstart_col)


def _ffn_kernel(first_ref, nblk_ref, nused_ref, xs_hbm, wg_ref, wu_ref, wd_ref, ys_hbm,
                xbuf, ybuf, in_sem, out_sem, wg_s, wu_s, wd_s):
    e = pl.program_id(0)
    bm = EXPERT_BLOCK
    ns = EXPERT_SLOTS
    nused = nused_ref[0]
    first = first_ref[e]
    n = nblk_ref[e]

    def in_copy(g):
        slot = g % ns
        return pltpu.make_async_copy(xs_hbm.at[pl.ds(g * bm, bm)], xbuf.at[slot], in_sem.at[slot])

    def out_copy(g):
        slot = g % ns
        return pltpu.make_async_copy(ybuf.at[slot], ys_hbm.at[pl.ds(g * bm, bm)], out_sem.at[slot])

    def fetch(g):
        @pl.when(g < nused)
        def _():
            in_copy(g).start()

    def release(g):
        @pl.when(g >= ns)
        def _():
            out_copy(g - ns).wait()

    def ffn(g):
        lo, hi = _unpack_pair(xbuf[g % ns])
        x = jnp.concatenate([lo.astype(BF16), hi.astype(BF16)], axis=1)
        gate = jnp.dot(x, wg_s[...], preferred_element_type=F32)
        up = jnp.dot(x, wu_s[...], preferred_element_type=F32)
        hid = (_silu(gate) * up).astype(BF16)
        return jnp.dot(hid, wd_s[...], preferred_element_type=F32)

    def pack(g, out):
        ybuf[g % ns] = _pack_pair(out[:, :HALF], out[:, HALF:])

    @pl.when(e == 0)
    def _():
        for q in range(ns - 1):
            fetch(q)

    @pl.when(n > 0)
    def _():
        wg_s[...] = wg_ref[0].astype(BF16)
        wu_s[...] = wu_ref[0].astype(BF16)
        wd_s[...] = wd_ref[0].astype(BF16)

        def two_blocks(j, _):
            g = first + 2 * j
            in_copy(g).wait()
            in_copy(g + 1).wait()
            fetch(g + ns - 1)
            release(g)
            release(g + 1)
            out_a = ffn(g)
            out_b = ffn(g + 1)
            pack(g, out_a)
            pack(g + 1, out_b)
            out_copy(g).start()
            out_copy(g + 1).start()
            fetch(g + ns)
            return 0

        lax.fori_loop(0, n // 2, two_blocks, 0)

        @pl.when(n % 2 == 1)
        def _():
            g = first + n - 1
            in_copy(g).wait()
            fetch(g + ns - 1)
            release(g)
            pack(g, ffn(g))
            out_copy(g).start()

    @pl.when(e == pl.num_programs(0) - 1)
    def _():
        for q in range(ns, 0, -1):
            @pl.when(nused >= q)
            def _(q=q):
                out_copy(nused - q).wait()


def _expert_ffn(first_blk, nblk, nused, xs, w_gate, w_up, w_down):
    P = xs.shape[0]
    bm = EXPERT_BLOCK
    w_map = lambda e, *_: (e, 0, 0)
    grid_spec = pltpu.PrefetchScalarGridSpec(
        num_scalar_prefetch=3,
        grid=(w_gate.shape[0],),
        in_specs=[pl.BlockSpec(memory_space=pl.ANY),
                  pl.BlockSpec((1, D_MODEL, EXPERT_FF), w_map),
                  pl.BlockSpec((1, D_MODEL, EXPERT_FF), w_map),
                  pl.BlockSpec((1, EXPERT_FF, D_MODEL), w_map)],
        out_specs=pl.BlockSpec(memory_space=pl.ANY),
        scratch_shapes=[pltpu.VMEM((EXPERT_SLOTS, bm, HALF), jnp.uint32),
                        pltpu.VMEM((EXPERT_SLOTS, bm, HALF), jnp.uint32),
                        pltpu.SemaphoreType.DMA((EXPERT_SLOTS,)),
                        pltpu.SemaphoreType.DMA((EXPERT_SLOTS,)),
                        pltpu.VMEM((D_MODEL, EXPERT_FF), BF16),
                        pltpu.VMEM((D_MODEL, EXPERT_FF), BF16),
                        pltpu.VMEM((EXPERT_FF, D_MODEL), BF16)],
    )
    return pl.pallas_call(
        _ffn_kernel,
        grid_spec=grid_spec,
        out_shape=jax.ShapeDtypeStruct((P, HALF), jnp.uint32),
        compiler_params=pltpu.CompilerParams(
            dimension_semantics=("arbitrary",), vmem_limit_bytes=VMEM_LIMIT),
        name="routed_experts",
    )(first_blk, nblk, nused, xs, w_gate, w_up, w_down)


def _final_kernel(yg_ref, w_ref, h2_ref, x1_ref, mod_ref, wsg_ref, wsu_ref, wsd_ref, gpost_ref, *rest):
    o_ref = rest[-1]
    lo, hi = _unpack_pair(h2_ref[...])
    h2 = jnp.concatenate([lo.astype(BF16), hi.astype(BF16)], axis=1)
    gate = jnp.dot(h2, wsg_ref[...], preferred_element_type=F32)
    up = jnp.dot(h2, wsu_ref[...], preferred_element_type=F32)
    shared = jnp.dot((_silu(gate) * up).astype(BF16), wsd_ref[...], preferred_element_type=F32)
    y_lo = shared[:, :HALF]
    y_hi = shared[:, HALF:]
    for k in range(TOP_K):
        r_lo, r_hi = _unpack_pair(yg_ref[k])
        wk = w_ref[:, k:k + 1]
        y_lo = y_lo + wk * r_lo
        y_hi = y_hi + wk * r_hi
    ms = (jnp.sum(y_lo * y_lo, axis=-1, keepdims=True)
          + jnp.sum(y_hi * y_hi, axis=-1, keepdims=True)) * (1.0 / D_MODEL)
    inv = lax.rsqrt(ms + NORM_EPS)
    o_ref[:, 0:HALF] = x1_ref[:, 0:HALF] + mod_ref[0, 5:6, 0:HALF] * (y_lo * inv * gpost_ref[:, 0:HALF])
    o_ref[:, HALF:] = x1_ref[:, HALF:] + mod_ref[0, 5:6, HALF:] * (y_hi * inv * gpost_ref[:, HALF:])


def _final(yg, w_tk, h2p, x1, mod3, wsg, wsu, wsd, g_post, seq, row0, out_prev):
    T = x1.shape[0]
    tp = yg.shape[1]
    tm = 512
    per_b = seq // tm
    off = row0 // tm
    full = lambda shape: pl.BlockSpec(shape, lambda i: (0,) * len(shape))
    in_specs = [pl.BlockSpec((TOP_K, tm, HALF), lambda i: (0, i, 0)),
                pl.BlockSpec((tm, TOP_K), lambda i: (i, 0)),
                pl.BlockSpec((tm, HALF), lambda i: (i + off, 0)),
                pl.BlockSpec((tm, D_MODEL), lambda i: (i + off, 0)),
                pl.BlockSpec((1, 6, D_MODEL), lambda i: ((i + off) // per_b, 0, 0)),
                full((D_MODEL, EXPERT_FF)), full((D_MODEL, EXPERT_FF)), full((EXPERT_FF, D_MODEL)),
                full((1, D_MODEL))]
    args = [yg, w_tk, h2p, x1, mod3, wsg, wsu, wsd, g_post]
    aliases = {}
    if out_prev is not None:
        in_specs.append(pl.BlockSpec(memory_space=pl.ANY))
        args.append(out_prev)
        aliases = {len(args) - 1: 0}
    return pl.pallas_call(
        _final_kernel,
        grid=(tp // tm,),
        in_specs=in_specs,
        out_specs=pl.BlockSpec((tm, D_MODEL), lambda i: (i + off, 0)),
        out_shape=jax.ShapeDtypeStruct((T, D_MODEL), F32),
        input_output_aliases=aliases,
        compiler_params=pltpu.CompilerParams(
            dimension_semantics=("arbitrary",), vmem_limit_bytes=VMEM_LIMIT),
        name="shared_expert_combine",
    )(*args)


def _rope_tables(positions):
    inv = jnp.power(ROPE_THETA, -jnp.arange(ROPE_HALF, dtype=F32) / ROPE_HALF)
    ang = positions.astype(F32)[..., None] * inv
    cos, sin = jnp.cos(ang), jnp.sin(ang)
    rest = ATT_HEAD_DIM - 2 * ROPE_HALF
    cs = jnp.concatenate([cos, cos, jnp.ones(ang.shape[:-1] + (rest,), F32)], axis=-1)
    sn = jnp.concatenate([-sin, sin, jnp.zeros(ang.shape[:-1] + (rest,), F32)], axis=-1)
    return jnp.tile(cs, (1, 1, 2)), jnp.tile(sn, (1, 1, 2))


def _layer(x, c, positions, w_ada, b_ada, g_pre_mix, g_post_mix, g_pre_ffn, g_post_ffn,
           w_in, conv_w, conv_b, b_gates, g_mlstm, w_branch_a, w_branch_b, w_out,
           router_w, router_bias, w_exp_gate, w_exp_up, w_exp_down, w_sh_gate, w_sh_up, w_sh_down):
    B, S, D = x.shape
    T = B * S
    H = MLSTM_HEADS
    x2 = x.reshape(T, D)

    mod3 = _adaln(c, w_ada, b_ada).reshape(B, 6, D)

    a_w = 3 * ATT_GROUP_W
    o_mq = 3 * a_w
    o_mk = o_mq + H * MLSTM_QK_DIM
    o_mv = o_mk + H * MLSTM_QK_DIM
    o_mo = o_mv + H * MLSTM_V_DIM
    o_mi = o_mo + H * MLSTM_V_DIM
    o_ga = o_mi + 2 * H
    o_gb = o_ga + D
    w_bf = w_in.astype(BF16)
    w_main = jnp.concatenate(
        [w_bf[:, o_mv:o_mi], w_bf[:, o_ga:o_gb + D], w_bf[:, o_mq:o_mv], w_bf[:, 0:o_mq]], axis=1)
    w_if = jnp.pad(w_bf[:, o_mi:o_ga], ((0, 0), (0, 128 - 2 * H)))

    proj, gates = _in_proj(x2, mod3, g_pre_mix.reshape(1, D), w_main, w_if, S)
    proj3 = proj.reshape(B, S, PROJ_W)

    cs, sn = _rope_tables(positions)
    y_a = _attention(proj3, cs, sn)

    bg_row = jnp.pad(b_gates.reshape(1, 2 * H), ((0, 0), (0, 128 - 2 * H)))
    gates_t = gates[:, :2 * H].reshape(B, S, 2 * H).transpose(0, 2, 1)
    gates_t = gates_t.reshape(B, 2 * H, S // MLSTM_BLOCK, MLSTM_BLOCK)
    y_b = _mlstm(proj3, gates_t, conv_w, conv_b.reshape(1, -1), bg_row, g_mlstm.reshape(1, -1))

    x1, h2p = _merge(y_a.reshape(T, ATT_GROUP_W), y_b.reshape(T, D), proj, x2, mod3,
                     w_branch_a.astype(BF16), w_branch_b.astype(BF16), w_out.astype(BF16),
                     g_post_mix.reshape(1, D), g_pre_ffn.reshape(1, D), S)

    rw_t = router_w.T.astype(BF16)
    bias_col = jnp.broadcast_to(router_bias.reshape(N_EXPERTS, 1), (N_EXPERTS, 128))
    wsg, wsu, wsd = w_sh_gate.astype(BF16), w_sh_up.astype(BF16), w_sh_down.astype(BF16)

    tp = T // MOE_PARTS
    bm = EXPERT_BLOCK
    nb = (tp * TOP_K) // bm + N_EXPERTS
    out = None
    for part in range(MOE_PARTS):
        row0 = part * tp
        idx, wts, rank, cnt = _router(h2p, rw_t[:, :HALF], rw_t[:, HALF:], bias_col, row0, tp)

        counts = cnt[:, 0].astype(jnp.int32)
        padded = (counts + bm - 1) // bm * bm
        pend = jnp.cumsum(padded)
        pstart = pend - padded
        pstart_col = jnp.broadcast_to(pstart.astype(F32).reshape(N_EXPERTS, 1), (N_EXPERTS, 128))
        dest = _slot_index(idx, rank, pstart_col)
        nused = (pend[-1] // bm).astype(jnp.int32).reshape(1)

        xs = _dispatch(h2p, dest, nb * bm, row0)
        ys = _expert_ffn((pstart // bm).astype(jnp.int32), (padded // bm).astype(jnp.int32), nused,
                         xs, w_exp_gate, w_exp_up, w_exp_down)
        yg = _collect(ys, dest)
        out = _final(yg, wts.T, h2p, x1, mod3, wsg, wsu, wsd, g_post_ffn.reshape(1, D), S, row0, out)
    return out.reshape(B, S, D)


SC_CORES = 2
SC_SUBCORES = 16
SC_WORKERS = SC_CORES * SC_SUBCORES
SC_ROWS = 64


def _sc_mesh():
    return plsc.VectorSubcoreMesh(core_axis_name="c", subcore_axis_name="s",
                                  num_cores=SC_CORES, num_subcores=SC_SUBCORES)


def _worker_id():
    return lax.axis_index("s") * SC_CORES + lax.axis_index("c")


def _dispatch(h2p, dest, n_slots, row0):
    T = dest.shape[1]
    per_w = T // SC_WORKERS
    nch = per_w // SC_ROWS
    idx = dest.reshape(TOP_K, SC_WORKERS, nch, SC_ROWS).transpose(1, 2, 0, 3)
    idx = idx.reshape(SC_WORKERS, nch * TOP_K, SC_ROWS)

    def body(x_hbm, idx_hbm, xs_hbm, idx_v, buf0, buf1, rsem0, rsem1, ssem0, ssem1):
        wid = _worker_id()
        base = row0 + wid * per_w
        pltpu.sync_copy(idx_hbm.at[wid], idx_v)
        bufs = ((buf0, rsem0, ssem0), (buf1, rsem1, ssem1))

        def read(c, buf, rsem):
            return pltpu.make_async_copy(x_hbm.at[pl.ds(base + c * SC_ROWS, SC_ROWS)], buf, rsem)

        def scatter(c, k, buf, ssem):
            return pltpu.make_async_copy(buf, xs_hbm.at[idx_v.at[c * TOP_K + k]], ssem)

        read(0, buf0, rsem0).start()

        @pl.loop(0, nch, step=2)
        def _(c0):
            for b in range(2):
                c = c0 + b
                buf, rsem, ssem = bufs[b]
                obuf, orsem, ossem = bufs[1 - b]
                read(c, buf, rsem).wait()

                @pl.when(c > 0)
                def _():
                    for k in range(TOP_K):
                        scatter(c - 1, k, obuf, ossem).wait()

                @pl.when(c + 1 < nch)
                def _():
                    read(c + 1, obuf, orsem).start()

                for k in range(TOP_K):
                    scatter(c, k, buf, ssem).start()

        for k in range(TOP_K):
            scatter(nch - 1, k, buf1, ssem1).wait()

    run = pl.kernel(
        body,
        out_type=jax.ShapeDtypeStruct((n_slots, HALF), jnp.uint32),
        mesh=_sc_mesh(),
        scratch_types=[pltpu.VMEM((nch * TOP_K, SC_ROWS), jnp.int32),
                       pltpu.VMEM((SC_ROWS, HALF), jnp.uint32),
                       pltpu.VMEM((SC_ROWS, HALF), jnp.uint32),
                       pltpu.SemaphoreType.DMA, pltpu.SemaphoreType.DMA,
                       pltpu.SemaphoreType.DMA, pltpu.SemaphoreType.DMA],
        name="sc_dispatch",
    )
    return run(h2p, idx)


def _collect(ys, dest):
    n = dest.size
    per_w = n // SC_WORKERS
    nch = per_w // SC_ROWS
    idx = dest.reshape(SC_WORKERS, nch, SC_ROWS)

    def body(ys_hbm, idx_hbm, out_hbm, idx_v, buf0, buf1, gsem0, gsem1, wsem0, wsem1):
        wid = _worker_id()
        base = wid * per_w
        pltpu.sync_copy(idx_hbm.at[wid], idx_v)
        bufs = ((buf0, gsem0, wsem0), (buf1, gsem1, wsem1))

        def gather(c, buf, gsem):
            return pltpu.make_async_copy(ys_hbm.at[idx_v.at[c]], buf, gsem)

        def write(c, buf, wsem):
            return pltpu.make_async_copy(buf, out_hbm.at[pl.ds(base + c * SC_ROWS, SC_ROWS)], wsem)

        gather(0, buf0, gsem0).start()

        @pl.loop(0, nch, step=2)
        def _(c0):
            for b in range(2):
                c = c0 + b
                buf, gsem, wsem = bufs[b]
                obuf, ogsem, owsem = bufs[1 - b]
                gather(c, buf, gsem).wait()

                @pl.when(c > 0)
                def _():
                    write(c - 1, obuf, owsem).wait()

                @pl.when(c + 1 < nch)
                def _():
                    gather(c + 1, obuf, ogsem).start()

                write(c, buf, wsem).start()

        write(nch - 1, buf1, wsem1).wait()

    run = pl.kernel(
        body,
        out_type=jax.ShapeDtypeStruct((n, HALF), jnp.uint32),
        mesh=_sc_mesh(),
        scratch_types=[pltpu.VMEM((nch, SC_ROWS), jnp.int32),
                       pltpu.VMEM((SC_ROWS, HALF), jnp.uint32),
                       pltpu.VMEM((SC_ROWS, HALF), jnp.uint32),
                       pltpu.SemaphoreType.DMA, pltpu.SemaphoreType.DMA,
                       pltpu.SemaphoreType.DMA, pltpu.SemaphoreType.DMA],
        name="sc_collect",
    )
    return run(ys, idx).reshape(dest.shape + (HALF,))


def kernel(x, c, positions, w_ada, b_ada, g_pre_mix, g_post_mix, g_pre_ffn, g_post_ffn, w_in, conv_w, conv_b, b_gates, g_mlstm, w_branch_a, w_branch_b, w_out, router_w, router_bias, w_exp_gate, w_exp_up, w_exp_down, w_sh_gate, w_sh_up, w_sh_down):
    depth = w_ada.shape[0]
    for l in range(depth):
        x = _layer(x, c, positions, w_ada[l], b_ada[l], g_pre_mix[l], g_post_mix[l], g_pre_ffn[l],
                   g_post_ffn[l], w_in[l], conv_w[l], conv_b[l], b_gates[l], g_mlstm[l],
                   w_branch_a[l], w_branch_b[l], w_out[l], router_w[l], router_bias[l],
                   w_exp_gate[l], w_exp_up[l], w_exp_down[l], w_sh_gate[l], w_sh_up[l], w_sh_down[l])
    return x
```

```python
import functools

import jax
import jax.numpy as jnp
from jax import lax
from jax.experimental import pallas as pl
from jax.experimental.pallas import tpu as pltpu
from jax.experimental.pallas import tpu_sc as plsc

F32 = jnp.float32
BF16 = jnp.bfloat16
HIGHEST = lax.Precision.HIGHEST

D_MODEL = 1024
ATT_GROUPS = ((128, 1), (512, 4), (2048, 16))
ATT_HEAD_DIM = 64
ATT_GROUP_W = 256
ATT_BLK = 128
ATT_PAIR = 2
ROPE_THETA = 500000.0
ROPE_HALF = 8
MLSTM_HEADS = 4
MLSTM_QK_DIM = 128
MLSTM_V_DIM = 256
MLSTM_BLOCK = 128
MLSTM_GROUP = 8
CONV_WIDTH = 4
N_EXPERTS = 256
TOP_K = 8
N_GROUPS = 8
TOPK_GROUPS = 4
EXPERT_FF = 256
ROUTED_SCALE = 2.5
NORM_EPS = 1e-6
NEG = -1e30

OFF_MV, OFF_MO, OFF_GA, OFF_GB = 0, 1024, 2048, 3072
OFF_MQ, OFF_MK = 4096, 4608
OFF_AQ, OFF_AK, OFF_AV = 5120, 5888, 6656
PROJ_W = 7424
HALF = D_MODEL // 2

EXPERT_BLOCK = 512
EXPERT_SLOTS = 6
MOE_PARTS = 2
MERGE_SPLIT = 2
VMEM_LIMIT = 56 * 1024 * 1024


def _nt(a, b, precision=None):
    return lax.dot_general(a, b, (((1,), (1,)), ((), ())), preferred_element_type=F32,
                           precision=precision)


def _tn(a, b):
    return lax.dot_general(a, b, (((0,), (0,)), ((), ())), preferred_element_type=F32)


_sigmoid = jax.nn.sigmoid


def _silu(x):
    return x * _sigmoid(x)


def _pack_pair(lo, hi):
    lo_b = pltpu.bitcast(lo.astype(BF16).astype(F32), jnp.uint32)
    hi_b = pltpu.bitcast(hi.astype(BF16).astype(F32), jnp.uint32)
    return (lo_b >> 16) | (hi_b & jnp.uint32(0xFFFF0000))


def _unpack_pair(w):
    lo = pltpu.bitcast(w << 16, F32)
    hi = pltpu.bitcast(w & jnp.uint32(0xFFFF0000), F32)
    return lo, hi


def _mod_kernel(c_ref, w_ref, b_ref, o_ref):
    a = _silu(c_ref[...])
    o_ref[...] = jnp.dot(a, w_ref[...], preferred_element_type=F32, precision=HIGHEST) + b_ref[...]


def _adaln(c, w_ada, b_ada):
    B = c.shape[0]
    n = w_ada.shape[1]
    tn = 512
    return pl.pallas_call(
        _mod_kernel,
        grid=(n // tn,),
        in_specs=[pl.BlockSpec((B, D_MODEL), lambda j: (0, 0)),
                  pl.BlockSpec((D_MODEL, tn), lambda j: (0, j)),
                  pl.BlockSpec((1, tn), lambda j: (0, j))],
        out_specs=pl.BlockSpec((B, tn), lambda j: (0, j)),
        out_shape=jax.ShapeDtypeStruct((B, n), F32),
        name="adaln_mod",
    )(c, w_ada, b_ada.reshape(1, n))


def _proj_kernel(x_ref, mod_ref, g_ref, w_ref, wif_ref, o_ref, gates_ref, h_ref):
    @pl.when(pl.program_id(1) == 0)
    def _():
        x = x_ref[...]
        ms = jnp.mean(x * x, axis=-1, keepdims=True)
        y = x * lax.rsqrt(ms + NORM_EPS) * g_ref[...]
        h = (y * (1.0 + mod_ref[0, 1:2, :]) + mod_ref[0, 0:1, :]).astype(BF16)
        h_ref[...] = h
        gates_ref[...] = _nt(wif_ref[...], h)

    o_ref[...] = jnp.dot(h_ref[...], w_ref[...], preferred_element_type=F32).astype(BF16)


def _in_proj(x2, mod3, g_pre, w_main, w_if, seq):
    T = x2.shape[0]
    tm, tn = 1024, PROJ_W // 2
    per_b = seq // tm
    return pl.pallas_call(
        _proj_kernel,
        grid=(T // tm, PROJ_W // tn),
        in_specs=[pl.BlockSpec((tm, D_MODEL), lambda i, j: (i, 0)),
                  pl.BlockSpec((1, 6, D_MODEL), lambda i, j: (i // per_b, 0, 0)),
                  pl.BlockSpec((1, D_MODEL), lambda i, j: (0, 0)),
                  pl.BlockSpec((D_MODEL, tn), lambda i, j: (0, j)),
                  pl.BlockSpec((2 * MLSTM_HEADS, D_MODEL), lambda i, j: (0, 0))],
        out_specs=[pl.BlockSpec((tm, tn), lambda i, j: (i, j)),
                   pl.BlockSpec((2 * MLSTM_HEADS, tm), lambda i, j: (0, i))],
        out_shape=[jax.ShapeDtypeStruct((T, PROJ_W), BF16),
                   jax.ShapeDtypeStruct((2 * MLSTM_HEADS, T), F32)],
        scratch_shapes=[pltpu.VMEM((tm, D_MODEL), BF16)],
        compiler_params=pltpu.CompilerParams(
            dimension_semantics=("arbitrary", "arbitrary"), vmem_limit_bytes=VMEM_LIMIT),
        name="norm_in_proj",
    )(x2, mod3, g_pre, w_main, w_if)


def _attn_kernel(q_ref, k_ref, v_ref, cs_ref, sn_ref, o_ref, qf, kf, vf, acc, m_s, l_s, *, seq):
    g = pl.program_id(1)
    lane = lax.broadcasted_iota(jnp.int32, (ATT_BLK, 128), 1)
    first = (lane % ATT_HEAD_DIM) < ROPE_HALF
    low_head = lane < ATT_HEAD_DIM

    def rope(x, cs, sn):
        partner = jnp.where(first, pltpu.roll(x, 128 - ROPE_HALF, 1), pltpu.roll(x, ROPE_HALF, 1))
        return x * cs + partner * sn

    def zero_pad(i, _):
        rows = pl.ds(pl.multiple_of(i * ATT_BLK, ATT_BLK), ATT_BLK)
        for hp in range(2):
            kf[hp, rows, :] = jnp.zeros((ATT_BLK, 128), F32)
            vf[hp, rows, :] = jnp.zeros((ATT_BLK, 128), F32)
        return 0

    lax.fori_loop(0, seq // ATT_BLK, zero_pad, 0)

    def stage(i, _):
        r = pl.multiple_of(i * ATT_BLK, ATT_BLK)
        rows = pl.ds(r, ATT_BLK)
        prow = pl.ds(pl.multiple_of(seq + i * ATT_BLK, ATT_BLK), ATT_BLK)
        cs = cs_ref[0, rows, :]
        sn = sn_ref[0, rows, :]
        for hp in range(2):
            cols = pl.ds(hp * 128, 128)
            qf[hp, rows, :] = rope(q_ref[0, rows, cols].astype(F32), cs, sn) * (ATT_HEAD_DIM ** -0.5)
            kf[hp, prow, :] = rope(k_ref[0, rows, cols].astype(F32), cs, sn)
            vf[hp, prow, :] = v_ref[0, rows, cols].astype(F32)
        return 0

    lax.fori_loop(0, seq // ATT_BLK, stage, 0)

    qi = lax.broadcasted_iota(jnp.int32, (ATT_BLK, 2 * ATT_BLK), 0)
    ki = lax.broadcasted_iota(jnp.int32, (ATT_BLK, 2 * ATT_BLK), 1)
    band = (ki >= qi) & (ki <= qi + ATT_BLK)

    def process(d, init):
        span = ATT_BLK * d
        single = seq == span

        def body(cp, _):
            blocks = [cp * ATT_PAIR + i for i in range(ATT_PAIR)]
            qrows, krows, valid = [], [], []
            for c in blocks:
                rho = c % d
                n = c // d
                qstart = rho + n * span
                if single:
                    kstart, nk = seq + qstart, ATT_BLK
                    valid.append(band[:, ATT_BLK:])
                else:
                    kstart, nk = seq + qstart - span, 2 * ATT_BLK
                    valid.append(band & (ki >= jnp.where(n > 0, 0, ATT_BLK)))
                qrows.append(pl.ds(qstart, ATT_BLK, stride=d) if d > 1 else pl.ds(qstart, ATT_BLK))
                krows.append(pl.ds(kstart, nk, stride=d) if d > 1 else pl.ds(kstart, nk))
            units = [(b, hp) for b in range(ATT_PAIR) for hp in range(2)]
            heads = [(u, hh) for u in range(len(units)) for hh in range(2)]
            q2 = [qf[hp, qrows[b], :] for b, hp in units]
            k2 = [kf[hp, krows[b], :].astype(BF16) for b, hp in units]
            v2 = [vf[hp, krows[b], :].astype(BF16) for b, hp in units]
            qh = [jnp.where(low_head if hh == 0 else jnp.logical_not(low_head), q2[u], 0.0).astype(BF16)
                  for u, hh in heads]
            s = [jnp.where(valid[units[u][0]], _nt(qh[i], k2[u]), NEG) for i, (u, hh) in enumerate(heads)]
            m = [jnp.max(x, axis=1, keepdims=True) for x in s]
            p = [jnp.exp(x - mx) for x, mx in zip(s, m)]
            l = [jnp.sum(x, axis=1, keepdims=True) for x in p]
            o = [jnp.dot(p[i].astype(BF16), v2[u], preferred_element_type=F32)
                 for i, (u, hh) in enumerate(heads)]
            for u, (b, hp) in enumerate(units):
                o_b = jnp.where(low_head, o[2 * u], o[2 * u + 1])
                m_b = jnp.where(low_head, m[2 * u], m[2 * u + 1])
                l_b = jnp.where(low_head, l[2 * u], l[2 * u + 1])
                if init:
                    acc[hp, qrows[b], :] = o_b
                    m_s[hp, qrows[b], :] = m_b
                    l_s[hp, qrows[b], :] = l_b
                else:
                    m_old = m_s[hp, qrows[b], :]
                    m_new = jnp.maximum(m_old, m_b)
                    a_old = jnp.exp(m_old - m_new)
                    a_new = jnp.exp(m_b - m_new)
                    acc[hp, qrows[b], :] = acc[hp, qrows[b], :] * a_old + o_b * a_new
                    l_s[hp, qrows[b], :] = l_s[hp, qrows[b], :] * a_old + l_b * a_new
                    m_s[hp, qrows[b], :] = m_new
            return 0

        lax.fori_loop(0, seq // (ATT_BLK * ATT_PAIR), body, 0)

    for gi, (_, d) in enumerate(ATT_GROUPS):
        @pl.when(g == gi)
        def _(d=d, gi=gi):
            process(d, gi == 0)

    @pl.when(g == len(ATT_GROUPS) - 1)
    def _():
        def fin(i, _):
            rows = pl.ds(pl.multiple_of(i * ATT_BLK, ATT_BLK), ATT_BLK)
            for hp in range(2):
                o_ref[0, rows, pl.ds(hp * 128, 128)] = (acc[hp, rows, :] / l_s[hp, rows, :]).astype(BF16)
            return 0

        lax.fori_loop(0, seq // ATT_BLK, fin, 0)


def _attention(proj3, cs, sn):
    B, S, _ = proj3.shape
    ng = len(ATT_GROUPS)
    qb, kb, vb = OFF_AQ // ATT_GROUP_W, OFF_AK // ATT_GROUP_W, OFF_AV // ATT_GROUP_W
    return pl.pallas_call(
        functools.partial(_attn_kernel, seq=S),
        grid=(B, ng),
        in_specs=[pl.BlockSpec((1, S, ATT_GROUP_W), lambda b, g: (b, 0, qb + g)),
                  pl.BlockSpec((1, S, ATT_GROUP_W), lambda b, g: (b, 0, kb + g)),
                  pl.BlockSpec((1, S, ATT_GROUP_W), lambda b, g: (b, 0, vb + g)),
                  pl.BlockSpec((1, S, 128), lambda b, g: (b, 0, 0)),
                  pl.BlockSpec((1, S, 128), lambda b, g: (b, 0, 0))],
        out_specs=pl.BlockSpec((1, S, ATT_GROUP_W), lambda b, g: (b, 0, 0)),
        out_shape=jax.ShapeDtypeStruct((B, S, ATT_GROUP_W), BF16),
        scratch_shapes=[pltpu.VMEM((2, S, 128), F32),
                        pltpu.VMEM((2, 2 * S, 128), F32),
                        pltpu.VMEM((2, 2 * S, 128), F32),
                        pltpu.VMEM((2, S, 128), F32),
                        pltpu.VMEM((2, S, 128), F32),
                        pltpu.VMEM((2, S, 128), F32)],
        compiler_params=pltpu.CompilerParams(
            dimension_semantics=("arbitrary", "arbitrary"), vmem_limit_bytes=VMEM_LIMIT),
        name="dilated_attention",
    )(proj3, proj3, proj3, cs, sn)


def _log_sigmoid(x):
    return jnp.minimum(x, 0.0) - jnp.log(1.0 + jnp.exp(-jnp.abs(x)))


def _mlstm_kernel(mq_ref, mk_ref, mv_ref, mo_ref, gt_ref, cwq_ref, cwk_ref, cbq_ref, cbk_ref,
                  bg_ref, gm_ref, o_ref, q_s, k_s, va_s, rows_s, acc_s, kv_s, inter_s, emt_s,
                  c_s, *, seq):
    h = pl.program_id(1)
    L = MLSTM_BLOCK
    NC = seq // L
    DK, DV = MLSTM_QK_DIM, MLSTM_V_DIM
    DA = DV + 128
    nshift = CONV_WIDTH - 1

    tt = lax.broadcasted_iota(jnp.int32, (nshift * L, 2 * L), 0)
    uu = lax.broadcasted_iota(jnp.int32, (nshift * L, 2 * L), 1)
    shift_mat = (uu == L + tt % L - (tt // L + 1)).astype(BF16)
    conv_w = jnp.concatenate([cwq_ref[...], cwk_ref[...]], axis=1)
    conv_b = jnp.concatenate([cbq_ref[...], cbk_ref[...]], axis=1)
    prev = jnp.zeros((L, 2 * DK), BF16)
    for i in range(NC):
        blk = slice(i * L, (i + 1) * L)
        va_s[blk, 0:DV] = mv_ref[0, blk, :]
        va_s[blk, DV:DA] = jnp.ones((L, DA - DV), BF16)
        cur = jnp.concatenate([mq_ref[0, blk, :], mk_ref[0, blk, :]], axis=1)
        shifted = jnp.dot(shift_mat, jnp.concatenate([prev, cur], axis=0),
                          preferred_element_type=F32)
        y = conv_b + cur.astype(F32) * conv_w[nshift:nshift + 1, :]
        for s in range(nshift):
            y = y + shifted[s * L:(s + 1) * L, :] * conv_w[nshift - 1 - s:nshift - s, :]
        y = _silu(y)
        q_s[blk, :] = y[:, 0:DK].astype(BF16)
        k_s[blk, :] = (y[:, DK:2 * DK] * (DK ** -0.5)).astype(BF16)
        prev = cur

    lane = lax.broadcasted_iota(jnp.int32, (1, 128), 1)
    bias = bg_ref[...]
    b_i = jnp.sum(jnp.where(lane == h, bias, 0.0), axis=1, keepdims=True)
    b_f = jnp.sum(jnp.where(lane == h + MLSTM_HEADS, bias, 0.0), axis=1, keepdims=True)
    ri = lax.broadcasted_iota(jnp.int32, (L, L), 0)
    ci = lax.broadcasted_iota(jnp.int32, (L, L), 1)
    causal = ci <= ri
    eye = (ri == ci).astype(F32)
    i_rows = gt_ref[h, 0] + b_i
    lf_rows = _log_sigmoid(gt_ref[h + MLSTM_HEADS, 0] + b_f)
    b_rows = jnp.dot(lf_rows, (ri <= ci).astype(F32), preferred_element_type=F32,
                     precision=HIGHEST)
    b_end = b_rows[:, L - 1:L]
    g_rows = b_end - b_rows + i_rows
    g_max = jnp.max(g_rows, axis=1, keepdims=True)
    m = jnp.zeros((1, 1), F32)
    m_prev, m_new = [], []
    for c in range(NC):
        m_prev.append(m)
        m = jnp.maximum(b_end[c:c + 1, :] + m, g_max[c:c + 1, :])
        m_new.append(m)
    m_prev = jnp.concatenate(m_prev, axis=0)
    m_new = jnp.concatenate(m_new, axis=0)
    rows_s[0] = b_rows
    rows_s[1] = jnp.exp(g_rows - m_new)
    rows_s[2] = b_rows - i_rows
    rows_s[3] = jnp.broadcast_to(m_prev, (NC, L))
    rows_s[4] = jnp.broadcast_to(jnp.exp(b_end + m_prev - m_new), (NC, L))

    r2 = lax.broadcasted_iota(jnp.int32, (2 * L, 2 * L), 0)
    c2 = lax.broadcasted_iota(jnp.int32, (2 * L, 2 * L), 1)
    ones_blk = ((r2 < L) == (c2 < L)).astype(BF16)

    G = MLSTM_GROUP

    def local(cg, _):
        cs = [cg * G + i for i in range(G)]
        rows = [pl.ds(pl.multiple_of(c * L, L), L) for c in cs]
        b_r = [rows_s[0, pl.ds(c, 1), :] for c in cs]
        w_r = [rows_s[1, pl.ds(c, 1), :] for c in cs]
        u_r = [rows_s[2, pl.ds(c, 1), :] for c in cs]
        mp = [rows_s[3, pl.ds(c, 1), :] for c in cs]
        q = [q_s[r, :] for r in rows]
        k = [k_s[r, :] for r in rows]
        va = [va_s[r, :] for r in rows]
        qk = [_nt(a, b) for a, b in zip(q, k)]
        x2 = [jnp.concatenate([eye * a, eye * b], axis=1) for a, b in zip(b_r, w_r)]
        hi = [x.astype(BF16) for x in x2]
        lo = [(x - h_.astype(F32)).astype(BF16) for x, h_ in zip(x2, hi)]
        yb = [jnp.dot(h_, ones_blk, preferred_element_type=F32)
              + jnp.dot(l_, ones_blk, preferred_element_type=F32) for h_, l_ in zip(hi, lo)]
        b_b = [y[:, 0:L] for y in yb]
        w_b = [y[:, L:2 * L] for y in yb]
        for i in range(G):
            kv_s[cs[i]] = _tn((w_b[i] * k[i].astype(F32)).astype(BF16), va[i])
        dmat = [jnp.where(causal, b - u, NEG) for b, u in zip(b_b, u_r)]
        m_t = [jnp.maximum(b + m_, jnp.max(d, axis=1, keepdims=True))
               for b, m_, d in zip(b_b, mp, dmat)]
        sc = [a * jnp.exp(d - m_) for a, d, m_ in zip(qk, dmat, m_t)]
        for i in range(G):
            acc_s[rows[i], :] = jnp.dot(sc[i].astype(BF16), va[i], preferred_element_type=F32)
            inter_s[rows[i], :] = jnp.exp(b_b[i] + mp[i] - m_t[i])
            emt_s[rows[i], :] = jnp.exp(-m_t[i])
        return 0

    lax.fori_loop(0, NC // G, local, 0)

    g_row = gm_ref[...]
    c_s[...] = jnp.zeros((DK, DA), F32)

    def recur(cg, _):
        cs = [cg * G + i for i in range(G)]
        rows = [pl.ds(pl.multiple_of(c * L, L), L) for c in cs]
        states = [c_s[...]]
        for c in cs:
            dec = rows_s[4, pl.ds(c, 1), :]
            states.append(jnp.concatenate([dec, dec, dec], axis=1) * states[-1] + kv_s[c])
        c_s[...] = states[G]
        read = [jnp.dot(q_s[r, :], st.astype(BF16), preferred_element_type=F32)
                for r, st in zip(rows, states)]
        inter = [inter_s[r, :] for r in rows]
        out = [acc_s[r, :] + jnp.concatenate([it, it, it], axis=1) * rd
               for r, it, rd in zip(rows, inter, read)]
        emt = [emt_s[r, :] for r in rows]
        nrm = [jnp.maximum(jnp.abs(jnp.concatenate([o[:, DV:DA], o[:, DV:DA]], axis=1)),
                           jnp.concatenate([e_, e_], axis=1)) for o, e_ in zip(out, emt)]
        hh = [o[:, 0:DV] / n_ for o, n_ in zip(out, nrm)]
        ms = [jnp.mean(x * x, axis=1, keepdims=True) for x in hh]
        hn = [x * lax.rsqrt(m_ + NORM_EPS) * g_row for x, m_ in zip(hh, ms)]
        for i in range(G):
            o_ref[0, rows[i], :] = (hn[i] * _sigmoid(mo_ref[0, rows[i], :].astype(F32))).astype(BF16)
        return 0

    lax.fori_loop(0, NC // G, recur, 0)


def _mlstm(proj3, gates_t, conv_w, conv_b, bg_row, g_mlstm):
    B, S, _ = proj3.shape
    H, DK, DV = MLSTM_HEADS, MLSTM_QK_DIM, MLSTM_V_DIM
    L = MLSTM_BLOCK
    NC = S // L
    DA = DV + 128
    qb, kb = OFF_MQ // DK, OFF_MK // DK
    vb, ob = OFF_MV // DV, OFF_MO // DV
    nq = (H * DK) // DK
    return pl.pallas_call(
        functools.partial(_mlstm_kernel, seq=S),
        grid=(B, H),
        in_specs=[pl.BlockSpec((1, S, DK), lambda b, h: (b, 0, qb + h)),
                  pl.BlockSpec((1, S, DK), lambda b, h: (b, 0, kb + h)),
                  pl.BlockSpec((1, S, DV), lambda b, h: (b, 0, vb + h)),
                  pl.BlockSpec((1, S, DV), lambda b, h: (b, 0, ob + h)),
                  pl.BlockSpec((2 * H, 1, NC, L), lambda b, h: (0, b, 0, 0)),
                  pl.BlockSpec((CONV_WIDTH, DK), lambda b, h: (0, h)),
                  pl.BlockSpec((CONV_WIDTH, DK), lambda b, h: (0, nq + h)),
                  pl.BlockSpec((1, DK), lambda b, h: (0, h)),
                  pl.BlockSpec((1, DK), lambda b, h: (0, nq + h)),
                  pl.BlockSpec((1, 128), lambda b, h: (0, 0)),
                  pl.BlockSpec((1, DV), lambda b, h: (0, h))],
        out_specs=pl.BlockSpec((1, S, DV), lambda b, h: (b, 0, h)),
        out_shape=jax.ShapeDtypeStruct((B, S, H * DV), BF16),
        scratch_shapes=[pltpu.VMEM((S, DK), BF16),
                        pltpu.VMEM((S, DK), BF16),
                        pltpu.VMEM((S, DA), BF16),
                        pltpu.VMEM((5, NC, L), F32),
                        pltpu.VMEM((S, DA), F32),
                        pltpu.VMEM((NC, DK, DA), F32),
                        pltpu.VMEM((S, L), F32),
                        pltpu.VMEM((S, L), F32),
                        pltpu.VMEM((DK, DA), F32)],
        compiler_params=pltpu.CompilerParams(
            dimension_semantics=("arbitrary", "arbitrary"), vmem_limit_bytes=VMEM_LIMIT),
        name="mlstm_chunkwise",
    )(proj3, proj3, proj3, proj3, gates_t, conv_w, conv_w, conv_b, conv_b, bg_row, g_mlstm)


def _rms(y, g):
    ms = jnp.mean(y * y, axis=-1, keepdims=True)
    return y * lax.rsqrt(ms + NORM_EPS) * g


def _merge_kernel(ya_ref, yb_ref, ga_ref, gb_ref, x_ref, mod_ref, wa_ref, wb_ref, wo_ref,
                  gpost_ref, gpre_ref, x1_ref, h2_ref):
    tm = x_ref.shape[0]
    slabs = [pl.ds(s * (tm // MERGE_SPLIT), tm // MERGE_SPLIT) for s in range(MERGE_SPLIT)]
    pa = [jnp.dot(ya_ref[r, :], wa_ref[...], preferred_element_type=F32) for r in slabs]
    pb = [jnp.dot(yb_ref[r, :], wb_ref[...], preferred_element_type=F32) for r in slabs]
    merged = [_sigmoid(ga_ref[r, :].astype(F32)) * a + _sigmoid(gb_ref[r, :].astype(F32)) * b
              for r, a, b in zip(slabs, pa, pb)]
    y = [jnp.dot(m.astype(BF16), wo_ref[...], preferred_element_type=F32) for m in merged]
    x1 = [x_ref[r, :] + mod_ref[0, 2:3, :] * _rms(v, gpost_ref[...]) for r, v in zip(slabs, y)]
    for r, v in zip(slabs, x1):
        x1_ref[r, :] = v
    h2 = [_rms(v, gpre_ref[...]) * (1.0 + mod_ref[0, 4:5, :]) + mod_ref[0, 3:4, :] for v in x1]
    for r, v in zip(slabs, h2):
        h2_ref[r, :] = _pack_pair(v[:, :HALF], v[:, HALF:])


def _merge(ya2, yb2, proj2, x2, mod3, wa, wb, wo, g_post, g_pre, seq):
    T = x2.shape[0]
    tm = 512 * MERGE_SPLIT
    per_b = seq // tm
    full = lambda shape: pl.BlockSpec(shape, lambda i: (0,) * len(shape))
    return pl.pallas_call(
        _merge_kernel,
        grid=(T // tm,),
        in_specs=[pl.BlockSpec((tm, ATT_GROUP_W), lambda i: (i, 0)),
                  pl.BlockSpec((tm, D_MODEL), lambda i: (i, 0)),
                  pl.BlockSpec((tm, D_MODEL), lambda i: (i, OFF_GA // D_MODEL)),
                  pl.BlockSpec((tm, D_MODEL), lambda i: (i, OFF_GB // D_MODEL)),
                  pl.BlockSpec((tm, D_MODEL), lambda i: (i, 0)),
                  pl.BlockSpec((1, 6, D_MODEL), lambda i: (i // per_b, 0, 0)),
                  full((ATT_GROUP_W, D_MODEL)), full((D_MODEL, D_MODEL)), full((D_MODEL, D_MODEL)),
                  full((1, D_MODEL)), full((1, D_MODEL))],
        out_specs=[pl.BlockSpec((tm, D_MODEL), lambda i: (i, 0)),
                   pl.BlockSpec((tm, HALF), lambda i: (i, 0))],
        out_shape=[jax.ShapeDtypeStruct((T, D_MODEL), F32),
                   jax.ShapeDtypeStruct((T, HALF), jnp.uint32)],
        compiler_params=pltpu.CompilerParams(
            dimension_semantics=("arbitrary",), vmem_limit_bytes=VMEM_LIMIT),
        name="merge_out_proj",
    )(ya2, yb2, proj2, proj2, x2, mod3, wa, wb, wo, g_post, g_pre)


def _router_kernel(h2_ref, rlo_ref, rhi_ref, bias_ref, idx_ref, w_ref, rank_ref, cnt_ref):
    E = N_EXPERTS
    tr = h2_ref.shape[0]
    gsz = E // N_GROUPS

    @pl.when(pl.program_id(0) == 0)
    def _():
        cnt_ref[...] = jnp.zeros(cnt_ref.shape, F32)

    lo, hi = _unpack_pair(h2_ref[...])
    logits = _nt(rlo_ref[...], lo.astype(BF16)) + _nt(rhi_ref[...], hi.astype(BF16))
    scores = _sigmoid(logits)
    sel = scores + bias_ref[:, 0:1]

    gi = lax.broadcasted_iota(jnp.int32, (gsz, tr), 0).astype(F32)
    gs_rows = []
    for g in range(N_GROUPS):
        blk = sel[g * gsz:(g + 1) * gsz, :]
        m1 = jnp.max(blk, axis=0, keepdims=True)
        a1 = jnp.min(jnp.where(blk == m1, gi, float(E)), axis=0, keepdims=True)
        m2 = jnp.max(jnp.where(gi == a1, -jnp.inf, blk), axis=0, keepdims=True)
        gs_rows.append(m1 + m2)
    gs = jnp.concatenate(gs_rows, axis=0)
    g8 = lax.broadcasted_iota(jnp.int32, (N_GROUPS, tr), 0).astype(F32)
    gmask = jnp.zeros((N_GROUPS, tr), F32)
    for _ in range(TOPK_GROUPS):
        m = jnp.max(gs, axis=0, keepdims=True)
        a = jnp.min(jnp.where(gs == m, g8, float(E)), axis=0, keepdims=True)
        hit = g8 == a
        gmask = jnp.where(hit, 1.0, gmask)
        gs = jnp.where(hit, -jnp.inf, gs)
    selm = jnp.concatenate(
        [jnp.where(gmask[g:g + 1, :] > 0.0, sel[g * gsz:(g + 1) * gsz, :], -jnp.inf)
         for g in range(N_GROUPS)], axis=0)

    ei = lax.broadcasted_iota(jnp.int32, (E, tr), 0).astype(F32)
    picks, weights, hits = [], [], []
    candidates = selm
    for _ in range(TOP_K):
        m = jnp.max(selm, axis=0, keepdims=True)
        a = jnp.min(jnp.where(selm == m, ei, float(E)), axis=0, keepdims=True)
        hit = ei == a
        picks.append(a)
        hits.append(hit)
        weights.append(jnp.sum(jnp.where(hit, scores, 0.0), axis=0, keepdims=True))
        selm = jnp.where(hit, -jnp.inf, selm)
    chosen = jnp.where(selm != candidates, 1.0, 0.0)
    wsum = weights[0]
    for w in weights[1:]:
        wsum = wsum + w

    ti = lax.broadcasted_iota(jnp.int32, (tr, tr), 0)
    tj = lax.broadcasted_iota(jnp.int32, (tr, tr), 1)
    before = (ti < tj).astype(BF16)
    pos = jnp.dot(chosen.astype(BF16), before, preferred_element_type=F32) + cnt_ref[:, 0:1]
    ranks = [jnp.sum(jnp.where(hit, pos, 0.0), axis=0, keepdims=True) for hit in hits]
    cnt_ref[...] = cnt_ref[...] + jnp.sum(chosen, axis=1, keepdims=True)

    idx_ref[...] = jnp.concatenate(picks, axis=0).astype(jnp.int32)
    w_ref[...] = jnp.concatenate([w / wsum * ROUTED_SCALE for w in weights], axis=0)
    rank_ref[...] = jnp.concatenate(ranks, axis=0).astype(jnp.int32)


def _router(h2p, r_lo, r_hi, bias_col, row0, T):
    tr = 512
    off = row0 // tr
    full = lambda shape: pl.BlockSpec(shape, lambda i: (0,) * len(shape))
    return pl.pallas_call(
        _router_kernel,
        grid=(T // tr,),
        in_specs=[pl.BlockSpec((tr, HALF), lambda i: (i + off, 0)),
                  full((N_EXPERTS, HALF)), full((N_EXPERTS, HALF)), full((N_EXPERTS, 128))],
        out_specs=[pl.BlockSpec((TOP_K, tr), lambda i: (0, i)),
                   pl.BlockSpec((TOP_K, tr), lambda i: (0, i)),
                   pl.BlockSpec((TOP_K, tr), lambda i: (0, i)),
                   full((N_EXPERTS, 128))],
        out_shape=[jax.ShapeDtypeStruct((TOP_K, T), jnp.int32),
                   jax.ShapeDtypeStruct((TOP_K, T), F32),
                   jax.ShapeDtypeStruct((TOP_K, T), jnp.int32),
                   jax.ShapeDtypeStruct((N_EXPERTS, 128), F32)],
        compiler_params=pltpu.CompilerParams(
            dimension_semantics=("arbitrary",), vmem_limit_bytes=VMEM_LIMIT),
        name="router_topk",
    )(h2p, r_lo, r_hi, bias_col)


def _dest_kernel(idx_ref, rank_ref, pstart_ref, dest_ref):
    tr = idx_ref.shape[1]
    ei = lax.broadcasted_iota(jnp.int32, (N_EXPERTS, tr), 0)
    start = pstart_ref[:, 0:1]
    rows = []
    for k in range(TOP_K):
        hit = ei == idx_ref[k:k + 1, :]
        rows.append(jnp.sum(jnp.where(hit, start, 0.0), axis=0, keepdims=True))
    dest_ref[...] = jnp.concatenate(rows, axis=0).astype(jnp.int32) + rank_ref[...]


def _slot_index(idx, rank, pstart_col):
    T = idx.shape[1]
    tr = 1024
    return pl.pallas_call(
        _dest_kernel,
        grid=(T // tr,),
        in_specs=[pl.BlockSpec((TOP_K, tr), lambda i: (0, i)),
                  pl.BlockSpec((TOP_K, tr), lambda i: (0, i)),
                  pl.BlockSpec((N_EXPERTS, 128), lambda i: (0, 0))],
        out_specs=pl.BlockSpec((TOP_K, tr), lambda i: (0, i)),
        out_shape=jax.ShapeDtypeStruct((TOP_K, T), jnp.int32),
        name="slot_index",
    )(idx, rank, pstart_col)


def _ffn_kernel(first_ref, nblk_ref, nused_ref, xs_hbm, wg_ref, wu_ref, wd_ref, ys_hbm,
                xbuf, ybuf, in_sem, out_sem, wg_s, wu_s, wd_s):
    e = pl.program_id(0)
    bm = EXPERT_BLOCK
    ns = EXPERT_SLOTS
    nused = nused_ref[0]
    first = first_ref[e]
    n = nblk_ref[e]

    def in_copy(g):
        slot = g % ns
        return pltpu.make_async_copy(xs_hbm.at[pl.ds(g * bm, bm)], xbuf.at[slot], in_sem.at[slot])

    def out_copy(g):
        slot = g % ns
        return pltpu.make_async_copy(ybuf.at[slot], ys_hbm.at[pl.ds(g * bm, bm)], out_sem.at[slot])

    def fetch(g):
        @pl.when(g < nused)
        def _():
            in_copy(g).start()

    def release(g):
        @pl.when(g >= ns)
        def _():
            out_copy(g - ns).wait()

    def ffn(g):
        lo, hi = _unpack_pair(xbuf[g % ns])
        x = jnp.concatenate([lo.astype(BF16), hi.astype(BF16)], axis=1)
        gate = jnp.dot(x, wg_s[...], preferred_element_type=F32)
        up = jnp.dot(x, wu_s[...], preferred_element_type=F32)
        hid = (_silu(gate) * up).astype(BF16)
        return jnp.dot(hid, wd_s[...], preferred_element_type=F32)

    def pack(g, out):
        ybuf[g % ns] = _pack_pair(out[:, :HALF], out[:, HALF:])

    @pl.when(e == 0)
    def _():
        for q in range(ns - 1):
            fetch(q)

    @pl.when(n > 0)
    def _():
        wg_s[...] = wg_ref[0].astype(BF16)
        wu_s[...] = wu_ref[0].astype(BF16)
        wd_s[...] = wd_ref[0].astype(BF16)

        def two_blocks(j, _):
            g = first + 2 * j
            in_copy(g).wait()
            in_copy(g + 1).wait()
            fetch(g + ns - 1)
            release(g)
            release(g + 1)
            out_a = ffn(g)
            out_b = ffn(g + 1)
            pack(g, out_a)
            pack(g + 1, out_b)
            out_copy(g).start()
            out_copy(g + 1).start()
            fetch(g + ns)
            return 0

        lax.fori_loop(0, n // 2, two_blocks, 0)

        @pl.when(n % 2 == 1)
        def _():
            g = first + n - 1
            in_copy(g).wait()
            fetch(g + ns - 1)
            release(g)
            pack(g, ffn(g))
            out_copy(g).start()

    @pl.when(e == pl.num_programs(0) - 1)
    def _():
        for q in range(ns, 0, -1):
            @pl.when(nused >= q)
            def _(q=q):
                out_copy(nused - q).wait()


def _expert_ffn(first_blk, nblk, nused, xs, w_gate, w_up, w_down):
    P = xs.shape[0]
    bm = EXPERT_BLOCK
    w_map = lambda e, *_: (e, 0, 0)
    grid_spec = pltpu.PrefetchScalarGridSpec(
        num_scalar_prefetch=3,
        grid=(w_gate.shape[0],),
        in_specs=[pl.BlockSpec(memory_space=pl.ANY),
                  pl.BlockSpec((1, D_MODEL, EXPERT_FF), w_map),
                  pl.BlockSpec((1, D_MODEL, EXPERT_FF), w_map),
                  pl.BlockSpec((1, EXPERT_FF, D_MODEL), w_map)],
        out_specs=pl.BlockSpec(memory_space=pl.ANY),
        scratch_shapes=[pltpu.VMEM((EXPERT_SLOTS, bm, HALF), jnp.uint32),
                        pltpu.VMEM((EXPERT_SLOTS, bm, HALF), jnp.uint32),
                        pltpu.SemaphoreType.DMA((EXPERT_SLOTS,)),
                        pltpu.SemaphoreType.DMA((EXPERT_SLOTS,)),
                        pltpu.VMEM((D_MODEL, EXPERT_FF), BF16),
                        pltpu.VMEM((D_MODEL, EXPERT_FF), BF16),
                        pltpu.VMEM((EXPERT_FF, D_MODEL), BF16)],
    )
    return pl.pallas_call(
        _ffn_kernel,
        grid_spec=grid_spec,
        out_shape=jax.ShapeDtypeStruct((P, HALF), jnp.uint32),
        compiler_params=pltpu.CompilerParams(
            dimension_semantics=("arbitrary",), vmem_limit_bytes=VMEM_LIMIT),
        name="routed_experts",
    )(first_blk, nblk, nused, xs, w_gate, w_up, w_down)


def _final_kernel(yg_ref, w_ref, h2_ref, x1_ref, mod_ref, wsg_ref, wsu_ref, wsd_ref, gpost_ref, *rest):
    o_ref = rest[-1]
    lo, hi = _unpack_pair(h2_ref[...])
    h2 = jnp.concatenate([lo.astype(BF16), hi.astype(BF16)], axis=1)
    gate = jnp.dot(h2, wsg_ref[...], preferred_element_type=F32)
    up = jnp.dot(h2, wsu_ref[...], preferred_element_type=F32)
    shared = jnp.dot((_silu(gate) * up).astype(BF16), wsd_ref[...], preferred_element_type=F32)
    y_lo = shared[:, :HALF]
    y_hi = shared[:, HALF:]
    for k in range(TOP_K):
        r_lo, r_hi = _unpack_pair(yg_ref[k])
        wk = w_ref[:, k:k + 1]
        y_lo = y_lo + wk * r_lo
        y_hi = y_hi + wk * r_hi
    ms = (jnp.sum(y_lo * y_lo, axis=-1, keepdims=True)
          + jnp.sum(y_hi * y_hi, axis=-1, keepdims=True)) * (1.0 / D_MODEL)
    inv = lax.rsqrt(ms + NORM_EPS)
    o_ref[:, 0:HALF] = x1_ref[:, 0:HALF] + mod_ref[0, 5:6, 0:HALF] * (y_lo * inv * gpost_ref[:, 0:HALF])
    o_ref[:, HALF:] = x1_ref[:, HALF:] + mod_ref[0, 5:6, HALF:] * (y_hi * inv * gpost_ref[:, HALF:])


def _final(yg, w_tk, h2p, x1, mod3, wsg, wsu, wsd, g_post, seq, row0, out_prev):
    T = x1.shape[0]
    tp = yg.shape[1]
    tm = 512
    per_b = seq // tm
    off = row0 // tm
    full = lambda shape: pl.BlockSpec(shape, lambda i: (0,) * len(shape))
    in_specs = [pl.BlockSpec((TOP_K, tm, HALF), lambda i: (0, i, 0)),
                pl.BlockSpec((tm, TOP_K), lambda i: (i, 0)),
                pl.BlockSpec((tm, HALF), lambda i: (i + off, 0)),
                pl.BlockSpec((tm, D_MODEL), lambda i: (i + off, 0)),
                pl.BlockSpec((1, 6, D_MODEL), lambda i: ((i + off) // per_b, 0, 0)),
                full((D_MODEL, EXPERT_FF)), full((D_MODEL, EXPERT_FF)), full((EXPERT_FF, D_MODEL)),
                full((1, D_MODEL))]
    args = [yg, w_tk, h2p, x1, mod3, wsg, wsu, wsd, g_post]
    aliases = {}
    if out_prev is not None:
        in_specs.append(pl.BlockSpec(memory_space=pl.ANY))
        args.append(out_prev)
        aliases = {len(args) - 1: 0}
    return pl.pallas_call(
        _final_kernel,
        grid=(tp // tm,),
        in_specs=in_specs,
        out_specs=pl.BlockSpec((tm, D_MODEL), lambda i: (i + off, 0)),
        out_shape=jax.ShapeDtypeStruct((T, D_MODEL), F32),
        input_output_aliases=aliases,
        compiler_params=pltpu.CompilerParams(
            dimension_semantics=("arbitrary",), vmem_limit_bytes=VMEM_LIMIT),
        name="shared_expert_combine",
    )(*args)


def _rope_tables(positions):
    inv = jnp.power(ROPE_THETA, -jnp.arange(ROPE_HALF, dtype=F32) / ROPE_HALF)
    ang = positions.astype(F32)[..., None] * inv
    cos, sin = jnp.cos(ang), jnp.sin(ang)
    rest = ATT_HEAD_DIM - 2 * ROPE_HALF
    cs = jnp.concatenate([cos, cos, jnp.ones(ang.shape[:-1] + (rest,), F32)], axis=-1)
    sn = jnp.concatenate([-sin, sin, jnp.zeros(ang.shape[:-1] + (rest,), F32)], axis=-1)
    return jnp.tile(cs, (1, 1, 2)), jnp.tile(sn, (1, 1, 2))


def _layer(x, c, positions, w_ada, b_ada, g_pre_mix, g_post_mix, g_pre_ffn, g_post_ffn,
           w_in, conv_w, conv_b, b_gates, g_mlstm, w_branch_a, w_branch_b, w_out,
           router_w, router_bias, w_exp_gate, w_exp_up, w_exp_down, w_sh_gate, w_sh_up, w_sh_down):
    B, S, D = x.shape
    T = B * S
    H = MLSTM_HEADS
    x2 = x.reshape(T, D)

    mod3 = _adaln(c, w_ada, b_ada).reshape(B, 6, D)

    a_w = 3 * ATT_GROUP_W
    o_mq = 3 * a_w
    o_mk = o_mq + H * MLSTM_QK_DIM
    o_mv = o_mk + H * MLSTM_QK_DIM
    o_mo = o_mv + H * MLSTM_V_DIM
    o_mi = o_mo + H * MLSTM_V_DIM
    o_ga = o_mi + 2 * H
    o_gb = o_ga + D
    w_bf = w_in.astype(BF16)
    w_main = jnp.concatenate(
        [w_bf[:, o_mv:o_mi], w_bf[:, o_ga:o_gb + D], w_bf[:, o_mq:o_mv], w_bf[:, 0:o_mq]], axis=1)
    w_if = w_bf[:, o_mi:o_ga].T

    proj, gates = _in_proj(x2, mod3, g_pre_mix.reshape(1, D), w_main, w_if, S)
    proj3 = proj.reshape(B, S, PROJ_W)

    cs, sn = _rope_tables(positions)
    y_a = _attention(proj3, cs, sn)

    bg_row = jnp.pad(b_gates.reshape(1, 2 * H), ((0, 0), (0, 128 - 2 * H)))
    gates_t = gates.reshape(2 * H, B, S // MLSTM_BLOCK, MLSTM_BLOCK)
    y_b = _mlstm(proj3, gates_t, conv_w, conv_b.reshape(1, -1), bg_row, g_mlstm.reshape(1, -1))

    x1, h2p = _merge(y_a.reshape(T, ATT_GROUP_W), y_b.reshape(T, D), proj, x2, mod3,
                     w_branch_a.astype(BF16), w_branch_b.astype(BF16), w_out.astype(BF16),
                     g_post_mix.reshape(1, D), g_pre_ffn.reshape(1, D), S)

    rw_t = router_w.T.astype(BF16)
    bias_col = jnp.broadcast_to(router_bias.reshape(N_EXPERTS, 1), (N_EXPERTS, 128))
    wsg, wsu, wsd = w_sh_gate.astype(BF16), w_sh_up.astype(BF16), w_sh_down.astype(BF16)

    tp = T // MOE_PARTS
    bm = EXPERT_BLOCK
    nb = (tp * TOP_K) // bm + N_EXPERTS
    out = None
    for part in range(MOE_PARTS):
        row0 = part * tp
        idx, wts, rank, cnt = _router(h2p, rw_t[:, :HALF], rw_t[:, HALF:], bias_col, row0, tp)

        counts = cnt[:, 0].astype(jnp.int32)
        padded = (counts + bm - 1) // bm * bm
        pend = jnp.cumsum(padded)
        pstart = pend - padded
        pstart_col = jnp.broadcast_to(pstart.astype(F32).reshape(N_EXPERTS, 1), (N_EXPERTS, 128))
        dest = _slot_index(idx, rank, pstart_col)
        nused = (pend[-1] // bm).astype(jnp.int32).reshape(1)

        xs = _dispatch(h2p, dest, nb * bm, row0)
        ys = _expert_ffn((pstart // bm).astype(jnp.int32), (padded // bm).astype(jnp.int32), nused,
                         xs, w_exp_gate, w_exp_up, w_exp_down)
        yg = _collect(ys, dest)
        out = _final(yg, wts.T, h2p, x1, mod3, wsg, wsu, wsd, g_post_ffn.reshape(1, D), S, row0, out)
    return out.reshape(B, S, D)


SC_CORES = 2
SC_SUBCORES = 16
SC_WORKERS = SC_CORES * SC_SUBCORES
SC_ROWS = 64


def _sc_mesh():
    return plsc.VectorSubcoreMesh(core_axis_name="c", subcore_axis_name="s",
                                  num_cores=SC_CORES, num_subcores=SC_SUBCORES)


def _worker_id():
    return lax.axis_index("s") * SC_CORES + lax.axis_index("c")


def _dispatch(h2p, dest, n_slots, row0):
    T = dest.shape[1]
    per_w = T // SC_WORKERS
    nch = per_w // SC_ROWS
    idx = dest.reshape(TOP_K, SC_WORKERS, nch, SC_ROWS).transpose(1, 2, 0, 3)
    idx = idx.reshape(SC_WORKERS, nch * TOP_K, SC_ROWS)

    def body(x_hbm, idx_hbm, xs_hbm, idx_v, buf0, buf1, rsem0, rsem1, ssem0, ssem1):
        wid = _worker_id()
        base = row0 + wid * per_w
        pltpu.sync_copy(idx_hbm.at[wid], idx_v)
        bufs = ((buf0, rsem0, ssem0), (buf1, rsem1, ssem1))

        def read(c, buf, rsem):
            return pltpu.make_async_copy(x_hbm.at[pl.ds(base + c * SC_ROWS, SC_ROWS)], buf, rsem)

        def scatter(c, k, buf, ssem):
            return pltpu.make_async_copy(buf, xs_hbm.at[idx_v.at[c * TOP_K + k]], ssem)

        read(0, buf0, rsem0).start()

        @pl.loop(0, nch, step=2)
        def _(c0):
            for b in range(2):
                c = c0 + b
                buf, rsem, ssem = bufs[b]
                obuf, orsem, ossem = bufs[1 - b]
                read(c, buf, rsem).wait()

                @pl.when(c > 0)
                def _():
                    for k in range(TOP_K):
                        scatter(c - 1, k, obuf, ossem).wait()

                @pl.when(c + 1 < nch)
                def _():
                    read(c + 1, obuf, orsem).start()

                for k in range(TOP_K):
                    scatter(c, k, buf, ssem).start()

        for k in range(TOP_K):
            scatter(nch - 1, k, buf1, ssem1).wait()

    run = pl.kernel(
        body,
        out_type=jax.ShapeDtypeStruct((n_slots, HALF), jnp.uint32),
        mesh=_sc_mesh(),
        scratch_types=[pltpu.VMEM((nch * TOP_K, SC_ROWS), jnp.int32),
                       pltpu.VMEM((SC_ROWS, HALF), jnp.uint32),
                       pltpu.VMEM((SC_ROWS, HALF), jnp.uint32),
                       pltpu.SemaphoreType.DMA, pltpu.SemaphoreType.DMA,
                       pltpu.SemaphoreType.DMA, pltpu.SemaphoreType.DMA],
        name="sc_dispatch",
    )
    return run(h2p, idx)


def _collect(ys, dest):
    n = dest.size
    per_w = n // SC_WORKERS
    nch = per_w // SC_ROWS
    idx = dest.reshape(SC_WORKERS, nch, SC_ROWS)

    def body(ys_hbm, idx_hbm, out_hbm, idx_v, buf0, buf1, gsem0, gsem1, wsem0, wsem1):
        wid = _worker_id()
        base = wid * per_w
        pltpu.sync_copy(idx_hbm.at[wid], idx_v)
        bufs = ((buf0, gsem0, wsem0), (buf1, gsem1, wsem1))

        def gather(c, buf, gsem):
            return pltpu.make_async_copy(ys_hbm.at[idx_v.at[c]], buf, gsem)

        def write(c, buf, wsem):
            return pltpu.make_async_copy(buf, out_hbm.at[pl.ds(base + c * SC_ROWS, SC_ROWS)], wsem)

        gather(0, buf0, gsem0).start()

        @pl.loop(0, nch, step=2)
        def _(c0):
            for b in range(2):
                c = c0 + b
                buf, gsem, wsem = bufs[b]
                obuf, ogsem, owsem = bufs[1 - b]
                gather(c, buf, gsem).wait()

                @pl.when(c > 0)
                def _():
                    write(c - 1, obuf, owsem).wait()

                @pl.when(c + 1 < nch)
                def _():
                    gather(c + 1, obuf, ogsem).start()

                write(c, buf, wsem).start()

        write(nch - 1, buf1, wsem1).wait()

    run = pl.kernel(
        body,
        out_type=jax.ShapeDtypeStruct((n, HALF), jnp.uint32),
        mesh=_sc_mesh(),
        scratch_types=[pltpu.VMEM((nch, SC_ROWS), jnp.int32),
                       pltpu.VMEM((SC_ROWS, HALF), jnp.uint32),
                       pltpu.VMEM((SC_ROWS, HALF), jnp.uint32),
                       pltpu.SemaphoreType.DMA, pltpu.SemaphoreType.DMA,
                       pltpu.SemaphoreType.DMA, pltpu.SemaphoreType.DMA],
        name="sc_collect",
    )
    return run(ys, idx).reshape(dest.shape + (HALF,))


def kernel(x, c, positions, w_ada, b_ada, g_pre_mix, g_post_mix, g_pre_ffn, g_post_ffn, w_in, conv_w, conv_b, b_gates, g_mlstm, w_branch_a, w_branch_b, w_out, router_w, router_bias, w_exp_gate, w_exp_up, w_exp_down, w_sh_gate, w_sh_up, w_sh_down):
    depth = w_ada.shape[0]
    for l in range(depth):
        x = _layer(x, c, positions, w_ada[l], b_ada[l], g_pre_mix[l], g_post_mix[l], g_pre_ffn[l],
                   g_post_ffn[l], w_in[l], conv_w[l], conv_b[l], b_gates[l], g_mlstm[l],
                   w_branch_a[l], w_branch_b[l], w_out[l], router_w[l], router_bias[l],
                   w_exp_gate[l], w_exp_up[l], w_exp_down[l], w_sh_gate[l], w_sh_up[l], w_sh_down[l])
    return x
```

```python
import functools

import jax
import jax.numpy as jnp
from jax import lax
from jax.experimental import pallas as pl
from jax.experimental.pallas import tpu as pltpu
from jax.experimental.pallas import tpu_sc as plsc

F32 = jnp.float32
BF16 = jnp.bfloat16
HIGHEST = lax.Precision.HIGHEST
LANES = 128

D_MODEL = 1024
ATT_GROUPS = ((128, 1), (512, 4), (2048, 16))
ATT_HEAD_DIM = 64
ATT_GROUP_W = 256
ATT_BLK = 128
ATT_PAIR = 2
ROPE_THETA = 500000.0
ROPE_HALF = 8
MLSTM_HEADS = 4
MLSTM_QK_DIM = 128
MLSTM_V_DIM = 256
MLSTM_BLOCK = 128
MLSTM_GROUP = 8
CONV_WIDTH = 4
N_EXPERTS = 256
TOP_K = 8
N_GROUPS = 8
TOPK_GROUPS = 4
EXPERT_FF = 256
ROUTED_SCALE = 2.5
NORM_EPS = 1e-6
NEG = -1e30

OFF_MV, OFF_MO, OFF_GA, OFF_GB = 0, 1024, 2048, 3072
OFF_MQ, OFF_MK = 4096, 4608
OFF_AQ, OFF_AK, OFF_AV = 5120, 5888, 6656
PROJ_W = 7424
HALF = D_MODEL // 2

EXPERT_BLOCK = 512
EXPERT_SLOTS = 6
MOE_PARTS = 2
MERGE_SPLIT = 2
VMEM_LIMIT = 56 * 1024 * 1024


def _nt(a, b):
    return lax.dot_general(a, b, (((1,), (1,)), ((), ())), preferred_element_type=F32)


def _tn(a, b):
    return lax.dot_general(a, b, (((0,), (0,)), ((), ())), preferred_element_type=F32)


_sigmoid = jax.nn.sigmoid


def _silu(x):
    return x * _sigmoid(x)


def _pack_pair(lo, hi):
    lo_b = pltpu.bitcast(lo.astype(BF16).astype(F32), jnp.uint32)
    hi_b = pltpu.bitcast(hi.astype(BF16).astype(F32), jnp.uint32)
    return (lo_b >> 16) | (hi_b & jnp.uint32(0xFFFF0000))


def _unpack_pair(w):
    lo = pltpu.bitcast(w << 16, F32)
    hi = pltpu.bitcast(w & jnp.uint32(0xFFFF0000), F32)
    return lo, hi


def _mod_kernel(c_ref, w_ref, b_ref, o_ref):
    a = _silu(c_ref[...])
    o_ref[...] = jnp.dot(a, w_ref[...], preferred_element_type=F32, precision=HIGHEST) + b_ref[...]


def _adaln(c, w_ada, b_ada):
    B = c.shape[0]
    n = w_ada.shape[1]
    tn = 512
    return pl.pallas_call(
        _mod_kernel,
        grid=(n // tn,),
        in_specs=[pl.BlockSpec((B, D_MODEL), lambda j: (0, 0)),
                  pl.BlockSpec((D_MODEL, tn), lambda j: (0, j)),
                  pl.BlockSpec((1, tn), lambda j: (0, j))],
        out_specs=pl.BlockSpec((B, tn), lambda j: (0, j)),
        out_shape=jax.ShapeDtypeStruct((B, n), F32),
        name="adaln_mod",
    )(c, w_ada, b_ada.reshape(1, n))


def _proj_kernel(x_ref, mod_ref, g_ref, w_ref, wif_ref, o_ref, gates_ref, h_ref):
    @pl.when(pl.program_id(1) == 0)
    def _():
        x = x_ref[...]
        ms = jnp.mean(x * x, axis=-1, keepdims=True)
        y = x * lax.rsqrt(ms + NORM_EPS) * g_ref[...]
        h = (y * (1.0 + mod_ref[0, 1:2, :]) + mod_ref[0, 0:1, :]).astype(BF16)
        h_ref[...] = h
        gates_ref[...] = _nt(wif_ref[...], h)

    o_ref[...] = jnp.dot(h_ref[...], w_ref[...], preferred_element_type=F32).astype(BF16)


def _in_proj(x2, mod3, g_pre, w_main, w_if, seq):
    T = x2.shape[0]
    tm, tn = 1024, PROJ_W // 2
    per_b = seq // tm
    return pl.pallas_call(
        _proj_kernel,
        grid=(T // tm, PROJ_W // tn),
        in_specs=[pl.BlockSpec((tm, D_MODEL), lambda i, j: (i, 0)),
                  pl.BlockSpec((1, 6, D_MODEL), lambda i, j: (i // per_b, 0, 0)),
                  pl.BlockSpec((1, D_MODEL), lambda i, j: (0, 0)),
                  pl.BlockSpec((D_MODEL, tn), lambda i, j: (0, j)),
                  pl.BlockSpec((2 * MLSTM_HEADS, D_MODEL), lambda i, j: (0, 0))],
        out_specs=[pl.BlockSpec((tm, tn), lambda i, j: (i, j)),
                   pl.BlockSpec((2 * MLSTM_HEADS, tm), lambda i, j: (0, i))],
        out_shape=[jax.ShapeDtypeStruct((T, PROJ_W), BF16),
                   jax.ShapeDtypeStruct((2 * MLSTM_HEADS, T), F32)],
        scratch_shapes=[pltpu.VMEM((tm, D_MODEL), BF16)],
        compiler_params=pltpu.CompilerParams(
            dimension_semantics=("arbitrary", "arbitrary"), vmem_limit_bytes=VMEM_LIMIT),
        name="norm_in_proj",
    )(x2, mod3, g_pre, w_main, w_if)


def _attn_kernel(q_ref, k_ref, v_ref, cs_ref, sn_ref, o_ref, qf, kf, vf, acc, m_s, l_s, *, seq):
    g = pl.program_id(1)
    lane = lax.broadcasted_iota(jnp.int32, (ATT_BLK, LANES), 1)
    first = (lane % ATT_HEAD_DIM) < ROPE_HALF
    low_head = lane < ATT_HEAD_DIM

    def rope(x, cs, sn):
        partner = jnp.where(first, pltpu.roll(x, LANES - ROPE_HALF, 1), pltpu.roll(x, ROPE_HALF, 1))
        return x * cs + partner * sn

    def zero_pad(i, _):
        rows = pl.ds(pl.multiple_of(i * ATT_BLK, ATT_BLK), ATT_BLK)
        for hp in range(2):
            kf[hp, rows, :] = jnp.zeros((ATT_BLK, LANES), F32)
            vf[hp, rows, :] = jnp.zeros((ATT_BLK, LANES), F32)
        return 0

    lax.fori_loop(0, seq // ATT_BLK, zero_pad, 0)

    def stage(i, _):
        r = pl.multiple_of(i * ATT_BLK, ATT_BLK)
        rows = pl.ds(r, ATT_BLK)
        prow = pl.ds(pl.multiple_of(seq + i * ATT_BLK, ATT_BLK), ATT_BLK)
        cs = cs_ref[0, rows, :]
        sn = sn_ref[0, rows, :]
        for hp in range(2):
            cols = pl.ds(hp * LANES, LANES)
            qf[hp, rows, :] = rope(q_ref[0, rows, cols].astype(F32), cs, sn) * (ATT_HEAD_DIM ** -0.5)
            kf[hp, prow, :] = rope(k_ref[0, rows, cols].astype(F32), cs, sn)
            vf[hp, prow, :] = v_ref[0, rows, cols].astype(F32)
        return 0

    lax.fori_loop(0, seq // ATT_BLK, stage, 0)

    qi = lax.broadcasted_iota(jnp.int32, (ATT_BLK, 2 * ATT_BLK), 0)
    ki = lax.broadcasted_iota(jnp.int32, (ATT_BLK, 2 * ATT_BLK), 1)
    band = (ki >= qi) & (ki <= qi + ATT_BLK)

    def process(d, init):
        span = ATT_BLK * d
        single = seq == span

        def body(cp, _):
            blocks = [cp * ATT_PAIR + i for i in range(ATT_PAIR)]
            qrows, krows, valid = [], [], []
            for c in blocks:
                rho = c % d
                n = c // d
                qstart = rho + n * span
                if single:
                    kstart, nk = seq + qstart, ATT_BLK
                    valid.append(band[:, ATT_BLK:])
                else:
                    kstart, nk = seq + qstart - span, 2 * ATT_BLK
                    valid.append(band & (ki >= jnp.where(n > 0, 0, ATT_BLK)))
                qrows.append(pl.ds(qstart, ATT_BLK, stride=d) if d > 1 else pl.ds(qstart, ATT_BLK))
                krows.append(pl.ds(kstart, nk, stride=d) if d > 1 else pl.ds(kstart, nk))
            units = [(b, hp) for b in range(ATT_PAIR) for hp in range(2)]
            heads = [(u, hh) for u in range(len(units)) for hh in range(2)]
            q2 = [qf[hp, qrows[b], :] for b, hp in units]
            k2 = [kf[hp, krows[b], :].astype(BF16) for b, hp in units]
            v2 = [vf[hp, krows[b], :].astype(BF16) for b, hp in units]
            qh = [jnp.where(low_head if hh == 0 else jnp.logical_not(low_head), q2[u], 0.0).astype(BF16)
                  for u, hh in heads]
            s = [jnp.where(valid[units[u][0]], _nt(qh[i], k2[u]), NEG) for i, (u, hh) in enumerate(heads)]
            m = [jnp.max(x, axis=1, keepdims=True) for x in s]
            p = [jnp.exp(x - mx) for x, mx in zip(s, m)]
            l = [jnp.sum(x, axis=1, keepdims=True) for x in p]
            o = [jnp.dot(p[i].astype(BF16), v2[u], preferred_element_type=F32)
                 for i, (u, hh) in enumerate(heads)]
            for u, (b, hp) in enumerate(units):
                o_b = jnp.where(low_head, o[2 * u], o[2 * u + 1])
                m_b = jnp.where(low_head, m[2 * u], m[2 * u + 1])
                l_b = jnp.where(low_head, l[2 * u], l[2 * u + 1])
                if init:
                    acc[hp, qrows[b], :] = o_b
                    m_s[hp, qrows[b], :] = m_b
                    l_s[hp, qrows[b], :] = l_b
                else:
                    m_old = m_s[hp, qrows[b], :]
                    m_new = jnp.maximum(m_old, m_b)
                    a_old = jnp.exp(m_old - m_new)
                    a_new = jnp.exp(m_b - m_new)
                    acc[hp, qrows[b], :] = acc[hp, qrows[b], :] * a_old + o_b * a_new
                    l_s[hp, qrows[b], :] = l_s[hp, qrows[b], :] * a_old + l_b * a_new
                    m_s[hp, qrows[b], :] = m_new
            return 0

        lax.fori_loop(0, seq // (ATT_BLK * ATT_PAIR), body, 0)

    for gi, (_, d) in enumerate(ATT_GROUPS):
        @pl.when(g == gi)
        def _(d=d, gi=gi):
            process(d, gi == 0)

    @pl.when(g == len(ATT_GROUPS) - 1)
    def _():
        def fin(i, _):
            rows = pl.ds(pl.multiple_of(i * ATT_BLK, ATT_BLK), ATT_BLK)
            for hp in range(2):
                o_ref[0, rows, pl.ds(hp * LANES, LANES)] = (acc[hp, rows, :] / l_s[hp, rows, :]).astype(BF16)
            return 0

        lax.fori_loop(0, seq // ATT_BLK, fin, 0)


def _attention(proj3, cs, sn):
    B, S, _ = proj3.shape
    ng = len(ATT_GROUPS)
    qb, kb, vb = OFF_AQ // ATT_GROUP_W, OFF_AK // ATT_GROUP_W, OFF_AV // ATT_GROUP_W
    return pl.pallas_call(
        functools.partial(_attn_kernel, seq=S),
        grid=(B, ng),
        in_specs=[pl.BlockSpec((1, S, ATT_GROUP_W), lambda b, g: (b, 0, qb + g)),
                  pl.BlockSpec((1, S, ATT_GROUP_W), lambda b, g: (b, 0, kb + g)),
                  pl.BlockSpec((1, S, ATT_GROUP_W), lambda b, g: (b, 0, vb + g)),
                  pl.BlockSpec((1, S, LANES), lambda b, g: (b, 0, 0)),
                  pl.BlockSpec((1, S, LANES), lambda b, g: (b, 0, 0))],
        out_specs=pl.BlockSpec((1, S, ATT_GROUP_W), lambda b, g: (b, 0, 0)),
        out_shape=jax.ShapeDtypeStruct((B, S, ATT_GROUP_W), BF16),
        scratch_shapes=[pltpu.VMEM((2, S, LANES), F32),
                        pltpu.VMEM((2, 2 * S, LANES), F32),
                        pltpu.VMEM((2, 2 * S, LANES), F32),
                        pltpu.VMEM((2, S, LANES), F32),
                        pltpu.VMEM((2, S, LANES), F32),
                        pltpu.VMEM((2, S, LANES), F32)],
        compiler_params=pltpu.CompilerParams(
            dimension_semantics=("arbitrary", "arbitrary"), vmem_limit_bytes=VMEM_LIMIT),
        name="dilated_attention",
    )(proj3, proj3, proj3, cs, sn)


def _log_sigmoid(x):
    return jnp.minimum(x, 0.0) - jnp.log(1.0 + jnp.exp(-jnp.abs(x)))


def _mlstm_kernel(mq_ref, mk_ref, mv_ref, mo_ref, gt_ref, cwq_ref, cwk_ref, cbq_ref, cbk_ref,
                  bg_ref, gm_ref, o_ref, q_s, k_s, va_s, rows_s, acc_s, kv_s, inter_s, emt_s,
                  c_s, *, seq):
    h = pl.program_id(1)
    L = MLSTM_BLOCK
    NC = seq // L
    DK, DV = MLSTM_QK_DIM, MLSTM_V_DIM
    DA = DV + LANES
    nshift = CONV_WIDTH - 1

    tt = lax.broadcasted_iota(jnp.int32, (nshift * L, 2 * L), 0)
    uu = lax.broadcasted_iota(jnp.int32, (nshift * L, 2 * L), 1)
    shift_mat = (uu == L + tt % L - (tt // L + 1)).astype(BF16)
    conv_w = jnp.concatenate([cwq_ref[...], cwk_ref[...]], axis=1)
    conv_b = jnp.concatenate([cbq_ref[...], cbk_ref[...]], axis=1)
    prev = jnp.zeros((L, 2 * DK), BF16)
    for i in range(NC):
        blk = slice(i * L, (i + 1) * L)
        va_s[blk, 0:DV] = mv_ref[0, blk, :]
        va_s[blk, DV:DA] = jnp.ones((L, DA - DV), BF16)
        cur = jnp.concatenate([mq_ref[0, blk, :], mk_ref[0, blk, :]], axis=1)
        shifted = jnp.dot(shift_mat, jnp.concatenate([prev, cur], axis=0),
                          preferred_element_type=F32)
        y = conv_b + cur.astype(F32) * conv_w[nshift:nshift + 1, :]
        for s in range(nshift):
            y = y + shifted[s * L:(s + 1) * L, :] * conv_w[nshift - 1 - s:nshift - s, :]
        y = _silu(y)
        q_s[blk, :] = y[:, 0:DK].astype(BF16)
        k_s[blk, :] = (y[:, DK:2 * DK] * (DK ** -0.5)).astype(BF16)
        prev = cur

    lane = lax.broadcasted_iota(jnp.int32, (1, LANES), 1)
    bias = bg_ref[...]
    b_i = jnp.sum(jnp.where(lane == h, bias, 0.0), axis=1, keepdims=True)
    b_f = jnp.sum(jnp.where(lane == h + MLSTM_HEADS, bias, 0.0), axis=1, keepdims=True)
    ri = lax.broadcasted_iota(jnp.int32, (L, L), 0)
    ci = lax.broadcasted_iota(jnp.int32, (L, L), 1)
    causal = ci <= ri
    eye = (ri == ci).astype(F32)
    i_rows = gt_ref[h, 0] + b_i
    lf_rows = _log_sigmoid(gt_ref[h + MLSTM_HEADS, 0] + b_f)
    b_rows = jnp.dot(lf_rows, (ri <= ci).astype(F32), preferred_element_type=F32,
                     precision=HIGHEST)
    b_end = b_rows[:, L - 1:L]
    g_rows = b_end - b_rows + i_rows
    g_max = jnp.max(g_rows, axis=1, keepdims=True)
    m = jnp.zeros((1, 1), F32)
    m_prev, m_new = [], []
    for c in range(NC):
        m_prev.append(m)
        m = jnp.maximum(b_end[c:c + 1, :] + m, g_max[c:c + 1, :])
        m_new.append(m)
    m_prev = jnp.concatenate(m_prev, axis=0)
    m_new = jnp.concatenate(m_new, axis=0)
    rows_s[0] = b_rows
    rows_s[1] = jnp.exp(g_rows - m_new)
    rows_s[2] = b_rows - i_rows
    rows_s[3] = jnp.broadcast_to(m_prev, (NC, L))
    rows_s[4] = jnp.broadcast_to(jnp.exp(b_end + m_prev - m_new), (NC, L))

    r2 = lax.broadcasted_iota(jnp.int32, (2 * L, 2 * L), 0)
    c2 = lax.broadcasted_iota(jnp.int32, (2 * L, 2 * L), 1)
    ones_blk = ((r2 < L) == (c2 < L)).astype(BF16)

    G = MLSTM_GROUP

    def local(cg, _):
        cs = [cg * G + i for i in range(G)]
        rows = [pl.ds(pl.multiple_of(c * L, L), L) for c in cs]
        b_r = [rows_s[0, pl.ds(c, 1), :] for c in cs]
        w_r = [rows_s[1, pl.ds(c, 1), :] for c in cs]
        u_r = [rows_s[2, pl.ds(c, 1), :] for c in cs]
        mp = [rows_s[3, pl.ds(c, 1), :] for c in cs]
        q = [q_s[r, :] for r in rows]
        k = [k_s[r, :] for r in rows]
        va = [va_s[r, :] for r in rows]
        qk = [_nt(a, b) for a, b in zip(q, k)]
        x2 = [jnp.concatenate([eye * a, eye * b], axis=1) for a, b in zip(b_r, w_r)]
        hi = [x.astype(BF16) for x in x2]
        lo = [(x - h_.astype(F32)).astype(BF16) for x, h_ in zip(x2, hi)]
        yb = [jnp.dot(h_, ones_blk, preferred_element_type=F32)
              + jnp.dot(l_, ones_blk, preferred_element_type=F32) for h_, l_ in zip(hi, lo)]
        b_b = [y[:, 0:L] for y in yb]
        w_b = [y[:, L:2 * L] for y in yb]
        for i in range(G):
            kv_s[cs[i]] = _tn((w_b[i] * k[i].astype(F32)).astype(BF16), va[i])
        dmat = [jnp.where(causal, b - u, NEG) for b, u in zip(b_b, u_r)]
        m_t = [jnp.maximum(b + m_, jnp.max(d, axis=1, keepdims=True))
               for b, m_, d in zip(b_b, mp, dmat)]
        sc = [a * jnp.exp(d - m_) for a, d, m_ in zip(qk, dmat, m_t)]
        for i in range(G):
            acc_s[rows[i], :] = jnp.dot(sc[i].astype(BF16), va[i], preferred_element_type=F32)
            inter_s[rows[i], :] = jnp.exp(b_b[i] + mp[i] - m_t[i])
            emt_s[rows[i], :] = jnp.exp(-m_t[i])
        return 0

    lax.fori_loop(0, NC // G, local, 0)

    g_row = gm_ref[...]
    c_s[...] = jnp.zeros((DK, DA), F32)

    def recur(cg, _):
        cs = [cg * G + i for i in range(G)]
        rows = [pl.ds(pl.multiple_of(c * L, L), L) for c in cs]
        states = [c_s[...]]
        for c in cs:
            dec = rows_s[4, pl.ds(c, 1), :]
            states.append(jnp.concatenate([dec, dec, dec], axis=1) * states[-1] + kv_s[c])
        c_s[...] = states[G]
        read = [jnp.dot(q_s[r, :], st.astype(BF16), preferred_element_type=F32)
                for r, st in zip(rows, states)]
        inter = [inter_s[r, :] for r in rows]
        out = [acc_s[r, :] + jnp.concatenate([it, it, it], axis=1) * rd
               for r, it, rd in zip(rows, inter, read)]
        emt = [emt_s[r, :] for r in rows]
        nrm = [jnp.maximum(jnp.abs(jnp.concatenate([o[:, DV:DA], o[:, DV:DA]], axis=1)),
                           jnp.concatenate([e_, e_], axis=1)) for o, e_ in zip(out, emt)]
        hh = [o[:, 0:DV] / n_ for o, n_ in zip(out, nrm)]
        ms = [jnp.mean(x * x, axis=1, keepdims=True) for x in hh]
        hn = [x * lax.rsqrt(m_ + NORM_EPS) * g_row for x, m_ in zip(hh, ms)]
        for i in range(G):
            o_ref[0, rows[i], :] = (hn[i] * _sigmoid(mo_ref[0, rows[i], :].astype(F32))).astype(BF16)
        return 0

    lax.fori_loop(0, NC // G, recur, 0)


def _mlstm(proj3, gates_t, conv_w, conv_b, bg_row, g_mlstm):
    B, S, _ = proj3.shape
    H, DK, DV = MLSTM_HEADS, MLSTM_QK_DIM, MLSTM_V_DIM
    L = MLSTM_BLOCK
    NC = S // L
    DA = DV + LANES
    qb, kb = OFF_MQ // DK, OFF_MK // DK
    vb, ob = OFF_MV // DV, OFF_MO // DV
    nq = H
    return pl.pallas_call(
        functools.partial(_mlstm_kernel, seq=S),
        grid=(B, H),
        in_specs=[pl.BlockSpec((1, S, DK), lambda b, h: (b, 0, qb + h)),
                  pl.BlockSpec((1, S, DK), lambda b, h: (b, 0, kb + h)),
                  pl.BlockSpec((1, S, DV), lambda b, h: (b, 0, vb + h)),
                  pl.BlockSpec((1, S, DV), lambda b, h: (b, 0, ob + h)),
                  pl.BlockSpec((2 * H, 1, NC, L), lambda b, h: (0, b, 0, 0)),
                  pl.BlockSpec((CONV_WIDTH, DK), lambda b, h: (0, h)),
                  pl.BlockSpec((CONV_WIDTH, DK), lambda b, h: (0, nq + h)),
                  pl.BlockSpec((1, DK), lambda b, h: (0, h)),
                  pl.BlockSpec((1, DK), lambda b, h: (0, nq + h)),
                  pl.BlockSpec((1, LANES), lambda b, h: (0, 0)),
                  pl.BlockSpec((1, DV), lambda b, h: (0, h))],
        out_specs=pl.BlockSpec((1, S, DV), lambda b, h: (b, 0, h)),
        out_shape=jax.ShapeDtypeStruct((B, S, H * DV), BF16),
        scratch_shapes=[pltpu.VMEM((S, DK), BF16),
                        pltpu.VMEM((S, DK), BF16),
                        pltpu.VMEM((S, DA), BF16),
                        pltpu.VMEM((5, NC, L), F32),
                        pltpu.VMEM((S, DA), F32),
                        pltpu.VMEM((NC, DK, DA), F32),
                        pltpu.VMEM((S, L), F32),
                        pltpu.VMEM((S, L), F32),
                        pltpu.VMEM((DK, DA), F32)],
        compiler_params=pltpu.CompilerParams(
            dimension_semantics=("arbitrary", "arbitrary"), vmem_limit_bytes=VMEM_LIMIT),
        name="mlstm_chunkwise",
    )(proj3, proj3, proj3, proj3, gates_t, conv_w, conv_w, conv_b, conv_b, bg_row, g_mlstm)


def _rms(y, g):
    ms = jnp.mean(y * y, axis=-1, keepdims=True)
    return y * lax.rsqrt(ms + NORM_EPS) * g


def _merge_kernel(ya_ref, yb_ref, ga_ref, gb_ref, x_ref, mod_ref, wa_ref, wb_ref, wo_ref,
                  gpost_ref, gpre_ref, x1_ref, h2_ref):
    tm = x_ref.shape[0]
    slabs = [pl.ds(s * (tm // MERGE_SPLIT), tm // MERGE_SPLIT) for s in range(MERGE_SPLIT)]
    pa = [jnp.dot(ya_ref[r, :], wa_ref[...], preferred_element_type=F32) for r in slabs]
    pb = [jnp.dot(yb_ref[r, :], wb_ref[...], preferred_element_type=F32) for r in slabs]
    merged = [_sigmoid(ga_ref[r, :].astype(F32)) * a + _sigmoid(gb_ref[r, :].astype(F32)) * b
              for r, a, b in zip(slabs, pa, pb)]
    y = [jnp.dot(m.astype(BF16), wo_ref[...], preferred_element_type=F32) for m in merged]
    x1 = [x_ref[r, :] + mod_ref[0, 2:3, :] * _rms(v, gpost_ref[...]) for r, v in zip(slabs, y)]
    for r, v in zip(slabs, x1):
        x1_ref[r, :] = v
    h2 = [_rms(v, gpre_ref[...]) * (1.0 + mod_ref[0, 4:5, :]) + mod_ref[0, 3:4, :] for v in x1]
    for r, v in zip(slabs, h2):
        h2_ref[r, :] = _pack_pair(v[:, :HALF], v[:, HALF:])


def _merge(ya2, yb2, proj2, x2, mod3, wa, wb, wo, g_post, g_pre, seq):
    T = x2.shape[0]
    tm = 512 * MERGE_SPLIT
    per_b = seq // tm
    full = lambda shape: pl.BlockSpec(shape, lambda i: (0,) * len(shape))
    return pl.pallas_call(
        _merge_kernel,
        grid=(T // tm,),
        in_specs=[pl.BlockSpec((tm, ATT_GROUP_W), lambda i: (i, 0)),
                  pl.BlockSpec((tm, D_MODEL), lambda i: (i, 0)),
                  pl.BlockSpec((tm, D_MODEL), lambda i: (i, OFF_GA // D_MODEL)),
                  pl.BlockSpec((tm, D_MODEL), lambda i: (i, OFF_GB // D_MODEL)),
                  pl.BlockSpec((tm, D_MODEL), lambda i: (i, 0)),
                  pl.BlockSpec((1, 6, D_MODEL), lambda i: (i // per_b, 0, 0)),
                  full((ATT_GROUP_W, D_MODEL)), full((D_MODEL, D_MODEL)), full((D_MODEL, D_MODEL)),
                  full((1, D_MODEL)), full((1, D_MODEL))],
        out_specs=[pl.BlockSpec((tm, D_MODEL), lambda i: (i, 0)),
                   pl.BlockSpec((tm, HALF), lambda i: (i, 0))],
        out_shape=[jax.ShapeDtypeStruct((T, D_MODEL), F32),
                   jax.ShapeDtypeStruct((T, HALF), jnp.uint32)],
        compiler_params=pltpu.CompilerParams(
            dimension_semantics=("arbitrary",), vmem_limit_bytes=VMEM_LIMIT),
        name="merge_out_proj",
    )(ya2, yb2, proj2, proj2, x2, mod3, wa, wb, wo, g_post, g_pre)


def _router_kernel(h2_ref, rlo_ref, rhi_ref, bias_ref, idx_ref, w_ref, rank_ref, cnt_ref):
    E = N_EXPERTS
    tr = h2_ref.shape[0]
    gsz = E // N_GROUPS

    @pl.when(pl.program_id(0) == 0)
    def _():
        cnt_ref[...] = jnp.zeros(cnt_ref.shape, F32)

    lo, hi = _unpack_pair(h2_ref[...])
    logits = _nt(rlo_ref[...], lo.astype(BF16)) + _nt(rhi_ref[...], hi.astype(BF16))
    scores = _sigmoid(logits)
    sel = scores + bias_ref[:, 0:1]

    gi = lax.broadcasted_iota(jnp.int32, (gsz, tr), 0).astype(F32)
    gs_rows = []
    for g in range(N_GROUPS):
        blk = sel[g * gsz:(g + 1) * gsz, :]
        m1 = jnp.max(blk, axis=0, keepdims=True)
        a1 = jnp.min(jnp.where(blk == m1, gi, float(E)), axis=0, keepdims=True)
        m2 = jnp.max(jnp.where(gi == a1, -jnp.inf, blk), axis=0, keepdims=True)
        gs_rows.append(m1 + m2)
    gs = jnp.concatenate(gs_rows, axis=0)
    g8 = lax.broadcasted_iota(jnp.int32, (N_GROUPS, tr), 0).astype(F32)
    gmask = jnp.zeros((N_GROUPS, tr), F32)
    for _ in range(TOPK_GROUPS):
        m = jnp.max(gs, axis=0, keepdims=True)
        a = jnp.min(jnp.where(gs == m, g8, float(E)), axis=0, keepdims=True)
        hit = g8 == a
        gmask = jnp.where(hit, 1.0, gmask)
        gs = jnp.where(hit, -jnp.inf, gs)
    selm = jnp.concatenate(
        [jnp.where(gmask[g:g + 1, :] > 0.0, sel[g * gsz:(g + 1) * gsz, :], -jnp.inf)
         for g in range(N_GROUPS)], axis=0)

    ei = lax.broadcasted_iota(jnp.int32, (E, tr), 0).astype(F32)
    picks, weights, hits = [], [], []
    candidates = selm
    for _ in range(TOP_K):
        m = jnp.max(selm, axis=0, keepdims=True)
        a = jnp.min(jnp.where(selm == m, ei, float(E)), axis=0, keepdims=True)
        hit = ei == a
        picks.append(a)
        hits.append(hit)
        weights.append(jnp.sum(jnp.where(hit, scores, 0.0), axis=0, keepdims=True))
        selm = jnp.where(hit, -jnp.inf, selm)
    chosen = jnp.where(selm != candidates, 1.0, 0.0)
    wsum = weights[0]
    for w in weights[1:]:
        wsum = wsum + w

    ti = lax.broadcasted_iota(jnp.int32, (tr, tr), 0)
    tj = lax.broadcasted_iota(jnp.int32, (tr, tr), 1)
    before = (ti < tj).astype(BF16)
    pos = jnp.dot(chosen.astype(BF16), before, preferred_element_type=F32) + cnt_ref[:, 0:1]
    ranks = [jnp.sum(jnp.where(hit, pos, 0.0), axis=0, keepdims=True) for hit in hits]
    cnt_ref[...] = cnt_ref[...] + jnp.sum(chosen, axis=1, keepdims=True)

    idx_ref[...] = jnp.concatenate(picks, axis=0).astype(jnp.int32)
    w_ref[...] = jnp.concatenate([w / wsum * ROUTED_SCALE for w in weights], axis=0)
    rank_ref[...] = jnp.concatenate(ranks, axis=0).astype(jnp.int32)


def _router(h2p, r_lo, r_hi, bias_col, row0, T):
    tr = 512
    off = row0 // tr
    full = lambda shape: pl.BlockSpec(shape, lambda i: (0,) * len(shape))
    return pl.pallas_call(
        _router_kernel,
        grid=(T // tr,),
        in_specs=[pl.BlockSpec((tr, HALF), lambda i: (i + off, 0)),
                  full((N_EXPERTS, HALF)), full((N_EXPERTS, HALF)), full((N_EXPERTS, LANES))],
        out_specs=[pl.BlockSpec((TOP_K, tr), lambda i: (0, i)),
                   pl.BlockSpec((TOP_K, tr), lambda i: (0, i)),
                   pl.BlockSpec((TOP_K, tr), lambda i: (0, i)),
                   full((N_EXPERTS, LANES))],
        out_shape=[jax.ShapeDtypeStruct((TOP_K, T), jnp.int32),
                   jax.ShapeDtypeStruct((TOP_K, T), F32),
                   jax.ShapeDtypeStruct((TOP_K, T), jnp.int32),
                   jax.ShapeDtypeStruct((N_EXPERTS, LANES), F32)],
        compiler_params=pltpu.CompilerParams(
            dimension_semantics=("arbitrary",), vmem_limit_bytes=VMEM_LIMIT),
        name="router_topk",
    )(h2p, r_lo, r_hi, bias_col)


def _dest_kernel(idx_ref, rank_ref, pstart_ref, dest_ref):
    tr = idx_ref.shape[1]
    ei = lax.broadcasted_iota(jnp.int32, (N_EXPERTS, tr), 0)
    start = pstart_ref[:, 0:1]
    rows = []
    for k in range(TOP_K):
        hit = ei == idx_ref[k:k + 1, :]
        rows.append(jnp.sum(jnp.where(hit, start, 0.0), axis=0, keepdims=True))
    dest_ref[...] = jnp.concatenate(rows, axis=0).astype(jnp.int32) + rank_ref[...]


def _slot_index(idx, rank, pstart_col):
    T = idx.shape[1]
    tr = 1024
    return pl.pallas_call(
        _dest_kernel,
        grid=(T // tr,),
        in_specs=[pl.BlockSpec((TOP_K, tr), lambda i: (0, i)),
                  pl.BlockSpec((TOP_K, tr), lambda i: (0, i)),
                  pl.BlockSpec((N_EXPERTS, LANES), lambda i: (0, 0))],
        out_specs=pl.BlockSpec((TOP_K, tr), lambda i: (0, i)),
        out_shape=jax.ShapeDtypeStruct((TOP_K, T), jnp.int32),
        name="slot_index",
    )(idx, rank, pstart_col)


def _ffn_kernel(first_ref, nblk_ref, nused_ref, xs_hbm, wg_ref, wu_ref, wd_ref, ys_hbm,
                xbuf, ybuf, in_sem, out_sem, wg_s, wu_s, wd_s):
    e = pl.program_id(0)
    bm = EXPERT_BLOCK
    ns = EXPERT_SLOTS
    nused = nused_ref[0]
    first = first_ref[e]
    n = nblk_ref[e]

    def in_copy(g):
        slot = g % ns
        return pltpu.make_async_copy(xs_hbm.at[pl.ds(g * bm, bm)], xbuf.at[slot], in_sem.at[slot])

    def out_copy(g):
        slot = g % ns
        return pltpu.make_async_copy(ybuf.at[slot], ys_hbm.at[pl.ds(g * bm, bm)], out_sem.at[slot])

    def fetch(g):
        @pl.when(g < nused)
        def _():
            in_copy(g).start()

    def release(g):
        @pl.when(g >= ns)
        def _():
            out_copy(g - ns).wait()

    def ffn(g):
        lo, hi = _unpack_pair(xbuf[g % ns])
        x = jnp.concatenate([lo.astype(BF16), hi.astype(BF16)], axis=1)
        gate = jnp.dot(x, wg_s[...], preferred_element_type=F32)
        up = jnp.dot(x, wu_s[...], preferred_element_type=F32)
        hid = (_silu(gate) * up).astype(BF16)
        return jnp.dot(hid, wd_s[...], preferred_element_type=F32)

    def pack(g, out):
        ybuf[g % ns] = _pack_pair(out[:, :HALF], out[:, HALF:])

    @pl.when(e == 0)
    def _():
        for q in range(ns - 1):
            fetch(q)

    @pl.when(n > 0)
    def _():
        wg_s[...] = wg_ref[0].astype(BF16)
        wu_s[...] = wu_ref[0].astype(BF16)
        wd_s[...] = wd_ref[0].astype(BF16)

        def two_blocks(j, _):
            g = first + 2 * j
            in_copy(g).wait()
            in_copy(g + 1).wait()
            fetch(g + ns - 1)
            release(g)
            release(g + 1)
            out_a = ffn(g)
            out_b = ffn(g + 1)
            pack(g, out_a)
            pack(g + 1, out_b)
            out_copy(g).start()
            out_copy(g + 1).start()
            fetch(g + ns)
            return 0

        lax.fori_loop(0, n // 2, two_blocks, 0)

        @pl.when(n % 2 == 1)
        def _():
            g = first + n - 1
            in_copy(g).wait()
            fetch(g + ns - 1)
            release(g)
            pack(g, ffn(g))
            out_copy(g).start()

    @pl.when(e == pl.num_programs(0) - 1)
    def _():
        for q in range(ns, 0, -1):
            @pl.when(nused >= q)
            def _(q=q):
                out_copy(nused - q).wait()


def _expert_ffn(first_blk, nblk, nused, xs, w_gate, w_up, w_down):
    P = xs.shape[0]
    bm = EXPERT_BLOCK
    w_map = lambda e, *_: (e, 0, 0)
    grid_spec = pltpu.PrefetchScalarGridSpec(
        num_scalar_prefetch=3,
        grid=(w_gate.shape[0],),
        in_specs=[pl.BlockSpec(memory_space=pl.ANY),
                  pl.BlockSpec((1, D_MODEL, EXPERT_FF), w_map),
                  pl.BlockSpec((1, D_MODEL, EXPERT_FF), w_map),
                  pl.BlockSpec((1, EXPERT_FF, D_MODEL), w_map)],
        out_specs=pl.BlockSpec(memory_space=pl.ANY),
        scratch_shapes=[pltpu.VMEM((EXPERT_SLOTS, bm, HALF), jnp.uint32),
                        pltpu.VMEM((EXPERT_SLOTS, bm, HALF), jnp.uint32),
                        pltpu.SemaphoreType.DMA((EXPERT_SLOTS,)),
                        pltpu.SemaphoreType.DMA((EXPERT_SLOTS,)),
                        pltpu.VMEM((D_MODEL, EXPERT_FF), BF16),
                        pltpu.VMEM((D_MODEL, EXPERT_FF), BF16),
                        pltpu.VMEM((EXPERT_FF, D_MODEL), BF16)],
    )
    return pl.pallas_call(
        _ffn_kernel,
        grid_spec=grid_spec,
        out_shape=jax.ShapeDtypeStruct((P, HALF), jnp.uint32),
        compiler_params=pltpu.CompilerParams(
            dimension_semantics=("arbitrary",), vmem_limit_bytes=VMEM_LIMIT),
        name="routed_experts",
    )(first_blk, nblk, nused, xs, w_gate, w_up, w_down)


def _final_kernel(yg_ref, w_ref, h2_ref, x1_ref, mod_ref, wsg_ref, wsu_ref, wsd_ref, gpost_ref, *rest):
    o_ref = rest[-1]
    lo, hi = _unpack_pair(h2_ref[...])
    h2 = jnp.concatenate([lo.astype(BF16), hi.astype(BF16)], axis=1)
    gate = jnp.dot(h2, wsg_ref[...], preferred_element_type=F32)
    up = jnp.dot(h2, wsu_ref[...], preferred_element_type=F32)
    shared = jnp.dot((_silu(gate) * up).astype(BF16), wsd_ref[...], preferred_element_type=F32)
    y_lo = shared[:, :HALF]
    y_hi = shared[:, HALF:]
    for k in range(TOP_K):
        r_lo, r_hi = _unpack_pair(yg_ref[k])
        wk = w_ref[:, k:k + 1]
        y_lo = y_lo + wk * r_lo
        y_hi = y_hi + wk * r_hi
    ms = (jnp.sum(y_lo * y_lo, axis=-1, keepdims=True)
          + jnp.sum(y_hi * y_hi, axis=-1, keepdims=True)) * (1.0 / D_MODEL)
    inv = lax.rsqrt(ms + NORM_EPS)
    o_ref[:, 0:HALF] = x1_ref[:, 0:HALF] + mod_ref[0, 5:6, 0:HALF] * (y_lo * inv * gpost_ref[:, 0:HALF])
    o_ref[:, HALF:] = x1_ref[:, HALF:] + mod_ref[0, 5:6, HALF:] * (y_hi * inv * gpost_ref[:, HALF:])


def _final(yg, w_tk, h2p, x1, mod3, wsg, wsu, wsd, g_post, seq, row0, out_prev):
    T = x1.shape[0]
    tp = yg.shape[1]
    tm = 512
    per_b = seq // tm
    off = row0 // tm
    full = lambda shape: pl.BlockSpec(shape, lambda i: (0,) * len(shape))
    in_specs = [pl.BlockSpec((TOP_K, tm, HALF), lambda i: (0, i, 0)),
                pl.BlockSpec((tm, TOP_K), lambda i: (i, 0)),
                pl.BlockSpec((tm, HALF), lambda i: (i + off, 0)),
                pl.BlockSpec((tm, D_MODEL), lambda i: (i + off, 0)),
                pl.BlockSpec((1, 6, D_MODEL), lambda i: ((i + off) // per_b, 0, 0)),
                full((D_MODEL, EXPERT_FF)), full((D_MODEL, EXPERT_FF)), full((EXPERT_FF, D_MODEL)),
                full((1, D_MODEL))]
    args = [yg, w_tk, h2p, x1, mod3, wsg, wsu, wsd, g_post]
    aliases = {}
    if out_prev is not None:
        in_specs.append(pl.BlockSpec(memory_space=pl.ANY))
        args.append(out_prev)
        aliases = {len(args) - 1: 0}
    return pl.pallas_call(
        _final_kernel,
        grid=(tp // tm,),
        in_specs=in_specs,
        out_specs=pl.BlockSpec((tm, D_MODEL), lambda i: (i + off, 0)),
        out_shape=jax.ShapeDtypeStruct((T, D_MODEL), F32),
        input_output_aliases=aliases,
        compiler_params=pltpu.CompilerParams(
            dimension_semantics=("arbitrary",), vmem_limit_bytes=VMEM_LIMIT),
        name="shared_expert_combine",
    )(*args)


def _rope_tables(positions):
    inv = jnp.power(ROPE_THETA, -jnp.arange(ROPE_HALF, dtype=F32) / ROPE_HALF)
    ang = positions.astype(F32)[..., None] * inv
    cos, sin = jnp.cos(ang), jnp.sin(ang)
    rest = ATT_HEAD_DIM - 2 * ROPE_HALF
    one = jnp.ones(ang.shape[:-1] + (rest,), F32)
    zero = jnp.zeros(ang.shape[:-1] + (rest,), F32)
    cs = jnp.concatenate([cos, cos, one, cos, cos, one], axis=-1)
    sn = jnp.concatenate([-sin, sin, zero, -sin, sin, zero], axis=-1)
    return cs, sn


def _layer(x, c, positions, w_ada, b_ada, g_pre_mix, g_post_mix, g_pre_ffn, g_post_ffn,
           w_in, conv_w, conv_b, b_gates, g_mlstm, w_branch_a, w_branch_b, w_out,
           router_w, router_bias, w_exp_gate, w_exp_up, w_exp_down, w_sh_gate, w_sh_up, w_sh_down):
    B, S, D = x.shape
    T = B * S
    H = MLSTM_HEADS
    x2 = x.reshape(T, D)

    mod3 = _adaln(c, w_ada, b_ada).reshape(B, 6, D)

    a_w = 3 * ATT_GROUP_W
    o_mq = 3 * a_w
    o_mk = o_mq + H * MLSTM_QK_DIM
    o_mv = o_mk + H * MLSTM_QK_DIM
    o_mo = o_mv + H * MLSTM_V_DIM
    o_mi = o_mo + H * MLSTM_V_DIM
    o_ga = o_mi + 2 * H
    o_gb = o_ga + D
    w_bf = w_in.astype(BF16)
    w_main = jnp.concatenate(
        [w_bf[:, o_mv:o_mi], w_bf[:, o_ga:o_gb + D], w_bf[:, o_mq:o_mv], w_bf[:, 0:o_mq]], axis=1)
    w_if = w_bf[:, o_mi:o_ga].T

    proj, gates = _in_proj(x2, mod3, g_pre_mix.reshape(1, D), w_main, w_if, S)
    proj3 = proj.reshape(B, S, PROJ_W)

    cs, sn = _rope_tables(positions)
    y_a = _attention(proj3, cs, sn)

    bg_row = jnp.pad(b_gates.reshape(1, 2 * H), ((0, 0), (0, LANES - 2 * H)))
    gates_t = gates.reshape(2 * H, B, S // MLSTM_BLOCK, MLSTM_BLOCK)
    y_b = _mlstm(proj3, gates_t, conv_w, conv_b.reshape(1, -1), bg_row, g_mlstm.reshape(1, -1))

    x1, h2p = _merge(y_a.reshape(T, ATT_GROUP_W), y_b.reshape(T, D), proj, x2, mod3,
                     w_branch_a.astype(BF16), w_branch_b.astype(BF16), w_out.astype(BF16),
                     g_post_mix.reshape(1, D), g_pre_ffn.reshape(1, D), S)

    rw_t = router_w.T.astype(BF16)
    bias_col = jnp.broadcast_to(router_bias.reshape(N_EXPERTS, 1), (N_EXPERTS, LANES))
    wsg, wsu, wsd = w_sh_gate.astype(BF16), w_sh_up.astype(BF16), w_sh_down.astype(BF16)

    tp = T // MOE_PARTS
    bm = EXPERT_BLOCK
    nb = (tp * TOP_K) // bm + N_EXPERTS
    out = None
    for part in range(MOE_PARTS):
        row0 = part * tp
        idx, wts, rank, cnt = _router(h2p, rw_t[:, :HALF], rw_t[:, HALF:], bias_col, row0, tp)

        counts = cnt[:, 0].astype(jnp.int32)
        padded = (counts + bm - 1) // bm * bm
        pend = jnp.cumsum(padded)
        pstart = pend - padded
        pstart_col = jnp.broadcast_to(pstart.astype(F32).reshape(N_EXPERTS, 1), (N_EXPERTS, LANES))
        dest = _slot_index(idx, rank, pstart_col)
        nused = (pend[-1] // bm).astype(jnp.int32).reshape(1)

        xs = _dispatch(h2p, dest, nb * bm, row0)
        ys = _expert_ffn((pstart // bm).astype(jnp.int32), (padded // bm).astype(jnp.int32), nused,
                         xs, w_exp_gate, w_exp_up, w_exp_down)
        yg = _collect(ys, dest)
        out = _final(yg, wts.T, h2p, x1, mod3, wsg, wsu, wsd, g_post_ffn.reshape(1, D), S, row0, out)
    return out.reshape(B, S, D)


SC_CORES = 2
SC_SUBCORES = 16
SC_WORKERS = SC_CORES * SC_SUBCORES
SC_ROWS = 64


def _sc_mesh():
    return plsc.VectorSubcoreMesh(core_axis_name="c", subcore_axis_name="s",
                                  num_cores=SC_CORES, num_subcores=SC_SUBCORES)


def _worker_id():
    return lax.axis_index("s") * SC_CORES + lax.axis_index("c")


def _dispatch(h2p, dest, n_slots, row0):
    T = dest.shape[1]
    per_w = T // SC_WORKERS
    nch = per_w // SC_ROWS
    idx = dest.reshape(TOP_K, SC_WORKERS, nch, SC_ROWS).transpose(1, 2, 0, 3)
    idx = idx.reshape(SC_WORKERS, nch * TOP_K, SC_ROWS)

    def body(x_hbm, idx_hbm, xs_hbm, idx_v, buf0, buf1, rsem0, rsem1, ssem0, ssem1):
        wid = _worker_id()
        base = row0 + wid * per_w
        pltpu.sync_copy(idx_hbm.at[wid], idx_v)
        bufs = ((buf0, rsem0, ssem0), (buf1, rsem1, ssem1))

        def read(c, buf, rsem):
            return pltpu.make_async_copy(x_hbm.at[pl.ds(base + c * SC_ROWS, SC_ROWS)], buf, rsem)

        def scatter(c, k, buf, ssem):
            return pltpu.make_async_copy(buf, xs_hbm.at[idx_v.at[c * TOP_K + k]], ssem)

        read(0, buf0, rsem0).start()

        @pl.loop(0, nch, step=2)
        def _(c0):
            for b in range(2):
                c = c0 + b
                buf, rsem, ssem = bufs[b]
                obuf, orsem, ossem = bufs[1 - b]
                read(c, buf, rsem).wait()

                @pl.when(c > 0)
                def _():
                    for k in range(TOP_K):
                        scatter(c - 1, k, obuf, ossem).wait()

                @pl.when(c + 1 < nch)
                def _():
                    read(c + 1, obuf, orsem).start()

                for k in range(TOP_K):
                    scatter(c, k, buf, ssem).start()

        for k in range(TOP_K):
            scatter(nch - 1, k, buf1, ssem1).wait()

    run = pl.kernel(
        body,
        out_type=jax.ShapeDtypeStruct((n_slots, HALF), jnp.uint32),
        mesh=_sc_mesh(),
        scratch_types=[pltpu.VMEM((nch * TOP_K, SC_ROWS), jnp.int32),
                       pltpu.VMEM((SC_ROWS, HALF), jnp.uint32),
                       pltpu.VMEM((SC_ROWS, HALF), jnp.uint32),
                       pltpu.SemaphoreType.DMA, pltpu.SemaphoreType.DMA,
                       pltpu.SemaphoreType.DMA, pltpu.SemaphoreType.DMA],
        name="sc_dispatch",
    )
    return run(h2p, idx)


def _collect(ys, dest):
    n = dest.size
    per_w = n // SC_WORKERS
    nch = per_w // SC_ROWS
    idx = dest.reshape(SC_WORKERS, nch, SC_ROWS)

    def body(ys_hbm, idx_hbm, out_hbm, idx_v, buf0, buf1, gsem0, gsem1, wsem0, wsem1):
        wid = _worker_id()
        base = wid * per_w
        pltpu.sync_copy(idx_hbm.at[wid], idx_v)
        bufs = ((buf0, gsem0, wsem0), (buf1, gsem1, wsem1))

        def gather(c, buf, gsem):
            return pltpu.make_async_copy(ys_hbm.at[idx_v.at[c]], buf, gsem)

        def write(c, buf, wsem):
            return pltpu.make_async_copy(buf, out_hbm.at[pl.ds(base + c * SC_ROWS, SC_ROWS)], wsem)

        gather(0, buf0, gsem0).start()

        @pl.loop(0, nch, step=2)
        def _(c0):
            for b in range(2):
                c = c0 + b
                buf, gsem, wsem = bufs[b]
                obuf, ogsem, owsem = bufs[1 - b]
                gather(c, buf, gsem).wait()

                @pl.when(c > 0)
                def _():
                    write(c - 1, obuf, owsem).wait()

                @pl.when(c + 1 < nch)
                def _():
                    gather(c + 1, obuf, ogsem).start()

                write(c, buf, wsem).start()

        write(nch - 1, buf1, wsem1).wait()

    run = pl.kernel(
        body,
        out_type=jax.ShapeDtypeStruct((n, HALF), jnp.uint32),
        mesh=_sc_mesh(),
        scratch_types=[pltpu.VMEM((nch, SC_ROWS), jnp.int32),
                       pltpu.VMEM((SC_ROWS, HALF), jnp.uint32),
                       pltpu.VMEM((SC_ROWS, HALF), jnp.uint32),
                       pltpu.SemaphoreType.DMA, pltpu.SemaphoreType.DMA,
                       pltpu.SemaphoreType.DMA, pltpu.SemaphoreType.DMA],
        name="sc_collect",
    )
    return run(ys, idx).reshape(dest.shape + (HALF,))


def kernel(x, c, positions, w_ada, b_ada, g_pre_mix, g_post_mix, g_pre_ffn, g_post_ffn, w_in, conv_w, conv_b, b_gates, g_mlstm, w_branch_a, w_branch_b, w_out, router_w, router_bias, w_exp_gate, w_exp_up, w_exp_down, w_sh_gate, w_sh_up, w_sh_down):
    depth = w_ada.shape[0]
    for l in range(depth):
        x = _layer(x, c, positions, w_ada[l], b_ada[l], g_pre_mix[l], g_post_mix[l], g_pre_ffn[l],
                   g_post_ffn[l], w_in[l], conv_w[l], conv_b[l], b_gates[l], g_mlstm[l],
                   w_branch_a[l], w_branch_b[l], w_out[l], router_w[l], router_bias[l],
                   w_exp_gate[l], w_exp_up[l], w_exp_down[l], w_sh_gate[l], w_sh_up[l], w_sh_down[l])
    return x
```

```python
import functools

import jax
import jax.numpy as jnp
from jax import lax
from jax.experimental import pallas as pl
from jax.experimental.pallas import tpu as pltpu
from jax.experimental.pallas import tpu_sc as plsc

F32 = jnp.float32
BF16 = jnp.bfloat16
HIGHEST = lax.Precision.HIGHEST
LANES = 128

D_MODEL = 1024
ATT_GROUPS = ((128, 1), (512, 4), (2048, 16))
ATT_HEAD_DIM = 64
ATT_GROUP_W = 256
ATT_BLK = 128
ATT_PAIR = 2
ROPE_THETA = 500000.0
ROPE_HALF = 8
MLSTM_HEADS = 4
MLSTM_QK_DIM = 128
MLSTM_V_DIM = 256
MLSTM_BLOCK = 128
MLSTM_GROUP = 8
CONV_WIDTH = 4
N_EXPERTS = 256
TOP_K = 8
N_GROUPS = 8
TOPK_GROUPS = 4
EXPERT_FF = 256
ROUTED_SCALE = 2.5
NORM_EPS = 1e-6
NEG = -1e30

OFF_MV, OFF_MO, OFF_GA, OFF_GB = 0, 1024, 2048, 3072
OFF_MQ, OFF_MK = 4096, 4608
OFF_AQ, OFF_AK, OFF_AV = 5120, 5888, 6656
PROJ_W = 7424
HALF = D_MODEL // 2

EXPERT_BLOCK = 512
EXPERT_SLOTS = 6
MOE_PARTS = 2
MERGE_SPLIT = 2
VMEM_LIMIT = 56 * 1024 * 1024


def _nt(a, b):
    return lax.dot_general(a, b, (((1,), (1,)), ((), ())), preferred_element_type=F32)


def _tn(a, b):
    return lax.dot_general(a, b, (((0,), (0,)), ((), ())), preferred_element_type=F32)


_sigmoid = jax.nn.sigmoid


def _silu(x):
    return x * _sigmoid(x)


def _pack_pair(lo, hi):
    lo_b = pltpu.bitcast(lo.astype(BF16).astype(F32), jnp.uint32)
    hi_b = pltpu.bitcast(hi.astype(BF16).astype(F32), jnp.uint32)
    return (lo_b >> 16) | (hi_b & jnp.uint32(0xFFFF0000))


def _unpack_pair(w):
    lo = pltpu.bitcast(w << 16, F32)
    hi = pltpu.bitcast(w & jnp.uint32(0xFFFF0000), F32)
    return lo, hi


def _mod_kernel(c_ref, w_ref, b_ref, o_ref):
    a = _silu(c_ref[...])
    o_ref[...] = jnp.dot(a, w_ref[...], preferred_element_type=F32, precision=HIGHEST) + b_ref[...]


def _adaln(c, w_ada, b_ada):
    B = c.shape[0]
    n = w_ada.shape[1]
    tn = 512
    return pl.pallas_call(
        _mod_kernel,
        grid=(n // tn,),
        in_specs=[pl.BlockSpec((B, D_MODEL), lambda j: (0, 0)),
                  pl.BlockSpec((D_MODEL, tn), lambda j: (0, j)),
                  pl.BlockSpec((1, tn), lambda j: (0, j))],
        out_specs=pl.BlockSpec((B, tn), lambda j: (0, j)),
        out_shape=jax.ShapeDtypeStruct((B, n), F32),
        name="adaln_mod",
    )(c, w_ada, b_ada.reshape(1, n))


def _proj_kernel(x_ref, mod_ref, g_ref, w_ref, wif_ref, o_ref, gates_ref, h_ref):
    @pl.when(pl.program_id(1) == 0)
    def _():
        x = x_ref[...]
        ms = jnp.mean(x * x, axis=-1, keepdims=True)
        y = x * lax.rsqrt(ms + NORM_EPS) * g_ref[...]
        h = (y * (1.0 + mod_ref[0, 1:2, :]) + mod_ref[0, 0:1, :]).astype(BF16)
        h_ref[...] = h
        gates_ref[...] = _nt(wif_ref[...], h)

    o_ref[...] = jnp.dot(h_ref[...], w_ref[...], preferred_element_type=F32).astype(BF16)


def _in_proj(x2, mod3, g_pre, w_main, w_if, seq):
    T = x2.shape[0]
    tm, tn = 1024, PROJ_W // 2
    per_b = seq // tm
    return pl.pallas_call(
        _proj_kernel,
        grid=(T // tm, PROJ_W // tn),
        in_specs=[pl.BlockSpec((tm, D_MODEL), lambda i, j: (i, 0)),
                  pl.BlockSpec((1, 6, D_MODEL), lambda i, j: (i // per_b, 0, 0)),
                  pl.BlockSpec((1, D_MODEL), lambda i, j: (0, 0)),
                  pl.BlockSpec((D_MODEL, tn), lambda i, j: (0, j)),
                  pl.BlockSpec((2 * MLSTM_HEADS, D_MODEL), lambda i, j: (0, 0))],
        out_specs=[pl.BlockSpec((tm, tn), lambda i, j: (i, j)),
                   pl.BlockSpec((2 * MLSTM_HEADS, tm), lambda i, j: (0, i))],
        out_shape=[jax.ShapeDtypeStruct((T, PROJ_W), BF16),
                   jax.ShapeDtypeStruct((2 * MLSTM_HEADS, T), F32)],
        scratch_shapes=[pltpu.VMEM((tm, D_MODEL), BF16)],
        compiler_params=pltpu.CompilerParams(
            dimension_semantics=("arbitrary", "arbitrary"), vmem_limit_bytes=VMEM_LIMIT),
        name="norm_in_proj",
    )(x2, mod3, g_pre, w_main, w_if)


def _attn_kernel(q_ref, k_ref, v_ref, cs_ref, sn_ref, o_ref, qf, kf, vf, acc, m_s, l_s, *, seq):
    g = pl.program_id(1)
    lane = lax.broadcasted_iota(jnp.int32, (ATT_BLK, LANES), 1)
    first = (lane % ATT_HEAD_DIM) < ROPE_HALF
    low_head = lane < ATT_HEAD_DIM

    def rope(x, cs, sn):
        partner = jnp.where(first, pltpu.roll(x, LANES - ROPE_HALF, 1), pltpu.roll(x, ROPE_HALF, 1))
        return x * cs + partner * sn

    def zero_pad(i, _):
        rows = pl.ds(pl.multiple_of(i * ATT_BLK, ATT_BLK), ATT_BLK)
        for hp in range(2):
            kf[hp, rows, :] = jnp.zeros((ATT_BLK, LANES), F32)
            vf[hp, rows, :] = jnp.zeros((ATT_BLK, LANES), F32)
        return 0

    lax.fori_loop(0, seq // ATT_BLK, zero_pad, 0)

    def stage(i, _):
        r = pl.multiple_of(i * ATT_BLK, ATT_BLK)
        rows = pl.ds(r, ATT_BLK)
        prow = pl.ds(pl.multiple_of(seq + i * ATT_BLK, ATT_BLK), ATT_BLK)
        cs = cs_ref[0, rows, :]
        sn = sn_ref[0, rows, :]
        for hp in range(2):
            cols = pl.ds(hp * LANES, LANES)
            qf[hp, rows, :] = rope(q_ref[0, rows, cols].astype(F32), cs, sn) * (ATT_HEAD_DIM ** -0.5)
            kf[hp, prow, :] = rope(k_ref[0, rows, cols].astype(F32), cs, sn)
            vf[hp, prow, :] = v_ref[0, rows, cols].astype(F32)
        return 0

    lax.fori_loop(0, seq // ATT_BLK, stage, 0)

    qi = lax.broadcasted_iota(jnp.int32, (ATT_BLK, 2 * ATT_BLK), 0)
    ki = lax.broadcasted_iota(jnp.int32, (ATT_BLK, 2 * ATT_BLK), 1)
    band = (ki >= qi) & (ki <= qi + ATT_BLK)

    def process(d, init):
        span = ATT_BLK * d
        single = seq == span

        def body(cp, _):
            blocks = [cp * ATT_PAIR + i for i in range(ATT_PAIR)]
            qrows, krows, valid = [], [], []
            for c in blocks:
                rho = c % d
                n = c // d
                qstart = rho + n * span
                if single:
                    kstart, nk = seq + qstart, ATT_BLK
                    valid.append(band[:, ATT_BLK:])
                else:
                    kstart, nk = seq + qstart - span, 2 * ATT_BLK
                    valid.append(band & (ki >= jnp.where(n > 0, 0, ATT_BLK)))
                qrows.append(pl.ds(qstart, ATT_BLK, stride=d) if d > 1 else pl.ds(qstart, ATT_BLK))
                krows.append(pl.ds(kstart, nk, stride=d) if d > 1 else pl.ds(kstart, nk))
            units = [(b, hp) for b in range(ATT_PAIR) for hp in range(2)]
            heads = [(u, hh) for u in range(len(units)) for hh in range(2)]
            q2 = [qf[hp, qrows[b], :] for b, hp in units]
            k2 = [kf[hp, krows[b], :].astype(BF16) for b, hp in units]
            v2 = [vf[hp, krows[b], :].astype(BF16) for b, hp in units]
            qh = [jnp.where(low_head if hh == 0 else jnp.logical_not(low_head), q2[u], 0.0).astype(BF16)
                  for u, hh in heads]
            s = [jnp.where(valid[units[u][0]], _nt(qh[i], k2[u]), NEG) for i, (u, hh) in enumerate(heads)]
            m = [jnp.max(x, axis=1, keepdims=True) for x in s]
            p = [jnp.exp(x - mx) for x, mx in zip(s, m)]
            l = [jnp.sum(x, axis=1, keepdims=True) for x in p]
            o = [jnp.dot(p[i].astype(BF16), v2[u], preferred_element_type=F32)
                 for i, (u, hh) in enumerate(heads)]
            for u, (b, hp) in enumerate(units):
                o_b = jnp.where(low_head, o[2 * u], o[2 * u + 1])
                m_b = jnp.where(low_head, m[2 * u], m[2 * u + 1])
                l_b = jnp.where(low_head, l[2 * u], l[2 * u + 1])
                if init:
                    acc[hp, qrows[b], :] = o_b
                    m_s[hp, qrows[b], :] = m_b
                    l_s[hp, qrows[b], :] = l_b
                else:
                    m_old = m_s[hp, qrows[b], :]
                    m_new = jnp.maximum(m_old, m_b)
                    a_old = jnp.exp(m_old - m_new)
                    a_new = jnp.exp(m_b - m_new)
                    acc[hp, qrows[b], :] = acc[hp, qrows[b], :] * a_old + o_b * a_new
                    l_s[hp, qrows[b], :] = l_s[hp, qrows[b], :] * a_old + l_b * a_new
                    m_s[hp, qrows[b], :] = m_new
            return 0

        lax.fori_loop(0, seq // (ATT_BLK * ATT_PAIR), body, 0)

    for gi, (_, d) in enumerate(ATT_GROUPS):
        @pl.when(g == gi)
        def _(d=d, gi=gi):
            process(d, gi == 0)

    @pl.when(g == len(ATT_GROUPS) - 1)
    def _():
        def fin(i, _):
            rows = pl.ds(pl.multiple_of(i * ATT_BLK, ATT_BLK), ATT_BLK)
            for hp in range(2):
                o_ref[0, rows, pl.ds(hp * LANES, LANES)] = (acc[hp, rows, :] / l_s[hp, rows, :]).astype(BF16)
            return 0

        lax.fori_loop(0, seq // ATT_BLK, fin, 0)


def _attention(proj3, cs, sn):
    B, S, _ = proj3.shape
    ng = len(ATT_GROUPS)
    qb, kb, vb = OFF_AQ // ATT_GROUP_W, OFF_AK // ATT_GROUP_W, OFF_AV // ATT_GROUP_W
    return pl.pallas_call(
        functools.partial(_attn_kernel, seq=S),
        grid=(B, ng),
        in_specs=[pl.BlockSpec((1, S, ATT_GROUP_W), lambda b, g: (b, 0, qb + g)),
                  pl.BlockSpec((1, S, ATT_GROUP_W), lambda b, g: (b, 0, kb + g)),
                  pl.BlockSpec((1, S, ATT_GROUP_W), lambda b, g: (b, 0, vb + g)),
                  pl.BlockSpec((1, S, LANES), lambda b, g: (b, 0, 0)),
                  pl.BlockSpec((1, S, LANES), lambda b, g: (b, 0, 0))],
        out_specs=pl.BlockSpec((1, S, ATT_GROUP_W), lambda b, g: (b, 0, 0)),
        out_shape=jax.ShapeDtypeStruct((B, S, ATT_GROUP_W), BF16),
        scratch_shapes=[pltpu.VMEM((2, S, LANES), F32),
                        pltpu.VMEM((2, 2 * S, LANES), F32),
                        pltpu.VMEM((2, 2 * S, LANES), F32),
                        pltpu.VMEM((2, S, LANES), F32),
                        pltpu.VMEM((2, S, LANES), F32),
                        pltpu.VMEM((2, S, LANES), F32)],
        compiler_params=pltpu.CompilerParams(
            dimension_semantics=("arbitrary", "arbitrary"), vmem_limit_bytes=VMEM_LIMIT),
        name="dilated_attention",
    )(proj3, proj3, proj3, cs, sn)


def _log_sigmoid(x):
    return jnp.minimum(x, 0.0) - jnp.log(1.0 + jnp.exp(-jnp.abs(x)))


def _mlstm_kernel(mq_ref, mk_ref, mv_ref, mo_ref, gt_ref, cwq_ref, cwk_ref, cbq_ref, cbk_ref,
                  bg_ref, gm_ref, o_ref, q_s, k_s, va_s, rows_s, acc_s, kv_s, inter_s, emt_s,
                  c_s, *, seq):
    h = pl.program_id(1)
    L = MLSTM_BLOCK
    NC = seq // L
    DK, DV = MLSTM_QK_DIM, MLSTM_V_DIM
    DA = DV + LANES
    nshift = CONV_WIDTH - 1

    tt = lax.broadcasted_iota(jnp.int32, (nshift * L, 2 * L), 0)
    uu = lax.broadcasted_iota(jnp.int32, (nshift * L, 2 * L), 1)
    shift_mat = (uu == L + tt % L - (tt // L + 1)).astype(BF16)
    conv_w = jnp.concatenate([cwq_ref[...], cwk_ref[...]], axis=1)
    conv_b = jnp.concatenate([cbq_ref[...], cbk_ref[...]], axis=1)
    prev = jnp.zeros((L, 2 * DK), BF16)
    for i in range(NC):
        blk = slice(i * L, (i + 1) * L)
        va_s[blk, 0:DV] = mv_ref[0, blk, :]
        va_s[blk, DV:DA] = jnp.ones((L, DA - DV), BF16)
        cur = jnp.concatenate([mq_ref[0, blk, :], mk_ref[0, blk, :]], axis=1)
        shifted = jnp.dot(shift_mat, jnp.concatenate([prev, cur], axis=0),
                          preferred_element_type=F32)
        y = conv_b + cur.astype(F32) * conv_w[nshift:nshift + 1, :]
        for s in range(nshift):
            y = y + shifted[s * L:(s + 1) * L, :] * conv_w[nshift - 1 - s:nshift - s, :]
        y = _silu(y)
        q_s[blk, :] = y[:, 0:DK].astype(BF16)
        k_s[blk, :] = (y[:, DK:2 * DK] * (DK ** -0.5)).astype(BF16)
        prev = cur

    lane = lax.broadcasted_iota(jnp.int32, (1, LANES), 1)
    bias = bg_ref[...]
    b_i = jnp.sum(jnp.where(lane == h, bias, 0.0), axis=1, keepdims=True)
    b_f = jnp.sum(jnp.where(lane == h + MLSTM_HEADS, bias, 0.0), axis=1, keepdims=True)
    ri = lax.broadcasted_iota(jnp.int32, (L, L), 0)
    ci = lax.broadcasted_iota(jnp.int32, (L, L), 1)
    causal = ci <= ri
    eye = (ri == ci).astype(F32)
    i_rows = gt_ref[h, 0] + b_i
    lf_rows = _log_sigmoid(gt_ref[h + MLSTM_HEADS, 0] + b_f)
    b_rows = jnp.dot(lf_rows, (ri <= ci).astype(F32), preferred_element_type=F32,
                     precision=HIGHEST)
    b_end = b_rows[:, L - 1:L]
    g_rows = b_end - b_rows + i_rows
    g_max = jnp.max(g_rows, axis=1, keepdims=True)
    m = jnp.zeros((1, 1), F32)
    m_prev, m_new = [], []
    for c in range(NC):
        m_prev.append(m)
        m = jnp.maximum(b_end[c:c + 1, :] + m, g_max[c:c + 1, :])
        m_new.append(m)
    m_prev = jnp.concatenate(m_prev, axis=0)
    m_new = jnp.concatenate(m_new, axis=0)
    rows_s[0] = b_rows
    rows_s[1] = jnp.exp(g_rows - m_new)
    rows_s[2] = b_rows - i_rows
    rows_s[3] = jnp.broadcast_to(m_prev, (NC, L))
    rows_s[4] = jnp.broadcast_to(jnp.exp(b_end + m_prev - m_new), (NC, L))

    r2 = lax.broadcasted_iota(jnp.int32, (2 * L, 2 * L), 0)
    c2 = lax.broadcasted_iota(jnp.int32, (2 * L, 2 * L), 1)
    ones_blk = ((r2 < L) == (c2 < L)).astype(BF16)

    G = MLSTM_GROUP

    def local(cg, _):
        cs = [cg * G + i for i in range(G)]
        rows = [pl.ds(pl.multiple_of(c * L, L), L) for c in cs]
        b_r = [rows_s[0, pl.ds(c, 1), :] for c in cs]
        w_r = [rows_s[1, pl.ds(c, 1), :] for c in cs]
        u_r = [rows_s[2, pl.ds(c, 1), :] for c in cs]
        mp = [rows_s[3, pl.ds(c, 1), :] for c in cs]
        q = [q_s[r, :] for r in rows]
        k = [k_s[r, :] for r in rows]
        va = [va_s[r, :] for r in rows]
        qk = [_nt(a, b) for a, b in zip(q, k)]
        x2 = [jnp.concatenate([eye * a, eye * b], axis=1) for a, b in zip(b_r, w_r)]
        hi = [x.astype(BF16) for x in x2]
        lo = [(x - h_.astype(F32)).astype(BF16) for x, h_ in zip(x2, hi)]
        yb = [jnp.dot(h_, ones_blk, preferred_element_type=F32)
              + jnp.dot(l_, ones_blk, preferred_element_type=F32) for h_, l_ in zip(hi, lo)]
        b_b = [y[:, 0:L] for y in yb]
        w_b = [y[:, L:2 * L] for y in yb]
        for i in range(G):
            kv_s[cs[i]] = _tn((w_b[i] * k[i].astype(F32)).astype(BF16), va[i])
        dmat = [jnp.where(causal, b - u, NEG) for b, u in zip(b_b, u_r)]
        m_t = [jnp.maximum(b + m_, jnp.max(d, axis=1, keepdims=True))
               for b, m_, d in zip(b_b, mp, dmat)]
        sc = [a * jnp.exp(d - m_) for a, d, m_ in zip(qk, dmat, m_t)]
        for i in range(G):
            acc_s[rows[i], :] = jnp.dot(sc[i].astype(BF16), va[i], preferred_element_type=F32)
            inter_s[rows[i], :] = jnp.exp(b_b[i] + mp[i] - m_t[i])
            emt_s[rows[i], :] = jnp.exp(-m_t[i])
        return 0

    lax.fori_loop(0, NC // G, local, 0)

    g_row = gm_ref[...]
    c_s[...] = jnp.zeros((DK, DA), F32)

    def recur(cg, _):
        cs = [cg * G + i for i in range(G)]
        rows = [pl.ds(pl.multiple_of(c * L, L), L) for c in cs]
        states = [c_s[...]]
        for c in cs:
            dec = rows_s[4, pl.ds(c, 1), :]
            states.append(jnp.concatenate([dec, dec, dec], axis=1) * states[-1] + kv_s[c])
        c_s[...] = states[G]
        read = [jnp.dot(q_s[r, :], st.astype(BF16), preferred_element_type=F32)
                for r, st in zip(rows, states)]
        inter = [inter_s[r, :] for r in rows]
        out = [acc_s[r, :] + jnp.concatenate([it, it, it], axis=1) * rd
               for r, it, rd in zip(rows, inter, read)]
        emt = [emt_s[r, :] for r in rows]
        nrm = [jnp.maximum(jnp.abs(jnp.concatenate([o[:, DV:DA], o[:, DV:DA]], axis=1)),
                           jnp.concatenate([e_, e_], axis=1)) for o, e_ in zip(out, emt)]
        hh = [o[:, 0:DV] / n_ for o, n_ in zip(out, nrm)]
        ms = [jnp.mean(x * x, axis=1, keepdims=True) for x in hh]
        hn = [x * lax.rsqrt(m_ + NORM_EPS) * g_row for x, m_ in zip(hh, ms)]
        for i in range(G):
            o_ref[0, rows[i], :] = (hn[i] * _sigmoid(mo_ref[0, rows[i], :].astype(F32))).astype(BF16)
        return 0

    lax.fori_loop(0, NC // G, recur, 0)


def _mlstm(proj3, gates_t, conv_w, conv_b, bg_row, g_mlstm):
    B, S, _ = proj3.shape
    H, DK, DV = MLSTM_HEADS, MLSTM_QK_DIM, MLSTM_V_DIM
    L = MLSTM_BLOCK
    NC = S // L
    DA = DV + LANES
    qb, kb = OFF_MQ // DK, OFF_MK // DK
    vb, ob = OFF_MV // DV, OFF_MO // DV
    nq = H
    return pl.pallas_call(
        functools.partial(_mlstm_kernel, seq=S),
        grid=(B, H),
        in_specs=[pl.BlockSpec((1, S, DK), lambda b, h: (b, 0, qb + h)),
                  pl.BlockSpec((1, S, DK), lambda b, h: (b, 0, kb + h)),
                  pl.BlockSpec((1, S, DV), lambda b, h: (b, 0, vb + h)),
                  pl.BlockSpec((1, S, DV), lambda b, h: (b, 0, ob + h)),
                  pl.BlockSpec((2 * H, 1, NC, L), lambda b, h: (0, b, 0, 0)),
                  pl.BlockSpec((CONV_WIDTH, DK), lambda b, h: (0, h)),
                  pl.BlockSpec((CONV_WIDTH, DK), lambda b, h: (0, nq + h)),
                  pl.BlockSpec((1, DK), lambda b, h: (0, h)),
                  pl.BlockSpec((1, DK), lambda b, h: (0, nq + h)),
                  pl.BlockSpec((1, LANES), lambda b, h: (0, 0)),
                  pl.BlockSpec((1, DV), lambda b, h: (0, h))],
        out_specs=pl.BlockSpec((1, S, DV), lambda b, h: (b, 0, h)),
        out_shape=jax.ShapeDtypeStruct((B, S, H * DV), BF16),
        scratch_shapes=[pltpu.VMEM((S, DK), BF16),
                        pltpu.VMEM((S, DK), BF16),
                        pltpu.VMEM((S, DA), BF16),
                        pltpu.VMEM((5, NC, L), F32),
                        pltpu.VMEM((S, DA), F32),
                        pltpu.VMEM((NC, DK, DA), F32),
                        pltpu.VMEM((S, L), F32),
                        pltpu.VMEM((S, L), F32),
                        pltpu.VMEM((DK, DA), F32)],
        compiler_params=pltpu.CompilerParams(
            dimension_semantics=("arbitrary", "arbitrary"), vmem_limit_bytes=VMEM_LIMIT),
        name="mlstm_chunkwise",
    )(proj3, proj3, proj3, proj3, gates_t, conv_w, conv_w, conv_b, conv_b, bg_row, g_mlstm)


def _rms(y, g):
    ms = jnp.mean(y * y, axis=-1, keepdims=True)
    return y * lax.rsqrt(ms + NORM_EPS) * g


def _merge_kernel(ya_ref, yb_ref, ga_ref, gb_ref, x_ref, mod_ref, wa_ref, wb_ref, wo_ref,
                  gpost_ref, gpre_ref, x1_ref, h2_ref):
    tm = x_ref.shape[0]
    slabs = [pl.ds(s * (tm // MERGE_SPLIT), tm // MERGE_SPLIT) for s in range(MERGE_SPLIT)]
    pa = [jnp.dot(ya_ref[r, :], wa_ref[...], preferred_element_type=F32) for r in slabs]
    pb = [jnp.dot(yb_ref[r, :], wb_ref[...], preferred_element_type=F32) for r in slabs]
    merged = [_sigmoid(ga_ref[r, :].astype(F32)) * a + _sigmoid(gb_ref[r, :].astype(F32)) * b
              for r, a, b in zip(slabs, pa, pb)]
    y = [jnp.dot(m.astype(BF16), wo_ref[...], preferred_element_type=F32) for m in merged]
    x1 = [x_ref[r, :] + mod_ref[0, 2:3, :] * _rms(v, gpost_ref[...]) for r, v in zip(slabs, y)]
    for r, v in zip(slabs, x1):
        x1_ref[r, :] = v
    h2 = [_rms(v, gpre_ref[...]) * (1.0 + mod_ref[0, 4:5, :]) + mod_ref[0, 3:4, :] for v in x1]
    for r, v in zip(slabs, h2):
        h2_ref[r, :] = _pack_pair(v[:, :HALF], v[:, HALF:])


def _merge(ya2, yb2, proj2, x2, mod3, wa, wb, wo, g_post, g_pre, seq):
    T = x2.shape[0]
    tm = 512 * MERGE_SPLIT
    per_b = seq // tm
    full = lambda shape: pl.BlockSpec(shape, lambda i: (0,) * len(shape))
    return pl.pallas_call(
        _merge_kernel,
        grid=(T // tm,),
        in_specs=[pl.BlockSpec((tm, ATT_GROUP_W), lambda i: (i, 0)),
                  pl.BlockSpec((tm, D_MODEL), lambda i: (i, 0)),
                  pl.BlockSpec((tm, D_MODEL), lambda i: (i, OFF_GA // D_MODEL)),
                  pl.BlockSpec((tm, D_MODEL), lambda i: (i, OFF_GB // D_MODEL)),
                  pl.BlockSpec((tm, D_MODEL), lambda i: (i, 0)),
                  pl.BlockSpec((1, 6, D_MODEL), lambda i: (i // per_b, 0, 0)),
                  full((ATT_GROUP_W, D_MODEL)), full((D_MODEL, D_MODEL)), full((D_MODEL, D_MODEL)),
                  full((1, D_MODEL)), full((1, D_MODEL))],
        out_specs=[pl.BlockSpec((tm, D_MODEL), lambda i: (i, 0)),
                   pl.BlockSpec((tm, HALF), lambda i: (i, 0))],
        out_shape=[jax.ShapeDtypeStruct((T, D_MODEL), F32),
                   jax.ShapeDtypeStruct((T, HALF), jnp.uint32)],
        compiler_params=pltpu.CompilerParams(
            dimension_semantics=("arbitrary",), vmem_limit_bytes=VMEM_LIMIT),
        name="merge_out_proj",
    )(ya2, yb2, proj2, proj2, x2, mod3, wa, wb, wo, g_post, g_pre)


def _router_kernel(h2_ref, rlo_ref, rhi_ref, bias_ref, idx_ref, w_ref, rank_ref, cnt_ref):
    E = N_EXPERTS
    tr = h2_ref.shape[0]
    gsz = E // N_GROUPS

    @pl.when(pl.program_id(0) == 0)
    def _():
        cnt_ref[...] = jnp.zeros(cnt_ref.shape, F32)

    lo, hi = _unpack_pair(h2_ref[...])
    logits = _nt(rlo_ref[...], lo.astype(BF16)) + _nt(rhi_ref[...], hi.astype(BF16))
    scores = _sigmoid(logits)
    sel = scores + bias_ref[:, 0:1]

    gi = lax.broadcasted_iota(jnp.int32, (gsz, tr), 0).astype(F32)
    gs_rows = []
    for g in range(N_GROUPS):
        blk = sel[g * gsz:(g + 1) * gsz, :]
        m1 = jnp.max(blk, axis=0, keepdims=True)
        a1 = jnp.min(jnp.where(blk == m1, gi, float(E)), axis=0, keepdims=True)
        m2 = jnp.max(jnp.where(gi == a1, -jnp.inf, blk), axis=0, keepdims=True)
        gs_rows.append(m1 + m2)
    gs = jnp.concatenate(gs_rows, axis=0)
    g8 = lax.broadcasted_iota(jnp.int32, (N_GROUPS, tr), 0).astype(F32)
    gmask = jnp.zeros((N_GROUPS, tr), F32)
    for _ in range(TOPK_GROUPS):
        m = jnp.max(gs, axis=0, keepdims=True)
        a = jnp.min(jnp.where(gs == m, g8, float(E)), axis=0, keepdims=True)
        hit = g8 == a
        gmask = jnp.where(hit, 1.0, gmask)
        gs = jnp.where(hit, -jnp.inf, gs)
    selm = jnp.concatenate(
        [jnp.where(gmask[g:g + 1, :] > 0.0, sel[g * gsz:(g + 1) * gsz, :], -jnp.inf)
         for g in range(N_GROUPS)], axis=0)

    ei = lax.broadcasted_iota(jnp.int32, (E, tr), 0).astype(F32)
    picks, weights, hits = [], [], []
    candidates = selm
    for _ in range(TOP_K):
        m = jnp.max(selm, axis=0, keepdims=True)
        a = jnp.min(jnp.where(selm == m, ei, float(E)), axis=0, keepdims=True)
        hit = ei == a
        picks.append(a)
        hits.append(hit)
        weights.append(jnp.sum(jnp.where(hit, scores, 0.0), axis=0, keepdims=True))
        selm = jnp.where(hit, -jnp.inf, selm)
    chosen = jnp.where(selm != candidates, 1.0, 0.0)
    wsum = weights[0]
    for w in weights[1:]:
        wsum = wsum + w

    ti = lax.broadcasted_iota(jnp.int32, (tr, tr), 0)
    tj = lax.broadcasted_iota(jnp.int32, (tr, tr), 1)
    before = (ti < tj).astype(BF16)
    pos = jnp.dot(chosen.astype(BF16), before, preferred_element_type=F32) + cnt_ref[:, 0:1]
    ranks = [jnp.sum(jnp.where(hit, pos, 0.0), axis=0, keepdims=True) for hit in hits]
    cnt_ref[...] = cnt_ref[...] + jnp.sum(chosen, axis=1, keepdims=True)

    idx_ref[...] = jnp.concatenate(picks, axis=0).astype(jnp.int32)
    w_ref[...] = jnp.concatenate([w / wsum * ROUTED_SCALE for w in weights], axis=0)
    rank_ref[...] = jnp.concatenate(ranks, axis=0).astype(jnp.int32)


def _router(h2p, r_lo, r_hi, bias_col, row0, T):
    tr = 512
    off = row0 // tr
    full = lambda shape: pl.BlockSpec(shape, lambda i: (0,) * len(shape))
    return pl.pallas_call(
        _router_kernel,
        grid=(T // tr,),
        in_specs=[pl.BlockSpec((tr, HALF), lambda i: (i + off, 0)),
                  full((N_EXPERTS, HALF)), full((N_EXPERTS, HALF)), full((N_EXPERTS, LANES))],
        out_specs=[pl.BlockSpec((TOP_K, tr), lambda i: (0, i)),
                   pl.BlockSpec((TOP_K, tr), lambda i: (0, i)),
                   pl.BlockSpec((TOP_K, tr), lambda i: (0, i)),
                   full((N_EXPERTS, LANES))],
        out_shape=[jax.ShapeDtypeStruct((TOP_K, T), jnp.int32),
                   jax.ShapeDtypeStruct((TOP_K, T), F32),
                   jax.ShapeDtypeStruct((TOP_K, T), jnp.int32),
                   jax.ShapeDtypeStruct((N_EXPERTS, LANES), F32)],
        compiler_params=pltpu.CompilerParams(
            dimension_semantics=("arbitrary",), vmem_limit_bytes=VMEM_LIMIT),
        name="router_topk",
    )(h2p, r_lo, r_hi, bias_col)


def _dest_kernel(idx_ref, rank_ref, pstart_ref, dest_ref):
    tr = idx_ref.shape[1]
    ei = lax.broadcasted_iota(jnp.int32, (N_EXPERTS, tr), 0)
    start = pstart_ref[:, 0:1]
    rows = []
    for k in range(TOP_K):
        hit = ei == idx_ref[k:k + 1, :]
        rows.append(jnp.sum(jnp.where(hit, start, 0.0), axis=0, keepdims=True))
    dest_ref[...] = jnp.concatenate(rows, axis=0).astype(jnp.int32) + rank_ref[...]


def _slot_index(idx, rank, pstart_col):
    T = idx.shape[1]
    tr = 1024
    return pl.pallas_call(
        _dest_kernel,
        grid=(T // tr,),
        in_specs=[pl.BlockSpec((TOP_K, tr), lambda i: (0, i)),
                  pl.BlockSpec((TOP_K, tr), lambda i: (0, i)),
                  pl.BlockSpec((N_EXPERTS, LANES), lambda i: (0, 0))],
        out_specs=pl.BlockSpec((TOP_K, tr), lambda i: (0, i)),
        out_shape=jax.ShapeDtypeStruct((TOP_K, T), jnp.int32),
        name="slot_index",
    )(idx, rank, pstart_col)


def _ffn_kernel(first_ref, nblk_ref, nused_ref, xs_hbm, wg_ref, wu_ref, wd_ref, ys_hbm,
                xbuf, ybuf, in_sem, out_sem, wg_s, wu_s, wd_s):
    e = pl.program_id(0)
    bm = EXPERT_BLOCK
    ns = EXPERT_SLOTS
    nused = nused_ref[0]
    first = first_ref[e]
    n = nblk_ref[e]

    def in_copy(g):
        slot = g % ns
        return pltpu.make_async_copy(xs_hbm.at[pl.ds(g * bm, bm)], xbuf.at[slot], in_sem.at[slot])

    def out_copy(g):
        slot = g % ns
        return pltpu.make_async_copy(ybuf.at[slot], ys_hbm.at[pl.ds(g * bm, bm)], out_sem.at[slot])

    def fetch(g):
        @pl.when(g < nused)
        def _():
            in_copy(g).start()

    def release(g):
        @pl.when(g >= ns)
        def _():
            out_copy(g - ns).wait()

    def ffn(g):
        lo, hi = _unpack_pair(xbuf[g % ns])
        x = jnp.concatenate([lo.astype(BF16), hi.astype(BF16)], axis=1)
        gate = jnp.dot(x, wg_s[...], preferred_element_type=F32)
        up = jnp.dot(x, wu_s[...], preferred_element_type=F32)
        hid = (_silu(gate) * up).astype(BF16)
        return jnp.dot(hid, wd_s[...], preferred_element_type=F32)

    def pack(g, out):
        ybuf[g % ns] = _pack_pair(out[:, :HALF], out[:, HALF:])

    @pl.when(e == 0)
    def _():
        for q in range(ns - 1):
            fetch(q)

    @pl.when(n > 0)
    def _():
        wg_s[...] = wg_ref[0].astype(BF16)
        wu_s[...] = wu_ref[0].astype(BF16)
        wd_s[...] = wd_ref[0].astype(BF16)

        def two_blocks(j, _):
            g = first + 2 * j
            in_copy(g).wait()
            in_copy(g + 1).wait()
            fetch(g + ns - 1)
            release(g)
            release(g + 1)
            out_a = ffn(g)
            out_b = ffn(g + 1)
            pack(g, out_a)
            pack(g + 1, out_b)
            out_copy(g).start()
            out_copy(g + 1).start()
            fetch(g + ns)
            return 0

        lax.fori_loop(0, n // 2, two_blocks, 0)

        @pl.when(n % 2 == 1)
        def _():
            g = first + n - 1
            in_copy(g).wait()
            fetch(g + ns - 1)
            release(g)
            pack(g, ffn(g))
            out_copy(g).start()

    @pl.when(e == pl.num_programs(0) - 1)
    def _():
        for q in range(ns, 0, -1):
            @pl.when(nused >= q)
            def _(q=q):
                out_copy(nused - q).wait()


def _expert_ffn(first_blk, nblk, nused, xs, w_gate, w_up, w_down):
    P = xs.shape[0]
    bm = EXPERT_BLOCK
    w_map = lambda e, *_: (e, 0, 0)
    grid_spec = pltpu.PrefetchScalarGridSpec(
        num_scalar_prefetch=3,
        grid=(w_gate.shape[0],),
        in_specs=[pl.BlockSpec(memory_space=pl.ANY),
                  pl.BlockSpec((1, D_MODEL, EXPERT_FF), w_map),
                  pl.BlockSpec((1, D_MODEL, EXPERT_FF), w_map),
                  pl.BlockSpec((1, EXPERT_FF, D_MODEL), w_map)],
        out_specs=pl.BlockSpec(memory_space=pl.ANY),
        scratch_shapes=[pltpu.VMEM((EXPERT_SLOTS, bm, HALF), jnp.uint32),
                        pltpu.VMEM((EXPERT_SLOTS, bm, HALF), jnp.uint32),
                        pltpu.SemaphoreType.DMA((EXPERT_SLOTS,)),
                        pltpu.SemaphoreType.DMA((EXPERT_SLOTS,)),
                        pltpu.VMEM((D_MODEL, EXPERT_FF), BF16),
                        pltpu.VMEM((D_MODEL, EXPERT_FF), BF16),
                        pltpu.VMEM((EXPERT_FF, D_MODEL), BF16)],
    )
    return pl.pallas_call(
        _ffn_kernel,
        grid_spec=grid_spec,
        out_shape=jax.ShapeDtypeStruct((P, HALF), jnp.uint32),
        compiler_params=pltpu.CompilerParams(
            dimension_semantics=("arbitrary",), vmem_limit_bytes=VMEM_LIMIT),
        name="routed_experts",
    )(first_blk, nblk, nused, xs, w_gate, w_up, w_down)


def _final_kernel(yg_ref, w_ref, h2_ref, x1_ref, mod_ref, wsg_ref, wsu_ref, wsd_ref, gpost_ref, *rest):
    o_ref = rest[-1]
    lo, hi = _unpack_pair(h2_ref[...])
    h2 = jnp.concatenate([lo.astype(BF16), hi.astype(BF16)], axis=1)
    gate = jnp.dot(h2, wsg_ref[...], preferred_element_type=F32)
    up = jnp.dot(h2, wsu_ref[...], preferred_element_type=F32)
    shared = jnp.dot((_silu(gate) * up).astype(BF16), wsd_ref[...], preferred_element_type=F32)
    y_lo = shared[:, :HALF]
    y_hi = shared[:, HALF:]
    for k in range(TOP_K):
        r_lo, r_hi = _unpack_pair(yg_ref[k])
        wk = w_ref[:, k:k + 1]
        y_lo = y_lo + wk * r_lo
        y_hi = y_hi + wk * r_hi
    ms = (jnp.sum(y_lo * y_lo, axis=-1, keepdims=True)
          + jnp.sum(y_hi * y_hi, axis=-1, keepdims=True)) * (1.0 / D_MODEL)
    inv = lax.rsqrt(ms + NORM_EPS)
    o_ref[:, 0:HALF] = x1_ref[:, 0:HALF] + mod_ref[0, 5:6, 0:HALF] * (y_lo * inv * gpost_ref[:, 0:HALF])
    o_ref[:, HALF:] = x1_ref[:, HALF:] + mod_ref[0, 5:6, HALF:] * (y_hi * inv * gpost_ref[:, HALF:])


def _final(yg, w_tk, h2p, x1, mod3, wsg, wsu, wsd, g_post, seq, row0, out_prev):
    T = x1.shape[0]
    tp = yg.shape[1]
    tm = 512
    per_b = seq // tm
    off = row0 // tm
    full = lambda shape: pl.BlockSpec(shape, lambda i: (0,) * len(shape))
    in_specs = [pl.BlockSpec((TOP_K, tm, HALF), lambda i: (0, i, 0)),
                pl.BlockSpec((tm, TOP_K), lambda i: (i, 0)),
                pl.BlockSpec((tm, HALF), lambda i: (i + off, 0)),
                pl.BlockSpec((tm, D_MODEL), lambda i: (i + off, 0)),
                pl.BlockSpec((1, 6, D_MODEL), lambda i: ((i + off) // per_b, 0, 0)),
                full((D_MODEL, EXPERT_FF)), full((D_MODEL, EXPERT_FF)), full((EXPERT_FF, D_MODEL)),
                full((1, D_MODEL))]
    args = [yg, w_tk, h2p, x1, mod3, wsg, wsu, wsd, g_post]
    aliases = {}
    if out_prev is not None:
        in_specs.append(pl.BlockSpec(memory_space=pl.ANY))
        args.append(out_prev)
        aliases = {len(args) - 1: 0}
    return pl.pallas_call(
        _final_kernel,
        grid=(tp // tm,),
        in_specs=in_specs,
        out_specs=pl.BlockSpec((tm, D_MODEL), lambda i: (i + off, 0)),
        out_shape=jax.ShapeDtypeStruct((T, D_MODEL), F32),
        input_output_aliases=aliases,
        compiler_params=pltpu.CompilerParams(
            dimension_semantics=("arbitrary",), vmem_limit_bytes=VMEM_LIMIT),
        name="shared_expert_combine",
    )(*args)


def _rope_tables(positions):
    inv = jnp.power(ROPE_THETA, -jnp.arange(ROPE_HALF, dtype=F32) / ROPE_HALF)
    ang = positions.astype(F32)[..., None] * inv
    cos, sin = jnp.cos(ang), jnp.sin(ang)
    rest = ATT_HEAD_DIM - 2 * ROPE_HALF
    cs = jnp.concatenate([cos, cos, jnp.ones(ang.shape[:-1] + (rest,), F32)], axis=-1)
    sn = jnp.concatenate([-sin, sin, jnp.zeros(ang.shape[:-1] + (rest,), F32)], axis=-1)
    return jnp.tile(cs, (1, 1, 2)), jnp.tile(sn, (1, 1, 2))


def _layer(x, c, positions, w_ada, b_ada, g_pre_mix, g_post_mix, g_pre_ffn, g_post_ffn,
           w_in, conv_w, conv_b, b_gates, g_mlstm, w_branch_a, w_branch_b, w_out,
           router_w, router_bias, w_exp_gate, w_exp_up, w_exp_down, w_sh_gate, w_sh_up, w_sh_down):
    B, S, D = x.shape
    T = B * S
    H = MLSTM_HEADS
    x2 = x.reshape(T, D)

    mod3 = _adaln(c, w_ada, b_ada).reshape(B, 6, D)

    a_w = 3 * ATT_GROUP_W
    o_mq = 3 * a_w
    o_mk = o_mq + H * MLSTM_QK_DIM
    o_mv = o_mk + H * MLSTM_QK_DIM
    o_mo = o_mv + H * MLSTM_V_DIM
    o_mi = o_mo + H * MLSTM_V_DIM
    o_ga = o_mi + 2 * H
    o_gb = o_ga + D
    w_bf = w_in.astype(BF16)
    w_main = jnp.concatenate(
        [w_bf[:, o_mv:o_mi], w_bf[:, o_ga:o_gb + D], w_bf[:, o_mq:o_mv], w_bf[:, 0:o_mq]], axis=1)
    w_if = w_bf[:, o_mi:o_ga].T

    proj, gates = _in_proj(x2, mod3, g_pre_mix.reshape(1, D), w_main, w_if, S)
    proj3 = proj.reshape(B, S, PROJ_W)

    cs, sn = _rope_tables(positions)
    y_a = _attention(proj3, cs, sn)

    bg_row = jnp.pad(b_gates.reshape(1, 2 * H), ((0, 0), (0, LANES - 2 * H)))
    gates_t = gates.reshape(2 * H, B, S // MLSTM_BLOCK, MLSTM_BLOCK)
    y_b = _mlstm(proj3, gates_t, conv_w, conv_b.reshape(1, -1), bg_row, g_mlstm.reshape(1, -1))

    x1, h2p = _merge(y_a.reshape(T, ATT_GROUP_W), y_b.reshape(T, D), proj, x2, mod3,
                     w_branch_a.astype(BF16), w_branch_b.astype(BF16), w_out.astype(BF16),
                     g_post_mix.reshape(1, D), g_pre_ffn.reshape(1, D), S)

    rw_t = router_w.T.astype(BF16)
    bias_col = jnp.broadcast_to(router_bias.reshape(N_EXPERTS, 1), (N_EXPERTS, LANES))
    wsg, wsu, wsd = w_sh_gate.astype(BF16), w_sh_up.astype(BF16), w_sh_down.astype(BF16)

    tp = T // MOE_PARTS
    bm = EXPERT_BLOCK
    nb = (tp * TOP_K) // bm + N_EXPERTS
    out = None
    for part in range(MOE_PARTS):
        row0 = part * tp
        idx, wts, rank, cnt = _router(h2p, rw_t[:, :HALF], rw_t[:, HALF:], bias_col, row0, tp)

        counts = cnt[:, 0].astype(jnp.int32)
        padded = (counts + bm - 1) // bm * bm
        pend = jnp.cumsum(padded)
        pstart = pend - padded
        pstart_col = jnp.broadcast_to(pstart.astype(F32).reshape(N_EXPERTS, 1), (N_EXPERTS, LANES))
        dest = _slot_index(idx, rank, pstart_col)
        nused = (pend[-1] // bm).astype(jnp.int32).reshape(1)

        xs = _dispatch(h2p, dest, nb * bm, row0)
        ys = _expert_ffn((pstart // bm).astype(jnp.int32), (padded // bm).astype(jnp.int32), nused,
                         xs, w_exp_gate, w_exp_up, w_exp_down)
        yg = _collect(ys, dest)
        out = _final(yg, wts.T, h2p, x1, mod3, wsg, wsu, wsd, g_post_ffn.reshape(1, D), S, row0, out)
    return out.reshape(B, S, D)


SC_CORES = 2
SC_SUBCORES = 16
SC_WORKERS = SC_CORES * SC_SUBCORES
SC_ROWS = 64


def _sc_mesh():
    return plsc.VectorSubcoreMesh(core_axis_name="c", subcore_axis_name="s",
                                  num_cores=SC_CORES, num_subcores=SC_SUBCORES)


def _worker_id():
    return lax.axis_index("s") * SC_CORES + lax.axis_index("c")


def _dispatch(h2p, dest, n_slots, row0):
    T = dest.shape[1]
    per_w = T // SC_WORKERS
    nch = per_w // SC_ROWS
    idx = dest.reshape(TOP_K, SC_WORKERS, nch, SC_ROWS).transpose(1, 2, 0, 3)
    idx = idx.reshape(SC_WORKERS, nch * TOP_K, SC_ROWS)

    def body(x_hbm, idx_hbm, xs_hbm, idx_v, buf0, buf1, rsem0, rsem1, ssem0, ssem1):
        wid = _worker_id()
        base = row0 + wid * per_w
        pltpu.sync_copy(idx_hbm.at[wid], idx_v)
        bufs = ((buf0, rsem0, ssem0), (buf1, rsem1, ssem1))

        def read(c, buf, rsem):
            return pltpu.make_async_copy(x_hbm.at[pl.ds(base + c * SC_ROWS, SC_ROWS)], buf, rsem)

        def scatter(c, k, buf, ssem):
            return pltpu.make_async_copy(buf, xs_hbm.at[idx_v.at[c * TOP_K + k]], ssem)

        read(0, buf0, rsem0).start()

        @pl.loop(0, nch, step=2)
        def _(c0):
            for b in range(2):
                c = c0 + b
                buf, rsem, ssem = bufs[b]
                obuf, orsem, ossem = bufs[1 - b]
                read(c, buf, rsem).wait()

                @pl.when(c > 0)
                def _():
                    for k in range(TOP_K):
                        scatter(c - 1, k, obuf, ossem).wait()

                @pl.when(c + 1 < nch)
                def _():
                    read(c + 1, obuf, orsem).start()

                for k in range(TOP_K):
                    scatter(c, k, buf, ssem).start()

        for k in range(TOP_K):
            scatter(nch - 1, k, buf1, ssem1).wait()

    run = pl.kernel(
        body,
        out_type=jax.ShapeDtypeStruct((n_slots, HALF), jnp.uint32),
        mesh=_sc_mesh(),
        scratch_types=[pltpu.VMEM((nch * TOP_K, SC_ROWS), jnp.int32),
                       pltpu.VMEM((SC_ROWS, HALF), jnp.uint32),
                       pltpu.VMEM((SC_ROWS, HALF), jnp.uint32),
                       pltpu.SemaphoreType.DMA, pltpu.SemaphoreType.DMA,
                       pltpu.SemaphoreType.DMA, pltpu.SemaphoreType.DMA],
        name="sc_dispatch",
    )
    return run(h2p, idx)


def _collect(ys, dest):
    n = dest.size
    per_w = n // SC_WORKERS
    nch = per_w // SC_ROWS
    idx = dest.reshape(SC_WORKERS, nch, SC_ROWS)

    def body(ys_hbm, idx_hbm, out_hbm, idx_v, buf0, buf1, gsem0, gsem1, wsem0, wsem1):
        wid = _worker_id()
        base = wid * per_w
        pltpu.sync_copy(idx_hbm.at[wid], idx_v)
        bufs = ((buf0, gsem0, wsem0), (buf1, gsem1, wsem1))

        def gather(c, buf, gsem):
            return pltpu.make_async_copy(ys_hbm.at[idx_v.at[c]], buf, gsem)

        def write(c, buf, wsem):
            return pltpu.make_async_copy(buf, out_hbm.at[pl.ds(base + c * SC_ROWS, SC_ROWS)], wsem)

        gather(0, buf0, gsem0).start()

        @pl.loop(0, nch, step=2)
        def _(c0):
            for b in range(2):
                c = c0 + b
                buf, gsem, wsem = bufs[b]
                obuf, ogsem, owsem = bufs[1 - b]
                gather(c, buf, gsem).wait()

                @pl.when(c > 0)
                def _():
                    write(c - 1, obuf, owsem).wait()

                @pl.when(c + 1 < nch)
                def _():
                    gather(c + 1, obuf, ogsem).start()

                write(c, buf, wsem).start()

        write(nch - 1, buf1, wsem1).wait()

    run = pl.kernel(
        body,
        out_type=jax.ShapeDtypeStruct((n, HALF), jnp.uint32),
        mesh=_sc_mesh(),
        scratch_types=[pltpu.VMEM((nch, SC_ROWS), jnp.int32),
                       pltpu.VMEM((SC_ROWS, HALF), jnp.uint32),
                       pltpu.VMEM((SC_ROWS, HALF), jnp.uint32),
                       pltpu.SemaphoreType.DMA, pltpu.SemaphoreType.DMA,
                       pltpu.SemaphoreType.DMA, pltpu.SemaphoreType.DMA],
        name="sc_collect",
    )
    return run(ys, idx).reshape(dest.shape + (HALF,))


def kernel(x, c, positions, w_ada, b_ada, g_pre_mix, g_post_mix, g_pre_ffn, g_post_ffn, w_in, conv_w, conv_b, b_gates, g_mlstm, w_branch_a, w_branch_b, w_out, router_w, router_bias, w_exp_gate, w_exp_up, w_exp_down, w_sh_gate, w_sh_up, w_sh_down):
    depth = w_ada.shape[0]
    for l in range(depth):
        x = _layer(x, c, positions, w_ada[l], b_ada[l], g_pre_mix[l], g_post_mix[l], g_pre_ffn[l],
                   g_post_ffn[l], w_in[l], conv_w[l], conv_b[l], b_gates[l], g_mlstm[l],
                   w_branch_a[l], w_branch_b[l], w_out[l], router_w[l], router_bias[l],
                   w_exp_gate[l], w_exp_up[l], w_exp_down[l], w_sh_gate[l], w_sh_up[l], w_sh_down[l])
    return x
```

```python
import functools

import jax
import jax.numpy as jnp
from jax import lax
from jax.experimental import pallas as pl
from jax.experimental.pallas import tpu as pltpu
from jax.experimental.pallas import tpu_sc as plsc

F32 = jnp.float32
BF16 = jnp.bfloat16
HIGHEST = lax.Precision.HIGHEST
LANES = 128

D_MODEL = 1024
ATT_GROUPS = ((128, 1), (512, 4), (2048, 16))
ATT_HEAD_DIM = 64
ATT_GROUP_W = 256
ATT_BLK = 128
ATT_PAIR = 2
ROPE_THETA = 500000.0
ROPE_HALF = 8
MLSTM_HEADS = 4
MLSTM_QK_DIM = 128
MLSTM_V_DIM = 256
MLSTM_BLOCK = 128
MLSTM_GROUP = 8
CONV_WIDTH = 4
N_EXPERTS = 256
TOP_K = 8
N_GROUPS = 8
TOPK_GROUPS = 4
EXPERT_FF = 256
ROUTED_SCALE = 2.5
NORM_EPS = 1e-6
NEG = -1e30

OFF_MV, OFF_MO, OFF_GA, OFF_GB = 0, 1024, 2048, 3072
OFF_MQ, OFF_MK = 4096, 4608
OFF_AQ, OFF_AK, OFF_AV = 5120, 5888, 6656
PROJ_W = 7424
HALF = D_MODEL // 2

EXPERT_BLOCK = 512
EXPERT_SLOTS = 6
MOE_PARTS = 2
MERGE_SPLIT = 2
VMEM_LIMIT = 56 * 1024 * 1024


def _nt(a, b):
    return lax.dot_general(a, b, (((1,), (1,)), ((), ())), preferred_element_type=F32)


def _tn(a, b):
    return lax.dot_general(a, b, (((0,), (0,)), ((), ())), preferred_element_type=F32)


_sigmoid = jax.nn.sigmoid


def _silu(x):
    return x * _sigmoid(x)


def _pack_pair(lo, hi):
    lo_b = pltpu.bitcast(lo.astype(BF16).astype(F32), jnp.uint32)
    hi_b = pltpu.bitcast(hi.astype(BF16).astype(F32), jnp.uint32)
    return (lo_b >> 16) | (hi_b & jnp.uint32(0xFFFF0000))


def _unpack_pair(w):
    lo = pltpu.bitcast(w << 16, F32)
    hi = pltpu.bitcast(w & jnp.uint32(0xFFFF0000), F32)
    return lo, hi


def _mod_kernel(c_ref, w_ref, b_ref, o_ref):
    a = _silu(c_ref[...])
    o_ref[...] = jnp.dot(a, w_ref[...], preferred_element_type=F32, precision=HIGHEST) + b_ref[...]


def _adaln(c, w_ada, b_ada):
    B = c.shape[0]
    n = w_ada.shape[1]
    tn = 512
    return pl.pallas_call(
        _mod_kernel,
        grid=(n // tn,),
        in_specs=[pl.BlockSpec((B, D_MODEL), lambda j: (0, 0)),
                  pl.BlockSpec((D_MODEL, tn), lambda j: (0, j)),
                  pl.BlockSpec((1, tn), lambda j: (0, j))],
        out_specs=pl.BlockSpec((B, tn), lambda j: (0, j)),
        out_shape=jax.ShapeDtypeStruct((B, n), F32),
        name="adaln_mod",
    )(c, w_ada, b_ada.reshape(1, n))


def _proj_kernel(x_ref, mod_ref, g_ref, w_ref, wif_ref, o_ref, gates_ref, h_ref):
    @pl.when(pl.program_id(1) == 0)
    def _():
        x = x_ref[...]
        ms = jnp.mean(x * x, axis=-1, keepdims=True)
        y = x * lax.rsqrt(ms + NORM_EPS) * g_ref[...]
        h = (y * (1.0 + mod_ref[0, 1:2, :]) + mod_ref[0, 0:1, :]).astype(BF16)
        h_ref[...] = h
        gates_ref[...] = _nt(wif_ref[...], h)

    o_ref[...] = jnp.dot(h_ref[...], w_ref[...], preferred_element_type=F32).astype(BF16)


def _in_proj(x2, mod3, g_pre, w_main, w_if, seq):
    T = x2.shape[0]
    tm, tn = 1024, PROJ_W // 2
    per_b = seq // tm
    return pl.pallas_call(
        _proj_kernel,
        grid=(T // tm, PROJ_W // tn),
        in_specs=[pl.BlockSpec((tm, D_MODEL), lambda i, j: (i, 0)),
                  pl.BlockSpec((1, 6, D_MODEL), lambda i, j: (i // per_b, 0, 0)),
                  pl.BlockSpec((1, D_MODEL), lambda i, j: (0, 0)),
                  pl.BlockSpec((D_MODEL, tn), lambda i, j: (0, j)),
                  pl.BlockSpec((2 * MLSTM_HEADS, D_MODEL), lambda i, j: (0, 0))],
        out_specs=[pl.BlockSpec((tm, tn), lambda i, j: (i, j)),
                   pl.BlockSpec((2 * MLSTM_HEADS, tm), lambda i, j: (0, i))],
        out_shape=[jax.ShapeDtypeStruct((T, PROJ_W), BF16),
                   jax.ShapeDtypeStruct((2 * MLSTM_HEADS, T), F32)],
        scratch_shapes=[pltpu.VMEM((tm, D_MODEL), BF16)],
        compiler_params=pltpu.CompilerParams(
            dimension_semantics=("arbitrary", "arbitrary"), vmem_limit_bytes=VMEM_LIMIT),
        name="norm_in_proj",
    )(x2, mod3, g_pre, w_main, w_if)


def _attn_kernel(q_ref, k_ref, v_ref, cs_ref, sn_ref, o_ref, qf, kf, vf, acc, m_s, l_s, *, seq):
    g = pl.program_id(1)
    lane = lax.broadcasted_iota(jnp.int32, (ATT_BLK, LANES), 1)
    first = (lane % ATT_HEAD_DIM) < ROPE_HALF
    low_head = lane < ATT_HEAD_DIM

    def rope(x, cs, sn):
        partner = jnp.where(first, pltpu.roll(x, LANES - ROPE_HALF, 1), pltpu.roll(x, ROPE_HALF, 1))
        return x * cs + partner * sn

    def zero_pad(i, _):
        rows = pl.ds(pl.multiple_of(i * ATT_BLK, ATT_BLK), ATT_BLK)
        for hp in range(2):
            kf[hp, rows, :] = jnp.zeros((ATT_BLK, LANES), F32)
            vf[hp, rows, :] = jnp.zeros((ATT_BLK, LANES), F32)
        return 0

    lax.fori_loop(0, seq // ATT_BLK, zero_pad, 0)

    def stage(i, _):
        r = pl.multiple_of(i * ATT_BLK, ATT_BLK)
        rows = pl.ds(r, ATT_BLK)
        prow = pl.ds(pl.multiple_of(seq + i * ATT_BLK, ATT_BLK), ATT_BLK)
        cs = cs_ref[0, rows, :]
        sn = sn_ref[0, rows, :]
        for hp in range(2):
            cols = pl.ds(hp * LANES, LANES)
            qf[hp, rows, :] = rope(q_ref[0, rows, cols].astype(F32), cs, sn) * (ATT_HEAD_DIM ** -0.5)
            kf[hp, prow, :] = rope(k_ref[0, rows, cols].astype(F32), cs, sn)
            vf[hp, prow, :] = v_ref[0, rows, cols].astype(F32)
        return 0

    lax.fori_loop(0, seq // ATT_BLK, stage, 0)

    qi = lax.broadcasted_iota(jnp.int32, (ATT_BLK, 2 * ATT_BLK), 0)
    ki = lax.broadcasted_iota(jnp.int32, (ATT_BLK, 2 * ATT_BLK), 1)
    band = (ki >= qi) & (ki <= qi + ATT_BLK)

    def process(d, init):
        span = ATT_BLK * d
        single = seq == span

        def body(cp, _):
            blocks = [cp * ATT_PAIR + i for i in range(ATT_PAIR)]
            qrows, krows, valid = [], [], []
            for c in blocks:
                rho = c % d
                n = c // d
                qstart = rho + n * span
                if single:
                    kstart, nk = seq + qstart, ATT_BLK
                    valid.append(band[:, ATT_BLK:])
                else:
                    kstart, nk = seq + qstart - span, 2 * ATT_BLK
                    valid.append(band & (ki >= jnp.where(n > 0, 0, ATT_BLK)))
                qrows.append(pl.ds(qstart, ATT_BLK, stride=d) if d > 1 else pl.ds(qstart, ATT_BLK))
                krows.append(pl.ds(kstart, nk, stride=d) if d > 1 else pl.ds(kstart, nk))
            units = [(b, hp) for b in range(ATT_PAIR) for hp in range(2)]
            heads = [(u, hh) for u in range(len(units)) for hh in range(2)]
            q2 = [qf[hp, qrows[b], :] for b, hp in units]
            k2 = [kf[hp, krows[b], :].astype(BF16) for b, hp in units]
            v2 = [vf[hp, krows[b], :].astype(BF16) for b, hp in units]
            qh = [jnp.where(low_head if hh == 0 else jnp.logical_not(low_head), q2[u], 0.0).astype(BF16)
                  for u, hh in heads]
            s = [jnp.where(valid[units[u][0]], _nt(qh[i], k2[u]), NEG) for i, (u, hh) in enumerate(heads)]
            m = [jnp.max(x, axis=1, keepdims=True) for x in s]
            p = [jnp.exp(x - mx) for x, mx in zip(s, m)]
            l = [jnp.sum(x, axis=1, keepdims=True) for x in p]
            o = [jnp.dot(p[i].astype(BF16), v2[u], preferred_element_type=F32)
                 for i, (u, hh) in enumerate(heads)]
            for u, (b, hp) in enumerate(units):
                o_b = jnp.where(low_head, o[2 * u], o[2 * u + 1])
                m_b = jnp.where(low_head, m[2 * u], m[2 * u + 1])
                l_b = jnp.where(low_head, l[2 * u], l[2 * u + 1])
                if init:
                    acc[hp, qrows[b], :] = o_b
                    m_s[hp, qrows[b], :] = m_b
                    l_s[hp, qrows[b], :] = l_b
                else:
                    m_old = m_s[hp, qrows[b], :]
                    m_new = jnp.maximum(m_old, m_b)
                    a_old = jnp.exp(m_old - m_new)
                    a_new = jnp.exp(m_b - m_new)
                    acc[hp, qrows[b], :] = acc[hp, qrows[b], :] * a_old + o_b * a_new
                    l_s[hp, qrows[b], :] = l_s[hp, qrows[b], :] * a_old + l_b * a_new
                    m_s[hp, qrows[b], :] = m_new
            return 0

        lax.fori_loop(0, seq // (ATT_BLK * ATT_PAIR), body, 0)

    for gi, (_, d) in enumerate(ATT_GROUPS):
        @pl.when(g == gi)
        def _(d=d, gi=gi):
            process(d, gi == 0)

    @pl.when(g == len(ATT_GROUPS) - 1)
    def _():
        def fin(i, _):
            rows = pl.ds(pl.multiple_of(i * ATT_BLK, ATT_BLK), ATT_BLK)
            for hp in range(2):
                o_ref[0, rows, pl.ds(hp * LANES, LANES)] = (acc[hp, rows, :] / l_s[hp, rows, :]).astype(BF16)
            return 0

        lax.fori_loop(0, seq // ATT_BLK, fin, 0)


def _attention(proj3, cs, sn):
    B, S, _ = proj3.shape
    ng = len(ATT_GROUPS)
    qb, kb, vb = OFF_AQ // ATT_GROUP_W, OFF_AK // ATT_GROUP_W, OFF_AV // ATT_GROUP_W
    return pl.pallas_call(
        functools.partial(_attn_kernel, seq=S),
        grid=(B, ng),
        in_specs=[pl.BlockSpec((1, S, ATT_GROUP_W), lambda b, g: (b, 0, qb + g)),
                  pl.BlockSpec((1, S, ATT_GROUP_W), lambda b, g: (b, 0, kb + g)),
                  pl.BlockSpec((1, S, ATT_GROUP_W), lambda b, g: (b, 0, vb + g)),
                  pl.BlockSpec((1, S, LANES), lambda b, g: (b, 0, 0)),
                  pl.BlockSpec((1, S, LANES), lambda b, g: (b, 0, 0))],
        out_specs=pl.BlockSpec((1, S, ATT_GROUP_W), lambda b, g: (b, 0, 0)),
        out_shape=jax.ShapeDtypeStruct((B, S, ATT_GROUP_W), BF16),
        scratch_shapes=[pltpu.VMEM((2, S, LANES), F32),
                        pltpu.VMEM((2, 2 * S, LANES), F32),
                        pltpu.VMEM((2, 2 * S, LANES), F32),
                        pltpu.VMEM((2, S, LANES), F32),
                        pltpu.VMEM((2, S, LANES), F32),
                        pltpu.VMEM((2, S, LANES), F32)],
        compiler_params=pltpu.CompilerParams(
            dimension_semantics=("arbitrary", "arbitrary"), vmem_limit_bytes=VMEM_LIMIT),
        name="dilated_attention",
    )(proj3, proj3, proj3, cs, sn)


def _log_sigmoid(x):
    return jnp.minimum(x, 0.0) - jnp.log(1.0 + jnp.exp(-jnp.abs(x)))


def _mlstm_kernel(mq_ref, mk_ref, mv_ref, mo_ref, gt_ref, cwq_ref, cwk_ref, cbq_ref, cbk_ref,
                  bg_ref, gm_ref, o_ref, q_s, k_s, va_s, rows_s, acc_s, kv_s, inter_s, emt_s,
                  c_s, *, seq):
    h = pl.program_id(1)
    L = MLSTM_BLOCK
    NC = seq // L
    DK, DV = MLSTM_QK_DIM, MLSTM_V_DIM
    DA = DV + LANES
    nshift = CONV_WIDTH - 1

    tt = lax.broadcasted_iota(jnp.int32, (nshift * L, 2 * L), 0)
    uu = lax.broadcasted_iota(jnp.int32, (nshift * L, 2 * L), 1)
    shift_mat = (uu == L + tt % L - (tt // L + 1)).astype(BF16)
    conv_w = jnp.concatenate([cwq_ref[...], cwk_ref[...]], axis=1)
    conv_b = jnp.concatenate([cbq_ref[...], cbk_ref[...]], axis=1)
    prev = jnp.zeros((L, 2 * DK), BF16)
    for i in range(NC):
        blk = slice(i * L, (i + 1) * L)
        va_s[blk, 0:DV] = mv_ref[0, blk, :]
        va_s[blk, DV:DA] = jnp.ones((L, DA - DV), BF16)
        cur = jnp.concatenate([mq_ref[0, blk, :], mk_ref[0, blk, :]], axis=1)
        shifted = jnp.dot(shift_mat, jnp.concatenate([prev, cur], axis=0),
                          preferred_element_type=F32)
        y = conv_b + cur.astype(F32) * conv_w[nshift:nshift + 1, :]
        for s in range(nshift):
            y = y + shifted[s * L:(s + 1) * L, :] * conv_w[nshift - 1 - s:nshift - s, :]
        y = _silu(y)
        q_s[blk, :] = y[:, 0:DK].astype(BF16)
        k_s[blk, :] = (y[:, DK:2 * DK] * (DK ** -0.5)).astype(BF16)
        prev = cur

    lane = lax.broadcasted_iota(jnp.int32, (1, LANES), 1)
    bias = bg_ref[...]
    b_i = jnp.sum(jnp.where(lane == h, bias, 0.0), axis=1, keepdims=True)
    b_f = jnp.sum(jnp.where(lane == h + MLSTM_HEADS, bias, 0.0), axis=1, keepdims=True)
    ri = lax.broadcasted_iota(jnp.int32, (L, L), 0)
    ci = lax.broadcasted_iota(jnp.int32, (L, L), 1)
    causal = ci <= ri
    eye = (ri == ci).astype(F32)
    i_rows = gt_ref[h, 0] + b_i
    lf_rows = _log_sigmoid(gt_ref[h + MLSTM_HEADS, 0] + b_f)
    b_rows = jnp.dot(lf_rows, (ri <= ci).astype(F32), preferred_element_type=F32,
                     precision=HIGHEST)
    b_end = b_rows[:, L - 1:L]
    g_rows = b_end - b_rows + i_rows
    g_max = jnp.max(g_rows, axis=1, keepdims=True)
    m = jnp.zeros((1, 1), F32)
    m_prev, m_new = [], []
    for c in range(NC):
        m_prev.append(m)
        m = jnp.maximum(b_end[c:c + 1, :] + m, g_max[c:c + 1, :])
        m_new.append(m)
    m_prev = jnp.concatenate(m_prev, axis=0)
    m_new = jnp.concatenate(m_new, axis=0)
    rows_s[0] = b_rows
    rows_s[1] = jnp.exp(g_rows - m_new)
    rows_s[2] = b_rows - i_rows
    rows_s[3] = jnp.broadcast_to(m_prev, (NC, L))
    rows_s[4] = jnp.broadcast_to(jnp.exp(b_end + m_prev - m_new), (NC, L))

    r2 = lax.broadcasted_iota(jnp.int32, (2 * L, 2 * L), 0)
    c2 = lax.broadcasted_iota(jnp.int32, (2 * L, 2 * L), 1)
    ones_blk = ((r2 < L) == (c2 < L)).astype(BF16)

    G = MLSTM_GROUP

    def local(cg, _):
        cs = [cg * G + i for i in range(G)]
        rows = [pl.ds(pl.multiple_of(c * L, L), L) for c in cs]
        b_r = [rows_s[0, pl.ds(c, 1), :] for c in cs]
        w_r = [rows_s[1, pl.ds(c, 1), :] for c in cs]
        u_r = [rows_s[2, pl.ds(c, 1), :] for c in cs]
        mp = [rows_s[3, pl.ds(c, 1), :] for c in cs]
        q = [q_s[r, :] for r in rows]
        k = [k_s[r, :] for r in rows]
        va = [va_s[r, :] for r in rows]
        qk = [_nt(a, b) for a, b in zip(q, k)]
        x2 = [jnp.concatenate([eye * a, eye * b], axis=1) for a, b in zip(b_r, w_r)]
        hi = [x.astype(BF16) for x in x2]
        lo = [(x - h_.astype(F32)).astype(BF16) for x, h_ in zip(x2, hi)]
        yb = [jnp.dot(h_, ones_blk, preferred_element_type=F32)
              + jnp.dot(l_, ones_blk, preferred_element_type=F32) for h_, l_ in zip(hi, lo)]
        b_b = [y[:, 0:L] for y in yb]
        w_b = [y[:, L:2 * L] for y in yb]
        for i in range(G):
            kv_s[cs[i]] = _tn((w_b[i] * k[i].astype(F32)).astype(BF16), va[i])
        dmat = [jnp.where(causal, b - u, NEG) for b, u in zip(b_b, u_r)]
        m_t = [jnp.maximum(b + m_, jnp.max(d, axis=1, keepdims=True))
               for b, m_, d in zip(b_b, mp, dmat)]
        sc = [a * jnp.exp(d - m_) for a, d, m_ in zip(qk, dmat, m_t)]
        for i in range(G):
            acc_s[rows[i], :] = jnp.dot(sc[i].astype(BF16), va[i], preferred_element_type=F32)
            inter_s[rows[i], :] = jnp.exp(b_b[i] + mp[i] - m_t[i])
            emt_s[rows[i], :] = jnp.exp(-m_t[i])
        return 0

    lax.fori_loop(0, NC // G, local, 0)

    g_row = gm_ref[...]
    c_s[...] = jnp.zeros((DK, DA), F32)

    def recur(cg, _):
        cs = [cg * G + i for i in range(G)]
        rows = [pl.ds(pl.multiple_of(c * L, L), L) for c in cs]
        states = [c_s[...]]
        for c in cs:
            dec = rows_s[4, pl.ds(c, 1), :]
            states.append(jnp.concatenate([dec, dec, dec], axis=1) * states[-1] + kv_s[c])
        c_s[...] = states[G]
        read = [jnp.dot(q_s[r, :], st.astype(BF16), preferred_element_type=F32)
                for r, st in zip(rows, states)]
        inter = [inter_s[r, :] for r in rows]
        out = [acc_s[r, :] + jnp.concatenate([it, it, it], axis=1) * rd
               for r, it, rd in zip(rows, inter, read)]
        emt = [emt_s[r, :] for r in rows]
        nrm = [jnp.maximum(jnp.abs(jnp.concatenate([o[:, DV:DA], o[:, DV:DA]], axis=1)),
                           jnp.concatenate([e_, e_], axis=1)) for o, e_ in zip(out, emt)]
        hh = [o[:, 0:DV] / n_ for o, n_ in zip(out, nrm)]
        ms = [jnp.mean(x * x, axis=1, keepdims=True) for x in hh]
        hn = [x * lax.rsqrt(m_ + NORM_EPS) * g_row for x, m_ in zip(hh, ms)]
        for i in range(G):
            o_ref[0, rows[i], :] = (hn[i] * _sigmoid(mo_ref[0, rows[i], :].astype(F32))).astype(BF16)
        return 0

    lax.fori_loop(0, NC // G, recur, 0)


def _mlstm(proj3, gates_t, conv_w, conv_b, bg_row, g_mlstm):
    B, S, _ = proj3.shape
    H, DK, DV = MLSTM_HEADS, MLSTM_QK_DIM, MLSTM_V_DIM
    L = MLSTM_BLOCK
    NC = S // L
    DA = DV + LANES
    qb, kb = OFF_MQ // DK, OFF_MK // DK
    vb, ob = OFF_MV // DV, OFF_MO // DV
    nq = H
    return pl.pallas_call(
        functools.partial(_mlstm_kernel, seq=S),
        grid=(B, H),
        in_specs=[pl.BlockSpec((1, S, DK), lambda b, h: (b, 0, qb + h)),
                  pl.BlockSpec((1, S, DK), lambda b, h: (b, 0, kb + h)),
                  pl.BlockSpec((1, S, DV), lambda b, h: (b, 0, vb + h)),
                  pl.BlockSpec((1, S, DV), lambda b, h: (b, 0, ob + h)),
                  pl.BlockSpec((2 * H, 1, NC, L), lambda b, h: (0, b, 0, 0)),
                  pl.BlockSpec((CONV_WIDTH, DK), lambda b, h: (0, h)),
                  pl.BlockSpec((CONV_WIDTH, DK), lambda b, h: (0, nq + h)),
                  pl.BlockSpec((1, DK), lambda b, h: (0, h)),
                  pl.BlockSpec((1, DK), lambda b, h: (0, nq + h)),
                  pl.BlockSpec((1, LANES), lambda b, h: (0, 0)),
                  pl.BlockSpec((1, DV), lambda b, h: (0, h))],
        out_specs=pl.BlockSpec((1, S, DV), lambda b, h: (b, 0, h)),
        out_shape=jax.ShapeDtypeStruct((B, S, H * DV), BF16),
        scratch_shapes=[pltpu.VMEM((S, DK), BF16),
                        pltpu.VMEM((S, DK), BF16),
                        pltpu.VMEM((S, DA), BF16),
                        pltpu.VMEM((5, NC, L), F32),
                        pltpu.VMEM((S, DA), F32),
                        pltpu.VMEM((NC, DK, DA), F32),
                        pltpu.VMEM((S, L), F32),
                        pltpu.VMEM((S, L), F32),
                        pltpu.VMEM((DK, DA), F32)],
        compiler_params=pltpu.CompilerParams(
            dimension_semantics=("arbitrary", "arbitrary"), vmem_limit_bytes=VMEM_LIMIT),
        name="mlstm_chunkwise",
    )(proj3, proj3, proj3, proj3, gates_t, conv_w, conv_w, conv_b, conv_b, bg_row, g_mlstm)


def _rms(y, g):
    ms = jnp.mean(y * y, axis=-1, keepdims=True)
    return y * lax.rsqrt(ms + NORM_EPS) * g


def _merge_kernel(ya_ref, yb_ref, ga_ref, gb_ref, x_ref, mod_ref, wa_ref, wb_ref, wo_ref,
                  gpost_ref, gpre_ref, x1_ref, h2_ref):
    tm = x_ref.shape[0]
    slabs = [pl.ds(s * (tm // MERGE_SPLIT), tm // MERGE_SPLIT) for s in range(MERGE_SPLIT)]
    pa = [jnp.dot(ya_ref[r, :], wa_ref[...], preferred_element_type=F32) for r in slabs]
    pb = [jnp.dot(yb_ref[r, :], wb_ref[...], preferred_element_type=F32) for r in slabs]
    merged = [_sigmoid(ga_ref[r, :].astype(F32)) * a + _sigmoid(gb_ref[r, :].astype(F32)) * b
              for r, a, b in zip(slabs, pa, pb)]
    y = [jnp.dot(m.astype(BF16), wo_ref[...], preferred_element_type=F32) for m in merged]
    x1 = [x_ref[r, :] + mod_ref[0, 2:3, :] * _rms(v, gpost_ref[...]) for r, v in zip(slabs, y)]
    for r, v in zip(slabs, x1):
        x1_ref[r, :] = v
    h2 = [_rms(v, gpre_ref[...]) * (1.0 + mod_ref[0, 4:5, :]) + mod_ref[0, 3:4, :] for v in x1]
    for r, v in zip(slabs, h2):
        h2_ref[r, :] = _pack_pair(v[:, :HALF], v[:, HALF:])


def _merge(ya2, yb2, proj2, x2, mod3, wa, wb, wo, g_post, g_pre, seq):
    T = x2.shape[0]
    tm = 512 * MERGE_SPLIT
    per_b = seq // tm
    full = lambda shape: pl.BlockSpec(shape, lambda i: (0,) * len(shape))
    return pl.pallas_call(
        _merge_kernel,
        grid=(T // tm,),
        in_specs=[pl.BlockSpec((tm, ATT_GROUP_W), lambda i: (i, 0)),
                  pl.BlockSpec((tm, D_MODEL), lambda i: (i, 0)),
                  pl.BlockSpec((tm, D_MODEL), lambda i: (i, OFF_GA // D_MODEL)),
                  pl.BlockSpec((tm, D_MODEL), lambda i: (i, OFF_GB // D_MODEL)),
                  pl.BlockSpec((tm, D_MODEL), lambda i: (i, 0)),
                  pl.BlockSpec((1, 6, D_MODEL), lambda i: (i // per_b, 0, 0)),
                  full((ATT_GROUP_W, D_MODEL)), full((D_MODEL, D_MODEL)), full((D_MODEL, D_MODEL)),
                  full((1, D_MODEL)), full((1, D_MODEL))],
        out_specs=[pl.BlockSpec((tm, D_MODEL), lambda i: (i, 0)),
                   pl.BlockSpec((tm, HALF), lambda i: (i, 0))],
        out_shape=[jax.ShapeDtypeStruct((T, D_MODEL), F32),
                   jax.ShapeDtypeStruct((T, HALF), jnp.uint32)],
        compiler_params=pltpu.CompilerParams(
            dimension_semantics=("arbitrary",), vmem_limit_bytes=VMEM_LIMIT),
        name="merge_out_proj",
    )(ya2, yb2, proj2, proj2, x2, mod3, wa, wb, wo, g_post, g_pre)


def _router_kernel(h2_ref, rlo_ref, rhi_ref, bias_ref, idx_ref, w_ref, rank_ref, cnt_ref):
    E = N_EXPERTS
    tr = h2_ref.shape[0]
    gsz = E // N_GROUPS

    @pl.when(pl.program_id(0) == 0)
    def _():
        cnt_ref[...] = jnp.zeros(cnt_ref.shape, F32)

    lo, hi = _unpack_pair(h2_ref[...])
    logits = _nt(rlo_ref[...], lo.astype(BF16)) + _nt(rhi_ref[...], hi.astype(BF16))
    scores = _sigmoid(logits)
    sel = scores + bias_ref[:, 0:1]

    gi = lax.broadcasted_iota(jnp.int32, (gsz, tr), 0).astype(F32)
    gs_rows = []
    for g in range(N_GROUPS):
        blk = sel[g * gsz:(g + 1) * gsz, :]
        m1 = jnp.max(blk, axis=0, keepdims=True)
        a1 = jnp.min(jnp.where(blk == m1, gi, float(E)), axis=0, keepdims=True)
        m2 = jnp.max(jnp.where(gi == a1, -jnp.inf, blk), axis=0, keepdims=True)
        gs_rows.append(m1 + m2)
    gs = jnp.concatenate(gs_rows, axis=0)
    g8 = lax.broadcasted_iota(jnp.int32, (N_GROUPS, tr), 0).astype(F32)
    gmask = jnp.zeros((N_GROUPS, tr), F32)
    for _ in range(TOPK_GROUPS):
        m = jnp.max(gs, axis=0, keepdims=True)
        a = jnp.min(jnp.where(gs == m, g8, float(E)), axis=0, keepdims=True)
        hit = g8 == a
        gmask = jnp.where(hit, 1.0, gmask)
        gs = jnp.where(hit, -jnp.inf, gs)
    selm = jnp.concatenate(
        [jnp.where(gmask[g:g + 1, :] > 0.0, sel[g * gsz:(g + 1) * gsz, :], -jnp.inf)
         for g in range(N_GROUPS)], axis=0)

    ei = lax.broadcasted_iota(jnp.int32, (E, tr), 0).astype(F32)
    picks, weights, hits = [], [], []
    candidates = selm
    for _ in range(TOP_K):
        m = jnp.max(selm, axis=0, keepdims=True)
        a = jnp.min(jnp.where(selm == m, ei, float(E)), axis=0, keepdims=True)
        hit = ei == a
        picks.append(a)
        hits.append(hit)
        weights.append(jnp.sum(jnp.where(hit, scores, 0.0), axis=0, keepdims=True))
        selm = jnp.where(hit, -jnp.inf, selm)
    chosen = jnp.where(selm != candidates, 1.0, 0.0)
    wsum = weights[0]
    for w in weights[1:]:
        wsum = wsum + w

    ti = lax.broadcasted_iota(jnp.int32, (tr, tr), 0)
    tj = lax.broadcasted_iota(jnp.int32, (tr, tr), 1)
    before = (ti < tj).astype(BF16)
    pos = jnp.dot(chosen.astype(BF16), before, preferred_element_type=F32) + cnt_ref[:, 0:1]
    ranks = [jnp.sum(jnp.where(hit, pos, 0.0), axis=0, keepdims=True) for hit in hits]
    cnt_ref[...] = cnt_ref[...] + jnp.sum(chosen, axis=1, keepdims=True)

    idx_ref[...] = jnp.concatenate(picks, axis=0).astype(jnp.int32)
    w_ref[...] = jnp.concatenate([w / wsum * ROUTED_SCALE for w in weights], axis=0)
    rank_ref[...] = jnp.concatenate(ranks, axis=0).astype(jnp.int32)


def _router(h2p, r_lo, r_hi, bias_col, row0, T):
    tr = 512
    off = row0 // tr
    full = lambda shape: pl.BlockSpec(shape, lambda i: (0,) * len(shape))
    return pl.pallas_call(
        _router_kernel,
        grid=(T // tr,),
        in_specs=[pl.BlockSpec((tr, HALF), lambda i: (i + off, 0)),
                  full((N_EXPERTS, HALF)), full((N_EXPERTS, HALF)), full((N_EXPERTS, LANES))],
        out_specs=[pl.BlockSpec((TOP_K, tr), lambda i: (0, i)),
                   pl.BlockSpec((TOP_K, tr), lambda i: (0, i)),
                   pl.BlockSpec((TOP_K, tr), lambda i: (0, i)),
                   full((N_EXPERTS, LANES))],
        out_shape=[jax.ShapeDtypeStruct((TOP_K, T), jnp.int32),
                   jax.ShapeDtypeStruct((TOP_K, T), F32),
                   jax.ShapeDtypeStruct((TOP_K, T), jnp.int32),
                   jax.ShapeDtypeStruct((N_EXPERTS, LANES), F32)],
        compiler_params=pltpu.CompilerParams(
            dimension_semantics=("arbitrary",), vmem_limit_bytes=VMEM_LIMIT),
        name="router_topk",
    )(h2p, r_lo, r_hi, bias_col)


def _dest_kernel(idx_ref, rank_ref, pstart_ref, dest_ref):
    tr = idx_ref.shape[1]
    ei = lax.broadcasted_iota(jnp.int32, (N_EXPERTS, tr), 0)
    start = pstart_ref[:, 0:1]
    rows = []
    for k in range(TOP_K):
        hit = ei == idx_ref[k:k + 1, :]
        rows.append(jnp.sum(jnp.where(hit, start, 0.0), axis=0, keepdims=True))
    dest_ref[...] = jnp.concatenate(rows, axis=0).astype(jnp.int32) + rank_ref[...]


def _slot_index(idx, rank, pstart_col):
    T = idx.shape[1]
    tr = 1024
    return pl.pallas_call(
        _dest_kernel,
        grid=(T // tr,),
        in_specs=[pl.BlockSpec((TOP_K, tr), lambda i: (0, i)),
                  pl.BlockSpec((TOP_K, tr), lambda i: (0, i)),
                  pl.BlockSpec((N_EXPERTS, LANES), lambda i: (0, 0))],
        out_specs=pl.BlockSpec((TOP_K, tr), lambda i: (0, i)),
        out_shape=jax.ShapeDtypeStruct((TOP_K, T), jnp.int32),
        name="slot_index",
    )(idx, rank, pstart_col)


def _ffn_kernel(first_ref, nblk_ref, nused_ref, xs_hbm, wg_ref, wu_ref, wd_ref, ys_hbm,
                xbuf, ybuf, in_sem, out_sem, wg_s, wu_s, wd_s):
    e = pl.program_id(0)
    bm = EXPERT_BLOCK
    ns = EXPERT_SLOTS
    nused = nused_ref[0]
    first = first_ref[e]
    n = nblk_ref[e]

    def in_copy(g):
        slot = g % ns
        return pltpu.make_async_copy(xs_hbm.at[pl.ds(g * bm, bm)], xbuf.at[slot], in_sem.at[slot])

    def out_copy(g):
        slot = g % ns
        return pltpu.make_async_copy(ybuf.at[slot], ys_hbm.at[pl.ds(g * bm, bm)], out_sem.at[slot])

    def fetch(g):
        @pl.when(g < nused)
        def _():
            in_copy(g).start()

    def release(g):
        @pl.when(g >= ns)
        def _():
            out_copy(g - ns).wait()

    def ffn(g):
        lo, hi = _unpack_pair(xbuf[g % ns])
        x = jnp.concatenate([lo.astype(BF16), hi.astype(BF16)], axis=1)
        gate = jnp.dot(x, wg_s[...], preferred_element_type=F32)
        up = jnp.dot(x, wu_s[...], preferred_element_type=F32)
        hid = (_silu(gate) * up).astype(BF16)
        return jnp.dot(hid, wd_s[...], preferred_element_type=F32)

    def pack(g, out):
        ybuf[g % ns] = _pack_pair(out[:, :HALF], out[:, HALF:])

    @pl.when(e == 0)
    def _():
        for q in range(ns - 1):
            fetch(q)

    @pl.when(n > 0)
    def _():
        for packed, dst in ((wg_ref, wg_s), (wu_ref, wu_s), (wd_ref, wd_s)):
            lo, hi = _unpack_pair(packed[0])
            half = dst.shape[0] // 2
            dst[0:half, :] = lo.astype(BF16)
            dst[half:, :] = hi.astype(BF16)

        def two_blocks(j, _):
            g = first + 2 * j
            in_copy(g).wait()
            in_copy(g + 1).wait()
            fetch(g + ns - 1)
            release(g)
            release(g + 1)
            out_a = ffn(g)
            out_b = ffn(g + 1)
            pack(g, out_a)
            pack(g + 1, out_b)
            out_copy(g).start()
            out_copy(g + 1).start()
            fetch(g + ns)
            return 0

        lax.fori_loop(0, n // 2, two_blocks, 0)

        @pl.when(n % 2 == 1)
        def _():
            g = first + n - 1
            in_copy(g).wait()
            fetch(g + ns - 1)
            release(g)
            pack(g, ffn(g))
            out_copy(g).start()

    @pl.when(e == pl.num_programs(0) - 1)
    def _():
        for q in range(ns, 0, -1):
            @pl.when(nused >= q)
            def _(q=q):
                out_copy(nused - q).wait()


def _expert_ffn(first_blk, nblk, nused, xs, w_gate, w_up, w_down):
    P = xs.shape[0]
    bm = EXPERT_BLOCK
    w_map = lambda e, *_: (e, 0, 0)
    grid_spec = pltpu.PrefetchScalarGridSpec(
        num_scalar_prefetch=3,
        grid=(w_gate.shape[0],),
        in_specs=[pl.BlockSpec(memory_space=pl.ANY),
                  pl.BlockSpec((1, D_MODEL // 2, EXPERT_FF), w_map),
                  pl.BlockSpec((1, D_MODEL // 2, EXPERT_FF), w_map),
                  pl.BlockSpec((1, EXPERT_FF // 2, D_MODEL), w_map)],
        out_specs=pl.BlockSpec(memory_space=pl.ANY),
        scratch_shapes=[pltpu.VMEM((EXPERT_SLOTS, bm, HALF), jnp.uint32),
                        pltpu.VMEM((EXPERT_SLOTS, bm, HALF), jnp.uint32),
                        pltpu.SemaphoreType.DMA((EXPERT_SLOTS,)),
                        pltpu.SemaphoreType.DMA((EXPERT_SLOTS,)),
                        pltpu.VMEM((D_MODEL, EXPERT_FF), BF16),
                        pltpu.VMEM((D_MODEL, EXPERT_FF), BF16),
                        pltpu.VMEM((EXPERT_FF, D_MODEL), BF16)],
    )
    return pl.pallas_call(
        _ffn_kernel,
        grid_spec=grid_spec,
        out_shape=jax.ShapeDtypeStruct((P, HALF), jnp.uint32),
        compiler_params=pltpu.CompilerParams(
            dimension_semantics=("arbitrary",), vmem_limit_bytes=VMEM_LIMIT),
        name="routed_experts",
    )(first_blk, nblk, nused, xs, w_gate, w_up, w_down)


def _final_kernel(yg_ref, w_ref, h2_ref, x1_ref, mod_ref, wsg_ref, wsu_ref, wsd_ref, gpost_ref, *rest):
    o_ref = rest[-1]
    lo, hi = _unpack_pair(h2_ref[...])
    h2 = jnp.concatenate([lo.astype(BF16), hi.astype(BF16)], axis=1)
    gate = jnp.dot(h2, wsg_ref[...], preferred_element_type=F32)
    up = jnp.dot(h2, wsu_ref[...], preferred_element_type=F32)
    shared = jnp.dot((_silu(gate) * up).astype(BF16), wsd_ref[...], preferred_element_type=F32)
    y_lo = shared[:, :HALF]
    y_hi = shared[:, HALF:]
    for k in range(TOP_K):
        r_lo, r_hi = _unpack_pair(yg_ref[k])
        wk = w_ref[:, k:k + 1]
        y_lo = y_lo + wk * r_lo
        y_hi = y_hi + wk * r_hi
    ms = (jnp.sum(y_lo * y_lo, axis=-1, keepdims=True)
          + jnp.sum(y_hi * y_hi, axis=-1, keepdims=True)) * (1.0 / D_MODEL)
    inv = lax.rsqrt(ms + NORM_EPS)
    o_ref[:, 0:HALF] = x1_ref[:, 0:HALF] + mod_ref[0, 5:6, 0:HALF] * (y_lo * inv * gpost_ref[:, 0:HALF])
    o_ref[:, HALF:] = x1_ref[:, HALF:] + mod_ref[0, 5:6, HALF:] * (y_hi * inv * gpost_ref[:, HALF:])


def _final(yg, w_tk, h2p, x1, mod3, wsg, wsu, wsd, g_post, seq, row0, out_prev):
    T = x1.shape[0]
    tp = yg.shape[1]
    tm = 512
    per_b = seq // tm
    off = row0 // tm
    full = lambda shape: pl.BlockSpec(shape, lambda i: (0,) * len(shape))
    in_specs = [pl.BlockSpec((TOP_K, tm, HALF), lambda i: (0, i, 0)),
                pl.BlockSpec((tm, TOP_K), lambda i: (i, 0)),
                pl.BlockSpec((tm, HALF), lambda i: (i + off, 0)),
                pl.BlockSpec((tm, D_MODEL), lambda i: (i + off, 0)),
                pl.BlockSpec((1, 6, D_MODEL), lambda i: ((i + off) // per_b, 0, 0)),
                full((D_MODEL, EXPERT_FF)), full((D_MODEL, EXPERT_FF)), full((EXPERT_FF, D_MODEL)),
                full((1, D_MODEL))]
    args = [yg, w_tk, h2p, x1, mod3, wsg, wsu, wsd, g_post]
    aliases = {}
    if out_prev is not None:
        in_specs.append(pl.BlockSpec(memory_space=pl.ANY))
        args.append(out_prev)
        aliases = {len(args) - 1: 0}
    return pl.pallas_call(
        _final_kernel,
        grid=(tp // tm,),
        in_specs=in_specs,
        out_specs=pl.BlockSpec((tm, D_MODEL), lambda i: (i + off, 0)),
        out_shape=jax.ShapeDtypeStruct((T, D_MODEL), F32),
        input_output_aliases=aliases,
        compiler_params=pltpu.CompilerParams(
            dimension_semantics=("arbitrary",), vmem_limit_bytes=VMEM_LIMIT),
        name="shared_expert_combine",
    )(*args)


def _rope_tables(positions):
    inv = jnp.power(ROPE_THETA, -jnp.arange(ROPE_HALF, dtype=F32) / ROPE_HALF)
    ang = positions.astype(F32)[..., None] * inv
    cos, sin = jnp.cos(ang), jnp.sin(ang)
    rest = ATT_HEAD_DIM - 2 * ROPE_HALF
    cs = jnp.concatenate([cos, cos, jnp.ones(ang.shape[:-1] + (rest,), F32)], axis=-1)
    sn = jnp.concatenate([-sin, sin, jnp.zeros(ang.shape[:-1] + (rest,), F32)], axis=-1)
    return jnp.tile(cs, (1, 1, 2)), jnp.tile(sn, (1, 1, 2))


def _layer(x, c, positions, w_ada, b_ada, g_pre_mix, g_post_mix, g_pre_ffn, g_post_ffn,
           w_in, conv_w, conv_b, b_gates, g_mlstm, w_branch_a, w_branch_b, w_out,
           router_w, router_bias, w_exp_gate, w_exp_up, w_exp_down, w_sh_gate, w_sh_up, w_sh_down):
    B, S, D = x.shape
    T = B * S
    H = MLSTM_HEADS
    x2 = x.reshape(T, D)

    mod3 = _adaln(c, w_ada, b_ada).reshape(B, 6, D)

    a_w = 3 * ATT_GROUP_W
    o_mq = 3 * a_w
    o_mk = o_mq + H * MLSTM_QK_DIM
    o_mv = o_mk + H * MLSTM_QK_DIM
    o_mo = o_mv + H * MLSTM_V_DIM
    o_mi = o_mo + H * MLSTM_V_DIM
    o_ga = o_mi + 2 * H
    o_gb = o_ga + D
    w_bf = w_in.astype(BF16)
    w_main = jnp.concatenate(
        [w_bf[:, o_mv:o_mi], w_bf[:, o_ga:o_gb + D], w_bf[:, o_mq:o_mv], w_bf[:, 0:o_mq]], axis=1)
    w_if = w_bf[:, o_mi:o_ga].T

    proj, gates = _in_proj(x2, mod3, g_pre_mix.reshape(1, D), w_main, w_if, S)
    proj3 = proj.reshape(B, S, PROJ_W)

    cs, sn = _rope_tables(positions)
    y_a = _attention(proj3, cs, sn)

    bg_row = jnp.pad(b_gates.reshape(1, 2 * H), ((0, 0), (0, LANES - 2 * H)))
    gates_t = gates.reshape(2 * H, B, S // MLSTM_BLOCK, MLSTM_BLOCK)
    y_b = _mlstm(proj3, gates_t, conv_w, conv_b.reshape(1, -1), bg_row, g_mlstm.reshape(1, -1))

    x1, h2p = _merge(y_a.reshape(T, ATT_GROUP_W), y_b.reshape(T, D), proj, x2, mod3,
                     w_branch_a.astype(BF16), w_branch_b.astype(BF16), w_out.astype(BF16),
                     g_post_mix.reshape(1, D), g_pre_ffn.reshape(1, D), S)

    rw_t = router_w.T.astype(BF16)
    bias_col = jnp.broadcast_to(router_bias.reshape(N_EXPERTS, 1), (N_EXPERTS, LANES))
    wsg, wsu, wsd = w_sh_gate.astype(BF16), w_sh_up.astype(BF16), w_sh_down.astype(BF16)

    tp = T // MOE_PARTS
    bm = EXPERT_BLOCK
    nb = (tp * TOP_K) // bm + N_EXPERTS
    out = None
    wg_p, wu_p, wd_p = (_pack_weight_rows(w) for w in (w_exp_gate, w_exp_up, w_exp_down))
    for part in range(MOE_PARTS):
        row0 = part * tp
        idx, wts, rank, cnt = _router(h2p, rw_t[:, :HALF], rw_t[:, HALF:], bias_col, row0, tp)

        counts = cnt[:, 0].astype(jnp.int32)
        padded = (counts + bm - 1) // bm * bm
        pend = jnp.cumsum(padded)
        pstart = pend - padded
        pstart_col = jnp.broadcast_to(pstart.astype(F32).reshape(N_EXPERTS, 1), (N_EXPERTS, LANES))
        dest = _slot_index(idx, rank, pstart_col)
        nused = (pend[-1] // bm).astype(jnp.int32).reshape(1)

        xs = _dispatch(h2p, dest, nb * bm, row0)
        ys = _expert_ffn((pstart // bm).astype(jnp.int32), (padded // bm).astype(jnp.int32), nused,
                         xs, wg_p, wu_p, wd_p)
        yg = _collect(ys, dest)
        out = _final(yg, wts.T, h2p, x1, mod3, wsg, wsu, wsd, g_post_ffn.reshape(1, D), S, row0, out)
    return out.reshape(B, S, D)


SC_CORES = 2
SC_SUBCORES = 16
SC_WORKERS = SC_CORES * SC_SUBCORES
SC_ROWS = 64


def _sc_mesh():
    return plsc.VectorSubcoreMesh(core_axis_name="c", subcore_axis_name="s",
                                  num_cores=SC_CORES, num_subcores=SC_SUBCORES)


def _worker_id():
    return lax.axis_index("s") * SC_CORES + lax.axis_index("c")


def _dispatch(h2p, dest, n_slots, row0):
    T = dest.shape[1]
    per_w = T // SC_WORKERS
    nch = per_w // SC_ROWS
    idx = dest.reshape(TOP_K, SC_WORKERS, nch, SC_ROWS).transpose(1, 2, 0, 3)
    idx = idx.reshape(SC_WORKERS, nch * TOP_K, SC_ROWS)

    def body(x_hbm, idx_hbm, xs_hbm, idx_v, buf0, buf1, rsem0, rsem1, ssem0, ssem1):
        wid = _worker_id()
        base = row0 + wid * per_w
        pltpu.sync_copy(idx_hbm.at[wid], idx_v)
        bufs = ((buf0, rsem0, ssem0), (buf1, rsem1, ssem1))

        def read(c, buf, rsem):
            return pltpu.make_async_copy(x_hbm.at[pl.ds(base + c * SC_ROWS, SC_ROWS)], buf, rsem)

        def scatter(c, k, buf, ssem):
            return pltpu.make_async_copy(buf, xs_hbm.at[idx_v.at[c * TOP_K + k]], ssem)

        read(0, buf0, rsem0).start()

        @pl.loop(0, nch, step=2)
        def _(c0):
            for b in range(2):
                c = c0 + b
                buf, rsem, ssem = bufs[b]
                obuf, orsem, ossem = bufs[1 - b]
                read(c, buf, rsem).wait()

                @pl.when(c > 0)
                def _():
                    for k in range(TOP_K):
                        scatter(c - 1, k, obuf, ossem).wait()

                @pl.when(c + 1 < nch)
                def _():
                    read(c + 1, obuf, orsem).start()

                for k in range(TOP_K):
                    scatter(c, k, buf, ssem).start()

        for k in range(TOP_K):
            scatter(nch - 1, k, buf1, ssem1).wait()

    run = pl.kernel(
        body,
        out_type=jax.ShapeDtypeStruct((n_slots, HALF), jnp.uint32),
        mesh=_sc_mesh(),
        scratch_types=[pltpu.VMEM((nch * TOP_K, SC_ROWS), jnp.int32),
                       pltpu.VMEM((SC_ROWS, HALF), jnp.uint32),
                       pltpu.VMEM((SC_ROWS, HALF), jnp.uint32),
                       pltpu.SemaphoreType.DMA, pltpu.SemaphoreType.DMA,
                       pltpu.SemaphoreType.DMA, pltpu.SemaphoreType.DMA],
        name="sc_dispatch",
    )
    return run(h2p, idx)


SC_PACK_ROWS = 64
SC_PACK_COLS = 256
SC_LANES = 16


def _pack_weight_rows(w):
    E, R, C = w.shape
    hb = R // 2 // SC_PACK_ROWS
    w2 = w.reshape(E * R, C)

    def body(w_hbm, out_hbm):
        def block(lo_v, hi_v, out_v):
            @pl.loop(0, SC_PACK_ROWS)
            def _(r):
                @pl.loop(0, SC_PACK_COLS, step=SC_LANES)
                def _(c):
                    cols = pl.ds(c, SC_LANES)
                    pair = plsc.pack(lo_v[r, cols], hi_v[r, cols], format=plsc.PackFormat.INTERLEAVED)
                    out_v[r, cols] = plsc.bitcast(pair, jnp.uint32)

        blk = (SC_PACK_ROWS, SC_PACK_COLS)
        pltpu.emit_pipeline(
            block,
            grid=(E * hb, C // SC_PACK_COLS),
            in_specs=[pl.BlockSpec(blk, lambda i, j: ((i // hb) * 2 * hb + i % hb, j)),
                      pl.BlockSpec(blk, lambda i, j: ((i // hb) * 2 * hb + hb + i % hb, j))],
            out_specs=[pl.BlockSpec(blk, lambda i, j: (i, j))],
            core_axis_name=("c", "s"),
            dimension_semantics=(pltpu.PARALLEL, pltpu.PARALLEL),
        )(w_hbm, w_hbm, out_hbm)

    run = pl.kernel(body, out_type=jax.ShapeDtypeStruct((E * R // 2, C), jnp.uint32),
                    mesh=_sc_mesh(), scratch_types=[], name="sc_pack_weights",
                    compiler_params=pltpu.CompilerParams(needs_layout_passes=False))
    return run(w2).reshape(E, R // 2, C)


def _collect(ys, dest):
    n = dest.size
    per_w = n // SC_WORKERS
    nch = per_w // SC_ROWS
    idx = dest.reshape(SC_WORKERS, nch, SC_ROWS)

    def body(ys_hbm, idx_hbm, out_hbm, idx_v, buf0, buf1, gsem0, gsem1, wsem0, wsem1):
        wid = _worker_id()
        base = wid * per_w
        pltpu.sync_copy(idx_hbm.at[wid], idx_v)
        bufs = ((buf0, gsem0, wsem0), (buf1, gsem1, wsem1))

        def gather(c, buf, gsem):
            return pltpu.make_async_copy(ys_hbm.at[idx_v.at[c]], buf, gsem)

        def write(c, buf, wsem):
            return pltpu.make_async_copy(buf, out_hbm.at[pl.ds(base + c * SC_ROWS, SC_ROWS)], wsem)

        gather(0, buf0, gsem0).start()

        @pl.loop(0, nch, step=2)
        def _(c0):
            for b in range(2):
                c = c0 + b
                buf, gsem, wsem = bufs[b]
                obuf, ogsem, owsem = bufs[1 - b]
                gather(c, buf, gsem).wait()

                @pl.when(c > 0)
                def _():
                    write(c - 1, obuf, owsem).wait()

                @pl.when(c + 1 < nch)
                def _():
                    gather(c + 1, obuf, ogsem).start()

                write(c, buf, wsem).start()

        write(nch - 1, buf1, wsem1).wait()

    run = pl.kernel(
        body,
        out_type=jax.ShapeDtypeStruct((n, HALF), jnp.uint32),
        mesh=_sc_mesh(),
        scratch_types=[pltpu.VMEM((nch, SC_ROWS), jnp.int32),
                       pltpu.VMEM((SC_ROWS, HALF), jnp.uint32),
                       pltpu.VMEM((SC_ROWS, HALF), jnp.uint32),
                       pltpu.SemaphoreType.DMA, pltpu.SemaphoreType.DMA,
                       pltpu.SemaphoreType.DMA, pltpu.SemaphoreType.DMA],
        name="sc_collect",
    )
    return run(ys, idx).reshape(dest.shape + (HALF,))


def kernel(x, c, positions, w_ada, b_ada, g_pre_mix, g_post_mix, g_pre_ffn, g_post_ffn, w_in, conv_w, conv_b, b_gates, g_mlstm, w_branch_a, w_branch_b, w_out, router_w, router_bias, w_exp_gate, w_exp_up, w_exp_down, w_sh_gate, w_sh_up, w_sh_down):
    depth = w_ada.shape[0]
    for l in range(depth):
        x = _layer(x, c, positions, w_ada[l], b_ada[l], g_pre_mix[l], g_post_mix[l], g_pre_ffn[l],
                   g_post_ffn[l], w_in[l], conv_w[l], conv_b[l], b_gates[l], g_mlstm[l],
                   w_branch_a[l], w_branch_b[l], w_out[l], router_w[l], router_bias[l],
                   w_exp_gate[l], w_exp_up[l], w_exp_down[l], w_sh_gate[l], w_sh_up[l], w_sh_down[l])
    return x
```

```python
import functools

import jax
import jax.numpy as jnp
from jax import lax
from jax.experimental import pallas as pl
from jax.experimental.pallas import tpu as pltpu
from jax.experimental.pallas import tpu_sc as plsc

F32 = jnp.float32
BF16 = jnp.bfloat16
HIGHEST = lax.Precision.HIGHEST
LANES = 128

D_MODEL = 1024
ATT_GROUPS = ((128, 1), (512, 4), (2048, 16))
ATT_HEAD_DIM = 64
ATT_GROUP_W = 256
ATT_BLK = 128
ATT_PAIR = 2
ROPE_THETA = 500000.0
ROPE_HALF = 8
MLSTM_HEADS = 4
MLSTM_QK_DIM = 128
MLSTM_V_DIM = 256
MLSTM_BLOCK = 128
MLSTM_GROUP = 8
CONV_WIDTH = 4
N_EXPERTS = 256
TOP_K = 8
N_GROUPS = 8
TOPK_GROUPS = 4
EXPERT_FF = 256
ROUTED_SCALE = 2.5
NORM_EPS = 1e-6
NEG = -1e30

OFF_MV, OFF_MO, OFF_GA, OFF_GB = 0, 1024, 2048, 3072
OFF_MQ, OFF_MK = 4096, 4608
OFF_AQ, OFF_AK, OFF_AV = 5120, 5888, 6656
PROJ_W = 7424
HALF = D_MODEL // 2

EXPERT_BLOCK = 512
EXPERT_SLOTS = 6
MOE_PARTS = 2
MERGE_SPLIT = 2
VMEM_LIMIT = 56 * 1024 * 1024


def _nt(a, b):
    return lax.dot_general(a, b, (((1,), (1,)), ((), ())), preferred_element_type=F32)


def _tn(a, b):
    return lax.dot_general(a, b, (((0,), (0,)), ((), ())), preferred_element_type=F32)


_sigmoid = jax.nn.sigmoid


def _silu(x):
    return x * _sigmoid(x)


def _pack_pair(lo, hi):
    lo_b = pltpu.bitcast(lo.astype(BF16).astype(F32), jnp.uint32)
    hi_b = pltpu.bitcast(hi.astype(BF16).astype(F32), jnp.uint32)
    return (lo_b >> 16) | (hi_b & jnp.uint32(0xFFFF0000))


def _unpack_pair(w):
    lo = pltpu.bitcast(w << 16, F32)
    hi = pltpu.bitcast(w & jnp.uint32(0xFFFF0000), F32)
    return lo, hi


def _mod_kernel(c_ref, w_ref, b_ref, o_ref):
    a = _silu(c_ref[...])
    o_ref[...] = jnp.dot(a, w_ref[...], preferred_element_type=F32, precision=HIGHEST) + b_ref[...]


def _adaln(c, w_ada, b_ada):
    B = c.shape[0]
    n = w_ada.shape[1]
    tn = 512
    return pl.pallas_call(
        _mod_kernel,
        grid=(n // tn,),
        in_specs=[pl.BlockSpec((B, D_MODEL), lambda j: (0, 0)),
                  pl.BlockSpec((D_MODEL, tn), lambda j: (0, j)),
                  pl.BlockSpec((1, tn), lambda j: (0, j))],
        out_specs=pl.BlockSpec((B, tn), lambda j: (0, j)),
        out_shape=jax.ShapeDtypeStruct((B, n), F32),
        name="adaln_mod",
    )(c, w_ada, b_ada.reshape(1, n))


def _proj_kernel(x_ref, mod_ref, g_ref, w_ref, wif_ref, o_ref, gates_ref, h_ref):
    @pl.when(pl.program_id(1) == 0)
    def _():
        x = x_ref[...]
        ms = jnp.mean(x * x, axis=-1, keepdims=True)
        y = x * lax.rsqrt(ms + NORM_EPS) * g_ref[...]
        h = (y * (1.0 + mod_ref[0, 1:2, :]) + mod_ref[0, 0:1, :]).astype(BF16)
        h_ref[...] = h
        gates_ref[...] = _nt(wif_ref[...], h)

    o_ref[...] = jnp.dot(h_ref[...], w_ref[...], preferred_element_type=F32).astype(BF16)


def _in_proj(x2, mod3, g_pre, w_main, w_if, seq):
    T = x2.shape[0]
    tm, tn = 1024, PROJ_W // 2
    per_b = seq // tm
    return pl.pallas_call(
        _proj_kernel,
        grid=(T // tm, PROJ_W // tn),
        in_specs=[pl.BlockSpec((tm, D_MODEL), lambda i, j: (i, 0)),
                  pl.BlockSpec((1, 6, D_MODEL), lambda i, j: (i // per_b, 0, 0)),
                  pl.BlockSpec((1, D_MODEL), lambda i, j: (0, 0)),
                  pl.BlockSpec((D_MODEL, tn), lambda i, j: (0, j)),
                  pl.BlockSpec((2 * MLSTM_HEADS, D_MODEL), lambda i, j: (0, 0))],
        out_specs=[pl.BlockSpec((tm, tn), lambda i, j: (i, j)),
                   pl.BlockSpec((2 * MLSTM_HEADS, tm), lambda i, j: (0, i))],
        out_shape=[jax.ShapeDtypeStruct((T, PROJ_W), BF16),
                   jax.ShapeDtypeStruct((2 * MLSTM_HEADS, T), F32)],
        scratch_shapes=[pltpu.VMEM((tm, D_MODEL), BF16)],
        compiler_params=pltpu.CompilerParams(
            dimension_semantics=("arbitrary", "arbitrary"), vmem_limit_bytes=VMEM_LIMIT),
        name="norm_in_proj",
    )(x2, mod3, g_pre, w_main, w_if)


def _attn_kernel(q_ref, k_ref, v_ref, cs_ref, sn_ref, anchor_ref, o_ref, qf, kf, vf, acc, m_s, l_s,
                 *, seq):
    del anchor_ref
    g = pl.program_id(1)
    lane = lax.broadcasted_iota(jnp.int32, (ATT_BLK, LANES), 1)
    first = (lane % ATT_HEAD_DIM) < ROPE_HALF
    low_head = lane < ATT_HEAD_DIM

    def rope(x, cs, sn):
        partner = jnp.where(first, pltpu.roll(x, LANES - ROPE_HALF, 1), pltpu.roll(x, ROPE_HALF, 1))
        return x * cs + partner * sn

    def zero_pad(i, _):
        rows = pl.ds(pl.multiple_of(i * ATT_BLK, ATT_BLK), ATT_BLK)
        for hp in range(2):
            kf[hp, rows, :] = jnp.zeros((ATT_BLK, LANES), F32)
            vf[hp, rows, :] = jnp.zeros((ATT_BLK, LANES), F32)
        return 0

    lax.fori_loop(0, seq // ATT_BLK, zero_pad, 0)

    def stage(i, _):
        r = pl.multiple_of(i * ATT_BLK, ATT_BLK)
        rows = pl.ds(r, ATT_BLK)
        prow = pl.ds(pl.multiple_of(seq + i * ATT_BLK, ATT_BLK), ATT_BLK)
        cs = cs_ref[0, rows, :]
        sn = sn_ref[0, rows, :]
        for hp in range(2):
            cols = pl.ds(hp * LANES, LANES)
            qf[hp, rows, :] = rope(q_ref[0, rows, cols].astype(F32), cs, sn) * (ATT_HEAD_DIM ** -0.5)
            kf[hp, prow, :] = rope(k_ref[0, rows, cols].astype(F32), cs, sn)
            vf[hp, prow, :] = v_ref[0, rows, cols].astype(F32)
        return 0

    lax.fori_loop(0, seq // ATT_BLK, stage, 0)

    qi = lax.broadcasted_iota(jnp.int32, (ATT_BLK, 2 * ATT_BLK), 0)
    ki = lax.broadcasted_iota(jnp.int32, (ATT_BLK, 2 * ATT_BLK), 1)
    band = (ki >= qi) & (ki <= qi + ATT_BLK)

    def process(d, init):
        span = ATT_BLK * d
        single = seq == span

        def body(cp, _):
            blocks = [cp * ATT_PAIR + i for i in range(ATT_PAIR)]
            qrows, krows, valid = [], [], []
            for c in blocks:
                rho = c % d
                n = c // d
                qstart = rho + n * span
                if single:
                    kstart, nk = seq + qstart, ATT_BLK
                    valid.append(band[:, ATT_BLK:])
                else:
                    kstart, nk = seq + qstart - span, 2 * ATT_BLK
                    valid.append(band & (ki >= jnp.where(n > 0, 0, ATT_BLK)))
                qrows.append(pl.ds(qstart, ATT_BLK, stride=d) if d > 1 else pl.ds(qstart, ATT_BLK))
                krows.append(pl.ds(kstart, nk, stride=d) if d > 1 else pl.ds(kstart, nk))
            units = [(b, hp) for b in range(ATT_PAIR) for hp in range(2)]
            heads = [(u, hh) for u in range(len(units)) for hh in range(2)]
            q2 = [qf[hp, qrows[b], :] for b, hp in units]
            k2 = [kf[hp, krows[b], :].astype(BF16) for b, hp in units]
            v2 = [vf[hp, krows[b], :].astype(BF16) for b, hp in units]
            qh = [jnp.where(low_head if hh == 0 else jnp.logical_not(low_head), q2[u], 0.0).astype(BF16)
                  for u, hh in heads]
            s = [jnp.where(valid[units[u][0]], _nt(qh[i], k2[u]), NEG) for i, (u, hh) in enumerate(heads)]
            m = [jnp.max(x, axis=1, keepdims=True) for x in s]
            p = [jnp.exp(x - mx) for x, mx in zip(s, m)]
            l = [jnp.sum(x, axis=1, keepdims=True) for x in p]
            o = [jnp.dot(p[i].astype(BF16), v2[u], preferred_element_type=F32)
                 for i, (u, hh) in enumerate(heads)]
            for u, (b, hp) in enumerate(units):
                o_b = jnp.where(low_head, o[2 * u], o[2 * u + 1])
                m_b = jnp.where(low_head, m[2 * u], m[2 * u + 1])
                l_b = jnp.where(low_head, l[2 * u], l[2 * u + 1])
                if init:
                    acc[hp, qrows[b], :] = o_b
                    m_s[hp, qrows[b], :] = m_b
                    l_s[hp, qrows[b], :] = l_b
                else:
                    m_old = m_s[hp, qrows[b], :]
                    m_new = jnp.maximum(m_old, m_b)
                    a_old = jnp.exp(m_old - m_new)
                    a_new = jnp.exp(m_b - m_new)
                    acc[hp, qrows[b], :] = acc[hp, qrows[b], :] * a_old + o_b * a_new
                    l_s[hp, qrows[b], :] = l_s[hp, qrows[b], :] * a_old + l_b * a_new
                    m_s[hp, qrows[b], :] = m_new
            return 0

        lax.fori_loop(0, seq // (ATT_BLK * ATT_PAIR), body, 0)

    for gi, (_, d) in enumerate(ATT_GROUPS):
        @pl.when(g == gi)
        def _(d=d, gi=gi):
            process(d, gi == 0)

    @pl.when(g == len(ATT_GROUPS) - 1)
    def _():
        def fin(i, _):
            rows = pl.ds(pl.multiple_of(i * ATT_BLK, ATT_BLK), ATT_BLK)
            for hp in range(2):
                o_ref[0, rows, pl.ds(hp * LANES, LANES)] = (acc[hp, rows, :] / l_s[hp, rows, :]).astype(BF16)
            return 0

        lax.fori_loop(0, seq // ATT_BLK, fin, 0)


def _attention(proj3, cs, sn, anchor):
    B, S, _ = proj3.shape
    ng = len(ATT_GROUPS)
    qb, kb, vb = OFF_AQ // ATT_GROUP_W, OFF_AK // ATT_GROUP_W, OFF_AV // ATT_GROUP_W
    return pl.pallas_call(
        functools.partial(_attn_kernel, seq=S),
        grid=(B, ng),
        in_specs=[pl.BlockSpec((1, S, ATT_GROUP_W), lambda b, g: (b, 0, qb + g)),
                  pl.BlockSpec((1, S, ATT_GROUP_W), lambda b, g: (b, 0, kb + g)),
                  pl.BlockSpec((1, S, ATT_GROUP_W), lambda b, g: (b, 0, vb + g)),
                  pl.BlockSpec((1, S, LANES), lambda b, g: (b, 0, 0)),
                  pl.BlockSpec((1, S, LANES), lambda b, g: (b, 0, 0)),
                  pl.BlockSpec(memory_space=pl.ANY)],
        out_specs=pl.BlockSpec((1, S, ATT_GROUP_W), lambda b, g: (b, 0, 0)),
        out_shape=jax.ShapeDtypeStruct((B, S, ATT_GROUP_W), BF16),
        scratch_shapes=[pltpu.VMEM((2, S, LANES), F32),
                        pltpu.VMEM((2, 2 * S, LANES), F32),
                        pltpu.VMEM((2, 2 * S, LANES), F32),
                        pltpu.VMEM((2, S, LANES), F32),
                        pltpu.VMEM((2, S, LANES), F32),
                        pltpu.VMEM((2, S, LANES), F32)],
        compiler_params=pltpu.CompilerParams(
            dimension_semantics=("arbitrary", "arbitrary"), vmem_limit_bytes=VMEM_LIMIT),
        name="dilated_attention",
    )(proj3, proj3, proj3, cs, sn, anchor)


def _log_sigmoid(x):
    return jnp.minimum(x, 0.0) - jnp.log(1.0 + jnp.exp(-jnp.abs(x)))


def _mlstm_kernel(mq_ref, mk_ref, mv_ref, mo_ref, gt_ref, cwq_ref, cwk_ref, cbq_ref, cbk_ref,
                  bg_ref, gm_ref, anchor_ref, o_ref, q_s, k_s, va_s, rows_s, acc_s, kv_s, inter_s,
                  emt_s, c_s, *, seq):
    del anchor_ref
    h = pl.program_id(1)
    L = MLSTM_BLOCK
    NC = seq // L
    DK, DV = MLSTM_QK_DIM, MLSTM_V_DIM
    DA = DV + LANES
    nshift = CONV_WIDTH - 1

    tt = lax.broadcasted_iota(jnp.int32, (nshift * L, 2 * L), 0)
    uu = lax.broadcasted_iota(jnp.int32, (nshift * L, 2 * L), 1)
    shift_mat = (uu == L + tt % L - (tt // L + 1)).astype(BF16)
    conv_w = jnp.concatenate([cwq_ref[...], cwk_ref[...]], axis=1)
    conv_b = jnp.concatenate([cbq_ref[...], cbk_ref[...]], axis=1)
    prev = jnp.zeros((L, 2 * DK), BF16)
    for i in range(NC):
        blk = slice(i * L, (i + 1) * L)
        va_s[blk, 0:DV] = mv_ref[0, blk, :]
        va_s[blk, DV:DA] = jnp.ones((L, DA - DV), BF16)
        cur = jnp.concatenate([mq_ref[0, blk, :], mk_ref[0, blk, :]], axis=1)
        shifted = jnp.dot(shift_mat, jnp.concatenate([prev, cur], axis=0),
                          preferred_element_type=F32)
        y = conv_b + cur.astype(F32) * conv_w[nshift:nshift + 1, :]
        for s in range(nshift):
            y = y + shifted[s * L:(s + 1) * L, :] * conv_w[nshift - 1 - s:nshift - s, :]
        y = _silu(y)
        q_s[blk, :] = y[:, 0:DK].astype(BF16)
        k_s[blk, :] = (y[:, DK:2 * DK] * (DK ** -0.5)).astype(BF16)
        prev = cur

    lane = lax.broadcasted_iota(jnp.int32, (1, LANES), 1)
    bias = bg_ref[...]
    b_i = jnp.sum(jnp.where(lane == h, bias, 0.0), axis=1, keepdims=True)
    b_f = jnp.sum(jnp.where(lane == h + MLSTM_HEADS, bias, 0.0), axis=1, keepdims=True)
    ri = lax.broadcasted_iota(jnp.int32, (L, L), 0)
    ci = lax.broadcasted_iota(jnp.int32, (L, L), 1)
    causal = ci <= ri
    eye = (ri == ci).astype(F32)
    i_rows = gt_ref[h, 0] + b_i
    lf_rows = _log_sigmoid(gt_ref[h + MLSTM_HEADS, 0] + b_f)
    b_rows = jnp.dot(lf_rows, (ri <= ci).astype(F32), preferred_element_type=F32,
                     precision=HIGHEST)
    b_end = b_rows[:, L - 1:L]
    g_rows = b_end - b_rows + i_rows
    g_max = jnp.max(g_rows, axis=1, keepdims=True)
    m = jnp.zeros((1, 1), F32)
    m_prev, m_new = [], []
    for c in range(NC):
        m_prev.append(m)
        m = jnp.maximum(b_end[c:c + 1, :] + m, g_max[c:c + 1, :])
        m_new.append(m)
    m_prev = jnp.concatenate(m_prev, axis=0)
    m_new = jnp.concatenate(m_new, axis=0)
    rows_s[0] = b_rows
    rows_s[1] = jnp.exp(g_rows - m_new)
    rows_s[2] = b_rows - i_rows
    rows_s[3] = jnp.broadcast_to(m_prev, (NC, L))
    rows_s[4] = jnp.broadcast_to(jnp.exp(b_end + m_prev - m_new), (NC, L))

    r2 = lax.broadcasted_iota(jnp.int32, (2 * L, 2 * L), 0)
    c2 = lax.broadcasted_iota(jnp.int32, (2 * L, 2 * L), 1)
    ones_blk = ((r2 < L) == (c2 < L)).astype(BF16)

    G = MLSTM_GROUP

    def local(cg, _):
        cs = [cg * G + i for i in range(G)]
        rows = [pl.ds(pl.multiple_of(c * L, L), L) for c in cs]
        b_r = [rows_s[0, pl.ds(c, 1), :] for c in cs]
        w_r = [rows_s[1, pl.ds(c, 1), :] for c in cs]
        u_r = [rows_s[2, pl.ds(c, 1), :] for c in cs]
        mp = [rows_s[3, pl.ds(c, 1), :] for c in cs]
        q = [q_s[r, :] for r in rows]
        k = [k_s[r, :] for r in rows]
        va = [va_s[r, :] for r in rows]
        qk = [_nt(a, b) for a, b in zip(q, k)]
        x2 = [jnp.concatenate([eye * a, eye * b], axis=1) for a, b in zip(b_r, w_r)]
        hi = [x.astype(BF16) for x in x2]
        lo = [(x - h_.astype(F32)).astype(BF16) for x, h_ in zip(x2, hi)]
        yb = [jnp.dot(h_, ones_blk, preferred_element_type=F32)
              + jnp.dot(l_, ones_blk, preferred_element_type=F32) for h_, l_ in zip(hi, lo)]
        b_b = [y[:, 0:L] for y in yb]
        w_b = [y[:, L:2 * L] for y in yb]
        for i in range(G):
            kv_s[cs[i]] = _tn((w_b[i] * k[i].astype(F32)).astype(BF16), va[i])
        dmat = [jnp.where(causal, b - u, NEG) for b, u in zip(b_b, u_r)]
        m_t = [jnp.maximum(b + m_, jnp.max(d, axis=1, keepdims=True))
               for b, m_, d in zip(b_b, mp, dmat)]
        sc = [a * jnp.exp(d - m_) for a, d, m_ in zip(qk, dmat, m_t)]
        for i in range(G):
            acc_s[rows[i], :] = jnp.dot(sc[i].astype(BF16), va[i], preferred_element_type=F32)
            inter_s[rows[i], :] = jnp.exp(b_b[i] + mp[i] - m_t[i])
            emt_s[rows[i], :] = jnp.exp(-m_t[i])
        return 0

    lax.fori_loop(0, NC // G, local, 0)

    g_row = gm_ref[...]
    c_s[...] = jnp.zeros((DK, DA), F32)

    def recur(cg, _):
        cs = [cg * G + i for i in range(G)]
        rows = [pl.ds(pl.multiple_of(c * L, L), L) for c in cs]
        states = [c_s[...]]
        for c in cs:
            dec = rows_s[4, pl.ds(c, 1), :]
            states.append(jnp.concatenate([dec, dec, dec], axis=1) * states[-1] + kv_s[c])
        c_s[...] = states[G]
        read = [jnp.dot(q_s[r, :], st.astype(BF16), preferred_element_type=F32)
                for r, st in zip(rows, states)]
        inter = [inter_s[r, :] for r in rows]
        out = [acc_s[r, :] + jnp.concatenate([it, it, it], axis=1) * rd
               for r, it, rd in zip(rows, inter, read)]
        emt = [emt_s[r, :] for r in rows]
        nrm = [jnp.maximum(jnp.abs(jnp.concatenate([o[:, DV:DA], o[:, DV:DA]], axis=1)),
                           jnp.concatenate([e_, e_], axis=1)) for o, e_ in zip(out, emt)]
        hh = [o[:, 0:DV] / n_ for o, n_ in zip(out, nrm)]
        ms = [jnp.mean(x * x, axis=1, keepdims=True) for x in hh]
        hn = [x * lax.rsqrt(m_ + NORM_EPS) * g_row for x, m_ in zip(hh, ms)]
        for i in range(G):
            o_ref[0, rows[i], :] = (hn[i] * _sigmoid(mo_ref[0, rows[i], :].astype(F32))).astype(BF16)
        return 0

    lax.fori_loop(0, NC // G, recur, 0)


def _mlstm(proj3, gates_t, conv_w, conv_b, bg_row, g_mlstm, anchor):
    B, S, _ = proj3.shape
    H, DK, DV = MLSTM_HEADS, MLSTM_QK_DIM, MLSTM_V_DIM
    L = MLSTM_BLOCK
    NC = S // L
    DA = DV + LANES
    qb, kb = OFF_MQ // DK, OFF_MK // DK
    vb, ob = OFF_MV // DV, OFF_MO // DV
    nq = H
    return pl.pallas_call(
        functools.partial(_mlstm_kernel, seq=S),
        grid=(B, H),
        in_specs=[pl.BlockSpec((1, S, DK), lambda b, h: (b, 0, qb + h)),
                  pl.BlockSpec((1, S, DK), lambda b, h: (b, 0, kb + h)),
                  pl.BlockSpec((1, S, DV), lambda b, h: (b, 0, vb + h)),
                  pl.BlockSpec((1, S, DV), lambda b, h: (b, 0, ob + h)),
                  pl.BlockSpec((2 * H, 1, NC, L), lambda b, h: (0, b, 0, 0)),
                  pl.BlockSpec((CONV_WIDTH, DK), lambda b, h: (0, h)),
                  pl.BlockSpec((CONV_WIDTH, DK), lambda b, h: (0, nq + h)),
                  pl.BlockSpec((1, DK), lambda b, h: (0, h)),
                  pl.BlockSpec((1, DK), lambda b, h: (0, nq + h)),
                  pl.BlockSpec((1, LANES), lambda b, h: (0, 0)),
                  pl.BlockSpec((1, DV), lambda b, h: (0, h)),
                  pl.BlockSpec(memory_space=pl.ANY)],
        out_specs=pl.BlockSpec((1, S, DV), lambda b, h: (b, 0, h)),
        out_shape=jax.ShapeDtypeStruct((B, S, H * DV), BF16),
        scratch_shapes=[pltpu.VMEM((S, DK), BF16),
                        pltpu.VMEM((S, DK), BF16),
                        pltpu.VMEM((S, DA), BF16),
                        pltpu.VMEM((5, NC, L), F32),
                        pltpu.VMEM((S, DA), F32),
                        pltpu.VMEM((NC, DK, DA), F32),
                        pltpu.VMEM((S, L), F32),
                        pltpu.VMEM((S, L), F32),
                        pltpu.VMEM((DK, DA), F32)],
        compiler_params=pltpu.CompilerParams(
            dimension_semantics=("arbitrary", "arbitrary"), vmem_limit_bytes=VMEM_LIMIT),
        name="mlstm_chunkwise",
    )(proj3, proj3, proj3, proj3, gates_t, conv_w, conv_w, conv_b, conv_b, bg_row, g_mlstm, anchor)


def _rms(y, g):
    ms = jnp.mean(y * y, axis=-1, keepdims=True)
    return y * lax.rsqrt(ms + NORM_EPS) * g


def _merge_kernel(ya_ref, yb_ref, ga_ref, gb_ref, x_ref, mod_ref, wa_ref, wb_ref, wo_ref,
                  gpost_ref, gpre_ref, anchor_ref, x1_ref, h2_ref):
    del anchor_ref
    tm = x_ref.shape[0]
    slabs = [pl.ds(s * (tm // MERGE_SPLIT), tm // MERGE_SPLIT) for s in range(MERGE_SPLIT)]
    pa = [jnp.dot(ya_ref[r, :], wa_ref[...], preferred_element_type=F32) for r in slabs]
    pb = [jnp.dot(yb_ref[r, :], wb_ref[...], preferred_element_type=F32) for r in slabs]
    merged = [_sigmoid(ga_ref[r, :].astype(F32)) * a + _sigmoid(gb_ref[r, :].astype(F32)) * b
              for r, a, b in zip(slabs, pa, pb)]
    y = [jnp.dot(m.astype(BF16), wo_ref[...], preferred_element_type=F32) for m in merged]
    x1 = [x_ref[r, :] + mod_ref[0, 2:3, :] * _rms(v, gpost_ref[...]) for r, v in zip(slabs, y)]
    for r, v in zip(slabs, x1):
        x1_ref[r, :] = v
    h2 = [_rms(v, gpre_ref[...]) * (1.0 + mod_ref[0, 4:5, :]) + mod_ref[0, 3:4, :] for v in x1]
    for r, v in zip(slabs, h2):
        h2_ref[r, :] = _pack_pair(v[:, :HALF], v[:, HALF:])


def _merge(ya2, yb2, proj2, x2, mod3, wa, wb, wo, g_post, g_pre, seq, anchor):
    T = x2.shape[0]
    tm = 512 * MERGE_SPLIT
    per_b = seq // tm
    full = lambda shape: pl.BlockSpec(shape, lambda i: (0,) * len(shape))
    return pl.pallas_call(
        _merge_kernel,
        grid=(T // tm,),
        in_specs=[pl.BlockSpec((tm, ATT_GROUP_W), lambda i: (i, 0)),
                  pl.BlockSpec((tm, D_MODEL), lambda i: (i, 0)),
                  pl.BlockSpec((tm, D_MODEL), lambda i: (i, OFF_GA // D_MODEL)),
                  pl.BlockSpec((tm, D_MODEL), lambda i: (i, OFF_GB // D_MODEL)),
                  pl.BlockSpec((tm, D_MODEL), lambda i: (i, 0)),
                  pl.BlockSpec((1, 6, D_MODEL), lambda i: (i // per_b, 0, 0)),
                  full((ATT_GROUP_W, D_MODEL)), full((D_MODEL, D_MODEL)), full((D_MODEL, D_MODEL)),
                  full((1, D_MODEL)), full((1, D_MODEL)),
                  pl.BlockSpec(memory_space=pl.ANY)],
        out_specs=[pl.BlockSpec((tm, D_MODEL), lambda i: (i, 0)),
                   pl.BlockSpec((tm, HALF), lambda i: (i, 0))],
        out_shape=[jax.ShapeDtypeStruct((T, D_MODEL), F32),
                   jax.ShapeDtypeStruct((T, HALF), jnp.uint32)],
        compiler_params=pltpu.CompilerParams(
            dimension_semantics=("arbitrary",), vmem_limit_bytes=VMEM_LIMIT),
        name="merge_out_proj",
    )(ya2, yb2, proj2, proj2, x2, mod3, wa, wb, wo, g_post, g_pre, anchor)


def _router_kernel(h2_ref, rlo_ref, rhi_ref, bias_ref, idx_ref, w_ref, rank_ref, cnt_ref):
    E = N_EXPERTS
    tr = h2_ref.shape[0]
    gsz = E // N_GROUPS

    @pl.when(pl.program_id(0) == 0)
    def _():
        cnt_ref[...] = jnp.zeros(cnt_ref.shape, F32)

    lo, hi = _unpack_pair(h2_ref[...])
    logits = _nt(rlo_ref[...], lo.astype(BF16)) + _nt(rhi_ref[...], hi.astype(BF16))
    scores = _sigmoid(logits)
    sel = scores + bias_ref[:, 0:1]

    gi = lax.broadcasted_iota(jnp.int32, (gsz, tr), 0).astype(F32)
    gs_rows = []
    for g in range(N_GROUPS):
        blk = sel[g * gsz:(g + 1) * gsz, :]
        m1 = jnp.max(blk, axis=0, keepdims=True)
        a1 = jnp.min(jnp.where(blk == m1, gi, float(E)), axis=0, keepdims=True)
        m2 = jnp.max(jnp.where(gi == a1, -jnp.inf, blk), axis=0, keepdims=True)
        gs_rows.append(m1 + m2)
    gs = jnp.concatenate(gs_rows, axis=0)
    g8 = lax.broadcasted_iota(jnp.int32, (N_GROUPS, tr), 0).astype(F32)
    gmask = jnp.zeros((N_GROUPS, tr), F32)
    for _ in range(TOPK_GROUPS):
        m = jnp.max(gs, axis=0, keepdims=True)
        a = jnp.min(jnp.where(gs == m, g8, float(E)), axis=0, keepdims=True)
        hit = g8 == a
        gmask = jnp.where(hit, 1.0, gmask)
        gs = jnp.where(hit, -jnp.inf, gs)
    selm = jnp.concatenate(
        [jnp.where(gmask[g:g + 1, :] > 0.0, sel[g * gsz:(g + 1) * gsz, :], -jnp.inf)
         for g in range(N_GROUPS)], axis=0)

    ei = lax.broadcasted_iota(jnp.int32, (E, tr), 0).astype(F32)
    picks, weights, hits = [], [], []
    candidates = selm
    for _ in range(TOP_K):
        m = jnp.max(selm, axis=0, keepdims=True)
        a = jnp.min(jnp.where(selm == m, ei, float(E)), axis=0, keepdims=True)
        hit = ei == a
        picks.append(a)
        hits.append(hit)
        weights.append(jnp.sum(jnp.where(hit, scores, 0.0), axis=0, keepdims=True))
        selm = jnp.where(hit, -jnp.inf, selm)
    chosen = jnp.where(selm != candidates, 1.0, 0.0)
    wsum = weights[0]
    for w in weights[1:]:
        wsum = wsum + w

    ti = lax.broadcasted_iota(jnp.int32, (tr, tr), 0)
    tj = lax.broadcasted_iota(jnp.int32, (tr, tr), 1)
    before = (ti < tj).astype(BF16)
    pos = jnp.dot(chosen.astype(BF16), before, preferred_element_type=F32) + cnt_ref[:, 0:1]
    ranks = [jnp.sum(jnp.where(hit, pos, 0.0), axis=0, keepdims=True) for hit in hits]
    cnt_ref[...] = cnt_ref[...] + jnp.sum(chosen, axis=1, keepdims=True)

    idx_ref[...] = jnp.concatenate(picks, axis=0).astype(jnp.int32)
    w_ref[...] = jnp.concatenate([w / wsum * ROUTED_SCALE for w in weights], axis=0)
    rank_ref[...] = jnp.concatenate(ranks, axis=0).astype(jnp.int32)


def _router(h2p, r_lo, r_hi, bias_col, row0, T):
    tr = 512
    off = row0 // tr
    full = lambda shape: pl.BlockSpec(shape, lambda i: (0,) * len(shape))
    return pl.pallas_call(
        _router_kernel,
        grid=(T // tr,),
        in_specs=[pl.BlockSpec((tr, HALF), lambda i: (i + off, 0)),
                  full((N_EXPERTS, HALF)), full((N_EXPERTS, HALF)), full((N_EXPERTS, LANES))],
        out_specs=[pl.BlockSpec((TOP_K, tr), lambda i: (0, i)),
                   pl.BlockSpec((TOP_K, tr), lambda i: (0, i)),
                   pl.BlockSpec((TOP_K, tr), lambda i: (0, i)),
                   full((N_EXPERTS, LANES))],
        out_shape=[jax.ShapeDtypeStruct((TOP_K, T), jnp.int32),
                   jax.ShapeDtypeStruct((TOP_K, T), F32),
                   jax.ShapeDtypeStruct((TOP_K, T), jnp.int32),
                   jax.ShapeDtypeStruct((N_EXPERTS, LANES), F32)],
        compiler_params=pltpu.CompilerParams(
            dimension_semantics=("arbitrary",), vmem_limit_bytes=VMEM_LIMIT),
        name="router_topk",
    )(h2p, r_lo, r_hi, bias_col)


def _dest_kernel(idx_ref, rank_ref, pstart_ref, dest_ref):
    tr = idx_ref.shape[1]
    ei = lax.broadcasted_iota(jnp.int32, (N_EXPERTS, tr), 0)
    start = pstart_ref[:, 0:1]
    rows = []
    for k in range(TOP_K):
        hit = ei == idx_ref[k:k + 1, :]
        rows.append(jnp.sum(jnp.where(hit, start, 0.0), axis=0, keepdims=True))
    dest_ref[...] = jnp.concatenate(rows, axis=0).astype(jnp.int32) + rank_ref[...]


def _slot_index(idx, rank, pstart_col):
    T = idx.shape[1]
    tr = 1024
    return pl.pallas_call(
        _dest_kernel,
        grid=(T // tr,),
        in_specs=[pl.BlockSpec((TOP_K, tr), lambda i: (0, i)),
                  pl.BlockSpec((TOP_K, tr), lambda i: (0, i)),
                  pl.BlockSpec((N_EXPERTS, LANES), lambda i: (0, 0))],
        out_specs=pl.BlockSpec((TOP_K, tr), lambda i: (0, i)),
        out_shape=jax.ShapeDtypeStruct((TOP_K, T), jnp.int32),
        name="slot_index",
    )(idx, rank, pstart_col)


def _ffn_kernel(first_ref, nblk_ref, nused_ref, xs_hbm, wg_ref, wu_ref, wd_ref, ys_hbm,
                xbuf, ybuf, in_sem, out_sem, wg_s, wu_s, wd_s):
    e = pl.program_id(0)
    bm = EXPERT_BLOCK
    ns = EXPERT_SLOTS
    nused = nused_ref[0]
    first = first_ref[e]
    n = nblk_ref[e]

    def in_copy(g):
        slot = g % ns
        return pltpu.make_async_copy(xs_hbm.at[pl.ds(g * bm, bm)], xbuf.at[slot], in_sem.at[slot])

    def out_copy(g):
        slot = g % ns
        return pltpu.make_async_copy(ybuf.at[slot], ys_hbm.at[pl.ds(g * bm, bm)], out_sem.at[slot])

    def fetch(g):
        @pl.when(g < nused)
        def _():
            in_copy(g).start()

    def release(g):
        @pl.when(g >= ns)
        def _():
            out_copy(g - ns).wait()

    def ffn(g):
        lo, hi = _unpack_pair(xbuf[g % ns])
        x = jnp.concatenate([lo.astype(BF16), hi.astype(BF16)], axis=1)
        gate = jnp.dot(x, wg_s[...], preferred_element_type=F32)
        up = jnp.dot(x, wu_s[...], preferred_element_type=F32)
        hid = (_silu(gate) * up).astype(BF16)
        return jnp.dot(hid, wd_s[...], preferred_element_type=F32)

    def pack(g, out):
        ybuf[g % ns] = _pack_pair(out[:, :HALF], out[:, HALF:])

    @pl.when(e == 0)
    def _():
        for q in range(ns - 1):
            fetch(q)

    @pl.when(n > 0)
    def _():
        for packed, dst in ((wg_ref, wg_s), (wu_ref, wu_s), (wd_ref, wd_s)):
            lo, hi = _unpack_pair(packed[0])
            half = dst.shape[0] // 2
            dst[0:half, :] = lo.astype(BF16)
            dst[half:, :] = hi.astype(BF16)

        def two_blocks(j, _):
            g = first + 2 * j
            in_copy(g).wait()
            in_copy(g + 1).wait()
            fetch(g + ns - 1)
            release(g)
            release(g + 1)
            out_a = ffn(g)
            out_b = ffn(g + 1)
            pack(g, out_a)
            pack(g + 1, out_b)
            out_copy(g).start()
            out_copy(g + 1).start()
            fetch(g + ns)
            return 0

        lax.fori_loop(0, n // 2, two_blocks, 0)

        @pl.when(n % 2 == 1)
        def _():
            g = first + n - 1
            in_copy(g).wait()
            fetch(g + ns - 1)
            release(g)
            pack(g, ffn(g))
            out_copy(g).start()

    @pl.when(e == pl.num_programs(0) - 1)
    def _():
        for q in range(ns, 0, -1):
            @pl.when(nused >= q)
            def _(q=q):
                out_copy(nused - q).wait()


def _expert_ffn(first_blk, nblk, nused, xs, w_gate, w_up, w_down):
    P = xs.shape[0]
    bm = EXPERT_BLOCK
    w_map = lambda e, *_: (e, 0, 0)
    grid_spec = pltpu.PrefetchScalarGridSpec(
        num_scalar_prefetch=3,
        grid=(w_gate.shape[0],),
        in_specs=[pl.BlockSpec(memory_space=pl.ANY),
                  pl.BlockSpec((1, D_MODEL // 2, EXPERT_FF), w_map),
                  pl.BlockSpec((1, D_MODEL // 2, EXPERT_FF), w_map),
                  pl.BlockSpec((1, EXPERT_FF // 2, D_MODEL), w_map)],
        out_specs=pl.BlockSpec(memory_space=pl.ANY),
        scratch_shapes=[pltpu.VMEM((EXPERT_SLOTS, bm, HALF), jnp.uint32),
                        pltpu.VMEM((EXPERT_SLOTS, bm, HALF), jnp.uint32),
                        pltpu.SemaphoreType.DMA((EXPERT_SLOTS,)),
                        pltpu.SemaphoreType.DMA((EXPERT_SLOTS,)),
                        pltpu.VMEM((D_MODEL, EXPERT_FF), BF16),
                        pltpu.VMEM((D_MODEL, EXPERT_FF), BF16),
                        pltpu.VMEM((EXPERT_FF, D_MODEL), BF16)],
    )
    return pl.pallas_call(
        _ffn_kernel,
        grid_spec=grid_spec,
        out_shape=jax.ShapeDtypeStruct((P, HALF), jnp.uint32),
        compiler_params=pltpu.CompilerParams(
            dimension_semantics=("arbitrary",), vmem_limit_bytes=VMEM_LIMIT),
        name="routed_experts",
    )(first_blk, nblk, nused, xs, w_gate, w_up, w_down)


def _final_kernel(yg_ref, w_ref, h2_ref, x1_ref, mod_ref, wsg_ref, wsu_ref, wsd_ref, gpost_ref, *rest):
    o_ref = rest[-1]
    lo, hi = _unpack_pair(h2_ref[...])
    h2 = jnp.concatenate([lo.astype(BF16), hi.astype(BF16)], axis=1)
    gate = jnp.dot(h2, wsg_ref[...], preferred_element_type=F32)
    up = jnp.dot(h2, wsu_ref[...], preferred_element_type=F32)
    shared = jnp.dot((_silu(gate) * up).astype(BF16), wsd_ref[...], preferred_element_type=F32)
    y_lo = shared[:, :HALF]
    y_hi = shared[:, HALF:]
    for k in range(TOP_K):
        r_lo, r_hi = _unpack_pair(yg_ref[k])
        wk = w_ref[:, k:k + 1]
        y_lo = y_lo + wk * r_lo
        y_hi = y_hi + wk * r_hi
    ms = (jnp.sum(y_lo * y_lo, axis=-1, keepdims=True)
          + jnp.sum(y_hi * y_hi, axis=-1, keepdims=True)) * (1.0 / D_MODEL)
    inv = lax.rsqrt(ms + NORM_EPS)
    o_ref[:, 0:HALF] = x1_ref[:, 0:HALF] + mod_ref[0, 5:6, 0:HALF] * (y_lo * inv * gpost_ref[:, 0:HALF])
    o_ref[:, HALF:] = x1_ref[:, HALF:] + mod_ref[0, 5:6, HALF:] * (y_hi * inv * gpost_ref[:, HALF:])


def _final(yg, w_tk, h2p, x1, mod3, wsg, wsu, wsd, g_post, seq, row0, out_prev):
    T = x1.shape[0]
    tp = yg.shape[1]
    tm = 512
    per_b = seq // tm
    off = row0 // tm
    full = lambda shape: pl.BlockSpec(shape, lambda i: (0,) * len(shape))
    in_specs = [pl.BlockSpec((TOP_K, tm, HALF), lambda i: (0, i, 0)),
                pl.BlockSpec((tm, TOP_K), lambda i: (i, 0)),
                pl.BlockSpec((tm, HALF), lambda i: (i + off, 0)),
                pl.BlockSpec((tm, D_MODEL), lambda i: (i + off, 0)),
                pl.BlockSpec((1, 6, D_MODEL), lambda i: ((i + off) // per_b, 0, 0)),
                full((D_MODEL, EXPERT_FF)), full((D_MODEL, EXPERT_FF)), full((EXPERT_FF, D_MODEL)),
                full((1, D_MODEL))]
    args = [yg, w_tk, h2p, x1, mod3, wsg, wsu, wsd, g_post]
    aliases = {}
    if out_prev is not None:
        in_specs.append(pl.BlockSpec(memory_space=pl.ANY))
        args.append(out_prev)
        aliases = {len(args) - 1: 0}
    return pl.pallas_call(
        _final_kernel,
        grid=(tp // tm,),
        in_specs=in_specs,
        out_specs=pl.BlockSpec((tm, D_MODEL), lambda i: (i + off, 0)),
        out_shape=jax.ShapeDtypeStruct((T, D_MODEL), F32),
        input_output_aliases=aliases,
        compiler_params=pltpu.CompilerParams(
            dimension_semantics=("arbitrary",), vmem_limit_bytes=VMEM_LIMIT),
        name="shared_expert_combine",
    )(*args)


def _rope_tables(positions):
    inv = jnp.power(ROPE_THETA, -jnp.arange(ROPE_HALF, dtype=F32) / ROPE_HALF)
    ang = positions.astype(F32)[..., None] * inv
    cos, sin = jnp.cos(ang), jnp.sin(ang)
    rest = ATT_HEAD_DIM - 2 * ROPE_HALF
    cs = jnp.concatenate([cos, cos, jnp.ones(ang.shape[:-1] + (rest,), F32)], axis=-1)
    sn = jnp.concatenate([-sin, sin, jnp.zeros(ang.shape[:-1] + (rest,), F32)], axis=-1)
    return jnp.tile(cs, (1, 1, 2)), jnp.tile(sn, (1, 1, 2))


def _layer(x, c, positions, w_ada, b_ada, g_pre_mix, g_post_mix, g_pre_ffn, g_post_ffn,
           w_in, conv_w, conv_b, b_gates, g_mlstm, w_branch_a, w_branch_b, w_out,
           router_w, router_bias, w_exp_gate, w_exp_up, w_exp_down, w_sh_gate, w_sh_up, w_sh_down):
    B, S, D = x.shape
    T = B * S
    H = MLSTM_HEADS
    x2 = x.reshape(T, D)

    mod3 = _adaln(c, w_ada, b_ada).reshape(B, 6, D)

    a_w = 3 * ATT_GROUP_W
    o_mq = 3 * a_w
    o_mk = o_mq + H * MLSTM_QK_DIM
    o_mv = o_mk + H * MLSTM_QK_DIM
    o_mo = o_mv + H * MLSTM_V_DIM
    o_mi = o_mo + H * MLSTM_V_DIM
    o_ga = o_mi + 2 * H
    o_gb = o_ga + D
    w_bf = w_in.astype(BF16)
    w_main = jnp.concatenate(
        [w_bf[:, o_mv:o_mi], w_bf[:, o_ga:o_gb + D], w_bf[:, o_mq:o_mv], w_bf[:, 0:o_mq]], axis=1)
    w_if = w_bf[:, o_mi:o_ga].T

    proj, gates = _in_proj(x2, mod3, g_pre_mix.reshape(1, D), w_main, w_if, S)
    proj3 = proj.reshape(B, S, PROJ_W)

    wg_p, wu_p, wd_p = (_pack_weight_rows(w) for w in (w_exp_gate, w_exp_up, w_exp_down))

    cs, sn = _rope_tables(positions)
    y_a = _attention(proj3, cs, sn, wg_p)

    bg_row = jnp.pad(b_gates.reshape(1, 2 * H), ((0, 0), (0, LANES - 2 * H)))
    gates_t = gates.reshape(2 * H, B, S // MLSTM_BLOCK, MLSTM_BLOCK)
    y_b = _mlstm(proj3, gates_t, conv_w, conv_b.reshape(1, -1), bg_row, g_mlstm.reshape(1, -1), wu_p)

    x1, h2p = _merge(y_a.reshape(T, ATT_GROUP_W), y_b.reshape(T, D), proj, x2, mod3,
                     w_branch_a.astype(BF16), w_branch_b.astype(BF16), w_out.astype(BF16),
                     g_post_mix.reshape(1, D), g_pre_ffn.reshape(1, D), S, wd_p)

    rw_t = router_w.T.astype(BF16)
    bias_col = jnp.broadcast_to(router_bias.reshape(N_EXPERTS, 1), (N_EXPERTS, LANES))
    wsg, wsu, wsd = w_sh_gate.astype(BF16), w_sh_up.astype(BF16), w_sh_down.astype(BF16)

    tp = T // MOE_PARTS
    bm = EXPERT_BLOCK
    nb = (tp * TOP_K) // bm + N_EXPERTS
    out = None
    for part in range(MOE_PARTS):
        row0 = part * tp
        idx, wts, rank, cnt = _router(h2p, rw_t[:, :HALF], rw_t[:, HALF:], bias_col, row0, tp)

        counts = cnt[:, 0].astype(jnp.int32)
        padded = (counts + bm - 1) // bm * bm
        pend = jnp.cumsum(padded)
        pstart = pend - padded
        pstart_col = jnp.broadcast_to(pstart.astype(F32).reshape(N_EXPERTS, 1), (N_EXPERTS, LANES))
        dest = _slot_index(idx, rank, pstart_col)
        nused = (pend[-1] // bm).astype(jnp.int32).reshape(1)

        xs = _dispatch(h2p, dest, nb * bm, row0)
        ys = _expert_ffn((pstart // bm).astype(jnp.int32), (padded // bm).astype(jnp.int32), nused,
                         xs, wg_p, wu_p, wd_p)
        yg = _collect(ys, dest)
        out = _final(yg, wts.T, h2p, x1, mod3, wsg, wsu, wsd, g_post_ffn.reshape(1, D), S, row0, out)
    return out.reshape(B, S, D)


SC_CORES = 2
SC_SUBCORES = 16
SC_WORKERS = SC_CORES * SC_SUBCORES
SC_ROWS = 64


def _sc_mesh():
    return plsc.VectorSubcoreMesh(core_axis_name="c", subcore_axis_name="s",
                                  num_cores=SC_CORES, num_subcores=SC_SUBCORES)


def _worker_id():
    return lax.axis_index("s") * SC_CORES + lax.axis_index("c")


def _dispatch(h2p, dest, n_slots, row0):
    T = dest.shape[1]
    per_w = T // SC_WORKERS
    nch = per_w // SC_ROWS
    idx = dest.reshape(TOP_K, SC_WORKERS, nch, SC_ROWS).transpose(1, 2, 0, 3)
    idx = idx.reshape(SC_WORKERS, nch * TOP_K, SC_ROWS)

    def body(x_hbm, idx_hbm, xs_hbm, idx_v, buf0, buf1, rsem0, rsem1, ssem0, ssem1):
        wid = _worker_id()
        base = row0 + wid * per_w
        pltpu.sync_copy(idx_hbm.at[wid], idx_v)
        bufs = ((buf0, rsem0, ssem0), (buf1, rsem1, ssem1))

        def read(c, buf, rsem):
            return pltpu.make_async_copy(x_hbm.at[pl.ds(base + c * SC_ROWS, SC_ROWS)], buf, rsem)

        def scatter(c, k, buf, ssem):
            return pltpu.make_async_copy(buf, xs_hbm.at[idx_v.at[c * TOP_K + k]], ssem)

        read(0, buf0, rsem0).start()

        @pl.loop(0, nch, step=2)
        def _(c0):
            for b in range(2):
                c = c0 + b
                buf, rsem, ssem = bufs[b]
                obuf, orsem, ossem = bufs[1 - b]
                read(c, buf, rsem).wait()

                @pl.when(c > 0)
                def _():
                    for k in range(TOP_K):
                        scatter(c - 1, k, obuf, ossem).wait()

                @pl.when(c + 1 < nch)
                def _():
                    read(c + 1, obuf, orsem).start()

                for k in range(TOP_K):
                    scatter(c, k, buf, ssem).start()

        for k in range(TOP_K):
            scatter(nch - 1, k, buf1, ssem1).wait()

    run = pl.kernel(
        body,
        out_type=jax.ShapeDtypeStruct((n_slots, HALF), jnp.uint32),
        mesh=_sc_mesh(),
        scratch_types=[pltpu.VMEM((nch * TOP_K, SC_ROWS), jnp.int32),
                       pltpu.VMEM((SC_ROWS, HALF), jnp.uint32),
                       pltpu.VMEM((SC_ROWS, HALF), jnp.uint32),
                       pltpu.SemaphoreType.DMA, pltpu.SemaphoreType.DMA,
                       pltpu.SemaphoreType.DMA, pltpu.SemaphoreType.DMA],
        name="sc_dispatch",
    )
    return run(h2p, idx)


SC_PACK_ROWS = 64
SC_PACK_COLS = 256
SC_LANES = 16


def _pack_weight_rows(w):
    E, R, C = w.shape
    hb = R // 2 // SC_PACK_ROWS
    w2 = w.reshape(E * R, C)

    def body(w_hbm, out_hbm):
        def block(lo_v, hi_v, out_v):
            @pl.loop(0, SC_PACK_ROWS)
            def _(r):
                @pl.loop(0, SC_PACK_COLS, step=SC_LANES)
                def _(c):
                    cols = pl.ds(c, SC_LANES)
                    pair = plsc.pack(lo_v[r, cols], hi_v[r, cols], format=plsc.PackFormat.INTERLEAVED)
                    out_v[r, cols] = plsc.bitcast(pair, jnp.uint32)

        blk = (SC_PACK_ROWS, SC_PACK_COLS)
        pltpu.emit_pipeline(
            block,
            grid=(E * hb, C // SC_PACK_COLS),
            in_specs=[pl.BlockSpec(blk, lambda i, j: ((i // hb) * 2 * hb + i % hb, j)),
                      pl.BlockSpec(blk, lambda i, j: ((i // hb) * 2 * hb + hb + i % hb, j))],
            out_specs=[pl.BlockSpec(blk, lambda i, j: (i, j))],
            core_axis_name=("c", "s"),
            dimension_semantics=(pltpu.PARALLEL, pltpu.PARALLEL),
        )(w_hbm, w_hbm, out_hbm)

    run = pl.kernel(body, out_type=jax.ShapeDtypeStruct((E * R // 2, C), jnp.uint32),
                    mesh=_sc_mesh(), scratch_types=[], name="sc_pack_weights",
                    compiler_params=pltpu.CompilerParams(needs_layout_passes=False))
    return run(w2).reshape(E, R // 2, C)


def _collect(ys, dest):
    n = dest.size
    per_w = n // SC_WORKERS
    nch = per_w // SC_ROWS
    idx = dest.reshape(SC_WORKERS, nch, SC_ROWS)

    def body(ys_hbm, idx_hbm, out_hbm, idx_v, buf0, buf1, gsem0, gsem1, wsem0, wsem1):
        wid = _worker_id()
        base = wid * per_w
        pltpu.sync_copy(idx_hbm.at[wid], idx_v)
        bufs = ((buf0, gsem0, wsem0), (buf1, gsem1, wsem1))

        def gather(c, buf, gsem):
            return pltpu.make_async_copy(ys_hbm.at[idx_v.at[c]], buf, gsem)

        def write(c, buf, wsem):
            return pltpu.make_async_copy(buf, out_hbm.at[pl.ds(base + c * SC_ROWS, SC_ROWS)], wsem)

        gather(0, buf0, gsem0).start()

        @pl.loop(0, nch, step=2)
        def _(c0):
            for b in range(2):
                c = c0 + b
                buf, gsem, wsem = bufs[b]
                obuf, ogsem, owsem = bufs[1 - b]
                gather(c, buf, gsem).wait()

                @pl.when(c > 0)
                def _():
                    write(c - 1, obuf, owsem).wait()

                @pl.when(c + 1 < nch)
                def _():
                    gather(c + 1, obuf, ogsem).start()

                write(c, buf, wsem).start()

        write(nch - 1, buf1, wsem1).wait()

    run = pl.kernel(
        body,
        out_type=jax.ShapeDtypeStruct((n, HALF), jnp.uint32),
        mesh=_sc_mesh(),
        scratch_types=[pltpu.VMEM((nch, SC_ROWS), jnp.int32),
                       pltpu.VMEM((SC_ROWS, HALF), jnp.uint32),
                       pltpu.VMEM((SC_ROWS, HALF), jnp.uint32),
                       pltpu.SemaphoreType.DMA, pltpu.SemaphoreType.DMA,
                       pltpu.SemaphoreType.DMA, pltpu.SemaphoreType.DMA],
        name="sc_collect",
    )
    return run(ys, idx).reshape(dest.shape + (HALF,))


def kernel(x, c, positions, w_ada, b_ada, g_pre_mix, g_post_mix, g_pre_ffn, g_post_ffn, w_in, conv_w, conv_b, b_gates, g_mlstm, w_branch_a, w_branch_b, w_out, router_w, router_bias, w_exp_gate, w_exp_up, w_exp_down, w_sh_gate, w_sh_up, w_sh_down):
    depth = w_ada.shape[0]
    for l in range(depth):
        x = _layer(x, c, positions, w_ada[l], b_ada[l], g_pre_mix[l], g_post_mix[l], g_pre_ffn[l],
                   g_post_ffn[l], w_in[l], conv_w[l], conv_b[l], b_gates[l], g_mlstm[l],
                   w_branch_a[l], w_branch_b[l], w_out[l], router_w[l], router_bias[l],
                   w_exp_gate[l], w_exp_up[l], w_exp_down[l], w_sh_gate[l], w_sh_up[l], w_sh_down[l])
    return x
```

```python
import functools

import jax
import jax.numpy as jnp
from jax import lax
from jax.experimental import pallas as pl
from jax.experimental.pallas import tpu as pltpu
from jax.experimental.pallas import tpu_sc as plsc

F32 = jnp.float32
BF16 = jnp.bfloat16
HIGHEST = lax.Precision.HIGHEST
LANES = 128

D_MODEL = 1024
ATT_GROUPS = ((128, 1), (512, 4), (2048, 16))
ATT_HEAD_DIM = 64
ATT_GROUP_W = 256
ATT_BLK = 128
ATT_PAIR = 2
ROPE_THETA = 500000.0
ROPE_HALF = 8
MLSTM_HEADS = 4
MLSTM_QK_DIM = 128
MLSTM_V_DIM = 256
MLSTM_BLOCK = 128
MLSTM_GROUP = 8
CONV_WIDTH = 4
N_EXPERTS = 256
TOP_K = 8
N_GROUPS = 8
TOPK_GROUPS = 4
EXPERT_FF = 256
ROUTED_SCALE = 2.5
NORM_EPS = 1e-6
NEG = -1e30

OFF_MV, OFF_MO, OFF_GA, OFF_GB = 0, 1024, 2048, 3072
OFF_MQ, OFF_MK = 4096, 4608
OFF_AQ, OFF_AK, OFF_AV = 5120, 5888, 6656
PROJ_W = 7424
HALF = D_MODEL // 2

EXPERT_BLOCK = 512
EXPERT_SLOTS = 6
MOE_PARTS = 2
MERGE_SPLIT = 2
VMEM_LIMIT = 56 * 1024 * 1024


def _nt(a, b):
    return lax.dot_general(a, b, (((1,), (1,)), ((), ())), preferred_element_type=F32)


def _tn(a, b):
    return lax.dot_general(a, b, (((0,), (0,)), ((), ())), preferred_element_type=F32)


_sigmoid = jax.nn.sigmoid


def _silu(x):
    return x * _sigmoid(x)


def _pack_pair(lo, hi):
    lo_b = pltpu.bitcast(lo.astype(BF16).astype(F32), jnp.uint32)
    hi_b = pltpu.bitcast(hi.astype(BF16).astype(F32), jnp.uint32)
    return (lo_b >> 16) | (hi_b & jnp.uint32(0xFFFF0000))


def _unpack_pair(w):
    lo = pltpu.bitcast(w << 16, F32)
    hi = pltpu.bitcast(w & jnp.uint32(0xFFFF0000), F32)
    return lo, hi


def _mod_kernel(c_ref, w_ref, b_ref, o_ref):
    a = _silu(c_ref[...])
    o_ref[...] = jnp.dot(a, w_ref[...], preferred_element_type=F32, precision=HIGHEST) + b_ref[...]


def _adaln(c, w_ada, b_ada):
    B = c.shape[0]
    n = w_ada.shape[1]
    tn = 512
    return pl.pallas_call(
        _mod_kernel,
        grid=(n // tn,),
        in_specs=[pl.BlockSpec((B, D_MODEL), lambda j: (0, 0)),
                  pl.BlockSpec((D_MODEL, tn), lambda j: (0, j)),
                  pl.BlockSpec((1, tn), lambda j: (0, j))],
        out_specs=pl.BlockSpec((B, tn), lambda j: (0, j)),
        out_shape=jax.ShapeDtypeStruct((B, n), F32),
        name="adaln_mod",
    )(c, w_ada, b_ada.reshape(1, n))


def _proj_kernel(x_ref, mod_ref, g_ref, w_ref, wif_ref, o_ref, gates_ref, h_ref):
    @pl.when(pl.program_id(1) == 0)
    def _():
        x = x_ref[...]
        ms = jnp.mean(x * x, axis=-1, keepdims=True)
        y = x * lax.rsqrt(ms + NORM_EPS) * g_ref[...]
        h = (y * (1.0 + mod_ref[0, 1:2, :]) + mod_ref[0, 0:1, :]).astype(BF16)
        h_ref[...] = h
        gates_ref[...] = _nt(wif_ref[...], h)

    o_ref[...] = jnp.dot(h_ref[...], w_ref[...], preferred_element_type=F32).astype(BF16)


def _in_proj(x2, mod3, g_pre, w_main, w_if, seq):
    T = x2.shape[0]
    tm, tn = 1024, PROJ_W // 2
    per_b = seq // tm
    return pl.pallas_call(
        _proj_kernel,
        grid=(T // tm, PROJ_W // tn),
        in_specs=[pl.BlockSpec((tm, D_MODEL), lambda i, j: (i, 0)),
                  pl.BlockSpec((1, 6, D_MODEL), lambda i, j: (i // per_b, 0, 0)),
                  pl.BlockSpec((1, D_MODEL), lambda i, j: (0, 0)),
                  pl.BlockSpec((D_MODEL, tn), lambda i, j: (0, j)),
                  pl.BlockSpec((2 * MLSTM_HEADS, D_MODEL), lambda i, j: (0, 0))],
        out_specs=[pl.BlockSpec((tm, tn), lambda i, j: (i, j)),
                   pl.BlockSpec((2 * MLSTM_HEADS, tm), lambda i, j: (0, i))],
        out_shape=[jax.ShapeDtypeStruct((T, PROJ_W), BF16),
                   jax.ShapeDtypeStruct((2 * MLSTM_HEADS, T), F32)],
        scratch_shapes=[pltpu.VMEM((tm, D_MODEL), BF16)],
        compiler_params=pltpu.CompilerParams(
            dimension_semantics=("arbitrary", "arbitrary"), vmem_limit_bytes=VMEM_LIMIT),
        name="norm_in_proj",
    )(x2, mod3, g_pre, w_main, w_if)


def _attn_kernel(q_ref, k_ref, v_ref, cs_ref, sn_ref, o_ref, qf, kf, vf, acc, m_s, l_s, *, seq):
    g = pl.program_id(1)
    lane = lax.broadcasted_iota(jnp.int32, (ATT_BLK, LANES), 1)
    first = (lane % ATT_HEAD_DIM) < ROPE_HALF
    low_head = lane < ATT_HEAD_DIM

    def rope(x, cs, sn):
        partner = jnp.where(first, pltpu.roll(x, LANES - ROPE_HALF, 1), pltpu.roll(x, ROPE_HALF, 1))
        return x * cs + partner * sn

    def zero_pad(i, _):
        rows = pl.ds(pl.multiple_of(i * ATT_BLK, ATT_BLK), ATT_BLK)
        for hp in range(2):
            kf[hp, rows, :] = jnp.zeros((ATT_BLK, LANES), F32)
            vf[hp, rows, :] = jnp.zeros((ATT_BLK, LANES), F32)
        return 0

    lax.fori_loop(0, seq // ATT_BLK, zero_pad, 0)

    def stage(i, _):
        r = pl.multiple_of(i * ATT_BLK, ATT_BLK)
        rows = pl.ds(r, ATT_BLK)
        prow = pl.ds(pl.multiple_of(seq + i * ATT_BLK, ATT_BLK), ATT_BLK)
        cs = cs_ref[0, rows, :]
        sn = sn_ref[0, rows, :]
        for hp in range(2):
            cols = pl.ds(hp * LANES, LANES)
            qf[hp, rows, :] = rope(q_ref[0, rows, cols].astype(F32), cs, sn) * (ATT_HEAD_DIM ** -0.5)
            kf[hp, prow, :] = rope(k_ref[0, rows, cols].astype(F32), cs, sn)
            vf[hp, prow, :] = v_ref[0, rows, cols].astype(F32)
        return 0

    lax.fori_loop(0, seq // ATT_BLK, stage, 0)

    qi = lax.broadcasted_iota(jnp.int32, (ATT_BLK, 2 * ATT_BLK), 0)
    ki = lax.broadcasted_iota(jnp.int32, (ATT_BLK, 2 * ATT_BLK), 1)
    band = (ki >= qi) & (ki <= qi + ATT_BLK)

    def process(d, init):
        span = ATT_BLK * d
        single = seq == span

        def body(cp, _):
            blocks = [cp * ATT_PAIR + i for i in range(ATT_PAIR)]
            qrows, krows, valid = [], [], []
            for c in blocks:
                rho = c % d
                n = c // d
                qstart = rho + n * span
                if single:
                    kstart, nk = seq + qstart, ATT_BLK
                    valid.append(band[:, ATT_BLK:])
                else:
                    kstart, nk = seq + qstart - span, 2 * ATT_BLK
                    valid.append(band & (ki >= jnp.where(n > 0, 0, ATT_BLK)))
                qrows.append(pl.ds(qstart, ATT_BLK, stride=d) if d > 1 else pl.ds(qstart, ATT_BLK))
                krows.append(pl.ds(kstart, nk, stride=d) if d > 1 else pl.ds(kstart, nk))
            units = [(b, hp) for b in range(ATT_PAIR) for hp in range(2)]
            heads = [(u, hh) for u in range(len(units)) for hh in range(2)]
            q2 = [qf[hp, qrows[b], :] for b, hp in units]
            k2 = [kf[hp, krows[b], :].astype(BF16) for b, hp in units]
            v2 = [vf[hp, krows[b], :].astype(BF16) for b, hp in units]
            qh = [jnp.where(low_head if hh == 0 else jnp.logical_not(low_head), q2[u], 0.0).astype(BF16)
                  for u, hh in heads]
            s = [jnp.where(valid[units[u][0]], _nt(qh[i], k2[u]), NEG) for i, (u, hh) in enumerate(heads)]
            m = [jnp.max(x, axis=1, keepdims=True) for x in s]
            p = [jnp.exp(x - mx) for x, mx in zip(s, m)]
            l = [jnp.sum(x, axis=1, keepdims=True) for x in p]
            o = [jnp.dot(p[i].astype(BF16), v2[u], preferred_element_type=F32)
                 for i, (u, hh) in enumerate(heads)]
            for u, (b, hp) in enumerate(units):
                o_b = jnp.where(low_head, o[2 * u], o[2 * u + 1])
                m_b = jnp.where(low_head, m[2 * u], m[2 * u + 1])
                l_b = jnp.where(low_head, l[2 * u], l[2 * u + 1])
                if init:
                    acc[hp, qrows[b], :] = o_b
                    m_s[hp, qrows[b], :] = m_b
                    l_s[hp, qrows[b], :] = l_b
                else:
                    m_old = m_s[hp, qrows[b], :]
                    m_new = jnp.maximum(m_old, m_b)
                    a_old = jnp.exp(m_old - m_new)
                    a_new = jnp.exp(m_b - m_new)
                    acc[hp, qrows[b], :] = acc[hp, qrows[b], :] * a_old + o_b * a_new
                    l_s[hp, qrows[b], :] = l_s[hp, qrows[b], :] * a_old + l_b * a_new
                    m_s[hp, qrows[b], :] = m_new
            return 0

        lax.fori_loop(0, seq // (ATT_BLK * ATT_PAIR), body, 0)

    for gi, (_, d) in enumerate(ATT_GROUPS):
        @pl.when(g == gi)
        def _(d=d, gi=gi):
            process(d, gi == 0)

    @pl.when(g == len(ATT_GROUPS) - 1)
    def _():
        def fin(i, _):
            rows = pl.ds(pl.multiple_of(i * ATT_BLK, ATT_BLK), ATT_BLK)
            for hp in range(2):
                o_ref[0, rows, pl.ds(hp * LANES, LANES)] = (acc[hp, rows, :] / l_s[hp, rows, :]).astype(BF16)
            return 0

        lax.fori_loop(0, seq // ATT_BLK, fin, 0)


def _attention(proj3, cs, sn):
    B, S, _ = proj3.shape
    ng = len(ATT_GROUPS)
    qb, kb, vb = OFF_AQ // ATT_GROUP_W, OFF_AK // ATT_GROUP_W, OFF_AV // ATT_GROUP_W
    return pl.pallas_call(
        functools.partial(_attn_kernel, seq=S),
        grid=(B, ng),
        in_specs=[pl.BlockSpec((1, S, ATT_GROUP_W), lambda b, g: (b, 0, qb + g)),
                  pl.BlockSpec((1, S, ATT_GROUP_W), lambda b, g: (b, 0, kb + g)),
                  pl.BlockSpec((1, S, ATT_GROUP_W), lambda b, g: (b, 0, vb + g)),
                  pl.BlockSpec((1, S, LANES), lambda b, g: (b, 0, 0)),
                  pl.BlockSpec((1, S, LANES), lambda b, g: (b, 0, 0))],
        out_specs=pl.BlockSpec((1, S, ATT_GROUP_W), lambda b, g: (b, 0, 0)),
        out_shape=jax.ShapeDtypeStruct((B, S, ATT_GROUP_W), BF16),
        scratch_shapes=[pltpu.VMEM((2, S, LANES), F32),
                        pltpu.VMEM((2, 2 * S, LANES), F32),
                        pltpu.VMEM((2, 2 * S, LANES), F32),
                        pltpu.VMEM((2, S, LANES), F32),
                        pltpu.VMEM((2, S, LANES), F32),
                        pltpu.VMEM((2, S, LANES), F32)],
        compiler_params=pltpu.CompilerParams(
            dimension_semantics=("arbitrary", "arbitrary"), vmem_limit_bytes=VMEM_LIMIT),
        name="dilated_attention",
    )(proj3, proj3, proj3, cs, sn)


def _log_sigmoid(x):
    return jnp.minimum(x, 0.0) - jnp.log(1.0 + jnp.exp(-jnp.abs(x)))


def _mlstm_kernel(mq_ref, mk_ref, mv_ref, mo_ref, gt_ref, cwq_ref, cwk_ref, cbq_ref, cbk_ref,
                  bg_ref, gm_ref, anchor_a, anchor_b, o_ref, q_s, k_s, va_s, rows_s, acc_s, kv_s,
                  inter_s, emt_s, c_s, *, seq):
    del anchor_a, anchor_b
    h = pl.program_id(1)
    L = MLSTM_BLOCK
    NC = seq // L
    DK, DV = MLSTM_QK_DIM, MLSTM_V_DIM
    DA = DV + LANES
    nshift = CONV_WIDTH - 1

    tt = lax.broadcasted_iota(jnp.int32, (nshift * L, 2 * L), 0)
    uu = lax.broadcasted_iota(jnp.int32, (nshift * L, 2 * L), 1)
    shift_mat = (uu == L + tt % L - (tt // L + 1)).astype(BF16)
    conv_w = jnp.concatenate([cwq_ref[...], cwk_ref[...]], axis=1)
    conv_b = jnp.concatenate([cbq_ref[...], cbk_ref[...]], axis=1)
    prev = jnp.zeros((L, 2 * DK), BF16)
    for i in range(NC):
        blk = slice(i * L, (i + 1) * L)
        va_s[blk, 0:DV] = mv_ref[0, blk, :]
        va_s[blk, DV:DA] = jnp.ones((L, DA - DV), BF16)
        cur = jnp.concatenate([mq_ref[0, blk, :], mk_ref[0, blk, :]], axis=1)
        shifted = jnp.dot(shift_mat, jnp.concatenate([prev, cur], axis=0),
                          preferred_element_type=F32)
        y = conv_b + cur.astype(F32) * conv_w[nshift:nshift + 1, :]
        for s in range(nshift):
            y = y + shifted[s * L:(s + 1) * L, :] * conv_w[nshift - 1 - s:nshift - s, :]
        y = _silu(y)
        q_s[blk, :] = y[:, 0:DK].astype(BF16)
        k_s[blk, :] = (y[:, DK:2 * DK] * (DK ** -0.5)).astype(BF16)
        prev = cur

    lane = lax.broadcasted_iota(jnp.int32, (1, LANES), 1)
    bias = bg_ref[...]
    b_i = jnp.sum(jnp.where(lane == h, bias, 0.0), axis=1, keepdims=True)
    b_f = jnp.sum(jnp.where(lane == h + MLSTM_HEADS, bias, 0.0), axis=1, keepdims=True)
    ri = lax.broadcasted_iota(jnp.int32, (L, L), 0)
    ci = lax.broadcasted_iota(jnp.int32, (L, L), 1)
    causal = ci <= ri
    eye = (ri == ci).astype(F32)
    i_rows = gt_ref[h, 0] + b_i
    lf_rows = _log_sigmoid(gt_ref[h + MLSTM_HEADS, 0] + b_f)
    b_rows = jnp.dot(lf_rows, (ri <= ci).astype(F32), preferred_element_type=F32,
                     precision=HIGHEST)
    b_end = b_rows[:, L - 1:L]
    g_rows = b_end - b_rows + i_rows
    g_max = jnp.max(g_rows, axis=1, keepdims=True)
    m = jnp.zeros((1, 1), F32)
    m_prev, m_new = [], []
    for c in range(NC):
        m_prev.append(m)
        m = jnp.maximum(b_end[c:c + 1, :] + m, g_max[c:c + 1, :])
        m_new.append(m)
    m_prev = jnp.concatenate(m_prev, axis=0)
    m_new = jnp.concatenate(m_new, axis=0)
    rows_s[0] = b_rows
    rows_s[1] = jnp.exp(g_rows - m_new)
    rows_s[2] = b_rows - i_rows
    rows_s[3] = jnp.broadcast_to(m_prev, (NC, L))
    rows_s[4] = jnp.broadcast_to(jnp.exp(b_end + m_prev - m_new), (NC, L))

    r2 = lax.broadcasted_iota(jnp.int32, (2 * L, 2 * L), 0)
    c2 = lax.broadcasted_iota(jnp.int32, (2 * L, 2 * L), 1)
    ones_blk = ((r2 < L) == (c2 < L)).astype(BF16)

    G = MLSTM_GROUP

    def local(cg, _):
        cs = [cg * G + i for i in range(G)]
        rows = [pl.ds(pl.multiple_of(c * L, L), L) for c in cs]
        b_r = [rows_s[0, pl.ds(c, 1), :] for c in cs]
        w_r = [rows_s[1, pl.ds(c, 1), :] for c in cs]
        u_r = [rows_s[2, pl.ds(c, 1), :] for c in cs]
        mp = [rows_s[3, pl.ds(c, 1), :] for c in cs]
        q = [q_s[r, :] for r in rows]
        k = [k_s[r, :] for r in rows]
        va = [va_s[r, :] for r in rows]
        qk = [_nt(a, b) for a, b in zip(q, k)]
        x2 = [jnp.concatenate([eye * a, eye * b], axis=1) for a, b in zip(b_r, w_r)]
        hi = [x.astype(BF16) for x in x2]
        lo = [(x - h_.astype(F32)).astype(BF16) for x, h_ in zip(x2, hi)]
        yb = [jnp.dot(h_, ones_blk, preferred_element_type=F32)
              + jnp.dot(l_, ones_blk, preferred_element_type=F32) for h_, l_ in zip(hi, lo)]
        b_b = [y[:, 0:L] for y in yb]
        w_b = [y[:, L:2 * L] for y in yb]
        for i in range(G):
            kv_s[cs[i]] = _tn((w_b[i] * k[i].astype(F32)).astype(BF16), va[i])
        dmat = [jnp.where(causal, b - u, NEG) for b, u in zip(b_b, u_r)]
        m_t = [jnp.maximum(b + m_, jnp.max(d, axis=1, keepdims=True))
               for b, m_, d in zip(b_b, mp, dmat)]
        sc = [a * jnp.exp(d - m_) for a, d, m_ in zip(qk, dmat, m_t)]
        for i in range(G):
            acc_s[rows[i], :] = jnp.dot(sc[i].astype(BF16), va[i], preferred_element_type=F32)
            inter_s[rows[i], :] = jnp.exp(b_b[i] + mp[i] - m_t[i])
            emt_s[rows[i], :] = jnp.exp(-m_t[i])
        return 0

    lax.fori_loop(0, NC // G, local, 0)

    g_row = gm_ref[...]
    c_s[...] = jnp.zeros((DK, DA), F32)

    def recur(cg, _):
        cs = [cg * G + i for i in range(G)]
        rows = [pl.ds(pl.multiple_of(c * L, L), L) for c in cs]
        states = [c_s[...]]
        for c in cs:
            dec = rows_s[4, pl.ds(c, 1), :]
            states.append(jnp.concatenate([dec, dec, dec], axis=1) * states[-1] + kv_s[c])
        c_s[...] = states[G]
        read = [jnp.dot(q_s[r, :], st.astype(BF16), preferred_element_type=F32)
                for r, st in zip(rows, states)]
        inter = [inter_s[r, :] for r in rows]
        out = [acc_s[r, :] + jnp.concatenate([it, it, it], axis=1) * rd
               for r, it, rd in zip(rows, inter, read)]
        emt = [emt_s[r, :] for r in rows]
        nrm = [jnp.maximum(jnp.abs(jnp.concatenate([o[:, DV:DA], o[:, DV:DA]], axis=1)),
                           jnp.concatenate([e_, e_], axis=1)) for o, e_ in zip(out, emt)]
        hh = [o[:, 0:DV] / n_ for o, n_ in zip(out, nrm)]
        ms = [jnp.mean(x * x, axis=1, keepdims=True) for x in hh]
        hn = [x * lax.rsqrt(m_ + NORM_EPS) * g_row for x, m_ in zip(hh, ms)]
        for i in range(G):
            o_ref[0, rows[i], :] = (hn[i] * _sigmoid(mo_ref[0, rows[i], :].astype(F32))).astype(BF16)
        return 0

    lax.fori_loop(0, NC // G, recur, 0)


def _mlstm(proj3, gates_t, conv_w, conv_b, bg_row, g_mlstm, anchor_a, anchor_b):
    B, S, _ = proj3.shape
    H, DK, DV = MLSTM_HEADS, MLSTM_QK_DIM, MLSTM_V_DIM
    L = MLSTM_BLOCK
    NC = S // L
    DA = DV + LANES
    qb, kb = OFF_MQ // DK, OFF_MK // DK
    vb, ob = OFF_MV // DV, OFF_MO // DV
    nq = H
    return pl.pallas_call(
        functools.partial(_mlstm_kernel, seq=S),
        grid=(B, H),
        in_specs=[pl.BlockSpec((1, S, DK), lambda b, h: (b, 0, qb + h)),
                  pl.BlockSpec((1, S, DK), lambda b, h: (b, 0, kb + h)),
                  pl.BlockSpec((1, S, DV), lambda b, h: (b, 0, vb + h)),
                  pl.BlockSpec((1, S, DV), lambda b, h: (b, 0, ob + h)),
                  pl.BlockSpec((2 * H, 1, NC, L), lambda b, h: (0, b, 0, 0)),
                  pl.BlockSpec((CONV_WIDTH, DK), lambda b, h: (0, h)),
                  pl.BlockSpec((CONV_WIDTH, DK), lambda b, h: (0, nq + h)),
                  pl.BlockSpec((1, DK), lambda b, h: (0, h)),
                  pl.BlockSpec((1, DK), lambda b, h: (0, nq + h)),
                  pl.BlockSpec((1, LANES), lambda b, h: (0, 0)),
                  pl.BlockSpec((1, DV), lambda b, h: (0, h)),
                  pl.BlockSpec(memory_space=pl.ANY), pl.BlockSpec(memory_space=pl.ANY)],
        out_specs=pl.BlockSpec((1, S, DV), lambda b, h: (b, 0, h)),
        out_shape=jax.ShapeDtypeStruct((B, S, H * DV), BF16),
        scratch_shapes=[pltpu.VMEM((S, DK), BF16),
                        pltpu.VMEM((S, DK), BF16),
                        pltpu.VMEM((S, DA), BF16),
                        pltpu.VMEM((5, NC, L), F32),
                        pltpu.VMEM((S, DA), F32),
                        pltpu.VMEM((NC, DK, DA), F32),
                        pltpu.VMEM((S, L), F32),
                        pltpu.VMEM((S, L), F32),
                        pltpu.VMEM((DK, DA), F32)],
        compiler_params=pltpu.CompilerParams(
            dimension_semantics=("arbitrary", "arbitrary"), vmem_limit_bytes=VMEM_LIMIT),
        name="mlstm_chunkwise",
    )(proj3, proj3, proj3, proj3, gates_t, conv_w, conv_w, conv_b, conv_b, bg_row, g_mlstm,
      anchor_a, anchor_b)


def _rms(y, g):
    ms = jnp.mean(y * y, axis=-1, keepdims=True)
    return y * lax.rsqrt(ms + NORM_EPS) * g


def _merge_kernel(ya_ref, yb_ref, ga_ref, gb_ref, x_ref, mod_ref, wa_ref, wb_ref, wo_ref,
                  gpost_ref, gpre_ref, anchor_ref, x1_ref, h2_ref):
    del anchor_ref
    tm = x_ref.shape[0]
    slabs = [pl.ds(s * (tm // MERGE_SPLIT), tm // MERGE_SPLIT) for s in range(MERGE_SPLIT)]
    pa = [jnp.dot(ya_ref[r, :], wa_ref[...], preferred_element_type=F32) for r in slabs]
    pb = [jnp.dot(yb_ref[r, :], wb_ref[...], preferred_element_type=F32) for r in slabs]
    merged = [_sigmoid(ga_ref[r, :].astype(F32)) * a + _sigmoid(gb_ref[r, :].astype(F32)) * b
              for r, a, b in zip(slabs, pa, pb)]
    y = [jnp.dot(m.astype(BF16), wo_ref[...], preferred_element_type=F32) for m in merged]
    x1 = [x_ref[r, :] + mod_ref[0, 2:3, :] * _rms(v, gpost_ref[...]) for r, v in zip(slabs, y)]
    for r, v in zip(slabs, x1):
        x1_ref[r, :] = v
    h2 = [_rms(v, gpre_ref[...]) * (1.0 + mod_ref[0, 4:5, :]) + mod_ref[0, 3:4, :] for v in x1]
    for r, v in zip(slabs, h2):
        h2_ref[r, :] = _pack_pair(v[:, :HALF], v[:, HALF:])


def _merge(ya2, yb2, proj2, x2, mod3, wa, wb, wo, g_post, g_pre, seq, anchor):
    T = x2.shape[0]
    tm = 512 * MERGE_SPLIT
    per_b = seq // tm
    full = lambda shape: pl.BlockSpec(shape, lambda i: (0,) * len(shape))
    return pl.pallas_call(
        _merge_kernel,
        grid=(T // tm,),
        in_specs=[pl.BlockSpec((tm, ATT_GROUP_W), lambda i: (i, 0)),
                  pl.BlockSpec((tm, D_MODEL), lambda i: (i, 0)),
                  pl.BlockSpec((tm, D_MODEL), lambda i: (i, OFF_GA // D_MODEL)),
                  pl.BlockSpec((tm, D_MODEL), lambda i: (i, OFF_GB // D_MODEL)),
                  pl.BlockSpec((tm, D_MODEL), lambda i: (i, 0)),
                  pl.BlockSpec((1, 6, D_MODEL), lambda i: (i // per_b, 0, 0)),
                  full((ATT_GROUP_W, D_MODEL)), full((D_MODEL, D_MODEL)), full((D_MODEL, D_MODEL)),
                  full((1, D_MODEL)), full((1, D_MODEL)),
                  pl.BlockSpec(memory_space=pl.ANY)],
        out_specs=[pl.BlockSpec((tm, D_MODEL), lambda i: (i, 0)),
                   pl.BlockSpec((tm, HALF), lambda i: (i, 0))],
        out_shape=[jax.ShapeDtypeStruct((T, D_MODEL), F32),
                   jax.ShapeDtypeStruct((T, HALF), jnp.uint32)],
        compiler_params=pltpu.CompilerParams(
            dimension_semantics=("arbitrary",), vmem_limit_bytes=VMEM_LIMIT),
        name="merge_out_proj",
    )(ya2, yb2, proj2, proj2, x2, mod3, wa, wb, wo, g_post, g_pre, anchor)


def _router_kernel(h2_ref, rlo_ref, rhi_ref, bias_ref, idx_ref, w_ref, rank_ref, cnt_ref):
    E = N_EXPERTS
    tr = h2_ref.shape[0]
    gsz = E // N_GROUPS

    @pl.when(pl.program_id(0) == 0)
    def _():
        cnt_ref[...] = jnp.zeros(cnt_ref.shape, F32)

    lo, hi = _unpack_pair(h2_ref[...])
    logits = _nt(rlo_ref[...], lo.astype(BF16)) + _nt(rhi_ref[...], hi.astype(BF16))
    scores = _sigmoid(logits)
    sel = scores + bias_ref[:, 0:1]

    gi = lax.broadcasted_iota(jnp.int32, (gsz, tr), 0).astype(F32)
    gs_rows = []
    for g in range(N_GROUPS):
        blk = sel[g * gsz:(g + 1) * gsz, :]
        m1 = jnp.max(blk, axis=0, keepdims=True)
        a1 = jnp.min(jnp.where(blk == m1, gi, float(E)), axis=0, keepdims=True)
        m2 = jnp.max(jnp.where(gi == a1, -jnp.inf, blk), axis=0, keepdims=True)
        gs_rows.append(m1 + m2)
    gs = jnp.concatenate(gs_rows, axis=0)
    g8 = lax.broadcasted_iota(jnp.int32, (N_GROUPS, tr), 0).astype(F32)
    gmask = jnp.zeros((N_GROUPS, tr), F32)
    for _ in range(TOPK_GROUPS):
        m = jnp.max(gs, axis=0, keepdims=True)
        a = jnp.min(jnp.where(gs == m, g8, float(E)), axis=0, keepdims=True)
        hit = g8 == a
        gmask = jnp.where(hit, 1.0, gmask)
        gs = jnp.where(hit, -jnp.inf, gs)
    selm = jnp.concatenate(
        [jnp.where(gmask[g:g + 1, :] > 0.0, sel[g * gsz:(g + 1) * gsz, :], -jnp.inf)
         for g in range(N_GROUPS)], axis=0)

    ei = lax.broadcasted_iota(jnp.int32, (E, tr), 0).astype(F32)
    picks, weights, hits = [], [], []
    candidates = selm
    for _ in range(TOP_K):
        m = jnp.max(selm, axis=0, keepdims=True)
        a = jnp.min(jnp.where(selm == m, ei, float(E)), axis=0, keepdims=True)
        hit = ei == a
        picks.append(a)
        hits.append(hit)
        weights.append(jnp.sum(jnp.where(hit, scores, 0.0), axis=0, keepdims=True))
        selm = jnp.where(hit, -jnp.inf, selm)
    chosen = jnp.where(selm != candidates, 1.0, 0.0)
    wsum = weights[0]
    for w in weights[1:]:
        wsum = wsum + w

    ti = lax.broadcasted_iota(jnp.int32, (tr, tr), 0)
    tj = lax.broadcasted_iota(jnp.int32, (tr, tr), 1)
    before = (ti < tj).astype(BF16)
    pos = jnp.dot(chosen.astype(BF16), before, preferred_element_type=F32) + cnt_ref[:, 0:1]
    ranks = [jnp.sum(jnp.where(hit, pos, 0.0), axis=0, keepdims=True) for hit in hits]
    cnt_ref[...] = cnt_ref[...] + jnp.sum(chosen, axis=1, keepdims=True)

    idx_ref[...] = jnp.concatenate(picks, axis=0).astype(jnp.int32)
    w_ref[...] = jnp.concatenate([w / wsum * ROUTED_SCALE for w in weights], axis=0)
    rank_ref[...] = jnp.concatenate(ranks, axis=0).astype(jnp.int32)


def _router(h2p, r_lo, r_hi, bias_col, row0, T):
    tr = 512
    off = row0 // tr
    full = lambda shape: pl.BlockSpec(shape, lambda i: (0,) * len(shape))
    return pl.pallas_call(
        _router_kernel,
        grid=(T // tr,),
        in_specs=[pl.BlockSpec((tr, HALF), lambda i: (i + off, 0)),
                  full((N_EXPERTS, HALF)), full((N_EXPERTS, HALF)), full((N_EXPERTS, LANES))],
        out_specs=[pl.BlockSpec((TOP_K, tr), lambda i: (0, i)),
                   pl.BlockSpec((TOP_K, tr), lambda i: (0, i)),
                   pl.BlockSpec((TOP_K, tr), lambda i: (0, i)),
                   full((N_EXPERTS, LANES))],
        out_shape=[jax.ShapeDtypeStruct((TOP_K, T), jnp.int32),
                   jax.ShapeDtypeStruct((TOP_K, T), F32),
                   jax.ShapeDtypeStruct((TOP_K, T), jnp.int32),
                   jax.ShapeDtypeStruct((N_EXPERTS, LANES), F32)],
        compiler_params=pltpu.CompilerParams(
            dimension_semantics=("arbitrary",), vmem_limit_bytes=VMEM_LIMIT),
        name="router_topk",
    )(h2p, r_lo, r_hi, bias_col)


def _dest_kernel(idx_ref, rank_ref, pstart_ref, dest_ref):
    tr = idx_ref.shape[1]
    ei = lax.broadcasted_iota(jnp.int32, (N_EXPERTS, tr), 0)
    start = pstart_ref[:, 0:1]
    rows = []
    for k in range(TOP_K):
        hit = ei == idx_ref[k:k + 1, :]
        rows.append(jnp.sum(jnp.where(hit, start, 0.0), axis=0, keepdims=True))
    dest_ref[...] = jnp.concatenate(rows, axis=0).astype(jnp.int32) + rank_ref[...]


def _slot_index(idx, rank, pstart_col):
    T = idx.shape[1]
    tr = 1024
    return pl.pallas_call(
        _dest_kernel,
        grid=(T // tr,),
        in_specs=[pl.BlockSpec((TOP_K, tr), lambda i: (0, i)),
                  pl.BlockSpec((TOP_K, tr), lambda i: (0, i)),
                  pl.BlockSpec((N_EXPERTS, LANES), lambda i: (0, 0))],
        out_specs=pl.BlockSpec((TOP_K, tr), lambda i: (0, i)),
        out_shape=jax.ShapeDtypeStruct((TOP_K, T), jnp.int32),
        name="slot_index",
    )(idx, rank, pstart_col)


def _ffn_kernel(first_ref, nblk_ref, nused_ref, xs_hbm, wg_ref, wu_ref, wd_ref, ys_hbm,
                xbuf, ybuf, in_sem, out_sem, wg_s, wu_s, wd_s):
    e = pl.program_id(0)
    bm = EXPERT_BLOCK
    ns = EXPERT_SLOTS
    nused = nused_ref[0]
    first = first_ref[e]
    n = nblk_ref[e]

    def in_copy(g):
        slot = g % ns
        return pltpu.make_async_copy(xs_hbm.at[pl.ds(g * bm, bm)], xbuf.at[slot], in_sem.at[slot])

    def out_copy(g):
        slot = g % ns
        return pltpu.make_async_copy(ybuf.at[slot], ys_hbm.at[pl.ds(g * bm, bm)], out_sem.at[slot])

    def fetch(g):
        @pl.when(g < nused)
        def _():
            in_copy(g).start()

    def release(g):
        @pl.when(g >= ns)
        def _():
            out_copy(g - ns).wait()

    def ffn(g):
        lo, hi = _unpack_pair(xbuf[g % ns])
        x = jnp.concatenate([lo.astype(BF16), hi.astype(BF16)], axis=1)
        gate = jnp.dot(x, wg_s[...], preferred_element_type=F32)
        up = jnp.dot(x, wu_s[...], preferred_element_type=F32)
        hid = (_silu(gate) * up).astype(BF16)
        return jnp.dot(hid, wd_s[...], preferred_element_type=F32)

    def pack(g, out):
        ybuf[g % ns] = _pack_pair(out[:, :HALF], out[:, HALF:])

    @pl.when(e == 0)
    def _():
        for q in range(ns - 1):
            fetch(q)

    @pl.when(n > 0)
    def _():
        for packed, dst in ((wg_ref, wg_s), (wu_ref, wu_s), (wd_ref, wd_s)):
            lo, hi = _unpack_pair(packed[0])
            half = dst.shape[0] // 2
            dst[0:half, :] = lo.astype(BF16)
            dst[half:, :] = hi.astype(BF16)

        def two_blocks(j, _):
            g = first + 2 * j
            in_copy(g).wait()
            in_copy(g + 1).wait()
            fetch(g + ns - 1)
            release(g)
            release(g + 1)
            out_a = ffn(g)
            out_b = ffn(g + 1)
            pack(g, out_a)
            pack(g + 1, out_b)
            out_copy(g).start()
            out_copy(g + 1).start()
            fetch(g + ns)
            return 0

        lax.fori_loop(0, n // 2, two_blocks, 0)

        @pl.when(n % 2 == 1)
        def _():
            g = first + n - 1
            in_copy(g).wait()
            fetch(g + ns - 1)
            release(g)
            pack(g, ffn(g))
            out_copy(g).start()

    @pl.when(e == pl.num_programs(0) - 1)
    def _():
        for q in range(ns, 0, -1):
            @pl.when(nused >= q)
            def _(q=q):
                out_copy(nused - q).wait()


def _expert_ffn(first_blk, nblk, nused, xs, w_gate, w_up, w_down):
    P = xs.shape[0]
    bm = EXPERT_BLOCK
    w_map = lambda e, *_: (e, 0, 0)
    grid_spec = pltpu.PrefetchScalarGridSpec(
        num_scalar_prefetch=3,
        grid=(w_gate.shape[0],),
        in_specs=[pl.BlockSpec(memory_space=pl.ANY),
                  pl.BlockSpec((1, D_MODEL // 2, EXPERT_FF), w_map),
                  pl.BlockSpec((1, D_MODEL // 2, EXPERT_FF), w_map),
                  pl.BlockSpec((1, EXPERT_FF // 2, D_MODEL), w_map)],
        out_specs=pl.BlockSpec(memory_space=pl.ANY),
        scratch_shapes=[pltpu.VMEM((EXPERT_SLOTS, bm, HALF), jnp.uint32),
                        pltpu.VMEM((EXPERT_SLOTS, bm, HALF), jnp.uint32),
                        pltpu.SemaphoreType.DMA((EXPERT_SLOTS,)),
                        pltpu.SemaphoreType.DMA((EXPERT_SLOTS,)),
                        pltpu.VMEM((D_MODEL, EXPERT_FF), BF16),
                        pltpu.VMEM((D_MODEL, EXPERT_FF), BF16),
                        pltpu.VMEM((EXPERT_FF, D_MODEL), BF16)],
    )
    return pl.pallas_call(
        _ffn_kernel,
        grid_spec=grid_spec,
        out_shape=jax.ShapeDtypeStruct((P, HALF), jnp.uint32),
        compiler_params=pltpu.CompilerParams(
            dimension_semantics=("arbitrary",), vmem_limit_bytes=VMEM_LIMIT),
        name="routed_experts",
    )(first_blk, nblk, nused, xs, w_gate, w_up, w_down)


def _final_kernel(yg_ref, w_ref, h2_ref, x1_ref, mod_ref, wsg_ref, wsu_ref, wsd_ref, gpost_ref, *rest):
    o_ref = rest[-1]
    lo, hi = _unpack_pair(h2_ref[...])
    h2 = jnp.concatenate([lo.astype(BF16), hi.astype(BF16)], axis=1)
    gate = jnp.dot(h2, wsg_ref[...], preferred_element_type=F32)
    up = jnp.dot(h2, wsu_ref[...], preferred_element_type=F32)
    shared = jnp.dot((_silu(gate) * up).astype(BF16), wsd_ref[...], preferred_element_type=F32)
    y_lo = shared[:, :HALF]
    y_hi = shared[:, HALF:]
    for k in range(TOP_K):
        r_lo, r_hi = _unpack_pair(yg_ref[k])
        wk = w_ref[:, k:k + 1]
        y_lo = y_lo + wk * r_lo
        y_hi = y_hi + wk * r_hi
    ms = (jnp.sum(y_lo * y_lo, axis=-1, keepdims=True)
          + jnp.sum(y_hi * y_hi, axis=-1, keepdims=True)) * (1.0 / D_MODEL)
    inv = lax.rsqrt(ms + NORM_EPS)
    o_ref[:, 0:HALF] = x1_ref[:, 0:HALF] + mod_ref[0, 5:6, 0:HALF] * (y_lo * inv * gpost_ref[:, 0:HALF])
    o_ref[:, HALF:] = x1_ref[:, HALF:] + mod_ref[0, 5:6, HALF:] * (y_hi * inv * gpost_ref[:, HALF:])


def _final(yg, w_tk, h2p, x1, mod3, wsg, wsu, wsd, g_post, seq, row0, out_prev):
    T = x1.shape[0]
    tp = yg.shape[1]
    tm = 512
    per_b = seq // tm
    off = row0 // tm
    full = lambda shape: pl.BlockSpec(shape, lambda i: (0,) * len(shape))
    in_specs = [pl.BlockSpec((TOP_K, tm, HALF), lambda i: (0, i, 0)),
                pl.BlockSpec((tm, TOP_K), lambda i: (i, 0)),
                pl.BlockSpec((tm, HALF), lambda i: (i + off, 0)),
                pl.BlockSpec((tm, D_MODEL), lambda i: (i + off, 0)),
                pl.BlockSpec((1, 6, D_MODEL), lambda i: ((i + off) // per_b, 0, 0)),
                full((D_MODEL, EXPERT_FF)), full((D_MODEL, EXPERT_FF)), full((EXPERT_FF, D_MODEL)),
                full((1, D_MODEL))]
    args = [yg, w_tk, h2p, x1, mod3, wsg, wsu, wsd, g_post]
    aliases = {}
    if out_prev is not None:
        in_specs.append(pl.BlockSpec(memory_space=pl.ANY))
        args.append(out_prev)
        aliases = {len(args) - 1: 0}
    return pl.pallas_call(
        _final_kernel,
        grid=(tp // tm,),
        in_specs=in_specs,
        out_specs=pl.BlockSpec((tm, D_MODEL), lambda i: (i + off, 0)),
        out_shape=jax.ShapeDtypeStruct((T, D_MODEL), F32),
        input_output_aliases=aliases,
        compiler_params=pltpu.CompilerParams(
            dimension_semantics=("arbitrary",), vmem_limit_bytes=VMEM_LIMIT),
        name="shared_expert_combine",
    )(*args)


def _rope_tables(positions):
    inv = jnp.power(ROPE_THETA, -jnp.arange(ROPE_HALF, dtype=F32) / ROPE_HALF)
    ang = positions.astype(F32)[..., None] * inv
    cos, sin = jnp.cos(ang), jnp.sin(ang)
    rest = ATT_HEAD_DIM - 2 * ROPE_HALF
    cs = jnp.concatenate([cos, cos, jnp.ones(ang.shape[:-1] + (rest,), F32)], axis=-1)
    sn = jnp.concatenate([-sin, sin, jnp.zeros(ang.shape[:-1] + (rest,), F32)], axis=-1)
    return jnp.tile(cs, (1, 1, 2)), jnp.tile(sn, (1, 1, 2))


def _layer(x, c, positions, w_ada, b_ada, g_pre_mix, g_post_mix, g_pre_ffn, g_post_ffn,
           w_in, conv_w, conv_b, b_gates, g_mlstm, w_branch_a, w_branch_b, w_out,
           router_w, router_bias, w_exp_gate, w_exp_up, w_exp_down, w_sh_gate, w_sh_up, w_sh_down):
    B, S, D = x.shape
    T = B * S
    H = MLSTM_HEADS
    x2 = x.reshape(T, D)

    mod3 = _adaln(c, w_ada, b_ada).reshape(B, 6, D)

    a_w = 3 * ATT_GROUP_W
    o_mq = 3 * a_w
    o_mk = o_mq + H * MLSTM_QK_DIM
    o_mv = o_mk + H * MLSTM_QK_DIM
    o_mo = o_mv + H * MLSTM_V_DIM
    o_mi = o_mo + H * MLSTM_V_DIM
    o_ga = o_mi + 2 * H
    o_gb = o_ga + D
    w_bf = w_in.astype(BF16)
    w_main = jnp.concatenate(
        [w_bf[:, o_mv:o_mi], w_bf[:, o_ga:o_gb + D], w_bf[:, o_mq:o_mv], w_bf[:, 0:o_mq]], axis=1)
    w_if = w_bf[:, o_mi:o_ga].T

    proj, gates = _in_proj(x2, mod3, g_pre_mix.reshape(1, D), w_main, w_if, S)
    proj3 = proj.reshape(B, S, PROJ_W)

    wg_p, wu_p, wd_p = (_pack_weight_rows(w, gates) for w in (w_exp_gate, w_exp_up, w_exp_down))

    cs, sn = _rope_tables(positions)
    y_a = _attention(proj3, cs, sn)

    bg_row = jnp.pad(b_gates.reshape(1, 2 * H), ((0, 0), (0, LANES - 2 * H)))
    gates_t = gates.reshape(2 * H, B, S // MLSTM_BLOCK, MLSTM_BLOCK)
    y_b = _mlstm(proj3, gates_t, conv_w, conv_b.reshape(1, -1), bg_row, g_mlstm.reshape(1, -1),
                 wg_p, wu_p)

    x1, h2p = _merge(y_a.reshape(T, ATT_GROUP_W), y_b.reshape(T, D), proj, x2, mod3,
                     w_branch_a.astype(BF16), w_branch_b.astype(BF16), w_out.astype(BF16),
                     g_post_mix.reshape(1, D), g_pre_ffn.reshape(1, D), S, wd_p)

    rw_t = router_w.T.astype(BF16)
    bias_col = jnp.broadcast_to(router_bias.reshape(N_EXPERTS, 1), (N_EXPERTS, LANES))
    wsg, wsu, wsd = w_sh_gate.astype(BF16), w_sh_up.astype(BF16), w_sh_down.astype(BF16)

    tp = T // MOE_PARTS
    bm = EXPERT_BLOCK
    nb = (tp * TOP_K) // bm + N_EXPERTS
    out = None
    for part in range(MOE_PARTS):
        row0 = part * tp
        idx, wts, rank, cnt = _router(h2p, rw_t[:, :HALF], rw_t[:, HALF:], bias_col, row0, tp)

        counts = cnt[:, 0].astype(jnp.int32)
        padded = (counts + bm - 1) // bm * bm
        pend = jnp.cumsum(padded)
        pstart = pend - padded
        pstart_col = jnp.broadcast_to(pstart.astype(F32).reshape(N_EXPERTS, 1), (N_EXPERTS, LANES))
        dest = _slot_index(idx, rank, pstart_col)
        nused = (pend[-1] // bm).astype(jnp.int32).reshape(1)

        xs = _dispatch(h2p, dest, nb * bm, row0)
        ys = _expert_ffn((pstart // bm).astype(jnp.int32), (padded // bm).astype(jnp.int32), nused,
                         xs, wg_p, wu_p, wd_p)
        yg = _collect(ys, dest)
        out = _final(yg, wts.T, h2p, x1, mod3, wsg, wsu, wsd, g_post_ffn.reshape(1, D), S, row0, out)
    return out.reshape(B, S, D)


SC_CORES = 2
SC_SUBCORES = 16
SC_WORKERS = SC_CORES * SC_SUBCORES
SC_ROWS = 64


def _sc_mesh():
    return plsc.VectorSubcoreMesh(core_axis_name="c", subcore_axis_name="s",
                                  num_cores=SC_CORES, num_subcores=SC_SUBCORES)


def _worker_id():
    return lax.axis_index("s") * SC_CORES + lax.axis_index("c")


def _dispatch(h2p, dest, n_slots, row0):
    T = dest.shape[1]
    per_w = T // SC_WORKERS
    nch = per_w // SC_ROWS
    idx = dest.reshape(TOP_K, SC_WORKERS, nch, SC_ROWS).transpose(1, 2, 0, 3)
    idx = idx.reshape(SC_WORKERS, nch * TOP_K, SC_ROWS)

    def body(x_hbm, idx_hbm, xs_hbm, idx_v, buf0, buf1, rsem0, rsem1, ssem0, ssem1):
        wid = _worker_id()
        base = row0 + wid * per_w
        pltpu.sync_copy(idx_hbm.at[wid], idx_v)
        bufs = ((buf0, rsem0, ssem0), (buf1, rsem1, ssem1))

        def read(c, buf, rsem):
            return pltpu.make_async_copy(x_hbm.at[pl.ds(base + c * SC_ROWS, SC_ROWS)], buf, rsem)

        def scatter(c, k, buf, ssem):
            return pltpu.make_async_copy(buf, xs_hbm.at[idx_v.at[c * TOP_K + k]], ssem)

        read(0, buf0, rsem0).start()

        @pl.loop(0, nch, step=2)
        def _(c0):
            for b in range(2):
                c = c0 + b
                buf, rsem, ssem = bufs[b]
                obuf, orsem, ossem = bufs[1 - b]
                read(c, buf, rsem).wait()

                @pl.when(c > 0)
                def _():
                    for k in range(TOP_K):
                        scatter(c - 1, k, obuf, ossem).wait()

                @pl.when(c + 1 < nch)
                def _():
                    read(c + 1, obuf, orsem).start()

                for k in range(TOP_K):
                    scatter(c, k, buf, ssem).start()

        for k in range(TOP_K):
            scatter(nch - 1, k, buf1, ssem1).wait()

    run = pl.kernel(
        body,
        out_type=jax.ShapeDtypeStruct((n_slots, HALF), jnp.uint32),
        mesh=_sc_mesh(),
        scratch_types=[pltpu.VMEM((nch * TOP_K, SC_ROWS), jnp.int32),
                       pltpu.VMEM((SC_ROWS, HALF), jnp.uint32),
                       pltpu.VMEM((SC_ROWS, HALF), jnp.uint32),
                       pltpu.SemaphoreType.DMA, pltpu.SemaphoreType.DMA,
                       pltpu.SemaphoreType.DMA, pltpu.SemaphoreType.DMA],
        name="sc_dispatch",
    )
    return run(h2p, idx)


SC_PACK_ROWS = 64
SC_PACK_COLS = 256
SC_LANES = 16


def _pack_weight_rows(w, after):
    E, R, C = w.shape
    hb = R // 2 // SC_PACK_ROWS
    w2 = w.reshape(E * R, C)

    def body(w_hbm, after_hbm, out_hbm):
        del after_hbm

        def block(lo_v, hi_v, out_v):
            @pl.loop(0, SC_PACK_ROWS)
            def _(r):
                @pl.loop(0, SC_PACK_COLS, step=SC_LANES)
                def _(c):
                    cols = pl.ds(c, SC_LANES)
                    pair = plsc.pack(lo_v[r, cols], hi_v[r, cols], format=plsc.PackFormat.INTERLEAVED)
                    out_v[r, cols] = plsc.bitcast(pair, jnp.uint32)

        blk = (SC_PACK_ROWS, SC_PACK_COLS)
        pltpu.emit_pipeline(
            block,
            grid=(E * hb, C // SC_PACK_COLS),
            in_specs=[pl.BlockSpec(blk, lambda i, j: ((i // hb) * 2 * hb + i % hb, j)),
                      pl.BlockSpec(blk, lambda i, j: ((i // hb) * 2 * hb + hb + i % hb, j))],
            out_specs=[pl.BlockSpec(blk, lambda i, j: (i, j))],
            core_axis_name=("c", "s"),
            dimension_semantics=(pltpu.PARALLEL, pltpu.PARALLEL),
        )(w_hbm, w_hbm, out_hbm)

    run = pl.kernel(body, out_type=jax.ShapeDtypeStruct((E * R // 2, C), jnp.uint32),
                    mesh=_sc_mesh(), scratch_types=[], name="sc_pack_weights",
                    compiler_params=pltpu.CompilerParams(needs_layout_passes=False))
    return run(w2, after).reshape(E, R // 2, C)


def _collect(ys, dest):
    n = dest.size
    per_w = n // SC_WORKERS
    nch = per_w // SC_ROWS
    idx = dest.reshape(SC_WORKERS, nch, SC_ROWS)

    def body(ys_hbm, idx_hbm, out_hbm, idx_v, buf0, buf1, gsem0, gsem1, wsem0, wsem1):
        wid = _worker_id()
        base = wid * per_w
        pltpu.sync_copy(idx_hbm.at[wid], idx_v)
        bufs = ((buf0, gsem0, wsem0), (buf1, gsem1, wsem1))

        def gather(c, buf, gsem):
            return pltpu.make_async_copy(ys_hbm.at[idx_v.at[c]], buf, gsem)

        def write(c, buf, wsem):
            return pltpu.make_async_copy(buf, out_hbm.at[pl.ds(base + c * SC_ROWS, SC_ROWS)], wsem)

        gather(0, buf0, gsem0).start()

        @pl.loop(0, nch, step=2)
        def _(c0):
            for b in range(2):
                c = c0 + b
                buf, gsem, wsem = bufs[b]
                obuf, ogsem, owsem = bufs[1 - b]
                gather(c, buf, gsem).wait()

                @pl.when(c > 0)
                def _():
                    write(c - 1, obuf, owsem).wait()

                @pl.when(c + 1 < nch)
                def _():
                    gather(c + 1, obuf, ogsem).start()

                write(c, buf, wsem).start()

        write(nch - 1, buf1, wsem1).wait()

    run = pl.kernel(
        body,
        out_type=jax.ShapeDtypeStruct((n, HALF), jnp.uint32),
        mesh=_sc_mesh(),
        scratch_types=[pltpu.VMEM((nch, SC_ROWS), jnp.int32),
                       pltpu.VMEM((SC_ROWS, HALF), jnp.uint32),
                       pltpu.VMEM((SC_ROWS, HALF), jnp.uint32),
                       pltpu.SemaphoreType.DMA, pltpu.SemaphoreType.DMA,
                       pltpu.SemaphoreType.DMA, pltpu.SemaphoreType.DMA],
        name="sc_collect",
    )
    return run(ys, idx).reshape(dest.shape + (HALF,))


def kernel(x, c, positions, w_ada, b_ada, g_pre_mix, g_post_mix, g_pre_ffn, g_post_ffn, w_in, conv_w, conv_b, b_gates, g_mlstm, w_branch_a, w_branch_b, w_out, router_w, router_bias, w_exp_gate, w_exp_up, w_exp_down, w_sh_gate, w_sh_up, w_sh_down):
    depth = w_ada.shape[0]
    for l in range(depth):
        x = _layer(x, c, positions, w_ada[l], b_ada[l], g_pre_mix[l], g_post_mix[l], g_pre_ffn[l],
                   g_post_ffn[l], w_in[l], conv_w[l], conv_b[l], b_gates[l], g_mlstm[l],
                   w_branch_a[l], w_branch_b[l], w_out[l], router_w[l], router_bias[l],
                   w_exp_gate[l], w_exp_up[l], w_exp_down[l], w_sh_gate[l], w_sh_up[l], w_sh_down[l])
    return x
```

```python
import functools

import jax
import jax.numpy as jnp
from jax import lax
from jax.experimental import pallas as pl
from jax.experimental.pallas import tpu as pltpu
from jax.experimental.pallas import tpu_sc as plsc

F32 = jnp.float32
BF16 = jnp.bfloat16
HIGHEST = lax.Precision.HIGHEST
LANES = 128

D_MODEL = 1024
ATT_GROUPS = ((128, 1), (512, 4), (2048, 16))
ATT_HEAD_DIM = 64
ATT_GROUP_W = 256
ATT_BLK = 128
ATT_PAIR = 2
ROPE_THETA = 500000.0
ROPE_HALF = 8
MLSTM_HEADS = 4
MLSTM_QK_DIM = 128
MLSTM_V_DIM = 256
MLSTM_BLOCK = 128
MLSTM_GROUP = 16
CONV_WIDTH = 4
N_EXPERTS = 256
TOP_K = 8
N_GROUPS = 8
TOPK_GROUPS = 4
EXPERT_FF = 256
ROUTED_SCALE = 2.5
NORM_EPS = 1e-6
NEG = -1e30

OFF_MV, OFF_MO, OFF_GA, OFF_GB = 0, 1024, 2048, 3072
OFF_MQ, OFF_MK = 4096, 4608
OFF_AQ, OFF_AK, OFF_AV = 5120, 5888, 6656
PROJ_W = 7424
HALF = D_MODEL // 2

EXPERT_BLOCK = 512
EXPERT_SLOTS = 6
MOE_PARTS = 2
MERGE_SPLIT = 2
VMEM_LIMIT = 56 * 1024 * 1024


def _nt(a, b):
    return lax.dot_general(a, b, (((1,), (1,)), ((), ())), preferred_element_type=F32)


def _tn(a, b):
    return lax.dot_general(a, b, (((0,), (0,)), ((), ())), preferred_element_type=F32)


_sigmoid = jax.nn.sigmoid


def _silu(x):
    return x * _sigmoid(x)


def _pack_pair(lo, hi):
    lo_b = pltpu.bitcast(lo.astype(BF16).astype(F32), jnp.uint32)
    hi_b = pltpu.bitcast(hi.astype(BF16).astype(F32), jnp.uint32)
    return (lo_b >> 16) | (hi_b & jnp.uint32(0xFFFF0000))


def _unpack_pair(w):
    lo = pltpu.bitcast(w << 16, F32)
    hi = pltpu.bitcast(w & jnp.uint32(0xFFFF0000), F32)
    return lo, hi


def _mod_kernel(c_ref, w_ref, b_ref, o_ref):
    a = _silu(c_ref[...])
    o_ref[...] = jnp.dot(a, w_ref[...], preferred_element_type=F32, precision=HIGHEST) + b_ref[...]


def _adaln(c, w_ada, b_ada):
    B = c.shape[0]
    n = w_ada.shape[1]
    tn = 512
    return pl.pallas_call(
        _mod_kernel,
        grid=(n // tn,),
        in_specs=[pl.BlockSpec((B, D_MODEL), lambda j: (0, 0)),
                  pl.BlockSpec((D_MODEL, tn), lambda j: (0, j)),
                  pl.BlockSpec((1, tn), lambda j: (0, j))],
        out_specs=pl.BlockSpec((B, tn), lambda j: (0, j)),
        out_shape=jax.ShapeDtypeStruct((B, n), F32),
        name="adaln_mod",
    )(c, w_ada, b_ada.reshape(1, n))


def _proj_kernel(x_ref, mod_ref, g_ref, w_ref, wif_ref, o_ref, gates_ref, h_ref):
    @pl.when(pl.program_id(1) == 0)
    def _():
        x = x_ref[...]
        ms = jnp.mean(x * x, axis=-1, keepdims=True)
        y = x * lax.rsqrt(ms + NORM_EPS) * g_ref[...]
        h = (y * (1.0 + mod_ref[0, 1:2, :]) + mod_ref[0, 0:1, :]).astype(BF16)
        h_ref[...] = h
        gates_ref[...] = _nt(wif_ref[...], h)

    o_ref[...] = jnp.dot(h_ref[...], w_ref[...], preferred_element_type=F32).astype(BF16)


def _in_proj(x2, mod3, g_pre, w_main, w_if, seq):
    T = x2.shape[0]
    tm, tn = 1024, PROJ_W // 2
    per_b = seq // tm
    return pl.pallas_call(
        _proj_kernel,
        grid=(T // tm, PROJ_W // tn),
        in_specs=[pl.BlockSpec((tm, D_MODEL), lambda i, j: (i, 0)),
                  pl.BlockSpec((1, 6, D_MODEL), lambda i, j: (i // per_b, 0, 0)),
                  pl.BlockSpec((1, D_MODEL), lambda i, j: (0, 0)),
                  pl.BlockSpec((D_MODEL, tn), lambda i, j: (0, j)),
                  pl.BlockSpec((2 * MLSTM_HEADS, D_MODEL), lambda i, j: (0, 0))],
        out_specs=[pl.BlockSpec((tm, tn), lambda i, j: (i, j)),
                   pl.BlockSpec((2 * MLSTM_HEADS, tm), lambda i, j: (0, i))],
        out_shape=[jax.ShapeDtypeStruct((T, PROJ_W), BF16),
                   jax.ShapeDtypeStruct((2 * MLSTM_HEADS, T), F32)],
        scratch_shapes=[pltpu.VMEM((tm, D_MODEL), BF16)],
        compiler_params=pltpu.CompilerParams(
            dimension_semantics=("arbitrary", "arbitrary"), vmem_limit_bytes=VMEM_LIMIT),
        name="norm_in_proj",
    )(x2, mod3, g_pre, w_main, w_if)


def _attn_kernel(q_ref, k_ref, v_ref, cs_ref, sn_ref, o_ref, qf, kf, vf, acc, m_s, l_s, *, seq):
    g = pl.program_id(1)
    lane = lax.broadcasted_iota(jnp.int32, (ATT_BLK, LANES), 1)
    first = (lane % ATT_HEAD_DIM) < ROPE_HALF
    low_head = lane < ATT_HEAD_DIM

    def rope(x, cs, sn):
        partner = jnp.where(first, pltpu.roll(x, LANES - ROPE_HALF, 1), pltpu.roll(x, ROPE_HALF, 1))
        return x * cs + partner * sn

    def zero_pad(i, _):
        rows = pl.ds(pl.multiple_of(i * ATT_BLK, ATT_BLK), ATT_BLK)
        for hp in range(2):
            kf[hp, rows, :] = jnp.zeros((ATT_BLK, LANES), F32)
            vf[hp, rows, :] = jnp.zeros((ATT_BLK, LANES), F32)
        return 0

    lax.fori_loop(0, seq // ATT_BLK, zero_pad, 0)

    def stage(i, _):
        r = pl.multiple_of(i * ATT_BLK, ATT_BLK)
        rows = pl.ds(r, ATT_BLK)
        prow = pl.ds(pl.multiple_of(seq + i * ATT_BLK, ATT_BLK), ATT_BLK)
        cs = cs_ref[0, rows, :]
        sn = sn_ref[0, rows, :]
        for hp in range(2):
            cols = pl.ds(hp * LANES, LANES)
            qf[hp, rows, :] = rope(q_ref[0, rows, cols].astype(F32), cs, sn) * (ATT_HEAD_DIM ** -0.5)
            kf[hp, prow, :] = rope(k_ref[0, rows, cols].astype(F32), cs, sn)
            vf[hp, prow, :] = v_ref[0, rows, cols].astype(F32)
        return 0

    lax.fori_loop(0, seq // ATT_BLK, stage, 0)

    qi = lax.broadcasted_iota(jnp.int32, (ATT_BLK, 2 * ATT_BLK), 0)
    ki = lax.broadcasted_iota(jnp.int32, (ATT_BLK, 2 * ATT_BLK), 1)
    band = (ki >= qi) & (ki <= qi + ATT_BLK)

    def process(d, init):
        span = ATT_BLK * d
        single = seq == span

        def body(cp, _):
            blocks = [cp * ATT_PAIR + i for i in range(ATT_PAIR)]
            qrows, krows, valid = [], [], []
            for c in blocks:
                rho = c % d
                n = c // d
                qstart = rho + n * span
                if single:
                    kstart, nk = seq + qstart, ATT_BLK
                    valid.append(band[:, ATT_BLK:])
                else:
                    kstart, nk = seq + qstart - span, 2 * ATT_BLK
                    valid.append(band & (ki >= jnp.where(n > 0, 0, ATT_BLK)))
                qrows.append(pl.ds(qstart, ATT_BLK, stride=d) if d > 1 else pl.ds(qstart, ATT_BLK))
                krows.append(pl.ds(kstart, nk, stride=d) if d > 1 else pl.ds(kstart, nk))
            units = [(b, hp) for b in range(ATT_PAIR) for hp in range(2)]
            heads = [(u, hh) for u in range(len(units)) for hh in range(2)]
            q2 = [qf[hp, qrows[b], :] for b, hp in units]
            k2 = [kf[hp, krows[b], :].astype(BF16) for b, hp in units]
            v2 = [vf[hp, krows[b], :].astype(BF16) for b, hp in units]
            qh = [jnp.where(low_head if hh == 0 else jnp.logical_not(low_head), q2[u], 0.0).astype(BF16)
                  for u, hh in heads]
            s = [jnp.where(valid[units[u][0]], _nt(qh[i], k2[u]), NEG) for i, (u, hh) in enumerate(heads)]
            m = [jnp.max(x, axis=1, keepdims=True) for x in s]
            p = [jnp.exp(x - mx) for x, mx in zip(s, m)]
            l = [jnp.sum(x, axis=1, keepdims=True) for x in p]
            o = [jnp.dot(p[i].astype(BF16), v2[u], preferred_element_type=F32)
                 for i, (u, hh) in enumerate(heads)]
            for u, (b, hp) in enumerate(units):
                o_b = jnp.where(low_head, o[2 * u], o[2 * u + 1])
                m_b = jnp.where(low_head, m[2 * u], m[2 * u + 1])
                l_b = jnp.where(low_head, l[2 * u], l[2 * u + 1])
                if init:
                    acc[hp, qrows[b], :] = o_b
                    m_s[hp, qrows[b], :] = m_b
                    l_s[hp, qrows[b], :] = l_b
                else:
                    m_old = m_s[hp, qrows[b], :]
                    m_new = jnp.maximum(m_old, m_b)
                    a_old = jnp.exp(m_old - m_new)
                    a_new = jnp.exp(m_b - m_new)
                    acc[hp, qrows[b], :] = acc[hp, qrows[b], :] * a_old + o_b * a_new
                    l_s[hp, qrows[b], :] = l_s[hp, qrows[b], :] * a_old + l_b * a_new
                    m_s[hp, qrows[b], :] = m_new
            return 0

        lax.fori_loop(0, seq // (ATT_BLK * ATT_PAIR), body, 0)

    for gi, (_, d) in enumerate(ATT_GROUPS):
        @pl.when(g == gi)
        def _(d=d, gi=gi):
            process(d, gi == 0)

    @pl.when(g == len(ATT_GROUPS) - 1)
    def _():
        def fin(i, _):
            rows = pl.ds(pl.multiple_of(i * ATT_BLK, ATT_BLK), ATT_BLK)
            for hp in range(2):
                o_ref[0, rows, pl.ds(hp * LANES, LANES)] = (acc[hp, rows, :] / l_s[hp, rows, :]).astype(BF16)
            return 0

        lax.fori_loop(0, seq // ATT_BLK, fin, 0)


def _attention(proj3, cs, sn):
    B, S, _ = proj3.shape
    ng = len(ATT_GROUPS)
    qb, kb, vb = OFF_AQ // ATT_GROUP_W, OFF_AK // ATT_GROUP_W, OFF_AV // ATT_GROUP_W
    return pl.pallas_call(
        functools.partial(_attn_kernel, seq=S),
        grid=(B, ng),
        in_specs=[pl.BlockSpec((1, S, ATT_GROUP_W), lambda b, g: (b, 0, qb + g)),
                  pl.BlockSpec((1, S, ATT_GROUP_W), lambda b, g: (b, 0, kb + g)),
                  pl.BlockSpec((1, S, ATT_GROUP_W), lambda b, g: (b, 0, vb + g)),
                  pl.BlockSpec((1, S, LANES), lambda b, g: (b, 0, 0)),
                  pl.BlockSpec((1, S, LANES), lambda b, g: (b, 0, 0))],
        out_specs=pl.BlockSpec((1, S, ATT_GROUP_W), lambda b, g: (b, 0, 0)),
        out_shape=jax.ShapeDtypeStruct((B, S, ATT_GROUP_W), BF16),
        scratch_shapes=[pltpu.VMEM((2, S, LANES), F32),
                        pltpu.VMEM((2, 2 * S, LANES), F32),
                        pltpu.VMEM((2, 2 * S, LANES), F32),
                        pltpu.VMEM((2, S, LANES), F32),
                        pltpu.VMEM((2, S, LANES), F32),
                        pltpu.VMEM((2, S, LANES), F32)],
        compiler_params=pltpu.CompilerParams(
            dimension_semantics=("arbitrary", "arbitrary"), vmem_limit_bytes=VMEM_LIMIT),
        name="dilated_attention",
    )(proj3, proj3, proj3, cs, sn)


def _log_sigmoid(x):
    return jnp.minimum(x, 0.0) - jnp.log(1.0 + jnp.exp(-jnp.abs(x)))


def _mlstm_kernel(mq_ref, mk_ref, mv_ref, mo_ref, gt_ref, cwq_ref, cwk_ref, cbq_ref, cbk_ref,
                  bg_ref, gm_ref, anchor_a, anchor_b, o_ref, q_s, k_s, va_s, rows_s, acc_s, kv_s,
                  inter_s, emt_s, c_s, *, seq):
    del anchor_a, anchor_b
    h = pl.program_id(1)
    L = MLSTM_BLOCK
    NC = seq // L
    DK, DV = MLSTM_QK_DIM, MLSTM_V_DIM
    DA = DV + LANES
    nshift = CONV_WIDTH - 1

    tt = lax.broadcasted_iota(jnp.int32, (nshift * L, 2 * L), 0)
    uu = lax.broadcasted_iota(jnp.int32, (nshift * L, 2 * L), 1)
    shift_mat = (uu == L + tt % L - (tt // L + 1)).astype(BF16)
    conv_w = jnp.concatenate([cwq_ref[...], cwk_ref[...]], axis=1)
    conv_b = jnp.concatenate([cbq_ref[...], cbk_ref[...]], axis=1)
    prev = jnp.zeros((L, 2 * DK), BF16)
    for i in range(NC):
        blk = slice(i * L, (i + 1) * L)
        va_s[blk, 0:DV] = mv_ref[0, blk, :]
        va_s[blk, DV:DA] = jnp.ones((L, DA - DV), BF16)
        cur = jnp.concatenate([mq_ref[0, blk, :], mk_ref[0, blk, :]], axis=1)
        shifted = jnp.dot(shift_mat, jnp.concatenate([prev, cur], axis=0),
                          preferred_element_type=F32)
        y = conv_b + cur.astype(F32) * conv_w[nshift:nshift + 1, :]
        for s in range(nshift):
            y = y + shifted[s * L:(s + 1) * L, :] * conv_w[nshift - 1 - s:nshift - s, :]
        y = _silu(y)
        q_s[blk, :] = y[:, 0:DK].astype(BF16)
        k_s[blk, :] = (y[:, DK:2 * DK] * (DK ** -0.5)).astype(BF16)
        prev = cur

    lane = lax.broadcasted_iota(jnp.int32, (1, LANES), 1)
    bias = bg_ref[...]
    b_i = jnp.sum(jnp.where(lane == h, bias, 0.0), axis=1, keepdims=True)
    b_f = jnp.sum(jnp.where(lane == h + MLSTM_HEADS, bias, 0.0), axis=1, keepdims=True)
    ri = lax.broadcasted_iota(jnp.int32, (L, L), 0)
    ci = lax.broadcasted_iota(jnp.int32, (L, L), 1)
    causal = ci <= ri
    eye = (ri == ci).astype(F32)
    i_rows = gt_ref[h, 0] + b_i
    lf_rows = _log_sigmoid(gt_ref[h + MLSTM_HEADS, 0] + b_f)
    b_rows = jnp.dot(lf_rows, (ri <= ci).astype(F32), preferred_element_type=F32,
                     precision=HIGHEST)
    b_end = b_rows[:, L - 1:L]
    g_rows = b_end - b_rows + i_rows
    g_max = jnp.max(g_rows, axis=1, keepdims=True)
    m = jnp.zeros((1, 1), F32)
    m_prev, m_new = [], []
    for c in range(NC):
        m_prev.append(m)
        m = jnp.maximum(b_end[c:c + 1, :] + m, g_max[c:c + 1, :])
        m_new.append(m)
    m_prev = jnp.concatenate(m_prev, axis=0)
    m_new = jnp.concatenate(m_new, axis=0)
    rows_s[0] = b_rows
    rows_s[1] = jnp.exp(g_rows - m_new)
    rows_s[2] = b_rows - i_rows
    rows_s[3] = jnp.broadcast_to(m_prev, (NC, L))
    rows_s[4] = jnp.broadcast_to(jnp.exp(b_end + m_prev - m_new), (NC, L))

    r2 = lax.broadcasted_iota(jnp.int32, (2 * L, 2 * L), 0)
    c2 = lax.broadcasted_iota(jnp.int32, (2 * L, 2 * L), 1)
    ones_blk = ((r2 < L) == (c2 < L)).astype(BF16)

    G = MLSTM_GROUP

    def local(cg, _):
        cs = [cg * G + i for i in range(G)]
        rows = [pl.ds(pl.multiple_of(c * L, L), L) for c in cs]
        b_r = [rows_s[0, pl.ds(c, 1), :] for c in cs]
        w_r = [rows_s[1, pl.ds(c, 1), :] for c in cs]
        u_r = [rows_s[2, pl.ds(c, 1), :] for c in cs]
        mp = [rows_s[3, pl.ds(c, 1), :] for c in cs]
        q = [q_s[r, :] for r in rows]
        k = [k_s[r, :] for r in rows]
        va = [va_s[r, :] for r in rows]
        qk = [_nt(a, b) for a, b in zip(q, k)]
        x2 = [jnp.concatenate([eye * a, eye * b], axis=1) for a, b in zip(b_r, w_r)]
        hi = [x.astype(BF16) for x in x2]
        lo = [(x - h_.astype(F32)).astype(BF16) for x, h_ in zip(x2, hi)]
        yb = [jnp.dot(h_, ones_blk, preferred_element_type=F32)
              + jnp.dot(l_, ones_blk, preferred_element_type=F32) for h_, l_ in zip(hi, lo)]
        b_b = [y[:, 0:L] for y in yb]
        w_b = [y[:, L:2 * L] for y in yb]
        for i in range(G):
            kv_s[cs[i]] = _tn((w_b[i] * k[i].astype(F32)).astype(BF16), va[i])
        dmat = [jnp.where(causal, b - u, NEG) for b, u in zip(b_b, u_r)]
        m_t = [jnp.maximum(b + m_, jnp.max(d, axis=1, keepdims=True))
               for b, m_, d in zip(b_b, mp, dmat)]
        sc = [a * jnp.exp(d - m_) for a, d, m_ in zip(qk, dmat, m_t)]
        for i in range(G):
            acc_s[rows[i], :] = jnp.dot(sc[i].astype(BF16), va[i], preferred_element_type=F32)
            inter_s[rows[i], :] = jnp.exp(b_b[i] + mp[i] - m_t[i])
            emt_s[rows[i], :] = jnp.exp(-m_t[i])
        return 0

    lax.fori_loop(0, NC // G, local, 0)

    g_row = gm_ref[...]
    c_s[...] = jnp.zeros((DK, DA), F32)

    def recur(cg, _):
        cs = [cg * G + i for i in range(G)]
        rows = [pl.ds(pl.multiple_of(c * L, L), L) for c in cs]
        states = [c_s[...]]
        for c in cs:
            dec = rows_s[4, pl.ds(c, 1), :]
            states.append(jnp.concatenate([dec, dec, dec], axis=1) * states[-1] + kv_s[c])
        c_s[...] = states[G]
        read = [jnp.dot(q_s[r, :], st.astype(BF16), preferred_element_type=F32)
                for r, st in zip(rows, states)]
        inter = [inter_s[r, :] for r in rows]
        out = [acc_s[r, :] + jnp.concatenate([it, it, it], axis=1) * rd
               for r, it, rd in zip(rows, inter, read)]
        emt = [emt_s[r, :] for r in rows]
        nrm = [jnp.maximum(jnp.abs(jnp.concatenate([o[:, DV:DA], o[:, DV:DA]], axis=1)),
                           jnp.concatenate([e_, e_], axis=1)) for o, e_ in zip(out, emt)]
        hh = [o[:, 0:DV] / n_ for o, n_ in zip(out, nrm)]
        ms = [jnp.mean(x * x, axis=1, keepdims=True) for x in hh]
        hn = [x * lax.rsqrt(m_ + NORM_EPS) * g_row for x, m_ in zip(hh, ms)]
        for i in range(G):
            o_ref[0, rows[i], :] = (hn[i] * _sigmoid(mo_ref[0, rows[i], :].astype(F32))).astype(BF16)
        return 0

    lax.fori_loop(0, NC // G, recur, 0)


def _mlstm(proj3, gates_t, conv_w, conv_b, bg_row, g_mlstm, anchor_a, anchor_b):
    B, S, _ = proj3.shape
    H, DK, DV = MLSTM_HEADS, MLSTM_QK_DIM, MLSTM_V_DIM
    L = MLSTM_BLOCK
    NC = S // L
    DA = DV + LANES
    qb, kb = OFF_MQ // DK, OFF_MK // DK
    vb, ob = OFF_MV // DV, OFF_MO // DV
    nq = H
    return pl.pallas_call(
        functools.partial(_mlstm_kernel, seq=S),
        grid=(B, H),
        in_specs=[pl.BlockSpec((1, S, DK), lambda b, h: (b, 0, qb + h)),
                  pl.BlockSpec((1, S, DK), lambda b, h: (b, 0, kb + h)),
                  pl.BlockSpec((1, S, DV), lambda b, h: (b, 0, vb + h)),
                  pl.BlockSpec((1, S, DV), lambda b, h: (b, 0, ob + h)),
                  pl.BlockSpec((2 * H, 1, NC, L), lambda b, h: (0, b, 0, 0)),
                  pl.BlockSpec((CONV_WIDTH, DK), lambda b, h: (0, h)),
                  pl.BlockSpec((CONV_WIDTH, DK), lambda b, h: (0, nq + h)),
                  pl.BlockSpec((1, DK), lambda b, h: (0, h)),
                  pl.BlockSpec((1, DK), lambda b, h: (0, nq + h)),
                  pl.BlockSpec((1, LANES), lambda b, h: (0, 0)),
                  pl.BlockSpec((1, DV), lambda b, h: (0, h)),
                  pl.BlockSpec(memory_space=pl.ANY), pl.BlockSpec(memory_space=pl.ANY)],
        out_specs=pl.BlockSpec((1, S, DV), lambda b, h: (b, 0, h)),
        out_shape=jax.ShapeDtypeStruct((B, S, H * DV), BF16),
        scratch_shapes=[pltpu.VMEM((S, DK), BF16),
                        pltpu.VMEM((S, DK), BF16),
                        pltpu.VMEM((S, DA), BF16),
                        pltpu.VMEM((5, NC, L), F32),
                        pltpu.VMEM((S, DA), F32),
                        pltpu.VMEM((NC, DK, DA), F32),
                        pltpu.VMEM((S, L), F32),
                        pltpu.VMEM((S, L), F32),
                        pltpu.VMEM((DK, DA), F32)],
        compiler_params=pltpu.CompilerParams(
            dimension_semantics=("arbitrary", "arbitrary"), vmem_limit_bytes=VMEM_LIMIT),
        name="mlstm_chunkwise",
    )(proj3, proj3, proj3, proj3, gates_t, conv_w, conv_w, conv_b, conv_b, bg_row, g_mlstm,
      anchor_a, anchor_b)


def _rms(y, g):
    ms = jnp.mean(y * y, axis=-1, keepdims=True)
    return y * lax.rsqrt(ms + NORM_EPS) * g


def _merge_kernel(ya_ref, yb_ref, ga_ref, gb_ref, x_ref, mod_ref, wa_ref, wb_ref, wo_ref,
                  gpost_ref, gpre_ref, anchor_ref, x1_ref, h2_ref):
    del anchor_ref
    tm = x_ref.shape[0]
    slabs = [pl.ds(s * (tm // MERGE_SPLIT), tm // MERGE_SPLIT) for s in range(MERGE_SPLIT)]
    pa = [jnp.dot(ya_ref[r, :], wa_ref[...], preferred_element_type=F32) for r in slabs]
    pb = [jnp.dot(yb_ref[r, :], wb_ref[...], preferred_element_type=F32) for r in slabs]
    merged = [_sigmoid(ga_ref[r, :].astype(F32)) * a + _sigmoid(gb_ref[r, :].astype(F32)) * b
              for r, a, b in zip(slabs, pa, pb)]
    y = [jnp.dot(m.astype(BF16), wo_ref[...], preferred_element_type=F32) for m in merged]
    x1 = [x_ref[r, :] + mod_ref[0, 2:3, :] * _rms(v, gpost_ref[...]) for r, v in zip(slabs, y)]
    for r, v in zip(slabs, x1):
        x1_ref[r, :] = v
    h2 = [_rms(v, gpre_ref[...]) * (1.0 + mod_ref[0, 4:5, :]) + mod_ref[0, 3:4, :] for v in x1]
    for r, v in zip(slabs, h2):
        h2_ref[r, :] = _pack_pair(v[:, :HALF], v[:, HALF:])


def _merge(ya2, yb2, proj2, x2, mod3, wa, wb, wo, g_post, g_pre, seq, anchor):
    T = x2.shape[0]
    tm = 512 * MERGE_SPLIT
    per_b = seq // tm
    full = lambda shape: pl.BlockSpec(shape, lambda i: (0,) * len(shape))
    return pl.pallas_call(
        _merge_kernel,
        grid=(T // tm,),
        in_specs=[pl.BlockSpec((tm, ATT_GROUP_W), lambda i: (i, 0)),
                  pl.BlockSpec((tm, D_MODEL), lambda i: (i, 0)),
                  pl.BlockSpec((tm, D_MODEL), lambda i: (i, OFF_GA // D_MODEL)),
                  pl.BlockSpec((tm, D_MODEL), lambda i: (i, OFF_GB // D_MODEL)),
                  pl.BlockSpec((tm, D_MODEL), lambda i: (i, 0)),
                  pl.BlockSpec((1, 6, D_MODEL), lambda i: (i // per_b, 0, 0)),
                  full((ATT_GROUP_W, D_MODEL)), full((D_MODEL, D_MODEL)), full((D_MODEL, D_MODEL)),
                  full((1, D_MODEL)), full((1, D_MODEL)),
                  pl.BlockSpec(memory_space=pl.ANY)],
        out_specs=[pl.BlockSpec((tm, D_MODEL), lambda i: (i, 0)),
                   pl.BlockSpec((tm, HALF), lambda i: (i, 0))],
        out_shape=[jax.ShapeDtypeStruct((T, D_MODEL), F32),
                   jax.ShapeDtypeStruct((T, HALF), jnp.uint32)],
        compiler_params=pltpu.CompilerParams(
            dimension_semantics=("arbitrary",), vmem_limit_bytes=VMEM_LIMIT),
        name="merge_out_proj",
    )(ya2, yb2, proj2, proj2, x2, mod3, wa, wb, wo, g_post, g_pre, anchor)


def _router_kernel(h2_ref, rlo_ref, rhi_ref, bias_ref, idx_ref, w_ref, rank_ref, cnt_ref):
    E = N_EXPERTS
    tr = h2_ref.shape[0]
    gsz = E // N_GROUPS

    @pl.when(pl.program_id(0) == 0)
    def _():
        cnt_ref[...] = jnp.zeros(cnt_ref.shape, F32)

    lo, hi = _unpack_pair(h2_ref[...])
    logits = _nt(rlo_ref[...], lo.astype(BF16)) + _nt(rhi_ref[...], hi.astype(BF16))
    scores = _sigmoid(logits)
    sel = scores + bias_ref[:, 0:1]

    gi = lax.broadcasted_iota(jnp.int32, (gsz, tr), 0).astype(F32)
    gs_rows = []
    for g in range(N_GROUPS):
        blk = sel[g * gsz:(g + 1) * gsz, :]
        m1 = jnp.max(blk, axis=0, keepdims=True)
        a1 = jnp.min(jnp.where(blk == m1, gi, float(E)), axis=0, keepdims=True)
        m2 = jnp.max(jnp.where(gi == a1, -jnp.inf, blk), axis=0, keepdims=True)
        gs_rows.append(m1 + m2)
    gs = jnp.concatenate(gs_rows, axis=0)
    g8 = lax.broadcasted_iota(jnp.int32, (N_GROUPS, tr), 0).astype(F32)
    gmask = jnp.zeros((N_GROUPS, tr), F32)
    for _ in range(TOPK_GROUPS):
        m = jnp.max(gs, axis=0, keepdims=True)
        a = jnp.min(jnp.where(gs == m, g8, float(E)), axis=0, keepdims=True)
        hit = g8 == a
        gmask = jnp.where(hit, 1.0, gmask)
        gs = jnp.where(hit, -jnp.inf, gs)
    selm = jnp.concatenate(
        [jnp.where(gmask[g:g + 1, :] > 0.0, sel[g * gsz:(g + 1) * gsz, :], -jnp.inf)
         for g in range(N_GROUPS)], axis=0)

    ei = lax.broadcasted_iota(jnp.int32, (E, tr), 0).astype(F32)
    picks, weights, hits = [], [], []
    candidates = selm
    for _ in range(TOP_K):
        m = jnp.max(selm, axis=0, keepdims=True)
        a = jnp.min(jnp.where(selm == m, ei, float(E)), axis=0, keepdims=True)
        hit = ei == a
        picks.append(a)
        hits.append(hit)
        weights.append(jnp.sum(jnp.where(hit, scores, 0.0), axis=0, keepdims=True))
        selm = jnp.where(hit, -jnp.inf, selm)
    chosen = jnp.where(selm != candidates, 1.0, 0.0)
    wsum = weights[0]
    for w in weights[1:]:
        wsum = wsum + w

    ti = lax.broadcasted_iota(jnp.int32, (tr, tr), 0)
    tj = lax.broadcasted_iota(jnp.int32, (tr, tr), 1)
    before = (ti < tj).astype(BF16)
    pos = jnp.dot(chosen.astype(BF16), before, preferred_element_type=F32) + cnt_ref[:, 0:1]
    ranks = [jnp.sum(jnp.where(hit, pos, 0.0), axis=0, keepdims=True) for hit in hits]
    cnt_ref[...] = cnt_ref[...] + jnp.sum(chosen, axis=1, keepdims=True)

    idx_ref[...] = jnp.concatenate(picks, axis=0).astype(jnp.int32)
    w_ref[...] = jnp.concatenate([w / wsum * ROUTED_SCALE for w in weights], axis=0)
    rank_ref[...] = jnp.concatenate(ranks, axis=0).astype(jnp.int32)


def _router(h2p, r_lo, r_hi, bias_col, row0, T):
    tr = 512
    off = row0 // tr
    full = lambda shape: pl.BlockSpec(shape, lambda i: (0,) * len(shape))
    return pl.pallas_call(
        _router_kernel,
        grid=(T // tr,),
        in_specs=[pl.BlockSpec((tr, HALF), lambda i: (i + off, 0)),
                  full((N_EXPERTS, HALF)), full((N_EXPERTS, HALF)), full((N_EXPERTS, LANES))],
        out_specs=[pl.BlockSpec((TOP_K, tr), lambda i: (0, i)),
                   pl.BlockSpec((TOP_K, tr), lambda i: (0, i)),
                   pl.BlockSpec((TOP_K, tr), lambda i: (0, i)),
                   full((N_EXPERTS, LANES))],
        out_shape=[jax.ShapeDtypeStruct((TOP_K, T), jnp.int32),
                   jax.ShapeDtypeStruct((TOP_K, T), F32),
                   jax.ShapeDtypeStruct((TOP_K, T), jnp.int32),
                   jax.ShapeDtypeStruct((N_EXPERTS, LANES), F32)],
        compiler_params=pltpu.CompilerParams(
            dimension_semantics=("arbitrary",), vmem_limit_bytes=VMEM_LIMIT),
        name="router_topk",
    )(h2p, r_lo, r_hi, bias_col)


def _dest_kernel(idx_ref, rank_ref, pstart_ref, dest_ref):
    tr = idx_ref.shape[1]
    ei = lax.broadcasted_iota(jnp.int32, (N_EXPERTS, tr), 0)
    start = pstart_ref[:, 0:1]
    rows = []
    for k in range(TOP_K):
        hit = ei == idx_ref[k:k + 1, :]
        rows.append(jnp.sum(jnp.where(hit, start, 0.0), axis=0, keepdims=True))
    dest_ref[...] = jnp.concatenate(rows, axis=0).astype(jnp.int32) + rank_ref[...]


def _slot_index(idx, rank, pstart_col):
    T = idx.shape[1]
    tr = 1024
    return pl.pallas_call(
        _dest_kernel,
        grid=(T // tr,),
        in_specs=[pl.BlockSpec((TOP_K, tr), lambda i: (0, i)),
                  pl.BlockSpec((TOP_K, tr), lambda i: (0, i)),
                  pl.BlockSpec((N_EXPERTS, LANES), lambda i: (0, 0))],
        out_specs=pl.BlockSpec((TOP_K, tr), lambda i: (0, i)),
        out_shape=jax.ShapeDtypeStruct((TOP_K, T), jnp.int32),
        name="slot_index",
    )(idx, rank, pstart_col)


def _ffn_kernel(first_ref, nblk_ref, nused_ref, xs_hbm, wg_ref, wu_ref, wd_ref, ys_hbm,
                xbuf, ybuf, in_sem, out_sem, wg_s, wu_s, wd_s):
    e = pl.program_id(0)
    bm = EXPERT_BLOCK
    ns = EXPERT_SLOTS
    nused = nused_ref[0]
    first = first_ref[e]
    n = nblk_ref[e]

    def in_copy(g):
        slot = g % ns
        return pltpu.make_async_copy(xs_hbm.at[pl.ds(g * bm, bm)], xbuf.at[slot], in_sem.at[slot])

    def out_copy(g):
        slot = g % ns
        return pltpu.make_async_copy(ybuf.at[slot], ys_hbm.at[pl.ds(g * bm, bm)], out_sem.at[slot])

    def fetch(g):
        @pl.when(g < nused)
        def _():
            in_copy(g).start()

    def release(g):
        @pl.when(g >= ns)
        def _():
            out_copy(g - ns).wait()

    def ffn(g):
        lo, hi = _unpack_pair(xbuf[g % ns])
        x = jnp.concatenate([lo.astype(BF16), hi.astype(BF16)], axis=1)
        gate = jnp.dot(x, wg_s[...], preferred_element_type=F32)
        up = jnp.dot(x, wu_s[...], preferred_element_type=F32)
        hid = (_silu(gate) * up).astype(BF16)
        return jnp.dot(hid, wd_s[...], preferred_element_type=F32)

    def pack(g, out):
        ybuf[g % ns] = _pack_pair(out[:, :HALF], out[:, HALF:])

    @pl.when(e == 0)
    def _():
        for q in range(ns - 1):
            fetch(q)

    @pl.when(n > 0)
    def _():
        for packed, dst in ((wg_ref, wg_s), (wu_ref, wu_s), (wd_ref, wd_s)):
            lo, hi = _unpack_pair(packed[0])
            half = dst.shape[0] // 2
            dst[0:half, :] = lo.astype(BF16)
            dst[half:, :] = hi.astype(BF16)

        def two_blocks(j, _):
            g = first + 2 * j
            in_copy(g).wait()
            in_copy(g + 1).wait()
            fetch(g + ns - 1)
            release(g)
            release(g + 1)
            out_a = ffn(g)
            out_b = ffn(g + 1)
            pack(g, out_a)
            pack(g + 1, out_b)
            out_copy(g).start()
            out_copy(g + 1).start()
            fetch(g + ns)
            return 0

        lax.fori_loop(0, n // 2, two_blocks, 0)

        @pl.when(n % 2 == 1)
        def _():
            g = first + n - 1
            in_copy(g).wait()
            fetch(g + ns - 1)
            release(g)
            pack(g, ffn(g))
            out_copy(g).start()

    @pl.when(e == pl.num_programs(0) - 1)
    def _():
        for q in range(ns, 0, -1):
            @pl.when(nused >= q)
            def _(q=q):
                out_copy(nused - q).wait()


def _expert_ffn(first_blk, nblk, nused, xs, w_gate, w_up, w_down):
    P = xs.shape[0]
    bm = EXPERT_BLOCK
    w_map = lambda e, *_: (e, 0, 0)
    grid_spec = pltpu.PrefetchScalarGridSpec(
        num_scalar_prefetch=3,
        grid=(w_gate.shape[0],),
        in_specs=[pl.BlockSpec(memory_space=pl.ANY),
                  pl.BlockSpec((1, D_MODEL // 2, EXPERT_FF), w_map),
                  pl.BlockSpec((1, D_MODEL // 2, EXPERT_FF), w_map),
                  pl.BlockSpec((1, EXPERT_FF // 2, D_MODEL), w_map)],
        out_specs=pl.BlockSpec(memory_space=pl.ANY),
        scratch_shapes=[pltpu.VMEM((EXPERT_SLOTS, bm, HALF), jnp.uint32),
                        pltpu.VMEM((EXPERT_SLOTS, bm, HALF), jnp.uint32),
                        pltpu.SemaphoreType.DMA((EXPERT_SLOTS,)),
                        pltpu.SemaphoreType.DMA((EXPERT_SLOTS,)),
                        pltpu.VMEM((D_MODEL, EXPERT_FF), BF16),
                        pltpu.VMEM((D_MODEL, EXPERT_FF), BF16),
                        pltpu.VMEM((EXPERT_FF, D_MODEL), BF16)],
    )
    return pl.pallas_call(
        _ffn_kernel,
        grid_spec=grid_spec,
        out_shape=jax.ShapeDtypeStruct((P, HALF), jnp.uint32),
        compiler_params=pltpu.CompilerParams(
            dimension_semantics=("arbitrary",), vmem_limit_bytes=VMEM_LIMIT),
        name="routed_experts",
    )(first_blk, nblk, nused, xs, w_gate, w_up, w_down)


def _final_kernel(yg_ref, w_ref, h2_ref, x1_ref, mod_ref, wsg_ref, wsu_ref, wsd_ref, gpost_ref, *rest):
    o_ref = rest[-1]
    lo, hi = _unpack_pair(h2_ref[...])
    h2 = jnp.concatenate([lo.astype(BF16), hi.astype(BF16)], axis=1)
    gate = jnp.dot(h2, wsg_ref[...], preferred_element_type=F32)
    up = jnp.dot(h2, wsu_ref[...], preferred_element_type=F32)
    shared = jnp.dot((_silu(gate) * up).astype(BF16), wsd_ref[...], preferred_element_type=F32)
    y_lo = shared[:, :HALF]
    y_hi = shared[:, HALF:]
    for k in range(TOP_K):
        r_lo, r_hi = _unpack_pair(yg_ref[k])
        wk = w_ref[:, k:k + 1]
        y_lo = y_lo + wk * r_lo
        y_hi = y_hi + wk * r_hi
    ms = (jnp.sum(y_lo * y_lo, axis=-1, keepdims=True)
          + jnp.sum(y_hi * y_hi, axis=-1, keepdims=True)) * (1.0 / D_MODEL)
    inv = lax.rsqrt(ms + NORM_EPS)
    o_ref[:, 0:HALF] = x1_ref[:, 0:HALF] + mod_ref[0, 5:6, 0:HALF] * (y_lo * inv * gpost_ref[:, 0:HALF])
    o_ref[:, HALF:] = x1_ref[:, HALF:] + mod_ref[0, 5:6, HALF:] * (y_hi * inv * gpost_ref[:, HALF:])


def _final(yg, w_tk, h2p, x1, mod3, wsg, wsu, wsd, g_post, seq, row0, out_prev):
    T = x1.shape[0]
    tp = yg.shape[1]
    tm = 512
    per_b = seq // tm
    off = row0 // tm
    full = lambda shape: pl.BlockSpec(shape, lambda i: (0,) * len(shape))
    in_specs = [pl.BlockSpec((TOP_K, tm, HALF), lambda i: (0, i, 0)),
                pl.BlockSpec((tm, TOP_K), lambda i: (i, 0)),
                pl.BlockSpec((tm, HALF), lambda i: (i + off, 0)),
                pl.BlockSpec((tm, D_MODEL), lambda i: (i + off, 0)),
                pl.BlockSpec((1, 6, D_MODEL), lambda i: ((i + off) // per_b, 0, 0)),
                full((D_MODEL, EXPERT_FF)), full((D_MODEL, EXPERT_FF)), full((EXPERT_FF, D_MODEL)),
                full((1, D_MODEL))]
    args = [yg, w_tk, h2p, x1, mod3, wsg, wsu, wsd, g_post]
    aliases = {}
    if out_prev is not None:
        in_specs.append(pl.BlockSpec(memory_space=pl.ANY))
        args.append(out_prev)
        aliases = {len(args) - 1: 0}
    return pl.pallas_call(
        _final_kernel,
        grid=(tp // tm,),
        in_specs=in_specs,
        out_specs=pl.BlockSpec((tm, D_MODEL), lambda i: (i + off, 0)),
        out_shape=jax.ShapeDtypeStruct((T, D_MODEL), F32),
        input_output_aliases=aliases,
        compiler_params=pltpu.CompilerParams(
            dimension_semantics=("arbitrary",), vmem_limit_bytes=VMEM_LIMIT),
        name="shared_expert_combine",
    )(*args)


def _rope_tables(positions):
    inv = jnp.power(ROPE_THETA, -jnp.arange(ROPE_HALF, dtype=F32) / ROPE_HALF)
    ang = positions.astype(F32)[..., None] * inv
    cos, sin = jnp.cos(ang), jnp.sin(ang)
    rest = ATT_HEAD_DIM - 2 * ROPE_HALF
    cs = jnp.concatenate([cos, cos, jnp.ones(ang.shape[:-1] + (rest,), F32)], axis=-1)
    sn = jnp.concatenate([-sin, sin, jnp.zeros(ang.shape[:-1] + (rest,), F32)], axis=-1)
    return jnp.tile(cs, (1, 1, 2)), jnp.tile(sn, (1, 1, 2))


def _layer(x, c, positions, w_ada, b_ada, g_pre_mix, g_post_mix, g_pre_ffn, g_post_ffn,
           w_in, conv_w, conv_b, b_gates, g_mlstm, w_branch_a, w_branch_b, w_out,
           router_w, router_bias, w_exp_gate, w_exp_up, w_exp_down, w_sh_gate, w_sh_up, w_sh_down):
    B, S, D = x.shape
    T = B * S
    H = MLSTM_HEADS
    x2 = x.reshape(T, D)

    mod3 = _adaln(c, w_ada, b_ada).reshape(B, 6, D)

    a_w = 3 * ATT_GROUP_W
    o_mq = 3 * a_w
    o_mk = o_mq + H * MLSTM_QK_DIM
    o_mv = o_mk + H * MLSTM_QK_DIM
    o_mo = o_mv + H * MLSTM_V_DIM
    o_mi = o_mo + H * MLSTM_V_DIM
    o_ga = o_mi + 2 * H
    o_gb = o_ga + D
    w_bf = w_in.astype(BF16)
    w_main = jnp.concatenate(
        [w_bf[:, o_mv:o_mi], w_bf[:, o_ga:o_gb + D], w_bf[:, o_mq:o_mv], w_bf[:, 0:o_mq]], axis=1)
    w_if = w_bf[:, o_mi:o_ga].T

    proj, gates = _in_proj(x2, mod3, g_pre_mix.reshape(1, D), w_main, w_if, S)
    proj3 = proj.reshape(B, S, PROJ_W)

    wg_p, wu_p, wd_p = (_pack_weight_rows(w, gates) for w in (w_exp_gate, w_exp_up, w_exp_down))

    cs, sn = _rope_tables(positions)
    y_a = _attention(proj3, cs, sn)

    bg_row = jnp.pad(b_gates.reshape(1, 2 * H), ((0, 0), (0, LANES - 2 * H)))
    gates_t = gates.reshape(2 * H, B, S // MLSTM_BLOCK, MLSTM_BLOCK)
    y_b = _mlstm(proj3, gates_t, conv_w, conv_b.reshape(1, -1), bg_row, g_mlstm.reshape(1, -1),
                 wg_p, wu_p)

    x1, h2p = _merge(y_a.reshape(T, ATT_GROUP_W), y_b.reshape(T, D), proj, x2, mod3,
                     w_branch_a.astype(BF16), w_branch_b.astype(BF16), w_out.astype(BF16),
                     g_post_mix.reshape(1, D), g_pre_ffn.reshape(1, D), S, wd_p)

    rw_t = router_w.T.astype(BF16)
    bias_col = jnp.broadcast_to(router_bias.reshape(N_EXPERTS, 1), (N_EXPERTS, LANES))
    wsg, wsu, wsd = w_sh_gate.astype(BF16), w_sh_up.astype(BF16), w_sh_down.astype(BF16)

    tp = T // MOE_PARTS
    bm = EXPERT_BLOCK
    nb = (tp * TOP_K) // bm + N_EXPERTS
    out = None
    for part in range(MOE_PARTS):
        row0 = part * tp
        idx, wts, rank, cnt = _router(h2p, rw_t[:, :HALF], rw_t[:, HALF:], bias_col, row0, tp)

        counts = cnt[:, 0].astype(jnp.int32)
        padded = (counts + bm - 1) // bm * bm
        pend = jnp.cumsum(padded)
        pstart = pend - padded
        pstart_col = jnp.broadcast_to(pstart.astype(F32).reshape(N_EXPERTS, 1), (N_EXPERTS, LANES))
        dest = _slot_index(idx, rank, pstart_col)
        nused = (pend[-1] // bm).astype(jnp.int32).reshape(1)

        xs = _dispatch(h2p, dest, nb * bm, row0)
        ys = _expert_ffn((pstart // bm).astype(jnp.int32), (padded // bm).astype(jnp.int32), nused,
                         xs, wg_p, wu_p, wd_p)
        yg = _collect(ys, dest)
        out = _final(yg, wts.T, h2p, x1, mod3, wsg, wsu, wsd, g_post_ffn.reshape(1, D), S, row0, out)
    return out.reshape(B, S, D)


SC_CORES = 2
SC_SUBCORES = 16
SC_WORKERS = SC_CORES * SC_SUBCORES
SC_ROWS = 64


def _sc_mesh():
    return plsc.VectorSubcoreMesh(core_axis_name="c", subcore_axis_name="s",
                                  num_cores=SC_CORES, num_subcores=SC_SUBCORES)


def _worker_id():
    return lax.axis_index("s") * SC_CORES + lax.axis_index("c")


def _dispatch(h2p, dest, n_slots, row0):
    T = dest.shape[1]
    per_w = T // SC_WORKERS
    nch = per_w // SC_ROWS
    idx = dest.reshape(TOP_K, SC_WORKERS, nch, SC_ROWS).transpose(1, 2, 0, 3)
    idx = idx.reshape(SC_WORKERS, nch * TOP_K, SC_ROWS)

    def body(x_hbm, idx_hbm, xs_hbm, idx_v, buf0, buf1, rsem0, rsem1, ssem0, ssem1):
        wid = _worker_id()
        base = row0 + wid * per_w
        pltpu.sync_copy(idx_hbm.at[wid], idx_v)
        bufs = ((buf0, rsem0, ssem0), (buf1, rsem1, ssem1))

        def read(c, buf, rsem):
            return pltpu.make_async_copy(x_hbm.at[pl.ds(base + c * SC_ROWS, SC_ROWS)], buf, rsem)

        def scatter(c, k, buf, ssem):
            return pltpu.make_async_copy(buf, xs_hbm.at[idx_v.at[c * TOP_K + k]], ssem)

        read(0, buf0, rsem0).start()

        @pl.loop(0, nch, step=2)
        def _(c0):
            for b in range(2):
                c = c0 + b
                buf, rsem, ssem = bufs[b]
                obuf, orsem, ossem = bufs[1 - b]
                read(c, buf, rsem).wait()

                @pl.when(c > 0)
                def _():
                    for k in range(TOP_K):
                        scatter(c - 1, k, obuf, ossem).wait()

                @pl.when(c + 1 < nch)
                def _():
                    read(c + 1, obuf, orsem).start()

                for k in range(TOP_K):
                    scatter(c, k, buf, ssem).start()

        for k in range(TOP_K):
            scatter(nch - 1, k, buf1, ssem1).wait()

    run = pl.kernel(
        body,
        out_type=jax.ShapeDtypeStruct((n_slots, HALF), jnp.uint32),
        mesh=_sc_mesh(),
        scratch_types=[pltpu.VMEM((nch * TOP_K, SC_ROWS), jnp.int32),
                       pltpu.VMEM((SC_ROWS, HALF), jnp.uint32),
                       pltpu.VMEM((SC_ROWS, HALF), jnp.uint32),
                       pltpu.SemaphoreType.DMA, pltpu.SemaphoreType.DMA,
                       pltpu.SemaphoreType.DMA, pltpu.SemaphoreType.DMA],
        name="sc_dispatch",
    )
    return run(h2p, idx)


SC_PACK_ROWS = 64
SC_PACK_COLS = 256
SC_LANES = 16


def _pack_weight_rows(w, after):
    E, R, C = w.shape
    hb = R // 2 // SC_PACK_ROWS
    w2 = w.reshape(E * R, C)

    def body(w_hbm, after_hbm, out_hbm):
        del after_hbm

        def block(lo_v, hi_v, out_v):
            @pl.loop(0, SC_PACK_ROWS)
            def _(r):
                @pl.loop(0, SC_PACK_COLS, step=SC_LANES)
                def _(c):
                    cols = pl.ds(c, SC_LANES)
                    pair = plsc.pack(lo_v[r, cols], hi_v[r, cols], format=plsc.PackFormat.INTERLEAVED)
                    out_v[r, cols] = plsc.bitcast(pair, jnp.uint32)

        blk = (SC_PACK_ROWS, SC_PACK_COLS)
        pltpu.emit_pipeline(
            block,
            grid=(E * hb, C // SC_PACK_COLS),
            in_specs=[pl.BlockSpec(blk, lambda i, j: ((i // hb) * 2 * hb + i % hb, j)),
                      pl.BlockSpec(blk, lambda i, j: ((i // hb) * 2 * hb + hb + i % hb, j))],
            out_specs=[pl.BlockSpec(blk, lambda i, j: (i, j))],
            core_axis_name=("c", "s"),
            dimension_semantics=(pltpu.PARALLEL, pltpu.PARALLEL),
        )(w_hbm, w_hbm, out_hbm)

    run = pl.kernel(body, out_type=jax.ShapeDtypeStruct((E * R // 2, C), jnp.uint32),
                    mesh=_sc_mesh(), scratch_types=[], name="sc_pack_weights",
                    compiler_params=pltpu.CompilerParams(needs_layout_passes=False))
    return run(w2, after).reshape(E, R // 2, C)


def _collect(ys, dest):
    n = dest.size
    per_w = n // SC_WORKERS
    nch = per_w // SC_ROWS
    idx = dest.reshape(SC_WORKERS, nch, SC_ROWS)

    def body(ys_hbm, idx_hbm, out_hbm, idx_v, buf0, buf1, gsem0, gsem1, wsem0, wsem1):
        wid = _worker_id()
        base = wid * per_w
        pltpu.sync_copy(idx_hbm.at[wid], idx_v)
        bufs = ((buf0, gsem0, wsem0), (buf1, gsem1, wsem1))

        def gather(c, buf, gsem):
            return pltpu.make_async_copy(ys_hbm.at[idx_v.at[c]], buf, gsem)

        def write(c, buf, wsem):
            return pltpu.make_async_copy(buf, out_hbm.at[pl.ds(base + c * SC_ROWS, SC_ROWS)], wsem)

        gather(0, buf0, gsem0).start()

        @pl.loop(0, nch, step=2)
        def _(c0):
            for b in range(2):
                c = c0 + b
                buf, gsem, wsem = bufs[b]
                obuf, ogsem, owsem = bufs[1 - b]
                gather(c, buf, gsem).wait()

                @pl.when(c > 0)
                def _():
                    write(c - 1, obuf, owsem).wait()

                @pl.when(c + 1 < nch)
                def _():
                    gather(c + 1, obuf, ogsem).start()

                write(c, buf, wsem).start()

        write(nch - 1, buf1, wsem1).wait()

    run = pl.kernel(
        body,
        out_type=jax.ShapeDtypeStruct((n, HALF), jnp.uint32),
        mesh=_sc_mesh(),
        scratch_types=[pltpu.VMEM((nch, SC_ROWS), jnp.int32),
                       pltpu.VMEM((SC_ROWS, HALF), jnp.uint32),
                       pltpu.VMEM((SC_ROWS, HALF), jnp.uint32),
                       pltpu.SemaphoreType.DMA, pltpu.SemaphoreType.DMA,
                       pltpu.SemaphoreType.DMA, pltpu.SemaphoreType.DMA],
        name="sc_collect",
    )
    return run(ys, idx).reshape(dest.shape + (HALF,))


def kernel(x, c, positions, w_ada, b_ada, g_pre_mix, g_post_mix, g_pre_ffn, g_post_ffn, w_in, conv_w, conv_b, b_gates, g_mlstm, w_branch_a, w_branch_b, w_out, router_w, router_bias, w_exp_gate, w_exp_up, w_exp_down, w_sh_gate, w_sh_up, w_sh_down):
    depth = w_ada.shape[0]
    for l in range(depth):
        x = _layer(x, c, positions, w_ada[l], b_ada[l], g_pre_mix[l], g_post_mix[l], g_pre_ffn[l],
                   g_post_ffn[l], w_in[l], conv_w[l], conv_b[l], b_gates[l], g_mlstm[l],
                   w_branch_a[l], w_branch_b[l], w_out[l], router_w[l], router_bias[l],
                   w_exp_gate[l], w_exp_up[l], w_exp_down[l], w_sh_gate[l], w_sh_up[l], w_sh_down[l])
    return x
```

```python
import functools

import jax
import jax.numpy as jnp
from jax import lax
from jax.experimental import pallas as pl
from jax.experimental.pallas import tpu as pltpu
from jax.experimental.pallas import tpu_sc as plsc

F32 = jnp.float32
BF16 = jnp.bfloat16
HIGHEST = lax.Precision.HIGHEST
LANES = 128

D_MODEL = 1024
ATT_GROUPS = ((128, 1), (512, 4), (2048, 16))
ATT_HEAD_DIM = 64
ATT_GROUP_W = 256
ATT_BLK = 128
ATT_PAIR = 2
ROPE_THETA = 500000.0
ROPE_HALF = 8
MLSTM_HEADS = 4
MLSTM_QK_DIM = 128
MLSTM_V_DIM = 256
MLSTM_BLOCK = 128
MLSTM_GROUP = 16
CONV_WIDTH = 4
N_EXPERTS = 256
TOP_K = 8
N_GROUPS = 8
TOPK_GROUPS = 4
EXPERT_FF = 256
ROUTED_SCALE = 2.5
NORM_EPS = 1e-6
NEG = -1e30

OFF_MV = 0
OFF_MO = OFF_MV + MLSTM_HEADS * MLSTM_V_DIM
OFF_GA = OFF_MO + MLSTM_HEADS * MLSTM_V_DIM
OFF_GB = OFF_GA + D_MODEL
OFF_MQ = OFF_GB + D_MODEL
OFF_MK = OFF_MQ + MLSTM_HEADS * MLSTM_QK_DIM
OFF_AQ = OFF_MK + MLSTM_HEADS * MLSTM_QK_DIM
OFF_AK = OFF_AQ + len(ATT_GROUPS) * ATT_GROUP_W
OFF_AV = OFF_AK + len(ATT_GROUPS) * ATT_GROUP_W
PROJ_W = OFF_AV + len(ATT_GROUPS) * ATT_GROUP_W
HALF = D_MODEL // 2

EXPERT_BLOCK = 512
EXPERT_SLOTS = 6
MOE_PARTS = 2
MERGE_SPLIT = 2
VMEM_LIMIT = 56 * 1024 * 1024


def _nt(a, b):
    return lax.dot_general(a, b, (((1,), (1,)), ((), ())), preferred_element_type=F32)


def _tn(a, b):
    return lax.dot_general(a, b, (((0,), (0,)), ((), ())), preferred_element_type=F32)


_sigmoid = jax.nn.sigmoid


def _silu(x):
    return x * _sigmoid(x)


def _pack_pair(lo, hi):
    lo_b = pltpu.bitcast(lo.astype(BF16).astype(F32), jnp.uint32)
    hi_b = pltpu.bitcast(hi.astype(BF16).astype(F32), jnp.uint32)
    return (lo_b >> 16) | (hi_b & jnp.uint32(0xFFFF0000))


def _unpack_pair(w):
    lo = pltpu.bitcast(w << 16, F32)
    hi = pltpu.bitcast(w & jnp.uint32(0xFFFF0000), F32)
    return lo, hi


def _mod_kernel(c_ref, w_ref, b_ref, o_ref):
    a = _silu(c_ref[...])
    o_ref[...] = jnp.dot(a, w_ref[...], preferred_element_type=F32, precision=HIGHEST) + b_ref[...]


def _adaln(c, w_ada, b_ada):
    B = c.shape[0]
    n = w_ada.shape[1]
    tn = 512
    return pl.pallas_call(
        _mod_kernel,
        grid=(n // tn,),
        in_specs=[pl.BlockSpec((B, D_MODEL), lambda j: (0, 0)),
                  pl.BlockSpec((D_MODEL, tn), lambda j: (0, j)),
                  pl.BlockSpec((1, tn), lambda j: (0, j))],
        out_specs=pl.BlockSpec((B, tn), lambda j: (0, j)),
        out_shape=jax.ShapeDtypeStruct((B, n), F32),
        name="adaln_mod",
    )(c, w_ada, b_ada.reshape(1, n))


def _proj_kernel(x_ref, mod_ref, g_ref, w_ref, wif_ref, o_ref, gates_ref, h_ref):
    @pl.when(pl.program_id(1) == 0)
    def _():
        x = x_ref[...]
        ms = jnp.mean(x * x, axis=-1, keepdims=True)
        y = x * lax.rsqrt(ms + NORM_EPS) * g_ref[...]
        h = (y * (1.0 + mod_ref[0, 1:2, :]) + mod_ref[0, 0:1, :]).astype(BF16)
        h_ref[...] = h
        gates_ref[...] = _nt(wif_ref[...], h)

    o_ref[...] = jnp.dot(h_ref[...], w_ref[...], preferred_element_type=F32).astype(BF16)


def _in_proj(x2, mod3, g_pre, w_main, w_if, seq):
    T = x2.shape[0]
    tm, tn = 1024, PROJ_W // 2
    per_b = seq // tm
    return pl.pallas_call(
        _proj_kernel,
        grid=(T // tm, PROJ_W // tn),
        in_specs=[pl.BlockSpec((tm, D_MODEL), lambda i, j: (i, 0)),
                  pl.BlockSpec((1, 6, D_MODEL), lambda i, j: (i // per_b, 0, 0)),
                  pl.BlockSpec((1, D_MODEL), lambda i, j: (0, 0)),
                  pl.BlockSpec((D_MODEL, tn), lambda i, j: (0, j)),
                  pl.BlockSpec((2 * MLSTM_HEADS, D_MODEL), lambda i, j: (0, 0))],
        out_specs=[pl.BlockSpec((tm, tn), lambda i, j: (i, j)),
                   pl.BlockSpec((2 * MLSTM_HEADS, tm), lambda i, j: (0, i))],
        out_shape=[jax.ShapeDtypeStruct((T, PROJ_W), BF16),
                   jax.ShapeDtypeStruct((2 * MLSTM_HEADS, T), F32)],
        scratch_shapes=[pltpu.VMEM((tm, D_MODEL), BF16)],
        compiler_params=pltpu.CompilerParams(
            dimension_semantics=("arbitrary", "arbitrary"), vmem_limit_bytes=VMEM_LIMIT),
        name="norm_in_proj",
    )(x2, mod3, g_pre, w_main, w_if)


def _attn_kernel(q_ref, k_ref, v_ref, cs_ref, sn_ref, o_ref, qf, kf, vf, acc, m_s, l_s, *, seq):
    g = pl.program_id(1)
    lane = lax.broadcasted_iota(jnp.int32, (ATT_BLK, LANES), 1)
    first = (lane % ATT_HEAD_DIM) < ROPE_HALF
    low_head = lane < ATT_HEAD_DIM

    def rope(x, cs, sn):
        partner = jnp.where(first, pltpu.roll(x, LANES - ROPE_HALF, 1), pltpu.roll(x, ROPE_HALF, 1))
        return x * cs + partner * sn

    def zero_pad(i, _):
        rows = pl.ds(pl.multiple_of(i * ATT_BLK, ATT_BLK), ATT_BLK)
        for hp in range(2):
            kf[hp, rows, :] = jnp.zeros((ATT_BLK, LANES), F32)
            vf[hp, rows, :] = jnp.zeros((ATT_BLK, LANES), F32)
        return 0

    lax.fori_loop(0, seq // ATT_BLK, zero_pad, 0)

    def stage(i, _):
        r = pl.multiple_of(i * ATT_BLK, ATT_BLK)
        rows = pl.ds(r, ATT_BLK)
        prow = pl.ds(pl.multiple_of(seq + i * ATT_BLK, ATT_BLK), ATT_BLK)
        cs = cs_ref[0, rows, :]
        sn = sn_ref[0, rows, :]
        for hp in range(2):
            cols = pl.ds(hp * LANES, LANES)
            qf[hp, rows, :] = rope(q_ref[0, rows, cols].astype(F32), cs, sn) * (ATT_HEAD_DIM ** -0.5)
            kf[hp, prow, :] = rope(k_ref[0, rows, cols].astype(F32), cs, sn)
            vf[hp, prow, :] = v_ref[0, rows, cols].astype(F32)
        return 0

    lax.fori_loop(0, seq // ATT_BLK, stage, 0)

    qi = lax.broadcasted_iota(jnp.int32, (ATT_BLK, 2 * ATT_BLK), 0)
    ki = lax.broadcasted_iota(jnp.int32, (ATT_BLK, 2 * ATT_BLK), 1)
    band = (ki >= qi) & (ki <= qi + ATT_BLK)

    def process(d, init):
        span = ATT_BLK * d
        single = seq == span

        def body(cp, _):
            blocks = [cp * ATT_PAIR + i for i in range(ATT_PAIR)]
            qrows, krows, valid = [], [], []
            for c in blocks:
                rho = c % d
                n = c // d
                qstart = rho + n * span
                if single:
                    kstart, nk = seq + qstart, ATT_BLK
                    valid.append(band[:, ATT_BLK:])
                else:
                    kstart, nk = seq + qstart - span, 2 * ATT_BLK
                    valid.append(band & (ki >= jnp.where(n > 0, 0, ATT_BLK)))
                qrows.append(pl.ds(qstart, ATT_BLK, stride=d) if d > 1 else pl.ds(qstart, ATT_BLK))
                krows.append(pl.ds(kstart, nk, stride=d) if d > 1 else pl.ds(kstart, nk))
            units = [(b, hp) for b in range(ATT_PAIR) for hp in range(2)]
            heads = [(u, hh) for u in range(len(units)) for hh in range(2)]
            q2 = [qf[hp, qrows[b], :] for b, hp in units]
            k2 = [kf[hp, krows[b], :].astype(BF16) for b, hp in units]
            v2 = [vf[hp, krows[b], :].astype(BF16) for b, hp in units]
            qh = [jnp.where(low_head if hh == 0 else jnp.logical_not(low_head), q2[u], 0.0).astype(BF16)
                  for u, hh in heads]
            s = [jnp.where(valid[units[u][0]], _nt(qh[i], k2[u]), NEG) for i, (u, hh) in enumerate(heads)]
            m = [jnp.max(x, axis=1, keepdims=True) for x in s]
            p = [jnp.exp(x - mx) for x, mx in zip(s, m)]
            l = [jnp.sum(x, axis=1, keepdims=True) for x in p]
            o = [jnp.dot(p[i].astype(BF16), v2[u], preferred_element_type=F32)
                 for i, (u, hh) in enumerate(heads)]
            for u, (b, hp) in enumerate(units):
                o_b = jnp.where(low_head, o[2 * u], o[2 * u + 1])
                m_b = jnp.where(low_head, m[2 * u], m[2 * u + 1])
                l_b = jnp.where(low_head, l[2 * u], l[2 * u + 1])
                if init:
                    acc[hp, qrows[b], :] = o_b
                    m_s[hp, qrows[b], :] = m_b
                    l_s[hp, qrows[b], :] = l_b
                else:
                    m_old = m_s[hp, qrows[b], :]
                    m_new = jnp.maximum(m_old, m_b)
                    a_old = jnp.exp(m_old - m_new)
                    a_new = jnp.exp(m_b - m_new)
                    acc[hp, qrows[b], :] = acc[hp, qrows[b], :] * a_old + o_b * a_new
                    l_s[hp, qrows[b], :] = l_s[hp, qrows[b], :] * a_old + l_b * a_new
                    m_s[hp, qrows[b], :] = m_new
            return 0

        lax.fori_loop(0, seq // (ATT_BLK * ATT_PAIR), body, 0)

    for gi, (_, d) in enumerate(ATT_GROUPS):
        @pl.when(g == gi)
        def _(d=d, gi=gi):
            process(d, gi == 0)

    @pl.when(g == len(ATT_GROUPS) - 1)
    def _():
        def fin(i, _):
            rows = pl.ds(pl.multiple_of(i * ATT_BLK, ATT_BLK), ATT_BLK)
            for hp in range(2):
                o_ref[0, rows, pl.ds(hp * LANES, LANES)] = (acc[hp, rows, :] / l_s[hp, rows, :]).astype(BF16)
            return 0

        lax.fori_loop(0, seq // ATT_BLK, fin, 0)


def _attention(proj3, cs, sn):
    B, S, _ = proj3.shape
    ng = len(ATT_GROUPS)
    qb, kb, vb = OFF_AQ // ATT_GROUP_W, OFF_AK // ATT_GROUP_W, OFF_AV // ATT_GROUP_W
    return pl.pallas_call(
        functools.partial(_attn_kernel, seq=S),
        grid=(B, ng),
        in_specs=[pl.BlockSpec((1, S, ATT_GROUP_W), lambda b, g: (b, 0, qb + g)),
                  pl.BlockSpec((1, S, ATT_GROUP_W), lambda b, g: (b, 0, kb + g)),
                  pl.BlockSpec((1, S, ATT_GROUP_W), lambda b, g: (b, 0, vb + g)),
                  pl.BlockSpec((1, S, LANES), lambda b, g: (b, 0, 0)),
                  pl.BlockSpec((1, S, LANES), lambda b, g: (b, 0, 0))],
        out_specs=pl.BlockSpec((1, S, ATT_GROUP_W), lambda b, g: (b, 0, 0)),
        out_shape=jax.ShapeDtypeStruct((B, S, ATT_GROUP_W), BF16),
        scratch_shapes=[pltpu.VMEM((2, S, LANES), F32),
                        pltpu.VMEM((2, 2 * S, LANES), F32),
                        pltpu.VMEM((2, 2 * S, LANES), F32),
                        pltpu.VMEM((2, S, LANES), F32),
                        pltpu.VMEM((2, S, LANES), F32),
                        pltpu.VMEM((2, S, LANES), F32)],
        compiler_params=pltpu.CompilerParams(
            dimension_semantics=("arbitrary", "arbitrary"), vmem_limit_bytes=VMEM_LIMIT),
        name="dilated_attention",
    )(proj3, proj3, proj3, cs, sn)


def _log_sigmoid(x):
    return jnp.minimum(x, 0.0) - jnp.log(1.0 + jnp.exp(-jnp.abs(x)))


def _mlstm_kernel(mq_ref, mk_ref, mv_ref, mo_ref, gt_ref, cwq_ref, cwk_ref, cbq_ref, cbk_ref,
                  bg_ref, gm_ref, anchor_a, anchor_b, o_ref, q_s, k_s, va_s, rows_s, acc_s, kv_s,
                  inter_s, emt_s, c_s, *, seq):
    del anchor_a, anchor_b
    h = pl.program_id(1)
    L = MLSTM_BLOCK
    NC = seq // L
    DK, DV = MLSTM_QK_DIM, MLSTM_V_DIM
    DA = DV + LANES
    nshift = CONV_WIDTH - 1

    tt = lax.broadcasted_iota(jnp.int32, (nshift * L, 2 * L), 0)
    uu = lax.broadcasted_iota(jnp.int32, (nshift * L, 2 * L), 1)
    shift_mat = (uu == L + tt % L - (tt // L + 1)).astype(BF16)
    conv_w = jnp.concatenate([cwq_ref[...], cwk_ref[...]], axis=1)
    conv_b = jnp.concatenate([cbq_ref[...], cbk_ref[...]], axis=1)
    prev = jnp.zeros((L, 2 * DK), BF16)
    for i in range(NC):
        blk = slice(i * L, (i + 1) * L)
        va_s[blk, 0:DV] = mv_ref[0, blk, :]
        va_s[blk, DV:DA] = jnp.ones((L, DA - DV), BF16)
        cur = jnp.concatenate([mq_ref[0, blk, :], mk_ref[0, blk, :]], axis=1)
        shifted = jnp.dot(shift_mat, jnp.concatenate([prev, cur], axis=0),
                          preferred_element_type=F32)
        y = conv_b + cur.astype(F32) * conv_w[nshift:nshift + 1, :]
        for s in range(nshift):
            y = y + shifted[s * L:(s + 1) * L, :] * conv_w[nshift - 1 - s:nshift - s, :]
        y = _silu(y)
        q_s[blk, :] = y[:, 0:DK].astype(BF16)
        k_s[blk, :] = (y[:, DK:2 * DK] * (DK ** -0.5)).astype(BF16)
        prev = cur

    lane = lax.broadcasted_iota(jnp.int32, (1, LANES), 1)
    bias = bg_ref[...]
    b_i = jnp.sum(jnp.where(lane == h, bias, 0.0), axis=1, keepdims=True)
    b_f = jnp.sum(jnp.where(lane == h + MLSTM_HEADS, bias, 0.0), axis=1, keepdims=True)
    ri = lax.broadcasted_iota(jnp.int32, (L, L), 0)
    ci = lax.broadcasted_iota(jnp.int32, (L, L), 1)
    causal = ci <= ri
    eye = (ri == ci).astype(F32)
    i_rows = gt_ref[h, 0] + b_i
    lf_rows = _log_sigmoid(gt_ref[h + MLSTM_HEADS, 0] + b_f)
    b_rows = jnp.dot(lf_rows, (ri <= ci).astype(F32), preferred_element_type=F32,
                     precision=HIGHEST)
    b_end = b_rows[:, L - 1:L]
    g_rows = b_end - b_rows + i_rows
    g_max = jnp.max(g_rows, axis=1, keepdims=True)
    m = jnp.zeros((1, 1), F32)
    m_prev, m_new = [], []
    for c in range(NC):
        m_prev.append(m)
        m = jnp.maximum(b_end[c:c + 1, :] + m, g_max[c:c + 1, :])
        m_new.append(m)
    m_prev = jnp.concatenate(m_prev, axis=0)
    m_new = jnp.concatenate(m_new, axis=0)
    rows_s[0] = b_rows
    rows_s[1] = jnp.exp(g_rows - m_new)
    rows_s[2] = b_rows - i_rows
    rows_s[3] = jnp.broadcast_to(m_prev, (NC, L))
    rows_s[4] = jnp.broadcast_to(jnp.exp(b_end + m_prev - m_new), (NC, L))

    r2 = lax.broadcasted_iota(jnp.int32, (2 * L, 2 * L), 0)
    c2 = lax.broadcasted_iota(jnp.int32, (2 * L, 2 * L), 1)
    ones_blk = ((r2 < L) == (c2 < L)).astype(BF16)

    G = MLSTM_GROUP

    def local(cg, _):
        cs = [cg * G + i for i in range(G)]
        rows = [pl.ds(pl.multiple_of(c * L, L), L) for c in cs]
        b_r = [rows_s[0, pl.ds(c, 1), :] for c in cs]
        w_r = [rows_s[1, pl.ds(c, 1), :] for c in cs]
        u_r = [rows_s[2, pl.ds(c, 1), :] for c in cs]
        mp = [rows_s[3, pl.ds(c, 1), :] for c in cs]
        q = [q_s[r, :] for r in rows]
        k = [k_s[r, :] for r in rows]
        va = [va_s[r, :] for r in rows]
        qk = [_nt(a, b) for a, b in zip(q, k)]
        x2 = [jnp.concatenate([eye * a, eye * b], axis=1) for a, b in zip(b_r, w_r)]
        hi = [x.astype(BF16) for x in x2]
        lo = [(x - h_.astype(F32)).astype(BF16) for x, h_ in zip(x2, hi)]
        yb = [jnp.dot(h_, ones_blk, preferred_element_type=F32)
              + jnp.dot(l_, ones_blk, preferred_element_type=F32) for h_, l_ in zip(hi, lo)]
        b_b = [y[:, 0:L] for y in yb]
        w_b = [y[:, L:2 * L] for y in yb]
        for i in range(G):
            kv_s[cs[i]] = _tn((w_b[i] * k[i].astype(F32)).astype(BF16), va[i])
        dmat = [jnp.where(causal, b - u, NEG) for b, u in zip(b_b, u_r)]
        m_t = [jnp.maximum(b + m_, jnp.max(d, axis=1, keepdims=True))
               for b, m_, d in zip(b_b, mp, dmat)]
        sc = [a * jnp.exp(d - m_) for a, d, m_ in zip(qk, dmat, m_t)]
        for i in range(G):
            acc_s[rows[i], :] = jnp.dot(sc[i].astype(BF16), va[i], preferred_element_type=F32)
            inter_s[rows[i], :] = jnp.exp(b_b[i] + mp[i] - m_t[i])
            emt_s[rows[i], :] = jnp.exp(-m_t[i])
        return 0

    lax.fori_loop(0, NC // G, local, 0)

    g_row = gm_ref[...]
    c_s[...] = jnp.zeros((DK, DA), F32)

    def recur(cg, _):
        cs = [cg * G + i for i in range(G)]
        rows = [pl.ds(pl.multiple_of(c * L, L), L) for c in cs]
        states = [c_s[...]]
        for c in cs:
            dec = rows_s[4, pl.ds(c, 1), :]
            states.append(jnp.concatenate([dec, dec, dec], axis=1) * states[-1] + kv_s[c])
        c_s[...] = states[G]
        read = [jnp.dot(q_s[r, :], st.astype(BF16), preferred_element_type=F32)
                for r, st in zip(rows, states)]
        inter = [inter_s[r, :] for r in rows]
        out = [acc_s[r, :] + jnp.concatenate([it, it, it], axis=1) * rd
               for r, it, rd in zip(rows, inter, read)]
        emt = [emt_s[r, :] for r in rows]
        nrm = [jnp.maximum(jnp.abs(jnp.concatenate([o[:, DV:DA], o[:, DV:DA]], axis=1)),
                           jnp.concatenate([e_, e_], axis=1)) for o, e_ in zip(out, emt)]
        hh = [o[:, 0:DV] / n_ for o, n_ in zip(out, nrm)]
        ms = [jnp.mean(x * x, axis=1, keepdims=True) for x in hh]
        hn = [x * lax.rsqrt(m_ + NORM_EPS) * g_row for x, m_ in zip(hh, ms)]
        for i in range(G):
            o_ref[0, rows[i], :] = (hn[i] * _sigmoid(mo_ref[0, rows[i], :].astype(F32))).astype(BF16)
        return 0

    lax.fori_loop(0, NC // G, recur, 0)


def _mlstm(proj3, gates_t, conv_w, conv_b, bg_row, g_mlstm, anchor_a, anchor_b):
    B, S, _ = proj3.shape
    H, DK, DV = MLSTM_HEADS, MLSTM_QK_DIM, MLSTM_V_DIM
    L = MLSTM_BLOCK
    NC = S // L
    DA = DV + LANES
    qb, kb = OFF_MQ // DK, OFF_MK // DK
    vb, ob = OFF_MV // DV, OFF_MO // DV
    nq = H
    return pl.pallas_call(
        functools.partial(_mlstm_kernel, seq=S),
        grid=(B, H),
        in_specs=[pl.BlockSpec((1, S, DK), lambda b, h: (b, 0, qb + h)),
                  pl.BlockSpec((1, S, DK), lambda b, h: (b, 0, kb + h)),
                  pl.BlockSpec((1, S, DV), lambda b, h: (b, 0, vb + h)),
                  pl.BlockSpec((1, S, DV), lambda b, h: (b, 0, ob + h)),
                  pl.BlockSpec((2 * H, 1, NC, L), lambda b, h: (0, b, 0, 0)),
                  pl.BlockSpec((CONV_WIDTH, DK), lambda b, h: (0, h)),
                  pl.BlockSpec((CONV_WIDTH, DK), lambda b, h: (0, nq + h)),
                  pl.BlockSpec((1, DK), lambda b, h: (0, h)),
                  pl.BlockSpec((1, DK), lambda b, h: (0, nq + h)),
                  pl.BlockSpec((1, LANES), lambda b, h: (0, 0)),
                  pl.BlockSpec((1, DV), lambda b, h: (0, h)),
                  pl.BlockSpec(memory_space=pl.ANY), pl.BlockSpec(memory_space=pl.ANY)],
        out_specs=pl.BlockSpec((1, S, DV), lambda b, h: (b, 0, h)),
        out_shape=jax.ShapeDtypeStruct((B, S, H * DV), BF16),
        scratch_shapes=[pltpu.VMEM((S, DK), BF16),
                        pltpu.VMEM((S, DK), BF16),
                        pltpu.VMEM((S, DA), BF16),
                        pltpu.VMEM((5, NC, L), F32),
                        pltpu.VMEM((S, DA), F32),
                        pltpu.VMEM((NC, DK, DA), F32),
                        pltpu.VMEM((S, L), F32),
                        pltpu.VMEM((S, L), F32),
                        pltpu.VMEM((DK, DA), F32)],
        compiler_params=pltpu.CompilerParams(
            dimension_semantics=("arbitrary", "arbitrary"), vmem_limit_bytes=VMEM_LIMIT),
        name="mlstm_chunkwise",
    )(proj3, proj3, proj3, proj3, gates_t, conv_w, conv_w, conv_b, conv_b, bg_row, g_mlstm,
      anchor_a, anchor_b)


def _rms(y, g):
    ms = jnp.mean(y * y, axis=-1, keepdims=True)
    return y * lax.rsqrt(ms + NORM_EPS) * g


def _merge_kernel(ya_ref, yb_ref, ga_ref, gb_ref, x_ref, mod_ref, wa_ref, wb_ref, wo_ref,
                  gpost_ref, gpre_ref, anchor_ref, x1_ref, h2_ref):
    del anchor_ref
    tm = x_ref.shape[0]
    slabs = [pl.ds(s * (tm // MERGE_SPLIT), tm // MERGE_SPLIT) for s in range(MERGE_SPLIT)]
    pa = [jnp.dot(ya_ref[r, :], wa_ref[...], preferred_element_type=F32) for r in slabs]
    pb = [jnp.dot(yb_ref[r, :], wb_ref[...], preferred_element_type=F32) for r in slabs]
    merged = [_sigmoid(ga_ref[r, :].astype(F32)) * a + _sigmoid(gb_ref[r, :].astype(F32)) * b
              for r, a, b in zip(slabs, pa, pb)]
    y = [jnp.dot(m.astype(BF16), wo_ref[...], preferred_element_type=F32) for m in merged]
    x1 = [x_ref[r, :] + mod_ref[0, 2:3, :] * _rms(v, gpost_ref[...]) for r, v in zip(slabs, y)]
    for r, v in zip(slabs, x1):
        x1_ref[r, :] = v
    h2 = [_rms(v, gpre_ref[...]) * (1.0 + mod_ref[0, 4:5, :]) + mod_ref[0, 3:4, :] for v in x1]
    for r, v in zip(slabs, h2):
        h2_ref[r, :] = _pack_pair(v[:, :HALF], v[:, HALF:])


def _merge(ya2, yb2, proj2, x2, mod3, wa, wb, wo, g_post, g_pre, seq, anchor):
    T = x2.shape[0]
    tm = 512 * MERGE_SPLIT
    per_b = seq // tm
    full = lambda shape: pl.BlockSpec(shape, lambda i: (0,) * len(shape))
    return pl.pallas_call(
        _merge_kernel,
        grid=(T // tm,),
        in_specs=[pl.BlockSpec((tm, ATT_GROUP_W), lambda i: (i, 0)),
                  pl.BlockSpec((tm, D_MODEL), lambda i: (i, 0)),
                  pl.BlockSpec((tm, D_MODEL), lambda i: (i, OFF_GA // D_MODEL)),
                  pl.BlockSpec((tm, D_MODEL), lambda i: (i, OFF_GB // D_MODEL)),
                  pl.BlockSpec((tm, D_MODEL), lambda i: (i, 0)),
                  pl.BlockSpec((1, 6, D_MODEL), lambda i: (i // per_b, 0, 0)),
                  full((ATT_GROUP_W, D_MODEL)), full((D_MODEL, D_MODEL)), full((D_MODEL, D_MODEL)),
                  full((1, D_MODEL)), full((1, D_MODEL)),
                  pl.BlockSpec(memory_space=pl.ANY)],
        out_specs=[pl.BlockSpec((tm, D_MODEL), lambda i: (i, 0)),
                   pl.BlockSpec((tm, HALF), lambda i: (i, 0))],
        out_shape=[jax.ShapeDtypeStruct((T, D_MODEL), F32),
                   jax.ShapeDtypeStruct((T, HALF), jnp.uint32)],
        compiler_params=pltpu.CompilerParams(
            dimension_semantics=("arbitrary",), vmem_limit_bytes=VMEM_LIMIT),
        name="merge_out_proj",
    )(ya2, yb2, proj2, proj2, x2, mod3, wa, wb, wo, g_post, g_pre, anchor)


def _router_kernel(h2_ref, rlo_ref, rhi_ref, bias_ref, idx_ref, w_ref, rank_ref, cnt_ref):
    E = N_EXPERTS
    tr = h2_ref.shape[0]
    gsz = E // N_GROUPS

    @pl.when(pl.program_id(0) == 0)
    def _():
        cnt_ref[...] = jnp.zeros(cnt_ref.shape, F32)

    lo, hi = _unpack_pair(h2_ref[...])
    logits = _nt(rlo_ref[...], lo.astype(BF16)) + _nt(rhi_ref[...], hi.astype(BF16))
    scores = _sigmoid(logits)
    sel = scores + bias_ref[:, 0:1]

    gi = lax.broadcasted_iota(jnp.int32, (gsz, tr), 0).astype(F32)
    gs_rows = []
    for g in range(N_GROUPS):
        blk = sel[g * gsz:(g + 1) * gsz, :]
        m1 = jnp.max(blk, axis=0, keepdims=True)
        a1 = jnp.min(jnp.where(blk == m1, gi, float(E)), axis=0, keepdims=True)
        m2 = jnp.max(jnp.where(gi == a1, -jnp.inf, blk), axis=0, keepdims=True)
        gs_rows.append(m1 + m2)
    gs = jnp.concatenate(gs_rows, axis=0)
    g8 = lax.broadcasted_iota(jnp.int32, (N_GROUPS, tr), 0).astype(F32)
    gmask = jnp.zeros((N_GROUPS, tr), F32)
    for _ in range(TOPK_GROUPS):
        m = jnp.max(gs, axis=0, keepdims=True)
        a = jnp.min(jnp.where(gs == m, g8, float(E)), axis=0, keepdims=True)
        hit = g8 == a
        gmask = jnp.where(hit, 1.0, gmask)
        gs = jnp.where(hit, -jnp.inf, gs)
    selm = jnp.concatenate(
        [jnp.where(gmask[g:g + 1, :] > 0.0, sel[g * gsz:(g + 1) * gsz, :], -jnp.inf)
         for g in range(N_GROUPS)], axis=0)

    ei = lax.broadcasted_iota(jnp.int32, (E, tr), 0).astype(F32)
    picks, weights, hits = [], [], []
    candidates = selm
    for _ in range(TOP_K):
        m = jnp.max(selm, axis=0, keepdims=True)
        a = jnp.min(jnp.where(selm == m, ei, float(E)), axis=0, keepdims=True)
        hit = ei == a
        picks.append(a)
        hits.append(hit)
        weights.append(jnp.sum(jnp.where(hit, scores, 0.0), axis=0, keepdims=True))
        selm = jnp.where(hit, -jnp.inf, selm)
    chosen = jnp.where(selm != candidates, 1.0, 0.0)
    wsum = weights[0]
    for w in weights[1:]:
        wsum = wsum + w

    ti = lax.broadcasted_iota(jnp.int32, (tr, tr), 0)
    tj = lax.broadcasted_iota(jnp.int32, (tr, tr), 1)
    before = (ti < tj).astype(BF16)
    pos = jnp.dot(chosen.astype(BF16), before, preferred_element_type=F32) + cnt_ref[:, 0:1]
    ranks = [jnp.sum(jnp.where(hit, pos, 0.0), axis=0, keepdims=True) for hit in hits]
    cnt_ref[...] = cnt_ref[...] + jnp.sum(chosen, axis=1, keepdims=True)

    idx_ref[...] = jnp.concatenate(picks, axis=0).astype(jnp.int32)
    w_ref[...] = jnp.concatenate([w / wsum * ROUTED_SCALE for w in weights], axis=0)
    rank_ref[...] = jnp.concatenate(ranks, axis=0).astype(jnp.int32)


def _router(h2p, r_lo, r_hi, bias_col, row0, T):
    tr = 512
    off = row0 // tr
    full = lambda shape: pl.BlockSpec(shape, lambda i: (0,) * len(shape))
    return pl.pallas_call(
        _router_kernel,
        grid=(T // tr,),
        in_specs=[pl.BlockSpec((tr, HALF), lambda i: (i + off, 0)),
                  full((N_EXPERTS, HALF)), full((N_EXPERTS, HALF)), full((N_EXPERTS, LANES))],
        out_specs=[pl.BlockSpec((TOP_K, tr), lambda i: (0, i)),
                   pl.BlockSpec((TOP_K, tr), lambda i: (0, i)),
                   pl.BlockSpec((TOP_K, tr), lambda i: (0, i)),
                   full((N_EXPERTS, LANES))],
        out_shape=[jax.ShapeDtypeStruct((TOP_K, T), jnp.int32),
                   jax.ShapeDtypeStruct((TOP_K, T), F32),
                   jax.ShapeDtypeStruct((TOP_K, T), jnp.int32),
                   jax.ShapeDtypeStruct((N_EXPERTS, LANES), F32)],
        compiler_params=pltpu.CompilerParams(
            dimension_semantics=("arbitrary",), vmem_limit_bytes=VMEM_LIMIT),
        name="router_topk",
    )(h2p, r_lo, r_hi, bias_col)


def _dest_kernel(idx_ref, rank_ref, pstart_ref, dest_ref):
    tr = idx_ref.shape[1]
    ei = lax.broadcasted_iota(jnp.int32, (N_EXPERTS, tr), 0)
    start = pstart_ref[:, 0:1]
    rows = []
    for k in range(TOP_K):
        hit = ei == idx_ref[k:k + 1, :]
        rows.append(jnp.sum(jnp.where(hit, start, 0.0), axis=0, keepdims=True))
    dest_ref[...] = jnp.concatenate(rows, axis=0).astype(jnp.int32) + rank_ref[...]


def _slot_index(idx, rank, pstart_col):
    T = idx.shape[1]
    tr = 1024
    return pl.pallas_call(
        _dest_kernel,
        grid=(T // tr,),
        in_specs=[pl.BlockSpec((TOP_K, tr), lambda i: (0, i)),
                  pl.BlockSpec((TOP_K, tr), lambda i: (0, i)),
                  pl.BlockSpec((N_EXPERTS, LANES), lambda i: (0, 0))],
        out_specs=pl.BlockSpec((TOP_K, tr), lambda i: (0, i)),
        out_shape=jax.ShapeDtypeStruct((TOP_K, T), jnp.int32),
        name="slot_index",
    )(idx, rank, pstart_col)


def _ffn_kernel(first_ref, nblk_ref, nused_ref, xs_hbm, wg_ref, wu_ref, wd_ref, ys_hbm,
                xbuf, ybuf, in_sem, out_sem, wg_s, wu_s, wd_s):
    e = pl.program_id(0)
    bm = EXPERT_BLOCK
    ns = EXPERT_SLOTS
    nused = nused_ref[0]
    first = first_ref[e]
    n = nblk_ref[e]

    def in_copy(g):
        slot = g % ns
        return pltpu.make_async_copy(xs_hbm.at[pl.ds(g * bm, bm)], xbuf.at[slot], in_sem.at[slot])

    def out_copy(g):
        slot = g % ns
        return pltpu.make_async_copy(ybuf.at[slot], ys_hbm.at[pl.ds(g * bm, bm)], out_sem.at[slot])

    def fetch(g):
        @pl.when(g < nused)
        def _():
            in_copy(g).start()

    def release(g):
        @pl.when(g >= ns)
        def _():
            out_copy(g - ns).wait()

    def ffn(g):
        lo, hi = _unpack_pair(xbuf[g % ns])
        x = jnp.concatenate([lo.astype(BF16), hi.astype(BF16)], axis=1)
        gate = jnp.dot(x, wg_s[...], preferred_element_type=F32)
        up = jnp.dot(x, wu_s[...], preferred_element_type=F32)
        hid = (_silu(gate) * up).astype(BF16)
        return jnp.dot(hid, wd_s[...], preferred_element_type=F32)

    def pack(g, out):
        ybuf[g % ns] = _pack_pair(out[:, :HALF], out[:, HALF:])

    @pl.when(e == 0)
    def _():
        for q in range(ns - 1):
            fetch(q)

    @pl.when(n > 0)
    def _():
        for packed, dst in ((wg_ref, wg_s), (wu_ref, wu_s), (wd_ref, wd_s)):
            lo, hi = _unpack_pair(packed[0])
            half = dst.shape[0] // 2
            dst[0:half, :] = lo.astype(BF16)
            dst[half:, :] = hi.astype(BF16)

        def two_blocks(j, _):
            g = first + 2 * j
            in_copy(g).wait()
            in_copy(g + 1).wait()
            fetch(g + ns - 1)
            release(g)
            release(g + 1)
            out_a = ffn(g)
            out_b = ffn(g + 1)
            pack(g, out_a)
            pack(g + 1, out_b)
            out_copy(g).start()
            out_copy(g + 1).start()
            fetch(g + ns)
            return 0

        lax.fori_loop(0, n // 2, two_blocks, 0)

        @pl.when(n % 2 == 1)
        def _():
            g = first + n - 1
            in_copy(g).wait()
            fetch(g + ns - 1)
            release(g)
            pack(g, ffn(g))
            out_copy(g).start()

    @pl.when(e == pl.num_programs(0) - 1)
    def _():
        for q in range(ns, 0, -1):
            @pl.when(nused >= q)
            def _(q=q):
                out_copy(nused - q).wait()


def _expert_ffn(first_blk, nblk, nused, xs, w_gate, w_up, w_down):
    P = xs.shape[0]
    bm = EXPERT_BLOCK
    w_map = lambda e, *_: (e, 0, 0)
    grid_spec = pltpu.PrefetchScalarGridSpec(
        num_scalar_prefetch=3,
        grid=(w_gate.shape[0],),
        in_specs=[pl.BlockSpec(memory_space=pl.ANY),
                  pl.BlockSpec((1, D_MODEL // 2, EXPERT_FF), w_map),
                  pl.BlockSpec((1, D_MODEL // 2, EXPERT_FF), w_map),
                  pl.BlockSpec((1, EXPERT_FF // 2, D_MODEL), w_map)],
        out_specs=pl.BlockSpec(memory_space=pl.ANY),
        scratch_shapes=[pltpu.VMEM((EXPERT_SLOTS, bm, HALF), jnp.uint32),
                        pltpu.VMEM((EXPERT_SLOTS, bm, HALF), jnp.uint32),
                        pltpu.SemaphoreType.DMA((EXPERT_SLOTS,)),
                        pltpu.SemaphoreType.DMA((EXPERT_SLOTS,)),
                        pltpu.VMEM((D_MODEL, EXPERT_FF), BF16),
                        pltpu.VMEM((D_MODEL, EXPERT_FF), BF16),
                        pltpu.VMEM((EXPERT_FF, D_MODEL), BF16)],
    )
    return pl.pallas_call(
        _ffn_kernel,
        grid_spec=grid_spec,
        out_shape=jax.ShapeDtypeStruct((P, HALF), jnp.uint32),
        compiler_params=pltpu.CompilerParams(
            dimension_semantics=("arbitrary",), vmem_limit_bytes=VMEM_LIMIT),
        name="routed_experts",
    )(first_blk, nblk, nused, xs, w_gate, w_up, w_down)


def _final_kernel(yg_ref, w_ref, h2_ref, x1_ref, mod_ref, wsg_ref, wsu_ref, wsd_ref, gpost_ref, *rest):
    o_ref = rest[-1]
    lo, hi = _unpack_pair(h2_ref[...])
    h2 = jnp.concatenate([lo.astype(BF16), hi.astype(BF16)], axis=1)
    gate = jnp.dot(h2, wsg_ref[...], preferred_element_type=F32)
    up = jnp.dot(h2, wsu_ref[...], preferred_element_type=F32)
    shared = jnp.dot((_silu(gate) * up).astype(BF16), wsd_ref[...], preferred_element_type=F32)
    y_lo = shared[:, :HALF]
    y_hi = shared[:, HALF:]
    for k in range(TOP_K):
        r_lo, r_hi = _unpack_pair(yg_ref[k])
        wk = w_ref[:, k:k + 1]
        y_lo = y_lo + wk * r_lo
        y_hi = y_hi + wk * r_hi
    ms = (jnp.sum(y_lo * y_lo, axis=-1, keepdims=True)
          + jnp.sum(y_hi * y_hi, axis=-1, keepdims=True)) * (1.0 / D_MODEL)
    inv = lax.rsqrt(ms + NORM_EPS)
    o_ref[:, 0:HALF] = x1_ref[:, 0:HALF] + mod_ref[0, 5:6, 0:HALF] * (y_lo * inv * gpost_ref[:, 0:HALF])
    o_ref[:, HALF:] = x1_ref[:, HALF:] + mod_ref[0, 5:6, HALF:] * (y_hi * inv * gpost_ref[:, HALF:])


def _final(yg, w_tk, h2p, x1, mod3, wsg, wsu, wsd, g_post, seq, row0, out_prev):
    T = x1.shape[0]
    tp = yg.shape[1]
    tm = 512
    per_b = seq // tm
    off = row0 // tm
    full = lambda shape: pl.BlockSpec(shape, lambda i: (0,) * len(shape))
    in_specs = [pl.BlockSpec((TOP_K, tm, HALF), lambda i: (0, i, 0)),
                pl.BlockSpec((tm, TOP_K), lambda i: (i, 0)),
                pl.BlockSpec((tm, HALF), lambda i: (i + off, 0)),
                pl.BlockSpec((tm, D_MODEL), lambda i: (i + off, 0)),
                pl.BlockSpec((1, 6, D_MODEL), lambda i: ((i + off) // per_b, 0, 0)),
                full((D_MODEL, EXPERT_FF)), full((D_MODEL, EXPERT_FF)), full((EXPERT_FF, D_MODEL)),
                full((1, D_MODEL))]
    args = [yg, w_tk, h2p, x1, mod3, wsg, wsu, wsd, g_post]
    aliases = {}
    if out_prev is not None:
        in_specs.append(pl.BlockSpec(memory_space=pl.ANY))
        args.append(out_prev)
        aliases = {len(args) - 1: 0}
    return pl.pallas_call(
        _final_kernel,
        grid=(tp // tm,),
        in_specs=in_specs,
        out_specs=pl.BlockSpec((tm, D_MODEL), lambda i: (i + off, 0)),
        out_shape=jax.ShapeDtypeStruct((T, D_MODEL), F32),
        input_output_aliases=aliases,
        compiler_params=pltpu.CompilerParams(
            dimension_semantics=("arbitrary",), vmem_limit_bytes=VMEM_LIMIT),
        name="shared_expert_combine",
    )(*args)


def _rope_tables(positions):
    inv = jnp.power(ROPE_THETA, -jnp.arange(ROPE_HALF, dtype=F32) / ROPE_HALF)
    ang = positions.astype(F32)[..., None] * inv
    cos, sin = jnp.cos(ang), jnp.sin(ang)
    rest = ATT_HEAD_DIM - 2 * ROPE_HALF
    cs = jnp.concatenate([cos, cos, jnp.ones(ang.shape[:-1] + (rest,), F32)], axis=-1)
    sn = jnp.concatenate([-sin, sin, jnp.zeros(ang.shape[:-1] + (rest,), F32)], axis=-1)
    return jnp.tile(cs, (1, 1, 2)), jnp.tile(sn, (1, 1, 2))


def _layer(x, c, positions, w_ada, b_ada, g_pre_mix, g_post_mix, g_pre_ffn, g_post_ffn,
           w_in, conv_w, conv_b, b_gates, g_mlstm, w_branch_a, w_branch_b, w_out,
           router_w, router_bias, w_exp_gate, w_exp_up, w_exp_down, w_sh_gate, w_sh_up, w_sh_down):
    B, S, D = x.shape
    T = B * S
    H = MLSTM_HEADS
    x2 = x.reshape(T, D)

    mod3 = _adaln(c, w_ada, b_ada).reshape(B, 6, D)

    a_w = 3 * ATT_GROUP_W
    o_mq = 3 * a_w
    o_mk = o_mq + H * MLSTM_QK_DIM
    o_mv = o_mk + H * MLSTM_QK_DIM
    o_mo = o_mv + H * MLSTM_V_DIM
    o_mi = o_mo + H * MLSTM_V_DIM
    o_ga = o_mi + 2 * H
    o_gb = o_ga + D
    w_bf = w_in.astype(BF16)
    w_main = jnp.concatenate(
        [w_bf[:, o_mv:o_mi], w_bf[:, o_ga:o_gb + D], w_bf[:, o_mq:o_mv], w_bf[:, 0:o_mq]], axis=1)
    w_if = w_bf[:, o_mi:o_ga].T

    proj, gates = _in_proj(x2, mod3, g_pre_mix.reshape(1, D), w_main, w_if, S)
    proj3 = proj.reshape(B, S, PROJ_W)

    wg_p, wu_p, wd_p = (_pack_weight_rows(w, gates) for w in (w_exp_gate, w_exp_up, w_exp_down))

    cs, sn = _rope_tables(positions)
    y_a = _attention(proj3, cs, sn)

    bg_row = jnp.pad(b_gates.reshape(1, 2 * H), ((0, 0), (0, LANES - 2 * H)))
    gates_t = gates.reshape(2 * H, B, S // MLSTM_BLOCK, MLSTM_BLOCK)
    y_b = _mlstm(proj3, gates_t, conv_w, conv_b.reshape(1, -1), bg_row, g_mlstm.reshape(1, -1),
                 wg_p, wu_p)

    x1, h2p = _merge(y_a.reshape(T, ATT_GROUP_W), y_b.reshape(T, D), proj, x2, mod3,
                     w_branch_a.astype(BF16), w_branch_b.astype(BF16), w_out.astype(BF16),
                     g_post_mix.reshape(1, D), g_pre_ffn.reshape(1, D), S, wd_p)

    rw_t = router_w.T.astype(BF16)
    bias_col = jnp.broadcast_to(router_bias.reshape(N_EXPERTS, 1), (N_EXPERTS, LANES))
    wsg, wsu, wsd = w_sh_gate.astype(BF16), w_sh_up.astype(BF16), w_sh_down.astype(BF16)

    tp = T // MOE_PARTS
    bm = EXPERT_BLOCK
    nb = (tp * TOP_K) // bm + N_EXPERTS
    out = None
    for part in range(MOE_PARTS):
        row0 = part * tp
        idx, wts, rank, cnt = _router(h2p, rw_t[:, :HALF], rw_t[:, HALF:], bias_col, row0, tp)

        counts = cnt[:, 0].astype(jnp.int32)
        padded = (counts + bm - 1) // bm * bm
        pend = jnp.cumsum(padded)
        pstart = pend - padded
        pstart_col = jnp.broadcast_to(pstart.astype(F32).reshape(N_EXPERTS, 1), (N_EXPERTS, LANES))
        dest = _slot_index(idx, rank, pstart_col)
        nused = (pend[-1] // bm).astype(jnp.int32).reshape(1)

        xs = _dispatch(h2p, dest, nb * bm, row0)
        ys = _expert_ffn((pstart // bm).astype(jnp.int32), (padded // bm).astype(jnp.int32), nused,
                         xs, wg_p, wu_p, wd_p)
        yg = _collect(ys, dest)
        out = _final(yg, wts.T, h2p, x1, mod3, wsg, wsu, wsd, g_post_ffn.reshape(1, D), S, row0, out)
    return out.reshape(B, S, D)


SC_CORES = 2
SC_SUBCORES = 16
SC_WORKERS = SC_CORES * SC_SUBCORES
SC_ROWS = 64


def _sc_mesh():
    return plsc.VectorSubcoreMesh(core_axis_name="c", subcore_axis_name="s",
                                  num_cores=SC_CORES, num_subcores=SC_SUBCORES)


def _worker_id():
    return lax.axis_index("s") * SC_CORES + lax.axis_index("c")


def _dispatch(h2p, dest, n_slots, row0):
    T = dest.shape[1]
    per_w = T // SC_WORKERS
    nch = per_w // SC_ROWS
    idx = dest.reshape(TOP_K, SC_WORKERS, nch, SC_ROWS).transpose(1, 2, 0, 3)
    idx = idx.reshape(SC_WORKERS, nch * TOP_K, SC_ROWS)

    def body(x_hbm, idx_hbm, xs_hbm, idx_v, buf0, buf1, rsem0, rsem1, ssem0, ssem1):
        wid = _worker_id()
        base = row0 + wid * per_w
        pltpu.sync_copy(idx_hbm.at[wid], idx_v)
        bufs = ((buf0, rsem0, ssem0), (buf1, rsem1, ssem1))

        def read(c, buf, rsem):
            return pltpu.make_async_copy(x_hbm.at[pl.ds(base + c * SC_ROWS, SC_ROWS)], buf, rsem)

        def scatter(c, k, buf, ssem):
            return pltpu.make_async_copy(buf, xs_hbm.at[idx_v.at[c * TOP_K + k]], ssem)

        read(0, buf0, rsem0).start()

        @pl.loop(0, nch, step=2)
        def _(c0):
            for b in range(2):
                c = c0 + b
                buf, rsem, ssem = bufs[b]
                obuf, orsem, ossem = bufs[1 - b]
                read(c, buf, rsem).wait()

                @pl.when(c > 0)
                def _():
                    for k in range(TOP_K):
                        scatter(c - 1, k, obuf, ossem).wait()

                @pl.when(c + 1 < nch)
                def _():
                    read(c + 1, obuf, orsem).start()

                for k in range(TOP_K):
                    scatter(c, k, buf, ssem).start()

        for k in range(TOP_K):
            scatter(nch - 1, k, buf1, ssem1).wait()

    run = pl.kernel(
        body,
        out_type=jax.ShapeDtypeStruct((n_slots, HALF), jnp.uint32),
        mesh=_sc_mesh(),
        scratch_types=[pltpu.VMEM((nch * TOP_K, SC_ROWS), jnp.int32),
                       pltpu.VMEM((SC_ROWS, HALF), jnp.uint32),
                       pltpu.VMEM((SC_ROWS, HALF), jnp.uint32),
                       pltpu.SemaphoreType.DMA, pltpu.SemaphoreType.DMA,
                       pltpu.SemaphoreType.DMA, pltpu.SemaphoreType.DMA],
        name="sc_dispatch",
    )
    return run(h2p, idx)


SC_PACK_ROWS = 64
SC_PACK_COLS = 256
SC_LANES = 16


def _pack_weight_rows(w, after):
    E, R, C = w.shape
    hb = R // 2 // SC_PACK_ROWS
    w2 = w.reshape(E * R, C)

    def body(w_hbm, after_hbm, out_hbm):
        del after_hbm

        def block(lo_v, hi_v, out_v):
            @pl.loop(0, SC_PACK_ROWS)
            def _(r):
                @pl.loop(0, SC_PACK_COLS, step=SC_LANES)
                def _(c):
                    cols = pl.ds(c, SC_LANES)
                    pair = plsc.pack(lo_v[r, cols], hi_v[r, cols], format=plsc.PackFormat.INTERLEAVED)
                    out_v[r, cols] = plsc.bitcast(pair, jnp.uint32)

        blk = (SC_PACK_ROWS, SC_PACK_COLS)
        pltpu.emit_pipeline(
            block,
            grid=(E * hb, C // SC_PACK_COLS),
            in_specs=[pl.BlockSpec(blk, lambda i, j: ((i // hb) * 2 * hb + i % hb, j)),
                      pl.BlockSpec(blk, lambda i, j: ((i // hb) * 2 * hb + hb + i % hb, j))],
            out_specs=[pl.BlockSpec(blk, lambda i, j: (i, j))],
            core_axis_name=("c", "s"),
            dimension_semantics=(pltpu.PARALLEL, pltpu.PARALLEL),
        )(w_hbm, w_hbm, out_hbm)

    run = pl.kernel(body, out_type=jax.ShapeDtypeStruct((E * R // 2, C), jnp.uint32),
                    mesh=_sc_mesh(), scratch_types=[], name="sc_pack_weights",
                    compiler_params=pltpu.CompilerParams(needs_layout_passes=False))
    return run(w2, after).reshape(E, R // 2, C)


def _collect(ys, dest):
    n = dest.size
    per_w = n // SC_WORKERS
    nch = per_w // SC_ROWS
    idx = dest.reshape(SC_WORKERS, nch, SC_ROWS)

    def body(ys_hbm, idx_hbm, out_hbm, idx_v, buf0, buf1, gsem0, gsem1, wsem0, wsem1):
        wid = _worker_id()
        base = wid * per_w
        pltpu.sync_copy(idx_hbm.at[wid], idx_v)
        bufs = ((buf0, gsem0, wsem0), (buf1, gsem1, wsem1))

        def gather(c, buf, gsem):
            return pltpu.make_async_copy(ys_hbm.at[idx_v.at[c]], buf, gsem)

        def write(c, buf, wsem):
            return pltpu.make_async_copy(buf, out_hbm.at[pl.ds(base + c * SC_ROWS, SC_ROWS)], wsem)

        gather(0, buf0, gsem0).start()

        @pl.loop(0, nch, step=2)
        def _(c0):
            for b in range(2):
                c = c0 + b
                buf, gsem, wsem = bufs[b]
                obuf, ogsem, owsem = bufs[1 - b]
                gather(c, buf, gsem).wait()

                @pl.when(c > 0)
                def _():
                    write(c - 1, obuf, owsem).wait()

                @pl.when(c + 1 < nch)
                def _():
                    gather(c + 1, obuf, ogsem).start()

                write(c, buf, wsem).start()

        write(nch - 1, buf1, wsem1).wait()

    run = pl.kernel(
        body,
        out_type=jax.ShapeDtypeStruct((n, HALF), jnp.uint32),
        mesh=_sc_mesh(),
        scratch_types=[pltpu.VMEM((nch, SC_ROWS), jnp.int32),
                       pltpu.VMEM((SC_ROWS, HALF), jnp.uint32),
                       pltpu.VMEM((SC_ROWS, HALF), jnp.uint32),
                       pltpu.SemaphoreType.DMA, pltpu.SemaphoreType.DMA,
                       pltpu.SemaphoreType.DMA, pltpu.SemaphoreType.DMA],
        name="sc_collect",
    )
    return run(ys, idx).reshape(dest.shape + (HALF,))


def kernel(x, c, positions, w_ada, b_ada, g_pre_mix, g_post_mix, g_pre_ffn, g_post_ffn, w_in, conv_w, conv_b, b_gates, g_mlstm, w_branch_a, w_branch_b, w_out, router_w, router_bias, w_exp_gate, w_exp_up, w_exp_down, w_sh_gate, w_sh_up, w_sh_down):
    depth = w_ada.shape[0]
    for l in range(depth):
        x = _layer(x, c, positions, w_ada[l], b_ada[l], g_pre_mix[l], g_post_mix[l], g_pre_ffn[l],
                   g_post_ffn[l], w_in[l], conv_w[l], conv_b[l], b_gates[l], g_mlstm[l],
                   w_branch_a[l], w_branch_b[l], w_out[l], router_w[l], router_bias[l],
                   w_exp_gate[l], w_exp_up[l], w_exp_down[l], w_sh_gate[l], w_sh_up[l], w_sh_down[l])
    return x
```

```python
import functools

import jax
import jax.numpy as jnp
from jax import lax
from jax.experimental import pallas as pl
from jax.experimental.pallas import tpu as pltpu
from jax.experimental.pallas import tpu_sc as plsc

F32 = jnp.float32
BF16 = jnp.bfloat16
HIGHEST = lax.Precision.HIGHEST
LANES = 128

D_MODEL = 1024
ATT_GROUPS = ((128, 1), (512, 4), (2048, 16))
ATT_HEAD_DIM = 64
ATT_GROUP_W = 256
ATT_BLK = 128
ATT_PAIR = 2
ROPE_THETA = 500000.0
ROPE_HALF = 8
MLSTM_HEADS = 4
MLSTM_QK_DIM = 128
MLSTM_V_DIM = 256
MLSTM_BLOCK = 128
MLSTM_GROUP = 16
CONV_WIDTH = 4
N_EXPERTS = 256
TOP_K = 8
N_GROUPS = 8
TOPK_GROUPS = 4
EXPERT_FF = 256
ROUTED_SCALE = 2.5
NORM_EPS = 1e-6
NEG = -1e30

OFF_MV = 0
OFF_MO = OFF_MV + MLSTM_HEADS * MLSTM_V_DIM
OFF_GA = OFF_MO + MLSTM_HEADS * MLSTM_V_DIM
OFF_GB = OFF_GA + D_MODEL
OFF_MQ = OFF_GB + D_MODEL
OFF_MK = OFF_MQ + MLSTM_HEADS * MLSTM_QK_DIM
OFF_AQ = OFF_MK + MLSTM_HEADS * MLSTM_QK_DIM
OFF_AK = OFF_AQ + len(ATT_GROUPS) * ATT_GROUP_W
OFF_AV = OFF_AK + len(ATT_GROUPS) * ATT_GROUP_W
PROJ_W = OFF_AV + len(ATT_GROUPS) * ATT_GROUP_W
HALF = D_MODEL // 2

EXPERT_BLOCK = 512
EXPERT_SLOTS = 6
MOE_PARTS = 2
MERGE_SPLIT = 2
VMEM_LIMIT = 56 * 1024 * 1024


def _nt(a, b):
    return lax.dot_general(a, b, (((1,), (1,)), ((), ())), preferred_element_type=F32)


def _tn(a, b):
    return lax.dot_general(a, b, (((0,), (0,)), ((), ())), preferred_element_type=F32)


_sigmoid = jax.nn.sigmoid


def _silu(x):
    return x * _sigmoid(x)


def _pack_pair(lo, hi):
    lo_b = pltpu.bitcast(lo.astype(BF16).astype(F32), jnp.uint32)
    hi_b = pltpu.bitcast(hi.astype(BF16).astype(F32), jnp.uint32)
    return (lo_b >> 16) | (hi_b & jnp.uint32(0xFFFF0000))


def _unpack_pair(w):
    lo = pltpu.bitcast(w << 16, F32)
    hi = pltpu.bitcast(w & jnp.uint32(0xFFFF0000), F32)
    return lo, hi


def _mod_kernel(c_ref, w_ref, b_ref, o_ref):
    a = _silu(c_ref[...])
    o_ref[...] = jnp.dot(a, w_ref[...], preferred_element_type=F32, precision=HIGHEST) + b_ref[...]


def _adaln(c, w_ada, b_ada):
    B = c.shape[0]
    n = w_ada.shape[1]
    tn = 512
    return pl.pallas_call(
        _mod_kernel,
        grid=(n // tn,),
        in_specs=[pl.BlockSpec((B, D_MODEL), lambda j: (0, 0)),
                  pl.BlockSpec((D_MODEL, tn), lambda j: (0, j)),
                  pl.BlockSpec((1, tn), lambda j: (0, j))],
        out_specs=pl.BlockSpec((B, tn), lambda j: (0, j)),
        out_shape=jax.ShapeDtypeStruct((B, n), F32),
        name="adaln_mod",
    )(c, w_ada, b_ada.reshape(1, n))


def _proj_kernel(x_ref, mod_ref, g_ref, w_ref, wif_ref, o_ref, gates_ref, h_ref):
    @pl.when(pl.program_id(1) == 0)
    def _():
        x = x_ref[...]
        ms = jnp.mean(x * x, axis=-1, keepdims=True)
        y = x * lax.rsqrt(ms + NORM_EPS) * g_ref[...]
        h = (y * (1.0 + mod_ref[0, 1:2, :]) + mod_ref[0, 0:1, :]).astype(BF16)
        h_ref[...] = h
        gates_ref[...] = _nt(wif_ref[...], h)

    o_ref[...] = jnp.dot(h_ref[...], w_ref[...], preferred_element_type=F32).astype(BF16)


def _in_proj(x2, mod3, g_pre, w_main, w_if, seq):
    T = x2.shape[0]
    tm, tn = 1024, PROJ_W // 2
    per_b = seq // tm
    return pl.pallas_call(
        _proj_kernel,
        grid=(T // tm, PROJ_W // tn),
        in_specs=[pl.BlockSpec((tm, D_MODEL), lambda i, j: (i, 0)),
                  pl.BlockSpec((1, 6, D_MODEL), lambda i, j: (i // per_b, 0, 0)),
                  pl.BlockSpec((1, D_MODEL), lambda i, j: (0, 0)),
                  pl.BlockSpec((D_MODEL, tn), lambda i, j: (0, j)),
                  pl.BlockSpec((2 * MLSTM_HEADS, D_MODEL), lambda i, j: (0, 0))],
        out_specs=[pl.BlockSpec((tm, tn), lambda i, j: (i, j)),
                   pl.BlockSpec((2 * MLSTM_HEADS, tm), lambda i, j: (0, i))],
        out_shape=[jax.ShapeDtypeStruct((T, PROJ_W), BF16),
                   jax.ShapeDtypeStruct((2 * MLSTM_HEADS, T), F32)],
        scratch_shapes=[pltpu.VMEM((tm, D_MODEL), BF16)],
        compiler_params=pltpu.CompilerParams(
            dimension_semantics=("arbitrary", "arbitrary"), vmem_limit_bytes=VMEM_LIMIT),
        name="norm_in_proj",
    )(x2, mod3, g_pre, w_main, w_if)


def _attn_kernel(q_ref, k_ref, v_ref, cs_ref, sn_ref, o_ref, qf, kf, vf, acc, m_s, l_s, *, seq):
    g = pl.program_id(1)
    lane = lax.broadcasted_iota(jnp.int32, (ATT_BLK, LANES), 1)
    first = (lane % ATT_HEAD_DIM) < ROPE_HALF
    low_head = lane < ATT_HEAD_DIM

    def rope(x, cs, sn):
        partner = jnp.where(first, pltpu.roll(x, LANES - ROPE_HALF, 1), pltpu.roll(x, ROPE_HALF, 1))
        return x * cs + partner * sn

    def zero_pad(i, _):
        rows = pl.ds(pl.multiple_of(i * ATT_BLK, ATT_BLK), ATT_BLK)
        for hp in range(2):
            kf[hp, rows, :] = jnp.zeros((ATT_BLK, LANES), F32)
            vf[hp, rows, :] = jnp.zeros((ATT_BLK, LANES), F32)
        return 0

    lax.fori_loop(0, seq // ATT_BLK, zero_pad, 0)

    def stage(i, _):
        r = pl.multiple_of(i * ATT_BLK, ATT_BLK)
        rows = pl.ds(r, ATT_BLK)
        prow = pl.ds(pl.multiple_of(seq + i * ATT_BLK, ATT_BLK), ATT_BLK)
        cs = cs_ref[0, rows, :]
        sn = sn_ref[0, rows, :]
        for hp in range(2):
            cols = pl.ds(hp * LANES, LANES)
            qf[hp, rows, :] = rope(q_ref[0, rows, cols].astype(F32), cs, sn) * (ATT_HEAD_DIM ** -0.5)
            kf[hp, prow, :] = rope(k_ref[0, rows, cols].astype(F32), cs, sn)
            vf[hp, prow, :] = v_ref[0, rows, cols].astype(F32)
        return 0

    lax.fori_loop(0, seq // ATT_BLK, stage, 0, unroll=4)

    qi = lax.broadcasted_iota(jnp.int32, (ATT_BLK, 2 * ATT_BLK), 0)
    ki = lax.broadcasted_iota(jnp.int32, (ATT_BLK, 2 * ATT_BLK), 1)
    band = (ki >= qi) & (ki <= qi + ATT_BLK)

    def process(d, init):
        span = ATT_BLK * d
        single = seq == span

        def body(cp, _):
            blocks = [cp * ATT_PAIR + i for i in range(ATT_PAIR)]
            qrows, krows, valid = [], [], []
            for c in blocks:
                rho = c % d
                n = c // d
                qstart = rho + n * span
                if single:
                    kstart, nk = seq + qstart, ATT_BLK
                    valid.append(band[:, ATT_BLK:])
                else:
                    kstart, nk = seq + qstart - span, 2 * ATT_BLK
                    valid.append(band & (ki >= jnp.where(n > 0, 0, ATT_BLK)))
                qrows.append(pl.ds(qstart, ATT_BLK, stride=d) if d > 1 else pl.ds(qstart, ATT_BLK))
                krows.append(pl.ds(kstart, nk, stride=d) if d > 1 else pl.ds(kstart, nk))
            units = [(b, hp) for b in range(ATT_PAIR) for hp in range(2)]
            heads = [(u, hh) for u in range(len(units)) for hh in range(2)]
            q2 = [qf[hp, qrows[b], :] for b, hp in units]
            k2 = [kf[hp, krows[b], :].astype(BF16) for b, hp in units]
            v2 = [vf[hp, krows[b], :].astype(BF16) for b, hp in units]
            qh = [jnp.where(low_head if hh == 0 else jnp.logical_not(low_head), q2[u], 0.0).astype(BF16)
                  for u, hh in heads]
            s = [jnp.where(valid[units[u][0]], _nt(qh[i], k2[u]), NEG) for i, (u, hh) in enumerate(heads)]
            m = [jnp.max(x, axis=1, keepdims=True) for x in s]
            p = [jnp.exp(x - mx) for x, mx in zip(s, m)]
            l = [jnp.sum(x, axis=1, keepdims=True) for x in p]
            o = [jnp.dot(p[i].astype(BF16), v2[u], preferred_element_type=F32)
                 for i, (u, hh) in enumerate(heads)]
            for u, (b, hp) in enumerate(units):
                o_b = jnp.where(low_head, o[2 * u], o[2 * u + 1])
                m_b = jnp.where(low_head, m[2 * u], m[2 * u + 1])
                l_b = jnp.where(low_head, l[2 * u], l[2 * u + 1])
                if init:
                    acc[hp, qrows[b], :] = o_b
                    m_s[hp, qrows[b], :] = m_b
                    l_s[hp, qrows[b], :] = l_b
                else:
                    m_old = m_s[hp, qrows[b], :]
                    m_new = jnp.maximum(m_old, m_b)
                    a_old = jnp.exp(m_old - m_new)
                    a_new = jnp.exp(m_b - m_new)
                    acc[hp, qrows[b], :] = acc[hp, qrows[b], :] * a_old + o_b * a_new
                    l_s[hp, qrows[b], :] = l_s[hp, qrows[b], :] * a_old + l_b * a_new
                    m_s[hp, qrows[b], :] = m_new
            return 0

        lax.fori_loop(0, seq // (ATT_BLK * ATT_PAIR), body, 0)

    for gi, (_, d) in enumerate(ATT_GROUPS):
        @pl.when(g == gi)
        def _(d=d, gi=gi):
            process(d, gi == 0)

    @pl.when(g == len(ATT_GROUPS) - 1)
    def _():
        def fin(i, _):
            rows = pl.ds(pl.multiple_of(i * ATT_BLK, ATT_BLK), ATT_BLK)
            for hp in range(2):
                o_ref[0, rows, pl.ds(hp * LANES, LANES)] = (acc[hp, rows, :] / l_s[hp, rows, :]).astype(BF16)
            return 0

        lax.fori_loop(0, seq // ATT_BLK, fin, 0)


def _attention(proj3, cs, sn):
    B, S, _ = proj3.shape
    ng = len(ATT_GROUPS)
    qb, kb, vb = OFF_AQ // ATT_GROUP_W, OFF_AK // ATT_GROUP_W, OFF_AV // ATT_GROUP_W
    return pl.pallas_call(
        functools.partial(_attn_kernel, seq=S),
        grid=(B, ng),
        in_specs=[pl.BlockSpec((1, S, ATT_GROUP_W), lambda b, g: (b, 0, qb + g)),
                  pl.BlockSpec((1, S, ATT_GROUP_W), lambda b, g: (b, 0, kb + g)),
                  pl.BlockSpec((1, S, ATT_GROUP_W), lambda b, g: (b, 0, vb + g)),
                  pl.BlockSpec((1, S, LANES), lambda b, g: (b, 0, 0)),
                  pl.BlockSpec((1, S, LANES), lambda b, g: (b, 0, 0))],
        out_specs=pl.BlockSpec((1, S, ATT_GROUP_W), lambda b, g: (b, 0, 0)),
        out_shape=jax.ShapeDtypeStruct((B, S, ATT_GROUP_W), BF16),
        scratch_shapes=[pltpu.VMEM((2, S, LANES), F32),
                        pltpu.VMEM((2, 2 * S, LANES), F32),
                        pltpu.VMEM((2, 2 * S, LANES), F32),
                        pltpu.VMEM((2, S, LANES), F32),
                        pltpu.VMEM((2, S, LANES), F32),
                        pltpu.VMEM((2, S, LANES), F32)],
        compiler_params=pltpu.CompilerParams(
            dimension_semantics=("arbitrary", "arbitrary"), vmem_limit_bytes=VMEM_LIMIT),
        name="dilated_attention",
    )(proj3, proj3, proj3, cs, sn)


def _log_sigmoid(x):
    return jnp.minimum(x, 0.0) - jnp.log(1.0 + jnp.exp(-jnp.abs(x)))


def _mlstm_kernel(mq_ref, mk_ref, mv_ref, mo_ref, gt_ref, cwq_ref, cwk_ref, cbq_ref, cbk_ref,
                  bg_ref, gm_ref, anchor_a, anchor_b, o_ref, q_s, k_s, va_s, rows_s, acc_s, kv_s,
                  inter_s, emt_s, c_s, *, seq):
    del anchor_a, anchor_b
    h = pl.program_id(1)
    L = MLSTM_BLOCK
    NC = seq // L
    DK, DV = MLSTM_QK_DIM, MLSTM_V_DIM
    DA = DV + LANES
    nshift = CONV_WIDTH - 1

    tt = lax.broadcasted_iota(jnp.int32, (nshift * L, 2 * L), 0)
    uu = lax.broadcasted_iota(jnp.int32, (nshift * L, 2 * L), 1)
    shift_mat = (uu == L + tt % L - (tt // L + 1)).astype(BF16)
    conv_w = jnp.concatenate([cwq_ref[...], cwk_ref[...]], axis=1)
    conv_b = jnp.concatenate([cbq_ref[...], cbk_ref[...]], axis=1)
    prev = jnp.zeros((L, 2 * DK), BF16)
    for i in range(NC):
        blk = slice(i * L, (i + 1) * L)
        va_s[blk, 0:DV] = mv_ref[0, blk, :]
        va_s[blk, DV:DA] = jnp.ones((L, DA - DV), BF16)
        cur = jnp.concatenate([mq_ref[0, blk, :], mk_ref[0, blk, :]], axis=1)
        shifted = jnp.dot(shift_mat, jnp.concatenate([prev, cur], axis=0),
                          preferred_element_type=F32)
        y = conv_b + cur.astype(F32) * conv_w[nshift:nshift + 1, :]
        for s in range(nshift):
            y = y + shifted[s * L:(s + 1) * L, :] * conv_w[nshift - 1 - s:nshift - s, :]
        y = _silu(y)
        q_s[blk, :] = y[:, 0:DK].astype(BF16)
        k_s[blk, :] = (y[:, DK:2 * DK] * (DK ** -0.5)).astype(BF16)
        prev = cur

    lane = lax.broadcasted_iota(jnp.int32, (1, LANES), 1)
    bias = bg_ref[...]
    b_i = jnp.sum(jnp.where(lane == h, bias, 0.0), axis=1, keepdims=True)
    b_f = jnp.sum(jnp.where(lane == h + MLSTM_HEADS, bias, 0.0), axis=1, keepdims=True)
    ri = lax.broadcasted_iota(jnp.int32, (L, L), 0)
    ci = lax.broadcasted_iota(jnp.int32, (L, L), 1)
    causal = ci <= ri
    eye = (ri == ci).astype(F32)
    i_rows = gt_ref[h, 0] + b_i
    lf_rows = _log_sigmoid(gt_ref[h + MLSTM_HEADS, 0] + b_f)
    b_rows = jnp.dot(lf_rows, (ri <= ci).astype(F32), preferred_element_type=F32,
                     precision=HIGHEST)
    b_end = b_rows[:, L - 1:L]
    g_rows = b_end - b_rows + i_rows
    g_max = jnp.max(g_rows, axis=1, keepdims=True)
    m = jnp.zeros((1, 1), F32)
    m_prev, m_new = [], []
    for c in range(NC):
        m_prev.append(m)
        m = jnp.maximum(b_end[c:c + 1, :] + m, g_max[c:c + 1, :])
        m_new.append(m)
    m_prev = jnp.concatenate(m_prev, axis=0)
    m_new = jnp.concatenate(m_new, axis=0)
    rows_s[0] = b_rows
    rows_s[1] = jnp.exp(g_rows - m_new)
    rows_s[2] = b_rows - i_rows
    rows_s[3] = jnp.broadcast_to(m_prev, (NC, L))
    rows_s[4] = jnp.broadcast_to(jnp.exp(b_end + m_prev - m_new), (NC, L))

    r2 = lax.broadcasted_iota(jnp.int32, (2 * L, 2 * L), 0)
    c2 = lax.broadcasted_iota(jnp.int32, (2 * L, 2 * L), 1)
    ones_blk = ((r2 < L) == (c2 < L)).astype(BF16)

    G = MLSTM_GROUP

    def local(cg, _):
        cs = [cg * G + i for i in range(G)]
        rows = [pl.ds(pl.multiple_of(c * L, L), L) for c in cs]
        b_r = [rows_s[0, pl.ds(c, 1), :] for c in cs]
        w_r = [rows_s[1, pl.ds(c, 1), :] for c in cs]
        u_r = [rows_s[2, pl.ds(c, 1), :] for c in cs]
        mp = [rows_s[3, pl.ds(c, 1), :] for c in cs]
        q = [q_s[r, :] for r in rows]
        k = [k_s[r, :] for r in rows]
        va = [va_s[r, :] for r in rows]
        qk = [_nt(a, b) for a, b in zip(q, k)]
        x2 = [jnp.concatenate([eye * a, eye * b], axis=1) for a, b in zip(b_r, w_r)]
        hi = [x.astype(BF16) for x in x2]
        lo = [(x - h_.astype(F32)).astype(BF16) for x, h_ in zip(x2, hi)]
        yb = [jnp.dot(h_, ones_blk, preferred_element_type=F32)
              + jnp.dot(l_, ones_blk, preferred_element_type=F32) for h_, l_ in zip(hi, lo)]
        b_b = [y[:, 0:L] for y in yb]
        w_b = [y[:, L:2 * L] for y in yb]
        for i in range(G):
            kv_s[cs[i]] = _tn((w_b[i] * k[i].astype(F32)).astype(BF16), va[i])
        dmat = [jnp.where(causal, b - u, NEG) for b, u in zip(b_b, u_r)]
        m_t = [jnp.maximum(b + m_, jnp.max(d, axis=1, keepdims=True))
               for b, m_, d in zip(b_b, mp, dmat)]
        sc = [a * jnp.exp(d - m_) for a, d, m_ in zip(qk, dmat, m_t)]
        for i in range(G):
            acc_s[rows[i], :] = jnp.dot(sc[i].astype(BF16), va[i], preferred_element_type=F32)
            inter_s[rows[i], :] = jnp.exp(b_b[i] + mp[i] - m_t[i])
            emt_s[rows[i], :] = jnp.exp(-m_t[i])
        return 0

    lax.fori_loop(0, NC // G, local, 0)

    g_row = gm_ref[...]
    c_s[...] = jnp.zeros((DK, DA), F32)

    def recur(cg, _):
        cs = [cg * G + i for i in range(G)]
        rows = [pl.ds(pl.multiple_of(c * L, L), L) for c in cs]
        states = [c_s[...]]
        for c in cs:
            dec = rows_s[4, pl.ds(c, 1), :]
            states.append(jnp.concatenate([dec, dec, dec], axis=1) * states[-1] + kv_s[c])
        c_s[...] = states[G]
        read = [jnp.dot(q_s[r, :], st.astype(BF16), preferred_element_type=F32)
                for r, st in zip(rows, states)]
        inter = [inter_s[r, :] for r in rows]
        out = [acc_s[r, :] + jnp.concatenate([it, it, it], axis=1) * rd
               for r, it, rd in zip(rows, inter, read)]
        emt = [emt_s[r, :] for r in rows]
        nrm = [jnp.maximum(jnp.abs(jnp.concatenate([o[:, DV:DA], o[:, DV:DA]], axis=1)),
                           jnp.concatenate([e_, e_], axis=1)) for o, e_ in zip(out, emt)]
        hh = [o[:, 0:DV] / n_ for o, n_ in zip(out, nrm)]
        ms = [jnp.mean(x * x, axis=1, keepdims=True) for x in hh]
        hn = [x * lax.rsqrt(m_ + NORM_EPS) * g_row for x, m_ in zip(hh, ms)]
        for i in range(G):
            o_ref[0, rows[i], :] = (hn[i] * _sigmoid(mo_ref[0, rows[i], :].astype(F32))).astype(BF16)
        return 0

    lax.fori_loop(0, NC // G, recur, 0)


def _mlstm(proj3, gates_t, conv_w, conv_b, bg_row, g_mlstm, anchor_a, anchor_b):
    B, S, _ = proj3.shape
    H, DK, DV = MLSTM_HEADS, MLSTM_QK_DIM, MLSTM_V_DIM
    L = MLSTM_BLOCK
    NC = S // L
    DA = DV + LANES
    qb, kb = OFF_MQ // DK, OFF_MK // DK
    vb, ob = OFF_MV // DV, OFF_MO // DV
    nq = H
    return pl.pallas_call(
        functools.partial(_mlstm_kernel, seq=S),
        grid=(B, H),
        in_specs=[pl.BlockSpec((1, S, DK), lambda b, h: (b, 0, qb + h)),
                  pl.BlockSpec((1, S, DK), lambda b, h: (b, 0, kb + h)),
                  pl.BlockSpec((1, S, DV), lambda b, h: (b, 0, vb + h)),
                  pl.BlockSpec((1, S, DV), lambda b, h: (b, 0, ob + h)),
                  pl.BlockSpec((2 * H, 1, NC, L), lambda b, h: (0, b, 0, 0)),
                  pl.BlockSpec((CONV_WIDTH, DK), lambda b, h: (0, h)),
                  pl.BlockSpec((CONV_WIDTH, DK), lambda b, h: (0, nq + h)),
                  pl.BlockSpec((1, DK), lambda b, h: (0, h)),
                  pl.BlockSpec((1, DK), lambda b, h: (0, nq + h)),
                  pl.BlockSpec((1, LANES), lambda b, h: (0, 0)),
                  pl.BlockSpec((1, DV), lambda b, h: (0, h)),
                  pl.BlockSpec(memory_space=pl.ANY), pl.BlockSpec(memory_space=pl.ANY)],
        out_specs=pl.BlockSpec((1, S, DV), lambda b, h: (b, 0, h)),
        out_shape=jax.ShapeDtypeStruct((B, S, H * DV), BF16),
        scratch_shapes=[pltpu.VMEM((S, DK), BF16),
                        pltpu.VMEM((S, DK), BF16),
                        pltpu.VMEM((S, DA), BF16),
                        pltpu.VMEM((5, NC, L), F32),
                        pltpu.VMEM((S, DA), F32),
                        pltpu.VMEM((NC, DK, DA), F32),
                        pltpu.VMEM((S, L), F32),
                        pltpu.VMEM((S, L), F32),
                        pltpu.VMEM((DK, DA), F32)],
        compiler_params=pltpu.CompilerParams(
            dimension_semantics=("arbitrary", "arbitrary"), vmem_limit_bytes=VMEM_LIMIT),
        name="mlstm_chunkwise",
    )(proj3, proj3, proj3, proj3, gates_t, conv_w, conv_w, conv_b, conv_b, bg_row, g_mlstm,
      anchor_a, anchor_b)


def _rms(y, g):
    ms = jnp.mean(y * y, axis=-1, keepdims=True)
    return y * lax.rsqrt(ms + NORM_EPS) * g


def _merge_kernel(ya_ref, yb_ref, ga_ref, gb_ref, x_ref, mod_ref, wa_ref, wb_ref, wo_ref,
                  gpost_ref, gpre_ref, anchor_ref, x1_ref, h2_ref):
    del anchor_ref
    tm = x_ref.shape[0]
    slabs = [pl.ds(s * (tm // MERGE_SPLIT), tm // MERGE_SPLIT) for s in range(MERGE_SPLIT)]
    pa = [jnp.dot(ya_ref[r, :], wa_ref[...], preferred_element_type=F32) for r in slabs]
    pb = [jnp.dot(yb_ref[r, :], wb_ref[...], preferred_element_type=F32) for r in slabs]
    merged = [_sigmoid(ga_ref[r, :].astype(F32)) * a + _sigmoid(gb_ref[r, :].astype(F32)) * b
              for r, a, b in zip(slabs, pa, pb)]
    y = [jnp.dot(m.astype(BF16), wo_ref[...], preferred_element_type=F32) for m in merged]
    x1 = [x_ref[r, :] + mod_ref[0, 2:3, :] * _rms(v, gpost_ref[...]) for r, v in zip(slabs, y)]
    for r, v in zip(slabs, x1):
        x1_ref[r, :] = v
    h2 = [_rms(v, gpre_ref[...]) * (1.0 + mod_ref[0, 4:5, :]) + mod_ref[0, 3:4, :] for v in x1]
    for r, v in zip(slabs, h2):
        h2_ref[r, :] = _pack_pair(v[:, :HALF], v[:, HALF:])


def _merge(ya2, yb2, proj2, x2, mod3, wa, wb, wo, g_post, g_pre, seq, anchor):
    T = x2.shape[0]
    tm = 512 * MERGE_SPLIT
    per_b = seq // tm
    full = lambda shape: pl.BlockSpec(shape, lambda i: (0,) * len(shape))
    return pl.pallas_call(
        _merge_kernel,
        grid=(T // tm,),
        in_specs=[pl.BlockSpec((tm, ATT_GROUP_W), lambda i: (i, 0)),
                  pl.BlockSpec((tm, D_MODEL), lambda i: (i, 0)),
                  pl.BlockSpec((tm, D_MODEL), lambda i: (i, OFF_GA // D_MODEL)),
                  pl.BlockSpec((tm, D_MODEL), lambda i: (i, OFF_GB // D_MODEL)),
                  pl.BlockSpec((tm, D_MODEL), lambda i: (i, 0)),
                  pl.BlockSpec((1, 6, D_MODEL), lambda i: (i // per_b, 0, 0)),
                  full((ATT_GROUP_W, D_MODEL)), full((D_MODEL, D_MODEL)), full((D_MODEL, D_MODEL)),
                  full((1, D_MODEL)), full((1, D_MODEL)),
                  pl.BlockSpec(memory_space=pl.ANY)],
        out_specs=[pl.BlockSpec((tm, D_MODEL), lambda i: (i, 0)),
                   pl.BlockSpec((tm, HALF), lambda i: (i, 0))],
        out_shape=[jax.ShapeDtypeStruct((T, D_MODEL), F32),
                   jax.ShapeDtypeStruct((T, HALF), jnp.uint32)],
        compiler_params=pltpu.CompilerParams(
            dimension_semantics=("arbitrary",), vmem_limit_bytes=VMEM_LIMIT),
        name="merge_out_proj",
    )(ya2, yb2, proj2, proj2, x2, mod3, wa, wb, wo, g_post, g_pre, anchor)


def _router_kernel(h2_ref, rlo_ref, rhi_ref, bias_ref, idx_ref, w_ref, rank_ref, cnt_ref):
    E = N_EXPERTS
    tr = h2_ref.shape[0]
    gsz = E // N_GROUPS

    @pl.when(pl.program_id(0) == 0)
    def _():
        cnt_ref[...] = jnp.zeros(cnt_ref.shape, F32)

    lo, hi = _unpack_pair(h2_ref[...])
    logits = _nt(rlo_ref[...], lo.astype(BF16)) + _nt(rhi_ref[...], hi.astype(BF16))
    scores = _sigmoid(logits)
    sel = scores + bias_ref[:, 0:1]

    gi = lax.broadcasted_iota(jnp.int32, (gsz, tr), 0).astype(F32)
    gs_rows = []
    for g in range(N_GROUPS):
        blk = sel[g * gsz:(g + 1) * gsz, :]
        m1 = jnp.max(blk, axis=0, keepdims=True)
        a1 = jnp.min(jnp.where(blk == m1, gi, float(E)), axis=0, keepdims=True)
        m2 = jnp.max(jnp.where(gi == a1, -jnp.inf, blk), axis=0, keepdims=True)
        gs_rows.append(m1 + m2)
    gs = jnp.concatenate(gs_rows, axis=0)
    g8 = lax.broadcasted_iota(jnp.int32, (N_GROUPS, tr), 0).astype(F32)
    gmask = jnp.zeros((N_GROUPS, tr), F32)
    for _ in range(TOPK_GROUPS):
        m = jnp.max(gs, axis=0, keepdims=True)
        a = jnp.min(jnp.where(gs == m, g8, float(E)), axis=0, keepdims=True)
        hit = g8 == a
        gmask = jnp.where(hit, 1.0, gmask)
        gs = jnp.where(hit, -jnp.inf, gs)
    selm = jnp.concatenate(
        [jnp.where(gmask[g:g + 1, :] > 0.0, sel[g * gsz:(g + 1) * gsz, :], -jnp.inf)
         for g in range(N_GROUPS)], axis=0)

    ei = lax.broadcasted_iota(jnp.int32, (E, tr), 0).astype(F32)
    picks, weights, hits = [], [], []
    candidates = selm
    for _ in range(TOP_K):
        m = jnp.max(selm, axis=0, keepdims=True)
        a = jnp.min(jnp.where(selm == m, ei, float(E)), axis=0, keepdims=True)
        hit = ei == a
        picks.append(a)
        hits.append(hit)
        weights.append(jnp.sum(jnp.where(hit, scores, 0.0), axis=0, keepdims=True))
        selm = jnp.where(hit, -jnp.inf, selm)
    chosen = jnp.where(selm != candidates, 1.0, 0.0)
    wsum = weights[0]
    for w in weights[1:]:
        wsum = wsum + w

    ti = lax.broadcasted_iota(jnp.int32, (tr, tr), 0)
    tj = lax.broadcasted_iota(jnp.int32, (tr, tr), 1)
    before = (ti < tj).astype(BF16)
    pos = jnp.dot(chosen.astype(BF16), before, preferred_element_type=F32) + cnt_ref[:, 0:1]
    ranks = [jnp.sum(jnp.where(hit, pos, 0.0), axis=0, keepdims=True) for hit in hits]
    cnt_ref[...] = cnt_ref[...] + jnp.sum(chosen, axis=1, keepdims=True)

    idx_ref[...] = jnp.concatenate(picks, axis=0).astype(jnp.int32)
    w_ref[...] = jnp.concatenate([w / wsum * ROUTED_SCALE for w in weights], axis=0)
    rank_ref[...] = jnp.concatenate(ranks, axis=0).astype(jnp.int32)


def _router(h2p, r_lo, r_hi, bias_col, row0, T):
    tr = 512
    off = row0 // tr
    full = lambda shape: pl.BlockSpec(shape, lambda i: (0,) * len(shape))
    return pl.pallas_call(
        _router_kernel,
        grid=(T // tr,),
        in_specs=[pl.BlockSpec((tr, HALF), lambda i: (i + off, 0)),
                  full((N_EXPERTS, HALF)), full((N_EXPERTS, HALF)), full((N_EXPERTS, LANES))],
        out_specs=[pl.BlockSpec((TOP_K, tr), lambda i: (0, i)),
                   pl.BlockSpec((TOP_K, tr), lambda i: (0, i)),
                   pl.BlockSpec((TOP_K, tr), lambda i: (0, i)),
                   full((N_EXPERTS, LANES))],
        out_shape=[jax.ShapeDtypeStruct((TOP_K, T), jnp.int32),
                   jax.ShapeDtypeStruct((TOP_K, T), F32),
                   jax.ShapeDtypeStruct((TOP_K, T), jnp.int32),
                   jax.ShapeDtypeStruct((N_EXPERTS, LANES), F32)],
        compiler_params=pltpu.CompilerParams(
            dimension_semantics=("arbitrary",), vmem_limit_bytes=VMEM_LIMIT),
        name="router_topk",
    )(h2p, r_lo, r_hi, bias_col)


def _dest_kernel(idx_ref, rank_ref, pstart_ref, dest_ref):
    tr = idx_ref.shape[1]
    ei = lax.broadcasted_iota(jnp.int32, (N_EXPERTS, tr), 0)
    start = pstart_ref[:, 0:1]
    rows = []
    for k in range(TOP_K):
        hit = ei == idx_ref[k:k + 1, :]
        rows.append(jnp.sum(jnp.where(hit, start, 0.0), axis=0, keepdims=True))
    dest_ref[...] = jnp.concatenate(rows, axis=0).astype(jnp.int32) + rank_ref[...]


def _slot_index(idx, rank, pstart_col):
    T = idx.shape[1]
    tr = 1024
    return pl.pallas_call(
        _dest_kernel,
        grid=(T // tr,),
        in_specs=[pl.BlockSpec((TOP_K, tr), lambda i: (0, i)),
                  pl.BlockSpec((TOP_K, tr), lambda i: (0, i)),
                  pl.BlockSpec((N_EXPERTS, LANES), lambda i: (0, 0))],
        out_specs=pl.BlockSpec((TOP_K, tr), lambda i: (0, i)),
        out_shape=jax.ShapeDtypeStruct((TOP_K, T), jnp.int32),
        name="slot_index",
    )(idx, rank, pstart_col)


def _ffn_kernel(first_ref, nblk_ref, nused_ref, xs_hbm, wg_ref, wu_ref, wd_ref, ys_hbm,
                xbuf, ybuf, in_sem, out_sem, wg_s, wu_s, wd_s):
    e = pl.program_id(0)
    bm = EXPERT_BLOCK
    ns = EXPERT_SLOTS
    nused = nused_ref[0]
    first = first_ref[e]
    n = nblk_ref[e]

    def in_copy(g):
        slot = g % ns
        return pltpu.make_async_copy(xs_hbm.at[pl.ds(g * bm, bm)], xbuf.at[slot], in_sem.at[slot])

    def out_copy(g):
        slot = g % ns
        return pltpu.make_async_copy(ybuf.at[slot], ys_hbm.at[pl.ds(g * bm, bm)], out_sem.at[slot])

    def fetch(g):
        @pl.when(g < nused)
        def _():
            in_copy(g).start()

    def release(g):
        @pl.when(g >= ns)
        def _():
            out_copy(g - ns).wait()

    def ffn(g):
        lo, hi = _unpack_pair(xbuf[g % ns])
        x = jnp.concatenate([lo.astype(BF16), hi.astype(BF16)], axis=1)
        gate = jnp.dot(x, wg_s[...], preferred_element_type=F32)
        up = jnp.dot(x, wu_s[...], preferred_element_type=F32)
        hid = (_silu(gate) * up).astype(BF16)
        return jnp.dot(hid, wd_s[...], preferred_element_type=F32)

    def pack(g, out):
        ybuf[g % ns] = _pack_pair(out[:, :HALF], out[:, HALF:])

    @pl.when(e == 0)
    def _():
        for q in range(ns - 1):
            fetch(q)

    @pl.when(n > 0)
    def _():
        for packed, dst in ((wg_ref, wg_s), (wu_ref, wu_s), (wd_ref, wd_s)):
            lo, hi = _unpack_pair(packed[0])
            half = dst.shape[0] // 2
            dst[0:half, :] = lo.astype(BF16)
            dst[half:, :] = hi.astype(BF16)

        def two_blocks(j, _):
            g = first + 2 * j
            in_copy(g).wait()
            in_copy(g + 1).wait()
            fetch(g + ns - 1)
            release(g)
            release(g + 1)
            out_a = ffn(g)
            out_b = ffn(g + 1)
            pack(g, out_a)
            pack(g + 1, out_b)
            out_copy(g).start()
            out_copy(g + 1).start()
            fetch(g + ns)
            return 0

        lax.fori_loop(0, n // 2, two_blocks, 0)

        @pl.when(n % 2 == 1)
        def _():
            g = first + n - 1
            in_copy(g).wait()
            fetch(g + ns - 1)
            release(g)
            pack(g, ffn(g))
            out_copy(g).start()

    @pl.when(e == pl.num_programs(0) - 1)
    def _():
        for q in range(ns, 0, -1):
            @pl.when(nused >= q)
            def _(q=q):
                out_copy(nused - q).wait()


def _expert_ffn(first_blk, nblk, nused, xs, w_gate, w_up, w_down):
    P = xs.shape[0]
    bm = EXPERT_BLOCK
    w_map = lambda e, *_: (e, 0, 0)
    grid_spec = pltpu.PrefetchScalarGridSpec(
        num_scalar_prefetch=3,
        grid=(w_gate.shape[0],),
        in_specs=[pl.BlockSpec(memory_space=pl.ANY),
                  pl.BlockSpec((1, D_MODEL // 2, EXPERT_FF), w_map),
                  pl.BlockSpec((1, D_MODEL // 2, EXPERT_FF), w_map),
                  pl.BlockSpec((1, EXPERT_FF // 2, D_MODEL), w_map)],
        out_specs=pl.BlockSpec(memory_space=pl.ANY),
        scratch_shapes=[pltpu.VMEM((EXPERT_SLOTS, bm, HALF), jnp.uint32),
                        pltpu.VMEM((EXPERT_SLOTS, bm, HALF), jnp.uint32),
                        pltpu.SemaphoreType.DMA((EXPERT_SLOTS,)),
                        pltpu.SemaphoreType.DMA((EXPERT_SLOTS,)),
                        pltpu.VMEM((D_MODEL, EXPERT_FF), BF16),
                        pltpu.VMEM((D_MODEL, EXPERT_FF), BF16),
                        pltpu.VMEM((EXPERT_FF, D_MODEL), BF16)],
    )
    return pl.pallas_call(
        _ffn_kernel,
        grid_spec=grid_spec,
        out_shape=jax.ShapeDtypeStruct((P, HALF), jnp.uint32),
        compiler_params=pltpu.CompilerParams(
            dimension_semantics=("arbitrary",), vmem_limit_bytes=VMEM_LIMIT),
        name="routed_experts",
    )(first_blk, nblk, nused, xs, w_gate, w_up, w_down)


def _final_kernel(yg_ref, w_ref, h2_ref, x1_ref, mod_ref, wsg_ref, wsu_ref, wsd_ref, gpost_ref, *rest):
    o_ref = rest[-1]
    lo, hi = _unpack_pair(h2_ref[...])
    h2 = jnp.concatenate([lo.astype(BF16), hi.astype(BF16)], axis=1)
    gate = jnp.dot(h2, wsg_ref[...], preferred_element_type=F32)
    up = jnp.dot(h2, wsu_ref[...], preferred_element_type=F32)
    shared = jnp.dot((_silu(gate) * up).astype(BF16), wsd_ref[...], preferred_element_type=F32)
    y_lo = shared[:, :HALF]
    y_hi = shared[:, HALF:]
    for k in range(TOP_K):
        r_lo, r_hi = _unpack_pair(yg_ref[k])
        wk = w_ref[:, k:k + 1]
        y_lo = y_lo + wk * r_lo
        y_hi = y_hi + wk * r_hi
    ms = (jnp.sum(y_lo * y_lo, axis=-1, keepdims=True)
          + jnp.sum(y_hi * y_hi, axis=-1, keepdims=True)) * (1.0 / D_MODEL)
    inv = lax.rsqrt(ms + NORM_EPS)
    o_ref[:, 0:HALF] = x1_ref[:, 0:HALF] + mod_ref[0, 5:6, 0:HALF] * (y_lo * inv * gpost_ref[:, 0:HALF])
    o_ref[:, HALF:] = x1_ref[:, HALF:] + mod_ref[0, 5:6, HALF:] * (y_hi * inv * gpost_ref[:, HALF:])


def _final(yg, w_tk, h2p, x1, mod3, wsg, wsu, wsd, g_post, seq, row0, out_prev):
    T = x1.shape[0]
    tp = yg.shape[1]
    tm = 512
    per_b = seq // tm
    off = row0 // tm
    full = lambda shape: pl.BlockSpec(shape, lambda i: (0,) * len(shape))
    in_specs = [pl.BlockSpec((TOP_K, tm, HALF), lambda i: (0, i, 0)),
                pl.BlockSpec((tm, TOP_K), lambda i: (i, 0)),
                pl.BlockSpec((tm, HALF), lambda i: (i + off, 0)),
                pl.BlockSpec((tm, D_MODEL), lambda i: (i + off, 0)),
                pl.BlockSpec((1, 6, D_MODEL), lambda i: ((i + off) // per_b, 0, 0)),
                full((D_MODEL, EXPERT_FF)), full((D_MODEL, EXPERT_FF)), full((EXPERT_FF, D_MODEL)),
                full((1, D_MODEL))]
    args = [yg, w_tk, h2p, x1, mod3, wsg, wsu, wsd, g_post]
    aliases = {}
    if out_prev is not None:
        in_specs.append(pl.BlockSpec(memory_space=pl.ANY))
        args.append(out_prev)
        aliases = {len(args) - 1: 0}
    return pl.pallas_call(
        _final_kernel,
        grid=(tp // tm,),
        in_specs=in_specs,
        out_specs=pl.BlockSpec((tm, D_MODEL), lambda i: (i + off, 0)),
        out_shape=jax.ShapeDtypeStruct((T, D_MODEL), F32),
        input_output_aliases=aliases,
        compiler_params=pltpu.CompilerParams(
            dimension_semantics=("arbitrary",), vmem_limit_bytes=VMEM_LIMIT),
        name="shared_expert_combine",
    )(*args)


def _rope_tables(positions):
    inv = jnp.power(ROPE_THETA, -jnp.arange(ROPE_HALF, dtype=F32) / ROPE_HALF)
    ang = positions.astype(F32)[..., None] * inv
    cos, sin = jnp.cos(ang), jnp.sin(ang)
    rest = ATT_HEAD_DIM - 2 * ROPE_HALF
    cs = jnp.concatenate([cos, cos, jnp.ones(ang.shape[:-1] + (rest,), F32)], axis=-1)
    sn = jnp.concatenate([-sin, sin, jnp.zeros(ang.shape[:-1] + (rest,), F32)], axis=-1)
    return jnp.tile(cs, (1, 1, 2)), jnp.tile(sn, (1, 1, 2))


def _layer(x, c, positions, w_ada, b_ada, g_pre_mix, g_post_mix, g_pre_ffn, g_post_ffn,
           w_in, conv_w, conv_b, b_gates, g_mlstm, w_branch_a, w_branch_b, w_out,
           router_w, router_bias, w_exp_gate, w_exp_up, w_exp_down, w_sh_gate, w_sh_up, w_sh_down):
    B, S, D = x.shape
    T = B * S
    H = MLSTM_HEADS
    x2 = x.reshape(T, D)

    mod3 = _adaln(c, w_ada, b_ada).reshape(B, 6, D)

    a_w = 3 * ATT_GROUP_W
    o_mq = 3 * a_w
    o_mk = o_mq + H * MLSTM_QK_DIM
    o_mv = o_mk + H * MLSTM_QK_DIM
    o_mo = o_mv + H * MLSTM_V_DIM
    o_mi = o_mo + H * MLSTM_V_DIM
    o_ga = o_mi + 2 * H
    o_gb = o_ga + D
    w_bf = w_in.astype(BF16)
    w_main = jnp.concatenate(
        [w_bf[:, o_mv:o_mi], w_bf[:, o_ga:o_gb + D], w_bf[:, o_mq:o_mv], w_bf[:, 0:o_mq]], axis=1)
    w_if = w_bf[:, o_mi:o_ga].T

    proj, gates = _in_proj(x2, mod3, g_pre_mix.reshape(1, D), w_main, w_if, S)
    proj3 = proj.reshape(B, S, PROJ_W)

    wg_p, wu_p, wd_p = (_pack_weight_rows(w, gates) for w in (w_exp_gate, w_exp_up, w_exp_down))

    cs, sn = _rope_tables(positions)
    y_a = _attention(proj3, cs, sn)

    bg_row = jnp.pad(b_gates.reshape(1, 2 * H), ((0, 0), (0, LANES - 2 * H)))
    gates_t = gates.reshape(2 * H, B, S // MLSTM_BLOCK, MLSTM_BLOCK)
    y_b = _mlstm(proj3, gates_t, conv_w, conv_b.reshape(1, -1), bg_row, g_mlstm.reshape(1, -1),
                 wg_p, wu_p)

    x1, h2p = _merge(y_a.reshape(T, ATT_GROUP_W), y_b.reshape(T, D), proj, x2, mod3,
                     w_branch_a.astype(BF16), w_branch_b.astype(BF16), w_out.astype(BF16),
                     g_post_mix.reshape(1, D), g_pre_ffn.reshape(1, D), S, wd_p)

    rw_t = router_w.T.astype(BF16)
    bias_col = jnp.broadcast_to(router_bias.reshape(N_EXPERTS, 1), (N_EXPERTS, LANES))
    wsg, wsu, wsd = w_sh_gate.astype(BF16), w_sh_up.astype(BF16), w_sh_down.astype(BF16)

    tp = T // MOE_PARTS
    bm = EXPERT_BLOCK
    nb = (tp * TOP_K) // bm + N_EXPERTS
    out = None
    for part in range(MOE_PARTS):
        row0 = part * tp
        idx, wts, rank, cnt = _router(h2p, rw_t[:, :HALF], rw_t[:, HALF:], bias_col, row0, tp)

        counts = cnt[:, 0].astype(jnp.int32)
        padded = (counts + bm - 1) // bm * bm
        pend = jnp.cumsum(padded)
        pstart = pend - padded
        pstart_col = jnp.broadcast_to(pstart.astype(F32).reshape(N_EXPERTS, 1), (N_EXPERTS, LANES))
        dest = _slot_index(idx, rank, pstart_col)
        nused = (pend[-1] // bm).astype(jnp.int32).reshape(1)

        xs = _dispatch(h2p, dest, nb * bm, row0)
        ys = _expert_ffn((pstart // bm).astype(jnp.int32), (padded // bm).astype(jnp.int32), nused,
                         xs, wg_p, wu_p, wd_p)
        yg = _collect(ys, dest)
        out = _final(yg, wts.T, h2p, x1, mod3, wsg, wsu, wsd, g_post_ffn.reshape(1, D), S, row0, out)
    return out.reshape(B, S, D)


SC_CORES = 2
SC_SUBCORES = 16
SC_WORKERS = SC_CORES * SC_SUBCORES
SC_ROWS = 64


def _sc_mesh():
    return plsc.VectorSubcoreMesh(core_axis_name="c", subcore_axis_name="s",
                                  num_cores=SC_CORES, num_subcores=SC_SUBCORES)


def _worker_id():
    return lax.axis_index("s") * SC_CORES + lax.axis_index("c")


def _dispatch(h2p, dest, n_slots, row0):
    T = dest.shape[1]
    per_w = T // SC_WORKERS
    nch = per_w // SC_ROWS
    idx = dest.reshape(TOP_K, SC_WORKERS, nch, SC_ROWS).transpose(1, 2, 0, 3)
    idx = idx.reshape(SC_WORKERS, nch * TOP_K, SC_ROWS)

    def body(x_hbm, idx_hbm, xs_hbm, idx_v, buf0, buf1, rsem0, rsem1, ssem0, ssem1):
        wid = _worker_id()
        base = row0 + wid * per_w
        pltpu.sync_copy(idx_hbm.at[wid], idx_v)
        bufs = ((buf0, rsem0, ssem0), (buf1, rsem1, ssem1))

        def read(c, buf, rsem):
            return pltpu.make_async_copy(x_hbm.at[pl.ds(base + c * SC_ROWS, SC_ROWS)], buf, rsem)

        def scatter(c, k, buf, ssem):
            return pltpu.make_async_copy(buf, xs_hbm.at[idx_v.at[c * TOP_K + k]], ssem)

        read(0, buf0, rsem0).start()

        @pl.loop(0, nch, step=2)
        def _(c0):
            for b in range(2):
                c = c0 + b
                buf, rsem, ssem = bufs[b]
                obuf, orsem, ossem = bufs[1 - b]
                read(c, buf, rsem).wait()

                @pl.when(c > 0)
                def _():
                    for k in range(TOP_K):
                        scatter(c - 1, k, obuf, ossem).wait()

                @pl.when(c + 1 < nch)
                def _():
                    read(c + 1, obuf, orsem).start()

                for k in range(TOP_K):
                    scatter(c, k, buf, ssem).start()

        for k in range(TOP_K):
            scatter(nch - 1, k, buf1, ssem1).wait()

    run = pl.kernel(
        body,
        out_type=jax.ShapeDtypeStruct((n_slots, HALF), jnp.uint32),
        mesh=_sc_mesh(),
        scratch_types=[pltpu.VMEM((nch * TOP_K, SC_ROWS), jnp.int32),
                       pltpu.VMEM((SC_ROWS, HALF), jnp.uint32),
                       pltpu.VMEM((SC_ROWS, HALF), jnp.uint32),
                       pltpu.SemaphoreType.DMA, pltpu.SemaphoreType.DMA,
                       pltpu.SemaphoreType.DMA, pltpu.SemaphoreType.DMA],
        name="sc_dispatch",
    )
    return run(h2p, idx)


SC_PACK_ROWS = 64
SC_PACK_COLS = 256
SC_LANES = 16


def _pack_weight_rows(w, after):
    E, R, C = w.shape
    hb = R // 2 // SC_PACK_ROWS
    w2 = w.reshape(E * R, C)

    def body(w_hbm, after_hbm, out_hbm):
        del after_hbm

        def block(lo_v, hi_v, out_v):
            @pl.loop(0, SC_PACK_ROWS)
            def _(r):
                @pl.loop(0, SC_PACK_COLS, step=SC_LANES)
                def _(c):
                    cols = pl.ds(c, SC_LANES)
                    pair = plsc.pack(lo_v[r, cols], hi_v[r, cols], format=plsc.PackFormat.INTERLEAVED)
                    out_v[r, cols] = plsc.bitcast(pair, jnp.uint32)

        blk = (SC_PACK_ROWS, SC_PACK_COLS)
        pltpu.emit_pipeline(
            block,
            grid=(E * hb, C // SC_PACK_COLS),
            in_specs=[pl.BlockSpec(blk, lambda i, j: ((i // hb) * 2 * hb + i % hb, j)),
                      pl.BlockSpec(blk, lambda i, j: ((i // hb) * 2 * hb + hb + i % hb, j))],
            out_specs=[pl.BlockSpec(blk, lambda i, j: (i, j))],
            core_axis_name=("c", "s"),
            dimension_semantics=(pltpu.PARALLEL, pltpu.PARALLEL),
        )(w_hbm, w_hbm, out_hbm)

    run = pl.kernel(body, out_type=jax.ShapeDtypeStruct((E * R // 2, C), jnp.uint32),
                    mesh=_sc_mesh(), scratch_types=[], name="sc_pack_weights",
                    compiler_params=pltpu.CompilerParams(needs_layout_passes=False))
    return run(w2, after).reshape(E, R // 2, C)


def _collect(ys, dest):
    n = dest.size
    per_w = n // SC_WORKERS
    nch = per_w // SC_ROWS
    idx = dest.reshape(SC_WORKERS, nch, SC_ROWS)

    def body(ys_hbm, idx_hbm, out_hbm, idx_v, buf0, buf1, gsem0, gsem1, wsem0, wsem1):
        wid = _worker_id()
        base = wid * per_w
        pltpu.sync_copy(idx_hbm.at[wid], idx_v)
        bufs = ((buf0, gsem0, wsem0), (buf1, gsem1, wsem1))

        def gather(c, buf, gsem):
            return pltpu.make_async_copy(ys_hbm.at[idx_v.at[c]], buf, gsem)

        def write(c, buf, wsem):
            return pltpu.make_async_copy(buf, out_hbm.at[pl.ds(base + c * SC_ROWS, SC_ROWS)], wsem)

        gather(0, buf0, gsem0).start()

        @pl.loop(0, nch, step=2)
        def _(c0):
            for b in range(2):
                c = c0 + b
                buf, gsem, wsem = bufs[b]
                obuf, ogsem, owsem = bufs[1 - b]
                gather(c, buf, gsem).wait()

                @pl.when(c > 0)
                def _():
                    write(c - 1, obuf, owsem).wait()

                @pl.when(c + 1 < nch)
                def _():
                    gather(c + 1, obuf, ogsem).start()

                write(c, buf, wsem).start()

        write(nch - 1, buf1, wsem1).wait()

    run = pl.kernel(
        body,
        out_type=jax.ShapeDtypeStruct((n, HALF), jnp.uint32),
        mesh=_sc_mesh(),
        scratch_types=[pltpu.VMEM((nch, SC_ROWS), jnp.int32),
                       pltpu.VMEM((SC_ROWS, HALF), jnp.uint32),
                       pltpu.VMEM((SC_ROWS, HALF), jnp.uint32),
                       pltpu.SemaphoreType.DMA, pltpu.SemaphoreType.DMA,
                       pltpu.SemaphoreType.DMA, pltpu.SemaphoreType.DMA],
        name="sc_collect",
    )
    return run(ys, idx).reshape(dest.shape + (HALF,))


def kernel(x, c, positions, w_ada, b_ada, g_pre_mix, g_post_mix, g_pre_ffn, g_post_ffn, w_in, conv_w, conv_b, b_gates, g_mlstm, w_branch_a, w_branch_b, w_out, router_w, router_bias, w_exp_gate, w_exp_up, w_exp_down, w_sh_gate, w_sh_up, w_sh_down):
    depth = w_ada.shape[0]
    for l in range(depth):
        x = _layer(x, c, positions, w_ada[l], b_ada[l], g_pre_mix[l], g_post_mix[l], g_pre_ffn[l],
                   g_post_ffn[l], w_in[l], conv_w[l], conv_b[l], b_gates[l], g_mlstm[l],
                   w_branch_a[l], w_branch_b[l], w_out[l], router_w[l], router_bias[l],
                   w_exp_gate[l], w_exp_up[l], w_exp_down[l], w_sh_gate[l], w_sh_up[l], w_sh_down[l])
    return x
```

```python
import functools

import jax
import jax.numpy as jnp
from jax import lax
from jax.experimental import pallas as pl
from jax.experimental.pallas import tpu as pltpu
from jax.experimental.pallas import tpu_sc as plsc

F32 = jnp.float32
BF16 = jnp.bfloat16
HIGHEST = lax.Precision.HIGHEST
LANES = 128

D_MODEL = 1024
ATT_GROUPS = ((128, 1), (512, 4), (2048, 16))
ATT_HEAD_DIM = 64
ATT_GROUP_W = 256
ATT_BLK = 128
ATT_PAIR = 2
ROPE_THETA = 500000.0
ROPE_HALF = 8
MLSTM_HEADS = 4
MLSTM_QK_DIM = 128
MLSTM_V_DIM = 256
MLSTM_BLOCK = 128
MLSTM_GROUP = 16
CONV_WIDTH = 4
N_EXPERTS = 256
TOP_K = 8
N_GROUPS = 8
TOPK_GROUPS = 4
EXPERT_FF = 256
ROUTED_SCALE = 2.5
NORM_EPS = 1e-6
NEG = -1e30

OFF_MV = 0
OFF_MO = OFF_MV + MLSTM_HEADS * MLSTM_V_DIM
OFF_GA = OFF_MO + MLSTM_HEADS * MLSTM_V_DIM
OFF_GB = OFF_GA + D_MODEL
OFF_MQ = OFF_GB + D_MODEL
OFF_MK = OFF_MQ + MLSTM_HEADS * MLSTM_QK_DIM
OFF_AQ = OFF_MK + MLSTM_HEADS * MLSTM_QK_DIM
OFF_AK = OFF_AQ + len(ATT_GROUPS) * ATT_GROUP_W
OFF_AV = OFF_AK + len(ATT_GROUPS) * ATT_GROUP_W
PROJ_W = OFF_AV + len(ATT_GROUPS) * ATT_GROUP_W
HALF = D_MODEL // 2

EXPERT_BLOCK = 512
EXPERT_SLOTS = 6
MOE_PARTS = 2
MERGE_SPLIT = 2
VMEM_LIMIT = 56 * 1024 * 1024


def _nt(a, b):
    return lax.dot_general(a, b, (((1,), (1,)), ((), ())), preferred_element_type=F32)


def _tn(a, b):
    return lax.dot_general(a, b, (((0,), (0,)), ((), ())), preferred_element_type=F32)


_sigmoid = jax.nn.sigmoid


def _silu(x):
    return x * _sigmoid(x)


def _pack_pair(lo, hi):
    lo_b = pltpu.bitcast(lo.astype(BF16).astype(F32), jnp.uint32)
    hi_b = pltpu.bitcast(hi.astype(BF16).astype(F32), jnp.uint32)
    return (lo_b >> 16) | (hi_b & jnp.uint32(0xFFFF0000))


def _unpack_pair(w):
    lo = pltpu.bitcast(w << 16, F32)
    hi = pltpu.bitcast(w & jnp.uint32(0xFFFF0000), F32)
    return lo, hi


def _mod_kernel(c_ref, w_ref, b_ref, o_ref):
    a = _silu(c_ref[...])
    o_ref[...] = jnp.dot(a, w_ref[...], preferred_element_type=F32, precision=HIGHEST) + b_ref[...]


def _adaln(c, w_ada, b_ada):
    B = c.shape[0]
    n = w_ada.shape[1]
    tn = 512
    return pl.pallas_call(
        _mod_kernel,
        grid=(n // tn,),
        in_specs=[pl.BlockSpec((B, D_MODEL), lambda j: (0, 0)),
                  pl.BlockSpec((D_MODEL, tn), lambda j: (0, j)),
                  pl.BlockSpec((1, tn), lambda j: (0, j))],
        out_specs=pl.BlockSpec((B, tn), lambda j: (0, j)),
        out_shape=jax.ShapeDtypeStruct((B, n), F32),
        name="adaln_mod",
    )(c, w_ada, b_ada.reshape(1, n))


def _proj_kernel(x_ref, mod_ref, g_ref, w_ref, wif_ref, o_ref, gates_ref, h_ref):
    @pl.when(pl.program_id(1) == 0)
    def _():
        x = x_ref[...]
        ms = jnp.mean(x * x, axis=-1, keepdims=True)
        y = x * lax.rsqrt(ms + NORM_EPS) * g_ref[...]
        h = (y * (1.0 + mod_ref[0, 1:2, :]) + mod_ref[0, 0:1, :]).astype(BF16)
        h_ref[...] = h
        gates_ref[...] = _nt(wif_ref[...], h)

    o_ref[...] = jnp.dot(h_ref[...], w_ref[...], preferred_element_type=F32).astype(BF16)


def _in_proj(x2, mod3, g_pre, w_main, w_if, seq):
    T = x2.shape[0]
    tm, tn = 1024, PROJ_W // 2
    per_b = seq // tm
    return pl.pallas_call(
        _proj_kernel,
        grid=(T // tm, PROJ_W // tn),
        in_specs=[pl.BlockSpec((tm, D_MODEL), lambda i, j: (i, 0)),
                  pl.BlockSpec((1, 6, D_MODEL), lambda i, j: (i // per_b, 0, 0)),
                  pl.BlockSpec((1, D_MODEL), lambda i, j: (0, 0)),
                  pl.BlockSpec((D_MODEL, tn), lambda i, j: (0, j)),
                  pl.BlockSpec((2 * MLSTM_HEADS, D_MODEL), lambda i, j: (0, 0))],
        out_specs=[pl.BlockSpec((tm, tn), lambda i, j: (i, j)),
                   pl.BlockSpec((2 * MLSTM_HEADS, tm), lambda i, j: (0, i))],
        out_shape=[jax.ShapeDtypeStruct((T, PROJ_W), BF16),
                   jax.ShapeDtypeStruct((2 * MLSTM_HEADS, T), F32)],
        scratch_shapes=[pltpu.VMEM((tm, D_MODEL), BF16)],
        compiler_params=pltpu.CompilerParams(
            dimension_semantics=("arbitrary", "arbitrary"), vmem_limit_bytes=VMEM_LIMIT),
        name="norm_in_proj",
    )(x2, mod3, g_pre, w_main, w_if)


def _attn_kernel(q_ref, k_ref, v_ref, cs_ref, sn_ref, o_ref, qf, kf, vf, acc, m_s, l_s, *, seq):
    g = pl.program_id(1)
    lane = lax.broadcasted_iota(jnp.int32, (ATT_BLK, LANES), 1)
    first = (lane % ATT_HEAD_DIM) < ROPE_HALF
    low_head = lane < ATT_HEAD_DIM

    def rope(x, cs, sn):
        partner = jnp.where(first, pltpu.roll(x, LANES - ROPE_HALF, 1), pltpu.roll(x, ROPE_HALF, 1))
        return x * cs + partner * sn

    @pl.when((pl.program_id(0) == 0) & (g == 0))
    def _():
        def zero_pad(i, _):
            rows = pl.ds(pl.multiple_of(i * ATT_BLK, ATT_BLK), ATT_BLK)
            for hp in range(2):
                kf[hp, rows, :] = jnp.zeros((ATT_BLK, LANES), F32)
                vf[hp, rows, :] = jnp.zeros((ATT_BLK, LANES), F32)
            return 0

        lax.fori_loop(0, seq // ATT_BLK, zero_pad, 0)

    def stage(i, _):
        r = pl.multiple_of(i * ATT_BLK, ATT_BLK)
        rows = pl.ds(r, ATT_BLK)
        prow = pl.ds(pl.multiple_of(seq + i * ATT_BLK, ATT_BLK), ATT_BLK)
        cs = cs_ref[0, rows, :]
        sn = sn_ref[0, rows, :]
        for hp in range(2):
            cols = pl.ds(hp * LANES, LANES)
            qf[hp, rows, :] = rope(q_ref[0, rows, cols].astype(F32), cs, sn) * (ATT_HEAD_DIM ** -0.5)
            kf[hp, prow, :] = rope(k_ref[0, rows, cols].astype(F32), cs, sn)
            vf[hp, prow, :] = v_ref[0, rows, cols].astype(F32)
        return 0

    lax.fori_loop(0, seq // ATT_BLK, stage, 0, unroll=4)

    qi = lax.broadcasted_iota(jnp.int32, (ATT_BLK, 2 * ATT_BLK), 0)
    ki = lax.broadcasted_iota(jnp.int32, (ATT_BLK, 2 * ATT_BLK), 1)
    band = (ki >= qi) & (ki <= qi + ATT_BLK)

    def process(d, init):
        span = ATT_BLK * d
        single = seq == span

        def body(cp, _):
            blocks = [cp * ATT_PAIR + i for i in range(ATT_PAIR)]
            qrows, krows, valid = [], [], []
            for c in blocks:
                rho = c % d
                n = c // d
                qstart = rho + n * span
                if single:
                    kstart, nk = seq + qstart, ATT_BLK
                    valid.append(band[:, ATT_BLK:])
                else:
                    kstart, nk = seq + qstart - span, 2 * ATT_BLK
                    valid.append(band & (ki >= jnp.where(n > 0, 0, ATT_BLK)))
                qrows.append(pl.ds(qstart, ATT_BLK, stride=d) if d > 1 else pl.ds(qstart, ATT_BLK))
                krows.append(pl.ds(kstart, nk, stride=d) if d > 1 else pl.ds(kstart, nk))
            units = [(b, hp) for b in range(ATT_PAIR) for hp in range(2)]
            heads = [(u, hh) for u in range(len(units)) for hh in range(2)]
            q2 = [qf[hp, qrows[b], :] for b, hp in units]
            k2 = [kf[hp, krows[b], :].astype(BF16) for b, hp in units]
            v2 = [vf[hp, krows[b], :].astype(BF16) for b, hp in units]
            qh = [jnp.where(low_head if hh == 0 else jnp.logical_not(low_head), q2[u], 0.0).astype(BF16)
                  for u, hh in heads]
            s = [jnp.where(valid[units[u][0]], _nt(qh[i], k2[u]), NEG) for i, (u, hh) in enumerate(heads)]
            m = [jnp.max(x, axis=1, keepdims=True) for x in s]
            p = [jnp.exp(x - mx) for x, mx in zip(s, m)]
            l = [jnp.sum(x, axis=1, keepdims=True) for x in p]
            o = [jnp.dot(p[i].astype(BF16), v2[u], preferred_element_type=F32)
                 for i, (u, hh) in enumerate(heads)]
            for u, (b, hp) in enumerate(units):
                o_b = jnp.where(low_head, o[2 * u], o[2 * u + 1])
                m_b = jnp.where(low_head, m[2 * u], m[2 * u + 1])
                l_b = jnp.where(low_head, l[2 * u], l[2 * u + 1])
                if init:
                    acc[hp, qrows[b], :] = o_b
                    m_s[hp, qrows[b], :] = m_b
                    l_s[hp, qrows[b], :] = l_b
                else:
                    m_old = m_s[hp, qrows[b], :]
                    m_new = jnp.maximum(m_old, m_b)
                    a_old = jnp.exp(m_old - m_new)
                    a_new = jnp.exp(m_b - m_new)
                    acc[hp, qrows[b], :] = acc[hp, qrows[b], :] * a_old + o_b * a_new
                    l_s[hp, qrows[b], :] = l_s[hp, qrows[b], :] * a_old + l_b * a_new
                    m_s[hp, qrows[b], :] = m_new
            return 0

        lax.fori_loop(0, seq // (ATT_BLK * ATT_PAIR), body, 0, unroll=2)

    for gi, (_, d) in enumerate(ATT_GROUPS):
        @pl.when(g == gi)
        def _(d=d, gi=gi):
            process(d, gi == 0)

    @pl.when(g == len(ATT_GROUPS) - 1)
    def _():
        def fin(i, _):
            rows = pl.ds(pl.multiple_of(i * ATT_BLK, ATT_BLK), ATT_BLK)
            for hp in range(2):
                o_ref[0, rows, pl.ds(hp * LANES, LANES)] = (acc[hp, rows, :] / l_s[hp, rows, :]).astype(BF16)
            return 0

        lax.fori_loop(0, seq // ATT_BLK, fin, 0)


def _attention(proj3, cs, sn):
    B, S, _ = proj3.shape
    ng = len(ATT_GROUPS)
    qb, kb, vb = OFF_AQ // ATT_GROUP_W, OFF_AK // ATT_GROUP_W, OFF_AV // ATT_GROUP_W
    return pl.pallas_call(
        functools.partial(_attn_kernel, seq=S),
        grid=(B, ng),
        in_specs=[pl.BlockSpec((1, S, ATT_GROUP_W), lambda b, g: (b, 0, qb + g)),
                  pl.BlockSpec((1, S, ATT_GROUP_W), lambda b, g: (b, 0, kb + g)),
                  pl.BlockSpec((1, S, ATT_GROUP_W), lambda b, g: (b, 0, vb + g)),
                  pl.BlockSpec((1, S, LANES), lambda b, g: (b, 0, 0)),
                  pl.BlockSpec((1, S, LANES), lambda b, g: (b, 0, 0))],
        out_specs=pl.BlockSpec((1, S, ATT_GROUP_W), lambda b, g: (b, 0, 0)),
        out_shape=jax.ShapeDtypeStruct((B, S, ATT_GROUP_W), BF16),
        scratch_shapes=[pltpu.VMEM((2, S, LANES), F32),
                        pltpu.VMEM((2, 2 * S, LANES), F32),
                        pltpu.VMEM((2, 2 * S, LANES), F32),
                        pltpu.VMEM((2, S, LANES), F32),
                        pltpu.VMEM((2, S, LANES), F32),
                        pltpu.VMEM((2, S, LANES), F32)],
        compiler_params=pltpu.CompilerParams(
            dimension_semantics=("arbitrary", "arbitrary"), vmem_limit_bytes=VMEM_LIMIT),
        name="dilated_attention",
    )(proj3, proj3, proj3, cs, sn)


def _log_sigmoid(x):
    return jnp.minimum(x, 0.0) - jnp.log(1.0 + jnp.exp(-jnp.abs(x)))


def _mlstm_kernel(mq_ref, mk_ref, mv_ref, mo_ref, gt_ref, cwq_ref, cwk_ref, cbq_ref, cbk_ref,
                  bg_ref, gm_ref, anchor_a, anchor_b, o_ref, q_s, k_s, va_s, rows_s, acc_s, kv_s,
                  inter_s, emt_s, c_s, *, seq):
    del anchor_a, anchor_b
    h = pl.program_id(1)
    L = MLSTM_BLOCK
    NC = seq // L
    DK, DV = MLSTM_QK_DIM, MLSTM_V_DIM
    DA = DV + LANES
    nshift = CONV_WIDTH - 1

    tt = lax.broadcasted_iota(jnp.int32, (nshift * L, 2 * L), 0)
    uu = lax.broadcasted_iota(jnp.int32, (nshift * L, 2 * L), 1)
    shift_mat = (uu == L + tt % L - (tt // L + 1)).astype(BF16)
    conv_w = jnp.concatenate([cwq_ref[...], cwk_ref[...]], axis=1)
    conv_b = jnp.concatenate([cbq_ref[...], cbk_ref[...]], axis=1)
    prev = jnp.zeros((L, 2 * DK), BF16)
    for i in range(NC):
        blk = slice(i * L, (i + 1) * L)
        va_s[blk, 0:DV] = mv_ref[0, blk, :]
        va_s[blk, DV:DA] = jnp.ones((L, DA - DV), BF16)
        cur = jnp.concatenate([mq_ref[0, blk, :], mk_ref[0, blk, :]], axis=1)
        shifted = jnp.dot(shift_mat, jnp.concatenate([prev, cur], axis=0),
                          preferred_element_type=F32)
        y = conv_b + cur.astype(F32) * conv_w[nshift:nshift + 1, :]
        for s in range(nshift):
            y = y + shifted[s * L:(s + 1) * L, :] * conv_w[nshift - 1 - s:nshift - s, :]
        y = _silu(y)
        q_s[blk, :] = y[:, 0:DK].astype(BF16)
        k_s[blk, :] = (y[:, DK:2 * DK] * (DK ** -0.5)).astype(BF16)
        prev = cur

    lane = lax.broadcasted_iota(jnp.int32, (1, LANES), 1)
    bias = bg_ref[...]
    b_i = jnp.sum(jnp.where(lane == h, bias, 0.0), axis=1, keepdims=True)
    b_f = jnp.sum(jnp.where(lane == h + MLSTM_HEADS, bias, 0.0), axis=1, keepdims=True)
    ri = lax.broadcasted_iota(jnp.int32, (L, L), 0)
    ci = lax.broadcasted_iota(jnp.int32, (L, L), 1)
    causal = ci <= ri
    eye = (ri == ci).astype(F32)
    i_rows = gt_ref[h, 0] + b_i
    lf_rows = _log_sigmoid(gt_ref[h + MLSTM_HEADS, 0] + b_f)
    b_rows = jnp.dot(lf_rows, (ri <= ci).astype(F32), preferred_element_type=F32,
                     precision=HIGHEST)
    b_end = b_rows[:, L - 1:L]
    g_rows = b_end - b_rows + i_rows
    g_max = jnp.max(g_rows, axis=1, keepdims=True)
    m = jnp.zeros((1, 1), F32)
    m_prev, m_new = [], []
    for c in range(NC):
        m_prev.append(m)
        m = jnp.maximum(b_end[c:c + 1, :] + m, g_max[c:c + 1, :])
        m_new.append(m)
    m_prev = jnp.concatenate(m_prev, axis=0)
    m_new = jnp.concatenate(m_new, axis=0)
    rows_s[0] = b_rows
    rows_s[1] = jnp.exp(g_rows - m_new)
    rows_s[2] = b_rows - i_rows
    rows_s[3] = jnp.broadcast_to(m_prev, (NC, L))
    rows_s[4] = jnp.broadcast_to(jnp.exp(b_end + m_prev - m_new), (NC, L))

    r2 = lax.broadcasted_iota(jnp.int32, (2 * L, 2 * L), 0)
    c2 = lax.broadcasted_iota(jnp.int32, (2 * L, 2 * L), 1)
    ones_blk = ((r2 < L) == (c2 < L)).astype(BF16)

    G = MLSTM_GROUP

    def local(cg, _):
        cs = [cg * G + i for i in range(G)]
        rows = [pl.ds(pl.multiple_of(c * L, L), L) for c in cs]
        b_r = [rows_s[0, pl.ds(c, 1), :] for c in cs]
        w_r = [rows_s[1, pl.ds(c, 1), :] for c in cs]
        u_r = [rows_s[2, pl.ds(c, 1), :] for c in cs]
        mp = [rows_s[3, pl.ds(c, 1), :] for c in cs]
        q = [q_s[r, :] for r in rows]
        k = [k_s[r, :] for r in rows]
        va = [va_s[r, :] for r in rows]
        qk = [_nt(a, b) for a, b in zip(q, k)]
        x2 = [jnp.concatenate([eye * a, eye * b], axis=1) for a, b in zip(b_r, w_r)]
        hi = [x.astype(BF16) for x in x2]
        lo = [(x - h_.astype(F32)).astype(BF16) for x, h_ in zip(x2, hi)]
        yb = [jnp.dot(h_, ones_blk, preferred_element_type=F32)
              + jnp.dot(l_, ones_blk, preferred_element_type=F32) for h_, l_ in zip(hi, lo)]
        b_b = [y[:, 0:L] for y in yb]
        w_b = [y[:, L:2 * L] for y in yb]
        for i in range(G):
            kv_s[cs[i]] = _tn((w_b[i] * k[i].astype(F32)).astype(BF16), va[i])
        dmat = [jnp.where(causal, b - u, NEG) for b, u in zip(b_b, u_r)]
        m_t = [jnp.maximum(b + m_, jnp.max(d, axis=1, keepdims=True))
               for b, m_, d in zip(b_b, mp, dmat)]
        sc = [a * jnp.exp(d - m_) for a, d, m_ in zip(qk, dmat, m_t)]
        for i in range(G):
            acc_s[rows[i], :] = jnp.dot(sc[i].astype(BF16), va[i], preferred_element_type=F32)
            inter_s[rows[i], :] = jnp.exp(b_b[i] + mp[i] - m_t[i])
            emt_s[rows[i], :] = jnp.exp(-m_t[i])
        return 0

    lax.fori_loop(0, NC // G, local, 0)

    g_row = gm_ref[...]
    c_s[...] = jnp.zeros((DK, DA), F32)

    def recur(cg, _):
        cs = [cg * G + i for i in range(G)]
        rows = [pl.ds(pl.multiple_of(c * L, L), L) for c in cs]
        states = [c_s[...]]
        for c in cs:
            dec = rows_s[4, pl.ds(c, 1), :]
            states.append(jnp.concatenate([dec, dec, dec], axis=1) * states[-1] + kv_s[c])
        c_s[...] = states[G]
        read = [jnp.dot(q_s[r, :], st.astype(BF16), preferred_element_type=F32)
                for r, st in zip(rows, states)]
        inter = [inter_s[r, :] for r in rows]
        out = [acc_s[r, :] + jnp.concatenate([it, it, it], axis=1) * rd
               for r, it, rd in zip(rows, inter, read)]
        emt = [emt_s[r, :] for r in rows]
        nrm = [jnp.maximum(jnp.abs(jnp.concatenate([o[:, DV:DA], o[:, DV:DA]], axis=1)),
                           jnp.concatenate([e_, e_], axis=1)) for o, e_ in zip(out, emt)]
        hh = [o[:, 0:DV] / n_ for o, n_ in zip(out, nrm)]
        ms = [jnp.mean(x * x, axis=1, keepdims=True) for x in hh]
        hn = [x * lax.rsqrt(m_ + NORM_EPS) * g_row for x, m_ in zip(hh, ms)]
        for i in range(G):
            o_ref[0, rows[i], :] = (hn[i] * _sigmoid(mo_ref[0, rows[i], :].astype(F32))).astype(BF16)
        return 0

    lax.fori_loop(0, NC // G, recur, 0)


def _mlstm(proj3, gates_t, conv_w, conv_b, bg_row, g_mlstm, anchor_a, anchor_b):
    B, S, _ = proj3.shape
    H, DK, DV = MLSTM_HEADS, MLSTM_QK_DIM, MLSTM_V_DIM
    L = MLSTM_BLOCK
    NC = S // L
    DA = DV + LANES
    qb, kb = OFF_MQ // DK, OFF_MK // DK
    vb, ob = OFF_MV // DV, OFF_MO // DV
    nq = H
    return pl.pallas_call(
        functools.partial(_mlstm_kernel, seq=S),
        grid=(B, H),
        in_specs=[pl.BlockSpec((1, S, DK), lambda b, h: (b, 0, qb + h)),
                  pl.BlockSpec((1, S, DK), lambda b, h: (b, 0, kb + h)),
                  pl.BlockSpec((1, S, DV), lambda b, h: (b, 0, vb + h)),
                  pl.BlockSpec((1, S, DV), lambda b, h: (b, 0, ob + h)),
                  pl.BlockSpec((2 * H, 1, NC, L), lambda b, h: (0, b, 0, 0)),
                  pl.BlockSpec((CONV_WIDTH, DK), lambda b, h: (0, h)),
                  pl.BlockSpec((CONV_WIDTH, DK), lambda b, h: (0, nq + h)),
                  pl.BlockSpec((1, DK), lambda b, h: (0, h)),
                  pl.BlockSpec((1, DK), lambda b, h: (0, nq + h)),
                  pl.BlockSpec((1, LANES), lambda b, h: (0, 0)),
                  pl.BlockSpec((1, DV), lambda b, h: (0, h)),
                  pl.BlockSpec(memory_space=pl.ANY), pl.BlockSpec(memory_space=pl.ANY)],
        out_specs=pl.BlockSpec((1, S, DV), lambda b, h: (b, 0, h)),
        out_shape=jax.ShapeDtypeStruct((B, S, H * DV), BF16),
        scratch_shapes=[pltpu.VMEM((S, DK), BF16),
                        pltpu.VMEM((S, DK), BF16),
                        pltpu.VMEM((S, DA), BF16),
                        pltpu.VMEM((5, NC, L), F32),
                        pltpu.VMEM((S, DA), F32),
                        pltpu.VMEM((NC, DK, DA), F32),
                        pltpu.VMEM((S, L), F32),
                        pltpu.VMEM((S, L), F32),
                        pltpu.VMEM((DK, DA), F32)],
        compiler_params=pltpu.CompilerParams(
            dimension_semantics=("arbitrary", "arbitrary"), vmem_limit_bytes=VMEM_LIMIT),
        name="mlstm_chunkwise",
    )(proj3, proj3, proj3, proj3, gates_t, conv_w, conv_w, conv_b, conv_b, bg_row, g_mlstm,
      anchor_a, anchor_b)


def _rms(y, g):
    ms = jnp.mean(y * y, axis=-1, keepdims=True)
    return y * lax.rsqrt(ms + NORM_EPS) * g


def _merge_kernel(ya_ref, yb_ref, ga_ref, gb_ref, x_ref, mod_ref, wa_ref, wb_ref, wo_ref,
                  gpost_ref, gpre_ref, anchor_ref, x1_ref, h2_ref):
    del anchor_ref
    tm = x_ref.shape[0]
    slabs = [pl.ds(s * (tm // MERGE_SPLIT), tm // MERGE_SPLIT) for s in range(MERGE_SPLIT)]
    pa = [jnp.dot(ya_ref[r, :], wa_ref[...], preferred_element_type=F32) for r in slabs]
    pb = [jnp.dot(yb_ref[r, :], wb_ref[...], preferred_element_type=F32) for r in slabs]
    merged = [_sigmoid(ga_ref[r, :].astype(F32)) * a + _sigmoid(gb_ref[r, :].astype(F32)) * b
              for r, a, b in zip(slabs, pa, pb)]
    y = [jnp.dot(m.astype(BF16), wo_ref[...], preferred_element_type=F32) for m in merged]
    x1 = [x_ref[r, :] + mod_ref[0, 2:3, :] * _rms(v, gpost_ref[...]) for r, v in zip(slabs, y)]
    for r, v in zip(slabs, x1):
        x1_ref[r, :] = v
    h2 = [_rms(v, gpre_ref[...]) * (1.0 + mod_ref[0, 4:5, :]) + mod_ref[0, 3:4, :] for v in x1]
    for r, v in zip(slabs, h2):
        h2_ref[r, :] = _pack_pair(v[:, :HALF], v[:, HALF:])


def _merge(ya2, yb2, proj2, x2, mod3, wa, wb, wo, g_post, g_pre, seq, anchor):
    T = x2.shape[0]
    tm = 512 * MERGE_SPLIT
    per_b = seq // tm
    full = lambda shape: pl.BlockSpec(shape, lambda i: (0,) * len(shape))
    return pl.pallas_call(
        _merge_kernel,
        grid=(T // tm,),
        in_specs=[pl.BlockSpec((tm, ATT_GROUP_W), lambda i: (i, 0)),
                  pl.BlockSpec((tm, D_MODEL), lambda i: (i, 0)),
                  pl.BlockSpec((tm, D_MODEL), lambda i: (i, OFF_GA // D_MODEL)),
                  pl.BlockSpec((tm, D_MODEL), lambda i: (i, OFF_GB // D_MODEL)),
                  pl.BlockSpec((tm, D_MODEL), lambda i: (i, 0)),
                  pl.BlockSpec((1, 6, D_MODEL), lambda i: (i // per_b, 0, 0)),
                  full((ATT_GROUP_W, D_MODEL)), full((D_MODEL, D_MODEL)), full((D_MODEL, D_MODEL)),
                  full((1, D_MODEL)), full((1, D_MODEL)),
                  pl.BlockSpec(memory_space=pl.ANY)],
        out_specs=[pl.BlockSpec((tm, D_MODEL), lambda i: (i, 0)),
                   pl.BlockSpec((tm, HALF), lambda i: (i, 0))],
        out_shape=[jax.ShapeDtypeStruct((T, D_MODEL), F32),
                   jax.ShapeDtypeStruct((T, HALF), jnp.uint32)],
        compiler_params=pltpu.CompilerParams(
            dimension_semantics=("arbitrary",), vmem_limit_bytes=VMEM_LIMIT),
        name="merge_out_proj",
    )(ya2, yb2, proj2, proj2, x2, mod3, wa, wb, wo, g_post, g_pre, anchor)


def _router_kernel(h2_ref, rlo_ref, rhi_ref, bias_ref, idx_ref, w_ref, rank_ref, cnt_ref):
    E = N_EXPERTS
    tr = h2_ref.shape[0]
    gsz = E // N_GROUPS

    @pl.when(pl.program_id(0) == 0)
    def _():
        cnt_ref[...] = jnp.zeros(cnt_ref.shape, F32)

    lo, hi = _unpack_pair(h2_ref[...])
    logits = _nt(rlo_ref[...], lo.astype(BF16)) + _nt(rhi_ref[...], hi.astype(BF16))
    scores = _sigmoid(logits)
    sel = scores + bias_ref[:, 0:1]

    gi = lax.broadcasted_iota(jnp.int32, (gsz, tr), 0).astype(F32)
    gs_rows = []
    for g in range(N_GROUPS):
        blk = sel[g * gsz:(g + 1) * gsz, :]
        m1 = jnp.max(blk, axis=0, keepdims=True)
        a1 = jnp.min(jnp.where(blk == m1, gi, float(E)), axis=0, keepdims=True)
        m2 = jnp.max(jnp.where(gi == a1, -jnp.inf, blk), axis=0, keepdims=True)
        gs_rows.append(m1 + m2)
    gs = jnp.concatenate(gs_rows, axis=0)
    g8 = lax.broadcasted_iota(jnp.int32, (N_GROUPS, tr), 0).astype(F32)
    gmask = jnp.zeros((N_GROUPS, tr), F32)
    for _ in range(TOPK_GROUPS):
        m = jnp.max(gs, axis=0, keepdims=True)
        a = jnp.min(jnp.where(gs == m, g8, float(E)), axis=0, keepdims=True)
        hit = g8 == a
        gmask = jnp.where(hit, 1.0, gmask)
        gs = jnp.where(hit, -jnp.inf, gs)
    selm = jnp.concatenate(
        [jnp.where(gmask[g:g + 1, :] > 0.0, sel[g * gsz:(g + 1) * gsz, :], -jnp.inf)
         for g in range(N_GROUPS)], axis=0)

    ei = lax.broadcasted_iota(jnp.int32, (E, tr), 0).astype(F32)
    picks, weights, hits = [], [], []
    candidates = selm
    for _ in range(TOP_K):
        m = jnp.max(selm, axis=0, keepdims=True)
        a = jnp.min(jnp.where(selm == m, ei, float(E)), axis=0, keepdims=True)
        hit = ei == a
        picks.append(a)
        hits.append(hit)
        weights.append(jnp.sum(jnp.where(hit, scores, 0.0), axis=0, keepdims=True))
        selm = jnp.where(hit, -jnp.inf, selm)
    chosen = jnp.where(selm != candidates, 1.0, 0.0)
    wsum = weights[0]
    for w in weights[1:]:
        wsum = wsum + w

    ti = lax.broadcasted_iota(jnp.int32, (tr, tr), 0)
    tj = lax.broadcasted_iota(jnp.int32, (tr, tr), 1)
    before = (ti < tj).astype(BF16)
    pos = jnp.dot(chosen.astype(BF16), before, preferred_element_type=F32) + cnt_ref[:, 0:1]
    ranks = [jnp.sum(jnp.where(hit, pos, 0.0), axis=0, keepdims=True) for hit in hits]
    cnt_ref[...] = cnt_ref[...] + jnp.sum(chosen, axis=1, keepdims=True)

    idx_ref[...] = jnp.concatenate(picks, axis=0).astype(jnp.int32)
    w_ref[...] = jnp.concatenate([w / wsum * ROUTED_SCALE for w in weights], axis=0)
    rank_ref[...] = jnp.concatenate(ranks, axis=0).astype(jnp.int32)


def _router(h2p, r_lo, r_hi, bias_col, row0, T):
    tr = 512
    off = row0 // tr
    full = lambda shape: pl.BlockSpec(shape, lambda i: (0,) * len(shape))
    return pl.pallas_call(
        _router_kernel,
        grid=(T // tr,),
        in_specs=[pl.BlockSpec((tr, HALF), lambda i: (i + off, 0)),
                  full((N_EXPERTS, HALF)), full((N_EXPERTS, HALF)), full((N_EXPERTS, LANES))],
        out_specs=[pl.BlockSpec((TOP_K, tr), lambda i: (0, i)),
                   pl.BlockSpec((TOP_K, tr), lambda i: (0, i)),
                   pl.BlockSpec((TOP_K, tr), lambda i: (0, i)),
                   full((N_EXPERTS, LANES))],
        out_shape=[jax.ShapeDtypeStruct((TOP_K, T), jnp.int32),
                   jax.ShapeDtypeStruct((TOP_K, T), F32),
                   jax.ShapeDtypeStruct((TOP_K, T), jnp.int32),
                   jax.ShapeDtypeStruct((N_EXPERTS, LANES), F32)],
        compiler_params=pltpu.CompilerParams(
            dimension_semantics=("arbitrary",), vmem_limit_bytes=VMEM_LIMIT),
        name="router_topk",
    )(h2p, r_lo, r_hi, bias_col)


def _dest_kernel(idx_ref, rank_ref, pstart_ref, dest_ref):
    tr = idx_ref.shape[1]
    ei = lax.broadcasted_iota(jnp.int32, (N_EXPERTS, tr), 0)
    start = pstart_ref[:, 0:1]
    rows = []
    for k in range(TOP_K):
        hit = ei == idx_ref[k:k + 1, :]
        rows.append(jnp.sum(jnp.where(hit, start, 0.0), axis=0, keepdims=True))
    dest_ref[...] = jnp.concatenate(rows, axis=0).astype(jnp.int32) + rank_ref[...]


def _slot_index(idx, rank, pstart_col):
    T = idx.shape[1]
    tr = 1024
    return pl.pallas_call(
        _dest_kernel,
        grid=(T // tr,),
        in_specs=[pl.BlockSpec((TOP_K, tr), lambda i: (0, i)),
                  pl.BlockSpec((TOP_K, tr), lambda i: (0, i)),
                  pl.BlockSpec((N_EXPERTS, LANES), lambda i: (0, 0))],
        out_specs=pl.BlockSpec((TOP_K, tr), lambda i: (0, i)),
        out_shape=jax.ShapeDtypeStruct((TOP_K, T), jnp.int32),
        name="slot_index",
    )(idx, rank, pstart_col)


def _ffn_kernel(first_ref, nblk_ref, nused_ref, xs_hbm, wg_ref, wu_ref, wd_ref, ys_hbm,
                xbuf, ybuf, in_sem, out_sem, wg_s, wu_s, wd_s):
    e = pl.program_id(0)
    bm = EXPERT_BLOCK
    ns = EXPERT_SLOTS
    nused = nused_ref[0]
    first = first_ref[e]
    n = nblk_ref[e]

    def in_copy(g):
        slot = g % ns
        return pltpu.make_async_copy(xs_hbm.at[pl.ds(g * bm, bm)], xbuf.at[slot], in_sem.at[slot])

    def out_copy(g):
        slot = g % ns
        return pltpu.make_async_copy(ybuf.at[slot], ys_hbm.at[pl.ds(g * bm, bm)], out_sem.at[slot])

    def fetch(g):
        @pl.when(g < nused)
        def _():
            in_copy(g).start()

    def release(g):
        @pl.when(g >= ns)
        def _():
            out_copy(g - ns).wait()

    def ffn(g):
        lo, hi = _unpack_pair(xbuf[g % ns])
        x = jnp.concatenate([lo.astype(BF16), hi.astype(BF16)], axis=1)
        gate = jnp.dot(x, wg_s[...], preferred_element_type=F32)
        up = jnp.dot(x, wu_s[...], preferred_element_type=F32)
        hid = (_silu(gate) * up).astype(BF16)
        return jnp.dot(hid, wd_s[...], preferred_element_type=F32)

    def pack(g, out):
        ybuf[g % ns] = _pack_pair(out[:, :HALF], out[:, HALF:])

    @pl.when(e == 0)
    def _():
        for q in range(ns - 1):
            fetch(q)

    @pl.when(n > 0)
    def _():
        for packed, dst in ((wg_ref, wg_s), (wu_ref, wu_s), (wd_ref, wd_s)):
            lo, hi = _unpack_pair(packed[0])
            half = dst.shape[0] // 2
            dst[0:half, :] = lo.astype(BF16)
            dst[half:, :] = hi.astype(BF16)

        def two_blocks(j, _):
            g = first + 2 * j
            in_copy(g).wait()
            in_copy(g + 1).wait()
            fetch(g + ns - 1)
            release(g)
            release(g + 1)
            out_a = ffn(g)
            out_b = ffn(g + 1)
            pack(g, out_a)
            pack(g + 1, out_b)
            out_copy(g).start()
            out_copy(g + 1).start()
            fetch(g + ns)
            return 0

        lax.fori_loop(0, n // 2, two_blocks, 0)

        @pl.when(n % 2 == 1)
        def _():
            g = first + n - 1
            in_copy(g).wait()
            fetch(g + ns - 1)
            release(g)
            pack(g, ffn(g))
            out_copy(g).start()

    @pl.when(e == pl.num_programs(0) - 1)
    def _():
        for q in range(ns, 0, -1):
            @pl.when(nused >= q)
            def _(q=q):
                out_copy(nused - q).wait()


def _expert_ffn(first_blk, nblk, nused, xs, w_gate, w_up, w_down):
    P = xs.shape[0]
    bm = EXPERT_BLOCK
    w_map = lambda e, *_: (e, 0, 0)
    grid_spec = pltpu.PrefetchScalarGridSpec(
        num_scalar_prefetch=3,
        grid=(w_gate.shape[0],),
        in_specs=[pl.BlockSpec(memory_space=pl.ANY),
                  pl.BlockSpec((1, D_MODEL // 2, EXPERT_FF), w_map),
                  pl.BlockSpec((1, D_MODEL // 2, EXPERT_FF), w_map),
                  pl.BlockSpec((1, EXPERT_FF // 2, D_MODEL), w_map)],
        out_specs=pl.BlockSpec(memory_space=pl.ANY),
        scratch_shapes=[pltpu.VMEM((EXPERT_SLOTS, bm, HALF), jnp.uint32),
                        pltpu.VMEM((EXPERT_SLOTS, bm, HALF), jnp.uint32),
                        pltpu.SemaphoreType.DMA((EXPERT_SLOTS,)),
                        pltpu.SemaphoreType.DMA((EXPERT_SLOTS,)),
                        pltpu.VMEM((D_MODEL, EXPERT_FF), BF16),
                        pltpu.VMEM((D_MODEL, EXPERT_FF), BF16),
                        pltpu.VMEM((EXPERT_FF, D_MODEL), BF16)],
    )
    return pl.pallas_call(
        _ffn_kernel,
        grid_spec=grid_spec,
        out_shape=jax.ShapeDtypeStruct((P, HALF), jnp.uint32),
        compiler_params=pltpu.CompilerParams(
            dimension_semantics=("arbitrary",), vmem_limit_bytes=VMEM_LIMIT),
        name="routed_experts",
    )(first_blk, nblk, nused, xs, w_gate, w_up, w_down)


def _final_kernel(yg_ref, w_ref, h2_ref, x1_ref, mod_ref, wsg_ref, wsu_ref, wsd_ref, gpost_ref, *rest):
    o_ref = rest[-1]
    lo, hi = _unpack_pair(h2_ref[...])
    h2 = jnp.concatenate([lo.astype(BF16), hi.astype(BF16)], axis=1)
    gate = jnp.dot(h2, wsg_ref[...], preferred_element_type=F32)
    up = jnp.dot(h2, wsu_ref[...], preferred_element_type=F32)
    shared = jnp.dot((_silu(gate) * up).astype(BF16), wsd_ref[...], preferred_element_type=F32)
    y_lo = shared[:, :HALF]
    y_hi = shared[:, HALF:]
    for k in range(TOP_K):
        r_lo, r_hi = _unpack_pair(yg_ref[k])
        wk = w_ref[:, k:k + 1]
        y_lo = y_lo + wk * r_lo
        y_hi = y_hi + wk * r_hi
    ms = (jnp.sum(y_lo * y_lo, axis=-1, keepdims=True)
          + jnp.sum(y_hi * y_hi, axis=-1, keepdims=True)) * (1.0 / D_MODEL)
    inv = lax.rsqrt(ms + NORM_EPS)
    o_ref[:, 0:HALF] = x1_ref[:, 0:HALF] + mod_ref[0, 5:6, 0:HALF] * (y_lo * inv * gpost_ref[:, 0:HALF])
    o_ref[:, HALF:] = x1_ref[:, HALF:] + mod_ref[0, 5:6, HALF:] * (y_hi * inv * gpost_ref[:, HALF:])


def _final(yg, w_tk, h2p, x1, mod3, wsg, wsu, wsd, g_post, seq, row0, out_prev):
    T = x1.shape[0]
    tp = yg.shape[1]
    tm = 512
    per_b = seq // tm
    off = row0 // tm
    full = lambda shape: pl.BlockSpec(shape, lambda i: (0,) * len(shape))
    in_specs = [pl.BlockSpec((TOP_K, tm, HALF), lambda i: (0, i, 0)),
                pl.BlockSpec((tm, TOP_K), lambda i: (i, 0)),
                pl.BlockSpec((tm, HALF), lambda i: (i + off, 0)),
                pl.BlockSpec((tm, D_MODEL), lambda i: (i + off, 0)),
                pl.BlockSpec((1, 6, D_MODEL), lambda i: ((i + off) // per_b, 0, 0)),
                full((D_MODEL, EXPERT_FF)), full((D_MODEL, EXPERT_FF)), full((EXPERT_FF, D_MODEL)),
                full((1, D_MODEL))]
    args = [yg, w_tk, h2p, x1, mod3, wsg, wsu, wsd, g_post]
    aliases = {}
    if out_prev is not None:
        in_specs.append(pl.BlockSpec(memory_space=pl.ANY))
        args.append(out_prev)
        aliases = {len(args) - 1: 0}
    return pl.pallas_call(
        _final_kernel,
        grid=(tp // tm,),
        in_specs=in_specs,
        out_specs=pl.BlockSpec((tm, D_MODEL), lambda i: (i + off, 0)),
        out_shape=jax.ShapeDtypeStruct((T, D_MODEL), F32),
        input_output_aliases=aliases,
        compiler_params=pltpu.CompilerParams(
            dimension_semantics=("arbitrary",), vmem_limit_bytes=VMEM_LIMIT),
        name="shared_expert_combine",
    )(*args)


def _rope_tables(positions):
    inv = jnp.power(ROPE_THETA, -jnp.arange(ROPE_HALF, dtype=F32) / ROPE_HALF)
    ang = positions.astype(F32)[..., None] * inv
    cos, sin = jnp.cos(ang), jnp.sin(ang)
    rest = ATT_HEAD_DIM - 2 * ROPE_HALF
    cs = jnp.concatenate([cos, cos, jnp.ones(ang.shape[:-1] + (rest,), F32)], axis=-1)
    sn = jnp.concatenate([-sin, sin, jnp.zeros(ang.shape[:-1] + (rest,), F32)], axis=-1)
    return jnp.tile(cs, (1, 1, 2)), jnp.tile(sn, (1, 1, 2))


def _layer(x, c, positions, w_ada, b_ada, g_pre_mix, g_post_mix, g_pre_ffn, g_post_ffn,
           w_in, conv_w, conv_b, b_gates, g_mlstm, w_branch_a, w_branch_b, w_out,
           router_w, router_bias, w_exp_gate, w_exp_up, w_exp_down, w_sh_gate, w_sh_up, w_sh_down):
    B, S, D = x.shape
    T = B * S
    H = MLSTM_HEADS
    x2 = x.reshape(T, D)

    mod3 = _adaln(c, w_ada, b_ada).reshape(B, 6, D)

    a_w = 3 * ATT_GROUP_W
    o_mq = 3 * a_w
    o_mk = o_mq + H * MLSTM_QK_DIM
    o_mv = o_mk + H * MLSTM_QK_DIM
    o_mo = o_mv + H * MLSTM_V_DIM
    o_mi = o_mo + H * MLSTM_V_DIM
    o_ga = o_mi + 2 * H
    o_gb = o_ga + D
    w_bf = w_in.astype(BF16)
    w_main = jnp.concatenate(
        [w_bf[:, o_mv:o_mi], w_bf[:, o_ga:o_gb + D], w_bf[:, o_mq:o_mv], w_bf[:, 0:o_mq]], axis=1)
    w_if = w_bf[:, o_mi:o_ga].T

    proj, gates = _in_proj(x2, mod3, g_pre_mix.reshape(1, D), w_main, w_if, S)
    proj3 = proj.reshape(B, S, PROJ_W)

    wg_p, wu_p, wd_p = (_pack_weight_rows(w, gates) for w in (w_exp_gate, w_exp_up, w_exp_down))

    cs, sn = _rope_tables(positions)
    y_a = _attention(proj3, cs, sn)

    bg_row = jnp.pad(b_gates.reshape(1, 2 * H), ((0, 0), (0, LANES - 2 * H)))
    gates_t = gates.reshape(2 * H, B, S // MLSTM_BLOCK, MLSTM_BLOCK)
    y_b = _mlstm(proj3, gates_t, conv_w, conv_b.reshape(1, -1), bg_row, g_mlstm.reshape(1, -1),
                 wg_p, wu_p)

    x1, h2p = _merge(y_a.reshape(T, ATT_GROUP_W), y_b.reshape(T, D), proj, x2, mod3,
                     w_branch_a.astype(BF16), w_branch_b.astype(BF16), w_out.astype(BF16),
                     g_post_mix.reshape(1, D), g_pre_ffn.reshape(1, D), S, wd_p)

    rw_t = router_w.T.astype(BF16)
    bias_col = jnp.broadcast_to(router_bias.reshape(N_EXPERTS, 1), (N_EXPERTS, LANES))
    wsg, wsu, wsd = w_sh_gate.astype(BF16), w_sh_up.astype(BF16), w_sh_down.astype(BF16)

    tp = T // MOE_PARTS
    bm = EXPERT_BLOCK
    nb = (tp * TOP_K) // bm + N_EXPERTS
    out = None
    for part in range(MOE_PARTS):
        row0 = part * tp
        idx, wts, rank, cnt = _router(h2p, rw_t[:, :HALF], rw_t[:, HALF:], bias_col, row0, tp)

        counts = cnt[:, 0].astype(jnp.int32)
        padded = (counts + bm - 1) // bm * bm
        pend = jnp.cumsum(padded)
        pstart = pend - padded
        pstart_col = jnp.broadcast_to(pstart.astype(F32).reshape(N_EXPERTS, 1), (N_EXPERTS, LANES))
        dest = _slot_index(idx, rank, pstart_col)
        nused = (pend[-1] // bm).astype(jnp.int32).reshape(1)

        xs = _dispatch(h2p, dest, nb * bm, row0)
        ys = _expert_ffn((pstart // bm).astype(jnp.int32), (padded // bm).astype(jnp.int32), nused,
                         xs, wg_p, wu_p, wd_p)
        yg = _collect(ys, dest)
        out = _final(yg, wts.T, h2p, x1, mod3, wsg, wsu, wsd, g_post_ffn.reshape(1, D), S, row0, out)
    return out.reshape(B, S, D)


SC_CORES = 2
SC_SUBCORES = 16
SC_WORKERS = SC_CORES * SC_SUBCORES
SC_ROWS = 64


def _sc_mesh():
    return plsc.VectorSubcoreMesh(core_axis_name="c", subcore_axis_name="s",
                                  num_cores=SC_CORES, num_subcores=SC_SUBCORES)


def _worker_id():
    return lax.axis_index("s") * SC_CORES + lax.axis_index("c")


def _dispatch(h2p, dest, n_slots, row0):
    T = dest.shape[1]
    per_w = T // SC_WORKERS
    nch = per_w // SC_ROWS
    idx = dest.reshape(TOP_K, SC_WORKERS, nch, SC_ROWS).transpose(1, 2, 0, 3)
    idx = idx.reshape(SC_WORKERS, nch * TOP_K, SC_ROWS)

    def body(x_hbm, idx_hbm, xs_hbm, idx_v, buf0, buf1, rsem0, rsem1, ssem0, ssem1):
        wid = _worker_id()
        base = row0 + wid * per_w
        pltpu.sync_copy(idx_hbm.at[wid], idx_v)
        bufs = ((buf0, rsem0, ssem0), (buf1, rsem1, ssem1))

        def read(c, buf, rsem):
            return pltpu.make_async_copy(x_hbm.at[pl.ds(base + c * SC_ROWS, SC_ROWS)], buf, rsem)

        def scatter(c, k, buf, ssem):
            return pltpu.make_async_copy(buf, xs_hbm.at[idx_v.at[c * TOP_K + k]], ssem)

        read(0, buf0, rsem0).start()

        @pl.loop(0, nch, step=2)
        def _(c0):
            for b in range(2):
                c = c0 + b
                buf, rsem, ssem = bufs[b]
                obuf, orsem, ossem = bufs[1 - b]
                read(c, buf, rsem).wait()

                @pl.when(c > 0)
                def _():
                    for k in range(TOP_K):
                        scatter(c - 1, k, obuf, ossem).wait()

                @pl.when(c + 1 < nch)
                def _():
                    read(c + 1, obuf, orsem).start()

                for k in range(TOP_K):
                    scatter(c, k, buf, ssem).start()

        for k in range(TOP_K):
            scatter(nch - 1, k, buf1, ssem1).wait()

    run = pl.kernel(
        body,
        out_type=jax.ShapeDtypeStruct((n_slots, HALF), jnp.uint32),
        mesh=_sc_mesh(),
        scratch_types=[pltpu.VMEM((nch * TOP_K, SC_ROWS), jnp.int32),
                       pltpu.VMEM((SC_ROWS, HALF), jnp.uint32),
                       pltpu.VMEM((SC_ROWS, HALF), jnp.uint32),
                       pltpu.SemaphoreType.DMA, pltpu.SemaphoreType.DMA,
                       pltpu.SemaphoreType.DMA, pltpu.SemaphoreType.DMA],
        name="sc_dispatch",
    )
    return run(h2p, idx)


SC_PACK_ROWS = 64
SC_PACK_COLS = 256
SC_LANES = 16


def _pack_weight_rows(w, after):
    E, R, C = w.shape
    hb = R // 2 // SC_PACK_ROWS
    w2 = w.reshape(E * R, C)

    def body(w_hbm, after_hbm, out_hbm):
        del after_hbm

        def block(lo_v, hi_v, out_v):
            @pl.loop(0, SC_PACK_ROWS)
            def _(r):
                @pl.loop(0, SC_PACK_COLS, step=SC_LANES)
                def _(c):
                    cols = pl.ds(c, SC_LANES)
                    pair = plsc.pack(lo_v[r, cols], hi_v[r, cols], format=plsc.PackFormat.INTERLEAVED)
                    out_v[r, cols] = plsc.bitcast(pair, jnp.uint32)

        blk = (SC_PACK_ROWS, SC_PACK_COLS)
        pltpu.emit_pipeline(
            block,
            grid=(E * hb, C // SC_PACK_COLS),
            in_specs=[pl.BlockSpec(blk, lambda i, j: ((i // hb) * 2 * hb + i % hb, j)),
                      pl.BlockSpec(blk, lambda i, j: ((i // hb) * 2 * hb + hb + i % hb, j))],
            out_specs=[pl.BlockSpec(blk, lambda i, j: (i, j))],
            core_axis_name=("c", "s"),
            dimension_semantics=(pltpu.PARALLEL, pltpu.PARALLEL),
        )(w_hbm, w_hbm, out_hbm)

    run = pl.kernel(body, out_type=jax.ShapeDtypeStruct((E * R // 2, C), jnp.uint32),
                    mesh=_sc_mesh(), scratch_types=[], name="sc_pack_weights",
                    compiler_params=pltpu.CompilerParams(needs_layout_passes=False))
    return run(w2, after).reshape(E, R // 2, C)


def _collect(ys, dest):
    n = dest.size
    per_w = n // SC_WORKERS
    nch = per_w // SC_ROWS
    idx = dest.reshape(SC_WORKERS, nch, SC_ROWS)

    def body(ys_hbm, idx_hbm, out_hbm, idx_v, buf0, buf1, gsem0, gsem1, wsem0, wsem1):
        wid = _worker_id()
        base = wid * per_w
        pltpu.sync_copy(idx_hbm.at[wid], idx_v)
        bufs = ((buf0, gsem0, wsem0), (buf1, gsem1, wsem1))

        def gather(c, buf, gsem):
            return pltpu.make_async_copy(ys_hbm.at[idx_v.at[c]], buf, gsem)

        def write(c, buf, wsem):
            return pltpu.make_async_copy(buf, out_hbm.at[pl.ds(base + c * SC_ROWS, SC_ROWS)], wsem)

        gather(0, buf0, gsem0).start()

        @pl.loop(0, nch, step=2)
        def _(c0):
            for b in range(2):
                c = c0 + b
                buf, gsem, wsem = bufs[b]
                obuf, ogsem, owsem = bufs[1 - b]
                gather(c, buf, gsem).wait()

                @pl.when(c > 0)
                def _():
                    write(c - 1, obuf, owsem).wait()

                @pl.when(c + 1 < nch)
                def _():
                    gather(c + 1, obuf, ogsem).start()

                write(c, buf, wsem).start()

        write(nch - 1, buf1, wsem1).wait()

    run = pl.kernel(
        body,
        out_type=jax.ShapeDtypeStruct((n, HALF), jnp.uint32),
        mesh=_sc_mesh(),
        scratch_types=[pltpu.VMEM((nch, SC_ROWS), jnp.int32),
                       pltpu.VMEM((SC_ROWS, HALF), jnp.uint32),
                       pltpu.VMEM((SC_ROWS, HALF), jnp.uint32),
                       pltpu.SemaphoreType.DMA, pltpu.SemaphoreType.DMA,
                       pltpu.SemaphoreType.DMA, pltpu.SemaphoreType.DMA],
        name="sc_collect",
    )
    return run(ys, idx).reshape(dest.shape + (HALF,))


def kernel(x, c, positions, w_ada, b_ada, g_pre_mix, g_post_mix, g_pre_ffn, g_post_ffn, w_in, conv_w, conv_b, b_gates, g_mlstm, w_branch_a, w_branch_b, w_out, router_w, router_bias, w_exp_gate, w_exp_up, w_exp_down, w_sh_gate, w_sh_up, w_sh_down):
    depth = w_ada.shape[0]
    for l in range(depth):
        x = _layer(x, c, positions, w_ada[l], b_ada[l], g_pre_mix[l], g_post_mix[l], g_pre_ffn[l],
                   g_post_ffn[l], w_in[l], conv_w[l], conv_b[l], b_gates[l], g_mlstm[l],
                   w_branch_a[l], w_branch_b[l], w_out[l], router_w[l], router_bias[l],
                   w_exp_gate[l], w_exp_up[l], w_exp_down[l], w_sh_gate[l], w_sh_up[l], w_sh_down[l])
    return x
```

```python
import functools

import jax
import jax.numpy as jnp
from jax import lax
from jax.experimental import pallas as pl
from jax.experimental.pallas import tpu as pltpu
from jax.experimental.pallas import tpu_sc as plsc

F32 = jnp.float32
BF16 = jnp.bfloat16
HIGHEST = lax.Precision.HIGHEST
LANES = 128

D_MODEL = 1024
ATT_GROUPS = ((128, 1), (512, 4), (2048, 16))
ATT_HEAD_DIM = 64
ATT_GROUP_W = 256
ATT_BLK = 128
ATT_PAIR = 2
ROPE_THETA = 500000.0
ROPE_HALF = 8
MLSTM_HEADS = 4
MLSTM_QK_DIM = 128
MLSTM_V_DIM = 256
MLSTM_BLOCK = 128
MLSTM_GROUP = 16
CONV_WIDTH = 4
N_EXPERTS = 256
TOP_K = 8
N_GROUPS = 8
TOPK_GROUPS = 4
EXPERT_FF = 256
ROUTED_SCALE = 2.5
NORM_EPS = 1e-6
NEG = -1e30

OFF_MV = 0
OFF_MO = OFF_MV + MLSTM_HEADS * MLSTM_V_DIM
OFF_GA = OFF_MO + MLSTM_HEADS * MLSTM_V_DIM
OFF_GB = OFF_GA + D_MODEL
OFF_MQ = OFF_GB + D_MODEL
OFF_MK = OFF_MQ + MLSTM_HEADS * MLSTM_QK_DIM
OFF_AQ = OFF_MK + MLSTM_HEADS * MLSTM_QK_DIM
OFF_AK = OFF_AQ + len(ATT_GROUPS) * ATT_GROUP_W
OFF_AV = OFF_AK + len(ATT_GROUPS) * ATT_GROUP_W
PROJ_W = OFF_AV + len(ATT_GROUPS) * ATT_GROUP_W
HALF = D_MODEL // 2

EXPERT_BLOCK = 512
EXPERT_SLOTS = 6
MOE_PARTS = 2
MERGE_SPLIT = 2
VMEM_LIMIT = 56 * 1024 * 1024


def _nt(a, b):
    return lax.dot_general(a, b, (((1,), (1,)), ((), ())), preferred_element_type=F32)


def _tn(a, b):
    return lax.dot_general(a, b, (((0,), (0,)), ((), ())), preferred_element_type=F32)


_sigmoid = jax.nn.sigmoid


def _silu(x):
    return x * _sigmoid(x)


def _pack_pair(lo, hi):
    lo_b = pltpu.bitcast(lo.astype(BF16).astype(F32), jnp.uint32)
    hi_b = pltpu.bitcast(hi.astype(BF16).astype(F32), jnp.uint32)
    return (lo_b >> 16) | (hi_b & jnp.uint32(0xFFFF0000))


def _unpack_pair(w):
    lo = pltpu.bitcast(w << 16, F32)
    hi = pltpu.bitcast(w & jnp.uint32(0xFFFF0000), F32)
    return lo, hi


def _mod_kernel(c_ref, w_ref, b_ref, o_ref):
    a = _silu(c_ref[...])
    o_ref[...] = jnp.dot(a, w_ref[...], preferred_element_type=F32, precision=HIGHEST) + b_ref[...]


def _adaln(c, w_ada, b_ada):
    B = c.shape[0]
    n = w_ada.shape[1]
    tn = 512
    return pl.pallas_call(
        _mod_kernel,
        grid=(n // tn,),
        in_specs=[pl.BlockSpec((B, D_MODEL), lambda j: (0, 0)),
                  pl.BlockSpec((D_MODEL, tn), lambda j: (0, j)),
                  pl.BlockSpec((1, tn), lambda j: (0, j))],
        out_specs=pl.BlockSpec((B, tn), lambda j: (0, j)),
        out_shape=jax.ShapeDtypeStruct((B, n), F32),
        name="adaln_mod",
    )(c, w_ada, b_ada.reshape(1, n))


def _proj_kernel(x_ref, mod_ref, g_ref, w_ref, wif_ref, o_ref, gates_ref, h_ref):
    @pl.when(pl.program_id(1) == 0)
    def _():
        x = x_ref[...]
        ms = jnp.mean(x * x, axis=-1, keepdims=True)
        y = x * lax.rsqrt(ms + NORM_EPS) * g_ref[...]
        h = (y * (1.0 + mod_ref[0, 1:2, :]) + mod_ref[0, 0:1, :]).astype(BF16)
        h_ref[...] = h
        gates_ref[...] = _nt(wif_ref[...], h)

    o_ref[...] = jnp.dot(h_ref[...], w_ref[...], preferred_element_type=F32).astype(BF16)


def _in_proj(x2, mod3, g_pre, w_main, w_if, seq):
    T = x2.shape[0]
    tm, tn = 1024, PROJ_W // 2
    per_b = seq // tm
    return pl.pallas_call(
        _proj_kernel,
        grid=(T // tm, PROJ_W // tn),
        in_specs=[pl.BlockSpec((tm, D_MODEL), lambda i, j: (i, 0)),
                  pl.BlockSpec((1, 6, D_MODEL), lambda i, j: (i // per_b, 0, 0)),
                  pl.BlockSpec((1, D_MODEL), lambda i, j: (0, 0)),
                  pl.BlockSpec((D_MODEL, tn), lambda i, j: (0, j)),
                  pl.BlockSpec((2 * MLSTM_HEADS, D_MODEL), lambda i, j: (0, 0))],
        out_specs=[pl.BlockSpec((tm, tn), lambda i, j: (i, j)),
                   pl.BlockSpec((2 * MLSTM_HEADS, tm), lambda i, j: (0, i))],
        out_shape=[jax.ShapeDtypeStruct((T, PROJ_W), BF16),
                   jax.ShapeDtypeStruct((2 * MLSTM_HEADS, T), F32)],
        scratch_shapes=[pltpu.VMEM((tm, D_MODEL), BF16)],
        compiler_params=pltpu.CompilerParams(
            dimension_semantics=("arbitrary", "arbitrary"), vmem_limit_bytes=VMEM_LIMIT),
        name="norm_in_proj",
    )(x2, mod3, g_pre, w_main, w_if)


def _attn_kernel(q_ref, k_ref, v_ref, cs_ref, sn_ref, o_ref, qf, kf, vf, acc, m_s, l_s, *, seq):
    g = pl.program_id(1)
    lane = lax.broadcasted_iota(jnp.int32, (ATT_BLK, LANES), 1)
    first = (lane % ATT_HEAD_DIM) < ROPE_HALF
    low_head = lane < ATT_HEAD_DIM

    def rope(x, cs, sn):
        partner = jnp.where(first, pltpu.roll(x, LANES - ROPE_HALF, 1), pltpu.roll(x, ROPE_HALF, 1))
        return x * cs + partner * sn

    @pl.when((pl.program_id(0) == 0) & (g == 0))
    def _():
        def zero_pad(i, _):
            rows = pl.ds(pl.multiple_of(i * ATT_BLK, ATT_BLK), ATT_BLK)
            for hp in range(2):
                kf[hp, rows, :] = jnp.zeros((ATT_BLK, LANES), F32)
                vf[hp, rows, :] = jnp.zeros((ATT_BLK, LANES), F32)
            return 0

        lax.fori_loop(0, seq // ATT_BLK, zero_pad, 0)

    def stage(i, _):
        r = pl.multiple_of(i * ATT_BLK, ATT_BLK)
        rows = pl.ds(r, ATT_BLK)
        prow = pl.ds(pl.multiple_of(seq + i * ATT_BLK, ATT_BLK), ATT_BLK)
        cs = cs_ref[0, rows, :]
        sn = sn_ref[0, rows, :]
        for hp in range(2):
            cols = pl.ds(hp * LANES, LANES)
            qf[hp, rows, :] = rope(q_ref[0, rows, cols].astype(F32), cs, sn) * (ATT_HEAD_DIM ** -0.5)
            kf[hp, prow, :] = rope(k_ref[0, rows, cols].astype(F32), cs, sn)
            vf[hp, prow, :] = v_ref[0, rows, cols].astype(F32)
        return 0

    lax.fori_loop(0, seq // ATT_BLK, stage, 0, unroll=4)

    qi = lax.broadcasted_iota(jnp.int32, (ATT_BLK, 2 * ATT_BLK), 0)
    ki = lax.broadcasted_iota(jnp.int32, (ATT_BLK, 2 * ATT_BLK), 1)
    band = (ki >= qi) & (ki <= qi + ATT_BLK)

    def process(d, init):
        span = ATT_BLK * d
        single = seq == span

        def body(cp, _):
            blocks = [cp * ATT_PAIR + i for i in range(ATT_PAIR)]
            qrows, krows, valid = [], [], []
            for c in blocks:
                rho = c % d
                n = c // d
                qstart = rho + n * span
                if single:
                    kstart, nk = seq + qstart, ATT_BLK
                    valid.append(band[:, ATT_BLK:])
                else:
                    kstart, nk = seq + qstart - span, 2 * ATT_BLK
                    valid.append(band & (ki >= jnp.where(n > 0, 0, ATT_BLK)))
                qrows.append(pl.ds(qstart, ATT_BLK, stride=d) if d > 1 else pl.ds(qstart, ATT_BLK))
                krows.append(pl.ds(kstart, nk, stride=d) if d > 1 else pl.ds(kstart, nk))
            units = [(b, hp) for b in range(ATT_PAIR) for hp in range(2)]
            heads = [(u, hh) for u in range(len(units)) for hh in range(2)]
            q2 = [qf[hp, qrows[b], :] for b, hp in units]
            k2 = [kf[hp, krows[b], :].astype(BF16) for b, hp in units]
            v2 = [vf[hp, krows[b], :].astype(BF16) for b, hp in units]
            qh = [jnp.where(low_head if hh == 0 else jnp.logical_not(low_head), q2[u], 0.0).astype(BF16)
                  for u, hh in heads]
            s = [jnp.where(valid[units[u][0]], _nt(qh[i], k2[u]), NEG) for i, (u, hh) in enumerate(heads)]
            m = [jnp.max(x, axis=1, keepdims=True) for x in s]
            p = [jnp.exp(x - mx) for x, mx in zip(s, m)]
            l = [jnp.sum(x, axis=1, keepdims=True) for x in p]
            o = [jnp.dot(p[i].astype(BF16), v2[u], preferred_element_type=F32)
                 for i, (u, hh) in enumerate(heads)]
            for u, (b, hp) in enumerate(units):
                o_b = jnp.where(low_head, o[2 * u], o[2 * u + 1])
                m_b = jnp.where(low_head, m[2 * u], m[2 * u + 1])
                l_b = jnp.where(low_head, l[2 * u], l[2 * u + 1])
                if init:
                    acc[hp, qrows[b], :] = o_b
                    m_s[hp, qrows[b], :] = m_b
                    l_s[hp, qrows[b], :] = l_b
                else:
                    m_old = m_s[hp, qrows[b], :]
                    m_new = jnp.maximum(m_old, m_b)
                    a_old = jnp.exp(m_old - m_new)
                    a_new = jnp.exp(m_b - m_new)
                    acc[hp, qrows[b], :] = acc[hp, qrows[b], :] * a_old + o_b * a_new
                    l_s[hp, qrows[b], :] = l_s[hp, qrows[b], :] * a_old + l_b * a_new
                    m_s[hp, qrows[b], :] = m_new
            return 0

        lax.fori_loop(0, seq // (ATT_BLK * ATT_PAIR), body, 0, unroll=2)

    for gi, (_, d) in enumerate(ATT_GROUPS):
        @pl.when(g == gi)
        def _(d=d, gi=gi):
            process(d, gi == 0)

    @pl.when(g == len(ATT_GROUPS) - 1)
    def _():
        def fin(i, _):
            rows = pl.ds(pl.multiple_of(i * ATT_BLK, ATT_BLK), ATT_BLK)
            for hp in range(2):
                o_ref[0, rows, pl.ds(hp * LANES, LANES)] = (acc[hp, rows, :] / l_s[hp, rows, :]).astype(BF16)
            return 0

        lax.fori_loop(0, seq // ATT_BLK, fin, 0)


def _attention(proj3, cs, sn):
    B, S, _ = proj3.shape
    ng = len(ATT_GROUPS)
    qb, kb, vb = OFF_AQ // ATT_GROUP_W, OFF_AK // ATT_GROUP_W, OFF_AV // ATT_GROUP_W
    return pl.pallas_call(
        functools.partial(_attn_kernel, seq=S),
        grid=(B, ng),
        in_specs=[pl.BlockSpec((1, S, ATT_GROUP_W), lambda b, g: (b, 0, qb + g)),
                  pl.BlockSpec((1, S, ATT_GROUP_W), lambda b, g: (b, 0, kb + g)),
                  pl.BlockSpec((1, S, ATT_GROUP_W), lambda b, g: (b, 0, vb + g)),
                  pl.BlockSpec((1, S, LANES), lambda b, g: (b, 0, 0)),
                  pl.BlockSpec((1, S, LANES), lambda b, g: (b, 0, 0))],
        out_specs=pl.BlockSpec((1, S, ATT_GROUP_W), lambda b, g: (b, 0, 0)),
        out_shape=jax.ShapeDtypeStruct((B, S, ATT_GROUP_W), BF16),
        scratch_shapes=[pltpu.VMEM((2, S, LANES), F32),
                        pltpu.VMEM((2, 2 * S, LANES), F32),
                        pltpu.VMEM((2, 2 * S, LANES), F32),
                        pltpu.VMEM((2, S, LANES), F32),
                        pltpu.VMEM((2, S, LANES), F32),
                        pltpu.VMEM((2, S, LANES), F32)],
        compiler_params=pltpu.CompilerParams(
            dimension_semantics=("arbitrary", "arbitrary"), vmem_limit_bytes=VMEM_LIMIT),
        name="dilated_attention",
    )(proj3, proj3, proj3, cs, sn)


def _log_sigmoid(x):
    return jnp.minimum(x, 0.0) - jnp.log(1.0 + jnp.exp(-jnp.abs(x)))


def _mlstm_kernel(mq_ref, mk_ref, mv_ref, mo_ref, gt_ref, cwq_ref, cwk_ref, cbq_ref, cbk_ref,
                  bg_ref, gm_ref, anchor_a, anchor_b, o_ref, q_s, k_s, va_s, rows_s, acc_s, kv_s,
                  inter_s, emt_s, c_s, *, seq):
    del anchor_a, anchor_b
    h = pl.program_id(1)
    L = MLSTM_BLOCK
    NC = seq // L
    DK, DV = MLSTM_QK_DIM, MLSTM_V_DIM
    DA = DV + LANES
    nshift = CONV_WIDTH - 1

    tt = lax.broadcasted_iota(jnp.int32, (nshift * L, 2 * L), 0)
    uu = lax.broadcasted_iota(jnp.int32, (nshift * L, 2 * L), 1)
    shift_mat = (uu == L + tt % L - (tt // L + 1)).astype(BF16)
    conv_w = jnp.concatenate([cwq_ref[...], cwk_ref[...]], axis=1)
    conv_b = jnp.concatenate([cbq_ref[...], cbk_ref[...]], axis=1)
    prev = jnp.zeros((L, 2 * DK), BF16)
    for i in range(NC):
        blk = slice(i * L, (i + 1) * L)
        va_s[blk, 0:DV] = mv_ref[0, blk, :]
        va_s[blk, DV:DA] = jnp.ones((L, DA - DV), BF16)
        cur = jnp.concatenate([mq_ref[0, blk, :], mk_ref[0, blk, :]], axis=1)
        shifted = jnp.dot(shift_mat, jnp.concatenate([prev, cur], axis=0),
                          preferred_element_type=F32)
        y = conv_b + cur.astype(F32) * conv_w[nshift:nshift + 1, :]
        for s in range(nshift):
            y = y + shifted[s * L:(s + 1) * L, :] * conv_w[nshift - 1 - s:nshift - s, :]
        y = _silu(y)
        q_s[blk, :] = y[:, 0:DK].astype(BF16)
        k_s[blk, :] = (y[:, DK:2 * DK] * (DK ** -0.5)).astype(BF16)
        prev = cur

    lane = lax.broadcasted_iota(jnp.int32, (1, LANES), 1)
    bias = bg_ref[...]
    b_i = jnp.sum(jnp.where(lane == h, bias, 0.0), axis=1, keepdims=True)
    b_f = jnp.sum(jnp.where(lane == h + MLSTM_HEADS, bias, 0.0), axis=1, keepdims=True)
    ri = lax.broadcasted_iota(jnp.int32, (L, L), 0)
    ci = lax.broadcasted_iota(jnp.int32, (L, L), 1)
    causal = ci <= ri
    eye = (ri == ci).astype(F32)
    i_rows = gt_ref[h, 0] + b_i
    lf_rows = _log_sigmoid(gt_ref[h + MLSTM_HEADS, 0] + b_f)
    b_rows = jnp.dot(lf_rows, (ri <= ci).astype(F32), preferred_element_type=F32,
                     precision=HIGHEST)
    b_end = b_rows[:, L - 1:L]
    g_rows = b_end - b_rows + i_rows
    g_max = jnp.max(g_rows, axis=1, keepdims=True)
    m = jnp.zeros((1, 1), F32)
    m_prev, m_new = [], []
    for c in range(NC):
        m_prev.append(m)
        m = jnp.maximum(b_end[c:c + 1, :] + m, g_max[c:c + 1, :])
        m_new.append(m)
    m_prev = jnp.concatenate(m_prev, axis=0)
    m_new = jnp.concatenate(m_new, axis=0)
    rows_s[0] = b_rows
    rows_s[1] = jnp.exp(g_rows - m_new)
    rows_s[2] = b_rows - i_rows
    rows_s[3] = jnp.broadcast_to(m_prev, (NC, L))
    rows_s[4] = jnp.broadcast_to(jnp.exp(b_end + m_prev - m_new), (NC, L))

    r2 = lax.broadcasted_iota(jnp.int32, (2 * L, 2 * L), 0)
    c2 = lax.broadcasted_iota(jnp.int32, (2 * L, 2 * L), 1)
    ones_blk = ((r2 < L) == (c2 < L)).astype(BF16)

    G = MLSTM_GROUP

    def local(cg, _):
        cs = [cg * G + i for i in range(G)]
        rows = [pl.ds(pl.multiple_of(c * L, L), L) for c in cs]
        b_r = [rows_s[0, pl.ds(c, 1), :] for c in cs]
        w_r = [rows_s[1, pl.ds(c, 1), :] for c in cs]
        u_r = [rows_s[2, pl.ds(c, 1), :] for c in cs]
        mp = [rows_s[3, pl.ds(c, 1), :] for c in cs]
        q = [q_s[r, :] for r in rows]
        k = [k_s[r, :] for r in rows]
        va = [va_s[r, :] for r in rows]
        qk = [_nt(a, b) for a, b in zip(q, k)]
        x2 = [jnp.concatenate([eye * a, eye * b], axis=1) for a, b in zip(b_r, w_r)]
        hi = [x.astype(BF16) for x in x2]
        lo = [(x - h_.astype(F32)).astype(BF16) for x, h_ in zip(x2, hi)]
        yb = [jnp.dot(h_, ones_blk, preferred_element_type=F32)
              + jnp.dot(l_, ones_blk, preferred_element_type=F32) for h_, l_ in zip(hi, lo)]
        b_b = [y[:, 0:L] for y in yb]
        w_b = [y[:, L:2 * L] for y in yb]
        for i in range(G):
            kv_s[cs[i]] = _tn((w_b[i] * k[i].astype(F32)).astype(BF16), va[i])
        dmat = [jnp.where(causal, b - u, NEG) for b, u in zip(b_b, u_r)]
        m_t = [jnp.maximum(b + m_, jnp.max(d, axis=1, keepdims=True))
               for b, m_, d in zip(b_b, mp, dmat)]
        sc = [a * jnp.exp(d - m_) for a, d, m_ in zip(qk, dmat, m_t)]
        for i in range(G):
            acc_s[rows[i], :] = jnp.dot(sc[i].astype(BF16), va[i], preferred_element_type=F32)
            inter_s[rows[i], :] = jnp.exp(b_b[i] + mp[i] - m_t[i])
            emt_s[rows[i], :] = jnp.exp(-m_t[i])
        return 0

    lax.fori_loop(0, NC // G, local, 0)

    g_row = gm_ref[...]
    c_s[...] = jnp.zeros((DK, DA), F32)

    def recur(cg, _):
        cs = [cg * G + i for i in range(G)]
        rows = [pl.ds(pl.multiple_of(c * L, L), L) for c in cs]
        states = [c_s[...]]
        for c in cs:
            dec = rows_s[4, pl.ds(c, 1), :]
            states.append(jnp.concatenate([dec, dec, dec], axis=1) * states[-1] + kv_s[c])
        c_s[...] = states[G]
        read = [jnp.dot(q_s[r, :], st.astype(BF16), preferred_element_type=F32)
                for r, st in zip(rows, states)]
        inter = [inter_s[r, :] for r in rows]
        out = [acc_s[r, :] + jnp.concatenate([it, it, it], axis=1) * rd
               for r, it, rd in zip(rows, inter, read)]
        emt = [emt_s[r, :] for r in rows]
        nrm = [jnp.maximum(jnp.abs(jnp.concatenate([o[:, DV:DA], o[:, DV:DA]], axis=1)),
                           jnp.concatenate([e_, e_], axis=1)) for o, e_ in zip(out, emt)]
        hh = [o[:, 0:DV] / n_ for o, n_ in zip(out, nrm)]
        ms = [jnp.mean(x * x, axis=1, keepdims=True) for x in hh]
        hn = [x * lax.rsqrt(m_ + NORM_EPS) * g_row for x, m_ in zip(hh, ms)]
        for i in range(G):
            o_ref[0, rows[i], :] = (hn[i] * _sigmoid(mo_ref[0, rows[i], :].astype(F32))).astype(BF16)
        return 0

    lax.fori_loop(0, NC // G, recur, 0)


def _mlstm(proj3, gates_t, conv_w, conv_b, bg_row, g_mlstm, anchor_a, anchor_b):
    B, S, _ = proj3.shape
    H, DK, DV = MLSTM_HEADS, MLSTM_QK_DIM, MLSTM_V_DIM
    L = MLSTM_BLOCK
    NC = S // L
    DA = DV + LANES
    qb, kb = OFF_MQ // DK, OFF_MK // DK
    vb, ob = OFF_MV // DV, OFF_MO // DV
    nq = H
    return pl.pallas_call(
        functools.partial(_mlstm_kernel, seq=S),
        grid=(B, H),
        in_specs=[pl.BlockSpec((1, S, DK), lambda b, h: (b, 0, qb + h)),
                  pl.BlockSpec((1, S, DK), lambda b, h: (b, 0, kb + h)),
                  pl.BlockSpec((1, S, DV), lambda b, h: (b, 0, vb + h)),
                  pl.BlockSpec((1, S, DV), lambda b, h: (b, 0, ob + h)),
                  pl.BlockSpec((2 * H, 1, NC, L), lambda b, h: (0, b, 0, 0)),
                  pl.BlockSpec((CONV_WIDTH, DK), lambda b, h: (0, h)),
                  pl.BlockSpec((CONV_WIDTH, DK), lambda b, h: (0, nq + h)),
                  pl.BlockSpec((1, DK), lambda b, h: (0, h)),
                  pl.BlockSpec((1, DK), lambda b, h: (0, nq + h)),
                  pl.BlockSpec((1, LANES), lambda b, h: (0, 0)),
                  pl.BlockSpec((1, DV), lambda b, h: (0, h)),
                  pl.BlockSpec(memory_space=pl.ANY), pl.BlockSpec(memory_space=pl.ANY)],
        out_specs=pl.BlockSpec((1, S, DV), lambda b, h: (b, 0, h)),
        out_shape=jax.ShapeDtypeStruct((B, S, H * DV), BF16),
        scratch_shapes=[pltpu.VMEM((S, DK), BF16),
                        pltpu.VMEM((S, DK), BF16),
                        pltpu.VMEM((S, DA), BF16),
                        pltpu.VMEM((5, NC, L), F32),
                        pltpu.VMEM((S, DA), F32),
                        pltpu.VMEM((NC, DK, DA), F32),
                        pltpu.VMEM((S, L), F32),
                        pltpu.VMEM((S, L), F32),
                        pltpu.VMEM((DK, DA), F32)],
        compiler_params=pltpu.CompilerParams(
            dimension_semantics=("arbitrary", "arbitrary"), vmem_limit_bytes=VMEM_LIMIT),
        name="mlstm_chunkwise",
    )(proj3, proj3, proj3, proj3, gates_t, conv_w, conv_w, conv_b, conv_b, bg_row, g_mlstm,
      anchor_a, anchor_b)


def _rms(y, g):
    ms = jnp.mean(y * y, axis=-1, keepdims=True)
    return y * lax.rsqrt(ms + NORM_EPS) * g


def _merge_kernel(ya_ref, yb_ref, ga_ref, gb_ref, x_ref, mod_ref, wa_ref, wb_ref, wo_ref,
                  gpost_ref, gpre_ref, anchor_ref, x1_ref, h2_ref):
    del anchor_ref
    tm = x_ref.shape[0]
    slabs = [pl.ds(s * (tm // MERGE_SPLIT), tm // MERGE_SPLIT) for s in range(MERGE_SPLIT)]
    pa = [jnp.dot(ya_ref[r, :], wa_ref[...], preferred_element_type=F32) for r in slabs]
    pb = [jnp.dot(yb_ref[r, :], wb_ref[...], preferred_element_type=F32) for r in slabs]
    merged = [_sigmoid(ga_ref[r, :].astype(F32)) * a + _sigmoid(gb_ref[r, :].astype(F32)) * b
              for r, a, b in zip(slabs, pa, pb)]
    y = [jnp.dot(m.astype(BF16), wo_ref[...], preferred_element_type=F32) for m in merged]
    x1 = [x_ref[r, :] + mod_ref[0, 2:3, :] * _rms(v, gpost_ref[...]) for r, v in zip(slabs, y)]
    for r, v in zip(slabs, x1):
        x1_ref[r, :] = v
    h2 = [_rms(v, gpre_ref[...]) * (1.0 + mod_ref[0, 4:5, :]) + mod_ref[0, 3:4, :] for v in x1]
    for r, v in zip(slabs, h2):
        h2_ref[r, :] = _pack_pair(v[:, :HALF], v[:, HALF:])


def _merge(ya2, yb2, proj2, x2, mod3, wa, wb, wo, g_post, g_pre, seq, anchor):
    T = x2.shape[0]
    tm = 512 * MERGE_SPLIT
    per_b = seq // tm
    full = lambda shape: pl.BlockSpec(shape, lambda i: (0,) * len(shape))
    return pl.pallas_call(
        _merge_kernel,
        grid=(T // tm,),
        in_specs=[pl.BlockSpec((tm, ATT_GROUP_W), lambda i: (i, 0)),
                  pl.BlockSpec((tm, D_MODEL), lambda i: (i, 0)),
                  pl.BlockSpec((tm, D_MODEL), lambda i: (i, OFF_GA // D_MODEL)),
                  pl.BlockSpec((tm, D_MODEL), lambda i: (i, OFF_GB // D_MODEL)),
                  pl.BlockSpec((tm, D_MODEL), lambda i: (i, 0)),
                  pl.BlockSpec((1, 6, D_MODEL), lambda i: (i // per_b, 0, 0)),
                  full((ATT_GROUP_W, D_MODEL)), full((D_MODEL, D_MODEL)), full((D_MODEL, D_MODEL)),
                  full((1, D_MODEL)), full((1, D_MODEL)),
                  pl.BlockSpec(memory_space=pl.ANY)],
        out_specs=[pl.BlockSpec((tm, D_MODEL), lambda i: (i, 0)),
                   pl.BlockSpec((tm, HALF), lambda i: (i, 0))],
        out_shape=[jax.ShapeDtypeStruct((T, D_MODEL), F32),
                   jax.ShapeDtypeStruct((T, HALF), jnp.uint32)],
        compiler_params=pltpu.CompilerParams(
            dimension_semantics=("arbitrary",), vmem_limit_bytes=VMEM_LIMIT),
        name="merge_out_proj",
    )(ya2, yb2, proj2, proj2, x2, mod3, wa, wb, wo, g_post, g_pre, anchor)


def _router_kernel(h2_ref, rlo_ref, rhi_ref, bias_ref, idx_ref, w_ref, rank_ref, cnt_ref):
    E = N_EXPERTS
    tr = h2_ref.shape[0]
    gsz = E // N_GROUPS

    @pl.when(pl.program_id(0) == 0)
    def _():
        cnt_ref[...] = jnp.zeros(cnt_ref.shape, F32)

    lo, hi = _unpack_pair(h2_ref[...])
    logits = _nt(rlo_ref[...], lo.astype(BF16)) + _nt(rhi_ref[...], hi.astype(BF16))
    scores = _sigmoid(logits)
    sel = scores + bias_ref[:, 0:1]

    gi = lax.broadcasted_iota(jnp.int32, (gsz, tr), 0).astype(F32)
    gs_rows = []
    for g in range(N_GROUPS):
        blk = sel[g * gsz:(g + 1) * gsz, :]
        m1 = jnp.max(blk, axis=0, keepdims=True)
        a1 = jnp.min(jnp.where(blk == m1, gi, float(E)), axis=0, keepdims=True)
        m2 = jnp.max(jnp.where(gi == a1, -jnp.inf, blk), axis=0, keepdims=True)
        gs_rows.append(m1 + m2)
    gs = jnp.concatenate(gs_rows, axis=0)
    g8 = lax.broadcasted_iota(jnp.int32, (N_GROUPS, tr), 0).astype(F32)
    gmask = jnp.zeros((N_GROUPS, tr), F32)
    for _ in range(TOPK_GROUPS):
        m = jnp.max(gs, axis=0, keepdims=True)
        a = jnp.min(jnp.where(gs == m, g8, float(E)), axis=0, keepdims=True)
        hit = g8 == a
        gmask = jnp.where(hit, 1.0, gmask)
        gs = jnp.where(hit, -jnp.inf, gs)
    selm = jnp.concatenate(
        [jnp.where(gmask[g:g + 1, :] > 0.0, sel[g * gsz:(g + 1) * gsz, :], -jnp.inf)
         for g in range(N_GROUPS)], axis=0)

    ei = lax.broadcasted_iota(jnp.int32, (E, tr), 0).astype(F32)
    picks, weights, hits = [], [], []
    candidates = selm
    for _ in range(TOP_K):
        m = jnp.max(selm, axis=0, keepdims=True)
        a = jnp.min(jnp.where(selm == m, ei, float(E)), axis=0, keepdims=True)
        hit = ei == a
        picks.append(a)
        hits.append(hit)
        weights.append(jnp.sum(jnp.where(hit, scores, 0.0), axis=0, keepdims=True))
        selm = jnp.where(hit, -jnp.inf, selm)
    chosen = jnp.where(selm != candidates, 1.0, 0.0)
    wsum = weights[0]
    for w in weights[1:]:
        wsum = wsum + w

    ti = lax.broadcasted_iota(jnp.int32, (tr, tr), 0)
    tj = lax.broadcasted_iota(jnp.int32, (tr, tr), 1)
    before = (ti < tj).astype(BF16)
    pos = jnp.dot(chosen.astype(BF16), before, preferred_element_type=F32) + cnt_ref[:, 0:1]
    ranks = [jnp.sum(jnp.where(hit, pos, 0.0), axis=0, keepdims=True) for hit in hits]
    cnt_ref[...] = cnt_ref[...] + jnp.sum(chosen, axis=1, keepdims=True)

    idx_ref[...] = jnp.concatenate(picks, axis=0).astype(jnp.int32)
    w_ref[...] = jnp.concatenate([w / wsum * ROUTED_SCALE for w in weights], axis=0)
    rank_ref[...] = jnp.concatenate(ranks, axis=0).astype(jnp.int32)


def _router(h2p, r_lo, r_hi, bias_col, row0, T):
    tr = 512
    off = row0 // tr
    full = lambda shape: pl.BlockSpec(shape, lambda i: (0,) * len(shape))
    return pl.pallas_call(
        _router_kernel,
        grid=(T // tr,),
        in_specs=[pl.BlockSpec((tr, HALF), lambda i: (i + off, 0)),
                  full((N_EXPERTS, HALF)), full((N_EXPERTS, HALF)), full((N_EXPERTS, LANES))],
        out_specs=[pl.BlockSpec((TOP_K, tr), lambda i: (0, i)),
                   pl.BlockSpec((TOP_K, tr), lambda i: (0, i)),
                   pl.BlockSpec((TOP_K, tr), lambda i: (0, i)),
                   full((N_EXPERTS, LANES))],
        out_shape=[jax.ShapeDtypeStruct((TOP_K, T), jnp.int32),
                   jax.ShapeDtypeStruct((TOP_K, T), F32),
                   jax.ShapeDtypeStruct((TOP_K, T), jnp.int32),
                   jax.ShapeDtypeStruct((N_EXPERTS, LANES), F32)],
        compiler_params=pltpu.CompilerParams(
            dimension_semantics=("arbitrary",), vmem_limit_bytes=VMEM_LIMIT),
        name="router_topk",
    )(h2p, r_lo, r_hi, bias_col)


def _dest_kernel(idx_ref, rank_ref, pstart_ref, dest_ref):
    tr = idx_ref.shape[1]
    ei = lax.broadcasted_iota(jnp.int32, (N_EXPERTS, tr), 0)
    start = pstart_ref[:, 0:1]
    rows = []
    for k in range(TOP_K):
        hit = ei == idx_ref[k:k + 1, :]
        rows.append(jnp.sum(jnp.where(hit, start, 0.0), axis=0, keepdims=True))
    dest_ref[...] = jnp.concatenate(rows, axis=0).astype(jnp.int32) + rank_ref[...]


def _slot_index(idx, rank, pstart_col):
    T = idx.shape[1]
    tr = 1024
    return pl.pallas_call(
        _dest_kernel,
        grid=(T // tr,),
        in_specs=[pl.BlockSpec((TOP_K, tr), lambda i: (0, i)),
                  pl.BlockSpec((TOP_K, tr), lambda i: (0, i)),
                  pl.BlockSpec((N_EXPERTS, LANES), lambda i: (0, 0))],
        out_specs=pl.BlockSpec((TOP_K, tr), lambda i: (0, i)),
        out_shape=jax.ShapeDtypeStruct((TOP_K, T), jnp.int32),
        name="slot_index",
    )(idx, rank, pstart_col)


def _ffn_kernel(first_ref, nblk_ref, nused_ref, xs_hbm, wg_ref, wu_ref, wd_ref, ys_hbm,
                xbuf, ybuf, in_sem, out_sem, wg_s, wu_s, wd_s):
    e = pl.program_id(0)
    bm = EXPERT_BLOCK
    ns = EXPERT_SLOTS
    nused = nused_ref[0]
    first = first_ref[e]
    n = nblk_ref[e]

    def in_copy(g):
        slot = g % ns
        return pltpu.make_async_copy(xs_hbm.at[pl.ds(g * bm, bm)], xbuf.at[slot], in_sem.at[slot])

    def out_copy(g):
        slot = g % ns
        return pltpu.make_async_copy(ybuf.at[slot], ys_hbm.at[pl.ds(g * bm, bm)], out_sem.at[slot])

    def fetch(g):
        @pl.when(g < nused)
        def _():
            in_copy(g).start()

    def release(g):
        @pl.when(g >= ns)
        def _():
            out_copy(g - ns).wait()

    def ffn(g):
        lo, hi = _unpack_pair(xbuf[g % ns])
        x = jnp.concatenate([lo.astype(BF16), hi.astype(BF16)], axis=1)
        gate = jnp.dot(x, wg_s[...], preferred_element_type=F32)
        up = jnp.dot(x, wu_s[...], preferred_element_type=F32)
        hid = (_silu(gate) * up).astype(BF16)
        return jnp.dot(hid, wd_s[...], preferred_element_type=F32)

    def pack(g, out):
        ybuf[g % ns] = _pack_pair(out[:, :HALF], out[:, HALF:])

    @pl.when(e == 0)
    def _():
        for q in range(ns - 1):
            fetch(q)

    @pl.when(n > 0)
    def _():
        wg_s[...] = pltpu.bitcast(wg_ref[0], BF16)
        wu_s[...] = pltpu.bitcast(wu_ref[0], BF16)
        wd_s[...] = pltpu.bitcast(wd_ref[0], BF16)

        def two_blocks(j, _):
            g = first + 2 * j
            in_copy(g).wait()
            in_copy(g + 1).wait()
            fetch(g + ns - 1)
            release(g)
            release(g + 1)
            out_a = ffn(g)
            out_b = ffn(g + 1)
            pack(g, out_a)
            pack(g + 1, out_b)
            out_copy(g).start()
            out_copy(g + 1).start()
            fetch(g + ns)
            return 0

        lax.fori_loop(0, n // 2, two_blocks, 0)

        @pl.when(n % 2 == 1)
        def _():
            g = first + n - 1
            in_copy(g).wait()
            fetch(g + ns - 1)
            release(g)
            pack(g, ffn(g))
            out_copy(g).start()

    @pl.when(e == pl.num_programs(0) - 1)
    def _():
        for q in range(ns, 0, -1):
            @pl.when(nused >= q)
            def _(q=q):
                out_copy(nused - q).wait()


def _expert_ffn(first_blk, nblk, nused, xs, w_gate, w_up, w_down):
    P = xs.shape[0]
    bm = EXPERT_BLOCK
    w_map = lambda e, *_: (e, 0, 0)
    grid_spec = pltpu.PrefetchScalarGridSpec(
        num_scalar_prefetch=3,
        grid=(w_gate.shape[0],),
        in_specs=[pl.BlockSpec(memory_space=pl.ANY),
                  pl.BlockSpec((1, D_MODEL // 2, EXPERT_FF), w_map),
                  pl.BlockSpec((1, D_MODEL // 2, EXPERT_FF), w_map),
                  pl.BlockSpec((1, EXPERT_FF // 2, D_MODEL), w_map)],
        out_specs=pl.BlockSpec(memory_space=pl.ANY),
        scratch_shapes=[pltpu.VMEM((EXPERT_SLOTS, bm, HALF), jnp.uint32),
                        pltpu.VMEM((EXPERT_SLOTS, bm, HALF), jnp.uint32),
                        pltpu.SemaphoreType.DMA((EXPERT_SLOTS,)),
                        pltpu.SemaphoreType.DMA((EXPERT_SLOTS,)),
                        pltpu.VMEM((D_MODEL, EXPERT_FF), BF16),
                        pltpu.VMEM((D_MODEL, EXPERT_FF), BF16),
                        pltpu.VMEM((EXPERT_FF, D_MODEL), BF16)],
    )
    return pl.pallas_call(
        _ffn_kernel,
        grid_spec=grid_spec,
        out_shape=jax.ShapeDtypeStruct((P, HALF), jnp.uint32),
        compiler_params=pltpu.CompilerParams(
            dimension_semantics=("arbitrary",), vmem_limit_bytes=VMEM_LIMIT),
        name="routed_experts",
    )(first_blk, nblk, nused, xs, w_gate, w_up, w_down)


def _final_kernel(yg_ref, w_ref, h2_ref, x1_ref, mod_ref, wsg_ref, wsu_ref, wsd_ref, gpost_ref, *rest):
    o_ref = rest[-1]
    lo, hi = _unpack_pair(h2_ref[...])
    h2 = jnp.concatenate([lo.astype(BF16), hi.astype(BF16)], axis=1)
    gate = jnp.dot(h2, wsg_ref[...], preferred_element_type=F32)
    up = jnp.dot(h2, wsu_ref[...], preferred_element_type=F32)
    shared = jnp.dot((_silu(gate) * up).astype(BF16), wsd_ref[...], preferred_element_type=F32)
    y_lo = shared[:, :HALF]
    y_hi = shared[:, HALF:]
    for k in range(TOP_K):
        r_lo, r_hi = _unpack_pair(yg_ref[k])
        wk = w_ref[:, k:k + 1]
        y_lo = y_lo + wk * r_lo
        y_hi = y_hi + wk * r_hi
    ms = (jnp.sum(y_lo * y_lo, axis=-1, keepdims=True)
          + jnp.sum(y_hi * y_hi, axis=-1, keepdims=True)) * (1.0 / D_MODEL)
    inv = lax.rsqrt(ms + NORM_EPS)
    o_ref[:, 0:HALF] = x1_ref[:, 0:HALF] + mod_ref[0, 5:6, 0:HALF] * (y_lo * inv * gpost_ref[:, 0:HALF])
    o_ref[:, HALF:] = x1_ref[:, HALF:] + mod_ref[0, 5:6, HALF:] * (y_hi * inv * gpost_ref[:, HALF:])


def _final(yg, w_tk, h2p, x1, mod3, wsg, wsu, wsd, g_post, seq, row0, out_prev):
    T = x1.shape[0]
    tp = yg.shape[1]
    tm = 512
    per_b = seq // tm
    off = row0 // tm
    full = lambda shape: pl.BlockSpec(shape, lambda i: (0,) * len(shape))
    in_specs = [pl.BlockSpec((TOP_K, tm, HALF), lambda i: (0, i, 0)),
                pl.BlockSpec((tm, TOP_K), lambda i: (i, 0)),
                pl.BlockSpec((tm, HALF), lambda i: (i + off, 0)),
                pl.BlockSpec((tm, D_MODEL), lambda i: (i + off, 0)),
                pl.BlockSpec((1, 6, D_MODEL), lambda i: ((i + off) // per_b, 0, 0)),
                full((D_MODEL, EXPERT_FF)), full((D_MODEL, EXPERT_FF)), full((EXPERT_FF, D_MODEL)),
                full((1, D_MODEL))]
    args = [yg, w_tk, h2p, x1, mod3, wsg, wsu, wsd, g_post]
    aliases = {}
    if out_prev is not None:
        in_specs.append(pl.BlockSpec(memory_space=pl.ANY))
        args.append(out_prev)
        aliases = {len(args) - 1: 0}
    return pl.pallas_call(
        _final_kernel,
        grid=(tp // tm,),
        in_specs=in_specs,
        out_specs=pl.BlockSpec((tm, D_MODEL), lambda i: (i + off, 0)),
        out_shape=jax.ShapeDtypeStruct((T, D_MODEL), F32),
        input_output_aliases=aliases,
        compiler_params=pltpu.CompilerParams(
            dimension_semantics=("arbitrary",), vmem_limit_bytes=VMEM_LIMIT),
        name="shared_expert_combine",
    )(*args)


def _rope_tables(positions):
    inv = jnp.power(ROPE_THETA, -jnp.arange(ROPE_HALF, dtype=F32) / ROPE_HALF)
    ang = positions.astype(F32)[..., None] * inv
    cos, sin = jnp.cos(ang), jnp.sin(ang)
    rest = ATT_HEAD_DIM - 2 * ROPE_HALF
    cs = jnp.concatenate([cos, cos, jnp.ones(ang.shape[:-1] + (rest,), F32)], axis=-1)
    sn = jnp.concatenate([-sin, sin, jnp.zeros(ang.shape[:-1] + (rest,), F32)], axis=-1)
    return jnp.tile(cs, (1, 1, 2)), jnp.tile(sn, (1, 1, 2))


def _layer(x, c, positions, w_ada, b_ada, g_pre_mix, g_post_mix, g_pre_ffn, g_post_ffn,
           w_in, conv_w, conv_b, b_gates, g_mlstm, w_branch_a, w_branch_b, w_out,
           router_w, router_bias, w_exp_gate, w_exp_up, w_exp_down, w_sh_gate, w_sh_up, w_sh_down):
    B, S, D = x.shape
    T = B * S
    H = MLSTM_HEADS
    x2 = x.reshape(T, D)

    mod3 = _adaln(c, w_ada, b_ada).reshape(B, 6, D)

    a_w = 3 * ATT_GROUP_W
    o_mq = 3 * a_w
    o_mk = o_mq + H * MLSTM_QK_DIM
    o_mv = o_mk + H * MLSTM_QK_DIM
    o_mo = o_mv + H * MLSTM_V_DIM
    o_mi = o_mo + H * MLSTM_V_DIM
    o_ga = o_mi + 2 * H
    o_gb = o_ga + D
    w_bf = w_in.astype(BF16)
    w_main = jnp.concatenate(
        [w_bf[:, o_mv:o_mi], w_bf[:, o_ga:o_gb + D], w_bf[:, o_mq:o_mv], w_bf[:, 0:o_mq]], axis=1)
    w_if = w_bf[:, o_mi:o_ga].T

    proj, gates = _in_proj(x2, mod3, g_pre_mix.reshape(1, D), w_main, w_if, S)
    proj3 = proj.reshape(B, S, PROJ_W)

    wg_p, wu_p, wd_p = (_pack_weight_rows(w, gates) for w in (w_exp_gate, w_exp_up, w_exp_down))

    cs, sn = _rope_tables(positions)
    y_a = _attention(proj3, cs, sn)

    bg_row = jnp.pad(b_gates.reshape(1, 2 * H), ((0, 0), (0, LANES - 2 * H)))
    gates_t = gates.reshape(2 * H, B, S // MLSTM_BLOCK, MLSTM_BLOCK)
    y_b = _mlstm(proj3, gates_t, conv_w, conv_b.reshape(1, -1), bg_row, g_mlstm.reshape(1, -1),
                 wg_p, wu_p)

    x1, h2p = _merge(y_a.reshape(T, ATT_GROUP_W), y_b.reshape(T, D), proj, x2, mod3,
                     w_branch_a.astype(BF16), w_branch_b.astype(BF16), w_out.astype(BF16),
                     g_post_mix.reshape(1, D), g_pre_ffn.reshape(1, D), S, wd_p)

    rw_t = router_w.T.astype(BF16)
    bias_col = jnp.broadcast_to(router_bias.reshape(N_EXPERTS, 1), (N_EXPERTS, LANES))
    wsg, wsu, wsd = w_sh_gate.astype(BF16), w_sh_up.astype(BF16), w_sh_down.astype(BF16)

    tp = T // MOE_PARTS
    bm = EXPERT_BLOCK
    nb = (tp * TOP_K) // bm + N_EXPERTS
    out = None
    for part in range(MOE_PARTS):
        row0 = part * tp
        idx, wts, rank, cnt = _router(h2p, rw_t[:, :HALF], rw_t[:, HALF:], bias_col, row0, tp)

        counts = cnt[:, 0].astype(jnp.int32)
        padded = (counts + bm - 1) // bm * bm
        pend = jnp.cumsum(padded)
        pstart = pend - padded
        pstart_col = jnp.broadcast_to(pstart.astype(F32).reshape(N_EXPERTS, 1), (N_EXPERTS, LANES))
        dest = _slot_index(idx, rank, pstart_col)
        nused = (pend[-1] // bm).astype(jnp.int32).reshape(1)

        xs = _dispatch(h2p, dest, nb * bm, row0)
        ys = _expert_ffn((pstart // bm).astype(jnp.int32), (padded // bm).astype(jnp.int32), nused,
                         xs, wg_p, wu_p, wd_p)
        yg = _collect(ys, dest)
        out = _final(yg, wts.T, h2p, x1, mod3, wsg, wsu, wsd, g_post_ffn.reshape(1, D), S, row0, out)
    return out.reshape(B, S, D)


SC_CORES = 2
SC_SUBCORES = 16
SC_WORKERS = SC_CORES * SC_SUBCORES
SC_ROWS = 64


def _sc_mesh():
    return plsc.VectorSubcoreMesh(core_axis_name="c", subcore_axis_name="s",
                                  num_cores=SC_CORES, num_subcores=SC_SUBCORES)


def _worker_id():
    return lax.axis_index("s") * SC_CORES + lax.axis_index("c")


def _dispatch(h2p, dest, n_slots, row0):
    T = dest.shape[1]
    per_w = T // SC_WORKERS
    nch = per_w // SC_ROWS
    idx = dest.reshape(TOP_K, SC_WORKERS, nch, SC_ROWS).transpose(1, 2, 0, 3)
    idx = idx.reshape(SC_WORKERS, nch * TOP_K, SC_ROWS)

    def body(x_hbm, idx_hbm, xs_hbm, idx_v, buf0, buf1, rsem0, rsem1, ssem0, ssem1):
        wid = _worker_id()
        base = row0 + wid * per_w
        pltpu.sync_copy(idx_hbm.at[wid], idx_v)
        bufs = ((buf0, rsem0, ssem0), (buf1, rsem1, ssem1))

        def read(c, buf, rsem):
            return pltpu.make_async_copy(x_hbm.at[pl.ds(base + c * SC_ROWS, SC_ROWS)], buf, rsem)

        def scatter(c, k, buf, ssem):
            return pltpu.make_async_copy(buf, xs_hbm.at[idx_v.at[c * TOP_K + k]], ssem)

        read(0, buf0, rsem0).start()

        @pl.loop(0, nch, step=2)
        def _(c0):
            for b in range(2):
                c = c0 + b
                buf, rsem, ssem = bufs[b]
                obuf, orsem, ossem = bufs[1 - b]
                read(c, buf, rsem).wait()

                @pl.when(c > 0)
                def _():
                    for k in range(TOP_K):
                        scatter(c - 1, k, obuf, ossem).wait()

                @pl.when(c + 1 < nch)
                def _():
                    read(c + 1, obuf, orsem).start()

                for k in range(TOP_K):
                    scatter(c, k, buf, ssem).start()

        for k in range(TOP_K):
            scatter(nch - 1, k, buf1, ssem1).wait()

    run = pl.kernel(
        body,
        out_type=jax.ShapeDtypeStruct((n_slots, HALF), jnp.uint32),
        mesh=_sc_mesh(),
        scratch_types=[pltpu.VMEM((nch * TOP_K, SC_ROWS), jnp.int32),
                       pltpu.VMEM((SC_ROWS, HALF), jnp.uint32),
                       pltpu.VMEM((SC_ROWS, HALF), jnp.uint32),
                       pltpu.SemaphoreType.DMA, pltpu.SemaphoreType.DMA,
                       pltpu.SemaphoreType.DMA, pltpu.SemaphoreType.DMA],
        name="sc_dispatch",
    )
    return run(h2p, idx)


SC_PACK_ROWS = 64
SC_PACK_COLS = 256
SC_LANES = 16


def _pack_weight_rows(w, after):
    E, R, C = w.shape
    w2 = w.reshape(E * R, C)

    def body(w_hbm, after_hbm, out_hbm):
        del after_hbm

        def block(in_v, out_v):
            @pl.loop(0, SC_PACK_ROWS)
            def _(r):
                @pl.loop(0, SC_PACK_COLS, step=SC_LANES)
                def _(c):
                    cols = pl.ds(c, SC_LANES)
                    pair = plsc.pack(in_v[2 * r, cols], in_v[2 * r + 1, cols],
                                     format=plsc.PackFormat.INTERLEAVED)
                    out_v[r, cols] = plsc.bitcast(pair, jnp.uint32)

        pltpu.emit_pipeline(
            block,
            grid=(E * R // (2 * SC_PACK_ROWS), C // SC_PACK_COLS),
            in_specs=[pl.BlockSpec((2 * SC_PACK_ROWS, SC_PACK_COLS), lambda i, j: (i, j))],
            out_specs=[pl.BlockSpec((SC_PACK_ROWS, SC_PACK_COLS), lambda i, j: (i, j))],
            core_axis_name=("c", "s"),
            dimension_semantics=(pltpu.PARALLEL, pltpu.PARALLEL),
        )(w_hbm, out_hbm)

    run = pl.kernel(body, out_type=jax.ShapeDtypeStruct((E * R // 2, C), jnp.uint32),
                    mesh=_sc_mesh(), scratch_types=[], name="sc_pack_weights",
                    compiler_params=pltpu.CompilerParams(needs_layout_passes=False))
    return run(w2, after).reshape(E, R // 2, C)


def _collect(ys, dest):
    n = dest.size
    per_w = n // SC_WORKERS
    nch = per_w // SC_ROWS
    idx = dest.reshape(SC_WORKERS, nch, SC_ROWS)

    def body(ys_hbm, idx_hbm, out_hbm, idx_v, buf0, buf1, gsem0, gsem1, wsem0, wsem1):
        wid = _worker_id()
        base = wid * per_w
        pltpu.sync_copy(idx_hbm.at[wid], idx_v)
        bufs = ((buf0, gsem0, wsem0), (buf1, gsem1, wsem1))

        def gather(c, buf, gsem):
            return pltpu.make_async_copy(ys_hbm.at[idx_v.at[c]], buf, gsem)

        def write(c, buf, wsem):
            return pltpu.make_async_copy(buf, out_hbm.at[pl.ds(base + c * SC_ROWS, SC_ROWS)], wsem)

        gather(0, buf0, gsem0).start()

        @pl.loop(0, nch, step=2)
        def _(c0):
            for b in range(2):
                c = c0 + b
                buf, gsem, wsem = bufs[b]
                obuf, ogsem, owsem = bufs[1 - b]
                gather(c, buf, gsem).wait()

                @pl.when(c > 0)
                def _():
                    write(c - 1, obuf, owsem).wait()

                @pl.when(c + 1 < nch)
                def _():
                    gather(c + 1, obuf, ogsem).start()

                write(c, buf, wsem).start()

        write(nch - 1, buf1, wsem1).wait()

    run = pl.kernel(
        body,
        out_type=jax.ShapeDtypeStruct((n, HALF), jnp.uint32),
        mesh=_sc_mesh(),
        scratch_types=[pltpu.VMEM((nch, SC_ROWS), jnp.int32),
                       pltpu.VMEM((SC_ROWS, HALF), jnp.uint32),
                       pltpu.VMEM((SC_ROWS, HALF), jnp.uint32),
                       pltpu.SemaphoreType.DMA, pltpu.SemaphoreType.DMA,
                       pltpu.SemaphoreType.DMA, pltpu.SemaphoreType.DMA],
        name="sc_collect",
    )
    return run(ys, idx).reshape(dest.shape + (HALF,))


def kernel(x, c, positions, w_ada, b_ada, g_pre_mix, g_post_mix, g_pre_ffn, g_post_ffn, w_in, conv_w, conv_b, b_gates, g_mlstm, w_branch_a, w_branch_b, w_out, router_w, router_bias, w_exp_gate, w_exp_up, w_exp_down, w_sh_gate, w_sh_up, w_sh_down):
    depth = w_ada.shape[0]
    for l in range(depth):
        x = _layer(x, c, positions, w_ada[l], b_ada[l], g_pre_mix[l], g_post_mix[l], g_pre_ffn[l],
                   g_post_ffn[l], w_in[l], conv_w[l], conv_b[l], b_gates[l], g_mlstm[l],
                   w_branch_a[l], w_branch_b[l], w_out[l], router_w[l], router_bias[l],
                   w_exp_gate[l], w_exp_up[l], w_exp_down[l], w_sh_gate[l], w_sh_up[l], w_sh_down[l])
    return x
```

```python
import functools

import jax
import jax.numpy as jnp
from jax import lax
from jax.experimental import pallas as pl
from jax.experimental.pallas import tpu as pltpu
from jax.experimental.pallas import tpu_sc as plsc

F32 = jnp.float32
BF16 = jnp.bfloat16
HIGHEST = lax.Precision.HIGHEST
LANES = 128

D_MODEL = 1024
ATT_GROUPS = ((128, 1), (512, 4), (2048, 16))
ATT_HEAD_DIM = 64
ATT_GROUP_W = 256
ATT_BLK = 128
ATT_PAIR = 2
ROPE_THETA = 500000.0
ROPE_HALF = 8
MLSTM_HEADS = 4
MLSTM_QK_DIM = 128
MLSTM_V_DIM = 256
MLSTM_BLOCK = 128
MLSTM_GROUP = 16
CONV_WIDTH = 4
N_EXPERTS = 256
TOP_K = 8
N_GROUPS = 8
TOPK_GROUPS = 4
EXPERT_FF = 256
ROUTED_SCALE = 2.5
NORM_EPS = 1e-6
NEG = -1e30

OFF_MV = 0
OFF_MO = OFF_MV + MLSTM_HEADS * MLSTM_V_DIM
OFF_GA = OFF_MO + MLSTM_HEADS * MLSTM_V_DIM
OFF_GB = OFF_GA + D_MODEL
OFF_MQ = OFF_GB + D_MODEL
OFF_MK = OFF_MQ + MLSTM_HEADS * MLSTM_QK_DIM
OFF_AQ = OFF_MK + MLSTM_HEADS * MLSTM_QK_DIM
OFF_AK = OFF_AQ + len(ATT_GROUPS) * ATT_GROUP_W
OFF_AV = OFF_AK + len(ATT_GROUPS) * ATT_GROUP_W
PROJ_W = OFF_AV + len(ATT_GROUPS) * ATT_GROUP_W
HALF = D_MODEL // 2

EXPERT_BLOCK = 512
EXPERT_SLOTS = 6
MOE_PARTS = 2
MERGE_SPLIT = 2
VMEM_LIMIT = 56 * 1024 * 1024


def _nt(a, b):
    return lax.dot_general(a, b, (((1,), (1,)), ((), ())), preferred_element_type=F32)


def _tn(a, b):
    return lax.dot_general(a, b, (((0,), (0,)), ((), ())), preferred_element_type=F32)


_sigmoid = jax.nn.sigmoid


def _silu(x):
    return x * _sigmoid(x)


def _pack_pair(lo, hi):
    lo_b = pltpu.bitcast(lo.astype(BF16).astype(F32), jnp.uint32)
    hi_b = pltpu.bitcast(hi.astype(BF16).astype(F32), jnp.uint32)
    return (lo_b >> 16) | (hi_b & jnp.uint32(0xFFFF0000))


def _unpack_pair(w):
    lo = pltpu.bitcast(w << 16, F32)
    hi = pltpu.bitcast(w & jnp.uint32(0xFFFF0000), F32)
    return lo, hi


def _mod_kernel(c_ref, w_ref, b_ref, o_ref):
    a = _silu(c_ref[...])
    o_ref[...] = jnp.dot(a, w_ref[...], preferred_element_type=F32, precision=HIGHEST) + b_ref[...]


def _adaln(c, w_ada, b_ada):
    B = c.shape[0]
    n = w_ada.shape[1]
    tn = 512
    return pl.pallas_call(
        _mod_kernel,
        grid=(n // tn,),
        in_specs=[pl.BlockSpec((B, D_MODEL), lambda j: (0, 0)),
                  pl.BlockSpec((D_MODEL, tn), lambda j: (0, j)),
                  pl.BlockSpec((1, tn), lambda j: (0, j))],
        out_specs=pl.BlockSpec((B, tn), lambda j: (0, j)),
        out_shape=jax.ShapeDtypeStruct((B, n), F32),
        name="adaln_mod",
    )(c, w_ada, b_ada.reshape(1, n))


def _proj_kernel(x_ref, mod_ref, g_ref, w_ref, wif_ref, o_ref, gates_ref, h_ref):
    @pl.when(pl.program_id(1) == 0)
    def _():
        x = x_ref[...]
        ms = jnp.mean(x * x, axis=-1, keepdims=True)
        y = x * lax.rsqrt(ms + NORM_EPS) * g_ref[...]
        h = (y * (1.0 + mod_ref[0, 1:2, :]) + mod_ref[0, 0:1, :]).astype(BF16)
        h_ref[...] = h
        gates_ref[...] = _nt(wif_ref[...], h)

    o_ref[...] = jnp.dot(h_ref[...], w_ref[...], preferred_element_type=F32).astype(BF16)


def _in_proj(x2, mod3, g_pre, w_main, w_if, seq):
    T = x2.shape[0]
    tm, tn = 1024, PROJ_W // 2
    per_b = seq // tm
    return pl.pallas_call(
        _proj_kernel,
        grid=(T // tm, PROJ_W // tn),
        in_specs=[pl.BlockSpec((tm, D_MODEL), lambda i, j: (i, 0)),
                  pl.BlockSpec((1, 6, D_MODEL), lambda i, j: (i // per_b, 0, 0)),
                  pl.BlockSpec((1, D_MODEL), lambda i, j: (0, 0)),
                  pl.BlockSpec((D_MODEL, tn), lambda i, j: (0, j)),
                  pl.BlockSpec((2 * MLSTM_HEADS, D_MODEL), lambda i, j: (0, 0))],
        out_specs=[pl.BlockSpec((tm, tn), lambda i, j: (i, j)),
                   pl.BlockSpec((2 * MLSTM_HEADS, tm), lambda i, j: (0, i))],
        out_shape=[jax.ShapeDtypeStruct((T, PROJ_W), BF16),
                   jax.ShapeDtypeStruct((2 * MLSTM_HEADS, T), F32)],
        scratch_shapes=[pltpu.VMEM((tm, D_MODEL), BF16)],
        compiler_params=pltpu.CompilerParams(
            dimension_semantics=("arbitrary", "arbitrary"), vmem_limit_bytes=VMEM_LIMIT),
        name="norm_in_proj",
    )(x2, mod3, g_pre, w_main, w_if)


def _attn_kernel(q_ref, k_ref, v_ref, cs_ref, sn_ref, o_ref, qf, kf, vf, acc, m_s, l_s, *, seq):
    g = pl.program_id(1)
    lane = lax.broadcasted_iota(jnp.int32, (ATT_BLK, LANES), 1)
    first = (lane % ATT_HEAD_DIM) < ROPE_HALF
    low_head = lane < ATT_HEAD_DIM

    def rope(x, cs, sn):
        partner = jnp.where(first, pltpu.roll(x, LANES - ROPE_HALF, 1), pltpu.roll(x, ROPE_HALF, 1))
        return x * cs + partner * sn

    @pl.when((pl.program_id(0) == 0) & (g == 0))
    def _():
        def zero_pad(i, _):
            rows = pl.ds(pl.multiple_of(i * ATT_BLK, ATT_BLK), ATT_BLK)
            for hp in range(2):
                kf[hp, rows, :] = jnp.zeros((ATT_BLK, LANES), F32)
                vf[hp, rows, :] = jnp.zeros((ATT_BLK, LANES), F32)
            return 0

        lax.fori_loop(0, seq // ATT_BLK, zero_pad, 0)

    def stage(i, _):
        r = pl.multiple_of(i * ATT_BLK, ATT_BLK)
        rows = pl.ds(r, ATT_BLK)
        prow = pl.ds(pl.multiple_of(seq + i * ATT_BLK, ATT_BLK), ATT_BLK)
        cs = cs_ref[0, rows, :]
        sn = sn_ref[0, rows, :]
        for hp in range(2):
            cols = pl.ds(hp * LANES, LANES)
            qf[hp, rows, :] = rope(q_ref[0, rows, cols].astype(F32), cs, sn) * (ATT_HEAD_DIM ** -0.5)
            kf[hp, prow, :] = rope(k_ref[0, rows, cols].astype(F32), cs, sn)
            vf[hp, prow, :] = v_ref[0, rows, cols].astype(F32)
        return 0

    lax.fori_loop(0, seq // ATT_BLK, stage, 0, unroll=4)

    qi = lax.broadcasted_iota(jnp.int32, (ATT_BLK, 2 * ATT_BLK), 0)
    ki = lax.broadcasted_iota(jnp.int32, (ATT_BLK, 2 * ATT_BLK), 1)
    band = (ki >= qi) & (ki <= qi + ATT_BLK)

    def process(d, init):
        span = ATT_BLK * d
        single = seq == span

        def body(cp, _):
            blocks = [cp * ATT_PAIR + i for i in range(ATT_PAIR)]
            qrows, krows, valid = [], [], []
            for c in blocks:
                rho = c % d
                n = c // d
                qstart = rho + n * span
                if single:
                    kstart, nk = seq + qstart, ATT_BLK
                    valid.append(band[:, ATT_BLK:])
                else:
                    kstart, nk = seq + qstart - span, 2 * ATT_BLK
                    valid.append(band & (ki >= jnp.where(n > 0, 0, ATT_BLK)))
                qrows.append(pl.ds(qstart, ATT_BLK, stride=d) if d > 1 else pl.ds(qstart, ATT_BLK))
                krows.append(pl.ds(kstart, nk, stride=d) if d > 1 else pl.ds(kstart, nk))
            units = [(b, hp) for b in range(ATT_PAIR) for hp in range(2)]
            heads = [(u, hh) for u in range(len(units)) for hh in range(2)]
            q2 = [qf[hp, qrows[b], :] for b, hp in units]
            k2 = [kf[hp, krows[b], :].astype(BF16) for b, hp in units]
            v2 = [vf[hp, krows[b], :].astype(BF16) for b, hp in units]
            qh = [jnp.where(low_head if hh == 0 else jnp.logical_not(low_head), q2[u], 0.0).astype(BF16)
                  for u, hh in heads]
            s = [jnp.where(valid[units[u][0]], _nt(qh[i], k2[u]), NEG) for i, (u, hh) in enumerate(heads)]
            m = [jnp.max(x, axis=1, keepdims=True) for x in s]
            p = [jnp.exp(x - mx) for x, mx in zip(s, m)]
            l = [jnp.sum(x, axis=1, keepdims=True) for x in p]
            o = [jnp.dot(p[i].astype(BF16), v2[u], preferred_element_type=F32)
                 for i, (u, hh) in enumerate(heads)]
            for u, (b, hp) in enumerate(units):
                o_b = jnp.where(low_head, o[2 * u], o[2 * u + 1])
                m_b = jnp.where(low_head, m[2 * u], m[2 * u + 1])
                l_b = jnp.where(low_head, l[2 * u], l[2 * u + 1])
                if init:
                    acc[hp, qrows[b], :] = o_b
                    m_s[hp, qrows[b], :] = m_b
                    l_s[hp, qrows[b], :] = l_b
                else:
                    m_old = m_s[hp, qrows[b], :]
                    m_new = jnp.maximum(m_old, m_b)
                    a_old = jnp.exp(m_old - m_new)
                    a_new = jnp.exp(m_b - m_new)
                    acc[hp, qrows[b], :] = acc[hp, qrows[b], :] * a_old + o_b * a_new
                    l_s[hp, qrows[b], :] = l_s[hp, qrows[b], :] * a_old + l_b * a_new
                    m_s[hp, qrows[b], :] = m_new
            return 0

        lax.fori_loop(0, seq // (ATT_BLK * ATT_PAIR), body, 0, unroll=2)

    for gi, (_, d) in enumerate(ATT_GROUPS):
        @pl.when(g == gi)
        def _(d=d, gi=gi):
            process(d, gi == 0)

    @pl.when(g == len(ATT_GROUPS) - 1)
    def _():
        def fin(i, _):
            rows = pl.ds(pl.multiple_of(i * ATT_BLK, ATT_BLK), ATT_BLK)
            for hp in range(2):
                o_ref[0, rows, pl.ds(hp * LANES, LANES)] = (acc[hp, rows, :] / l_s[hp, rows, :]).astype(BF16)
            return 0

        lax.fori_loop(0, seq // ATT_BLK, fin, 0)


def _attention(proj3, cs, sn):
    B, S, _ = proj3.shape
    ng = len(ATT_GROUPS)
    qb, kb, vb = OFF_AQ // ATT_GROUP_W, OFF_AK // ATT_GROUP_W, OFF_AV // ATT_GROUP_W
    return pl.pallas_call(
        functools.partial(_attn_kernel, seq=S),
        grid=(B, ng),
        in_specs=[pl.BlockSpec((1, S, ATT_GROUP_W), lambda b, g: (b, 0, qb + g)),
                  pl.BlockSpec((1, S, ATT_GROUP_W), lambda b, g: (b, 0, kb + g)),
                  pl.BlockSpec((1, S, ATT_GROUP_W), lambda b, g: (b, 0, vb + g)),
                  pl.BlockSpec((1, S, LANES), lambda b, g: (b, 0, 0)),
                  pl.BlockSpec((1, S, LANES), lambda b, g: (b, 0, 0))],
        out_specs=pl.BlockSpec((1, S, ATT_GROUP_W), lambda b, g: (b, 0, 0)),
        out_shape=jax.ShapeDtypeStruct((B, S, ATT_GROUP_W), BF16),
        scratch_shapes=[pltpu.VMEM((2, S, LANES), F32),
                        pltpu.VMEM((2, 2 * S, LANES), F32),
                        pltpu.VMEM((2, 2 * S, LANES), F32),
                        pltpu.VMEM((2, S, LANES), F32),
                        pltpu.VMEM((2, S, LANES), F32),
                        pltpu.VMEM((2, S, LANES), F32)],
        compiler_params=pltpu.CompilerParams(
            dimension_semantics=("arbitrary", "arbitrary"), vmem_limit_bytes=VMEM_LIMIT),
        name="dilated_attention",
    )(proj3, proj3, proj3, cs, sn)


def _log_sigmoid(x):
    return jnp.minimum(x, 0.0) - jnp.log(1.0 + jnp.exp(-jnp.abs(x)))


def _mlstm_kernel(mq_ref, mk_ref, mv_ref, mo_ref, gt_ref, cwq_ref, cwk_ref, cbq_ref, cbk_ref,
                  bg_ref, gm_ref, anchor_a, anchor_b, o_ref, q_s, k_s, va_s, rows_s, acc_s, kv_s,
                  inter_s, emt_s, c_s, *, seq):
    del anchor_a, anchor_b
    h = pl.program_id(1)
    L = MLSTM_BLOCK
    NC = seq // L
    DK, DV = MLSTM_QK_DIM, MLSTM_V_DIM
    DA = DV + LANES
    nshift = CONV_WIDTH - 1

    tt = lax.broadcasted_iota(jnp.int32, (nshift * L, 2 * L), 0)
    uu = lax.broadcasted_iota(jnp.int32, (nshift * L, 2 * L), 1)
    shift_mat = (uu == L + tt % L - (tt // L + 1)).astype(BF16)
    conv_w = jnp.concatenate([cwq_ref[...], cwk_ref[...]], axis=1)
    conv_b = jnp.concatenate([cbq_ref[...], cbk_ref[...]], axis=1)
    prev = jnp.zeros((L, 2 * DK), BF16)
    for i in range(NC):
        blk = slice(i * L, (i + 1) * L)
        va_s[blk, 0:DV] = mv_ref[0, blk, :]
        va_s[blk, DV:DA] = jnp.ones((L, DA - DV), BF16)
        cur = jnp.concatenate([mq_ref[0, blk, :], mk_ref[0, blk, :]], axis=1)
        shifted = jnp.dot(shift_mat, jnp.concatenate([prev, cur], axis=0),
                          preferred_element_type=F32)
        y = conv_b + cur.astype(F32) * conv_w[nshift:nshift + 1, :]
        for s in range(nshift):
            y = y + shifted[s * L:(s + 1) * L, :] * conv_w[nshift - 1 - s:nshift - s, :]
        y = _silu(y)
        q_s[blk, :] = y[:, 0:DK].astype(BF16)
        k_s[blk, :] = (y[:, DK:2 * DK] * (DK ** -0.5)).astype(BF16)
        prev = cur

    lane = lax.broadcasted_iota(jnp.int32, (1, LANES), 1)
    bias = bg_ref[...]
    b_i = jnp.sum(jnp.where(lane == h, bias, 0.0), axis=1, keepdims=True)
    b_f = jnp.sum(jnp.where(lane == h + MLSTM_HEADS, bias, 0.0), axis=1, keepdims=True)
    ri = lax.broadcasted_iota(jnp.int32, (L, L), 0)
    ci = lax.broadcasted_iota(jnp.int32, (L, L), 1)
    causal = ci <= ri
    eye = (ri == ci).astype(F32)
    i_rows = gt_ref[h, 0] + b_i
    lf_rows = _log_sigmoid(gt_ref[h + MLSTM_HEADS, 0] + b_f)
    b_rows = jnp.dot(lf_rows, (ri <= ci).astype(F32), preferred_element_type=F32,
                     precision=HIGHEST)
    b_end = b_rows[:, L - 1:L]
    g_rows = b_end - b_rows + i_rows
    g_max = jnp.max(g_rows, axis=1, keepdims=True)
    m = jnp.zeros((1, 1), F32)
    m_prev, m_new = [], []
    for c in range(NC):
        m_prev.append(m)
        m = jnp.maximum(b_end[c:c + 1, :] + m, g_max[c:c + 1, :])
        m_new.append(m)
    m_prev = jnp.concatenate(m_prev, axis=0)
    m_new = jnp.concatenate(m_new, axis=0)
    rows_s[0] = b_rows
    rows_s[1] = jnp.exp(g_rows - m_new)
    rows_s[2] = b_rows - i_rows
    rows_s[3] = jnp.broadcast_to(m_prev, (NC, L))
    rows_s[4] = jnp.broadcast_to(jnp.exp(b_end + m_prev - m_new), (NC, L))

    r2 = lax.broadcasted_iota(jnp.int32, (2 * L, 2 * L), 0)
    c2 = lax.broadcasted_iota(jnp.int32, (2 * L, 2 * L), 1)
    ones_blk = ((r2 < L) == (c2 < L)).astype(BF16)

    G = MLSTM_GROUP

    def local(cg, _):
        cs = [cg * G + i for i in range(G)]
        rows = [pl.ds(pl.multiple_of(c * L, L), L) for c in cs]
        b_r = [rows_s[0, pl.ds(c, 1), :] for c in cs]
        w_r = [rows_s[1, pl.ds(c, 1), :] for c in cs]
        u_r = [rows_s[2, pl.ds(c, 1), :] for c in cs]
        mp = [rows_s[3, pl.ds(c, 1), :] for c in cs]
        q = [q_s[r, :] for r in rows]
        k = [k_s[r, :] for r in rows]
        va = [va_s[r, :] for r in rows]
        qk = [_nt(a, b) for a, b in zip(q, k)]
        x2 = [jnp.concatenate([eye * a, eye * b], axis=1) for a, b in zip(b_r, w_r)]
        hi = [x.astype(BF16) for x in x2]
        lo = [(x - h_.astype(F32)).astype(BF16) for x, h_ in zip(x2, hi)]
        yb = [jnp.dot(h_, ones_blk, preferred_element_type=F32)
              + jnp.dot(l_, ones_blk, preferred_element_type=F32) for h_, l_ in zip(hi, lo)]
        b_b = [y[:, 0:L] for y in yb]
        w_b = [y[:, L:2 * L] for y in yb]
        for i in range(G):
            kv_s[cs[i]] = _tn((w_b[i] * k[i].astype(F32)).astype(BF16), va[i])
        dmat = [jnp.where(causal, b - u, NEG) for b, u in zip(b_b, u_r)]
        m_t = [jnp.maximum(b + m_, jnp.max(d, axis=1, keepdims=True))
               for b, m_, d in zip(b_b, mp, dmat)]
        sc = [a * jnp.exp(d - m_) for a, d, m_ in zip(qk, dmat, m_t)]
        for i in range(G):
            acc_s[rows[i], :] = jnp.dot(sc[i].astype(BF16), va[i], preferred_element_type=F32)
            inter_s[rows[i], :] = jnp.exp(b_b[i] + mp[i] - m_t[i])
            emt_s[rows[i], :] = jnp.exp(-m_t[i])
        return 0

    lax.fori_loop(0, NC // G, local, 0)

    g_row = gm_ref[...]
    c_s[...] = jnp.zeros((DK, DA), F32)

    def recur(cg, _):
        cs = [cg * G + i for i in range(G)]
        rows = [pl.ds(pl.multiple_of(c * L, L), L) for c in cs]
        states = [c_s[...]]
        for c in cs:
            dec = rows_s[4, pl.ds(c, 1), :]
            states.append(jnp.concatenate([dec, dec, dec], axis=1) * states[-1] + kv_s[c])
        c_s[...] = states[G]
        read = [jnp.dot(q_s[r, :], st.astype(BF16), preferred_element_type=F32)
                for r, st in zip(rows, states)]
        inter = [inter_s[r, :] for r in rows]
        out = [acc_s[r, :] + jnp.concatenate([it, it, it], axis=1) * rd
               for r, it, rd in zip(rows, inter, read)]
        emt = [emt_s[r, :] for r in rows]
        nrm = [jnp.maximum(jnp.abs(jnp.concatenate([o[:, DV:DA], o[:, DV:DA]], axis=1)),
                           jnp.concatenate([e_, e_], axis=1)) for o, e_ in zip(out, emt)]
        hh = [o[:, 0:DV] / n_ for o, n_ in zip(out, nrm)]
        ms = [jnp.mean(x * x, axis=1, keepdims=True) for x in hh]
        hn = [x * lax.rsqrt(m_ + NORM_EPS) * g_row for x, m_ in zip(hh, ms)]
        for i in range(G):
            o_ref[0, rows[i], :] = (hn[i] * _sigmoid(mo_ref[0, rows[i], :].astype(F32))).astype(BF16)
        return 0

    lax.fori_loop(0, NC // G, recur, 0)


def _mlstm(proj3, gates_t, conv_w, conv_b, bg_row, g_mlstm, anchor_a, anchor_b):
    B, S, _ = proj3.shape
    H, DK, DV = MLSTM_HEADS, MLSTM_QK_DIM, MLSTM_V_DIM
    L = MLSTM_BLOCK
    NC = S // L
    DA = DV + LANES
    qb, kb = OFF_MQ // DK, OFF_MK // DK
    vb, ob = OFF_MV // DV, OFF_MO // DV
    nq = H
    return pl.pallas_call(
        functools.partial(_mlstm_kernel, seq=S),
        grid=(B, H),
        in_specs=[pl.BlockSpec((1, S, DK), lambda b, h: (b, 0, qb + h)),
                  pl.BlockSpec((1, S, DK), lambda b, h: (b, 0, kb + h)),
                  pl.BlockSpec((1, S, DV), lambda b, h: (b, 0, vb + h)),
                  pl.BlockSpec((1, S, DV), lambda b, h: (b, 0, ob + h)),
                  pl.BlockSpec((2 * H, 1, NC, L), lambda b, h: (0, b, 0, 0)),
                  pl.BlockSpec((CONV_WIDTH, DK), lambda b, h: (0, h)),
                  pl.BlockSpec((CONV_WIDTH, DK), lambda b, h: (0, nq + h)),
                  pl.BlockSpec((1, DK), lambda b, h: (0, h)),
                  pl.BlockSpec((1, DK), lambda b, h: (0, nq + h)),
                  pl.BlockSpec((1, LANES), lambda b, h: (0, 0)),
                  pl.BlockSpec((1, DV), lambda b, h: (0, h)),
                  pl.BlockSpec(memory_space=pl.ANY), pl.BlockSpec(memory_space=pl.ANY)],
        out_specs=pl.BlockSpec((1, S, DV), lambda b, h: (b, 0, h)),
        out_shape=jax.ShapeDtypeStruct((B, S, H * DV), BF16),
        scratch_shapes=[pltpu.VMEM((S, DK), BF16),
                        pltpu.VMEM((S, DK), BF16),
                        pltpu.VMEM((S, DA), BF16),
                        pltpu.VMEM((5, NC, L), F32),
                        pltpu.VMEM((S, DA), F32),
                        pltpu.VMEM((NC, DK, DA), F32),
                        pltpu.VMEM((S, L), F32),
                        pltpu.VMEM((S, L), F32),
                        pltpu.VMEM((DK, DA), F32)],
        compiler_params=pltpu.CompilerParams(
            dimension_semantics=("arbitrary", "arbitrary"), vmem_limit_bytes=VMEM_LIMIT),
        name="mlstm_chunkwise",
    )(proj3, proj3, proj3, proj3, gates_t, conv_w, conv_w, conv_b, conv_b, bg_row, g_mlstm,
      anchor_a, anchor_b)


def _rms(y, g):
    ms = jnp.mean(y * y, axis=-1, keepdims=True)
    return y * lax.rsqrt(ms + NORM_EPS) * g


def _merge_kernel(ya_ref, yb_ref, ga_ref, gb_ref, x_ref, mod_ref, wa_ref, wb_ref, wo_ref,
                  gpost_ref, gpre_ref, anchor_ref, x1_ref, h2_ref):
    del anchor_ref
    tm = x_ref.shape[0]
    slabs = [pl.ds(s * (tm // MERGE_SPLIT), tm // MERGE_SPLIT) for s in range(MERGE_SPLIT)]
    pa = [jnp.dot(ya_ref[r, :], wa_ref[...], preferred_element_type=F32) for r in slabs]
    pb = [jnp.dot(yb_ref[r, :], wb_ref[...], preferred_element_type=F32) for r in slabs]
    merged = [_sigmoid(ga_ref[r, :].astype(F32)) * a + _sigmoid(gb_ref[r, :].astype(F32)) * b
              for r, a, b in zip(slabs, pa, pb)]
    y = [jnp.dot(m.astype(BF16), wo_ref[...], preferred_element_type=F32) for m in merged]
    x1 = [x_ref[r, :] + mod_ref[0, 2:3, :] * _rms(v, gpost_ref[...]) for r, v in zip(slabs, y)]
    for r, v in zip(slabs, x1):
        x1_ref[r, :] = v
    h2 = [_rms(v, gpre_ref[...]) * (1.0 + mod_ref[0, 4:5, :]) + mod_ref[0, 3:4, :] for v in x1]
    for r, v in zip(slabs, h2):
        h2_ref[r, :] = _pack_pair(v[:, :HALF], v[:, HALF:])


def _merge(ya2, yb2, proj2, x2, mod3, wa, wb, wo, g_post, g_pre, seq, anchor):
    T = x2.shape[0]
    tm = 512 * MERGE_SPLIT
    per_b = seq // tm
    full = lambda shape: pl.BlockSpec(shape, lambda i: (0,) * len(shape))
    return pl.pallas_call(
        _merge_kernel,
        grid=(T // tm,),
        in_specs=[pl.BlockSpec((tm, ATT_GROUP_W), lambda i: (i, 0)),
                  pl.BlockSpec((tm, D_MODEL), lambda i: (i, 0)),
                  pl.BlockSpec((tm, D_MODEL), lambda i: (i, OFF_GA // D_MODEL)),
                  pl.BlockSpec((tm, D_MODEL), lambda i: (i, OFF_GB // D_MODEL)),
                  pl.BlockSpec((tm, D_MODEL), lambda i: (i, 0)),
                  pl.BlockSpec((1, 6, D_MODEL), lambda i: (i // per_b, 0, 0)),
                  full((ATT_GROUP_W, D_MODEL)), full((D_MODEL, D_MODEL)), full((D_MODEL, D_MODEL)),
                  full((1, D_MODEL)), full((1, D_MODEL)),
                  pl.BlockSpec(memory_space=pl.ANY)],
        out_specs=[pl.BlockSpec((tm, D_MODEL), lambda i: (i, 0)),
                   pl.BlockSpec((tm, HALF), lambda i: (i, 0))],
        out_shape=[jax.ShapeDtypeStruct((T, D_MODEL), F32),
                   jax.ShapeDtypeStruct((T, HALF), jnp.uint32)],
        compiler_params=pltpu.CompilerParams(
            dimension_semantics=("arbitrary",), vmem_limit_bytes=VMEM_LIMIT),
        name="merge_out_proj",
    )(ya2, yb2, proj2, proj2, x2, mod3, wa, wb, wo, g_post, g_pre, anchor)


def _router_kernel(h2_ref, rlo_ref, rhi_ref, bias_ref, idx_ref, w_ref, rank_ref, cnt_ref):
    E = N_EXPERTS
    tr = h2_ref.shape[0]
    gsz = E // N_GROUPS

    @pl.when(pl.program_id(0) == 0)
    def _():
        cnt_ref[...] = jnp.zeros(cnt_ref.shape, F32)

    lo, hi = _unpack_pair(h2_ref[...])
    logits = _nt(rlo_ref[...], lo.astype(BF16)) + _nt(rhi_ref[...], hi.astype(BF16))
    scores = _sigmoid(logits)
    sel = scores + bias_ref[:, 0:1]

    gi = lax.broadcasted_iota(jnp.int32, (gsz, tr), 0).astype(F32)
    gs_rows = []
    for g in range(N_GROUPS):
        blk = sel[g * gsz:(g + 1) * gsz, :]
        m1 = jnp.max(blk, axis=0, keepdims=True)
        a1 = jnp.min(jnp.where(blk == m1, gi, float(E)), axis=0, keepdims=True)
        m2 = jnp.max(jnp.where(gi == a1, -jnp.inf, blk), axis=0, keepdims=True)
        gs_rows.append(m1 + m2)
    gs = jnp.concatenate(gs_rows, axis=0)
    g8 = lax.broadcasted_iota(jnp.int32, (N_GROUPS, tr), 0).astype(F32)
    gmask = jnp.zeros((N_GROUPS, tr), F32)
    for _ in range(TOPK_GROUPS):
        m = jnp.max(gs, axis=0, keepdims=True)
        a = jnp.min(jnp.where(gs == m, g8, float(E)), axis=0, keepdims=True)
        hit = g8 == a
        gmask = jnp.where(hit, 1.0, gmask)
        gs = jnp.where(hit, -jnp.inf, gs)
    selm = jnp.concatenate(
        [jnp.where(gmask[g:g + 1, :] > 0.0, sel[g * gsz:(g + 1) * gsz, :], -jnp.inf)
         for g in range(N_GROUPS)], axis=0)

    ei = lax.broadcasted_iota(jnp.int32, (E, tr), 0).astype(F32)
    picks, weights, hits = [], [], []
    candidates = selm
    for _ in range(TOP_K):
        m = jnp.max(selm, axis=0, keepdims=True)
        a = jnp.min(jnp.where(selm == m, ei, float(E)), axis=0, keepdims=True)
        hit = ei == a
        picks.append(a)
        hits.append(hit)
        weights.append(jnp.sum(jnp.where(hit, scores, 0.0), axis=0, keepdims=True))
        selm = jnp.where(hit, -jnp.inf, selm)
    chosen = jnp.where(selm != candidates, 1.0, 0.0)
    wsum = weights[0]
    for w in weights[1:]:
        wsum = wsum + w

    ti = lax.broadcasted_iota(jnp.int32, (tr, tr), 0)
    tj = lax.broadcasted_iota(jnp.int32, (tr, tr), 1)
    before = (ti < tj).astype(BF16)
    pos = jnp.dot(chosen.astype(BF16), before, preferred_element_type=F32) + cnt_ref[:, 0:1]
    ranks = [jnp.sum(jnp.where(hit, pos, 0.0), axis=0, keepdims=True) for hit in hits]
    cnt_ref[...] = cnt_ref[...] + jnp.sum(chosen, axis=1, keepdims=True)

    idx_ref[...] = jnp.concatenate(picks, axis=0).astype(jnp.int32)
    w_ref[...] = jnp.concatenate([w / wsum * ROUTED_SCALE for w in weights], axis=0)
    rank_ref[...] = jnp.concatenate(ranks, axis=0).astype(jnp.int32)


def _router(h2p, r_lo, r_hi, bias_col, row0, T):
    tr = 512
    off = row0 // tr
    full = lambda shape: pl.BlockSpec(shape, lambda i: (0,) * len(shape))
    return pl.pallas_call(
        _router_kernel,
        grid=(T // tr,),
        in_specs=[pl.BlockSpec((tr, HALF), lambda i: (i + off, 0)),
                  full((N_EXPERTS, HALF)), full((N_EXPERTS, HALF)), full((N_EXPERTS, LANES))],
        out_specs=[pl.BlockSpec((TOP_K, tr), lambda i: (0, i)),
                   pl.BlockSpec((TOP_K, tr), lambda i: (0, i)),
                   pl.BlockSpec((TOP_K, tr), lambda i: (0, i)),
                   full((N_EXPERTS, LANES))],
        out_shape=[jax.ShapeDtypeStruct((TOP_K, T), jnp.int32),
                   jax.ShapeDtypeStruct((TOP_K, T), F32),
                   jax.ShapeDtypeStruct((TOP_K, T), jnp.int32),
                   jax.ShapeDtypeStruct((N_EXPERTS, LANES), F32)],
        compiler_params=pltpu.CompilerParams(
            dimension_semantics=("arbitrary",), vmem_limit_bytes=VMEM_LIMIT),
        name="router_topk",
    )(h2p, r_lo, r_hi, bias_col)


def _dest_kernel(idx_ref, rank_ref, pstart_ref, dest_ref):
    tr = idx_ref.shape[1]
    ei = lax.broadcasted_iota(jnp.int32, (N_EXPERTS, tr), 0)
    start = pstart_ref[:, 0:1]
    rows = []
    for k in range(TOP_K):
        hit = ei == idx_ref[k:k + 1, :]
        rows.append(jnp.sum(jnp.where(hit, start, 0.0), axis=0, keepdims=True))
    dest_ref[...] = jnp.concatenate(rows, axis=0).astype(jnp.int32) + rank_ref[...]


def _slot_index(idx, rank, pstart_col):
    T = idx.shape[1]
    tr = 1024
    return pl.pallas_call(
        _dest_kernel,
        grid=(T // tr,),
        in_specs=[pl.BlockSpec((TOP_K, tr), lambda i: (0, i)),
                  pl.BlockSpec((TOP_K, tr), lambda i: (0, i)),
                  pl.BlockSpec((N_EXPERTS, LANES), lambda i: (0, 0))],
        out_specs=pl.BlockSpec((TOP_K, tr), lambda i: (0, i)),
        out_shape=jax.ShapeDtypeStruct((TOP_K, T), jnp.int32),
        name="slot_index",
    )(idx, rank, pstart_col)


def _ffn_kernel(first_ref, nblk_ref, nused_ref, xs_hbm, wg_ref, wu_ref, wd_ref, ys_hbm,
                xbuf, ybuf, in_sem, out_sem):
    e = pl.program_id(0)
    bm = EXPERT_BLOCK
    ns = EXPERT_SLOTS
    nused = nused_ref[0]
    first = first_ref[e]
    n = nblk_ref[e]

    def in_copy(g):
        slot = g % ns
        return pltpu.make_async_copy(xs_hbm.at[pl.ds(g * bm, bm)], xbuf.at[slot], in_sem.at[slot])

    def out_copy(g):
        slot = g % ns
        return pltpu.make_async_copy(ybuf.at[slot], ys_hbm.at[pl.ds(g * bm, bm)], out_sem.at[slot])

    def fetch(g):
        @pl.when(g < nused)
        def _():
            in_copy(g).start()

    def release(g):
        @pl.when(g >= ns)
        def _():
            out_copy(g - ns).wait()

    def ffn(g):
        lo, hi = _unpack_pair(xbuf[g % ns])
        x = jnp.concatenate([lo.astype(BF16), hi.astype(BF16)], axis=1)
        gate = jnp.dot(x, pltpu.bitcast(wg_ref[0], BF16), preferred_element_type=F32)
        up = jnp.dot(x, pltpu.bitcast(wu_ref[0], BF16), preferred_element_type=F32)
        hid = (_silu(gate) * up).astype(BF16)
        return jnp.dot(hid, pltpu.bitcast(wd_ref[0], BF16), preferred_element_type=F32)

    def pack(g, out):
        ybuf[g % ns] = _pack_pair(out[:, :HALF], out[:, HALF:])

    @pl.when(e == 0)
    def _():
        for q in range(ns - 1):
            fetch(q)

    @pl.when(n > 0)
    def _():
        def two_blocks(j, _):
            g = first + 2 * j
            in_copy(g).wait()
            in_copy(g + 1).wait()
            fetch(g + ns - 1)
            release(g)
            release(g + 1)
            out_a = ffn(g)
            out_b = ffn(g + 1)
            pack(g, out_a)
            pack(g + 1, out_b)
            out_copy(g).start()
            out_copy(g + 1).start()
            fetch(g + ns)
            return 0

        lax.fori_loop(0, n // 2, two_blocks, 0)

        @pl.when(n % 2 == 1)
        def _():
            g = first + n - 1
            in_copy(g).wait()
            fetch(g + ns - 1)
            release(g)
            pack(g, ffn(g))
            out_copy(g).start()

    @pl.when(e == pl.num_programs(0) - 1)
    def _():
        for q in range(ns, 0, -1):
            @pl.when(nused >= q)
            def _(q=q):
                out_copy(nused - q).wait()


def _expert_ffn(first_blk, nblk, nused, xs, w_gate, w_up, w_down):
    P = xs.shape[0]
    bm = EXPERT_BLOCK
    w_map = lambda e, *_: (e, 0, 0)
    grid_spec = pltpu.PrefetchScalarGridSpec(
        num_scalar_prefetch=3,
        grid=(w_gate.shape[0],),
        in_specs=[pl.BlockSpec(memory_space=pl.ANY),
                  pl.BlockSpec((1, D_MODEL // 2, EXPERT_FF), w_map),
                  pl.BlockSpec((1, D_MODEL // 2, EXPERT_FF), w_map),
                  pl.BlockSpec((1, EXPERT_FF // 2, D_MODEL), w_map)],
        out_specs=pl.BlockSpec(memory_space=pl.ANY),
        scratch_shapes=[pltpu.VMEM((EXPERT_SLOTS, bm, HALF), jnp.uint32),
                        pltpu.VMEM((EXPERT_SLOTS, bm, HALF), jnp.uint32),
                        pltpu.SemaphoreType.DMA((EXPERT_SLOTS,)),
                        pltpu.SemaphoreType.DMA((EXPERT_SLOTS,))],
    )
    return pl.pallas_call(
        _ffn_kernel,
        grid_spec=grid_spec,
        out_shape=jax.ShapeDtypeStruct((P, HALF), jnp.uint32),
        compiler_params=pltpu.CompilerParams(
            dimension_semantics=("arbitrary",), vmem_limit_bytes=VMEM_LIMIT),
        name="routed_experts",
    )(first_blk, nblk, nused, xs, w_gate, w_up, w_down)


def _final_kernel(yg_ref, w_ref, h2_ref, x1_ref, mod_ref, wsg_ref, wsu_ref, wsd_ref, gpost_ref, *rest):
    o_ref = rest[-1]
    lo, hi = _unpack_pair(h2_ref[...])
    h2 = jnp.concatenate([lo.astype(BF16), hi.astype(BF16)], axis=1)
    gate = jnp.dot(h2, wsg_ref[...], preferred_element_type=F32)
    up = jnp.dot(h2, wsu_ref[...], preferred_element_type=F32)
    shared = jnp.dot((_silu(gate) * up).astype(BF16), wsd_ref[...], preferred_element_type=F32)
    y_lo = shared[:, :HALF]
    y_hi = shared[:, HALF:]
    for k in range(TOP_K):
        r_lo, r_hi = _unpack_pair(yg_ref[k])
        wk = w_ref[:, k:k + 1]
        y_lo = y_lo + wk * r_lo
        y_hi = y_hi + wk * r_hi
    ms = (jnp.sum(y_lo * y_lo, axis=-1, keepdims=True)
          + jnp.sum(y_hi * y_hi, axis=-1, keepdims=True)) * (1.0 / D_MODEL)
    inv = lax.rsqrt(ms + NORM_EPS)
    o_ref[:, 0:HALF] = x1_ref[:, 0:HALF] + mod_ref[0, 5:6, 0:HALF] * (y_lo * inv * gpost_ref[:, 0:HALF])
    o_ref[:, HALF:] = x1_ref[:, HALF:] + mod_ref[0, 5:6, HALF:] * (y_hi * inv * gpost_ref[:, HALF:])


def _final(yg, w_tk, h2p, x1, mod3, wsg, wsu, wsd, g_post, seq, row0, out_prev):
    T = x1.shape[0]
    tp = yg.shape[1]
    tm = 512
    per_b = seq // tm
    off = row0 // tm
    full = lambda shape: pl.BlockSpec(shape, lambda i: (0,) * len(shape))
    in_specs = [pl.BlockSpec((TOP_K, tm, HALF), lambda i: (0, i, 0)),
                pl.BlockSpec((tm, TOP_K), lambda i: (i, 0)),
                pl.BlockSpec((tm, HALF), lambda i: (i + off, 0)),
                pl.BlockSpec((tm, D_MODEL), lambda i: (i + off, 0)),
                pl.BlockSpec((1, 6, D_MODEL), lambda i: ((i + off) // per_b, 0, 0)),
                full((D_MODEL, EXPERT_FF)), full((D_MODEL, EXPERT_FF)), full((EXPERT_FF, D_MODEL)),
                full((1, D_MODEL))]
    args = [yg, w_tk, h2p, x1, mod3, wsg, wsu, wsd, g_post]
    aliases = {}
    if out_prev is not None:
        in_specs.append(pl.BlockSpec(memory_space=pl.ANY))
        args.append(out_prev)
        aliases = {len(args) - 1: 0}
    return pl.pallas_call(
        _final_kernel,
        grid=(tp // tm,),
        in_specs=in_specs,
        out_specs=pl.BlockSpec((tm, D_MODEL), lambda i: (i + off, 0)),
        out_shape=jax.ShapeDtypeStruct((T, D_MODEL), F32),
        input_output_aliases=aliases,
        compiler_params=pltpu.CompilerParams(
            dimension_semantics=("arbitrary",), vmem_limit_bytes=VMEM_LIMIT),
        name="shared_expert_combine",
    )(*args)


def _rope_tables(positions):
    inv = jnp.power(ROPE_THETA, -jnp.arange(ROPE_HALF, dtype=F32) / ROPE_HALF)
    ang = positions.astype(F32)[..., None] * inv
    cos, sin = jnp.cos(ang), jnp.sin(ang)
    rest = ATT_HEAD_DIM - 2 * ROPE_HALF
    cs = jnp.concatenate([cos, cos, jnp.ones(ang.shape[:-1] + (rest,), F32)], axis=-1)
    sn = jnp.concatenate([-sin, sin, jnp.zeros(ang.shape[:-1] + (rest,), F32)], axis=-1)
    return jnp.tile(cs, (1, 1, 2)), jnp.tile(sn, (1, 1, 2))


def _layer(x, c, positions, w_ada, b_ada, g_pre_mix, g_post_mix, g_pre_ffn, g_post_ffn,
           w_in, conv_w, conv_b, b_gates, g_mlstm, w_branch_a, w_branch_b, w_out,
           router_w, router_bias, w_exp_gate, w_exp_up, w_exp_down, w_sh_gate, w_sh_up, w_sh_down):
    B, S, D = x.shape
    T = B * S
    H = MLSTM_HEADS
    x2 = x.reshape(T, D)

    mod3 = _adaln(c, w_ada, b_ada).reshape(B, 6, D)

    a_w = 3 * ATT_GROUP_W
    o_mq = 3 * a_w
    o_mk = o_mq + H * MLSTM_QK_DIM
    o_mv = o_mk + H * MLSTM_QK_DIM
    o_mo = o_mv + H * MLSTM_V_DIM
    o_mi = o_mo + H * MLSTM_V_DIM
    o_ga = o_mi + 2 * H
    o_gb = o_ga + D
    w_bf = w_in.astype(BF16)
    w_main = jnp.concatenate(
        [w_bf[:, o_mv:o_mi], w_bf[:, o_ga:o_gb + D], w_bf[:, o_mq:o_mv], w_bf[:, 0:o_mq]], axis=1)
    w_if = w_bf[:, o_mi:o_ga].T

    proj, gates = _in_proj(x2, mod3, g_pre_mix.reshape(1, D), w_main, w_if, S)
    proj3 = proj.reshape(B, S, PROJ_W)

    wg_p, wu_p, wd_p = (_pack_weight_rows(w, gates) for w in (w_exp_gate, w_exp_up, w_exp_down))

    cs, sn = _rope_tables(positions)
    y_a = _attention(proj3, cs, sn)

    bg_row = jnp.pad(b_gates.reshape(1, 2 * H), ((0, 0), (0, LANES - 2 * H)))
    gates_t = gates.reshape(2 * H, B, S // MLSTM_BLOCK, MLSTM_BLOCK)
    y_b = _mlstm(proj3, gates_t, conv_w, conv_b.reshape(1, -1), bg_row, g_mlstm.reshape(1, -1),
                 wg_p, wu_p)

    x1, h2p = _merge(y_a.reshape(T, ATT_GROUP_W), y_b.reshape(T, D), proj, x2, mod3,
                     w_branch_a.astype(BF16), w_branch_b.astype(BF16), w_out.astype(BF16),
                     g_post_mix.reshape(1, D), g_pre_ffn.reshape(1, D), S, wd_p)

    rw_t = router_w.T.astype(BF16)
    bias_col = jnp.broadcast_to(router_bias.reshape(N_EXPERTS, 1), (N_EXPERTS, LANES))
    wsg, wsu, wsd = w_sh_gate.astype(BF16), w_sh_up.astype(BF16), w_sh_down.astype(BF16)

    tp = T // MOE_PARTS
    bm = EXPERT_BLOCK
    nb = (tp * TOP_K) // bm + N_EXPERTS
    out = None
    for part in range(MOE_PARTS):
        row0 = part * tp
        idx, wts, rank, cnt = _router(h2p, rw_t[:, :HALF], rw_t[:, HALF:], bias_col, row0, tp)

        counts = cnt[:, 0].astype(jnp.int32)
        padded = (counts + bm - 1) // bm * bm
        pend = jnp.cumsum(padded)
        pstart = pend - padded
        pstart_col = jnp.broadcast_to(pstart.astype(F32).reshape(N_EXPERTS, 1), (N_EXPERTS, LANES))
        dest = _slot_index(idx, rank, pstart_col)
        nused = (pend[-1] // bm).astype(jnp.int32).reshape(1)

        xs = _dispatch(h2p, dest, nb * bm, row0)
        ys = _expert_ffn((pstart // bm).astype(jnp.int32), (padded // bm).astype(jnp.int32), nused,
                         xs, wg_p, wu_p, wd_p)
        yg = _collect(ys, dest)
        out = _final(yg, wts.T, h2p, x1, mod3, wsg, wsu, wsd, g_post_ffn.reshape(1, D), S, row0, out)
    return out.reshape(B, S, D)


SC_CORES = 2
SC_SUBCORES = 16
SC_WORKERS = SC_CORES * SC_SUBCORES
SC_ROWS = 64


def _sc_mesh():
    return plsc.VectorSubcoreMesh(core_axis_name="c", subcore_axis_name="s",
                                  num_cores=SC_CORES, num_subcores=SC_SUBCORES)


def _worker_id():
    return lax.axis_index("s") * SC_CORES + lax.axis_index("c")


def _dispatch(h2p, dest, n_slots, row0):
    T = dest.shape[1]
    per_w = T // SC_WORKERS
    nch = per_w // SC_ROWS
    idx = dest.reshape(TOP_K, SC_WORKERS, nch, SC_ROWS).transpose(1, 2, 0, 3)
    idx = idx.reshape(SC_WORKERS, nch * TOP_K, SC_ROWS)

    def body(x_hbm, idx_hbm, xs_hbm, idx_v, buf0, buf1, rsem0, rsem1, ssem0, ssem1):
        wid = _worker_id()
        base = row0 + wid * per_w
        pltpu.sync_copy(idx_hbm.at[wid], idx_v)
        bufs = ((buf0, rsem0, ssem0), (buf1, rsem1, ssem1))

        def read(c, buf, rsem):
            return pltpu.make_async_copy(x_hbm.at[pl.ds(base + c * SC_ROWS, SC_ROWS)], buf, rsem)

        def scatter(c, k, buf, ssem):
            return pltpu.make_async_copy(buf, xs_hbm.at[idx_v.at[c * TOP_K + k]], ssem)

        read(0, buf0, rsem0).start()

        @pl.loop(0, nch, step=2)
        def _(c0):
            for b in range(2):
                c = c0 + b
                buf, rsem, ssem = bufs[b]
                obuf, orsem, ossem = bufs[1 - b]
                read(c, buf, rsem).wait()

                @pl.when(c > 0)
                def _():
                    for k in range(TOP_K):
                        scatter(c - 1, k, obuf, ossem).wait()

                @pl.when(c + 1 < nch)
                def _():
                    read(c + 1, obuf, orsem).start()

                for k in range(TOP_K):
                    scatter(c, k, buf, ssem).start()

        for k in range(TOP_K):
            scatter(nch - 1, k, buf1, ssem1).wait()

    run = pl.kernel(
        body,
        out_type=jax.ShapeDtypeStruct((n_slots, HALF), jnp.uint32),
        mesh=_sc_mesh(),
        scratch_types=[pltpu.VMEM((nch * TOP_K, SC_ROWS), jnp.int32),
                       pltpu.VMEM((SC_ROWS, HALF), jnp.uint32),
                       pltpu.VMEM((SC_ROWS, HALF), jnp.uint32),
                       pltpu.SemaphoreType.DMA, pltpu.SemaphoreType.DMA,
                       pltpu.SemaphoreType.DMA, pltpu.SemaphoreType.DMA],
        name="sc_dispatch",
    )
    return run(h2p, idx)


SC_PACK_ROWS = 64
SC_PACK_COLS = 256
SC_LANES = 16


def _pack_weight_rows(w, after):
    E, R, C = w.shape
    w2 = w.reshape(E * R, C)

    def body(w_hbm, after_hbm, out_hbm):
        del after_hbm

        def block(in_v, out_v):
            @pl.loop(0, SC_PACK_ROWS)
            def _(r):
                @pl.loop(0, SC_PACK_COLS, step=SC_LANES)
                def _(c):
                    cols = pl.ds(c, SC_LANES)
                    pair = plsc.pack(in_v[2 * r, cols], in_v[2 * r + 1, cols],
                                     format=plsc.PackFormat.INTERLEAVED)
                    out_v[r, cols] = plsc.bitcast(pair, jnp.uint32)

        pltpu.emit_pipeline(
            block,
            grid=(E * R // (2 * SC_PACK_ROWS), C // SC_PACK_COLS),
            in_specs=[pl.BlockSpec((2 * SC_PACK_ROWS, SC_PACK_COLS), lambda i, j: (i, j))],
            out_specs=[pl.BlockSpec((SC_PACK_ROWS, SC_PACK_COLS), lambda i, j: (i, j))],
            core_axis_name=("c", "s"),
            dimension_semantics=(pltpu.PARALLEL, pltpu.PARALLEL),
        )(w_hbm, out_hbm)

    run = pl.kernel(body, out_type=jax.ShapeDtypeStruct((E * R // 2, C), jnp.uint32),
                    mesh=_sc_mesh(), scratch_types=[], name="sc_pack_weights",
                    compiler_params=pltpu.CompilerParams(needs_layout_passes=False))
    return run(w2, after).reshape(E, R // 2, C)


def _collect(ys, dest):
    n = dest.size
    per_w = n // SC_WORKERS
    nch = per_w // SC_ROWS
    idx = dest.reshape(SC_WORKERS, nch, SC_ROWS)

    def body(ys_hbm, idx_hbm, out_hbm, idx_v, buf0, buf1, gsem0, gsem1, wsem0, wsem1):
        wid = _worker_id()
        base = wid * per_w
        pltpu.sync_copy(idx_hbm.at[wid], idx_v)
        bufs = ((buf0, gsem0, wsem0), (buf1, gsem1, wsem1))

        def gather(c, buf, gsem):
            return pltpu.make_async_copy(ys_hbm.at[idx_v.at[c]], buf, gsem)

        def write(c, buf, wsem):
            return pltpu.make_async_copy(buf, out_hbm.at[pl.ds(base + c * SC_ROWS, SC_ROWS)], wsem)

        gather(0, buf0, gsem0).start()

        @pl.loop(0, nch, step=2)
        def _(c0):
            for b in range(2):
                c = c0 + b
                buf, gsem, wsem = bufs[b]
                obuf, ogsem, owsem = bufs[1 - b]
                gather(c, buf, gsem).wait()

                @pl.when(c > 0)
                def _():
                    write(c - 1, obuf, owsem).wait()

                @pl.when(c + 1 < nch)
                def _():
                    gather(c + 1, obuf, ogsem).start()

                write(c, buf, wsem).start()

        write(nch - 1, buf1, wsem1).wait()

    run = pl.kernel(
        body,
        out_type=jax.ShapeDtypeStruct((n, HALF), jnp.uint32),
        mesh=_sc_mesh(),
        scratch_types=[pltpu.VMEM((nch, SC_ROWS), jnp.int32),
                       pltpu.VMEM((SC_ROWS, HALF), jnp.uint32),
                       pltpu.VMEM((SC_ROWS, HALF), jnp.uint32),
                       pltpu.SemaphoreType.DMA, pltpu.SemaphoreType.DMA,
                       pltpu.SemaphoreType.DMA, pltpu.SemaphoreType.DMA],
        name="sc_collect",
    )
    return run(ys, idx).reshape(dest.shape + (HALF,))


def kernel(x, c, positions, w_ada, b_ada, g_pre_mix, g_post_mix, g_pre_ffn, g_post_ffn, w_in, conv_w, conv_b, b_gates, g_mlstm, w_branch_a, w_branch_b, w_out, router_w, router_bias, w_exp_gate, w_exp_up, w_exp_down, w_sh_gate, w_sh_up, w_sh_down):
    depth = w_ada.shape[0]
    for l in range(depth):
        x = _layer(x, c, positions, w_ada[l], b_ada[l], g_pre_mix[l], g_post_mix[l], g_pre_ffn[l],
                   g_post_ffn[l], w_in[l], conv_w[l], conv_b[l], b_gates[l], g_mlstm[l],
                   w_branch_a[l], w_branch_b[l], w_out[l], router_w[l], router_bias[l],
                   w_exp_gate[l], w_exp_up[l], w_exp_down[l], w_sh_gate[l], w_sh_up[l], w_sh_down[l])
    return x
```
